```python
import jax, jax.numpy as jnp
from jax import lax
import numpy as np

D_MODEL = 2048
BATCH = 8
SEQ = 4096
DEPTH = 4

MIX_WIDTH = D_MODEL
GROUP_WIDTH = MIX_WIDTH // 4
HEAD_DIM = 128
N_HEADS_PER_MIXER = GROUP_WIDTH // HEAD_DIM
CHUNK = 128
SHORT_CONV = 3
CONFORMER_CONV = 31
POOL_WINDOWS = (2, 4, 8, 16)
POOL_GROUP = GROUP_WIDTH // len(POOL_WINDOWS)
D_FF = ((8 * D_MODEL // 3 + 255) // 256) * 256
PLE_DIM = 256
EPS = 1e-6
A_COLS = 2 * GROUP_WIDTH
B_COLS = 3 * GROUP_WIDTH
C_COLS = 2 * GROUP_WIDTH
D_COLS = GROUP_WIDTH
IN_COLS = A_COLS + B_COLS + C_COLS + D_COLS

kernel_name = "hybrid_sgu_conv_conformer_pool_trunk"


def _rms(x, g):
    xf = x.astype(jnp.float32)
    y = xf * lax.rsqrt(jnp.mean(xf * xf, axis=-1, keepdims=True) + EPS)
    return (y * g.astype(jnp.float32)).astype(x.dtype)


def _ln(x, g, b):
    xf = x.astype(jnp.float32)
    mu = jnp.mean(xf, axis=-1, keepdims=True)
    xc = xf - mu
    var = jnp.mean(xc * xc, axis=-1, keepdims=True)
    y = xc * lax.rsqrt(var + EPS) * g.astype(jnp.float32) + b.astype(jnp.float32)
    return y.astype(x.dtype)


def _causal_dwconv(x, w):
    k, c = w.shape
    return lax.conv_general_dilated(
        x, w[:, None, :].astype(x.dtype), window_strides=(1,), padding=[(k - 1, 0)],
        dimension_numbers=("NWC", "WIO", "NWC"), feature_group_count=c)


def _mixer_sgu(z, ln_g, ln_b, w_s, b_s):
    bsz, s, _ = z.shape
    n = s // CHUNK
    z = jax.nn.gelu(z)
    u, v = jnp.split(z, 2, axis=-1)
    v = _ln(v.reshape(bsz, s, N_HEADS_PER_MIXER, HEAD_DIM), ln_g, ln_b)
    v = v.reshape(bsz, n, CHUNK, N_HEADS_PER_MIXER, HEAD_DIM)
    mask = jnp.tril(jnp.ones((CHUNK, CHUNK), dtype=bool))
    wm = jnp.where(mask[None], w_s, jnp.zeros((), w_s.dtype))
    sp = jnp.einsum("hts,bnshd->bnthd", wm, v) + b_s.T[:, :, None]
    out = u.reshape(bsz, n, CHUNK, N_HEADS_PER_MIXER, HEAD_DIM) * sp
    return out.reshape(bsz, s, GROUP_WIDTH)


def _mixer_shortconv(z, conv_w):
    h, bg, cg = jnp.split(z, 3, axis=-1)
    return bg * _causal_dwconv(cg * h, conv_w)


def _mixer_conformer(z, conv_w, conv_b, ln_g, ln_b):
    a, g = jnp.split(z, 2, axis=-1)
    h = a * jax.nn.sigmoid(g)
    h = _causal_dwconv(h, conv_w) + conv_b
    h = _ln(h, ln_g, ln_b)
    return jax.nn.silu(h)


def _mixer_pool(z, pool_w, pool_scale):
    bsz, s, _ = z.shape
    zg = z.reshape(bsz, s, len(POOL_WINDOWS), POOL_GROUP)
    cs = jnp.cumsum(zg.astype(jnp.float32), axis=1)
    pos = jnp.arange(1, s + 1, dtype=jnp.int32)
    pooled = []
    for gi, w in enumerate(POOL_WINDOWS):
        c = cs[:, :, gi]
        lagged = jnp.pad(c, ((0, 0), (w, 0), (0, 0)))[:, :s]
        count = jnp.minimum(pos, w).astype(jnp.float32)
        pooled.append((c - lagged) / count[None, :, None])
    pooled = jnp.stack(pooled, axis=2).astype(z.dtype) - zg
    out = jnp.einsum("bsgc,gcd->bsgd", pooled, pool_w).reshape(bsz, s, GROUP_WIDTH)
    return out * pool_scale


def _fwd_setup_inputs(seed: int = 0) -> dict:
    key = jax.random.key(seed)
    ks = jax.random.split(key, 24)
    f = jnp.float32
    L, D, G, H, hd = DEPTH, D_MODEL, GROUP_WIDTH, N_HEADS_PER_MIXER, HEAD_DIM
    nrm = lambda k, shape, scale: jax.random.normal(k, shape, f) * scale
    gain = lambda k, shape: 1.0 + 0.05 * jax.random.normal(k, shape, f)
    return {
        "x": jax.random.normal(ks[0], (BATCH, SEQ, D), f),
        "p": jax.random.normal(ks[1], (DEPTH, BATCH, SEQ, PLE_DIM), f),
        "norm_mix_g": gain(ks[2], (L, D)),
        "w_in": nrm(ks[3], (L, D, IN_COLS), D ** -0.5),
        "sgu_ln_g": gain(ks[4], (L, H, hd)),
        "sgu_ln_b": nrm(ks[5], (L, H, hd), 0.02),
        "sgu_w": nrm(ks[6], (L, H, CHUNK, CHUNK), CHUNK ** -0.5),
        "sgu_b": gain(ks[7], (L, H, CHUNK)),
        "sc_conv_w": nrm(ks[8], (L, SHORT_CONV, G), SHORT_CONV ** -0.5),
        "cf_conv_w": nrm(ks[9], (L, CONFORMER_CONV, G), CONFORMER_CONV ** -0.5),
        "cf_conv_b": nrm(ks[10], (L, G), 0.02),
        "cf_ln_g": gain(ks[11], (L, G)),
        "cf_ln_b": nrm(ks[12], (L, G), 0.02),
        "pool_w": nrm(ks[13], (L, len(POOL_WINDOWS), POOL_GROUP, POOL_GROUP), POOL_GROUP ** -0.5),
        "pool_scale": 0.5 + 0.1 * jax.random.normal(ks[14], (L, G), f),
        "w_out": nrm(ks[15], (L, MIX_WIDTH, D), MIX_WIDTH ** -0.5),
        "norm_ffn_g": gain(ks[16], (L, D)),
        "w_gate": nrm(ks[17], (L, D, D_FF), D ** -0.5),
        "w_up": nrm(ks[18], (L, D, D_FF), D ** -0.5),
        "w_down": nrm(ks[19], (L, D_FF, D), D_FF ** -0.5),
        "norm_ple_g": gain(ks[20], (L, D)),
        "w_ple_gate": nrm(ks[21], (L, D, D), D ** -0.5),
        "w_ple_proj": nrm(ks[22], (L, PLE_DIM, D), PLE_DIM ** -0.5),
        "final_norm_g": gain(ks[23], (D,)),
    }


def _fwd_reference(x, p, norm_mix_g, w_in, sgu_ln_g, sgu_ln_b, sgu_w, sgu_b, sc_conv_w,
              cf_conv_w, cf_conv_b, cf_ln_g, cf_ln_b, pool_w, pool_scale, w_out,
              norm_ffn_g, w_gate, w_up, w_down, norm_ple_g, w_ple_gate, w_ple_proj,
              final_norm_g):
    h = x
    split_at = [A_COLS, A_COLS + B_COLS, A_COLS + B_COLS + C_COLS]
    for i in range(DEPTH):
        y = _rms(h, norm_mix_g[i])
        z = y @ w_in[i]
        za, zb, zc, zd = jnp.split(z, split_at, axis=-1)
        oa = _mixer_sgu(za, sgu_ln_g[i], sgu_ln_b[i], sgu_w[i], sgu_b[i])
        ob = _mixer_shortconv(zb, sc_conv_w[i])
        oc = _mixer_conformer(zc, cf_conv_w[i], cf_conv_b[i], cf_ln_g[i], cf_ln_b[i])
        od = _mixer_pool(zd, pool_w[i], pool_scale[i])
        h = h + jnp.concatenate([oa, ob, oc, od], axis=-1) @ w_out[i]
        y = _rms(h, norm_ffn_g[i])
        h = h + (jax.nn.silu(y @ w_gate[i]) * (y @ w_up[i])) @ w_down[i]
        y = _rms(h, norm_ple_g[i])
        h = h + jax.nn.sigmoid(y @ w_ple_gate[i]) * (p[i] @ w_ple_proj[i])
    return _rms(h, final_norm_g)


import jax as _jax
import jax.numpy as _jnp

TWIN_FORMAT = 'train_step'
FWD_PARAMS = ['x', 'p', 'norm_mix_g', 'w_in', 'sgu_ln_g', 'sgu_ln_b', 'sgu_w', 'sgu_b', 'sc_conv_w', 'cf_conv_w', 'cf_conv_b', 'cf_ln_g', 'cf_ln_b', 'pool_w', 'pool_scale', 'w_out', 'norm_ffn_g', 'w_gate', 'w_up', 'w_down', 'norm_ple_g', 'w_ple_gate', 'w_ple_proj', 'final_norm_g']
TWIN_WEIGHTS = ['norm_mix_g', 'w_in', 'sgu_ln_g', 'sgu_ln_b', 'sgu_w', 'sgu_b', 'sc_conv_w', 'cf_conv_w', 'cf_conv_b', 'cf_ln_g', 'cf_ln_b', 'pool_w', 'pool_scale', 'w_out', 'norm_ffn_g', 'w_gate', 'w_up', 'w_down', 'norm_ple_g', 'w_ple_gate', 'w_ple_proj', 'final_norm_g']
TWIN_DIFF_INPUT = 'x'
TWIN_INPUTS = ['x', 'p', 'norm_mix_g', 'w_in', 'sgu_ln_g', 'sgu_ln_b', 'sgu_w', 'sgu_b', 'sc_conv_w', 'cf_conv_w', 'cf_conv_b', 'cf_ln_g', 'cf_ln_b', 'pool_w', 'pool_scale', 'w_out', 'norm_ffn_g', 'w_gate', 'w_up', 'w_down', 'norm_ple_g', 'w_ple_gate', 'w_ple_proj', 'final_norm_g', 'loss_target', 'm_norm_mix_g', 'm_w_in', 'm_sgu_ln_g', 'm_sgu_ln_b', 'm_sgu_w', 'm_sgu_b', 'm_sc_conv_w', 'm_cf_conv_w', 'm_cf_conv_b', 'm_cf_ln_g', 'm_cf_ln_b', 'm_pool_w', 'm_pool_scale', 'm_w_out', 'm_norm_ffn_g', 'm_w_gate', 'm_w_up', 'm_w_down', 'm_norm_ple_g', 'm_w_ple_gate', 'm_w_ple_proj', 'm_final_norm_g', 'v_norm_mix_g', 'v_w_in', 'v_sgu_ln_g', 'v_sgu_ln_b', 'v_sgu_w', 'v_sgu_b', 'v_sc_conv_w', 'v_cf_conv_w', 'v_cf_conv_b', 'v_cf_ln_g', 'v_cf_ln_b', 'v_pool_w', 'v_pool_scale', 'v_w_out', 'v_norm_ffn_g', 'v_w_gate', 'v_w_up', 'v_w_down', 'v_norm_ple_g', 'v_w_ple_gate', 'v_w_ple_proj', 'v_final_norm_g']
TWIN_OUTPUTS = ['loss', 'grad_x', 'grad_norm_mix_g', 'grad_w_in', 'grad_sgu_ln_g', 'grad_sgu_ln_b', 'grad_sgu_w', 'grad_sgu_b', 'grad_sc_conv_w', 'grad_cf_conv_w', 'grad_cf_conv_b', 'grad_cf_ln_g', 'grad_cf_ln_b', 'grad_pool_w', 'grad_pool_scale', 'grad_w_out', 'grad_norm_ffn_g', 'grad_w_gate', 'grad_w_up', 'grad_w_down', 'grad_norm_ple_g', 'grad_w_ple_gate', 'grad_w_ple_proj', 'grad_final_norm_g', 'delta_norm_mix_g', 'delta_w_in', 'delta_sgu_ln_g', 'delta_sgu_ln_b', 'delta_sgu_w', 'delta_sgu_b', 'delta_sc_conv_w', 'delta_cf_conv_w', 'delta_cf_conv_b', 'delta_cf_ln_g', 'delta_cf_ln_b', 'delta_pool_w', 'delta_pool_scale', 'delta_w_out', 'delta_norm_ffn_g', 'delta_w_gate', 'delta_w_up', 'delta_w_down', 'delta_norm_ple_g', 'delta_w_ple_gate', 'delta_w_ple_proj', 'delta_final_norm_g', 'new_m_norm_mix_g', 'new_m_w_in', 'new_m_sgu_ln_g', 'new_m_sgu_ln_b', 'new_m_sgu_w', 'new_m_sgu_b', 'new_m_sc_conv_w', 'new_m_cf_conv_w', 'new_m_cf_conv_b', 'new_m_cf_ln_g', 'new_m_cf_ln_b', 'new_m_pool_w', 'new_m_pool_scale', 'new_m_w_out', 'new_m_norm_ffn_g', 'new_m_w_gate', 'new_m_w_up', 'new_m_w_down', 'new_m_norm_ple_g', 'new_m_w_ple_gate', 'new_m_w_ple_proj', 'new_m_final_norm_g', 'new_v_norm_mix_g', 'new_v_w_in', 'new_v_sgu_ln_g', 'new_v_sgu_ln_b', 'new_v_sgu_w', 'new_v_sgu_b', 'new_v_sc_conv_w', 'new_v_cf_conv_w', 'new_v_cf_conv_b', 'new_v_cf_ln_g', 'new_v_cf_ln_b', 'new_v_pool_w', 'new_v_pool_scale', 'new_v_w_out', 'new_v_norm_ffn_g', 'new_v_w_gate', 'new_v_w_up', 'new_v_w_down', 'new_v_norm_ple_g', 'new_v_w_ple_gate', 'new_v_w_ple_proj', 'new_v_final_norm_g']
TWIN_LEAF_KINDS = {'loss': 'loss', 'grad_x': 'grad_x', 'grad_norm_mix_g': 'grad_w', 'grad_w_in': 'grad_w', 'grad_sgu_ln_g': 'grad_w', 'grad_sgu_ln_b': 'grad_w', 'grad_sgu_w': 'grad_w', 'grad_sgu_b': 'grad_w', 'grad_sc_conv_w': 'grad_w', 'grad_cf_conv_w': 'grad_w', 'grad_cf_conv_b': 'grad_w', 'grad_cf_ln_g': 'grad_w', 'grad_cf_ln_b': 'grad_w', 'grad_pool_w': 'grad_w', 'grad_pool_scale': 'grad_w', 'grad_w_out': 'grad_w', 'grad_norm_ffn_g': 'grad_w', 'grad_w_gate': 'grad_w', 'grad_w_up': 'grad_w', 'grad_w_down': 'grad_w', 'grad_norm_ple_g': 'grad_w', 'grad_w_ple_gate': 'grad_w', 'grad_w_ple_proj': 'grad_w', 'grad_final_norm_g': 'grad_w', 'delta_norm_mix_g': 'delta_w', 'delta_w_in': 'delta_w', 'delta_sgu_ln_g': 'delta_w', 'delta_sgu_ln_b': 'delta_w', 'delta_sgu_w': 'delta_w', 'delta_sgu_b': 'delta_w', 'delta_sc_conv_w': 'delta_w', 'delta_cf_conv_w': 'delta_w', 'delta_cf_conv_b': 'delta_w', 'delta_cf_ln_g': 'delta_w', 'delta_cf_ln_b': 'delta_w', 'delta_pool_w': 'delta_w', 'delta_pool_scale': 'delta_w', 'delta_w_out': 'delta_w', 'delta_norm_ffn_g': 'delta_w', 'delta_w_gate': 'delta_w', 'delta_w_up': 'delta_w', 'delta_w_down': 'delta_w', 'delta_norm_ple_g': 'delta_w', 'delta_w_ple_gate': 'delta_w', 'delta_w_ple_proj': 'delta_w', 'delta_final_norm_g': 'delta_w', 'new_m_norm_mix_g': 'new_m', 'new_m_w_in': 'new_m', 'new_m_sgu_ln_g': 'new_m', 'new_m_sgu_ln_b': 'new_m', 'new_m_sgu_w': 'new_m', 'new_m_sgu_b': 'new_m', 'new_m_sc_conv_w': 'new_m', 'new_m_cf_conv_w': 'new_m', 'new_m_cf_conv_b': 'new_m', 'new_m_cf_ln_g': 'new_m', 'new_m_cf_ln_b': 'new_m', 'new_m_pool_w': 'new_m', 'new_m_pool_scale': 'new_m', 'new_m_w_out': 'new_m', 'new_m_norm_ffn_g': 'new_m', 'new_m_w_gate': 'new_m', 'new_m_w_up': 'new_m', 'new_m_w_down': 'new_m', 'new_m_norm_ple_g': 'new_m', 'new_m_w_ple_gate': 'new_m', 'new_m_w_ple_proj': 'new_m', 'new_m_final_norm_g': 'new_m', 'new_v_norm_mix_g': 'new_v', 'new_v_w_in': 'new_v', 'new_v_sgu_ln_g': 'new_v', 'new_v_sgu_ln_b': 'new_v', 'new_v_sgu_w': 'new_v', 'new_v_sgu_b': 'new_v', 'new_v_sc_conv_w': 'new_v', 'new_v_cf_conv_w': 'new_v', 'new_v_cf_conv_b': 'new_v', 'new_v_cf_ln_g': 'new_v', 'new_v_cf_ln_b': 'new_v', 'new_v_pool_w': 'new_v', 'new_v_pool_scale': 'new_v', 'new_v_w_out': 'new_v', 'new_v_norm_ffn_g': 'new_v', 'new_v_w_gate': 'new_v', 'new_v_w_up': 'new_v', 'new_v_w_down': 'new_v', 'new_v_norm_ple_g': 'new_v', 'new_v_w_ple_gate': 'new_v', 'new_v_w_ple_proj': 'new_v', 'new_v_final_norm_g': 'new_v'}


def _forward(args):
    return _fwd_reference(*[args[k] for k in FWD_PARAMS])


def _output_shape():
    out = _jax.eval_shape(lambda: _forward(_fwd_setup_inputs(0)))
    return out.shape, out.dtype

N_MICROBATCH = 1
ADAM_LR = 0.001
ADAM_B1 = 0.9
ADAM_B2 = 0.999
ADAM_EPS = 1e-08
ADAM_WD = 0.01
ADAM_STEP = 10
PER_EXAMPLE_BATCH_AXIS = {'x': 0, 'p': 1, 'loss_target': 0}
SHARED_INPUTS = []
_WEIGHT_DTYPES = {'norm_mix_g': _jnp.float32, 'w_in': _jnp.float32, 'sgu_ln_g': _jnp.float32, 'sgu_ln_b': _jnp.float32, 'sgu_w': _jnp.float32, 'sgu_b': _jnp.float32, 'sc_conv_w': _jnp.float32, 'cf_conv_w': _jnp.float32, 'cf_conv_b': _jnp.float32, 'cf_ln_g': _jnp.float32, 'cf_ln_b': _jnp.float32, 'pool_w': _jnp.float32, 'pool_scale': _jnp.float32, 'w_out': _jnp.float32, 'norm_ffn_g': _jnp.float32, 'w_gate': _jnp.float32, 'w_up': _jnp.float32, 'w_down': _jnp.float32, 'norm_ple_g': _jnp.float32, 'w_ple_gate': _jnp.float32, 'w_ple_proj': _jnp.float32, 'final_norm_g': _jnp.float32}
MOMENT_SCALE = {'norm_mix_g': 6.926171e-02, 'w_in': 4.975742e-02, 'sgu_ln_g': 3.097251e-02, 'sgu_ln_b': 3.087856e-02, 'sgu_w': 3.021272e-02, 'sgu_b': 4.309424e-02, 'sc_conv_w': 6.910950e-02, 'cf_conv_w': 4.118000e-02, 'cf_conv_b': 9.046810e-02, 'cf_ln_g': 4.874866e-02, 'cf_ln_b': 5.129451e-02, 'pool_w': 2.941819e-02, 'pool_scale': 5.837356e-02, 'w_out': 5.014014e-02, 'norm_ffn_g': 5.081266e-02, 'w_gate': 2.116407e-02, 'w_up': 2.059320e-02, 'w_down': 3.411259e-02, 'norm_ple_g': 1.202490e-02, 'w_ple_gate': 1.215547e-02, 'w_ple_proj': 3.059013e-02, 'final_norm_g': 1.600739e+01}


def _to_microbatches(a, axis):
    t = _jnp.moveaxis(a, axis, 0)
    t = t.reshape((N_MICROBATCH, t.shape[0] // N_MICROBATCH) + t.shape[1:])
    return _jnp.moveaxis(t, 1, axis + 1)


def setup_inputs(seed: int = 0) -> dict:
    inp = _fwd_setup_inputs(seed)
    key = _jax.random.fold_in(_jax.random.key(seed), 7919)
    shape, _ = _output_shape()
    out = dict(inp)
    out["loss_target"] = _jax.random.normal(_jax.random.fold_in(key, 0), shape, _jnp.float32)
    for i, name in enumerate(TWIN_WEIGHTS):
        w = inp[name].astype(_jnp.float32)
        if MOMENT_SCALE is None:
            s = _jnp.sqrt(_jnp.mean(_jnp.square(w)) + 1e-30)
        else:
            s = MOMENT_SCALE[name]
        km, kv = _jax.random.split(_jax.random.fold_in(key, i + 1))
        out[name] = w
        out["m_" + name] = s * _jax.random.normal(km, w.shape, _jnp.float32)
        out["v_" + name] = (s * s) * _jax.random.uniform(kv, w.shape, _jnp.float32, 0.5, 1.5)
    if N_MICROBATCH > 1:
        for name, axis in PER_EXAMPLE_BATCH_AXIS.items():
            out[name] = _to_microbatches(out[name], axis)
    return {'x': out['x'], 'p': out['p'], 'norm_mix_g': out['norm_mix_g'], 'w_in': out['w_in'], 'sgu_ln_g': out['sgu_ln_g'], 'sgu_ln_b': out['sgu_ln_b'], 'sgu_w': out['sgu_w'], 'sgu_b': out['sgu_b'], 'sc_conv_w': out['sc_conv_w'], 'cf_conv_w': out['cf_conv_w'], 'cf_conv_b': out['cf_conv_b'], 'cf_ln_g': out['cf_ln_g'], 'cf_ln_b': out['cf_ln_b'], 'pool_w': out['pool_w'], 'pool_scale': out['pool_scale'], 'w_out': out['w_out'], 'norm_ffn_g': out['norm_ffn_g'], 'w_gate': out['w_gate'], 'w_up': out['w_up'], 'w_down': out['w_down'], 'norm_ple_g': out['norm_ple_g'], 'w_ple_gate': out['w_ple_gate'], 'w_ple_proj': out['w_ple_proj'], 'final_norm_g': out['final_norm_g'], 'loss_target': out['loss_target'], 'm_norm_mix_g': out['m_norm_mix_g'], 'm_w_in': out['m_w_in'], 'm_sgu_ln_g': out['m_sgu_ln_g'], 'm_sgu_ln_b': out['m_sgu_ln_b'], 'm_sgu_w': out['m_sgu_w'], 'm_sgu_b': out['m_sgu_b'], 'm_sc_conv_w': out['m_sc_conv_w'], 'm_cf_conv_w': out['m_cf_conv_w'], 'm_cf_conv_b': out['m_cf_conv_b'], 'm_cf_ln_g': out['m_cf_ln_g'], 'm_cf_ln_b': out['m_cf_ln_b'], 'm_pool_w': out['m_pool_w'], 'm_pool_scale': out['m_pool_scale'], 'm_w_out': out['m_w_out'], 'm_norm_ffn_g': out['m_norm_ffn_g'], 'm_w_gate': out['m_w_gate'], 'm_w_up': out['m_w_up'], 'm_w_down': out['m_w_down'], 'm_norm_ple_g': out['m_norm_ple_g'], 'm_w_ple_gate': out['m_w_ple_gate'], 'm_w_ple_proj': out['m_w_ple_proj'], 'm_final_norm_g': out['m_final_norm_g'], 'v_norm_mix_g': out['v_norm_mix_g'], 'v_w_in': out['v_w_in'], 'v_sgu_ln_g': out['v_sgu_ln_g'], 'v_sgu_ln_b': out['v_sgu_ln_b'], 'v_sgu_w': out['v_sgu_w'], 'v_sgu_b': out['v_sgu_b'], 'v_sc_conv_w': out['v_sc_conv_w'], 'v_cf_conv_w': out['v_cf_conv_w'], 'v_cf_conv_b': out['v_cf_conv_b'], 'v_cf_ln_g': out['v_cf_ln_g'], 'v_cf_ln_b': out['v_cf_ln_b'], 'v_pool_w': out['v_pool_w'], 'v_pool_scale': out['v_pool_scale'], 'v_w_out': out['v_w_out'], 'v_norm_ffn_g': out['v_norm_ffn_g'], 'v_w_gate': out['v_w_gate'], 'v_w_up': out['v_w_up'], 'v_w_down': out['v_w_down'], 'v_norm_ple_g': out['v_norm_ple_g'], 'v_w_ple_gate': out['v_w_ple_gate'], 'v_w_ple_proj': out['v_w_ple_proj'], 'v_final_norm_g': out['v_final_norm_g']}


def _loss(weights, diff, rest, loss_target):
    with _jax.named_scope("forward"):
        args = {**rest, TWIN_DIFF_INPUT: diff, **{k: w.astype(_WEIGHT_DTYPES[k]) for k, w in weights.items()}}
        y = _forward(args)
    with _jax.named_scope("loss_head"):
        err = _jnp.square(y.astype(_jnp.float32) - loss_target)
        return 0.5 * _jnp.sum(_jnp.mean(err, axis=-1)) if err.ndim else 0.5 * err


def _adamw(w, g, m, v):
    m = ADAM_B1 * m + (1.0 - ADAM_B1) * g
    v = ADAM_B2 * v + (1.0 - ADAM_B2) * _jnp.square(g)
    m_hat = m / (1.0 - ADAM_B1 ** ADAM_STEP)
    v_hat = v / (1.0 - ADAM_B2 ** ADAM_STEP)
    delta = -ADAM_LR * (m_hat / (_jnp.sqrt(v_hat) + ADAM_EPS) + ADAM_WD * w)
    return delta, m, v


def reference(x, p, norm_mix_g, w_in, sgu_ln_g, sgu_ln_b, sgu_w, sgu_b, sc_conv_w, cf_conv_w, cf_conv_b, cf_ln_g, cf_ln_b, pool_w, pool_scale, w_out, norm_ffn_g, w_gate, w_up, w_down, norm_ple_g, w_ple_gate, w_ple_proj, final_norm_g, loss_target, m_norm_mix_g, m_w_in, m_sgu_ln_g, m_sgu_ln_b, m_sgu_w, m_sgu_b, m_sc_conv_w, m_cf_conv_w, m_cf_conv_b, m_cf_ln_g, m_cf_ln_b, m_pool_w, m_pool_scale, m_w_out, m_norm_ffn_g, m_w_gate, m_w_up, m_w_down, m_norm_ple_g, m_w_ple_gate, m_w_ple_proj, m_final_norm_g, v_norm_mix_g, v_w_in, v_sgu_ln_g, v_sgu_ln_b, v_sgu_w, v_sgu_b, v_sc_conv_w, v_cf_conv_w, v_cf_conv_b, v_cf_ln_g, v_cf_ln_b, v_pool_w, v_pool_scale, v_w_out, v_norm_ffn_g, v_w_gate, v_w_up, v_w_down, v_norm_ple_g, v_w_ple_gate, v_w_ple_proj, v_final_norm_g):
    given = dict(x=x, p=p, norm_mix_g=norm_mix_g, w_in=w_in, sgu_ln_g=sgu_ln_g, sgu_ln_b=sgu_ln_b, sgu_w=sgu_w, sgu_b=sgu_b, sc_conv_w=sc_conv_w, cf_conv_w=cf_conv_w, cf_conv_b=cf_conv_b, cf_ln_g=cf_ln_g, cf_ln_b=cf_ln_b, pool_w=pool_w, pool_scale=pool_scale, w_out=w_out, norm_ffn_g=norm_ffn_g, w_gate=w_gate, w_up=w_up, w_down=w_down, norm_ple_g=norm_ple_g, w_ple_gate=w_ple_gate, w_ple_proj=w_ple_proj, final_norm_g=final_norm_g, loss_target=loss_target, m_norm_mix_g=m_norm_mix_g, m_w_in=m_w_in, m_sgu_ln_g=m_sgu_ln_g, m_sgu_ln_b=m_sgu_ln_b, m_sgu_w=m_sgu_w, m_sgu_b=m_sgu_b, m_sc_conv_w=m_sc_conv_w, m_cf_conv_w=m_cf_conv_w, m_cf_conv_b=m_cf_conv_b, m_cf_ln_g=m_cf_ln_g, m_cf_ln_b=m_cf_ln_b, m_pool_w=m_pool_w, m_pool_scale=m_pool_scale, m_w_out=m_w_out, m_norm_ffn_g=m_norm_ffn_g, m_w_gate=m_w_gate, m_w_up=m_w_up, m_w_down=m_w_down, m_norm_ple_g=m_norm_ple_g, m_w_ple_gate=m_w_ple_gate, m_w_ple_proj=m_w_ple_proj, m_final_norm_g=m_final_norm_g, v_norm_mix_g=v_norm_mix_g, v_w_in=v_w_in, v_sgu_ln_g=v_sgu_ln_g, v_sgu_ln_b=v_sgu_ln_b, v_sgu_w=v_sgu_w, v_sgu_b=v_sgu_b, v_sc_conv_w=v_sc_conv_w, v_cf_conv_w=v_cf_conv_w, v_cf_conv_b=v_cf_conv_b, v_cf_ln_g=v_cf_ln_g, v_cf_ln_b=v_cf_ln_b, v_pool_w=v_pool_w, v_pool_scale=v_pool_scale, v_w_out=v_w_out, v_norm_ffn_g=v_norm_ffn_g, v_w_gate=v_w_gate, v_w_up=v_w_up, v_w_down=v_w_down, v_norm_ple_g=v_norm_ple_g, v_w_ple_gate=v_w_ple_gate, v_w_ple_proj=v_w_ple_proj, v_final_norm_g=v_final_norm_g)
    weights = {n: given[n] for n in TWIN_WEIGHTS}
    shared = {n: given[n] for n in SHARED_INPUTS}
    per_example = {n: given[n] for n in ['x', 'p']}
    grad_fn = _jax.value_and_grad(_loss, argnums=(0, 1))

    def one_microbatch(ex, loss_target):
        ex = dict(ex)
        diff = ex.pop(TWIN_DIFF_INPUT)
        return grad_fn(weights, diff, {**shared, **ex}, loss_target)

    if N_MICROBATCH == 1:
        loss, (grad_w, grad_x) = one_microbatch(per_example, given["loss_target"])
    else:
        def body(carry, xs):
            loss_sum, grad_sum = carry
            l_k, (gw_k, gx_k) = one_microbatch(xs[0], xs[1])
            with _jax.named_scope("update"):
                return (loss_sum + l_k, _jax.tree.map(_jnp.add, grad_sum, gw_k)), gx_k

        init = (_jnp.zeros((), _jnp.float32), _jax.tree.map(_jnp.zeros_like, weights))
        (loss, grad_w), grad_x = _jax.lax.scan(body, init, (per_example, given["loss_target"]))
    with _jax.named_scope("update"):
        delta_w, new_m, new_v = {}, {}, {}
        for n in TWIN_WEIGHTS:
            delta_w[n], new_m[n], new_v[n] = _adamw(weights[n], grad_w[n], given["m_" + n], given["v_" + n])
    return (loss, grad_x, *[grad_w[n] for n in TWIN_WEIGHTS], *[delta_w[n] for n in TWIN_WEIGHTS],
            *[new_m[n] for n in TWIN_WEIGHTS], *[new_v[n] for n in TWIN_WEIGHTS])
```

```python
import functools
import math

import jax
import jax.numpy as jnp
from jax import lax
from jax.experimental import pallas as pl
from jax.experimental.pallas import tpu as pltpu

F32 = jnp.float32
BF16 = jnp.bfloat16
MESH = pl.DeviceIdType.MESH

HEAD = 128
GROUP = 512
EPS = 1e-6
HALO = 32
CHUNK_ROWS = 128
POOL_LEVELS = 4

ADAM_LR = 0.001
ADAM_B1 = 0.9
ADAM_B2 = 0.999
ADAM_EPS = 1e-08
ADAM_WD = 0.01
ADAM_STEP = 10

VMEM_LIMIT = 56 * 1024 * 1024


def _cp(sem=None, vmem=VMEM_LIMIT):
    return pltpu.CompilerParams(dimension_semantics=sem, vmem_limit_bytes=vmem)


def _sigmoid(x):
    return 1.0 / (1.0 + jnp.exp(-x))


_GELU_K = math.sqrt(2.0 / math.pi)
_GELU_C = 0.044715


def _gelu(x):
    t = jnp.tanh(_GELU_K * (x + _GELU_C * x * x * x))
    return 0.5 * x * (1.0 + t)


def _gelu_grad(x):
    t = jnp.tanh(_GELU_K * (x + _GELU_C * x * x * x))
    return 0.5 * (1.0 + t) + 0.5 * x * (1.0 - t * t) * _GELU_K * (1.0 + 3.0 * _GELU_C * x * x)


def _mesh_pos():
    return lax.axis_index("x"), lax.axis_index("y"), lax.axis_index("c")


def _any_specs(n):
    return [pl.BlockSpec(memory_space=pl.ANY)] * n


def _allgather8(name, blocks):
    n = len(blocks)

    def body(*refs):
        ins, outs = refs[:n], refs[n:2 * n]
        send_sems, recv_sems, local_sems = refs[2 * n:]
        x, y, c = _mesh_pos()
        me, sibling = (x, y, c), (x, y, 1 - c)
        chips = [(1 - x, y), (x, 1 - y), (1 - x, 1 - y)]

        def slot(a, dev):
            return outs[a].at[4 * dev[0] + 2 * dev[1] + dev[2]]

        def copy(a, k, block, to, src=None):
            dst = slot(a, block)
            return pltpu.make_async_remote_copy(
                src_ref=dst if src is None else src, dst_ref=dst,
                send_sem=send_sems.at[7 * a + k], recv_sem=recv_sems.at[7 * a + k],
                device_id=to, device_id_type=MESH)

        mine, first, passed = [], [], []
        for a in range(n):
            cp = pltpu.make_async_copy(ins[a], slot(a, me), local_sems.at[a])
            cp.start()
            mine.append(cp)
            cps = [copy(a, 0, me, sibling, src=ins[a])]
            cps += [copy(a, 1 + j, me, (*chip, c), src=ins[a]) for j, chip in enumerate(chips)]
            for cp in cps:
                cp.start()
            first += cps
        for j, chip in enumerate(chips):
            for a in range(n):
                copy(a, 1 + j, (*chip, c), me).wait_recv()
                cp = copy(a, 4 + j, (*chip, c), sibling)
                cp.start()
                passed.append(cp)
        for a in range(n):
            copy(a, 0, sibling, me).wait_recv()
            for j, chip in enumerate(chips):
                copy(a, 4 + j, (*chip, 1 - c), me).wait_recv()
        for cp in first + passed:
            cp.wait_send()
        for cp in mine:
            cp.wait()

    return pl.pallas_call(
        body, name=name,
        out_shape=[jax.ShapeDtypeStruct((8,) + b.shape, b.dtype) for b in blocks],
        in_specs=_any_specs(n), out_specs=_any_specs(n),
        scratch_shapes=[pltpu.SemaphoreType.DMA((7 * n,)), pltpu.SemaphoreType.DMA((7 * n,)),
                        pltpu.SemaphoreType.DMA((n,))],
    )(*blocks)


def _gather_weight_halves(name, stacked, layer, axes):
    n = len(stacked)

    def half_shape(w, ax):
        _, r, cc = w.shape
        return (r // 2, cc) if ax == 0 else (r, cc // 2)

    def body(*refs):
        ins, outs = refs[:n], refs[n:2 * n]
        send_sems, recv_sems, local_sems = refs[2 * n:]
        x, y, c = _mesh_pos()
        me, sibling = (x, y, c), (x, y, 1 - c)
        chips = [(1 - x, y), (x, 1 - y), (1 - x, 1 - y)]

        def my_half(a):
            r2, c2 = half_shape(stacked[a], axes[a])
            w = ins[a].at[layer]
            if axes[a] == 0:
                return w.at[pl.ds(c * r2, r2), :]
            return w.at[:, pl.ds(c * c2, c2)]

        def slot(a, dev):
            return outs[a].at[4 * dev[0] + 2 * dev[1] + dev[2]]

        def copy(a, k, block, to, src=None):
            dst = slot(a, block)
            return pltpu.make_async_remote_copy(
                src_ref=dst if src is None else src, dst_ref=dst,
                send_sem=send_sems.at[7 * a + k], recv_sem=recv_sems.at[7 * a + k],
                device_id=to, device_id_type=MESH)

        mine, first, passed = [], [], []
        for a in range(n):
            cp = pltpu.make_async_copy(my_half(a), slot(a, me), local_sems.at[a])
            cp.start()
            mine.append(cp)
            cps = [copy(a, 1 + j, me, (*chip, c), src=my_half(a)) for j, chip in enumerate(chips)]
            cps.append(copy(a, 0, me, sibling, src=my_half(a)))
            for cp in cps:
                cp.start()
            first += cps
        for j, chip in enumerate(chips):
            for a in range(n):
                copy(a, 1 + j, (*chip, c), me).wait_recv()
                cp = copy(a, 4 + j, (*chip, c), sibling)
                cp.start()
                passed.append(cp)
        for a in range(n):
            copy(a, 0, sibling, me).wait_recv()
            for j, chip in enumerate(chips):
                copy(a, 4 + j, (*chip, 1 - c), me).wait_recv()
        for cp in first + passed:
            cp.wait_send()
        for cp in mine:
            cp.wait()

    return pl.pallas_call(
        body, name=name,
        out_shape=[jax.ShapeDtypeStruct((8,) + half_shape(w, ax), w.dtype) for w, ax in zip(stacked, axes)],
        in_specs=_any_specs(n), out_specs=_any_specs(n),
        scratch_shapes=[pltpu.SemaphoreType.DMA((7 * n,)), pltpu.SemaphoreType.DMA((7 * n,)),
                        pltpu.SemaphoreType.DMA((n,))],
    )(*stacked)


def _sibling_send_other_halves(name, grads):
    n = len(grads)

    def body(*refs):
        ins, outs = refs[:n], refs[n:2 * n]
        send_sems, recv_sems = refs[2 * n:]
        x, y, c = _mesh_pos()
        cps = []
        for a in range(n):
            for j in range(4):
                cp = pltpu.make_async_remote_copy(
                    src_ref=ins[a].at[j, 1 - c], dst_ref=outs[a].at[j],
                    send_sem=send_sems.at[4 * a + j], recv_sem=recv_sems.at[4 * a + j],
                    device_id=(x, y, 1 - c), device_id_type=MESH)
                cp.start()
                cps.append(cp)
        for cp in cps:
            cp.wait()

    return pl.pallas_call(
        body, name=name,
        out_shape=[jax.ShapeDtypeStruct((4,) + g.shape[2:], g.dtype) for g in grads],
        in_specs=_any_specs(n), out_specs=_any_specs(n),
        scratch_shapes=[pltpu.SemaphoreType.DMA((4 * n,)), pltpu.SemaphoreType.DMA((4 * n,))],
    )(*grads)


def _chip_exchange(name, parts):
    n = len(parts)

    def body(*refs):
        ins, outs = refs[:n], refs[n:2 * n]
        send_sems, recv_sems, local_sems = refs[2 * n:]
        x, y, c = _mesh_pos()
        my_chip = 2 * x + y
        chips = [(1 - x, y), (x, 1 - y), (1 - x, 1 - y)]
        cps, own = [], []
        for a in range(n):
            loc = pltpu.make_async_copy(ins[a].at[my_chip], outs[a].at[my_chip], local_sems.at[a])
            loc.start()
            own.append(loc)
            for k, chip in enumerate(chips):
                cp = pltpu.make_async_remote_copy(
                    src_ref=ins[a].at[2 * chip[0] + chip[1]], dst_ref=outs[a].at[my_chip],
                    send_sem=send_sems.at[3 * a + k], recv_sem=recv_sems.at[3 * a + k],
                    device_id=(*chip, c), device_id_type=MESH)
                cp.start()
                cps.append((cp, a, k, chip))
        for cp, a, k, chip in cps:
            cp.wait_send()
            pltpu.make_async_remote_copy(
                src_ref=ins[a].at[my_chip], dst_ref=outs[a].at[2 * chip[0] + chip[1]],
                send_sem=send_sems.at[3 * a + k], recv_sem=recv_sems.at[3 * a + k],
                device_id=(*chip, c), device_id_type=MESH).wait_recv()
        for loc in own:
            loc.wait()

    return pl.pallas_call(
        body, name=name,
        out_shape=[jax.ShapeDtypeStruct(p.shape, p.dtype) for p in parts],
        in_specs=_any_specs(n), out_specs=_any_specs(n),
        scratch_shapes=[pltpu.SemaphoreType.DMA((3 * n,)), pltpu.SemaphoreType.DMA((3 * n,)),
                        pltpu.SemaphoreType.DMA((n,))],
    )(*parts)


def _sibling_join_halves(name, halves, groups):
    n = len(halves)
    ng = len(groups)
    where = {}
    for gi, grp in enumerate(groups):
        for li, i in enumerate(grp):
            where[i] = (gi, li)

    def body(*refs):
        ins, outs = refs[:n], refs[n:n + ng]
        send_sems, recv_sems, local_sems = refs[n + ng:]
        x, y, c = _mesh_pos()
        cps = []
        for i in range(n):
            gi, li = where[i]
            loc = pltpu.make_async_copy(ins[i], outs[gi].at[li, c], local_sems.at[i])
            loc.start()
            rem = pltpu.make_async_remote_copy(
                src_ref=ins[i], dst_ref=outs[gi].at[li, c],
                send_sem=send_sems.at[i], recv_sem=recv_sems.at[i],
                device_id=(x, y, 1 - c), device_id_type=MESH)
            rem.start()
            cps.append((loc, rem, i))
        for loc, rem, i in cps:
            gi, li = where[i]
            loc.wait()
            rem.wait_send()
            pltpu.make_async_remote_copy(
                src_ref=ins[i], dst_ref=outs[gi].at[li, 1 - c],
                send_sem=send_sems.at[i], recv_sem=recv_sems.at[i],
                device_id=(x, y, 1 - c), device_id_type=MESH).wait_recv()

    out_shape = [jax.ShapeDtypeStruct((len(grp), 2) + halves[grp[0]].shape, halves[grp[0]].dtype) for grp in groups]
    return pl.pallas_call(
        body, name=name, out_shape=out_shape,
        in_specs=_any_specs(n), out_specs=_any_specs(ng),
        scratch_shapes=[pltpu.SemaphoreType.DMA((n,)), pltpu.SemaphoreType.DMA((n,)), pltpu.SemaphoreType.DMA((n,))],
    )(*halves)


def _row_tile(rows, want):
    t = 1
    while t * 2 <= min(rows, want):
        t *= 2
    while rows % t:
        t //= 2
    return t


def _cast_bf16(name, w):
    l, r, c = w.shape
    tr = _row_tile(r, 512 if c <= 1536 else 256)

    def body(w_ref, o_ref):
        o_ref[...] = w_ref[...].astype(BF16)

    spec = pl.BlockSpec((None, tr, c), lambda i, j: (i, j, 0))
    return pl.pallas_call(
        body, name=name, grid=(l, r // tr), in_specs=[spec], out_specs=spec,
        out_shape=jax.ShapeDtypeStruct(w.shape, BF16), compiler_params=_cp(("parallel", "parallel")),
    )(w)


def _add_halves(name, grads, recvd, c_idx):
    outs = []
    for a, (g, r) in enumerate(zip(grads, recvd)):
        _, _, r2, cc = g.shape
        tr = _row_tile(r2, 512)

        def body(c_ref, g_ref, r_ref, o_ref):
            o_ref[...] = (g_ref[...].astype(F32) + r_ref[...].astype(F32)).astype(BF16)

        outs.append(pl.pallas_call(
            body, name=f"{name}_{a}",
            grid_spec=pltpu.PrefetchScalarGridSpec(
                num_scalar_prefetch=1, grid=(4, r2 // tr),
                in_specs=[pl.BlockSpec((None, None, tr, cc), lambda j, i, c_ref: (j, c_ref[0], i, 0)),
                          pl.BlockSpec((None, tr, cc), lambda j, i, c_ref: (j, i, 0))],
                out_specs=pl.BlockSpec((None, tr, cc), lambda j, i, c_ref: (j, i, 0))),
            out_shape=jax.ShapeDtypeStruct(r.shape, BF16), compiler_params=_cp(("parallel", "parallel")),
        )(c_idx, g, r))
    return outs


def _sum_parts(name, recvd):
    outs = []
    for a, r in enumerate(recvd):
        _, r2, cc = r.shape
        tr = _row_tile(r2, 256)

        def body(r_ref, o_ref):
            acc = r_ref[0].astype(F32)
            for i in range(1, 4):
                acc = acc + r_ref[i].astype(F32)
            o_ref[...] = acc

        outs.append(pl.pallas_call(
            body, name=f"{name}_{a}", grid=(r2 // tr,),
            in_specs=[pl.BlockSpec((4, tr, cc), lambda i: (0, i, 0))],
            out_specs=pl.BlockSpec((tr, cc), lambda i: (i, 0)),
            out_shape=jax.ShapeDtypeStruct((r2, cc), F32), compiler_params=_cp(("parallel",)),
        )(r))
    return outs


def _sum8(name, gathered):
    _, r, cc = gathered.shape
    tr = _row_tile(r, 512)

    def body(g_ref, o_ref):
        acc = g_ref[0]
        for i in range(1, 8):
            acc = acc + g_ref[i]
        o_ref[...] = acc

    return pl.pallas_call(
        body, name=name, grid=(r // tr,),
        in_specs=[pl.BlockSpec((8, tr, cc), lambda i: (0, i, 0))],
        out_specs=pl.BlockSpec((tr, cc), lambda i: (i, 0)),
        out_shape=jax.ShapeDtypeStruct((r, cc), F32), compiler_params=_cp(("parallel",)),
    )(gathered)


def _adamw(name, w, g, m, v):
    r, cc = w.shape
    tr = _row_tile(r, max(8, (256 * 1024) // cc))

    def body(w_ref, g_ref, m_ref, v_ref, d_ref, mo_ref, vo_ref):
        gg = g_ref[...]
        mn = ADAM_B1 * m_ref[...] + (1.0 - ADAM_B1) * gg
        vn = ADAM_B2 * v_ref[...] + (1.0 - ADAM_B2) * (gg * gg)
        m_hat = mn / (1.0 - ADAM_B1 ** ADAM_STEP)
        v_hat = vn / (1.0 - ADAM_B2 ** ADAM_STEP)
        d_ref[...] = -ADAM_LR * (m_hat / (jnp.sqrt(v_hat) + ADAM_EPS) + ADAM_WD * w_ref[...])
        mo_ref[...] = mn
        vo_ref[...] = vn

    spec = pl.BlockSpec((tr, cc), lambda i: (i, 0))
    return pl.pallas_call(
        body, name=name, grid=(r // tr,), in_specs=[spec] * 4, out_specs=[spec] * 3,
        out_shape=[jax.ShapeDtypeStruct(w.shape, F32)] * 3, compiler_params=_cp(("parallel",)),
    )(w, g, m, v)


def _rms_fwd(name, h, g):
    s, d = h.shape
    tr = _row_tile(s, 512)

    def body(h_ref, g_ref, y_ref):
        xv = h_ref[...]
        r = lax.rsqrt(jnp.mean(xv * xv, axis=-1, keepdims=True) + EPS)
        y_ref[...] = (xv * r * g_ref[...]).astype(BF16)

    return pl.pallas_call(
        body, name=name, grid=(s // tr,),
        in_specs=[pl.BlockSpec((tr, d), lambda i: (i, 0)), pl.BlockSpec((1, d), lambda i: (0, 0))],
        out_specs=pl.BlockSpec((tr, d), lambda i: (i, 0)),
        out_shape=jax.ShapeDtypeStruct((s, d), BF16), compiler_params=_cp(("parallel",)),
    )(h, g)


def _rms_bwd_rows(xv, gv, dy):
    d = xv.shape[-1]
    r = lax.rsqrt(jnp.mean(xv * xv, axis=-1, keepdims=True) + EPS)
    dxn = dy * gv
    proj = jnp.sum(dxn * xv, axis=-1, keepdims=True) * (1.0 / d)
    dx = r * dxn - xv * (r * r * r) * proj
    return dx, dy * xv * r


def _rms_bwd(name, h, g, dy, dres):
    s, d = h.shape
    tr = _row_tile(s, 256)

    def body(h_ref, g_ref, dy_ref, dres_ref, dh_ref, dhb_ref, dg_ref):
        dx, dgp = _rms_bwd_rows(h_ref[...], g_ref[...], dy_ref[...].astype(F32))
        dh = dres_ref[...] + dx
        dh_ref[...] = dh
        dhb_ref[...] = dh.astype(BF16)

        @pl.when(pl.program_id(0) == 0)
        def _():
            dg_ref[...] = jnp.zeros_like(dg_ref)

        dg_ref[...] += jnp.sum(dgp, axis=0, keepdims=True)

    row = pl.BlockSpec((tr, d), lambda i: (i, 0))
    vec = pl.BlockSpec((1, d), lambda i: (0, 0))
    return pl.pallas_call(
        body, name=name, grid=(s // tr,), in_specs=[row, vec, row, row], out_specs=[row, row, vec],
        out_shape=[jax.ShapeDtypeStruct((s, d), F32), jax.ShapeDtypeStruct((s, d), BF16),
                   jax.ShapeDtypeStruct((1, d), F32)],
        compiler_params=_cp(("arbitrary",)),
    )(h, g, dy, dres)


def _loss_head(name, h, g, target):
    s, d = h.shape
    tr = _row_tile(s, 256)

    def body(h_ref, g_ref, t_ref, loss_ref, dh_ref, dhb_ref, dg_ref):
        xv = h_ref[...]
        gv = g_ref[...]
        r = lax.rsqrt(jnp.mean(xv * xv, axis=-1, keepdims=True) + EPS)
        err = xv * r * gv - t_ref[...]
        part = 0.5 * jnp.sum(jnp.sum(err * err, axis=-1, keepdims=True) * (1.0 / d), axis=0, keepdims=True)
        dx, dgp = _rms_bwd_rows(xv, gv, err * (1.0 / d))
        dh_ref[...] = dx
        dhb_ref[...] = dx.astype(BF16)

        @pl.when(pl.program_id(0) == 0)
        def _():
            dg_ref[...] = jnp.zeros_like(dg_ref)
            loss_ref[...] = jnp.zeros_like(loss_ref)

        dg_ref[...] += jnp.sum(dgp, axis=0, keepdims=True)
        loss_ref[...] += part

    row = pl.BlockSpec((tr, d), lambda i: (i, 0))
    vec = pl.BlockSpec((1, d), lambda i: (0, 0))
    one = pl.BlockSpec((1, 1), lambda i: (0, 0))
    return pl.pallas_call(
        body, name=name, grid=(s // tr,), in_specs=[row, vec, row], out_specs=[one, row, row, vec],
        out_shape=[jax.ShapeDtypeStruct((1, 1), F32), jax.ShapeDtypeStruct((s, d), F32),
                   jax.ShapeDtypeStruct((s, d), BF16), jax.ShapeDtypeStruct((1, d), F32)],
        compiler_params=_cp(("arbitrary",)),
    )(h, g, target)


NN = (((1,), (0,)), ((), ()))
NT = (((1,), (1,)), ((), ()))
TN = (((0,), (0,)), ((), ()))


def _matmul(name, grid, operands, in_specs, pairs, n_acc, acc_shape, epilogue, out_specs, out_shape):
    n_in = len(operands)
    n_out = len(out_shape)
    nk = grid[2]

    def body(*refs):
        ins, outs, accs = refs[:n_in], refs[n_in:n_in + n_out], refs[n_in + n_out:]
        k = pl.program_id(2)

        @pl.when(k == 0)
        def _():
            for acc in accs:
                acc[...] = jnp.zeros_like(acc)

        for li, ri, ai, dims in pairs:
            accs[ai][...] += lax.dot_general(ins[li][...], ins[ri][...], dims, preferred_element_type=F32)

        @pl.when(k == nk - 1)
        def _():
            epilogue([acc[...] for acc in accs], ins, outs)

    return pl.pallas_call(
        body, name=name, grid=grid, in_specs=in_specs, out_specs=out_specs, out_shape=out_shape,
        scratch_shapes=[pltpu.VMEM(acc_shape, F32)] * n_acc,
        compiler_params=_cp(("parallel", "parallel", "arbitrary")),
    )(*operands)


def _ep_store(dtype):
    def ep(accs, ins, outs):
        outs[0][...] = accs[0].astype(dtype)
    return ep


def _w_spec_nn(wg, kind, tk, tn):
    _, a, b = wg.shape
    if kind == "col":
        kph, bps = a // tk, b // tn
        return pl.BlockSpec((None, tk, tn), lambda i, j, k: (2 * (j // bps) + k // kph, k % kph, j % bps))
    kps, nph = a // tk, b // tn
    return pl.BlockSpec((None, tk, tn), lambda i, j, k: (2 * (k // kps) + j // nph, k % kps, j % nph))


def _w_spec_nt(wg, kind, to, tc):
    _, a, b = wg.shape
    if kind == "col":
        oph, cps = a // to, b // tc
        return pl.BlockSpec((None, to, tc), lambda i, jo, kc: (2 * (kc // cps) + jo // oph, jo % oph, kc % cps))
    ops, cph = a // to, b // tc
    return pl.BlockSpec((None, to, tc), lambda i, jo, kc: (2 * (jo // ops) + kc // cph, jo % ops, kc % cph))


def _mm_nn(name, x, wg, kind, tiles, epilogue=None, extra=(), extra_specs=(), out_specs=None, out_shape=None):
    m, kdim = x.shape
    _, a, b = wg.shape
    n = 4 * b if kind == "col" else 2 * b
    tm, tn, tk = tiles
    tm = min(tm, m)
    grid = (m // tm, n // tn, kdim // tk)
    in_specs = [pl.BlockSpec((tm, tk), lambda i, j, k: (i, k)), _w_spec_nn(wg, kind, tk, tn)] + list(extra_specs)
    if out_shape is None:
        out_shape = [jax.ShapeDtypeStruct((m, n), F32)]
        out_specs = [pl.BlockSpec((tm, tn), lambda i, j, k: (i, j))]
        epilogue = _ep_store(F32)
    return _matmul(name, grid, [x, wg] + list(extra), in_specs, [(0, 1, 0, NN)], 1, (tm, tn), epilogue,
                   out_specs, out_shape)


def _mm_nt(name, dy, wg, kind, tiles, out_dtype=BF16):
    m, n = dy.shape
    _, a, b = wg.shape
    kdim = 2 * a if kind == "col" else 4 * a
    tm, to, tc = tiles
    tm = min(tm, m)
    grid = (m // tm, kdim // to, n // tc)
    in_specs = [pl.BlockSpec((tm, tc), lambda i, j, k: (i, k)), _w_spec_nt(wg, kind, to, tc)]
    return _matmul(name, grid, [dy, wg], in_specs, [(0, 1, 0, NT)], 1, (tm, to), _ep_store(out_dtype),
                   [pl.BlockSpec((tm, to), lambda i, j, k: (i, j))], [jax.ShapeDtypeStruct((m, kdim), out_dtype)])[0]


def _mm_tn(name, a, dy, kind, tiles):
    m, kdim = a.shape
    _, n = dy.shape
    tr, tn, tmm = tiles
    tmm = min(tmm, m)
    grid = (kdim // tr, n // tn, m // tmm)
    in_specs = [pl.BlockSpec((tmm, tr), lambda r, j, k: (k, r)), pl.BlockSpec((tmm, tn), lambda r, j, k: (k, j))]
    if kind == "col":
        ns = n // 4
        bps = ns // tn
        out_shape = [jax.ShapeDtypeStruct((4, kdim, ns), BF16)]
        out_specs = [pl.BlockSpec((None, tr, tn), lambda r, j, k: (j // bps, r, j % bps))]
    else:
        rs = kdim // 4
        rps = rs // tr
        out_shape = [jax.ShapeDtypeStruct((4, rs, n), BF16)]
        out_specs = [pl.BlockSpec((None, tr, tn), lambda r, j, k: (r // rps, r % rps, j))]
    return _matmul(name, grid, [a, dy], in_specs, [(0, 1, 0, TN)], 1, (tr, tn), _ep_store(BF16),
                   out_specs, out_shape)[0]


def _zero_halo(pad_ref, s):
    z = jnp.zeros((HALO, pad_ref.shape[1]), F32)
    pad_ref[pl.ds(0, HALO), :] = z
    pad_ref[pl.ds(HALO + s, HALO), :] = z


def _window(pad_ref, r0, rows):
    return pad_ref[pl.ds(r0, rows + 2 * HALO), :]


def _delayed(win, k, rows):
    if k == 0:
        return win[HALO:HALO + rows]
    return pltpu.roll(win, k, axis=0)[HALO:HALO + rows]


def _advanced(win, k, rows):
    if k == 0:
        return win[HALO:HALO + rows]
    return pltpu.roll(win, win.shape[0] - k, axis=0)[HALO:HALO + rows]


def _fold8(x):
    return jnp.sum(x.reshape(x.shape[0] // 8, 8, x.shape[1]), axis=0)


def _chunks(s):
    rows = min(CHUNK_ROWS, s)
    return rows, s // rows


def _col_spec(s, first_block):
    return pl.BlockSpec((s, HEAD), lambda j: (0, first_block + j))


def _sgu_fwd(name, z, ln_g, ln_b, w, b):
    s = z.shape[0]
    tr = _row_tile(s, 1024)
    nh = GROUP // HEAD

    def body(u_ref, v_ref, lg_ref, lb_ref, w_ref, b_ref, o_ref):
        row = lax.broadcasted_iota(jnp.int32, (HEAD, HEAD), 0)
        col = lax.broadcasted_iota(jnp.int32, (HEAD, HEAD), 1)
        wm = jnp.where(row >= col, w_ref[...], 0.0).astype(BF16)
        for ck in range(tr // HEAD):
            rs = pl.ds(ck * HEAD, HEAD)
            u = _gelu(u_ref[rs, :])
            v = _gelu(v_ref[rs, :])
            mu = jnp.mean(v, axis=-1, keepdims=True)
            vc = v - mu
            var = jnp.mean(vc * vc, axis=-1, keepdims=True)
            vln = vc * lax.rsqrt(var + EPS) * lg_ref[...] + lb_ref[...]
            sp = jnp.dot(wm, vln.astype(BF16), preferred_element_type=F32) + b_ref[...]
            o_ref[rs, :] = (u * sp).astype(BF16)

    head_vec = pl.BlockSpec((None, 1, HEAD), lambda h, i: (h, 0, 0))
    return pl.pallas_call(
        body, name=name, grid=(nh, s // tr),
        in_specs=[pl.BlockSpec((tr, HEAD), lambda h, i: (i, h)), pl.BlockSpec((tr, HEAD), lambda h, i: (i, nh + h)),
                  head_vec, head_vec, pl.BlockSpec((None, HEAD, HEAD), lambda h, i: (h, 0, 0)),
                  pl.BlockSpec((None, HEAD, 1), lambda h, i: (h, 0, 0))],
        out_specs=pl.BlockSpec((tr, HEAD), lambda h, i: (i, h)),
        out_shape=jax.ShapeDtypeStruct((s, GROUP), BF16), compiler_params=_cp(("parallel", "parallel")),
    )(z, z, ln_g, ln_b, w, b)


def _sgu_bwd(name, z, d_o, ln_g, ln_b, w, b):
    s = z.shape[0]
    tr = _row_tile(s, 1024)
    nh = GROUP // HEAD

    def body(u_ref, v_ref, do_ref, lg_ref, lb_ref, w_ref, b_ref, du_ref, dv_ref, dlg_ref, dlb_ref, dw_ref, db_ref,
             dsp_acc):
        row = lax.broadcasted_iota(jnp.int32, (HEAD, HEAD), 0)
        col = lax.broadcasted_iota(jnp.int32, (HEAD, HEAD), 1)
        tril = row >= col
        wm = jnp.where(tril, w_ref[...], 0.0).astype(BF16)
        i = pl.program_id(1)

        @pl.when(i == 0)
        def _():
            dlg_ref[...] = jnp.zeros_like(dlg_ref)
            dlb_ref[...] = jnp.zeros_like(dlb_ref)
            dw_ref[...] = jnp.zeros_like(dw_ref)
            dsp_acc[...] = jnp.zeros_like(dsp_acc)

        dlg = jnp.zeros((1, HEAD), F32)
        dlb = jnp.zeros((1, HEAD), F32)
        dw = jnp.zeros((HEAD, HEAD), F32)
        dsp_sum = jnp.zeros((HEAD, HEAD), F32)
        for ck in range(tr // HEAD):
            rs = pl.ds(ck * HEAD, HEAD)
            zu = u_ref[rs, :]
            zv = v_ref[rs, :]
            u = _gelu(zu)
            v = _gelu(zv)
            mu = jnp.mean(v, axis=-1, keepdims=True)
            vc = v - mu
            var = jnp.mean(vc * vc, axis=-1, keepdims=True)
            rstd = lax.rsqrt(var + EPS)
            xh = vc * rstd
            vln = (xh * lg_ref[...] + lb_ref[...]).astype(BF16)
            sp = jnp.dot(wm, vln, preferred_element_type=F32) + b_ref[...]
            d_oa = do_ref[rs, :].astype(F32)
            du = d_oa * sp
            dsp = d_oa * u
            dsp_b = dsp.astype(BF16)
            dvln = lax.dot_general(wm, dsp_b, TN, preferred_element_type=F32)
            dw = dw + lax.dot_general(dsp_b, vln, NT, preferred_element_type=F32)
            dsp_sum = dsp_sum + dsp
            dlg = dlg + jnp.sum(dvln * xh, axis=0, keepdims=True)
            dlb = dlb + jnp.sum(dvln, axis=0, keepdims=True)
            dxh = dvln * lg_ref[...]
            dv = rstd * (dxh - jnp.mean(dxh, axis=-1, keepdims=True)
                         - xh * jnp.mean(dxh * xh, axis=-1, keepdims=True))
            du_ref[rs, :] = (du * _gelu_grad(zu)).astype(BF16)
            dv_ref[rs, :] = (dv * _gelu_grad(zv)).astype(BF16)
        dlg_ref[...] += dlg
        dlb_ref[...] += dlb
        dw_ref[...] += jnp.where(tril, dw, 0.0)
        dsp_acc[...] += dsp_sum

        @pl.when(i == pl.num_programs(1) - 1)
        def _():
            db_ref[...] = jnp.sum(dsp_acc[...], axis=1, keepdims=True)

    head_vec = pl.BlockSpec((None, 1, HEAD), lambda h, i: (h, 0, 0))
    head_mat = pl.BlockSpec((None, HEAD, HEAD), lambda h, i: (h, 0, 0))
    head_col = pl.BlockSpec((None, HEAD, 1), lambda h, i: (h, 0, 0))
    return pl.pallas_call(
        body, name=name, grid=(nh, s // tr),
        in_specs=[pl.BlockSpec((tr, HEAD), lambda h, i: (i, h)), pl.BlockSpec((tr, HEAD), lambda h, i: (i, nh + h)),
                  pl.BlockSpec((tr, HEAD), lambda h, i: (i, h)), head_vec, head_vec, head_mat, head_col],
        out_specs=[pl.BlockSpec((tr, HEAD), lambda h, i: (i, h)), pl.BlockSpec((tr, HEAD), lambda h, i: (i, h)),
                   head_vec, head_vec, head_mat, head_col],
        out_shape=[jax.ShapeDtypeStruct((s, GROUP), BF16), jax.ShapeDtypeStruct((s, GROUP), BF16),
                   jax.ShapeDtypeStruct((nh, 1, HEAD), F32), jax.ShapeDtypeStruct((nh, 1, HEAD), F32),
                   jax.ShapeDtypeStruct((nh, HEAD, HEAD), F32), jax.ShapeDtypeStruct((nh, HEAD, 1), F32)],
        scratch_shapes=[pltpu.VMEM((HEAD, HEAD), F32)],
        compiler_params=_cp(("parallel", "arbitrary")),
    )(z, z, d_o, ln_g, ln_b, w, b)


def _shortconv_fwd(name, z, w):
    s = z.shape[0]
    kw = w.shape[0]
    rows, nchunk = _chunks(s)
    nb = GROUP // HEAD

    def body(h_ref, bg_ref, cg_ref, w_ref, o_ref, pad):
        _zero_halo(pad, s)

        def fill(ci, carry):
            r0 = pl.multiple_of(ci * rows, rows)
            pad[pl.ds(pl.multiple_of(HALO + r0, 8), rows), :] = cg_ref[pl.ds(r0, rows), :] * h_ref[pl.ds(r0, rows), :]
            return carry

        lax.fori_loop(0, nchunk, fill, 0)

        def step(ci, carry):
            r0 = pl.multiple_of(ci * rows, rows)
            win = _window(pad, r0, rows)
            cv = jnp.zeros((rows, HEAD), F32)
            for k in range(kw):
                cv = cv + w_ref[k:k + 1, :] * _delayed(win, kw - 1 - k, rows)
            o_ref[pl.ds(r0, rows), :] = (bg_ref[pl.ds(r0, rows), :] * cv).astype(BF16)
            return carry

        lax.fori_loop(0, nchunk, step, 0)

    return pl.pallas_call(
        body, name=name, grid=(nb,),
        in_specs=[_col_spec(s, 8), _col_spec(s, 12), _col_spec(s, 16), pl.BlockSpec((kw, HEAD), lambda j: (0, j))],
        out_specs=_col_spec(s, 0),
        out_shape=jax.ShapeDtypeStruct((s, GROUP), BF16),
        scratch_shapes=[pltpu.VMEM((s + 2 * HALO, HEAD), F32)],
        compiler_params=_cp(("parallel",)),
    )(z, z, z, w)


def _shortconv_bwd(name, z, d_o, w):
    s = z.shape[0]
    kw = w.shape[0]
    rows, nchunk = _chunks(s)
    nb = GROUP // HEAD

    def body(h_ref, bg_ref, cg_ref, do_ref, w_ref, dh_ref, dbg_ref, dcg_ref, dw_ref, pad_q, pad_d, acc):
        _zero_halo(pad_q, s)
        _zero_halo(pad_d, s)
        acc[...] = jnp.zeros_like(acc)

        def fill(ci, carry):
            r0 = pl.multiple_of(ci * rows, rows)
            rs = pl.ds(r0, rows)
            ps = pl.ds(pl.multiple_of(HALO + r0, 8), rows)
            pad_q[ps, :] = cg_ref[rs, :] * h_ref[rs, :]
            pad_d[ps, :] = do_ref[rs, :].astype(F32) * bg_ref[rs, :]
            return carry

        lax.fori_loop(0, nchunk, fill, 0)

        def step(ci, carry):
            r0 = pl.multiple_of(ci * rows, rows)
            rs = pl.ds(r0, rows)
            wq = _window(pad_q, r0, rows)
            wd = _window(pad_d, r0, rows)
            dcv = wd[HALO:HALO + rows]
            cv = jnp.zeros((rows, HEAD), F32)
            dq = jnp.zeros((rows, HEAD), F32)
            for k in range(kw):
                qk = _delayed(wq, kw - 1 - k, rows)
                cv = cv + w_ref[k:k + 1, :] * qk
                dq = dq + w_ref[k:k + 1, :] * _advanced(wd, kw - 1 - k, rows)
                acc[k] += _fold8(dcv * qk)
            dbg_ref[rs, :] = (do_ref[rs, :].astype(F32) * cv).astype(BF16)
            dcg_ref[rs, :] = (dq * h_ref[rs, :]).astype(BF16)
            dh_ref[rs, :] = (dq * cg_ref[rs, :]).astype(BF16)
            return carry

        lax.fori_loop(0, nchunk, step, 0)
        for k in range(kw):
            dw_ref[k:k + 1, :] = jnp.sum(acc[k], axis=0, keepdims=True)

    col = _col_spec(s, 0)
    return pl.pallas_call(
        body, name=name, grid=(nb,),
        in_specs=[_col_spec(s, 8), _col_spec(s, 12), _col_spec(s, 16), col, pl.BlockSpec((kw, HEAD), lambda j: (0, j))],
        out_specs=[col, col, col, pl.BlockSpec((kw, HEAD), lambda j: (0, j))],
        out_shape=[jax.ShapeDtypeStruct((s, GROUP), BF16)] * 3 + [jax.ShapeDtypeStruct((kw, GROUP), F32)],
        scratch_shapes=[pltpu.VMEM((s + 2 * HALO, HEAD), F32), pltpu.VMEM((s + 2 * HALO, HEAD), F32),
                        pltpu.VMEM((kw, 8, HEAD), F32)],
        compiler_params=_cp(("parallel",)),
    )(z, z, z, d_o, w)


def _conformer_conv_fwd(name, z, w, bias):
    s = z.shape[0]
    kw = w.shape[0]
    rows, nchunk = _chunks(s)
    nb = GROUP // HEAD

    def body(a_ref, g_ref, w_ref, b_ref, o_ref, pad):
        _zero_halo(pad, s)

        def fill(ci, carry):
            r0 = pl.multiple_of(ci * rows, rows)
            rs = pl.ds(r0, rows)
            pad[pl.ds(pl.multiple_of(HALO + r0, 8), rows), :] = a_ref[rs, :] * _sigmoid(g_ref[rs, :])
            return carry

        lax.fori_loop(0, nchunk, fill, 0)

        def step(ci, carry):
            r0 = pl.multiple_of(ci * rows, rows)
            win = _window(pad, r0, rows)
            cc = jnp.zeros((rows, HEAD), F32)
            for k in range(kw):
                cc = cc + w_ref[k:k + 1, :] * _delayed(win, kw - 1 - k, rows)
            o_ref[pl.ds(r0, rows), :] = cc + b_ref[...]
            return carry

        lax.fori_loop(0, nchunk, step, 0)

    return pl.pallas_call(
        body, name=name, grid=(nb,),
        in_specs=[_col_spec(s, 20), _col_spec(s, 24), pl.BlockSpec((kw, HEAD), lambda j: (0, j)),
                  pl.BlockSpec((1, HEAD), lambda j: (0, j))],
        out_specs=_col_spec(s, 0),
        out_shape=jax.ShapeDtypeStruct((s, GROUP), F32),
        scratch_shapes=[pltpu.VMEM((s + 2 * HALO, HEAD), F32)],
        compiler_params=_cp(("parallel",)),
    )(z, z, w, bias)


def _ln_rows(cc, g, b):
    mu = jnp.mean(cc, axis=-1, keepdims=True)
    xc = cc - mu
    var = jnp.mean(xc * xc, axis=-1, keepdims=True)
    rstd = lax.rsqrt(var + EPS)
    xh = xc * rstd
    return xh, rstd, xh * g + b


def _conformer_ln_fwd(name, cc, g, b):
    s, d = cc.shape
    tr = _row_tile(s, 512)

    def body(c_ref, g_ref, b_ref, o_ref):
        _, _, l = _ln_rows(c_ref[...], g_ref[...], b_ref[...])
        o_ref[...] = (l * _sigmoid(l)).astype(BF16)

    row = pl.BlockSpec((tr, d), lambda i: (i, 0))
    vec = pl.BlockSpec((1, d), lambda i: (0, 0))
    return pl.pallas_call(
        body, name=name, grid=(s // tr,), in_specs=[row, vec, vec], out_specs=row,
        out_shape=jax.ShapeDtypeStruct((s, d), BF16), compiler_params=_cp(("parallel",)),
    )(cc, g, b)


def _conformer_ln_bwd(name, cc, d_o, g, b):
    s, d = cc.shape
    tr = _row_tile(s, 512)

    def body(c_ref, do_ref, g_ref, b_ref, dcc_ref, dg_ref, db_ref, dcb_ref):
        xh, rstd, l = _ln_rows(c_ref[...], g_ref[...], b_ref[...])
        sg = _sigmoid(l)
        dl = do_ref[...].astype(F32) * sg * (1.0 + l * (1.0 - sg))
        dxh = dl * g_ref[...]
        dcc = rstd * (dxh - jnp.mean(dxh, axis=-1, keepdims=True) - xh * jnp.mean(dxh * xh, axis=-1, keepdims=True))
        dcc_ref[...] = dcc

        @pl.when(pl.program_id(0) == 0)
        def _():
            dg_ref[...] = jnp.zeros_like(dg_ref)
            db_ref[...] = jnp.zeros_like(db_ref)
            dcb_ref[...] = jnp.zeros_like(dcb_ref)

        dg_ref[...] += jnp.sum(dl * xh, axis=0, keepdims=True)
        db_ref[...] += jnp.sum(dl, axis=0, keepdims=True)
        dcb_ref[...] += jnp.sum(dcc, axis=0, keepdims=True)

    row = pl.BlockSpec((tr, d), lambda i: (i, 0))
    vec = pl.BlockSpec((1, d), lambda i: (0, 0))
    return pl.pallas_call(
        body, name=name, grid=(s // tr,), in_specs=[row, row, vec, vec], out_specs=[row, vec, vec, vec],
        out_shape=[jax.ShapeDtypeStruct((s, d), F32)] + [jax.ShapeDtypeStruct((1, d), F32)] * 3,
        compiler_params=_cp(("arbitrary",)),
    )(cc, d_o, g, b)


def _conformer_conv_bwd(name, z, dcc, w):
    s = z.shape[0]
    kw = w.shape[0]
    rows, nchunk = _chunks(s)
    nb = GROUP // HEAD

    def body(a_ref, g_ref, d_ref, w_ref, da_ref, dg_ref, dw_ref, pad_h, pad_d, acc):
        _zero_halo(pad_h, s)
        _zero_halo(pad_d, s)
        acc[...] = jnp.zeros_like(acc)

        def fill(ci, carry):
            r0 = pl.multiple_of(ci * rows, rows)
            rs = pl.ds(r0, rows)
            ps = pl.ds(pl.multiple_of(HALO + r0, 8), rows)
            pad_h[ps, :] = a_ref[rs, :] * _sigmoid(g_ref[rs, :])
            pad_d[ps, :] = d_ref[rs, :]
            return carry

        lax.fori_loop(0, nchunk, fill, 0)

        def step(ci, carry):
            r0 = pl.multiple_of(ci * rows, rows)
            rs = pl.ds(r0, rows)
            wh = _window(pad_h, r0, rows)
            wd = _window(pad_d, r0, rows)
            dcc_c = wd[HALO:HALO + rows]
            dhc = jnp.zeros((rows, HEAD), F32)
            for k in range(kw):
                dhc = dhc + w_ref[k:k + 1, :] * _advanced(wd, kw - 1 - k, rows)
                acc[k] += _fold8(dcc_c * _delayed(wh, kw - 1 - k, rows))
            sg = _sigmoid(g_ref[rs, :])
            da_ref[rs, :] = (dhc * sg).astype(BF16)
            dg_ref[rs, :] = (dhc * a_ref[rs, :] * sg * (1.0 - sg)).astype(BF16)
            return carry

        lax.fori_loop(0, nchunk, step, 0)
        for k in range(kw):
            dw_ref[k:k + 1, :] = jnp.sum(acc[k], axis=0, keepdims=True)

    col = _col_spec(s, 0)
    return pl.pallas_call(
        body, name=name, grid=(nb,),
        in_specs=[_col_spec(s, 20), _col_spec(s, 24), col, pl.BlockSpec((kw, HEAD), lambda j: (0, j))],
        out_specs=[col, col, pl.BlockSpec((kw, HEAD), lambda j: (0, j))],
        out_shape=[jax.ShapeDtypeStruct((s, GROUP), BF16)] * 2 + [jax.ShapeDtypeStruct((kw, GROUP), F32)],
        scratch_shapes=[pltpu.VMEM((s + 2 * HALO, HEAD), F32), pltpu.VMEM((s + 2 * HALO, HEAD), F32),
                        pltpu.VMEM((kw, 8, HEAD), F32)],
        compiler_params=_cp(("parallel",)),
    )(z, z, dcc, w)


def _pool_window_sum(win, level, rows, shift):
    n = win.shape[0]

    def moved(v, k):
        return pltpu.roll(v, k if shift is _delayed else n - k, axis=0)

    s2 = win + moved(win, 1)
    s4 = s2 + moved(s2, 2)
    s8 = s4 + moved(s4, 4)
    s16 = s8 + moved(s8, 8)
    sel = jnp.where(level == 0, s2, jnp.where(level == 1, s4, jnp.where(level == 2, s8, s16)))
    return sel[HALO:HALO + rows]


def _pool_count(level, r0, rows):
    t = r0 + lax.broadcasted_iota(jnp.int32, (rows, 1), 0)
    width = jnp.left_shift(jnp.int32(2), level)
    return jnp.minimum(t + 1, width).astype(F32)


def _pool_fwd(name, z, pool_w, scale):
    s = z.shape[0]
    rows, nchunk = _chunks(s)

    def body(z_ref, w_ref, sc_ref, o_ref, pad):
        level = pl.program_id(0)
        _zero_halo(pad, s)

        def fill(ci, carry):
            r0 = pl.multiple_of(ci * rows, rows)
            pad[pl.ds(pl.multiple_of(HALO + r0, 8), rows), :] = z_ref[pl.ds(r0, rows), :]
            return carry

        lax.fori_loop(0, nchunk, fill, 0)
        wb = w_ref[...].astype(BF16)

        def step(ci, carry):
            r0 = pl.multiple_of(ci * rows, rows)
            win = _window(pad, r0, rows)
            pm = _pool_window_sum(win, level, rows, _delayed) / _pool_count(level, r0, rows) - win[HALO:HALO + rows]
            r = jnp.dot(pm.astype(BF16), wb, preferred_element_type=F32)
            o_ref[pl.ds(r0, rows), :] = (r * sc_ref[...]).astype(BF16)
            return carry

        lax.fori_loop(0, nchunk, step, 0)

    return pl.pallas_call(
        body, name=name, grid=(POOL_LEVELS,),
        in_specs=[_col_spec(s, 28), pl.BlockSpec((None, HEAD, HEAD), lambda j: (j, 0, 0)),
                  pl.BlockSpec((1, HEAD), lambda j: (0, j))],
        out_specs=_col_spec(s, 0),
        out_shape=jax.ShapeDtypeStruct((s, GROUP), BF16),
        scratch_shapes=[pltpu.VMEM((s + 2 * HALO, HEAD), F32)],
        compiler_params=_cp(("parallel",)),
    )(z, pool_w, scale)


def _pool_bwd(name, z, d_o, pool_w, scale):
    s = z.shape[0]
    rows, nchunk = _chunks(s)

    def body(z_ref, do_ref, w_ref, sc_ref, dz_ref, dw_ref, dsc_ref, pad, pad_q, dw_acc, dsc_acc):
        level = pl.program_id(0)
        _zero_halo(pad, s)
        _zero_halo(pad_q, s)
        dw_acc[...] = jnp.zeros_like(dw_acc)
        dsc_acc[...] = jnp.zeros_like(dsc_acc)

        def fill(ci, carry):
            r0 = pl.multiple_of(ci * rows, rows)
            pad[pl.ds(pl.multiple_of(HALO + r0, 8), rows), :] = z_ref[pl.ds(r0, rows), :]
            return carry

        lax.fori_loop(0, nchunk, fill, 0)
        wb = w_ref[...].astype(BF16)

        def first(ci, carry):
            r0 = pl.multiple_of(ci * rows, rows)
            win = _window(pad, r0, rows)
            cnt = _pool_count(level, r0, rows)
            pm = (_pool_window_sum(win, level, rows, _delayed) / cnt - win[HALO:HALO + rows]).astype(BF16)
            r = jnp.dot(pm, wb, preferred_element_type=F32)
            d_od = do_ref[pl.ds(r0, rows), :].astype(F32)
            dsc_acc[...] += _fold8(d_od * r)
            dr = (d_od * sc_ref[...]).astype(BF16)
            dw_acc[...] += lax.dot_general(pm, dr, TN, preferred_element_type=F32)
            dpm = lax.dot_general(dr, wb, NT, preferred_element_type=F32)
            pad_q[pl.ds(pl.multiple_of(HALO + r0, 8), rows), :] = dpm / cnt
            return carry

        lax.fori_loop(0, nchunk, first, 0)

        def second(ci, carry):
            r0 = pl.multiple_of(ci * rows, rows)
            wq = _window(pad_q, r0, rows)
            dpm = wq[HALO:HALO + rows] * _pool_count(level, r0, rows)
            dz_ref[pl.ds(r0, rows), :] = (_pool_window_sum(wq, level, rows, _advanced) - dpm).astype(BF16)
            return carry

        lax.fori_loop(0, nchunk, second, 0)
        dw_ref[...] = dw_acc[...]
        dsc_ref[...] = jnp.sum(dsc_acc[...], axis=0, keepdims=True)

    col = _col_spec(s, 0)
    mat = pl.BlockSpec((None, HEAD, HEAD), lambda j: (j, 0, 0))
    vec = pl.BlockSpec((1, HEAD), lambda j: (0, j))
    return pl.pallas_call(
        body, name=name, grid=(POOL_LEVELS,),
        in_specs=[_col_spec(s, 28), col, mat, vec], out_specs=[col, mat, vec],
        out_shape=[jax.ShapeDtypeStruct((s, GROUP), BF16), jax.ShapeDtypeStruct((POOL_LEVELS, HEAD, HEAD), F32),
                   jax.ShapeDtypeStruct((1, GROUP), F32)],
        scratch_shapes=[pltpu.VMEM((s + 2 * HALO, HEAD), F32), pltpu.VMEM((s + 2 * HALO, HEAD), F32),
                        pltpu.VMEM((HEAD, HEAD), F32), pltpu.VMEM((8, HEAD), F32)],
        compiler_params=_cp(("parallel",)),
    )(z, d_o, pool_w, scale)


def _ep_residual(accs, ins, outs):
    outs[0][...] = ins[2][...] + accs[0]


def _swiglu_fwd(name, y, wg_gate, wg_up, tiles):
    m, kdim = y.shape
    _, a, b = wg_gate.shape
    n = 4 * b
    tm, tn, tk = tiles
    tm = min(tm, m)
    grid = (m // tm, n // tn, kdim // tk)

    def ep(accs, ins, outs):
        gt, up = accs
        outs[0][...] = (gt * _sigmoid(gt) * up).astype(BF16)
        outs[1][...] = gt.astype(BF16)
        outs[2][...] = up.astype(BF16)

    wspec = _w_spec_nn(wg_gate, "col", tk, tn)
    out = pl.BlockSpec((tm, tn), lambda i, j, k: (i, j))
    return _matmul(name, grid, [y, wg_gate, wg_up], [pl.BlockSpec((tm, tk), lambda i, j, k: (i, k)), wspec, wspec],
                   [(0, 1, 0, NN), (0, 2, 1, NN)], 2, (tm, tn), ep, [out] * 3,
                   [jax.ShapeDtypeStruct((m, n), BF16)] * 3)


def _swiglu_bwd(name, dh, wg_down, gate, up, tiles):
    m, n = dh.shape
    _, a, b = wg_down.shape
    f = 4 * a
    tm, to, tc = tiles
    tm = min(tm, m)
    grid = (m // tm, f // to, n // tc)

    def ep(accs, ins, outs):
        d_act = accs[0]
        gt = ins[2][...].astype(F32)
        upv = ins[3][...].astype(F32)
        sg = _sigmoid(gt)
        outs[0][...] = (d_act * upv * sg * (1.0 + gt * (1.0 - sg))).astype(BF16)
        outs[1][...] = (d_act * gt * sg).astype(BF16)
        outs[2][...] = (gt * sg * upv).astype(BF16)

    tile = pl.BlockSpec((tm, to), lambda i, j, k: (i, j))
    return _matmul(name, grid, [dh, wg_down, gate, up],
                   [pl.BlockSpec((tm, tc), lambda i, j, k: (i, k)), _w_spec_nt(wg_down, "row", to, tc), tile, tile],
                   [(0, 1, 0, NT)], 1, (tm, to), ep, [tile] * 3, [jax.ShapeDtypeStruct((m, f), BF16)] * 3)


def _ffn_dy(name, d_gate, d_up, wg_gate, wg_up, tiles):
    m, n = d_gate.shape
    _, a, b = wg_gate.shape
    kdim = 2 * a
    tm, to, tc = tiles
    tm = min(tm, m)
    grid = (m // tm, kdim // to, n // tc)
    lhs = pl.BlockSpec((tm, tc), lambda i, j, k: (i, k))
    wspec = _w_spec_nt(wg_gate, "col", to, tc)
    return _matmul(name, grid, [d_gate, d_up, wg_gate, wg_up], [lhs, lhs, wspec, wspec],
                   [(0, 2, 0, NT), (1, 3, 0, NT)], 1, (tm, to), _ep_store(BF16),
                   [pl.BlockSpec((tm, to), lambda i, j, k: (i, j))], [jax.ShapeDtypeStruct((m, kdim), BF16)])[0]


def _ple_fwd(name, y, wg, h, pp, tiles):
    m, d = h.shape
    tm, tn, tk = tiles
    tm = min(tm, m)

    def ep(accs, ins, outs):
        pg = accs[0]
        outs[0][...] = ins[2][...] + _sigmoid(pg) * ins[3][...].astype(F32)
        outs[1][...] = pg.astype(BF16)

    tile = pl.BlockSpec((tm, tn), lambda i, j, k: (i, j))
    return _mm_nn(name, y, wg, "row", (tm, tn, tk), epilogue=ep, extra=[h, pp], extra_specs=[tile, tile],
                  out_specs=[tile, tile],
                  out_shape=[jax.ShapeDtypeStruct((m, d), F32), jax.ShapeDtypeStruct((m, d), BF16)])


def _ple_bwd(name, dh, pg, pp):
    s, d = dh.shape
    tr = _row_tile(s, 512)

    def body(dh_ref, pg_ref, pp_ref, dpp_ref, dpg_ref):
        dhv = dh_ref[...]
        sg = _sigmoid(pg_ref[...].astype(F32))
        dpp_ref[...] = (dhv * sg).astype(BF16)
        dpg_ref[...] = (dhv * pp_ref[...].astype(F32) * sg * (1.0 - sg)).astype(BF16)

    row = pl.BlockSpec((tr, d), lambda i: (i, 0))
    return pl.pallas_call(
        body, name=name, grid=(s // tr,), in_specs=[row] * 3, out_specs=[row] * 2,
        out_shape=[jax.ShapeDtypeStruct((s, d), BF16)] * 2, compiler_params=_cp(("parallel",)),
    )(dh, pg, pp)


BIG = ["w_in", "w_out", "w_gate", "w_up", "w_down", "w_ple_gate", "w_ple_proj"]
KIND = {"w_in": "col", "w_out": "row", "w_gate": "col", "w_up": "col", "w_down": "row", "w_ple_gate": "row",
        "w_ple_proj": "col"}
SMALL = ["norm_mix_g", "sgu_ln_g", "sgu_ln_b", "sgu_w", "sgu_b", "sc_conv_w", "cf_conv_w", "cf_conv_b", "cf_ln_g",
         "cf_ln_b", "pool_w", "pool_scale", "norm_ffn_g", "norm_ple_g", "final_norm_g"]
CHIP_SPLIT = ["sc_conv_w", "cf_conv_w"]
WEIGHTS = ['norm_mix_g', 'w_in', 'sgu_ln_g', 'sgu_ln_b', 'sgu_w', 'sgu_b', 'sc_conv_w', 'cf_conv_w', 'cf_conv_b',
           'cf_ln_g', 'cf_ln_b', 'pool_w', 'pool_scale', 'w_out', 'norm_ffn_g', 'w_gate', 'w_up', 'w_down',
           'norm_ple_g', 'w_ple_gate', 'w_ple_proj', 'final_norm_g']


def _tile(n, want):
    if n <= want:
        return n
    t = (want // 128) * 128
    while n % t:
        t -= 128
    return t


def _pack_rows(vecs):
    flat = jnp.concatenate([v.reshape(-1) for v in vecs])
    n = flat.shape[0]
    padded = ((n + 1023) // 1024) * 1024
    return jnp.pad(flat, (0, padded - n)).reshape(padded // 128, 128), n


def _unpack(flat, shapes):
    out, off = [], 0
    for shp in shapes:
        size = math.prod(shp)
        out.append(flat[off:off + size].reshape(shp))
        off += size
    return out


def kernel(x, p, norm_mix_g, w_in, sgu_ln_g, sgu_ln_b, sgu_w, sgu_b, sc_conv_w, cf_conv_w, cf_conv_b, cf_ln_g, cf_ln_b, pool_w, pool_scale, w_out, norm_ffn_g, w_gate, w_up, w_down, norm_ple_g, w_ple_gate, w_ple_proj, final_norm_g, loss_target, m_norm_mix_g, m_w_in, m_sgu_ln_g, m_sgu_ln_b, m_sgu_w, m_sgu_b, m_sc_conv_w, m_cf_conv_w, m_cf_conv_b, m_cf_ln_g, m_cf_ln_b, m_pool_w, m_pool_scale, m_w_out, m_norm_ffn_g, m_w_gate, m_w_up, m_w_down, m_norm_ple_g, m_w_ple_gate, m_w_ple_proj, m_final_norm_g, v_norm_mix_g, v_w_in, v_sgu_ln_g, v_sgu_ln_b, v_sgu_w, v_sgu_b, v_sc_conv_w, v_cf_conv_w, v_cf_conv_b, v_cf_ln_g, v_cf_ln_b, v_pool_w, v_pool_scale, v_w_out, v_norm_ffn_g, v_w_gate, v_w_up, v_w_down, v_norm_ple_g, v_w_ple_gate, v_w_ple_proj, v_final_norm_g):
    args = dict(locals())
    w = {n: args[n] for n in WEIGHTS}
    mom = {n: args["m_" + n] for n in WEIGHTS}
    var = {n: args["v_" + n] for n in WEIGHTS}
    depth = w_in.shape[0]
    s, d = x.shape[1], x.shape[2]
    f_dim = 4 * w_gate.shape[2]
    xi, yi, ci = lax.axis_index("x"), lax.axis_index("y"), lax.axis_index("c")
    c_idx = ci.astype(jnp.int32).reshape(1)

    wb = {n: _cast_bf16(f"cast_{n}", w[n]) for n in BIG}
    axes = [0 if KIND[n] == "col" else 1 for n in BIG]
    gathered = []
    for l in range(depth):
        outs = _gather_weight_halves(f"gather_weights_{l}", [wb[n] for n in BIG], l, axes)
        gathered.append(dict(zip(BIG, outs)))
    conv_pack = jnp.concatenate([sc_conv_w, cf_conv_w], axis=1)
    taps = conv_pack.shape[1]
    rows_pad = ((depth * taps + 7) // 8) * 8
    conv_rows = jnp.pad(conv_pack.reshape(depth * taps, HEAD), ((0, rows_pad - depth * taps), (0, 0)))
    conv_all = _allgather8("gather_conv_weights", [conv_rows])[0]
    conv_full = conv_all[0::2, :depth * taps].reshape(4, depth, taps, HEAD)
    conv_full = jnp.transpose(conv_full, (1, 2, 0, 3)).reshape(depth, taps, GROUP)
    sc_w_full, cf_w_full = conv_full[:, :3], conv_full[:, 3:]

    h = x[0]
    saved = []
    for l in range(depth):
        wg = gathered[l]
        sv = {"h0": h}
        y1 = _rms_fwd("rms_mix", h, norm_mix_g[l:l + 1])
        z = _mm_nn("mm_in", y1, wg["w_in"], "col", (1024, 1024, 1024))[0]
        lg, lb = sgu_ln_g[l][:, None, :], sgu_ln_b[l][:, None, :]
        sb = sgu_b[l][:, :, None]
        oa = _sgu_fwd("sgu_fwd", z, lg, lb, sgu_w[l], sb)
        ob = _shortconv_fwd("shortconv_fwd", z, sc_w_full[l])
        cc = _conformer_conv_fwd("conformer_conv_fwd", z, cf_w_full[l], cf_conv_b[l:l + 1])
        oc = _conformer_ln_fwd("conformer_ln_fwd", cc, cf_ln_g[l:l + 1], cf_ln_b[l:l + 1])
        od = _pool_fwd("pool_fwd", z, pool_w[l], pool_scale[l:l + 1])
        o = jnp.concatenate([oa, ob, oc, od], axis=1)
        tile = pl.BlockSpec((min(1024, s), 1024), lambda i, j, k: (i, j))
        h1 = _mm_nn("mm_out", o, wg["w_out"], "row", (1024, 1024, 512), epilogue=_ep_residual, extra=[h],
                    extra_specs=[tile], out_specs=[tile], out_shape=[jax.ShapeDtypeStruct((s, d), F32)])[0]
        y2 = _rms_fwd("rms_ffn", h1, norm_ffn_g[l:l + 1])
        fs = f_dim // 4
        act, gt, up = _swiglu_fwd("mm_swiglu", y2, wg["w_gate"], wg["w_up"], (512, fs, 1024))
        h2 = _mm_nn("mm_down", act, wg["w_down"], "row", (1024, 1024, fs), epilogue=_ep_residual, extra=[h1],
                    extra_specs=[tile], out_specs=[tile], out_shape=[jax.ShapeDtypeStruct((s, d), F32)])[0]
        y3 = _rms_fwd("rms_ple", h2, norm_ple_g[l:l + 1])
        pb = p[l, 0].astype(BF16)
        ple_k = w_ple_proj.shape[1] // 2
        ptile = pl.BlockSpec((min(1024, s), 512), lambda i, j, k: (i, j))
        pp = _mm_nn("mm_ple_proj", pb, wg["w_ple_proj"], "col", (1024, 512, ple_k), epilogue=_ep_store(BF16),
                    out_specs=[ptile], out_shape=[jax.ShapeDtypeStruct((s, d), BF16)])[0]
        h3, pg = _ple_fwd("mm_ple_gate", y3, wg["w_ple_gate"], h2, pp, (1024, 1024, 512))
        sv.update(y1=y1, z=z, cc=cc, o=o, h1=h1, y2=y2, gt=gt, up=up, h2=h2, y3=y3, pb=pb, pp=pp, pg=pg)
        saved.append(sv)
        h = h3

    loss_part, dh, dhb, d_final_g = _loss_head("loss_head", h, final_norm_g[None, :], loss_target[0])

    small_grads = [None] * depth
    big_grads = [None] * depth
    for l in reversed(range(depth)):
        wg = gathered[l]
        sv = saved[l]
        fs = f_dim // 4
        d_pp, d_pg = _ple_bwd("ple_bwd", dh, sv["pg"], sv["pp"])
        g_ple_proj = _mm_tn("dw_ple_proj", sv["pb"], d_pp, "col", (w_ple_proj.shape[1], 512, 1024))
        g_ple_gate = _mm_tn("dw_ple_gate", sv["y3"], d_pg, "row", (512, 1024, 1024))
        dy3 = _mm_nt("dx_ple_gate", d_pg, wg["w_ple_gate"], "row", (1024, 512, 1024))
        dh, dhb, dg_ple = _rms_bwd("rms_ple_bwd", sv["h2"], norm_ple_g[l:l + 1], dy3, dh)

        d_gt, d_up, act = _swiglu_bwd("dx_down_swiglu", dhb, wg["w_down"], sv["gt"], sv["up"], (512, fs, 1024))
        g_down = _mm_tn("dw_down", act, dhb, "row", (fs, 1024, 1024))
        g_gate = _mm_tn("dw_gate", sv["y2"], d_gt, "col", (1024, fs, 1024))
        g_up = _mm_tn("dw_up", sv["y2"], d_up, "col", (1024, fs, 1024))
        dy2 = _ffn_dy("dx_gate_up", d_gt, d_up, wg["w_gate"], wg["w_up"], (1024, 1024, fs))
        dh, dhb, dg_ffn = _rms_bwd("rms_ffn_bwd", sv["h1"], norm_ffn_g[l:l + 1], dy2, dh)

        g_out = _mm_tn("dw_out", sv["o"], dhb, "row", (512, 1024, 1024))
        d_o = _mm_nt("dx_out", dhb, wg["w_out"], "row", (1024, 512, 1024))
        z = sv["z"]
        lg, lb = sgu_ln_g[l][:, None, :], sgu_ln_b[l][:, None, :]
        sb = sgu_b[l][:, :, None]
        dzu, dzv, d_lg, d_lb, d_sw, d_sb = _sgu_bwd("sgu_bwd", z, d_o[:, 0:GROUP], lg, lb, sgu_w[l], sb)
        dzh, dzbg, dzcg, d_scw = _shortconv_bwd("shortconv_bwd", z, d_o[:, GROUP:2 * GROUP], sc_w_full[l])
        dcc, d_cflg, d_cflb, d_cfb = _conformer_ln_bwd("conformer_ln_bwd", sv["cc"], d_o[:, 2 * GROUP:3 * GROUP],
                                                       cf_ln_g[l:l + 1], cf_ln_b[l:l + 1])
        dza, dzg, d_cfw = _conformer_conv_bwd("conformer_conv_bwd", z, dcc, cf_w_full[l])
        dzd, d_pw, d_psc = _pool_bwd("pool_bwd", z, d_o[:, 3 * GROUP:], pool_w[l], pool_scale[l:l + 1])
        dz = jnp.concatenate([dzu, dzv, dzh, dzbg, dzcg, dza, dzg, dzd], axis=1)
        g_in = _mm_tn("dw_in", sv["y1"], dz, "col", (1024, 1024, 1024))
        dy1 = _mm_nt("dx_in", dz, wg["w_in"], "col", (1024, 1024, 1024))
        dh, dhb, dg_mix = _rms_bwd("rms_mix_bwd", sv["h0"], norm_mix_g[l:l + 1], dy1, dh)

        small_grads[l] = dict(norm_mix_g=dg_mix, sgu_ln_g=d_lg, sgu_ln_b=d_lb, sgu_w=d_sw, sgu_b=d_sb,
                              sc_conv_w=d_scw, cf_conv_w=d_cfw, cf_conv_b=d_cfb, cf_ln_g=d_cflg, cf_ln_b=d_cflb,
                              pool_w=d_pw, pool_scale=d_psc, norm_ffn_g=dg_ffn, norm_ple_g=dg_ple)
        big_grads[l] = dict(w_in=g_in, w_out=g_out, w_gate=g_gate, w_up=g_up, w_down=g_down,
                            w_ple_gate=g_ple_gate, w_ple_proj=g_ple_proj)
    grad_x = dh[None]

    half_sums = []
    for l in range(depth):
        gs = [big_grads[l][n] for n in BIG]
        gs = [g.reshape(4, 2, g.shape[1] // 2, g.shape[2]) for g in gs]
        from_sibling = _sibling_send_other_halves(f"rs_sibling_{l}", gs)
        chip_sums = _add_halves("rs_add", gs, from_sibling, c_idx)
        from_chips = _chip_exchange(f"rs_chips_{l}", chip_sums)
        half_sums += _sum_parts("rs_sum", from_chips)
    groups = [[l * len(BIG) + a for l in range(depth)] for a in range(len(BIG))]
    joined = _sibling_join_halves("rs_join", half_sums, groups)
    grads = {n: joined[a].reshape(w[n].shape) for a, n in enumerate(BIG)}

    per_layer = [n for n in SMALL if n != "final_norm_g"]
    vecs = [small_grads[l][n] for l in range(depth) for n in per_layer] + [d_final_g, loss_part]
    packed, _ = _pack_rows(vecs)
    total = _sum8("sum_small", _allgather8("gather_small", [packed])[0]).reshape(-1)
    full_shapes = {n: (w[n].shape[1:] if n not in CHIP_SPLIT else (w[n].shape[1], GROUP)) for n in per_layer}
    pieces = _unpack(total, [full_shapes[n] for _ in range(depth) for n in per_layer] + [(d,), ()])
    loss = pieces[-1]
    grads["final_norm_g"] = pieces[-2]
    chip_off = (2 * xi + yi) * HEAD
    for j, n in enumerate(per_layer):
        g = jnp.stack([pieces[l * len(per_layer) + j] for l in range(depth)])
        if n in CHIP_SPLIT:
            g = lax.dynamic_slice_in_dim(g, chip_off, HEAD, axis=2)
        grads[n] = g

    delta, new_m, new_v = {}, {}, {}
    for n in BIG:
        shp = w[n].shape
        two_d = (shp[0] * shp[1], shp[2])
        dl, mn, vn = _adamw(f"adamw_{n}", w[n].reshape(two_d), grads[n].reshape(two_d), mom[n].reshape(two_d),
                            var[n].reshape(two_d))
        delta[n], new_m[n], new_v[n] = dl.reshape(shp), mn.reshape(shp), vn.reshape(shp)
    small_shapes = [w[n].shape for n in SMALL]
    pw, _ = _pack_rows([w[n] for n in SMALL])
    pg_, _ = _pack_rows([grads[n] for n in SMALL])
    pm, _ = _pack_rows([mom[n] for n in SMALL])
    pv, _ = _pack_rows([var[n] for n in SMALL])
    dl, mn, vn = _adamw("adamw_small", pw, pg_, pm, pv)
    for n, a, b, cc_ in zip(SMALL, _unpack(dl.reshape(-1), small_shapes), _unpack(mn.reshape(-1), small_shapes),
                            _unpack(vn.reshape(-1), small_shapes)):
        delta[n], new_m[n], new_v[n] = a, b, cc_

    return (loss, grad_x, *[grads[n] for n in WEIGHTS], *[delta[n] for n in WEIGHTS],
            *[new_m[n] for n in WEIGHTS], *[new_v[n] for n in WEIGHTS])
```

```python
import functools
import math

import jax
import jax.numpy as jnp
from jax import lax
from jax.experimental import pallas as pl
from jax.experimental.pallas import tpu as pltpu

F32 = jnp.float32
BF16 = jnp.bfloat16
MESH = pl.DeviceIdType.MESH

HEAD = 128
GROUP = 512
EPS = 1e-6
HALO = 32
CHUNK_ROWS = 128
POOL_LEVELS = 4
PACK_ROWS = 512

ADAM_LR = 0.001
ADAM_B1 = 0.9
ADAM_B2 = 0.999
ADAM_EPS = 1e-08
ADAM_WD = 0.01
ADAM_STEP = 10

VMEM_LIMIT = 56 * 1024 * 1024


def _cp(sem=None, vmem=VMEM_LIMIT):
    return pltpu.CompilerParams(dimension_semantics=sem, vmem_limit_bytes=vmem)


def _sigmoid(x):
    return 1.0 / (1.0 + jnp.exp(-x))


_GELU_K = math.sqrt(2.0 / math.pi)
_GELU_C = 0.044715


def _gelu(x):
    t = jnp.tanh(_GELU_K * (x + _GELU_C * x * x * x))
    return 0.5 * x * (1.0 + t)


def _gelu_grad(x):
    t = jnp.tanh(_GELU_K * (x + _GELU_C * x * x * x))
    return 0.5 * (1.0 + t) + 0.5 * x * (1.0 - t * t) * _GELU_K * (1.0 + 3.0 * _GELU_C * x * x)


def _mesh_pos():
    return lax.axis_index("x"), lax.axis_index("y"), lax.axis_index("c")


def _any_specs(n):
    return [pl.BlockSpec(memory_space=pl.ANY)] * n


def _allgather8(name, blocks):
    n = len(blocks)

    def body(*refs):
        ins, outs = refs[:n], refs[n:2 * n]
        send_sems, recv_sems, local_sems = refs[2 * n:]
        x, y, c = _mesh_pos()
        me, sibling = (x, y, c), (x, y, 1 - c)
        chips = [(1 - x, y), (x, 1 - y), (1 - x, 1 - y)]

        def slot(a, dev):
            return outs[a].at[4 * dev[0] + 2 * dev[1] + dev[2]]

        def copy(a, k, block, to, src=None):
            dst = slot(a, block)
            return pltpu.make_async_remote_copy(
                src_ref=dst if src is None else src, dst_ref=dst,
                send_sem=send_sems.at[7 * a + k], recv_sem=recv_sems.at[7 * a + k],
                device_id=to, device_id_type=MESH)

        mine, first, passed = [], [], []
        for a in range(n):
            cp = pltpu.make_async_copy(ins[a], slot(a, me), local_sems.at[a])
            cp.start()
            mine.append(cp)
            cps = [copy(a, 0, me, sibling, src=ins[a])]
            cps += [copy(a, 1 + j, me, (*chip, c), src=ins[a]) for j, chip in enumerate(chips)]
            for cp in cps:
                cp.start()
            first += cps
        for j, chip in enumerate(chips):
            for a in range(n):
                copy(a, 1 + j, (*chip, c), me).wait_recv()
                cp = copy(a, 4 + j, (*chip, c), sibling)
                cp.start()
                passed.append(cp)
        for a in range(n):
            copy(a, 0, sibling, me).wait_recv()
            for j, chip in enumerate(chips):
                copy(a, 4 + j, (*chip, 1 - c), me).wait_recv()
        for cp in first + passed:
            cp.wait_send()
        for cp in mine:
            cp.wait()

    return pl.pallas_call(
        body, name=name,
        out_shape=[jax.ShapeDtypeStruct((8,) + b.shape, b.dtype) for b in blocks],
        in_specs=_any_specs(n), out_specs=_any_specs(n),
        scratch_shapes=[pltpu.SemaphoreType.DMA((7 * n,)), pltpu.SemaphoreType.DMA((7 * n,)),
                        pltpu.SemaphoreType.DMA((n,))],
    )(*blocks)


HBM_SPEC = pl.BlockSpec(memory_space=pltpu.HBM)
SEM_SPEC = pl.BlockSpec(memory_space=pltpu.SEMAPHORE)
ANY_SPEC = pl.BlockSpec(memory_space=pl.ANY)
SPLIT_COPY = pltpu.CompilerParams(has_side_effects=pltpu.SideEffectType.DATAFLOW_SIDE_EFFECTING)


def _hbm(x):
    return pltpu.with_memory_space_constraint(x, pltpu.HBM)


def _other_chips(x, y):
    return [(1 - x, y), (x, 1 - y), (1 - x, 1 - y)]


def _half_shape(w, ax):
    _, r, cc = w.shape
    return (r // 2, cc) if ax == 0 else (r, cc // 2)


def _my_half(w_ref, layer, ax, shape2, c):
    w = w_ref.at[layer]
    if ax == 0:
        return w.at[pl.ds(c * shape2[0], shape2[0]), :]
    return w.at[:, pl.ds(c * shape2[1], shape2[1])]


def _dev_slot(ref, dev):
    return ref.at[4 * dev[0] + 2 * dev[1] + dev[2]]


def _gather_ici_copies(ins, lands, send_sems, recv_sems, layer, axes, shapes):
    x, y, c = _mesh_pos()
    pairs = []
    for a in range(len(ins)):
        half = _my_half(ins[a], layer, axes[a], shapes[a], c)
        for j, chip in enumerate(_other_chips(x, y)):
            def copy(dst_dev):
                return pltpu.make_async_remote_copy(
                    src_ref=half, dst_ref=_dev_slot(lands[a], dst_dev),
                    send_sem=send_sems.at[3 * a + j], recv_sem=recv_sems.at[3 * a + j],
                    device_id=(*chip, c), device_id_type=MESH)
            pairs.append((copy((x, y, c)), copy((*chip, c))))
    return pairs


def _exchange_ici_copies(ins, lands, send_sems, recv_sems):
    x, y, c = _mesh_pos()
    pairs = []
    for a in range(len(ins)):
        for k, chip in enumerate(_other_chips(x, y)):
            there = 2 * chip[0] + chip[1]
            def copy(dst_entry):
                return pltpu.make_async_remote_copy(
                    src_ref=ins[a].at[there], dst_ref=lands[a].at[dst_entry],
                    send_sem=send_sems.at[3 * a + k], recv_sem=recv_sems.at[3 * a + k],
                    device_id=(*chip, c), device_id_type=MESH)
            pairs.append((copy(2 * x + y), copy(there)))
    return pairs


def _gather_start(name, stacked, layer, axes, dep):
    n = len(stacked)
    shapes = [_half_shape(w, ax) for w, ax in zip(stacked, axes)]

    def body(*refs):
        ins, lands = refs[:n], refs[n:2 * n]
        send_sems, recv_sems = refs[2 * n + 1], refs[2 * n + 2]
        token = refs[-1]
        for send, _ in _gather_ici_copies(ins, lands, send_sems, recv_sems, layer, axes, shapes):
            send.start()
        token[...] = jnp.zeros_like(token)

    lands = [_hbm(lax.empty((8,) + shp, w.dtype)) for shp, w in zip(shapes, stacked)]
    out = pl.pallas_call(
        body, name=name,
        out_shape=(pltpu.SemaphoreType.DMA((3 * n,)), pltpu.SemaphoreType.DMA((3 * n,)),
                   *[pltpu.HBM((8,) + shp, w.dtype) for shp, w in zip(shapes, stacked)],
                   jax.ShapeDtypeStruct((8, 128), F32)),
        in_specs=[HBM_SPEC] * (2 * n) + [ANY_SPEC],
        out_specs=(SEM_SPEC, SEM_SPEC, *[HBM_SPEC] * n, pl.BlockSpec(memory_space=pltpu.VMEM)),
        input_output_aliases={n + a: 2 + a for a in range(n)},
        compiler_params=SPLIT_COPY,
    )(*[_hbm(w) for w in stacked], *lands, dep)
    return out[0], out[1], list(out[2:2 + n]), out[-1]


def _gather_wait(name, stacked, layer, axes, send_sems, recv_sems, lands, after):
    n = len(stacked)
    shapes = [_half_shape(w, ax) for w, ax in zip(stacked, axes)]

    def body(*refs):
        ins, lands_in = refs[:n], refs[n:2 * n]
        for send, recv in _gather_ici_copies(ins, lands_in, refs[2 * n], refs[2 * n + 1], layer, axes, shapes):
            send.wait_send()
            recv.wait_recv()

    out = pl.pallas_call(
        body, name=name,
        out_shape=[pltpu.HBM(l.shape, l.dtype) for l in lands],
        in_specs=[HBM_SPEC] * (2 * n) + [SEM_SPEC, SEM_SPEC, ANY_SPEC],
        out_specs=[HBM_SPEC] * n,
        input_output_aliases={n + a: a for a in range(n)},
        compiler_params=SPLIT_COPY,
    )(*[_hbm(w) for w in stacked], *lands, send_sems, recv_sems, after)
    return list(out)


def _gather_share(name, stacked, layer, axes, lands):
    n = len(stacked)
    shapes = [_half_shape(w, ax) for w, ax in zip(stacked, axes)]

    def body(*refs):
        ins, outs = refs[:n], refs[2 * n:3 * n]
        send_sems, recv_sems, local_sems = refs[3 * n:]
        x, y, c = _mesh_pos()
        me, sibling = (x, y, c), (x, y, 1 - c)
        chips = _other_chips(x, y)

        def copy(a, k, block, src=None):
            dst = _dev_slot(outs[a], block)
            return pltpu.make_async_remote_copy(
                src_ref=dst if src is None else src, dst_ref=dst,
                send_sem=send_sems.at[4 * a + k], recv_sem=recv_sems.at[4 * a + k],
                device_id=sibling, device_id_type=MESH)

        cps, mine = [], []
        for a in range(n):
            half = _my_half(ins[a], layer, axes[a], shapes[a], c)
            loc = pltpu.make_async_copy(half, _dev_slot(outs[a], me), local_sems.at[a])
            loc.start()
            mine.append(loc)
            cps.append(copy(a, 0, me, src=half))
            cps += [copy(a, 1 + j, (*chip, c)) for j, chip in enumerate(chips)]
        for cp in cps:
            cp.start()
        for a in range(n):
            copy(a, 0, sibling).wait_recv()
            for j, chip in enumerate(chips):
                copy(a, 1 + j, (*chip, 1 - c)).wait_recv()
        for cp in cps:
            cp.wait_send()
        for loc in mine:
            loc.wait()

    return pl.pallas_call(
        body, name=name,
        out_shape=[jax.ShapeDtypeStruct(l.shape, l.dtype) for l in lands],
        in_specs=_any_specs(2 * n), out_specs=_any_specs(n),
        input_output_aliases={n + a: a for a in range(n)},
        scratch_shapes=[pltpu.SemaphoreType.DMA((4 * n,)), pltpu.SemaphoreType.DMA((4 * n,)),
                        pltpu.SemaphoreType.DMA((n,))],
    )(*stacked, *lands)


def _sibling_send_other_halves(name, grads):
    n = len(grads)

    def body(*refs):
        ins, outs = refs[:n], refs[n:2 * n]
        send_sems, recv_sems = refs[2 * n:]
        x, y, c = _mesh_pos()
        cps = []
        for a in range(n):
            for j in range(4):
                cp = pltpu.make_async_remote_copy(
                    src_ref=ins[a].at[j, 1 - c], dst_ref=outs[a].at[j],
                    send_sem=send_sems.at[4 * a + j], recv_sem=recv_sems.at[4 * a + j],
                    device_id=(x, y, 1 - c), device_id_type=MESH)
                cp.start()
                cps.append(cp)
        for cp in cps:
            cp.wait()

    return pl.pallas_call(
        body, name=name,
        out_shape=[jax.ShapeDtypeStruct((4,) + g.shape[2:], g.dtype) for g in grads],
        in_specs=_any_specs(n), out_specs=_any_specs(n),
        scratch_shapes=[pltpu.SemaphoreType.DMA((4 * n,)), pltpu.SemaphoreType.DMA((4 * n,))],
    )(*grads)


def _chip_exchange_start(name, parts):
    n = len(parts)

    def body(*refs):
        ins, lands = refs[:n], refs[n:2 * n]
        token = refs[-1]
        for send, _ in _exchange_ici_copies(ins, lands, refs[2 * n], refs[2 * n + 1]):
            send.start()
        token[...] = jnp.zeros_like(token)

    lands = [_hbm(lax.empty(p.shape, p.dtype)) for p in parts]
    out = pl.pallas_call(
        body, name=name,
        out_shape=(pltpu.SemaphoreType.DMA((3 * n,)), pltpu.SemaphoreType.DMA((3 * n,)),
                   *[pltpu.HBM(p.shape, p.dtype) for p in parts], *[pltpu.HBM(p.shape, p.dtype) for p in parts],
                   jax.ShapeDtypeStruct((8, 128), F32)),
        in_specs=[HBM_SPEC] * (2 * n),
        out_specs=(SEM_SPEC, SEM_SPEC, *[HBM_SPEC] * (2 * n), pl.BlockSpec(memory_space=pltpu.VMEM)),
        input_output_aliases={i: 2 + i for i in range(2 * n)},
        compiler_params=SPLIT_COPY,
    )(*[_hbm(p) for p in parts], *lands)
    return out[0], out[1], list(out[2:2 + n]), list(out[2 + n:2 + 2 * n]), out[-1]


def _chip_exchange_wait(name, send_sems, recv_sems, parts, lands, after):
    n = len(parts)

    def body(*refs):
        ins, lands_in = refs[:n], refs[n:2 * n]
        for send, recv in _exchange_ici_copies(ins, lands_in, refs[2 * n], refs[2 * n + 1]):
            send.wait_send()
            recv.wait_recv()

    out = pl.pallas_call(
        body, name=name,
        out_shape=[pltpu.HBM(p.shape, p.dtype) for p in parts] * 2,
        in_specs=[HBM_SPEC] * (2 * n) + [SEM_SPEC, SEM_SPEC, ANY_SPEC],
        out_specs=[HBM_SPEC] * (2 * n),
        input_output_aliases={i: i for i in range(2 * n)},
        compiler_params=SPLIT_COPY,
    )(*parts, *lands, send_sems, recv_sems, after)
    return list(out[:n]), list(out[n:])


def _sibling_join_halves(name, bufs):
    n = len(bufs)
    depth = bufs[0].shape[0]

    def body(*refs):
        outs = refs[n:2 * n]
        send_sems, recv_sems = refs[2 * n:]
        x, y, c = _mesh_pos()

        def copy(a, l, half):
            return pltpu.make_async_remote_copy(
                src_ref=outs[a].at[l, half], dst_ref=outs[a].at[l, half],
                send_sem=send_sems.at[depth * a + l], recv_sem=recv_sems.at[depth * a + l],
                device_id=(x, y, 1 - c), device_id_type=MESH)

        for a in range(n):
            for l in range(depth):
                copy(a, l, c).start()
        for a in range(n):
            for l in range(depth):
                copy(a, l, c).wait_send()
                copy(a, l, 1 - c).wait_recv()

    return pl.pallas_call(
        body, name=name, out_shape=[jax.ShapeDtypeStruct(b.shape, b.dtype) for b in bufs],
        in_specs=_any_specs(n), out_specs=_any_specs(n),
        input_output_aliases={a: a for a in range(n)},
        scratch_shapes=[pltpu.SemaphoreType.DMA((depth * n,)), pltpu.SemaphoreType.DMA((depth * n,))],
    )(*bufs)


def _row_tile(rows, want):
    t = 1
    while t * 2 <= min(rows, want):
        t *= 2
    while rows % t:
        t //= 2
    return t


def _cast_bf16(name, w):
    l, r, c = w.shape
    tr = _row_tile(r, 512 if c <= 1536 else 256)

    def body(w_ref, o_ref):
        o_ref[...] = w_ref[...].astype(BF16)

    spec = pl.BlockSpec((None, tr, c), lambda i, j: (i, j, 0))
    return pl.pallas_call(
        body, name=name, grid=(l, r // tr), in_specs=[spec], out_specs=spec,
        out_shape=jax.ShapeDtypeStruct(w.shape, BF16), compiler_params=_cp(("parallel", "parallel")),
    )(w)


def _add_halves(name, grads, recvd, c_idx):
    outs = []
    for a, (g, r) in enumerate(zip(grads, recvd)):
        _, _, r2, cc = g.shape
        tr = _row_tile(r2, 512)

        def body(c_ref, g_ref, r_ref, o_ref):
            o_ref[...] = (g_ref[...].astype(F32) + r_ref[...].astype(F32)).astype(BF16)

        outs.append(pl.pallas_call(
            body, name=f"{name}_{a}",
            grid_spec=pltpu.PrefetchScalarGridSpec(
                num_scalar_prefetch=1, grid=(4, r2 // tr),
                in_specs=[pl.BlockSpec((None, None, tr, cc), lambda j, i, c_ref: (j, c_ref[0], i, 0)),
                          pl.BlockSpec((None, tr, cc), lambda j, i, c_ref: (j, i, 0))],
                out_specs=pl.BlockSpec((None, tr, cc), lambda j, i, c_ref: (j, i, 0))),
            out_shape=jax.ShapeDtypeStruct(r.shape, BF16), compiler_params=_cp(("parallel", "parallel")),
        )(c_idx, g, r))
    return outs


def _sum_parts(name, parts, recvd, bufs, layer, where):
    outs = []
    for a, (p, r, buf) in enumerate(zip(parts, recvd, bufs)):
        _, r2, cc = p.shape
        tr = _row_tile(r2, 256)

        def body(where_ref, own_ref, r1_ref, r2_ref, r3_ref, buf_ref, o_ref):
            acc = own_ref[...].astype(F32)
            for ref in (r1_ref, r2_ref, r3_ref):
                acc = acc + ref[...].astype(F32)
            o_ref[...] = acc

        def entry(k):
            return pl.BlockSpec((None, tr, cc), lambda i, where_ref: (where_ref[k], i, 0))

        outs.append(pl.pallas_call(
            body, name=f"{name}_{a}",
            grid_spec=pltpu.PrefetchScalarGridSpec(
                num_scalar_prefetch=1, grid=(r2 // tr,),
                in_specs=[entry(0), entry(1), entry(2), entry(3), ANY_SPEC],
                out_specs=pl.BlockSpec((None, None, tr, cc), lambda i, where_ref: (layer, where_ref[4], i, 0))),
            out_shape=jax.ShapeDtypeStruct(buf.shape, F32), input_output_aliases={5: 0},
            compiler_params=_cp(("parallel",)),
        )(where, p, r, r, r, buf))
    return outs


def _sum8(name, gathered):
    _, r, cc = gathered.shape
    tr = _row_tile(r, 512)

    def body(g_ref, o_ref):
        acc = g_ref[0]
        for i in range(1, 8):
            acc = acc + g_ref[i]
        o_ref[...] = acc

    return pl.pallas_call(
        body, name=name, grid=(r // tr,),
        in_specs=[pl.BlockSpec((8, tr, cc), lambda i: (0, i, 0))],
        out_specs=pl.BlockSpec((tr, cc), lambda i: (i, 0)),
        out_shape=jax.ShapeDtypeStruct((r, cc), F32), compiler_params=_cp(("parallel",)),
    )(gathered)


def _adamw(name, w, g, m, v):
    r, cc = w.shape
    tr = _row_tile(r, max(8, (256 * 1024) // cc))

    def body(w_ref, g_ref, m_ref, v_ref, d_ref, mo_ref, vo_ref):
        gg = g_ref[...]
        mn = ADAM_B1 * m_ref[...] + (1.0 - ADAM_B1) * gg
        vn = ADAM_B2 * v_ref[...] + (1.0 - ADAM_B2) * (gg * gg)
        m_hat = mn / (1.0 - ADAM_B1 ** ADAM_STEP)
        v_hat = vn / (1.0 - ADAM_B2 ** ADAM_STEP)
        d_ref[...] = -ADAM_LR * (m_hat / (jnp.sqrt(v_hat) + ADAM_EPS) + ADAM_WD * w_ref[...])
        mo_ref[...] = mn
        vo_ref[...] = vn

    spec = pl.BlockSpec((tr, cc), lambda i: (i, 0))
    return pl.pallas_call(
        body, name=name, grid=(r // tr,), in_specs=[spec] * 4, out_specs=[spec] * 3,
        out_shape=[jax.ShapeDtypeStruct(w.shape, F32)] * 3, compiler_params=_cp(("parallel",)),
    )(w, g, m, v)


def _rms_fwd(name, h, g, after=()):
    s, d = h.shape
    tr = _row_tile(s, 512)

    def body(h_ref, g_ref, *rest):
        y_ref = rest[-1]
        xv = h_ref[...]
        r = lax.rsqrt(jnp.mean(xv * xv, axis=-1, keepdims=True) + EPS)
        y_ref[...] = (xv * r * g_ref[...]).astype(BF16)

    return pl.pallas_call(
        body, name=name, grid=(s // tr,),
        in_specs=[pl.BlockSpec((tr, d), lambda i: (i, 0)), pl.BlockSpec((1, d), lambda i: (0, 0))]
        + [ANY_SPEC] * len(after),
        out_specs=pl.BlockSpec((tr, d), lambda i: (i, 0)),
        out_shape=jax.ShapeDtypeStruct((s, d), BF16), compiler_params=_cp(("parallel",)),
    )(h, g, *after)


def _rms_bwd_rows(xv, gv, dy):
    d = xv.shape[-1]
    r = lax.rsqrt(jnp.mean(xv * xv, axis=-1, keepdims=True) + EPS)
    dxn = dy * gv
    proj = jnp.sum(dxn * xv, axis=-1, keepdims=True) * (1.0 / d)
    dx = r * dxn - xv * (r * r * r) * proj
    return dx, dy * xv * r


def _rms_bwd(name, h, g, dy, dres):
    s, d = h.shape
    tr = _row_tile(s, 256)

    def body(h_ref, g_ref, dy_ref, dres_ref, dh_ref, dhb_ref, dg_ref):
        dx, dgp = _rms_bwd_rows(h_ref[...], g_ref[...], dy_ref[...].astype(F32))
        dh = dres_ref[...] + dx
        dh_ref[...] = dh
        dhb_ref[...] = dh.astype(BF16)

        @pl.when(pl.program_id(0) == 0)
        def _():
            dg_ref[...] = jnp.zeros_like(dg_ref)

        dg_ref[...] += jnp.sum(dgp, axis=0, keepdims=True)

    row = pl.BlockSpec((tr, d), lambda i: (i, 0))
    vec = pl.BlockSpec((1, d), lambda i: (0, 0))
    return pl.pallas_call(
        body, name=name, grid=(s // tr,), in_specs=[row, vec, row, row], out_specs=[row, row, vec],
        out_shape=[jax.ShapeDtypeStruct((s, d), F32), jax.ShapeDtypeStruct((s, d), BF16),
                   jax.ShapeDtypeStruct((1, d), F32)],
        compiler_params=_cp(("arbitrary",)),
    )(h, g, dy, dres)


def _loss_head(name, h, g, target):
    s, d = h.shape
    tr = _row_tile(s, 256)

    def body(h_ref, g_ref, t_ref, loss_ref, dh_ref, dhb_ref, dg_ref):
        xv = h_ref[...]
        gv = g_ref[...]
        r = lax.rsqrt(jnp.mean(xv * xv, axis=-1, keepdims=True) + EPS)
        err = xv * r * gv - t_ref[...]
        part = 0.5 * jnp.sum(jnp.sum(err * err, axis=-1, keepdims=True) * (1.0 / d), axis=0, keepdims=True)
        dx, dgp = _rms_bwd_rows(xv, gv, err * (1.0 / d))
        dh_ref[...] = dx
        dhb_ref[...] = dx.astype(BF16)

        @pl.when(pl.program_id(0) == 0)
        def _():
            dg_ref[...] = jnp.zeros_like(dg_ref)
            loss_ref[...] = jnp.zeros_like(loss_ref)

        dg_ref[...] += jnp.sum(dgp, axis=0, keepdims=True)
        loss_ref[...] += part

    row = pl.BlockSpec((tr, d), lambda i: (i, 0))
    vec = pl.BlockSpec((1, d), lambda i: (0, 0))
    one = pl.BlockSpec((1, 1), lambda i: (0, 0))
    return pl.pallas_call(
        body, name=name, grid=(s // tr,), in_specs=[row, vec, row], out_specs=[one, row, row, vec],
        out_shape=[jax.ShapeDtypeStruct((1, 1), F32), jax.ShapeDtypeStruct((s, d), F32),
                   jax.ShapeDtypeStruct((s, d), BF16), jax.ShapeDtypeStruct((1, d), F32)],
        compiler_params=_cp(("arbitrary",)),
    )(h, g, target)


NN = (((1,), (0,)), ((), ()))
NT = (((1,), (1,)), ((), ()))
TN = (((0,), (0,)), ((), ()))


def _matmul(name, grid, operands, in_specs, pairs, n_acc, acc_shape, epilogue, out_specs, out_shape):
    n_in = len(operands)
    n_out = len(out_shape)
    nk = grid[2]

    def body(*refs):
        ins, outs, accs = refs[:n_in], refs[n_in:n_in + n_out], refs[n_in + n_out:]
        k = pl.program_id(2)

        @pl.when(k == 0)
        def _():
            for acc in accs:
                acc[...] = jnp.zeros_like(acc)

        for li, ri, ai, dims in pairs:
            accs[ai][...] += lax.dot_general(ins[li][...], ins[ri][...], dims, preferred_element_type=F32)

        @pl.when(k == nk - 1)
        def _():
            epilogue([acc[...] for acc in accs], ins, outs)

    return pl.pallas_call(
        body, name=name, grid=grid, in_specs=in_specs, out_specs=out_specs, out_shape=out_shape,
        scratch_shapes=[pltpu.VMEM(acc_shape, F32)] * n_acc,
        compiler_params=_cp(("parallel", "parallel", "arbitrary")),
    )(*operands)


def _ep_store(dtype):
    def ep(accs, ins, outs):
        outs[0][...] = accs[0].astype(dtype)
    return ep


def _w_spec_nn(wg, kind, tk, tn):
    _, a, b = wg.shape
    if kind == "col":
        kph, bps = a // tk, b // tn
        return pl.BlockSpec((None, tk, tn), lambda i, j, k: (2 * (j // bps) + k // kph, k % kph, j % bps))
    kps, nph = a // tk, b // tn
    return pl.BlockSpec((None, tk, tn), lambda i, j, k: (2 * (k // kps) + j // nph, k % kps, j % nph))


def _w_spec_nt(wg, kind, to, tc):
    _, a, b = wg.shape
    if kind == "col":
        oph, cps = a // to, b // tc
        return pl.BlockSpec((None, to, tc), lambda i, jo, kc: (2 * (kc // cps) + jo // oph, jo % oph, kc % cps))
    ops, cph = a // to, b // tc
    return pl.BlockSpec((None, to, tc), lambda i, jo, kc: (2 * (jo // ops) + kc // cph, jo % ops, kc % cph))


def _mm_nn(name, x, wg, kind, tiles, epilogue=None, extra=(), extra_specs=(), out_specs=None, out_shape=None):
    m, kdim = x.shape
    _, a, b = wg.shape
    n = 4 * b if kind == "col" else 2 * b
    tm, tn, tk = tiles
    tm = min(tm, m)
    grid = (m // tm, n // tn, kdim // tk)
    in_specs = [pl.BlockSpec((tm, tk), lambda i, j, k: (i, k)), _w_spec_nn(wg, kind, tk, tn)] + list(extra_specs)
    if out_shape is None:
        out_shape = [jax.ShapeDtypeStruct((m, n), F32)]
        out_specs = [pl.BlockSpec((tm, tn), lambda i, j, k: (i, j))]
        epilogue = _ep_store(F32)
    return _matmul(name, grid, [x, wg] + list(extra), in_specs, [(0, 1, 0, NN)], 1, (tm, tn), epilogue,
                   out_specs, out_shape)


def _mm_nt(name, dy, wg, kind, tiles, out_dtype=BF16):
    m, n = dy.shape
    _, a, b = wg.shape
    kdim = 2 * a if kind == "col" else 4 * a
    tm, to, tc = tiles
    tm = min(tm, m)
    grid = (m // tm, kdim // to, n // tc)
    in_specs = [pl.BlockSpec((tm, tc), lambda i, j, k: (i, k)), _w_spec_nt(wg, kind, to, tc)]
    return _matmul(name, grid, [dy, wg], in_specs, [(0, 1, 0, NT)], 1, (tm, to), _ep_store(out_dtype),
                   [pl.BlockSpec((tm, to), lambda i, j, k: (i, j))], [jax.ShapeDtypeStruct((m, kdim), out_dtype)])[0]


def _mm_tn(name, a, dy, kind, tiles):
    m, kdim = a.shape
    _, n = dy.shape
    tr, tn, tmm = tiles
    tmm = min(tmm, m)
    grid = (kdim // tr, n // tn, m // tmm)
    in_specs = [pl.BlockSpec((tmm, tr), lambda r, j, k: (k, r)), pl.BlockSpec((tmm, tn), lambda r, j, k: (k, j))]
    if kind == "col":
        ns = n // 4
        bps = ns // tn
        out_shape = [jax.ShapeDtypeStruct((4, kdim, ns), BF16)]
        out_specs = [pl.BlockSpec((None, tr, tn), lambda r, j, k: (j // bps, r, j % bps))]
    else:
        rs = kdim // 4
        rps = rs // tr
        out_shape = [jax.ShapeDtypeStruct((4, rs, n), BF16)]
        out_specs = [pl.BlockSpec((None, tr, tn), lambda r, j, k: (r // rps, r % rps, j))]
    return _matmul(name, grid, [a, dy], in_specs, [(0, 1, 0, TN)], 1, (tr, tn), _ep_store(BF16),
                   out_specs, out_shape)[0]


def _zero_halo(pad_ref, s):
    z = jnp.zeros((HALO, pad_ref.shape[1]), F32)
    pad_ref[pl.ds(0, HALO), :] = z
    pad_ref[pl.ds(HALO + s, HALO), :] = z


def _window(pad_ref, r0, rows):
    return pad_ref[pl.ds(r0, rows + 2 * HALO), :]


def _delayed(win, k, rows):
    if k == 0:
        return win[HALO:HALO + rows]
    return pltpu.roll(win, k, axis=0)[HALO:HALO + rows]


def _advanced(win, k, rows):
    if k == 0:
        return win[HALO:HALO + rows]
    return pltpu.roll(win, win.shape[0] - k, axis=0)[HALO:HALO + rows]


def _fold8(x):
    return jnp.sum(x.reshape(x.shape[0] // 8, 8, x.shape[1]), axis=0)


def _chunks(s):
    rows = min(CHUNK_ROWS, s)
    return rows, s // rows


def _col_spec(s, first_block):
    return pl.BlockSpec((s, HEAD), lambda j: (0, first_block + j))


def _sgu_fwd(name, z, ln_g, ln_b, w, b):
    s = z.shape[0]
    tr = _row_tile(s, 1024)
    nh = GROUP // HEAD

    def body(u_ref, v_ref, lg_ref, lb_ref, w_ref, b_ref, o_ref):
        row = lax.broadcasted_iota(jnp.int32, (HEAD, HEAD), 0)
        col = lax.broadcasted_iota(jnp.int32, (HEAD, HEAD), 1)
        wm = jnp.where(row >= col, w_ref[...], 0.0).astype(BF16)
        for ck in range(tr // HEAD):
            rs = pl.ds(ck * HEAD, HEAD)
            u = _gelu(u_ref[rs, :])
            v = _gelu(v_ref[rs, :])
            mu = jnp.mean(v, axis=-1, keepdims=True)
            vc = v - mu
            var = jnp.mean(vc * vc, axis=-1, keepdims=True)
            vln = vc * lax.rsqrt(var + EPS) * lg_ref[...] + lb_ref[...]
            sp = jnp.dot(wm, vln.astype(BF16), preferred_element_type=F32) + b_ref[...]
            o_ref[rs, :] = (u * sp).astype(BF16)

    head_vec = pl.BlockSpec((None, 1, HEAD), lambda h, i: (h, 0, 0))
    return pl.pallas_call(
        body, name=name, grid=(nh, s // tr),
        in_specs=[pl.BlockSpec((tr, HEAD), lambda h, i: (i, h)), pl.BlockSpec((tr, HEAD), lambda h, i: (i, nh + h)),
                  head_vec, head_vec, pl.BlockSpec((None, HEAD, HEAD), lambda h, i: (h, 0, 0)),
                  pl.BlockSpec((None, HEAD, 1), lambda h, i: (h, 0, 0))],
        out_specs=pl.BlockSpec((tr, HEAD), lambda h, i: (i, h)),
        out_shape=jax.ShapeDtypeStruct((s, GROUP), BF16), compiler_params=_cp(("parallel", "parallel")),
    )(z, z, ln_g, ln_b, w, b)


def _sgu_bwd(name, z, d_o, ln_g, ln_b, w, b):
    s = z.shape[0]
    tr = _row_tile(s, 1024)
    nh = GROUP // HEAD

    def body(u_ref, v_ref, do_ref, lg_ref, lb_ref, w_ref, b_ref, du_ref, dv_ref, dlg_ref, dlb_ref, dw_ref, db_ref,
             dsp_acc):
        row = lax.broadcasted_iota(jnp.int32, (HEAD, HEAD), 0)
        col = lax.broadcasted_iota(jnp.int32, (HEAD, HEAD), 1)
        tril = row >= col
        wm = jnp.where(tril, w_ref[...], 0.0).astype(BF16)
        i = pl.program_id(1)

        @pl.when(i == 0)
        def _():
            dlg_ref[...] = jnp.zeros_like(dlg_ref)
            dlb_ref[...] = jnp.zeros_like(dlb_ref)
            dw_ref[...] = jnp.zeros_like(dw_ref)
            dsp_acc[...] = jnp.zeros_like(dsp_acc)

        dlg = jnp.zeros((1, HEAD), F32)
        dlb = jnp.zeros((1, HEAD), F32)
        dw = jnp.zeros((HEAD, HEAD), F32)
        dsp_sum = jnp.zeros((HEAD, HEAD), F32)
        for ck in range(tr // HEAD):
            rs = pl.ds(ck * HEAD, HEAD)
            zu = u_ref[rs, :]
            zv = v_ref[rs, :]
            u = _gelu(zu)
            v = _gelu(zv)
            mu = jnp.mean(v, axis=-1, keepdims=True)
            vc = v - mu
            var = jnp.mean(vc * vc, axis=-1, keepdims=True)
            rstd = lax.rsqrt(var + EPS)
            xh = vc * rstd
            vln = (xh * lg_ref[...] + lb_ref[...]).astype(BF16)
            sp = jnp.dot(wm, vln, preferred_element_type=F32) + b_ref[...]
            d_oa = do_ref[rs, :].astype(F32)
            du = d_oa * sp
            dsp = d_oa * u
            dsp_b = dsp.astype(BF16)
            dvln = lax.dot_general(wm, dsp_b, TN, preferred_element_type=F32)
            dw = dw + lax.dot_general(dsp_b, vln, NT, preferred_element_type=F32)
            dsp_sum = dsp_sum + dsp
            dlg = dlg + jnp.sum(dvln * xh, axis=0, keepdims=True)
            dlb = dlb + jnp.sum(dvln, axis=0, keepdims=True)
            dxh = dvln * lg_ref[...]
            dv = rstd * (dxh - jnp.mean(dxh, axis=-1, keepdims=True)
                         - xh * jnp.mean(dxh * xh, axis=-1, keepdims=True))
            du_ref[rs, :] = (du * _gelu_grad(zu)).astype(BF16)
            dv_ref[rs, :] = (dv * _gelu_grad(zv)).astype(BF16)
        dlg_ref[...] += dlg
        dlb_ref[...] += dlb
        dw_ref[...] += jnp.where(tril, dw, 0.0)
        dsp_acc[...] += dsp_sum

        @pl.when(i == pl.num_programs(1) - 1)
        def _():
            db_ref[...] = jnp.sum(dsp_acc[...], axis=1, keepdims=True)

    head_vec = pl.BlockSpec((None, 1, HEAD), lambda h, i: (h, 0, 0))
    head_mat = pl.BlockSpec((None, HEAD, HEAD), lambda h, i: (h, 0, 0))
    head_col = pl.BlockSpec((None, HEAD, 1), lambda h, i: (h, 0, 0))
    return pl.pallas_call(
        body, name=name, grid=(nh, s // tr),
        in_specs=[pl.BlockSpec((tr, HEAD), lambda h, i: (i, h)), pl.BlockSpec((tr, HEAD), lambda h, i: (i, nh + h)),
                  pl.BlockSpec((tr, HEAD), lambda h, i: (i, h)), head_vec, head_vec, head_mat, head_col],
        out_specs=[pl.BlockSpec((tr, HEAD), lambda h, i: (i, h)), pl.BlockSpec((tr, HEAD), lambda h, i: (i, h)),
                   head_vec, head_vec, head_mat, head_col],
        out_shape=[jax.ShapeDtypeStruct((s, GROUP), BF16), jax.ShapeDtypeStruct((s, GROUP), BF16),
                   jax.ShapeDtypeStruct((nh, 1, HEAD), F32), jax.ShapeDtypeStruct((nh, 1, HEAD), F32),
                   jax.ShapeDtypeStruct((nh, HEAD, HEAD), F32), jax.ShapeDtypeStruct((nh, HEAD, 1), F32)],
        scratch_shapes=[pltpu.VMEM((HEAD, HEAD), F32)],
        compiler_params=_cp(("parallel", "arbitrary")),
    )(z, z, d_o, ln_g, ln_b, w, b)


def _shortconv_fwd(name, z, w):
    s = z.shape[0]
    kw = w.shape[0]
    rows, nchunk = _chunks(s)
    nb = GROUP // HEAD

    def body(h_ref, bg_ref, cg_ref, w_ref, o_ref, pad):
        _zero_halo(pad, s)

        def fill(ci, carry):
            r0 = pl.multiple_of(ci * rows, rows)
            pad[pl.ds(pl.multiple_of(HALO + r0, 8), rows), :] = cg_ref[pl.ds(r0, rows), :] * h_ref[pl.ds(r0, rows), :]
            return carry

        lax.fori_loop(0, nchunk, fill, 0)

        def step(ci, carry):
            r0 = pl.multiple_of(ci * rows, rows)
            win = _window(pad, r0, rows)
            cv = jnp.zeros((rows, HEAD), F32)
            for k in range(kw):
                cv = cv + w_ref[k:k + 1, :] * _delayed(win, kw - 1 - k, rows)
            o_ref[pl.ds(r0, rows), :] = (bg_ref[pl.ds(r0, rows), :] * cv).astype(BF16)
            return carry

        lax.fori_loop(0, nchunk, step, 0)

    return pl.pallas_call(
        body, name=name, grid=(nb,),
        in_specs=[_col_spec(s, 8), _col_spec(s, 12), _col_spec(s, 16), pl.BlockSpec((kw, HEAD), lambda j: (0, j))],
        out_specs=_col_spec(s, 0),
        out_shape=jax.ShapeDtypeStruct((s, GROUP), BF16),
        scratch_shapes=[pltpu.VMEM((s + 2 * HALO, HEAD), F32)],
        compiler_params=_cp(("parallel",)),
    )(z, z, z, w)


def _shortconv_bwd(name, z, d_o, w):
    s = z.shape[0]
    kw = w.shape[0]
    rows, nchunk = _chunks(s)
    nb = GROUP // HEAD

    def body(h_ref, bg_ref, cg_ref, do_ref, w_ref, dh_ref, dbg_ref, dcg_ref, dw_ref, pad_q, pad_d, acc):
        _zero_halo(pad_q, s)
        _zero_halo(pad_d, s)
        acc[...] = jnp.zeros_like(acc)

        def fill(ci, carry):
            r0 = pl.multiple_of(ci * rows, rows)
            rs = pl.ds(r0, rows)
            ps = pl.ds(pl.multiple_of(HALO + r0, 8), rows)
            pad_q[ps, :] = cg_ref[rs, :] * h_ref[rs, :]
            pad_d[ps, :] = do_ref[rs, :].astype(F32) * bg_ref[rs, :]
            return carry

        lax.fori_loop(0, nchunk, fill, 0)

        def step(ci, carry):
            r0 = pl.multiple_of(ci * rows, rows)
            rs = pl.ds(r0, rows)
            wq = _window(pad_q, r0, rows)
            wd = _window(pad_d, r0, rows)
            dcv = wd[HALO:HALO + rows]
            cv = jnp.zeros((rows, HEAD), F32)
            dq = jnp.zeros((rows, HEAD), F32)
            for k in range(kw):
                qk = _delayed(wq, kw - 1 - k, rows)
                cv = cv + w_ref[k:k + 1, :] * qk
                dq = dq + w_ref[k:k + 1, :] * _advanced(wd, kw - 1 - k, rows)
                acc[k] += _fold8(dcv * qk)
            dbg_ref[rs, :] = (do_ref[rs, :].astype(F32) * cv).astype(BF16)
            dcg_ref[rs, :] = (dq * h_ref[rs, :]).astype(BF16)
            dh_ref[rs, :] = (dq * cg_ref[rs, :]).astype(BF16)
            return carry

        lax.fori_loop(0, nchunk, step, 0)
        for k in range(kw):
            dw_ref[k:k + 1, :] = jnp.sum(acc[k], axis=0, keepdims=True)

    col = _col_spec(s, 0)
    return pl.pallas_call(
        body, name=name, grid=(nb,),
        in_specs=[_col_spec(s, 8), _col_spec(s, 12), _col_spec(s, 16), col, pl.BlockSpec((kw, HEAD), lambda j: (0, j))],
        out_specs=[col, col, col, pl.BlockSpec((kw, HEAD), lambda j: (0, j))],
        out_shape=[jax.ShapeDtypeStruct((s, GROUP), BF16)] * 3 + [jax.ShapeDtypeStruct((kw, GROUP), F32)],
        scratch_shapes=[pltpu.VMEM((s + 2 * HALO, HEAD), F32), pltpu.VMEM((s + 2 * HALO, HEAD), F32),
                        pltpu.VMEM((kw, 8, HEAD), F32)],
        compiler_params=_cp(("parallel",)),
    )(z, z, z, d_o, w)


def _conformer_conv_fwd(name, z, w, bias):
    s = z.shape[0]
    kw = w.shape[0]
    rows, nchunk = _chunks(s)
    nb = GROUP // HEAD

    def body(a_ref, g_ref, w_ref, b_ref, o_ref, pad):
        _zero_halo(pad, s)

        def fill(ci, carry):
            r0 = pl.multiple_of(ci * rows, rows)
            rs = pl.ds(r0, rows)
            pad[pl.ds(pl.multiple_of(HALO + r0, 8), rows), :] = a_ref[rs, :] * _sigmoid(g_ref[rs, :])
            return carry

        lax.fori_loop(0, nchunk, fill, 0)

        def step(ci, carry):
            r0 = pl.multiple_of(ci * rows, rows)
            win = _window(pad, r0, rows)
            cc = jnp.zeros((rows, HEAD), F32)
            for k in range(kw):
                cc = cc + w_ref[k:k + 1, :] * _delayed(win, kw - 1 - k, rows)
            o_ref[pl.ds(r0, rows), :] = cc + b_ref[...]
            return carry

        lax.fori_loop(0, nchunk, step, 0)

    return pl.pallas_call(
        body, name=name, grid=(nb,),
        in_specs=[_col_spec(s, 20), _col_spec(s, 24), pl.BlockSpec((kw, HEAD), lambda j: (0, j)),
                  pl.BlockSpec((1, HEAD), lambda j: (0, j))],
        out_specs=_col_spec(s, 0),
        out_shape=jax.ShapeDtypeStruct((s, GROUP), F32),
        scratch_shapes=[pltpu.VMEM((s + 2 * HALO, HEAD), F32)],
        compiler_params=_cp(("parallel",)),
    )(z, z, w, bias)


def _ln_rows(cc, g, b):
    mu = jnp.mean(cc, axis=-1, keepdims=True)
    xc = cc - mu
    var = jnp.mean(xc * xc, axis=-1, keepdims=True)
    rstd = lax.rsqrt(var + EPS)
    xh = xc * rstd
    return xh, rstd, xh * g + b


def _conformer_ln_fwd(name, cc, g, b):
    s, d = cc.shape
    tr = _row_tile(s, 512)

    def body(c_ref, g_ref, b_ref, o_ref):
        _, _, l = _ln_rows(c_ref[...], g_ref[...], b_ref[...])
        o_ref[...] = (l * _sigmoid(l)).astype(BF16)

    row = pl.BlockSpec((tr, d), lambda i: (i, 0))
    vec = pl.BlockSpec((1, d), lambda i: (0, 0))
    return pl.pallas_call(
        body, name=name, grid=(s // tr,), in_specs=[row, vec, vec], out_specs=row,
        out_shape=jax.ShapeDtypeStruct((s, d), BF16), compiler_params=_cp(("parallel",)),
    )(cc, g, b)


def _conformer_ln_bwd(name, cc, d_o, g, b):
    s, d = cc.shape
    tr = _row_tile(s, 512)

    def body(c_ref, do_ref, g_ref, b_ref, dcc_ref, dg_ref, db_ref, dcb_ref):
        xh, rstd, l = _ln_rows(c_ref[...], g_ref[...], b_ref[...])
        sg = _sigmoid(l)
        dl = do_ref[...].astype(F32) * sg * (1.0 + l * (1.0 - sg))
        dxh = dl * g_ref[...]
        dcc = rstd * (dxh - jnp.mean(dxh, axis=-1, keepdims=True) - xh * jnp.mean(dxh * xh, axis=-1, keepdims=True))
        dcc_ref[...] = dcc

        @pl.when(pl.program_id(0) == 0)
        def _():
            dg_ref[...] = jnp.zeros_like(dg_ref)
            db_ref[...] = jnp.zeros_like(db_ref)
            dcb_ref[...] = jnp.zeros_like(dcb_ref)

        dg_ref[...] += jnp.sum(dl * xh, axis=0, keepdims=True)
        db_ref[...] += jnp.sum(dl, axis=0, keepdims=True)
        dcb_ref[...] += jnp.sum(dcc, axis=0, keepdims=True)

    row = pl.BlockSpec((tr, d), lambda i: (i, 0))
    vec = pl.BlockSpec((1, d), lambda i: (0, 0))
    return pl.pallas_call(
        body, name=name, grid=(s // tr,), in_specs=[row, row, vec, vec], out_specs=[row, vec, vec, vec],
        out_shape=[jax.ShapeDtypeStruct((s, d), F32)] + [jax.ShapeDtypeStruct((1, d), F32)] * 3,
        compiler_params=_cp(("arbitrary",)),
    )(cc, d_o, g, b)


def _conformer_conv_bwd(name, z, dcc, w):
    s = z.shape[0]
    kw = w.shape[0]
    rows, nchunk = _chunks(s)
    nb = GROUP // HEAD

    def body(a_ref, g_ref, d_ref, w_ref, da_ref, dg_ref, dw_ref, pad_h, pad_d, acc):
        _zero_halo(pad_h, s)
        _zero_halo(pad_d, s)
        acc[...] = jnp.zeros_like(acc)

        def fill(ci, carry):
            r0 = pl.multiple_of(ci * rows, rows)
            rs = pl.ds(r0, rows)
            ps = pl.ds(pl.multiple_of(HALO + r0, 8), rows)
            pad_h[ps, :] = a_ref[rs, :] * _sigmoid(g_ref[rs, :])
            pad_d[ps, :] = d_ref[rs, :]
            return carry

        lax.fori_loop(0, nchunk, fill, 0)

        def step(ci, carry):
            r0 = pl.multiple_of(ci * rows, rows)
            rs = pl.ds(r0, rows)
            wh = _window(pad_h, r0, rows)
            wd = _window(pad_d, r0, rows)
            dcc_c = wd[HALO:HALO + rows]
            dhc = jnp.zeros((rows, HEAD), F32)
            for k in range(kw):
                dhc = dhc + w_ref[k:k + 1, :] * _advanced(wd, kw - 1 - k, rows)
                acc[k] += _fold8(dcc_c * _delayed(wh, kw - 1 - k, rows))
            sg = _sigmoid(g_ref[rs, :])
            da_ref[rs, :] = (dhc * sg).astype(BF16)
            dg_ref[rs, :] = (dhc * a_ref[rs, :] * sg * (1.0 - sg)).astype(BF16)
            return carry

        lax.fori_loop(0, nchunk, step, 0)
        for k in range(kw):
            dw_ref[k:k + 1, :] = jnp.sum(acc[k], axis=0, keepdims=True)

    col = _col_spec(s, 0)
    return pl.pallas_call(
        body, name=name, grid=(nb,),
        in_specs=[_col_spec(s, 20), _col_spec(s, 24), col, pl.BlockSpec((kw, HEAD), lambda j: (0, j))],
        out_specs=[col, col, pl.BlockSpec((kw, HEAD), lambda j: (0, j))],
        out_shape=[jax.ShapeDtypeStruct((s, GROUP), BF16)] * 2 + [jax.ShapeDtypeStruct((kw, GROUP), F32)],
        scratch_shapes=[pltpu.VMEM((s + 2 * HALO, HEAD), F32), pltpu.VMEM((s + 2 * HALO, HEAD), F32),
                        pltpu.VMEM((kw, 8, HEAD), F32)],
        compiler_params=_cp(("parallel",)),
    )(z, z, dcc, w)


def _pool_window_sum(win, level, rows, shift):
    n = win.shape[0]

    def moved(v, k):
        return pltpu.roll(v, k if shift is _delayed else n - k, axis=0)

    s2 = win + moved(win, 1)
    s4 = s2 + moved(s2, 2)
    s8 = s4 + moved(s4, 4)
    s16 = s8 + moved(s8, 8)
    sel = jnp.where(level == 0, s2, jnp.where(level == 1, s4, jnp.where(level == 2, s8, s16)))
    return sel[HALO:HALO + rows]


def _pool_count(level, r0, rows):
    t = r0 + lax.broadcasted_iota(jnp.int32, (rows, 1), 0)
    width = jnp.left_shift(jnp.int32(2), level)
    return jnp.minimum(t + 1, width).astype(F32)


def _pool_fwd(name, z, pool_w, scale):
    s = z.shape[0]
    rows, nchunk = _chunks(s)

    def body(z_ref, w_ref, sc_ref, o_ref, pad):
        level = pl.program_id(0)
        _zero_halo(pad, s)

        def fill(ci, carry):
            r0 = pl.multiple_of(ci * rows, rows)
            pad[pl.ds(pl.multiple_of(HALO + r0, 8), rows), :] = z_ref[pl.ds(r0, rows), :]
            return carry

        lax.fori_loop(0, nchunk, fill, 0)
        wb = w_ref[...].astype(BF16)

        def step(ci, carry):
            r0 = pl.multiple_of(ci * rows, rows)
            win = _window(pad, r0, rows)
            pm = _pool_window_sum(win, level, rows, _delayed) / _pool_count(level, r0, rows) - win[HALO:HALO + rows]
            r = jnp.dot(pm.astype(BF16), wb, preferred_element_type=F32)
            o_ref[pl.ds(r0, rows), :] = (r * sc_ref[...]).astype(BF16)
            return carry

        lax.fori_loop(0, nchunk, step, 0)

    return pl.pallas_call(
        body, name=name, grid=(POOL_LEVELS,),
        in_specs=[_col_spec(s, 28), pl.BlockSpec((None, HEAD, HEAD), lambda j: (j, 0, 0)),
                  pl.BlockSpec((1, HEAD), lambda j: (0, j))],
        out_specs=_col_spec(s, 0),
        out_shape=jax.ShapeDtypeStruct((s, GROUP), BF16),
        scratch_shapes=[pltpu.VMEM((s + 2 * HALO, HEAD), F32)],
        compiler_params=_cp(("parallel",)),
    )(z, pool_w, scale)


def _pool_bwd(name, z, d_o, pool_w, scale):
    s = z.shape[0]
    rows, nchunk = _chunks(s)

    def body(z_ref, do_ref, w_ref, sc_ref, dz_ref, dw_ref, dsc_ref, pad, pad_q, dw_acc, dsc_acc):
        level = pl.program_id(0)
        _zero_halo(pad, s)
        _zero_halo(pad_q, s)
        dw_acc[...] = jnp.zeros_like(dw_acc)
        dsc_acc[...] = jnp.zeros_like(dsc_acc)

        def fill(ci, carry):
            r0 = pl.multiple_of(ci * rows, rows)
            pad[pl.ds(pl.multiple_of(HALO + r0, 8), rows), :] = z_ref[pl.ds(r0, rows), :]
            return carry

        lax.fori_loop(0, nchunk, fill, 0)
        wb = w_ref[...].astype(BF16)

        def first(ci, carry):
            r0 = pl.multiple_of(ci * rows, rows)
            win = _window(pad, r0, rows)
            cnt = _pool_count(level, r0, rows)
            pm = (_pool_window_sum(win, level, rows, _delayed) / cnt - win[HALO:HALO + rows]).astype(BF16)
            r = jnp.dot(pm, wb, preferred_element_type=F32)
            d_od = do_ref[pl.ds(r0, rows), :].astype(F32)
            dsc_acc[...] += _fold8(d_od * r)
            dr = (d_od * sc_ref[...]).astype(BF16)
            dw_acc[...] += lax.dot_general(pm, dr, TN, preferred_element_type=F32)
            dpm = lax.dot_general(dr, wb, NT, preferred_element_type=F32)
            pad_q[pl.ds(pl.multiple_of(HALO + r0, 8), rows), :] = dpm / cnt
            return carry

        lax.fori_loop(0, nchunk, first, 0)

        def second(ci, carry):
            r0 = pl.multiple_of(ci * rows, rows)
            wq = _window(pad_q, r0, rows)
            dpm = wq[HALO:HALO + rows] * _pool_count(level, r0, rows)
            dz_ref[pl.ds(r0, rows), :] = (_pool_window_sum(wq, level, rows, _advanced) - dpm).astype(BF16)
            return carry

        lax.fori_loop(0, nchunk, second, 0)
        dw_ref[...] = dw_acc[...]
        dsc_ref[...] = jnp.sum(dsc_acc[...], axis=0, keepdims=True)

    col = _col_spec(s, 0)
    mat = pl.BlockSpec((None, HEAD, HEAD), lambda j: (j, 0, 0))
    vec = pl.BlockSpec((1, HEAD), lambda j: (0, j))
    return pl.pallas_call(
        body, name=name, grid=(POOL_LEVELS,),
        in_specs=[_col_spec(s, 28), col, mat, vec], out_specs=[col, mat, vec],
        out_shape=[jax.ShapeDtypeStruct((s, GROUP), BF16), jax.ShapeDtypeStruct((POOL_LEVELS, HEAD, HEAD), F32),
                   jax.ShapeDtypeStruct((1, GROUP), F32)],
        scratch_shapes=[pltpu.VMEM((s + 2 * HALO, HEAD), F32), pltpu.VMEM((s + 2 * HALO, HEAD), F32),
                        pltpu.VMEM((HEAD, HEAD), F32), pltpu.VMEM((8, HEAD), F32)],
        compiler_params=_cp(("parallel",)),
    )(z, d_o, pool_w, scale)


def _ep_residual(accs, ins, outs):
    outs[0][...] = ins[2][...] + accs[0]


def _swiglu_fwd(name, y, wg_gate, wg_up, tiles):
    m, kdim = y.shape
    _, a, b = wg_gate.shape
    n = 4 * b
    tm, tn, tk = tiles
    tm = min(tm, m)
    grid = (m // tm, n // tn, kdim // tk)

    def ep(accs, ins, outs):
        gt, up = accs
        outs[0][...] = (gt * _sigmoid(gt) * up).astype(BF16)
        outs[1][...] = gt.astype(BF16)
        outs[2][...] = up.astype(BF16)

    wspec = _w_spec_nn(wg_gate, "col", tk, tn)
    out = pl.BlockSpec((tm, tn), lambda i, j, k: (i, j))
    return _matmul(name, grid, [y, wg_gate, wg_up], [pl.BlockSpec((tm, tk), lambda i, j, k: (i, k)), wspec, wspec],
                   [(0, 1, 0, NN), (0, 2, 1, NN)], 2, (tm, tn), ep, [out] * 3,
                   [jax.ShapeDtypeStruct((m, n), BF16)] * 3)


def _swiglu_bwd(name, dh, wg_down, gate, up, tiles):
    m, n = dh.shape
    _, a, b = wg_down.shape
    f = 4 * a
    tm, to, tc = tiles
    tm = min(tm, m)
    grid = (m // tm, f // to, n // tc)

    def ep(accs, ins, outs):
        d_act = accs[0]
        gt = ins[2][...].astype(F32)
        upv = ins[3][...].astype(F32)
        sg = _sigmoid(gt)
        outs[0][...] = (d_act * upv * sg * (1.0 + gt * (1.0 - sg))).astype(BF16)
        outs[1][...] = (d_act * gt * sg).astype(BF16)
        outs[2][...] = (gt * sg * upv).astype(BF16)

    tile = pl.BlockSpec((tm, to), lambda i, j, k: (i, j))
    return _matmul(name, grid, [dh, wg_down, gate, up],
                   [pl.BlockSpec((tm, tc), lambda i, j, k: (i, k)), _w_spec_nt(wg_down, "row", to, tc), tile, tile],
                   [(0, 1, 0, NT)], 1, (tm, to), ep, [tile] * 3, [jax.ShapeDtypeStruct((m, f), BF16)] * 3)


def _ffn_dy(name, d_gate, d_up, wg_gate, wg_up, tiles):
    m, n = d_gate.shape
    _, a, b = wg_gate.shape
    kdim = 2 * a
    tm, to, tc = tiles
    tm = min(tm, m)
    grid = (m // tm, kdim // to, n // tc)
    lhs = pl.BlockSpec((tm, tc), lambda i, j, k: (i, k))
    wspec = _w_spec_nt(wg_gate, "col", to, tc)
    return _matmul(name, grid, [d_gate, d_up, wg_gate, wg_up], [lhs, lhs, wspec, wspec],
                   [(0, 2, 0, NT), (1, 3, 0, NT)], 1, (tm, to), _ep_store(BF16),
                   [pl.BlockSpec((tm, to), lambda i, j, k: (i, j))], [jax.ShapeDtypeStruct((m, kdim), BF16)])[0]


def _ple_fwd(name, y, wg, h, pp, tiles):
    m, d = h.shape
    tm, tn, tk = tiles
    tm = min(tm, m)

    def ep(accs, ins, outs):
        pg = accs[0]
        outs[0][...] = ins[2][...] + _sigmoid(pg) * ins[3][...].astype(F32)
        outs[1][...] = pg.astype(BF16)

    tile = pl.BlockSpec((tm, tn), lambda i, j, k: (i, j))
    return _mm_nn(name, y, wg, "row", (tm, tn, tk), epilogue=ep, extra=[h, pp], extra_specs=[tile, tile],
                  out_specs=[tile, tile],
                  out_shape=[jax.ShapeDtypeStruct((m, d), F32), jax.ShapeDtypeStruct((m, d), BF16)])


def _ple_bwd(name, dh, pg, pp, after=()):
    s, d = dh.shape
    tr = _row_tile(s, 512)

    def body(dh_ref, pg_ref, pp_ref, *rest):
        dpp_ref, dpg_ref = rest[-2:]
        dhv = dh_ref[...]
        sg = _sigmoid(pg_ref[...].astype(F32))
        dpp_ref[...] = (dhv * sg).astype(BF16)
        dpg_ref[...] = (dhv * pp_ref[...].astype(F32) * sg * (1.0 - sg)).astype(BF16)

    row = pl.BlockSpec((tr, d), lambda i: (i, 0))
    return pl.pallas_call(
        body, name=name, grid=(s // tr,), in_specs=[row] * 3 + [ANY_SPEC] * len(after), out_specs=[row] * 2,
        out_shape=[jax.ShapeDtypeStruct((s, d), BF16)] * 2, compiler_params=_cp(("parallel",)),
    )(dh, pg, pp, *after)


BIG = ["w_in", "w_out", "w_gate", "w_up", "w_down", "w_ple_gate", "w_ple_proj"]
KIND = {"w_in": "col", "w_out": "row", "w_gate": "col", "w_up": "col", "w_down": "row", "w_ple_gate": "row",
        "w_ple_proj": "col"}
SMALL = ["norm_mix_g", "sgu_ln_g", "sgu_ln_b", "sgu_w", "sgu_b", "sc_conv_w", "cf_conv_w", "cf_conv_b", "cf_ln_g",
         "cf_ln_b", "pool_w", "pool_scale", "norm_ffn_g", "norm_ple_g", "final_norm_g"]
CHIP_SPLIT = ["sc_conv_w", "cf_conv_w"]
WEIGHTS = ['norm_mix_g', 'w_in', 'sgu_ln_g', 'sgu_ln_b', 'sgu_w', 'sgu_b', 'sc_conv_w', 'cf_conv_w', 'cf_conv_b',
           'cf_ln_g', 'cf_ln_b', 'pool_w', 'pool_scale', 'w_out', 'norm_ffn_g', 'w_gate', 'w_up', 'w_down',
           'norm_ple_g', 'w_ple_gate', 'w_ple_proj', 'final_norm_g']


def _tile(n, want):
    if n <= want:
        return n
    t = (want // 128) * 128
    while n % t:
        t -= 128
    return t


def _pack_rows(vecs):
    flat = jnp.concatenate([v.reshape(-1) for v in vecs])
    n = flat.shape[0]
    quantum = PACK_ROWS * 128
    padded = ((n + quantum - 1) // quantum) * quantum
    return jnp.pad(flat, (0, padded - n)).reshape(padded // 128, 128), n


def _unpack(flat, shapes):
    out, off = [], 0
    for shp in shapes:
        size = math.prod(shp)
        out.append(flat[off:off + size].reshape(shp))
        off += size
    return out


def kernel(x, p, norm_mix_g, w_in, sgu_ln_g, sgu_ln_b, sgu_w, sgu_b, sc_conv_w, cf_conv_w, cf_conv_b, cf_ln_g, cf_ln_b, pool_w, pool_scale, w_out, norm_ffn_g, w_gate, w_up, w_down, norm_ple_g, w_ple_gate, w_ple_proj, final_norm_g, loss_target, m_norm_mix_g, m_w_in, m_sgu_ln_g, m_sgu_ln_b, m_sgu_w, m_sgu_b, m_sc_conv_w, m_cf_conv_w, m_cf_conv_b, m_cf_ln_g, m_cf_ln_b, m_pool_w, m_pool_scale, m_w_out, m_norm_ffn_g, m_w_gate, m_w_up, m_w_down, m_norm_ple_g, m_w_ple_gate, m_w_ple_proj, m_final_norm_g, v_norm_mix_g, v_w_in, v_sgu_ln_g, v_sgu_ln_b, v_sgu_w, v_sgu_b, v_sc_conv_w, v_cf_conv_w, v_cf_conv_b, v_cf_ln_g, v_cf_ln_b, v_pool_w, v_pool_scale, v_w_out, v_norm_ffn_g, v_w_gate, v_w_up, v_w_down, v_norm_ple_g, v_w_ple_gate, v_w_ple_proj, v_final_norm_g):
    args = dict(locals())
    w = {n: args[n] for n in WEIGHTS}
    mom = {n: args["m_" + n] for n in WEIGHTS}
    var = {n: args["v_" + n] for n in WEIGHTS}
    depth = w_in.shape[0]
    s, d = x.shape[1], x.shape[2]
    f_dim = 4 * w_gate.shape[2]
    xi, yi, ci = lax.axis_index("x"), lax.axis_index("y"), lax.axis_index("c")
    c_idx = ci.astype(jnp.int32).reshape(1)

    stacked = [_cast_bf16(f"cast_{n}", w[n]) for n in BIG]
    axes = [0 if KIND[n] == "col" else 1 for n in BIG]
    gather = _gather_start("gather_start_0", stacked, 0, axes, stacked[0])
    conv_pack = jnp.concatenate([sc_conv_w, cf_conv_w], axis=1)
    taps = conv_pack.shape[1]
    rows_pad = ((depth * taps + 7) // 8) * 8
    conv_rows = jnp.pad(conv_pack.reshape(depth * taps, HEAD), ((0, rows_pad - depth * taps), (0, 0)))
    conv_all = _allgather8("gather_conv_weights", [conv_rows])[0]
    conv_full = conv_all[0::2, :depth * taps].reshape(4, depth, taps, HEAD)
    conv_full = jnp.transpose(conv_full, (1, 2, 0, 3)).reshape(depth, taps, GROUP)
    sc_w_full, cf_w_full = conv_full[:, :3], conv_full[:, 3:]

    h = x[0]
    saved = []
    gathered = []
    for l in range(depth):
        send_sems, recv_sems, lands, _ = gather
        lands = _gather_wait(f"gather_wait_{l}", stacked, l, axes, send_sems, recv_sems, lands, h)
        lands = _gather_share(f"gather_share_{l}", stacked, l, axes, lands)
        wg = dict(zip(BIG, lands))
        gathered.append(wg)
        started = ()
        if l + 1 < depth:
            gather = _gather_start(f"gather_start_{l + 1}", stacked, l + 1, axes, lands[0])
            started = (gather[3],)
        sv = {"h0": h}
        y1 = _rms_fwd("rms_mix", h, norm_mix_g[l:l + 1], after=started)
        z = _mm_nn("mm_in", y1, wg["w_in"], "col", (1024, 1024, 1024))[0]
        lg, lb = sgu_ln_g[l][:, None, :], sgu_ln_b[l][:, None, :]
        sb = sgu_b[l][:, :, None]
        oa = _sgu_fwd("sgu_fwd", z, lg, lb, sgu_w[l], sb)
        ob = _shortconv_fwd("shortconv_fwd", z, sc_w_full[l])
        cc = _conformer_conv_fwd("conformer_conv_fwd", z, cf_w_full[l], cf_conv_b[l:l + 1])
        oc = _conformer_ln_fwd("conformer_ln_fwd", cc, cf_ln_g[l:l + 1], cf_ln_b[l:l + 1])
        od = _pool_fwd("pool_fwd", z, pool_w[l], pool_scale[l:l + 1])
        o = jnp.concatenate([oa, ob, oc, od], axis=1)
        tile = pl.BlockSpec((min(1024, s), 1024), lambda i, j, k: (i, j))
        h1 = _mm_nn("mm_out", o, wg["w_out"], "row", (1024, 1024, 512), epilogue=_ep_residual, extra=[h],
                    extra_specs=[tile], out_specs=[tile], out_shape=[jax.ShapeDtypeStruct((s, d), F32)])[0]
        y2 = _rms_fwd("rms_ffn", h1, norm_ffn_g[l:l + 1])
        fs = f_dim // 4
        act, gt, up = _swiglu_fwd("mm_swiglu", y2, wg["w_gate"], wg["w_up"], (512, fs, 1024))
        h2 = _mm_nn("mm_down", act, wg["w_down"], "row", (1024, 1024, fs), epilogue=_ep_residual, extra=[h1],
                    extra_specs=[tile], out_specs=[tile], out_shape=[jax.ShapeDtypeStruct((s, d), F32)])[0]
        y3 = _rms_fwd("rms_ple", h2, norm_ple_g[l:l + 1])
        pb = p[l, 0].astype(BF16)
        ple_k = w_ple_proj.shape[1] // 2
        ptile = pl.BlockSpec((min(1024, s), 512), lambda i, j, k: (i, j))
        pp = _mm_nn("mm_ple_proj", pb, wg["w_ple_proj"], "col", (1024, 512, ple_k), epilogue=_ep_store(BF16),
                    out_specs=[ptile], out_shape=[jax.ShapeDtypeStruct((s, d), BF16)])[0]
        h3, pg = _ple_fwd("mm_ple_gate", y3, wg["w_ple_gate"], h2, pp, (1024, 1024, 512))
        sv.update(y1=y1, z=z, cc=cc, o=o, h1=h1, y2=y2, gt=gt, up=up, h2=h2, y3=y3, pb=pb, pp=pp, pg=pg)
        saved.append(sv)
        h = h3

    loss_part, dh, dhb, d_final_g = _loss_head("loss_head", h, final_norm_g[None, :], loss_target[0])

    small_grads = [None] * depth
    where = jnp.stack([2 * xi + yi, 2 * (1 - xi) + yi, 2 * xi + (1 - yi), 2 * (1 - xi) + (1 - yi), ci]).astype(jnp.int32)
    grad_bufs = [lax.empty((depth, 2, wn.shape[1] // 2, wn.shape[2]), F32) for wn in stacked]
    exchange = None

    def finish_exchange(pending, layer, bufs, after):
        send_sems, recv_sems, parts, lands, _ = pending
        parts, lands = _chip_exchange_wait(f"rs_chips_wait_{layer}", send_sems, recv_sems, parts, lands, after)
        return _sum_parts("rs_sum", parts, lands, bufs, layer, where)

    for l in reversed(range(depth)):
        wg = gathered[l]
        sv = saved[l]
        fs = f_dim // 4
        started = () if exchange is None else (exchange[4],)
        d_pp, d_pg = _ple_bwd("ple_bwd", dh, sv["pg"], sv["pp"], after=started)
        g_ple_proj = _mm_tn("dw_ple_proj", sv["pb"], d_pp, "col", (w_ple_proj.shape[1], 512, 1024))
        g_ple_gate = _mm_tn("dw_ple_gate", sv["y3"], d_pg, "row", (512, 1024, 1024))
        dy3 = _mm_nt("dx_ple_gate", d_pg, wg["w_ple_gate"], "row", (1024, 512, 1024))
        dh, dhb, dg_ple = _rms_bwd("rms_ple_bwd", sv["h2"], norm_ple_g[l:l + 1], dy3, dh)

        d_gt, d_up, act = _swiglu_bwd("dx_down_swiglu", dhb, wg["w_down"], sv["gt"], sv["up"], (512, fs, 1024))
        g_down = _mm_tn("dw_down", act, dhb, "row", (fs, 1024, 1024))
        g_gate = _mm_tn("dw_gate", sv["y2"], d_gt, "col", (1024, fs, 1024))
        g_up = _mm_tn("dw_up", sv["y2"], d_up, "col", (1024, fs, 1024))
        dy2 = _ffn_dy("dx_gate_up", d_gt, d_up, wg["w_gate"], wg["w_up"], (1024, 1024, fs))
        dh, dhb, dg_ffn = _rms_bwd("rms_ffn_bwd", sv["h1"], norm_ffn_g[l:l + 1], dy2, dh)

        g_out = _mm_tn("dw_out", sv["o"], dhb, "row", (512, 1024, 1024))
        d_o = _mm_nt("dx_out", dhb, wg["w_out"], "row", (1024, 512, 1024))
        z = sv["z"]
        lg, lb = sgu_ln_g[l][:, None, :], sgu_ln_b[l][:, None, :]
        sb = sgu_b[l][:, :, None]
        dzu, dzv, d_lg, d_lb, d_sw, d_sb = _sgu_bwd("sgu_bwd", z, d_o[:, 0:GROUP], lg, lb, sgu_w[l], sb)
        dzh, dzbg, dzcg, d_scw = _shortconv_bwd("shortconv_bwd", z, d_o[:, GROUP:2 * GROUP], sc_w_full[l])
        dcc, d_cflg, d_cflb, d_cfb = _conformer_ln_bwd("conformer_ln_bwd", sv["cc"], d_o[:, 2 * GROUP:3 * GROUP],
                                                       cf_ln_g[l:l + 1], cf_ln_b[l:l + 1])
        dza, dzg, d_cfw = _conformer_conv_bwd("conformer_conv_bwd", z, dcc, cf_w_full[l])
        dzd, d_pw, d_psc = _pool_bwd("pool_bwd", z, d_o[:, 3 * GROUP:], pool_w[l], pool_scale[l:l + 1])
        dz = jnp.concatenate([dzu, dzv, dzh, dzbg, dzcg, dza, dzg, dzd], axis=1)
        g_in = _mm_tn("dw_in", sv["y1"], dz, "col", (1024, 1024, 1024))
        dy1 = _mm_nt("dx_in", dz, wg["w_in"], "col", (1024, 1024, 1024))
        dh, dhb, dg_mix = _rms_bwd("rms_mix_bwd", sv["h0"], norm_mix_g[l:l + 1], dy1, dh)

        small_grads[l] = dict(norm_mix_g=dg_mix, sgu_ln_g=d_lg, sgu_ln_b=d_lb, sgu_w=d_sw, sgu_b=d_sb,
                              sc_conv_w=d_scw, cf_conv_w=d_cfw, cf_conv_b=d_cfb, cf_ln_g=d_cflg, cf_ln_b=d_cflb,
                              pool_w=d_pw, pool_scale=d_psc, norm_ffn_g=dg_ffn, norm_ple_g=dg_ple)
        big = dict(w_in=g_in, w_out=g_out, w_gate=g_gate, w_up=g_up, w_down=g_down, w_ple_gate=g_ple_gate,
                   w_ple_proj=g_ple_proj)
        if exchange is not None:
            grad_bufs = finish_exchange(exchange, l + 1, grad_bufs, dh)
        gs = [big[n].reshape(4, 2, big[n].shape[1] // 2, big[n].shape[2]) for n in BIG]
        from_sibling = _sibling_send_other_halves(f"rs_sibling_{l}", gs)
        chip_sums = _add_halves("rs_add", gs, from_sibling, c_idx)
        exchange = _chip_exchange_start(f"rs_chips_start_{l}", chip_sums)
    grad_x = dh[None]
    grad_bufs = finish_exchange(exchange, 0, grad_bufs, dh)
    joined = _sibling_join_halves("rs_join", grad_bufs)
    grads = {n: joined[a].reshape(w[n].shape) for a, n in enumerate(BIG)}

    per_layer = [n for n in SMALL if n != "final_norm_g"]
    vecs = [small_grads[l][n] for l in range(depth) for n in per_layer] + [d_final_g, loss_part]
    packed, _ = _pack_rows(vecs)
    total = _sum8("sum_small", _allgather8("gather_small", [packed])[0]).reshape(-1)
    full_shapes = {n: (w[n].shape[1:] if n not in CHIP_SPLIT else (w[n].shape[1], GROUP)) for n in per_layer}
    pieces = _unpack(total, [full_shapes[n] for _ in range(depth) for n in per_layer] + [(d,), ()])
    loss = pieces[-1]
    grads["final_norm_g"] = pieces[-2]
    chip_off = (2 * xi + yi) * HEAD
    for j, n in enumerate(per_layer):
        g = jnp.stack([pieces[l * len(per_layer) + j] for l in range(depth)])
        if n in CHIP_SPLIT:
            g = lax.dynamic_slice_in_dim(g, chip_off, HEAD, axis=2)
        grads[n] = g

    delta, new_m, new_v = {}, {}, {}
    for n in BIG:
        shp = w[n].shape
        two_d = (shp[0] * shp[1], shp[2])
        dl, mn, vn = _adamw(f"adamw_{n}", w[n].reshape(two_d), grads[n].reshape(two_d), mom[n].reshape(two_d),
                            var[n].reshape(two_d))
        delta[n], new_m[n], new_v[n] = dl.reshape(shp), mn.reshape(shp), vn.reshape(shp)
    small_shapes = [w[n].shape for n in SMALL]
    pw, _ = _pack_rows([w[n] for n in SMALL])
    pg_, _ = _pack_rows([grads[n] for n in SMALL])
    pm, _ = _pack_rows([mom[n] for n in SMALL])
    pv, _ = _pack_rows([var[n] for n in SMALL])
    dl, mn, vn = _adamw("adamw_small", pw, pg_, pm, pv)
    for n, a, b, cc_ in zip(SMALL, _unpack(dl.reshape(-1), small_shapes), _unpack(mn.reshape(-1), small_shapes),
                            _unpack(vn.reshape(-1), small_shapes)):
        delta[n], new_m[n], new_v[n] = a, b, cc_

    return (loss, grad_x, *[grads[n] for n in WEIGHTS], *[delta[n] for n in WEIGHTS],
            *[new_m[n] for n in WEIGHTS], *[new_v[n] for n in WEIGHTS])
```

```python
import functools
import math

import jax
import jax.numpy as jnp
from jax import lax
from jax.experimental import pallas as pl
from jax.experimental.pallas import tpu as pltpu

F32 = jnp.float32
BF16 = jnp.bfloat16
MESH = pl.DeviceIdType.MESH

HEAD = 128
GROUP = 512
EPS = 1e-6
HALO = 32
CHUNK_ROWS = 128
POOL_LEVELS = 4
PACK_ROWS = 512

ADAM_LR = 0.001
ADAM_B1 = 0.9
ADAM_B2 = 0.999
ADAM_EPS = 1e-08
ADAM_WD = 0.01
ADAM_STEP = 10

VMEM_LIMIT = 56 * 1024 * 1024


def _cp(sem=None, vmem=VMEM_LIMIT):
    return pltpu.CompilerParams(dimension_semantics=sem, vmem_limit_bytes=vmem)


def _sigmoid(x):
    return 0.5 * jnp.tanh(0.5 * x) + 0.5


_GELU_K = math.sqrt(2.0 / math.pi)
_GELU_C = 0.044715


def _gelu(x):
    t = jnp.tanh(_GELU_K * (x + _GELU_C * x * x * x))
    return 0.5 * x * (1.0 + t)


def _gelu_grad(x):
    t = jnp.tanh(_GELU_K * (x + _GELU_C * x * x * x))
    return 0.5 * (1.0 + t) + 0.5 * x * (1.0 - t * t) * _GELU_K * (1.0 + 3.0 * _GELU_C * x * x)


def _mesh_pos():
    return lax.axis_index("x"), lax.axis_index("y"), lax.axis_index("c")


def _any_specs(n):
    return [pl.BlockSpec(memory_space=pl.ANY)] * n


def _allgather8(name, blocks):
    n = len(blocks)

    def body(*refs):
        ins, outs = refs[:n], refs[n:2 * n]
        send_sems, recv_sems, local_sems = refs[2 * n:]
        x, y, c = _mesh_pos()
        me, sibling = (x, y, c), (x, y, 1 - c)
        chips = [(1 - x, y), (x, 1 - y), (1 - x, 1 - y)]

        def slot(a, dev):
            return outs[a].at[4 * dev[0] + 2 * dev[1] + dev[2]]

        def copy(a, k, block, to, src=None):
            dst = slot(a, block)
            return pltpu.make_async_remote_copy(
                src_ref=dst if src is None else src, dst_ref=dst,
                send_sem=send_sems.at[7 * a + k], recv_sem=recv_sems.at[7 * a + k],
                device_id=to, device_id_type=MESH)

        mine, first, passed = [], [], []
        for a in range(n):
            cp = pltpu.make_async_copy(ins[a], slot(a, me), local_sems.at[a])
            cp.start()
            mine.append(cp)
            cps = [copy(a, 0, me, sibling, src=ins[a])]
            cps += [copy(a, 1 + j, me, (*chip, c), src=ins[a]) for j, chip in enumerate(chips)]
            for cp in cps:
                cp.start()
            first += cps
        for j, chip in enumerate(chips):
            for a in range(n):
                copy(a, 1 + j, (*chip, c), me).wait_recv()
                cp = copy(a, 4 + j, (*chip, c), sibling)
                cp.start()
                passed.append(cp)
        for a in range(n):
            copy(a, 0, sibling, me).wait_recv()
            for j, chip in enumerate(chips):
                copy(a, 4 + j, (*chip, 1 - c), me).wait_recv()
        for cp in first + passed:
            cp.wait_send()
        for cp in mine:
            cp.wait()

    return pl.pallas_call(
        body, name=name,
        out_shape=[jax.ShapeDtypeStruct((8,) + b.shape, b.dtype) for b in blocks],
        in_specs=_any_specs(n), out_specs=_any_specs(n),
        scratch_shapes=[pltpu.SemaphoreType.DMA((7 * n,)), pltpu.SemaphoreType.DMA((7 * n,)),
                        pltpu.SemaphoreType.DMA((n,))],
    )(*blocks)


HBM_SPEC = pl.BlockSpec(memory_space=pltpu.HBM)
SEM_SPEC = pl.BlockSpec(memory_space=pltpu.SEMAPHORE)
ANY_SPEC = pl.BlockSpec(memory_space=pl.ANY)
SPLIT_COPY = pltpu.CompilerParams(has_side_effects=pltpu.SideEffectType.DATAFLOW_SIDE_EFFECTING)


def _hbm(x):
    return pltpu.with_memory_space_constraint(x, pltpu.HBM)


def _other_chips(x, y):
    return [(1 - x, y), (x, 1 - y), (1 - x, 1 - y)]


def _dev_slot(ref, dev):
    return ref.at[4 * dev[0] + 2 * dev[1] + dev[2]]


def _cast_into_landing(name, w, layer, ax, chip_idx):
    _, r, cc = w.shape
    r2, c2 = (r // 2, cc) if ax == 0 else (r, cc // 2)
    tr = _row_tile(r2, 512 if c2 <= 1536 else 256)
    nt = r2 // tr

    def body(chip_ref, w_ref, o_ref):
        o_ref[...] = w_ref[...].astype(BF16)

    if ax == 0:
        in_spec = pl.BlockSpec((None, tr, c2), lambda hf, i, chip_ref: (layer, hf * nt + i, 0))
    else:
        in_spec = pl.BlockSpec((None, tr, c2), lambda hf, i, chip_ref: (layer, i, hf))
    return pl.pallas_call(
        body, name=name,
        grid_spec=pltpu.PrefetchScalarGridSpec(
            num_scalar_prefetch=1, grid=(2, nt), in_specs=[in_spec],
            out_specs=pl.BlockSpec((None, tr, c2), lambda hf, i, chip_ref: (2 * chip_ref[0] + hf, i, 0))),
        out_shape=jax.ShapeDtypeStruct((8, r2, c2), BF16), compiler_params=_cp(("parallel", "parallel")),
    )(chip_idx, w)


def _gather_ici_copies(lands, send_sems, recv_sems):
    x, y, c = _mesh_pos()
    pairs = []
    for a in range(len(lands)):
        for j, chip in enumerate(_other_chips(x, y)):
            def copy(dev):
                return pltpu.make_async_remote_copy(
                    src_ref=_dev_slot(lands[a], dev), dst_ref=_dev_slot(lands[a], dev),
                    send_sem=send_sems.at[3 * a + j], recv_sem=recv_sems.at[3 * a + j],
                    device_id=(*chip, c), device_id_type=MESH)
            pairs.append((copy((x, y, c)), copy((*chip, c))))
    return pairs


def _exchange_ici_copies(ins, lands, send_sems, recv_sems):
    x, y, c = _mesh_pos()
    pairs = []
    for a in range(len(ins)):
        for k, chip in enumerate(_other_chips(x, y)):
            there = 2 * chip[0] + chip[1]
            def copy(dst_entry):
                return pltpu.make_async_remote_copy(
                    src_ref=ins[a].at[there], dst_ref=lands[a].at[dst_entry],
                    send_sem=send_sems.at[3 * a + k], recv_sem=recv_sems.at[3 * a + k],
                    device_id=(*chip, c), device_id_type=MESH)
            pairs.append((copy(2 * x + y), copy(there)))
    return pairs


def _gather_start(name, lands, dep):
    n = len(lands)

    def body(*refs):
        token = refs[-1]
        for send, _ in _gather_ici_copies(refs[:n], refs[n + 1], refs[n + 2]):
            send.start()
        token[...] = jnp.zeros_like(token)

    out = pl.pallas_call(
        body, name=name,
        out_shape=(pltpu.SemaphoreType.DMA((3 * n,)), pltpu.SemaphoreType.DMA((3 * n,)),
                   *[pltpu.HBM(l.shape, l.dtype) for l in lands], jax.ShapeDtypeStruct((8, 128), F32)),
        in_specs=[HBM_SPEC] * n + [ANY_SPEC],
        out_specs=(SEM_SPEC, SEM_SPEC, *[HBM_SPEC] * n, pl.BlockSpec(memory_space=pltpu.VMEM)),
        input_output_aliases={a: 2 + a for a in range(n)},
        compiler_params=SPLIT_COPY,
    )(*[_hbm(l) for l in lands], dep)
    return out[0], out[1], list(out[2:2 + n]), out[-1]


def _gather_wait(name, send_sems, recv_sems, lands, after):
    n = len(lands)

    def body(*refs):
        for send, recv in _gather_ici_copies(refs[:n], refs[n], refs[n + 1]):
            send.wait_send()
            recv.wait_recv()

    out = pl.pallas_call(
        body, name=name,
        out_shape=[pltpu.HBM(l.shape, l.dtype) for l in lands],
        in_specs=[HBM_SPEC] * n + [SEM_SPEC, SEM_SPEC, ANY_SPEC],
        out_specs=[HBM_SPEC] * n,
        input_output_aliases={a: a for a in range(n)},
        compiler_params=SPLIT_COPY,
    )(*lands, send_sems, recv_sems, after)
    return list(out)


def _gather_share(name, lands):
    n = len(lands)

    def body(*refs):
        outs = refs[n:2 * n]
        send_sems, recv_sems = refs[2 * n:]
        x, y, c = _mesh_pos()

        def copy(a, j, dev):
            slot = _dev_slot(outs[a], dev)
            return pltpu.make_async_remote_copy(
                src_ref=slot, dst_ref=slot, send_sem=send_sems.at[3 * a + j], recv_sem=recv_sems.at[3 * a + j],
                device_id=(x, y, 1 - c), device_id_type=MESH)

        chips = _other_chips(x, y)
        for a in range(n):
            for j, chip in enumerate(chips):
                copy(a, j, (*chip, c)).start()
        for a in range(n):
            for j, chip in enumerate(chips):
                copy(a, j, (*chip, c)).wait_send()
                copy(a, j, (*chip, 1 - c)).wait_recv()

    return pl.pallas_call(
        body, name=name,
        out_shape=[jax.ShapeDtypeStruct(l.shape, l.dtype) for l in lands],
        in_specs=_any_specs(n), out_specs=_any_specs(n),
        input_output_aliases={a: a for a in range(n)},
        scratch_shapes=[pltpu.SemaphoreType.DMA((3 * n,)), pltpu.SemaphoreType.DMA((3 * n,))],
    )(*lands)


def _sibling_send_other_halves(name, grads):
    n = len(grads)

    def body(*refs):
        ins, outs = refs[:n], refs[n:2 * n]
        send_sems, recv_sems = refs[2 * n:]
        x, y, c = _mesh_pos()
        cps = []
        for a in range(n):
            for j in range(4):
                cp = pltpu.make_async_remote_copy(
                    src_ref=ins[a].at[j, 1 - c], dst_ref=outs[a].at[j],
                    send_sem=send_sems.at[4 * a + j], recv_sem=recv_sems.at[4 * a + j],
                    device_id=(x, y, 1 - c), device_id_type=MESH)
                cp.start()
                cps.append(cp)
        for cp in cps:
            cp.wait()

    return pl.pallas_call(
        body, name=name,
        out_shape=[jax.ShapeDtypeStruct((4,) + g.shape[2:], g.dtype) for g in grads],
        in_specs=_any_specs(n), out_specs=_any_specs(n),
        scratch_shapes=[pltpu.SemaphoreType.DMA((4 * n,)), pltpu.SemaphoreType.DMA((4 * n,))],
    )(*grads)


def _chip_exchange_start(name, parts):
    n = len(parts)

    def body(*refs):
        ins, lands = refs[:n], refs[n:2 * n]
        token = refs[-1]
        for send, _ in _exchange_ici_copies(ins, lands, refs[2 * n], refs[2 * n + 1]):
            send.start()
        token[...] = jnp.zeros_like(token)

    lands = [_hbm(lax.empty(p.shape, p.dtype)) for p in parts]
    out = pl.pallas_call(
        body, name=name,
        out_shape=(pltpu.SemaphoreType.DMA((3 * n,)), pltpu.SemaphoreType.DMA((3 * n,)),
                   *[pltpu.HBM(p.shape, p.dtype) for p in parts], *[pltpu.HBM(p.shape, p.dtype) for p in parts],
                   jax.ShapeDtypeStruct((8, 128), F32)),
        in_specs=[HBM_SPEC] * (2 * n),
        out_specs=(SEM_SPEC, SEM_SPEC, *[HBM_SPEC] * (2 * n), pl.BlockSpec(memory_space=pltpu.VMEM)),
        input_output_aliases={i: 2 + i for i in range(2 * n)},
        compiler_params=SPLIT_COPY,
    )(*[_hbm(p) for p in parts], *lands)
    return out[0], out[1], list(out[2:2 + n]), list(out[2 + n:2 + 2 * n]), out[-1]


def _chip_exchange_wait(name, send_sems, recv_sems, parts, lands, after):
    n = len(parts)

    def body(*refs):
        ins, lands_in = refs[:n], refs[n:2 * n]
        for send, recv in _exchange_ici_copies(ins, lands_in, refs[2 * n], refs[2 * n + 1]):
            send.wait_send()
            recv.wait_recv()

    out = pl.pallas_call(
        body, name=name,
        out_shape=[pltpu.HBM(p.shape, p.dtype) for p in parts] * 2,
        in_specs=[HBM_SPEC] * (2 * n) + [SEM_SPEC, SEM_SPEC, ANY_SPEC],
        out_specs=[HBM_SPEC] * (2 * n),
        input_output_aliases={i: i for i in range(2 * n)},
        compiler_params=SPLIT_COPY,
    )(*parts, *lands, send_sems, recv_sems, after)
    return list(out[:n]), list(out[n:])


def _sibling_join_halves(name, bufs):
    n = len(bufs)
    depth = bufs[0].shape[0]

    def body(*refs):
        outs = refs[n:2 * n]
        send_sems, recv_sems = refs[2 * n:]
        x, y, c = _mesh_pos()

        def copy(a, l, half):
            return pltpu.make_async_remote_copy(
                src_ref=outs[a].at[l, half], dst_ref=outs[a].at[l, half],
                send_sem=send_sems.at[depth * a + l], recv_sem=recv_sems.at[depth * a + l],
                device_id=(x, y, 1 - c), device_id_type=MESH)

        for a in range(n):
            for l in range(depth):
                copy(a, l, c).start()
        for a in range(n):
            for l in range(depth):
                copy(a, l, c).wait_send()
                copy(a, l, 1 - c).wait_recv()

    return pl.pallas_call(
        body, name=name, out_shape=[jax.ShapeDtypeStruct(b.shape, b.dtype) for b in bufs],
        in_specs=_any_specs(n), out_specs=_any_specs(n),
        input_output_aliases={a: a for a in range(n)},
        scratch_shapes=[pltpu.SemaphoreType.DMA((depth * n,)), pltpu.SemaphoreType.DMA((depth * n,))],
    )(*bufs)


def _row_tile(rows, want):
    t = 1
    while t * 2 <= min(rows, want):
        t *= 2
    while rows % t:
        t //= 2
    return t


def _cast_bf16(name, w):
    l, r, c = w.shape
    tr = _row_tile(r, 512 if c <= 1536 else 256)

    def body(w_ref, o_ref):
        o_ref[...] = w_ref[...].astype(BF16)

    spec = pl.BlockSpec((None, tr, c), lambda i, j: (i, j, 0))
    return pl.pallas_call(
        body, name=name, grid=(l, r // tr), in_specs=[spec], out_specs=spec,
        out_shape=jax.ShapeDtypeStruct(w.shape, BF16), compiler_params=_cp(("parallel", "parallel")),
    )(w)


def _add_halves(name, grads, recvd, c_idx):
    outs = []
    for a, (g, r) in enumerate(zip(grads, recvd)):
        _, _, r2, cc = g.shape
        tr = _row_tile(r2, 512)

        def body(c_ref, g_ref, r_ref, o_ref):
            o_ref[...] = (g_ref[...].astype(F32) + r_ref[...].astype(F32)).astype(BF16)

        outs.append(pl.pallas_call(
            body, name=f"{name}_{a}",
            grid_spec=pltpu.PrefetchScalarGridSpec(
                num_scalar_prefetch=1, grid=(4, r2 // tr),
                in_specs=[pl.BlockSpec((None, None, tr, cc), lambda j, i, c_ref: (j, c_ref[0], i, 0)),
                          pl.BlockSpec((None, tr, cc), lambda j, i, c_ref: (j, i, 0))],
                out_specs=pl.BlockSpec((None, tr, cc), lambda j, i, c_ref: (j, i, 0))),
            out_shape=jax.ShapeDtypeStruct(r.shape, BF16), compiler_params=_cp(("parallel", "parallel")),
        )(c_idx, g, r))
    return outs


def _sum_parts(name, parts, recvd, bufs, layer, where):
    outs = []
    for a, (p, r, buf) in enumerate(zip(parts, recvd, bufs)):
        _, r2, cc = p.shape
        tr = _row_tile(r2, 256)

        def body(where_ref, own_ref, r1_ref, r2_ref, r3_ref, buf_ref, o_ref):
            acc = own_ref[...].astype(F32)
            for ref in (r1_ref, r2_ref, r3_ref):
                acc = acc + ref[...].astype(F32)
            o_ref[...] = acc

        def entry(k):
            return pl.BlockSpec((None, tr, cc), lambda i, where_ref: (where_ref[k], i, 0))

        outs.append(pl.pallas_call(
            body, name=f"{name}_{a}",
            grid_spec=pltpu.PrefetchScalarGridSpec(
                num_scalar_prefetch=1, grid=(r2 // tr,),
                in_specs=[entry(0), entry(1), entry(2), entry(3), ANY_SPEC],
                out_specs=pl.BlockSpec((None, None, tr, cc), lambda i, where_ref: (layer, where_ref[4], i, 0))),
            out_shape=jax.ShapeDtypeStruct(buf.shape, F32), input_output_aliases={5: 0},
            compiler_params=_cp(("parallel",)),
        )(where, p, r, r, r, buf))
    return outs


def _sum8(name, gathered):
    _, r, cc = gathered.shape
    tr = _row_tile(r, 512)

    def body(g_ref, o_ref):
        acc = g_ref[0]
        for i in range(1, 8):
            acc = acc + g_ref[i]
        o_ref[...] = acc

    return pl.pallas_call(
        body, name=name, grid=(r // tr,),
        in_specs=[pl.BlockSpec((8, tr, cc), lambda i: (0, i, 0))],
        out_specs=pl.BlockSpec((tr, cc), lambda i: (i, 0)),
        out_shape=jax.ShapeDtypeStruct((r, cc), F32), compiler_params=_cp(("parallel",)),
    )(gathered)


def _adamw(name, w, g, m, v):
    r, cc = w.shape
    tr = _row_tile(r, max(8, (256 * 1024) // cc))

    def body(w_ref, g_ref, m_ref, v_ref, d_ref, mo_ref, vo_ref):
        gg = g_ref[...]
        mn = ADAM_B1 * m_ref[...] + (1.0 - ADAM_B1) * gg
        vn = ADAM_B2 * v_ref[...] + (1.0 - ADAM_B2) * (gg * gg)
        m_hat = mn / (1.0 - ADAM_B1 ** ADAM_STEP)
        v_hat = vn / (1.0 - ADAM_B2 ** ADAM_STEP)
        d_ref[...] = -ADAM_LR * (m_hat / (jnp.sqrt(v_hat) + ADAM_EPS) + ADAM_WD * w_ref[...])
        mo_ref[...] = mn
        vo_ref[...] = vn

    spec = pl.BlockSpec((tr, cc), lambda i: (i, 0))
    return pl.pallas_call(
        body, name=name, grid=(r // tr,), in_specs=[spec] * 4, out_specs=[spec] * 3,
        out_shape=[jax.ShapeDtypeStruct(w.shape, F32)] * 3, compiler_params=_cp(("parallel",)),
    )(w, g, m, v)


def _rms_fwd(name, h, g, after=()):
    s, d = h.shape
    tr = _row_tile(s, 512)

    def body(h_ref, g_ref, *rest):
        y_ref = rest[-1]
        xv = h_ref[...]
        r = lax.rsqrt(jnp.mean(xv * xv, axis=-1, keepdims=True) + EPS)
        y_ref[...] = (xv * r * g_ref[...]).astype(BF16)

    return pl.pallas_call(
        body, name=name, grid=(s // tr,),
        in_specs=[pl.BlockSpec((tr, d), lambda i: (i, 0)), pl.BlockSpec((1, d), lambda i: (0, 0))]
        + [ANY_SPEC] * len(after),
        out_specs=pl.BlockSpec((tr, d), lambda i: (i, 0)),
        out_shape=jax.ShapeDtypeStruct((s, d), BF16), compiler_params=_cp(("parallel",)),
    )(h, g, *after)


def _rms_bwd_rows(xv, gv, dy):
    d = xv.shape[-1]
    r = lax.rsqrt(jnp.mean(xv * xv, axis=-1, keepdims=True) + EPS)
    dxn = dy * gv
    proj = jnp.sum(dxn * xv, axis=-1, keepdims=True) * (1.0 / d)
    dx = r * dxn - xv * (r * r * r) * proj
    return dx, dy * xv * r


def _rms_bwd(name, h, g, dy, dres):
    s, d = h.shape
    tr = _row_tile(s, 256)

    def body(h_ref, g_ref, dy_ref, dres_ref, dh_ref, dhb_ref, dg_ref):
        dx, dgp = _rms_bwd_rows(h_ref[...], g_ref[...], dy_ref[...].astype(F32))
        dh = dres_ref[...] + dx
        dh_ref[...] = dh
        dhb_ref[...] = dh.astype(BF16)

        @pl.when(pl.program_id(0) == 0)
        def _():
            dg_ref[...] = jnp.zeros_like(dg_ref)

        dg_ref[...] += jnp.sum(dgp, axis=0, keepdims=True)

    row = pl.BlockSpec((tr, d), lambda i: (i, 0))
    vec = pl.BlockSpec((1, d), lambda i: (0, 0))
    return pl.pallas_call(
        body, name=name, grid=(s // tr,), in_specs=[row, vec, row, row], out_specs=[row, row, vec],
        out_shape=[jax.ShapeDtypeStruct((s, d), F32), jax.ShapeDtypeStruct((s, d), BF16),
                   jax.ShapeDtypeStruct((1, d), F32)],
        compiler_params=_cp(("arbitrary",)),
    )(h, g, dy, dres)


def _loss_head(name, h, g, target):
    s, d = h.shape
    tr = _row_tile(s, 256)

    def body(h_ref, g_ref, t_ref, loss_ref, dh_ref, dhb_ref, dg_ref):
        xv = h_ref[...]
        gv = g_ref[...]
        r = lax.rsqrt(jnp.mean(xv * xv, axis=-1, keepdims=True) + EPS)
        err = xv * r * gv - t_ref[...]
        part = 0.5 * jnp.sum(jnp.sum(err * err, axis=-1, keepdims=True) * (1.0 / d), axis=0, keepdims=True)
        dx, dgp = _rms_bwd_rows(xv, gv, err * (1.0 / d))
        dh_ref[...] = dx
        dhb_ref[...] = dx.astype(BF16)

        @pl.when(pl.program_id(0) == 0)
        def _():
            dg_ref[...] = jnp.zeros_like(dg_ref)
            loss_ref[...] = jnp.zeros_like(loss_ref)

        dg_ref[...] += jnp.sum(dgp, axis=0, keepdims=True)
        loss_ref[...] += part

    row = pl.BlockSpec((tr, d), lambda i: (i, 0))
    vec = pl.BlockSpec((1, d), lambda i: (0, 0))
    one = pl.BlockSpec((1, 1), lambda i: (0, 0))
    return pl.pallas_call(
        body, name=name, grid=(s // tr,), in_specs=[row, vec, row], out_specs=[one, row, row, vec],
        out_shape=[jax.ShapeDtypeStruct((1, 1), F32), jax.ShapeDtypeStruct((s, d), F32),
                   jax.ShapeDtypeStruct((s, d), BF16), jax.ShapeDtypeStruct((1, d), F32)],
        compiler_params=_cp(("arbitrary",)),
    )(h, g, target)


NN = (((1,), (0,)), ((), ()))
NT = (((1,), (1,)), ((), ()))
TN = (((0,), (0,)), ((), ()))


def _matmul(name, grid, operands, in_specs, pairs, n_acc, acc_shape, epilogue, out_specs, out_shape):
    n_in = len(operands)
    n_out = len(out_shape)
    nk = grid[2]

    def body(*refs):
        ins, outs, accs = refs[:n_in], refs[n_in:n_in + n_out], refs[n_in + n_out:]
        k = pl.program_id(2)

        @pl.when(k == 0)
        def _():
            for acc in accs:
                acc[...] = jnp.zeros_like(acc)

        for li, ri, ai, dims in pairs:
            accs[ai][...] += lax.dot_general(ins[li][...], ins[ri][...], dims, preferred_element_type=F32)

        @pl.when(k == nk - 1)
        def _():
            epilogue([acc[...] for acc in accs], ins, outs)

    return pl.pallas_call(
        body, name=name, grid=grid, in_specs=in_specs, out_specs=out_specs, out_shape=out_shape,
        scratch_shapes=[pltpu.VMEM(acc_shape, F32)] * n_acc,
        compiler_params=_cp(("parallel", "parallel", "arbitrary")),
    )(*operands)


def _ep_store(dtype):
    def ep(accs, ins, outs):
        outs[0][...] = accs[0].astype(dtype)
    return ep


def _w_spec_nn(wg, kind, tk, tn):
    _, a, b = wg.shape
    if kind == "col":
        kph, bps = a // tk, b // tn
        return pl.BlockSpec((None, tk, tn), lambda i, j, k: (2 * (j // bps) + k // kph, k % kph, j % bps))
    kps, nph = a // tk, b // tn
    return pl.BlockSpec((None, tk, tn), lambda i, j, k: (2 * (k // kps) + j // nph, k % kps, j % nph))


def _w_spec_nt(wg, kind, to, tc):
    _, a, b = wg.shape
    if kind == "col":
        oph, cps = a // to, b // tc
        return pl.BlockSpec((None, to, tc), lambda i, jo, kc: (2 * (kc // cps) + jo // oph, jo % oph, kc % cps))
    ops, cph = a // to, b // tc
    return pl.BlockSpec((None, to, tc), lambda i, jo, kc: (2 * (jo // ops) + kc // cph, jo % ops, kc % cph))


def _mm_nn(name, x, wg, kind, tiles, epilogue=None, extra=(), extra_specs=(), out_specs=None, out_shape=None):
    m, kdim = x.shape
    _, a, b = wg.shape
    n = 4 * b if kind == "col" else 2 * b
    tm, tn, tk = tiles
    tm = min(tm, m)
    grid = (m // tm, n // tn, kdim // tk)
    in_specs = [pl.BlockSpec((tm, tk), lambda i, j, k: (i, k)), _w_spec_nn(wg, kind, tk, tn)] + list(extra_specs)
    if out_shape is None:
        out_shape = [jax.ShapeDtypeStruct((m, n), F32)]
        out_specs = [pl.BlockSpec((tm, tn), lambda i, j, k: (i, j))]
        epilogue = _ep_store(F32)
    return _matmul(name, grid, [x, wg] + list(extra), in_specs, [(0, 1, 0, NN)], 1, (tm, tn), epilogue,
                   out_specs, out_shape)


SUB_COLS = 256


def _matmul_wide(name, grid, operands, in_specs, groups, tn, epilogue, out_specs, out_shape):
    n_in = len(operands)

    def body(*refs):
        ins, outs = refs[:n_in], refs[n_in:]
        for s0 in range(0, tn, SUB_COLS):
            cols = slice(s0, min(s0 + SUB_COLS, tn))
            accs = []
            for group in groups:
                acc = None
                for li, (c0, cw), ri, dims in group:
                    rhs = ins[ri][:, cols] if dims == NN else ins[ri][cols, :]
                    part = lax.dot_general(ins[li][:, c0:c0 + cw], rhs, dims, preferred_element_type=F32)
                    acc = part if acc is None else acc + part
                accs.append(acc)
            epilogue(accs, ins, outs, cols)

    return pl.pallas_call(
        body, name=name, grid=grid, in_specs=in_specs, out_specs=out_specs, out_shape=out_shape,
        compiler_params=_cp(("parallel", "parallel")),
    )(*operands)


def _wide_store(dtype):
    def ep(accs, ins, outs, cols):
        outs[0][:, cols] = accs[0].astype(dtype)
    return ep


def _wide_nn_weights(wg, kind, tn, first):
    _, a, b = wg.shape
    per = b // tn
    if kind == "col":
        specs = [pl.BlockSpec((None, a, tn), lambda i, j, hf=hf: (2 * (j // per) + hf, 0, j % per)) for hf in range(2)]
    else:
        specs = [pl.BlockSpec((None, a, tn), lambda i, j, ch=ch: (2 * ch + j // per, 0, j % per)) for ch in range(4)]
    dots = [(0, (q * a, a), first + q, NN) for q in range(len(specs))]
    return [wg] * len(specs), specs, dots


def _wide_nt_weights(wg, kind, to, first):
    _, a, b = wg.shape
    per = a // to
    if kind == "col":
        specs = [pl.BlockSpec((None, to, b), lambda i, j, ch=ch: (2 * ch + j // per, j % per, 0)) for ch in range(4)]
    else:
        specs = [pl.BlockSpec((None, to, b), lambda i, j, hf=hf: (2 * (j // per) + hf, j % per, 0)) for hf in range(2)]
    dots = [(0, (q * b, b), first + q, NT) for q in range(len(specs))]
    return [wg] * len(specs), specs, dots


def _mm_nn_wide(name, x, wg, kind, tm, tn, epilogue=None, extra=(), extra_specs=(), out_specs=None, out_shape=None):
    m, kdim = x.shape
    _, a, b = wg.shape
    n = 4 * b if kind == "col" else 2 * b
    tm = min(tm, m)
    ws, wspecs, dots = _wide_nn_weights(wg, kind, tn, 1)
    if out_shape is None:
        out_shape = [jax.ShapeDtypeStruct((m, n), F32)]
        out_specs = [pl.BlockSpec((tm, tn), lambda i, j: (i, j))]
        epilogue = _wide_store(F32)
    return _matmul_wide(name, (m // tm, n // tn), [x] + ws + list(extra),
                        [pl.BlockSpec((tm, kdim), lambda i, j: (i, 0))] + wspecs + list(extra_specs),
                        [dots], tn, epilogue, out_specs, out_shape)


def _mm_nt_wide(name, dy, wg, kind, tm, to, out_dtype=BF16):
    m, n = dy.shape
    _, a, b = wg.shape
    kdim = 2 * a if kind == "col" else 4 * a
    tm = min(tm, m)
    ws, wspecs, dots = _wide_nt_weights(wg, kind, to, 1)
    return _matmul_wide(name, (m // tm, kdim // to), [dy] + ws,
                        [pl.BlockSpec((tm, n), lambda i, j: (i, 0))] + wspecs, [dots], to, _wide_store(out_dtype),
                        [pl.BlockSpec((tm, to), lambda i, j: (i, j))],
                        [jax.ShapeDtypeStruct((m, kdim), out_dtype)])[0]


def _mm_tn(name, a, dy, kind, tiles):
    m, kdim = a.shape
    _, n = dy.shape
    tr, tn, tmm = tiles
    tmm = min(tmm, m)
    grid = (kdim // tr, n // tn, m // tmm)
    in_specs = [pl.BlockSpec((tmm, tr), lambda r, j, k: (k, r)), pl.BlockSpec((tmm, tn), lambda r, j, k: (k, j))]
    if kind == "col":
        ns = n // 4
        bps = ns // tn
        out_shape = [jax.ShapeDtypeStruct((4, kdim, ns), BF16)]
        out_specs = [pl.BlockSpec((None, tr, tn), lambda r, j, k: (j // bps, r, j % bps))]
    else:
        rs = kdim // 4
        rps = rs // tr
        out_shape = [jax.ShapeDtypeStruct((4, rs, n), BF16)]
        out_specs = [pl.BlockSpec((None, tr, tn), lambda r, j, k: (r // rps, r % rps, j))]
    return _matmul(name, grid, [a, dy], in_specs, [(0, 1, 0, TN)], 1, (tr, tn), _ep_store(BF16),
                   out_specs, out_shape)[0]


def _zero_halo(pad_ref, s):
    z = jnp.zeros((HALO, pad_ref.shape[1]), F32)
    pad_ref[pl.ds(0, HALO), :] = z
    pad_ref[pl.ds(HALO + s, HALO), :] = z


def _window(pad_ref, r0, rows):
    return pad_ref[pl.ds(r0, rows + 2 * HALO), :]


def _delayed(win, k, rows):
    if k == 0:
        return win[HALO:HALO + rows]
    return pltpu.roll(win, k, axis=0)[HALO:HALO + rows]


def _advanced(win, k, rows):
    if k == 0:
        return win[HALO:HALO + rows]
    return pltpu.roll(win, win.shape[0] - k, axis=0)[HALO:HALO + rows]


def _fold8(x):
    return jnp.sum(x.reshape(x.shape[0] // 8, 8, x.shape[1]), axis=0)


def _chunks(s):
    rows = min(CHUNK_ROWS, s)
    return rows, s // rows


def _col_spec(s, first_block):
    return pl.BlockSpec((s, HEAD), lambda j: (0, first_block + j))


def _sgu_fwd(name, z, ln_g, ln_b, w, b):
    s = z.shape[0]
    tr = _row_tile(s, 1024)
    nh = GROUP // HEAD

    def body(u_ref, v_ref, lg_ref, lb_ref, w_ref, b_ref, o_ref):
        row = lax.broadcasted_iota(jnp.int32, (HEAD, HEAD), 0)
        col = lax.broadcasted_iota(jnp.int32, (HEAD, HEAD), 1)
        wm = jnp.where(row >= col, w_ref[...], 0.0).astype(BF16)
        for ck in range(tr // HEAD):
            rs = pl.ds(ck * HEAD, HEAD)
            u = _gelu(u_ref[rs, :])
            v = _gelu(v_ref[rs, :])
            mu = jnp.mean(v, axis=-1, keepdims=True)
            vc = v - mu
            var = jnp.mean(vc * vc, axis=-1, keepdims=True)
            vln = vc * lax.rsqrt(var + EPS) * lg_ref[...] + lb_ref[...]
            sp = jnp.dot(wm, vln.astype(BF16), preferred_element_type=F32) + b_ref[...]
            o_ref[rs, :] = (u * sp).astype(BF16)

    head_vec = pl.BlockSpec((None, 1, HEAD), lambda h, i: (h, 0, 0))
    return pl.pallas_call(
        body, name=name, grid=(nh, s // tr),
        in_specs=[pl.BlockSpec((tr, HEAD), lambda h, i: (i, h)), pl.BlockSpec((tr, HEAD), lambda h, i: (i, nh + h)),
                  head_vec, head_vec, pl.BlockSpec((None, HEAD, HEAD), lambda h, i: (h, 0, 0)),
                  pl.BlockSpec((None, HEAD, 1), lambda h, i: (h, 0, 0))],
        out_specs=pl.BlockSpec((tr, HEAD), lambda h, i: (i, h)),
        out_shape=jax.ShapeDtypeStruct((s, GROUP), BF16), compiler_params=_cp(("parallel", "parallel")),
    )(z, z, ln_g, ln_b, w, b)


def _sgu_bwd(name, z, d_o, ln_g, ln_b, w, b):
    s = z.shape[0]
    tr = _row_tile(s, 1024)
    nh = GROUP // HEAD

    def body(u_ref, v_ref, do_ref, lg_ref, lb_ref, w_ref, b_ref, du_ref, dv_ref, dlg_ref, dlb_ref, dw_ref, db_ref,
             dsp_acc):
        row = lax.broadcasted_iota(jnp.int32, (HEAD, HEAD), 0)
        col = lax.broadcasted_iota(jnp.int32, (HEAD, HEAD), 1)
        tril = row >= col
        wm = jnp.where(tril, w_ref[...], 0.0).astype(BF16)
        i = pl.program_id(1)

        @pl.when(i == 0)
        def _():
            dlg_ref[...] = jnp.zeros_like(dlg_ref)
            dlb_ref[...] = jnp.zeros_like(dlb_ref)
            dw_ref[...] = jnp.zeros_like(dw_ref)
            dsp_acc[...] = jnp.zeros_like(dsp_acc)

        dlg = jnp.zeros((1, HEAD), F32)
        dlb = jnp.zeros((1, HEAD), F32)
        dw = jnp.zeros((HEAD, HEAD), F32)
        dsp_sum = jnp.zeros((HEAD, HEAD), F32)
        for ck in range(tr // HEAD):
            rs = pl.ds(ck * HEAD, HEAD)
            zu = u_ref[rs, :]
            zv = v_ref[rs, :]
            u = _gelu(zu)
            v = _gelu(zv)
            mu = jnp.mean(v, axis=-1, keepdims=True)
            vc = v - mu
            var = jnp.mean(vc * vc, axis=-1, keepdims=True)
            rstd = lax.rsqrt(var + EPS)
            xh = vc * rstd
            vln = (xh * lg_ref[...] + lb_ref[...]).astype(BF16)
            sp = jnp.dot(wm, vln, preferred_element_type=F32) + b_ref[...]
            d_oa = do_ref[rs, :].astype(F32)
            du = d_oa * sp
            dsp = d_oa * u
            dsp_b = dsp.astype(BF16)
            dvln = lax.dot_general(wm, dsp_b, TN, preferred_element_type=F32)
            dw = dw + lax.dot_general(dsp_b, vln, NT, preferred_element_type=F32)
            dsp_sum = dsp_sum + dsp
            dlg = dlg + jnp.sum(dvln * xh, axis=0, keepdims=True)
            dlb = dlb + jnp.sum(dvln, axis=0, keepdims=True)
            dxh = dvln * lg_ref[...]
            dv = rstd * (dxh - jnp.mean(dxh, axis=-1, keepdims=True)
                         - xh * jnp.mean(dxh * xh, axis=-1, keepdims=True))
            du_ref[rs, :] = (du * _gelu_grad(zu)).astype(BF16)
            dv_ref[rs, :] = (dv * _gelu_grad(zv)).astype(BF16)
        dlg_ref[...] += dlg
        dlb_ref[...] += dlb
        dw_ref[...] += jnp.where(tril, dw, 0.0)
        dsp_acc[...] += dsp_sum

        @pl.when(i == pl.num_programs(1) - 1)
        def _():
            db_ref[...] = jnp.sum(dsp_acc[...], axis=1, keepdims=True)

    head_vec = pl.BlockSpec((None, 1, HEAD), lambda h, i: (h, 0, 0))
    head_mat = pl.BlockSpec((None, HEAD, HEAD), lambda h, i: (h, 0, 0))
    head_col = pl.BlockSpec((None, HEAD, 1), lambda h, i: (h, 0, 0))
    return pl.pallas_call(
        body, name=name, grid=(nh, s // tr),
        in_specs=[pl.BlockSpec((tr, HEAD), lambda h, i: (i, h)), pl.BlockSpec((tr, HEAD), lambda h, i: (i, nh + h)),
                  pl.BlockSpec((tr, HEAD), lambda h, i: (i, h)), head_vec, head_vec, head_mat, head_col],
        out_specs=[pl.BlockSpec((tr, HEAD), lambda h, i: (i, h)), pl.BlockSpec((tr, HEAD), lambda h, i: (i, h)),
                   head_vec, head_vec, head_mat, head_col],
        out_shape=[jax.ShapeDtypeStruct((s, GROUP), BF16), jax.ShapeDtypeStruct((s, GROUP), BF16),
                   jax.ShapeDtypeStruct((nh, 1, HEAD), F32), jax.ShapeDtypeStruct((nh, 1, HEAD), F32),
                   jax.ShapeDtypeStruct((nh, HEAD, HEAD), F32), jax.ShapeDtypeStruct((nh, HEAD, 1), F32)],
        scratch_shapes=[pltpu.VMEM((HEAD, HEAD), F32)],
        compiler_params=_cp(("parallel", "arbitrary")),
    )(z, z, d_o, ln_g, ln_b, w, b)


def _shortconv_fwd(name, z, w):
    s = z.shape[0]
    kw = w.shape[0]
    rows, nchunk = _chunks(s)
    nb = GROUP // HEAD

    def body(h_ref, bg_ref, cg_ref, w_ref, o_ref, pad):
        _zero_halo(pad, s)

        def fill(ci, carry):
            r0 = pl.multiple_of(ci * rows, rows)
            pad[pl.ds(pl.multiple_of(HALO + r0, 8), rows), :] = cg_ref[pl.ds(r0, rows), :] * h_ref[pl.ds(r0, rows), :]
            return carry

        lax.fori_loop(0, nchunk, fill, 0)

        def step(ci, carry):
            r0 = pl.multiple_of(ci * rows, rows)
            win = _window(pad, r0, rows)
            cv = jnp.zeros((rows, HEAD), F32)
            for k in range(kw):
                cv = cv + w_ref[k:k + 1, :] * _delayed(win, kw - 1 - k, rows)
            o_ref[pl.ds(r0, rows), :] = (bg_ref[pl.ds(r0, rows), :] * cv).astype(BF16)
            return carry

        lax.fori_loop(0, nchunk, step, 0)

    return pl.pallas_call(
        body, name=name, grid=(nb,),
        in_specs=[_col_spec(s, 8), _col_spec(s, 12), _col_spec(s, 16), pl.BlockSpec((kw, HEAD), lambda j: (0, j))],
        out_specs=_col_spec(s, 0),
        out_shape=jax.ShapeDtypeStruct((s, GROUP), BF16),
        scratch_shapes=[pltpu.VMEM((s + 2 * HALO, HEAD), F32)],
        compiler_params=_cp(("parallel",)),
    )(z, z, z, w)


def _shortconv_bwd(name, z, d_o, w):
    s = z.shape[0]
    kw = w.shape[0]
    rows, nchunk = _chunks(s)
    nb = GROUP // HEAD

    def body(h_ref, bg_ref, cg_ref, do_ref, w_ref, dh_ref, dbg_ref, dcg_ref, dw_ref, pad_q, pad_d, acc):
        _zero_halo(pad_q, s)
        _zero_halo(pad_d, s)
        acc[...] = jnp.zeros_like(acc)

        def fill(ci, carry):
            r0 = pl.multiple_of(ci * rows, rows)
            rs = pl.ds(r0, rows)
            ps = pl.ds(pl.multiple_of(HALO + r0, 8), rows)
            pad_q[ps, :] = cg_ref[rs, :] * h_ref[rs, :]
            pad_d[ps, :] = do_ref[rs, :].astype(F32) * bg_ref[rs, :]
            return carry

        lax.fori_loop(0, nchunk, fill, 0)

        def step(ci, carry):
            r0 = pl.multiple_of(ci * rows, rows)
            rs = pl.ds(r0, rows)
            wq = _window(pad_q, r0, rows)
            wd = _window(pad_d, r0, rows)
            dcv = wd[HALO:HALO + rows]
            cv = jnp.zeros((rows, HEAD), F32)
            dq = jnp.zeros((rows, HEAD), F32)
            for k in range(kw):
                qk = _delayed(wq, kw - 1 - k, rows)
                cv = cv + w_ref[k:k + 1, :] * qk
                dq = dq + w_ref[k:k + 1, :] * _advanced(wd, kw - 1 - k, rows)
                acc[k] += _fold8(dcv * qk)
            dbg_ref[rs, :] = (do_ref[rs, :].astype(F32) * cv).astype(BF16)
            dcg_ref[rs, :] = (dq * h_ref[rs, :]).astype(BF16)
            dh_ref[rs, :] = (dq * cg_ref[rs, :]).astype(BF16)
            return carry

        lax.fori_loop(0, nchunk, step, 0)
        for k in range(kw):
            dw_ref[k:k + 1, :] = jnp.sum(acc[k], axis=0, keepdims=True)

    col = _col_spec(s, 0)
    return pl.pallas_call(
        body, name=name, grid=(nb,),
        in_specs=[_col_spec(s, 8), _col_spec(s, 12), _col_spec(s, 16), col, pl.BlockSpec((kw, HEAD), lambda j: (0, j))],
        out_specs=[col, col, col, pl.BlockSpec((kw, HEAD), lambda j: (0, j))],
        out_shape=[jax.ShapeDtypeStruct((s, GROUP), BF16)] * 3 + [jax.ShapeDtypeStruct((kw, GROUP), F32)],
        scratch_shapes=[pltpu.VMEM((s + 2 * HALO, HEAD), F32), pltpu.VMEM((s + 2 * HALO, HEAD), F32),
                        pltpu.VMEM((kw, 8, HEAD), F32)],
        compiler_params=_cp(("parallel",)),
    )(z, z, z, d_o, w)


def _conformer_conv_fwd(name, z, w, bias):
    s = z.shape[0]
    kw = w.shape[0]
    rows, nchunk = _chunks(s)
    nb = GROUP // HEAD

    def body(a_ref, g_ref, w_ref, b_ref, o_ref, pad):
        _zero_halo(pad, s)

        def fill(ci, carry):
            r0 = pl.multiple_of(ci * rows, rows)
            rs = pl.ds(r0, rows)
            pad[pl.ds(pl.multiple_of(HALO + r0, 8), rows), :] = a_ref[rs, :] * _sigmoid(g_ref[rs, :])
            return carry

        lax.fori_loop(0, nchunk, fill, 0)

        def step(ci, carry):
            r0 = pl.multiple_of(ci * rows, rows)
            win = _window(pad, r0, rows)
            cc = jnp.zeros((rows, HEAD), F32)
            for k in range(kw):
                cc = cc + w_ref[k:k + 1, :] * _delayed(win, kw - 1 - k, rows)
            o_ref[pl.ds(r0, rows), :] = cc + b_ref[...]
            return carry

        lax.fori_loop(0, nchunk, step, 0)

    return pl.pallas_call(
        body, name=name, grid=(nb,),
        in_specs=[_col_spec(s, 20), _col_spec(s, 24), pl.BlockSpec((kw, HEAD), lambda j: (0, j)),
                  pl.BlockSpec((1, HEAD), lambda j: (0, j))],
        out_specs=_col_spec(s, 0),
        out_shape=jax.ShapeDtypeStruct((s, GROUP), F32),
        scratch_shapes=[pltpu.VMEM((s + 2 * HALO, HEAD), F32)],
        compiler_params=_cp(("parallel",)),
    )(z, z, w, bias)


def _ln_rows(cc, g, b):
    mu = jnp.mean(cc, axis=-1, keepdims=True)
    xc = cc - mu
    var = jnp.mean(xc * xc, axis=-1, keepdims=True)
    rstd = lax.rsqrt(var + EPS)
    xh = xc * rstd
    return xh, rstd, xh * g + b


def _conformer_ln_fwd(name, cc, g, b):
    s, d = cc.shape
    tr = _row_tile(s, 512)

    def body(c_ref, g_ref, b_ref, o_ref):
        _, _, l = _ln_rows(c_ref[...], g_ref[...], b_ref[...])
        o_ref[...] = (l * _sigmoid(l)).astype(BF16)

    row = pl.BlockSpec((tr, d), lambda i: (i, 0))
    vec = pl.BlockSpec((1, d), lambda i: (0, 0))
    return pl.pallas_call(
        body, name=name, grid=(s // tr,), in_specs=[row, vec, vec], out_specs=row,
        out_shape=jax.ShapeDtypeStruct((s, d), BF16), compiler_params=_cp(("parallel",)),
    )(cc, g, b)


def _conformer_ln_bwd(name, cc, d_o, g, b):
    s, d = cc.shape
    tr = _row_tile(s, 512)

    def body(c_ref, do_ref, g_ref, b_ref, dcc_ref, dg_ref, db_ref, dcb_ref):
        xh, rstd, l = _ln_rows(c_ref[...], g_ref[...], b_ref[...])
        sg = _sigmoid(l)
        dl = do_ref[...].astype(F32) * sg * (1.0 + l * (1.0 - sg))
        dxh = dl * g_ref[...]
        dcc = rstd * (dxh - jnp.mean(dxh, axis=-1, keepdims=True) - xh * jnp.mean(dxh * xh, axis=-1, keepdims=True))
        dcc_ref[...] = dcc

        @pl.when(pl.program_id(0) == 0)
        def _():
            dg_ref[...] = jnp.zeros_like(dg_ref)
            db_ref[...] = jnp.zeros_like(db_ref)
            dcb_ref[...] = jnp.zeros_like(dcb_ref)

        dg_ref[...] += jnp.sum(dl * xh, axis=0, keepdims=True)
        db_ref[...] += jnp.sum(dl, axis=0, keepdims=True)
        dcb_ref[...] += jnp.sum(dcc, axis=0, keepdims=True)

    row = pl.BlockSpec((tr, d), lambda i: (i, 0))
    vec = pl.BlockSpec((1, d), lambda i: (0, 0))
    return pl.pallas_call(
        body, name=name, grid=(s // tr,), in_specs=[row, row, vec, vec], out_specs=[row, vec, vec, vec],
        out_shape=[jax.ShapeDtypeStruct((s, d), F32)] + [jax.ShapeDtypeStruct((1, d), F32)] * 3,
        compiler_params=_cp(("arbitrary",)),
    )(cc, d_o, g, b)


def _conformer_conv_bwd(name, z, dcc, w):
    s = z.shape[0]
    kw = w.shape[0]
    rows, nchunk = _chunks(s)
    nb = GROUP // HEAD

    def body(a_ref, g_ref, d_ref, w_ref, da_ref, dg_ref, dw_ref, pad_h, pad_d, acc):
        _zero_halo(pad_h, s)
        _zero_halo(pad_d, s)
        acc[...] = jnp.zeros_like(acc)

        def fill(ci, carry):
            r0 = pl.multiple_of(ci * rows, rows)
            rs = pl.ds(r0, rows)
            ps = pl.ds(pl.multiple_of(HALO + r0, 8), rows)
            pad_h[ps, :] = a_ref[rs, :] * _sigmoid(g_ref[rs, :])
            pad_d[ps, :] = d_ref[rs, :]
            return carry

        lax.fori_loop(0, nchunk, fill, 0)

        def step(ci, carry):
            r0 = pl.multiple_of(ci * rows, rows)
            rs = pl.ds(r0, rows)
            wh = _window(pad_h, r0, rows)
            wd = _window(pad_d, r0, rows)
            dcc_c = wd[HALO:HALO + rows]
            dhc = jnp.zeros((rows, HEAD), F32)
            for k in range(kw):
                dhc = dhc + w_ref[k:k + 1, :] * _advanced(wd, kw - 1 - k, rows)
                acc[k] += _fold8(dcc_c * _delayed(wh, kw - 1 - k, rows))
            sg = _sigmoid(g_ref[rs, :])
            da_ref[rs, :] = (dhc * sg).astype(BF16)
            dg_ref[rs, :] = (dhc * a_ref[rs, :] * sg * (1.0 - sg)).astype(BF16)
            return carry

        lax.fori_loop(0, nchunk, step, 0)
        for k in range(kw):
            dw_ref[k:k + 1, :] = jnp.sum(acc[k], axis=0, keepdims=True)

    col = _col_spec(s, 0)
    return pl.pallas_call(
        body, name=name, grid=(nb,),
        in_specs=[_col_spec(s, 20), _col_spec(s, 24), col, pl.BlockSpec((kw, HEAD), lambda j: (0, j))],
        out_specs=[col, col, pl.BlockSpec((kw, HEAD), lambda j: (0, j))],
        out_shape=[jax.ShapeDtypeStruct((s, GROUP), BF16)] * 2 + [jax.ShapeDtypeStruct((kw, GROUP), F32)],
        scratch_shapes=[pltpu.VMEM((s + 2 * HALO, HEAD), F32), pltpu.VMEM((s + 2 * HALO, HEAD), F32),
                        pltpu.VMEM((kw, 8, HEAD), F32)],
        compiler_params=_cp(("parallel",)),
    )(z, z, dcc, w)


def _pool_window_sum(win, level, rows, shift):
    n = win.shape[0]

    def moved(v, k):
        return pltpu.roll(v, k if shift is _delayed else n - k, axis=0)

    s2 = win + moved(win, 1)
    s4 = s2 + moved(s2, 2)
    s8 = s4 + moved(s4, 4)
    s16 = s8 + moved(s8, 8)
    sel = jnp.where(level == 0, s2, jnp.where(level == 1, s4, jnp.where(level == 2, s8, s16)))
    return sel[HALO:HALO + rows]


def _pool_count(level, r0, rows):
    t = r0 + lax.broadcasted_iota(jnp.int32, (rows, 1), 0)
    width = jnp.left_shift(jnp.int32(2), level)
    return jnp.minimum(t + 1, width).astype(F32)


def _pool_fwd(name, z, pool_w, scale):
    s = z.shape[0]
    rows, nchunk = _chunks(s)

    def body(z_ref, w_ref, sc_ref, o_ref, pad):
        level = pl.program_id(0)
        _zero_halo(pad, s)

        def fill(ci, carry):
            r0 = pl.multiple_of(ci * rows, rows)
            pad[pl.ds(pl.multiple_of(HALO + r0, 8), rows), :] = z_ref[pl.ds(r0, rows), :]
            return carry

        lax.fori_loop(0, nchunk, fill, 0)
        wb = w_ref[...].astype(BF16)

        def step(ci, carry):
            r0 = pl.multiple_of(ci * rows, rows)
            win = _window(pad, r0, rows)
            pm = _pool_window_sum(win, level, rows, _delayed) / _pool_count(level, r0, rows) - win[HALO:HALO + rows]
            r = jnp.dot(pm.astype(BF16), wb, preferred_element_type=F32)
            o_ref[pl.ds(r0, rows), :] = (r * sc_ref[...]).astype(BF16)
            return carry

        lax.fori_loop(0, nchunk, step, 0)

    return pl.pallas_call(
        body, name=name, grid=(POOL_LEVELS,),
        in_specs=[_col_spec(s, 28), pl.BlockSpec((None, HEAD, HEAD), lambda j: (j, 0, 0)),
                  pl.BlockSpec((1, HEAD), lambda j: (0, j))],
        out_specs=_col_spec(s, 0),
        out_shape=jax.ShapeDtypeStruct((s, GROUP), BF16),
        scratch_shapes=[pltpu.VMEM((s + 2 * HALO, HEAD), F32)],
        compiler_params=_cp(("parallel",)),
    )(z, pool_w, scale)


def _pool_bwd(name, z, d_o, pool_w, scale):
    s = z.shape[0]
    rows, nchunk = _chunks(s)

    def body(z_ref, do_ref, w_ref, sc_ref, dz_ref, dw_ref, dsc_ref, pad, pad_q, dw_acc, dsc_acc):
        level = pl.program_id(0)
        _zero_halo(pad, s)
        _zero_halo(pad_q, s)
        dw_acc[...] = jnp.zeros_like(dw_acc)
        dsc_acc[...] = jnp.zeros_like(dsc_acc)

        def fill(ci, carry):
            r0 = pl.multiple_of(ci * rows, rows)
            pad[pl.ds(pl.multiple_of(HALO + r0, 8), rows), :] = z_ref[pl.ds(r0, rows), :]
            return carry

        lax.fori_loop(0, nchunk, fill, 0)
        wb = w_ref[...].astype(BF16)

        def first(ci, carry):
            r0 = pl.multiple_of(ci * rows, rows)
            win = _window(pad, r0, rows)
            cnt = _pool_count(level, r0, rows)
            pm = (_pool_window_sum(win, level, rows, _delayed) / cnt - win[HALO:HALO + rows]).astype(BF16)
            r = jnp.dot(pm, wb, preferred_element_type=F32)
            d_od = do_ref[pl.ds(r0, rows), :].astype(F32)
            dsc_acc[...] += _fold8(d_od * r)
            dr = (d_od * sc_ref[...]).astype(BF16)
            dw_acc[...] += lax.dot_general(pm, dr, TN, preferred_element_type=F32)
            dpm = lax.dot_general(dr, wb, NT, preferred_element_type=F32)
            pad_q[pl.ds(pl.multiple_of(HALO + r0, 8), rows), :] = dpm / cnt
            return carry

        lax.fori_loop(0, nchunk, first, 0)

        def second(ci, carry):
            r0 = pl.multiple_of(ci * rows, rows)
            wq = _window(pad_q, r0, rows)
            dpm = wq[HALO:HALO + rows] * _pool_count(level, r0, rows)
            dz_ref[pl.ds(r0, rows), :] = (_pool_window_sum(wq, level, rows, _advanced) - dpm).astype(BF16)
            return carry

        lax.fori_loop(0, nchunk, second, 0)
        dw_ref[...] = dw_acc[...]
        dsc_ref[...] = jnp.sum(dsc_acc[...], axis=0, keepdims=True)

    col = _col_spec(s, 0)
    mat = pl.BlockSpec((None, HEAD, HEAD), lambda j: (j, 0, 0))
    vec = pl.BlockSpec((1, HEAD), lambda j: (0, j))
    return pl.pallas_call(
        body, name=name, grid=(POOL_LEVELS,),
        in_specs=[_col_spec(s, 28), col, mat, vec], out_specs=[col, mat, vec],
        out_shape=[jax.ShapeDtypeStruct((s, GROUP), BF16), jax.ShapeDtypeStruct((POOL_LEVELS, HEAD, HEAD), F32),
                   jax.ShapeDtypeStruct((1, GROUP), F32)],
        scratch_shapes=[pltpu.VMEM((s + 2 * HALO, HEAD), F32), pltpu.VMEM((s + 2 * HALO, HEAD), F32),
                        pltpu.VMEM((HEAD, HEAD), F32), pltpu.VMEM((8, HEAD), F32)],
        compiler_params=_cp(("parallel",)),
    )(z, d_o, pool_w, scale)


def _ep_residual(accs, ins, outs):
    outs[0][...] = ins[2][...] + accs[0]


def _mm_residual(name, x, wg, h, tm, tn):
    m, d = h.shape
    tm = min(tm, m)

    def ep(accs, ins, outs, cols):
        outs[0][:, cols] = ins[5][:, cols] + accs[0]

    tile = pl.BlockSpec((tm, tn), lambda i, j: (i, j))
    return _mm_nn_wide(name, x, wg, "row", tm, tn, epilogue=ep, extra=[h], extra_specs=[tile], out_specs=[tile],
                       out_shape=[jax.ShapeDtypeStruct((m, d), F32)])[0]


def _swiglu_fwd(name, y, wg_gate, wg_up, tm):
    m, kdim = y.shape
    _, a, b = wg_gate.shape
    tm = min(tm, m)

    def ep(accs, ins, outs, cols):
        gt, up = accs
        outs[0][:, cols] = (gt * _sigmoid(gt) * up).astype(BF16)
        outs[1][:, cols] = gt.astype(BF16)
        outs[2][:, cols] = up.astype(BF16)

    gws, gspecs, gdots = _wide_nn_weights(wg_gate, "col", b, 1)
    uws, uspecs, udots = _wide_nn_weights(wg_up, "col", b, 1 + len(gws))
    out = pl.BlockSpec((tm, b), lambda i, j: (i, j))
    return _matmul_wide(name, (m // tm, 4), [y] + gws + uws,
                        [pl.BlockSpec((tm, kdim), lambda i, j: (i, 0))] + gspecs + uspecs, [gdots, udots], b, ep,
                        [out] * 3, [jax.ShapeDtypeStruct((m, 4 * b), BF16)] * 3)


def _swiglu_bwd(name, dh, wg_down, gate, up, tm):
    m, n = dh.shape
    _, a, b = wg_down.shape
    tm = min(tm, m)

    def ep(accs, ins, outs, cols):
        d_act = accs[0]
        gt = ins[3][:, cols].astype(F32)
        upv = ins[4][:, cols].astype(F32)
        sg = _sigmoid(gt)
        outs[0][:, cols] = (d_act * upv * sg * (1.0 + gt * (1.0 - sg))).astype(BF16)
        outs[1][:, cols] = (d_act * gt * sg).astype(BF16)
        outs[2][:, cols] = (gt * sg * upv).astype(BF16)

    ws, wspecs, dots = _wide_nt_weights(wg_down, "row", a, 1)
    tile = pl.BlockSpec((tm, a), lambda i, j: (i, j))
    return _matmul_wide(name, (m // tm, 4), [dh] + ws + [gate, up],
                        [pl.BlockSpec((tm, n), lambda i, j: (i, 0))] + wspecs + [tile, tile], [dots], a, ep,
                        [tile] * 3, [jax.ShapeDtypeStruct((m, 4 * a), BF16)] * 3)


def _ffn_dy(name, d_gate, d_up, wg_gate, wg_up, tiles):
    m, n = d_gate.shape
    _, a, b = wg_gate.shape
    kdim = 2 * a
    tm, to, tc = tiles
    tm = min(tm, m)
    grid = (m // tm, kdim // to, n // tc)
    lhs = pl.BlockSpec((tm, tc), lambda i, j, k: (i, k))
    wspec = _w_spec_nt(wg_gate, "col", to, tc)
    return _matmul(name, grid, [d_gate, d_up, wg_gate, wg_up], [lhs, lhs, wspec, wspec],
                   [(0, 2, 0, NT), (1, 3, 0, NT)], 1, (tm, to), _ep_store(BF16),
                   [pl.BlockSpec((tm, to), lambda i, j, k: (i, j))], [jax.ShapeDtypeStruct((m, kdim), BF16)])[0]


def _ple_fwd(name, y, wg, h, pp, tm, tn):
    m, d = h.shape
    tm = min(tm, m)

    def ep(accs, ins, outs, cols):
        pg = accs[0]
        outs[0][:, cols] = ins[5][:, cols] + _sigmoid(pg) * ins[6][:, cols].astype(F32)
        outs[1][:, cols] = pg.astype(BF16)

    tile = pl.BlockSpec((tm, tn), lambda i, j: (i, j))
    return _mm_nn_wide(name, y, wg, "row", tm, tn, epilogue=ep, extra=[h, pp], extra_specs=[tile, tile],
                       out_specs=[tile, tile],
                       out_shape=[jax.ShapeDtypeStruct((m, d), F32), jax.ShapeDtypeStruct((m, d), BF16)])


def _ple_bwd(name, dh, pg, pp, after=()):
    s, d = dh.shape
    tr = _row_tile(s, 512)

    def body(dh_ref, pg_ref, pp_ref, *rest):
        dpp_ref, dpg_ref = rest[-2:]
        dhv = dh_ref[...]
        sg = _sigmoid(pg_ref[...].astype(F32))
        dpp_ref[...] = (dhv * sg).astype(BF16)
        dpg_ref[...] = (dhv * pp_ref[...].astype(F32) * sg * (1.0 - sg)).astype(BF16)

    row = pl.BlockSpec((tr, d), lambda i: (i, 0))
    return pl.pallas_call(
        body, name=name, grid=(s // tr,), in_specs=[row] * 3 + [ANY_SPEC] * len(after), out_specs=[row] * 2,
        out_shape=[jax.ShapeDtypeStruct((s, d), BF16)] * 2, compiler_params=_cp(("parallel",)),
    )(dh, pg, pp, *after)


BIG = ["w_in", "w_out", "w_gate", "w_up", "w_down", "w_ple_gate", "w_ple_proj"]
KIND = {"w_in": "col", "w_out": "row", "w_gate": "col", "w_up": "col", "w_down": "row", "w_ple_gate": "row",
        "w_ple_proj": "col"}
SMALL = ["norm_mix_g", "sgu_ln_g", "sgu_ln_b", "sgu_w", "sgu_b", "sc_conv_w", "cf_conv_w", "cf_conv_b", "cf_ln_g",
         "cf_ln_b", "pool_w", "pool_scale", "norm_ffn_g", "norm_ple_g", "final_norm_g"]
CHIP_SPLIT = ["sc_conv_w", "cf_conv_w"]
WEIGHTS = ['norm_mix_g', 'w_in', 'sgu_ln_g', 'sgu_ln_b', 'sgu_w', 'sgu_b', 'sc_conv_w', 'cf_conv_w', 'cf_conv_b',
           'cf_ln_g', 'cf_ln_b', 'pool_w', 'pool_scale', 'w_out', 'norm_ffn_g', 'w_gate', 'w_up', 'w_down',
           'norm_ple_g', 'w_ple_gate', 'w_ple_proj', 'final_norm_g']


def _tile(n, want):
    if n <= want:
        return n
    t = (want // 128) * 128
    while n % t:
        t -= 128
    return t


def _pack_rows(vecs):
    flat = jnp.concatenate([v.reshape(-1) for v in vecs])
    n = flat.shape[0]
    quantum = PACK_ROWS * 128
    padded = ((n + quantum - 1) // quantum) * quantum
    return jnp.pad(flat, (0, padded - n)).reshape(padded // 128, 128), n


def _unpack(flat, shapes):
    out, off = [], 0
    for shp in shapes:
        size = math.prod(shp)
        out.append(flat[off:off + size].reshape(shp))
        off += size
    return out


def kernel(x, p, norm_mix_g, w_in, sgu_ln_g, sgu_ln_b, sgu_w, sgu_b, sc_conv_w, cf_conv_w, cf_conv_b, cf_ln_g, cf_ln_b, pool_w, pool_scale, w_out, norm_ffn_g, w_gate, w_up, w_down, norm_ple_g, w_ple_gate, w_ple_proj, final_norm_g, loss_target, m_norm_mix_g, m_w_in, m_sgu_ln_g, m_sgu_ln_b, m_sgu_w, m_sgu_b, m_sc_conv_w, m_cf_conv_w, m_cf_conv_b, m_cf_ln_g, m_cf_ln_b, m_pool_w, m_pool_scale, m_w_out, m_norm_ffn_g, m_w_gate, m_w_up, m_w_down, m_norm_ple_g, m_w_ple_gate, m_w_ple_proj, m_final_norm_g, v_norm_mix_g, v_w_in, v_sgu_ln_g, v_sgu_ln_b, v_sgu_w, v_sgu_b, v_sc_conv_w, v_cf_conv_w, v_cf_conv_b, v_cf_ln_g, v_cf_ln_b, v_pool_w, v_pool_scale, v_w_out, v_norm_ffn_g, v_w_gate, v_w_up, v_w_down, v_norm_ple_g, v_w_ple_gate, v_w_ple_proj, v_final_norm_g):
    args = dict(locals())
    w = {n: args[n] for n in WEIGHTS}
    mom = {n: args["m_" + n] for n in WEIGHTS}
    var = {n: args["v_" + n] for n in WEIGHTS}
    depth = w_in.shape[0]
    s, d = x.shape[1], x.shape[2]
    f_dim = 4 * w_gate.shape[2]
    xi, yi, ci = lax.axis_index("x"), lax.axis_index("y"), lax.axis_index("c")
    c_idx = ci.astype(jnp.int32).reshape(1)

    chip_idx = (2 * xi + yi).astype(jnp.int32).reshape(1)

    def landing(l):
        return [_cast_into_landing(f"cast_{n}", w[n], l, 0 if KIND[n] == "col" else 1, chip_idx) for n in BIG]

    conv_pack = jnp.concatenate([sc_conv_w, cf_conv_w], axis=1)
    taps = conv_pack.shape[1]
    rows_pad = ((depth * taps + 7) // 8) * 8
    conv_rows = jnp.pad(conv_pack.reshape(depth * taps, HEAD), ((0, rows_pad - depth * taps), (0, 0)))
    conv_all = _allgather8("gather_conv_weights", [conv_rows])[0]
    conv_full = conv_all[0::2, :depth * taps].reshape(4, depth, taps, HEAD)
    conv_full = jnp.transpose(conv_full, (1, 2, 0, 3)).reshape(depth, taps, GROUP)
    sc_w_full, cf_w_full = conv_full[:, :3], conv_full[:, 3:]

    gather = _gather_start("gather_start_0", landing(0), conv_all)
    h = x[0]
    saved = []
    gathered = []
    for l in range(depth):
        send_sems, recv_sems, lands, _ = gather
        lands = _gather_wait(f"gather_wait_{l}", send_sems, recv_sems, lands, h)
        lands = _gather_share(f"gather_share_{l}", lands)
        wg = dict(zip(BIG, lands))
        gathered.append(wg)
        started = ()
        if l + 1 < depth:
            gather = _gather_start(f"gather_start_{l + 1}", landing(l + 1), lands[0])
            started = (gather[3],)
        sv = {"h0": h}
        y1 = _rms_fwd("rms_mix", h, norm_mix_g[l:l + 1], after=started)
        z = _mm_nn_wide("mm_in", y1, wg["w_in"], "col", 1024, 1024)[0]
        lg, lb = sgu_ln_g[l][:, None, :], sgu_ln_b[l][:, None, :]
        sb = sgu_b[l][:, :, None]
        oa = _sgu_fwd("sgu_fwd", z, lg, lb, sgu_w[l], sb)
        ob = _shortconv_fwd("shortconv_fwd", z, sc_w_full[l])
        cc = _conformer_conv_fwd("conformer_conv_fwd", z, cf_w_full[l], cf_conv_b[l:l + 1])
        oc = _conformer_ln_fwd("conformer_ln_fwd", cc, cf_ln_g[l:l + 1], cf_ln_b[l:l + 1])
        od = _pool_fwd("pool_fwd", z, pool_w[l], pool_scale[l:l + 1])
        o = jnp.concatenate([oa, ob, oc, od], axis=1)
        tile = pl.BlockSpec((min(1024, s), 1024), lambda i, j, k: (i, j))
        h1 = _mm_residual("mm_out", o, wg["w_out"], h, 1024, 1024)
        y2 = _rms_fwd("rms_ffn", h1, norm_ffn_g[l:l + 1])
        fs = f_dim // 4
        act, gt, up = _swiglu_fwd("mm_swiglu", y2, wg["w_gate"], wg["w_up"], 512)
        h2 = _mm_nn("mm_down", act, wg["w_down"], "row", (1024, 1024, fs), epilogue=_ep_residual, extra=[h1],
                    extra_specs=[tile], out_specs=[tile], out_shape=[jax.ShapeDtypeStruct((s, d), F32)])[0]
        y3 = _rms_fwd("rms_ple", h2, norm_ple_g[l:l + 1])
        pb = p[l, 0].astype(BF16)
        ptile = pl.BlockSpec((min(1024, s), 512), lambda i, j: (i, j))
        pp = _mm_nn_wide("mm_ple_proj", pb, wg["w_ple_proj"], "col", 1024, 512, epilogue=_wide_store(BF16),
                         out_specs=[ptile], out_shape=[jax.ShapeDtypeStruct((s, d), BF16)])[0]
        h3, pg = _ple_fwd("mm_ple_gate", y3, wg["w_ple_gate"], h2, pp, 1024, 1024)
        sv.update(y1=y1, z=z, cc=cc, o=o, h1=h1, y2=y2, gt=gt, up=up, h2=h2, y3=y3, pb=pb, pp=pp, pg=pg)
        saved.append(sv)
        h = h3

    loss_part, dh, dhb, d_final_g = _loss_head("loss_head", h, final_norm_g[None, :], loss_target[0])

    small_grads = [None] * depth
    where = jnp.stack([2 * xi + yi, 2 * (1 - xi) + yi, 2 * xi + (1 - yi), 2 * (1 - xi) + (1 - yi), ci]).astype(jnp.int32)
    grad_bufs = [lax.empty((depth, 2, w[n].shape[1] // 2, w[n].shape[2]), F32) for n in BIG]
    exchange = None

    def finish_exchange(pending, layer, bufs, after):
        send_sems, recv_sems, parts, lands, _ = pending
        parts, lands = _chip_exchange_wait(f"rs_chips_wait_{layer}", send_sems, recv_sems, parts, lands, after)
        return _sum_parts("rs_sum", parts, lands, bufs, layer, where)

    for l in reversed(range(depth)):
        wg = gathered[l]
        sv = saved[l]
        fs = f_dim // 4
        started = () if exchange is None else (exchange[4],)
        d_pp, d_pg = _ple_bwd("ple_bwd", dh, sv["pg"], sv["pp"], after=started)
        g_ple_proj = _mm_tn("dw_ple_proj", sv["pb"], d_pp, "col", (w_ple_proj.shape[1], 512, 1024))
        g_ple_gate = _mm_tn("dw_ple_gate", sv["y3"], d_pg, "row", (512, 1024, 1024))
        dy3 = _mm_nt_wide("dx_ple_gate", d_pg, wg["w_ple_gate"], "row", 1024, 512)
        dh, dhb, dg_ple = _rms_bwd("rms_ple_bwd", sv["h2"], norm_ple_g[l:l + 1], dy3, dh)

        d_gt, d_up, act = _swiglu_bwd("dx_down_swiglu", dhb, wg["w_down"], sv["gt"], sv["up"], 512)
        g_down = _mm_tn("dw_down", act, dhb, "row", (fs, 1024, 1024))
        g_gate = _mm_tn("dw_gate", sv["y2"], d_gt, "col", (1024, fs, 1024))
        g_up = _mm_tn("dw_up", sv["y2"], d_up, "col", (1024, fs, 1024))
        dy2 = _ffn_dy("dx_gate_up", d_gt, d_up, wg["w_gate"], wg["w_up"], (1024, 1024, fs))
        dh, dhb, dg_ffn = _rms_bwd("rms_ffn_bwd", sv["h1"], norm_ffn_g[l:l + 1], dy2, dh)

        g_out = _mm_tn("dw_out", sv["o"], dhb, "row", (512, 1024, 1024))
        d_o = _mm_nt_wide("dx_out", dhb, wg["w_out"], "row", 1024, 512)
        z = sv["z"]
        lg, lb = sgu_ln_g[l][:, None, :], sgu_ln_b[l][:, None, :]
        sb = sgu_b[l][:, :, None]
        dzu, dzv, d_lg, d_lb, d_sw, d_sb = _sgu_bwd("sgu_bwd", z, d_o[:, 0:GROUP], lg, lb, sgu_w[l], sb)
        dzh, dzbg, dzcg, d_scw = _shortconv_bwd("shortconv_bwd", z, d_o[:, GROUP:2 * GROUP], sc_w_full[l])
        dcc, d_cflg, d_cflb, d_cfb = _conformer_ln_bwd("conformer_ln_bwd", sv["cc"], d_o[:, 2 * GROUP:3 * GROUP],
                                                       cf_ln_g[l:l + 1], cf_ln_b[l:l + 1])
        dza, dzg, d_cfw = _conformer_conv_bwd("conformer_conv_bwd", z, dcc, cf_w_full[l])
        dzd, d_pw, d_psc = _pool_bwd("pool_bwd", z, d_o[:, 3 * GROUP:], pool_w[l], pool_scale[l:l + 1])
        dz = jnp.concatenate([dzu, dzv, dzh, dzbg, dzcg, dza, dzg, dzd], axis=1)
        g_in = _mm_tn("dw_in", sv["y1"], dz, "col", (1024, 1024, 1024))
        dy1 = _mm_nt_wide("dx_in", dz, wg["w_in"], "col", 1024, 512)
        dh, dhb, dg_mix = _rms_bwd("rms_mix_bwd", sv["h0"], norm_mix_g[l:l + 1], dy1, dh)

        small_grads[l] = dict(norm_mix_g=dg_mix, sgu_ln_g=d_lg, sgu_ln_b=d_lb, sgu_w=d_sw, sgu_b=d_sb,
                              sc_conv_w=d_scw, cf_conv_w=d_cfw, cf_conv_b=d_cfb, cf_ln_g=d_cflg, cf_ln_b=d_cflb,
                              pool_w=d_pw, pool_scale=d_psc, norm_ffn_g=dg_ffn, norm_ple_g=dg_ple)
        big = dict(w_in=g_in, w_out=g_out, w_gate=g_gate, w_up=g_up, w_down=g_down, w_ple_gate=g_ple_gate,
                   w_ple_proj=g_ple_proj)
        if exchange is not None:
            grad_bufs = finish_exchange(exchange, l + 1, grad_bufs, dh)
        gs = [big[n].reshape(4, 2, big[n].shape[1] // 2, big[n].shape[2]) for n in BIG]
        from_sibling = _sibling_send_other_halves(f"rs_sibling_{l}", gs)
        chip_sums = _add_halves("rs_add", gs, from_sibling, c_idx)
        exchange = _chip_exchange_start(f"rs_chips_start_{l}", chip_sums)
    grad_x = dh[None]
    grad_bufs = finish_exchange(exchange, 0, grad_bufs, dh)
    joined = _sibling_join_halves("rs_join", grad_bufs)
    grads = {n: joined[a].reshape(w[n].shape) for a, n in enumerate(BIG)}

    per_layer = [n for n in SMALL if n != "final_norm_g"]
    vecs = [small_grads[l][n] for l in range(depth) for n in per_layer] + [d_final_g, loss_part]
    packed, _ = _pack_rows(vecs)
    total = _sum8("sum_small", _allgather8("gather_small", [packed])[0]).reshape(-1)
    full_shapes = {n: (w[n].shape[1:] if n not in CHIP_SPLIT else (w[n].shape[1], GROUP)) for n in per_layer}
    pieces = _unpack(total, [full_shapes[n] for _ in range(depth) for n in per_layer] + [(d,), ()])
    loss = pieces[-1]
    grads["final_norm_g"] = pieces[-2]
    chip_off = (2 * xi + yi) * HEAD
    for j, n in enumerate(per_layer):
        g = jnp.stack([pieces[l * len(per_layer) + j] for l in range(depth)])
        if n in CHIP_SPLIT:
            g = lax.dynamic_slice_in_dim(g, chip_off, HEAD, axis=2)
        grads[n] = g

    delta, new_m, new_v = {}, {}, {}
    for n in BIG:
        shp = w[n].shape
        two_d = (shp[0] * shp[1], shp[2])
        dl, mn, vn = _adamw(f"adamw_{n}", w[n].reshape(two_d), grads[n].reshape(two_d), mom[n].reshape(two_d),
                            var[n].reshape(two_d))
        delta[n], new_m[n], new_v[n] = dl.reshape(shp), mn.reshape(shp), vn.reshape(shp)
    small_shapes = [w[n].shape for n in SMALL]
    pw, _ = _pack_rows([w[n] for n in SMALL])
    pg_, _ = _pack_rows([grads[n] for n in SMALL])
    pm, _ = _pack_rows([mom[n] for n in SMALL])
    pv, _ = _pack_rows([var[n] for n in SMALL])
    dl, mn, vn = _adamw("adamw_small", pw, pg_, pm, pv)
    for n, a, b, cc_ in zip(SMALL, _unpack(dl.reshape(-1), small_shapes), _unpack(mn.reshape(-1), small_shapes),
                            _unpack(vn.reshape(-1), small_shapes)):
        delta[n], new_m[n], new_v[n] = a, b, cc_

    return (loss, grad_x, *[grads[n] for n in WEIGHTS], *[delta[n] for n in WEIGHTS],
            *[new_m[n] for n in WEIGHTS], *[new_v[n] for n in WEIGHTS])
```

```python
import functools
import math

import jax
import jax.numpy as jnp
from jax import lax
from jax.experimental import pallas as pl
from jax.experimental.pallas import tpu as pltpu

F32 = jnp.float32
BF16 = jnp.bfloat16
MESH = pl.DeviceIdType.MESH

HEAD = 128
GROUP = 512
EPS = 1e-6
HALO = 32
CHUNK_ROWS = 128
POOL_LEVELS = 4
PACK_ROWS = 512

ADAM_LR = 0.001
ADAM_B1 = 0.9
ADAM_B2 = 0.999
ADAM_EPS = 1e-08
ADAM_WD = 0.01
ADAM_STEP = 10

VMEM_LIMIT = 56 * 1024 * 1024


def _cp(sem=None, vmem=VMEM_LIMIT):
    return pltpu.CompilerParams(dimension_semantics=sem, vmem_limit_bytes=vmem)


def _sigmoid(x):
    return 0.5 * jnp.tanh(0.5 * x) + 0.5


_GELU_K = math.sqrt(2.0 / math.pi)
_GELU_C = 0.044715


def _gelu(x):
    t = jnp.tanh(_GELU_K * (x + _GELU_C * x * x * x))
    return 0.5 * x * (1.0 + t)


def _gelu_grad(x):
    t = jnp.tanh(_GELU_K * (x + _GELU_C * x * x * x))
    return 0.5 * (1.0 + t) + 0.5 * x * (1.0 - t * t) * _GELU_K * (1.0 + 3.0 * _GELU_C * x * x)


def _mesh_pos():
    return lax.axis_index("x"), lax.axis_index("y"), lax.axis_index("c")


def _any_specs(n):
    return [pl.BlockSpec(memory_space=pl.ANY)] * n


def _allgather8(name, blocks):
    n = len(blocks)

    def body(*refs):
        ins, outs = refs[:n], refs[n:2 * n]
        send_sems, recv_sems, local_sems = refs[2 * n:]
        x, y, c = _mesh_pos()
        me, sibling = (x, y, c), (x, y, 1 - c)
        chips = [(1 - x, y), (x, 1 - y), (1 - x, 1 - y)]

        def slot(a, dev):
            return outs[a].at[4 * dev[0] + 2 * dev[1] + dev[2]]

        def copy(a, k, block, to, src=None):
            dst = slot(a, block)
            return pltpu.make_async_remote_copy(
                src_ref=dst if src is None else src, dst_ref=dst,
                send_sem=send_sems.at[7 * a + k], recv_sem=recv_sems.at[7 * a + k],
                device_id=to, device_id_type=MESH)

        mine, first, passed = [], [], []
        for a in range(n):
            cp = pltpu.make_async_copy(ins[a], slot(a, me), local_sems.at[a])
            cp.start()
            mine.append(cp)
            cps = [copy(a, 0, me, sibling, src=ins[a])]
            cps += [copy(a, 1 + j, me, (*chip, c), src=ins[a]) for j, chip in enumerate(chips)]
            for cp in cps:
                cp.start()
            first += cps
        for j, chip in enumerate(chips):
            for a in range(n):
                copy(a, 1 + j, (*chip, c), me).wait_recv()
                cp = copy(a, 4 + j, (*chip, c), sibling)
                cp.start()
                passed.append(cp)
        for a in range(n):
            copy(a, 0, sibling, me).wait_recv()
            for j, chip in enumerate(chips):
                copy(a, 4 + j, (*chip, 1 - c), me).wait_recv()
        for cp in first + passed:
            cp.wait_send()
        for cp in mine:
            cp.wait()

    return pl.pallas_call(
        body, name=name,
        out_shape=[jax.ShapeDtypeStruct((8,) + b.shape, b.dtype) for b in blocks],
        in_specs=_any_specs(n), out_specs=_any_specs(n),
        scratch_shapes=[pltpu.SemaphoreType.DMA((7 * n,)), pltpu.SemaphoreType.DMA((7 * n,)),
                        pltpu.SemaphoreType.DMA((n,))],
    )(*blocks)


HBM_SPEC = pl.BlockSpec(memory_space=pltpu.HBM)
SEM_SPEC = pl.BlockSpec(memory_space=pltpu.SEMAPHORE)
ANY_SPEC = pl.BlockSpec(memory_space=pl.ANY)
SPLIT_COPY = pltpu.CompilerParams(has_side_effects=pltpu.SideEffectType.DATAFLOW_SIDE_EFFECTING)


def _hbm(x):
    return pltpu.with_memory_space_constraint(x, pltpu.HBM)


def _other_chips(x, y):
    return [(1 - x, y), (x, 1 - y), (1 - x, 1 - y)]


def _dev_slot(ref, dev):
    return ref.at[4 * dev[0] + 2 * dev[1] + dev[2]]


def _cast_into_landing(name, w, layer, ax, chip_idx):
    _, r, cc = w.shape
    r2, c2 = (r // 2, cc) if ax == 0 else (r, cc // 2)
    tr = _row_tile(r2, 512 if c2 <= 1536 else 256)
    nt = r2 // tr

    def body(chip_ref, w_ref, o_ref):
        o_ref[...] = w_ref[...].astype(BF16)

    if ax == 0:
        in_spec = pl.BlockSpec((None, tr, c2), lambda hf, i, chip_ref: (layer, hf * nt + i, 0))
    else:
        in_spec = pl.BlockSpec((None, tr, c2), lambda hf, i, chip_ref: (layer, i, hf))
    return pl.pallas_call(
        body, name=name,
        grid_spec=pltpu.PrefetchScalarGridSpec(
            num_scalar_prefetch=1, grid=(2, nt), in_specs=[in_spec],
            out_specs=pl.BlockSpec((None, tr, c2), lambda hf, i, chip_ref: (2 * chip_ref[0] + hf, i, 0))),
        out_shape=jax.ShapeDtypeStruct((8, r2, c2), BF16), compiler_params=_cp(("parallel", "parallel")),
    )(chip_idx, w)


def _gather_ici_copies(lands, send_sems, recv_sems):
    x, y, c = _mesh_pos()
    pairs = []
    for a in range(len(lands)):
        for j, chip in enumerate(_other_chips(x, y)):
            def copy(dev):
                return pltpu.make_async_remote_copy(
                    src_ref=_dev_slot(lands[a], dev), dst_ref=_dev_slot(lands[a], dev),
                    send_sem=send_sems.at[3 * a + j], recv_sem=recv_sems.at[3 * a + j],
                    device_id=(*chip, c), device_id_type=MESH)
            pairs.append((copy((x, y, c)), copy((*chip, c))))
    return pairs


def _exchange_ici_copies(ins, lands, send_sems, recv_sems):
    x, y, c = _mesh_pos()
    pairs = []
    for a in range(len(ins)):
        for k, chip in enumerate(_other_chips(x, y)):
            there = 2 * chip[0] + chip[1]
            def copy(dst_entry):
                return pltpu.make_async_remote_copy(
                    src_ref=ins[a].at[there], dst_ref=lands[a].at[dst_entry],
                    send_sem=send_sems.at[3 * a + k], recv_sem=recv_sems.at[3 * a + k],
                    device_id=(*chip, c), device_id_type=MESH)
            pairs.append((copy(2 * x + y), copy(there)))
    return pairs


def _gather_start(name, lands, dep):
    n = len(lands)

    def body(*refs):
        token = refs[-1]
        for send, _ in _gather_ici_copies(refs[:n], refs[n + 1], refs[n + 2]):
            send.start()
        token[...] = jnp.zeros_like(token)

    out = pl.pallas_call(
        body, name=name,
        out_shape=(pltpu.SemaphoreType.DMA((3 * n,)), pltpu.SemaphoreType.DMA((3 * n,)),
                   *[pltpu.HBM(l.shape, l.dtype) for l in lands], jax.ShapeDtypeStruct((8, 128), F32)),
        in_specs=[HBM_SPEC] * n + [ANY_SPEC],
        out_specs=(SEM_SPEC, SEM_SPEC, *[HBM_SPEC] * n, pl.BlockSpec(memory_space=pltpu.VMEM)),
        input_output_aliases={a: 2 + a for a in range(n)},
        compiler_params=SPLIT_COPY,
    )(*[_hbm(l) for l in lands], dep)
    return out[0], out[1], list(out[2:2 + n]), out[-1]


def _gather_wait(name, send_sems, recv_sems, lands, after):
    n = len(lands)

    def body(*refs):
        for send, recv in _gather_ici_copies(refs[:n], refs[n], refs[n + 1]):
            send.wait_send()
            recv.wait_recv()

    out = pl.pallas_call(
        body, name=name,
        out_shape=[pltpu.HBM(l.shape, l.dtype) for l in lands],
        in_specs=[HBM_SPEC] * n + [SEM_SPEC, SEM_SPEC, ANY_SPEC],
        out_specs=[HBM_SPEC] * n,
        input_output_aliases={a: a for a in range(n)},
        compiler_params=SPLIT_COPY,
    )(*lands, send_sems, recv_sems, after)
    return list(out)


def _gather_share(name, lands):
    n = len(lands)

    def body(*refs):
        outs = refs[n:2 * n]
        send_sems, recv_sems = refs[2 * n:]
        x, y, c = _mesh_pos()

        def copy(a, j, dev):
            slot = _dev_slot(outs[a], dev)
            return pltpu.make_async_remote_copy(
                src_ref=slot, dst_ref=slot, send_sem=send_sems.at[3 * a + j], recv_sem=recv_sems.at[3 * a + j],
                device_id=(x, y, 1 - c), device_id_type=MESH)

        chips = _other_chips(x, y)
        for a in range(n):
            for j, chip in enumerate(chips):
                copy(a, j, (*chip, c)).start()
        for a in range(n):
            for j, chip in enumerate(chips):
                copy(a, j, (*chip, c)).wait_send()
                copy(a, j, (*chip, 1 - c)).wait_recv()

    return pl.pallas_call(
        body, name=name,
        out_shape=[jax.ShapeDtypeStruct(l.shape, l.dtype) for l in lands],
        in_specs=_any_specs(n), out_specs=_any_specs(n),
        input_output_aliases={a: a for a in range(n)},
        scratch_shapes=[pltpu.SemaphoreType.DMA((3 * n,)), pltpu.SemaphoreType.DMA((3 * n,))],
    )(*lands)


def _sibling_copies(ins, lands, send_sems, recv_sems):
    x, y, c = _mesh_pos()
    return [pltpu.make_async_remote_copy(
        src_ref=ins[a].at[j, 1 - c], dst_ref=lands[a].at[j],
        send_sem=send_sems.at[4 * a + j], recv_sem=recv_sems.at[4 * a + j],
        device_id=(x, y, 1 - c), device_id_type=MESH) for a in range(len(ins)) for j in range(4)]


def _sibling_start(name, grads):
    n = len(grads)

    def body(*refs):
        token = refs[-1]
        for cp in _sibling_copies(refs[:n], refs[n:2 * n], refs[2 * n], refs[2 * n + 1]):
            cp.start()
        token[...] = jnp.zeros_like(token)

    lands = [_hbm(lax.empty((4,) + g.shape[2:], g.dtype)) for g in grads]
    out = pl.pallas_call(
        body, name=name,
        out_shape=(pltpu.SemaphoreType.DMA((4 * n,)), pltpu.SemaphoreType.DMA((4 * n,)),
                   *[pltpu.HBM(g.shape, g.dtype) for g in grads], *[pltpu.HBM(l.shape, l.dtype) for l in lands],
                   jax.ShapeDtypeStruct((8, 128), F32)),
        in_specs=[HBM_SPEC] * (2 * n),
        out_specs=(SEM_SPEC, SEM_SPEC, *[HBM_SPEC] * (2 * n), pl.BlockSpec(memory_space=pltpu.VMEM)),
        input_output_aliases={i: 2 + i for i in range(2 * n)},
        compiler_params=SPLIT_COPY,
    )(*[_hbm(g) for g in grads], *lands)
    return out[0], out[1], list(out[2:2 + n]), list(out[2 + n:2 + 2 * n]), out[-1]


def _sibling_wait(name, send_sems, recv_sems, grads, lands, after):
    n = len(grads)

    def body(*refs):
        for cp in _sibling_copies(refs[:n], refs[n:2 * n], refs[2 * n], refs[2 * n + 1]):
            cp.wait_send()
            cp.wait_recv()

    out = pl.pallas_call(
        body, name=name,
        out_shape=[pltpu.HBM(g.shape, g.dtype) for g in grads] + [pltpu.HBM(l.shape, l.dtype) for l in lands],
        in_specs=[HBM_SPEC] * (2 * n) + [SEM_SPEC, SEM_SPEC, ANY_SPEC],
        out_specs=[HBM_SPEC] * (2 * n),
        input_output_aliases={i: i for i in range(2 * n)},
        compiler_params=SPLIT_COPY,
    )(*grads, *lands, send_sems, recv_sems, after)
    return list(out[:n]), list(out[n:])


def _chip_exchange_start(name, parts):
    n = len(parts)

    def body(*refs):
        ins, lands = refs[:n], refs[n:2 * n]
        token = refs[-1]
        for send, _ in _exchange_ici_copies(ins, lands, refs[2 * n], refs[2 * n + 1]):
            send.start()
        token[...] = jnp.zeros_like(token)

    lands = [_hbm(lax.empty(p.shape, p.dtype)) for p in parts]
    out = pl.pallas_call(
        body, name=name,
        out_shape=(pltpu.SemaphoreType.DMA((3 * n,)), pltpu.SemaphoreType.DMA((3 * n,)),
                   *[pltpu.HBM(p.shape, p.dtype) for p in parts], *[pltpu.HBM(p.shape, p.dtype) for p in parts],
                   jax.ShapeDtypeStruct((8, 128), F32)),
        in_specs=[HBM_SPEC] * (2 * n),
        out_specs=(SEM_SPEC, SEM_SPEC, *[HBM_SPEC] * (2 * n), pl.BlockSpec(memory_space=pltpu.VMEM)),
        input_output_aliases={i: 2 + i for i in range(2 * n)},
        compiler_params=SPLIT_COPY,
    )(*[_hbm(p) for p in parts], *lands)
    return out[0], out[1], list(out[2:2 + n]), list(out[2 + n:2 + 2 * n]), out[-1]


def _chip_exchange_wait(name, send_sems, recv_sems, parts, lands, after):
    n = len(parts)

    def body(*refs):
        ins, lands_in = refs[:n], refs[n:2 * n]
        for send, recv in _exchange_ici_copies(ins, lands_in, refs[2 * n], refs[2 * n + 1]):
            send.wait_send()
            recv.wait_recv()

    out = pl.pallas_call(
        body, name=name,
        out_shape=[pltpu.HBM(p.shape, p.dtype) for p in parts] * 2,
        in_specs=[HBM_SPEC] * (2 * n) + [SEM_SPEC, SEM_SPEC, ANY_SPEC],
        out_specs=[HBM_SPEC] * (2 * n),
        input_output_aliases={i: i for i in range(2 * n)},
        compiler_params=SPLIT_COPY,
    )(*parts, *lands, send_sems, recv_sems, after)
    return list(out[:n]), list(out[n:])


def _sibling_join_halves(name, bufs):
    n = len(bufs)
    depth = bufs[0].shape[0]

    def body(*refs):
        outs = refs[n:2 * n]
        send_sems, recv_sems = refs[2 * n:]
        x, y, c = _mesh_pos()

        def copy(a, l, half):
            return pltpu.make_async_remote_copy(
                src_ref=outs[a].at[l, half], dst_ref=outs[a].at[l, half],
                send_sem=send_sems.at[depth * a + l], recv_sem=recv_sems.at[depth * a + l],
                device_id=(x, y, 1 - c), device_id_type=MESH)

        for a in range(n):
            for l in range(depth):
                copy(a, l, c).start()
        for a in range(n):
            for l in range(depth):
                copy(a, l, c).wait_send()
                copy(a, l, 1 - c).wait_recv()

    return pl.pallas_call(
        body, name=name, out_shape=[jax.ShapeDtypeStruct(b.shape, b.dtype) for b in bufs],
        in_specs=_any_specs(n), out_specs=_any_specs(n),
        input_output_aliases={a: a for a in range(n)},
        scratch_shapes=[pltpu.SemaphoreType.DMA((depth * n,)), pltpu.SemaphoreType.DMA((depth * n,))],
    )(*bufs)


def _row_tile(rows, want):
    t = 1
    while t * 2 <= min(rows, want):
        t *= 2
    while rows % t:
        t //= 2
    return t


def _add_halves(name, grads, recvd, c_idx):
    outs = []
    for a, (g, r) in enumerate(zip(grads, recvd)):
        _, _, r2, cc = g.shape
        tr = _row_tile(r2, 512)

        def body(c_ref, g_ref, r_ref, o_ref):
            o_ref[...] = (g_ref[...].astype(F32) + r_ref[...].astype(F32)).astype(BF16)

        outs.append(pl.pallas_call(
            body, name=f"{name}_{a}",
            grid_spec=pltpu.PrefetchScalarGridSpec(
                num_scalar_prefetch=1, grid=(4, r2 // tr),
                in_specs=[pl.BlockSpec((None, None, tr, cc), lambda j, i, c_ref: (j, c_ref[0], i, 0)),
                          pl.BlockSpec((None, tr, cc), lambda j, i, c_ref: (j, i, 0))],
                out_specs=pl.BlockSpec((None, tr, cc), lambda j, i, c_ref: (j, i, 0))),
            out_shape=jax.ShapeDtypeStruct(r.shape, BF16), compiler_params=_cp(("parallel", "parallel")),
        )(c_idx, g, r))
    return outs


def _sum_parts(name, parts, recvd, bufs, layer, where):
    outs = []
    for a, (p, r, buf) in enumerate(zip(parts, recvd, bufs)):
        _, r2, cc = p.shape
        tr = _row_tile(r2, 256)

        def body(where_ref, own_ref, r1_ref, r2_ref, r3_ref, buf_ref, o_ref):
            acc = own_ref[...].astype(F32)
            for ref in (r1_ref, r2_ref, r3_ref):
                acc = acc + ref[...].astype(F32)
            o_ref[...] = acc

        def entry(k):
            return pl.BlockSpec((None, tr, cc), lambda i, where_ref: (where_ref[k], i, 0))

        outs.append(pl.pallas_call(
            body, name=f"{name}_{a}",
            grid_spec=pltpu.PrefetchScalarGridSpec(
                num_scalar_prefetch=1, grid=(r2 // tr,),
                in_specs=[entry(0), entry(1), entry(2), entry(3), ANY_SPEC],
                out_specs=pl.BlockSpec((None, None, tr, cc), lambda i, where_ref: (layer, where_ref[4], i, 0))),
            out_shape=jax.ShapeDtypeStruct(buf.shape, F32), input_output_aliases={5: 0},
            compiler_params=_cp(("parallel",)),
        )(where, p, r, r, r, buf))
    return outs


def _sum8(name, gathered):
    _, r, cc = gathered.shape
    tr = _row_tile(r, 512)

    def body(g_ref, o_ref):
        acc = g_ref[0]
        for i in range(1, 8):
            acc = acc + g_ref[i]
        o_ref[...] = acc

    return pl.pallas_call(
        body, name=name, grid=(r // tr,),
        in_specs=[pl.BlockSpec((8, tr, cc), lambda i: (0, i, 0))],
        out_specs=pl.BlockSpec((tr, cc), lambda i: (i, 0)),
        out_shape=jax.ShapeDtypeStruct((r, cc), F32), compiler_params=_cp(("parallel",)),
    )(gathered)


def _adamw(name, w, g, m, v):
    r, cc = w.shape
    tr = _row_tile(r, max(8, (256 * 1024) // cc))

    def body(w_ref, g_ref, m_ref, v_ref, d_ref, mo_ref, vo_ref):
        gg = g_ref[...]
        mn = ADAM_B1 * m_ref[...] + (1.0 - ADAM_B1) * gg
        vn = ADAM_B2 * v_ref[...] + (1.0 - ADAM_B2) * (gg * gg)
        m_hat = mn / (1.0 - ADAM_B1 ** ADAM_STEP)
        v_hat = vn / (1.0 - ADAM_B2 ** ADAM_STEP)
        d_ref[...] = -ADAM_LR * (m_hat / (jnp.sqrt(v_hat) + ADAM_EPS) + ADAM_WD * w_ref[...])
        mo_ref[...] = mn
        vo_ref[...] = vn

    spec = pl.BlockSpec((tr, cc), lambda i: (i, 0))
    return pl.pallas_call(
        body, name=name, grid=(r // tr,), in_specs=[spec] * 4, out_specs=[spec] * 3,
        out_shape=[jax.ShapeDtypeStruct(w.shape, F32)] * 3, compiler_params=_cp(("parallel",)),
    )(w, g, m, v)


def _rms_fwd(name, h, g, after=()):
    s, d = h.shape
    tr = _row_tile(s, 512)

    def body(h_ref, g_ref, *rest):
        y_ref = rest[-1]
        xv = h_ref[...]
        r = lax.rsqrt(jnp.mean(xv * xv, axis=-1, keepdims=True) + EPS)
        y_ref[...] = (xv * r * g_ref[...]).astype(BF16)

    return pl.pallas_call(
        body, name=name, grid=(s // tr,),
        in_specs=[pl.BlockSpec((tr, d), lambda i: (i, 0)), pl.BlockSpec((1, d), lambda i: (0, 0))]
        + [ANY_SPEC] * len(after),
        out_specs=pl.BlockSpec((tr, d), lambda i: (i, 0)),
        out_shape=jax.ShapeDtypeStruct((s, d), BF16), compiler_params=_cp(("parallel",)),
    )(h, g, *after)


def _rms_bwd_rows(xv, gv, dy):
    d = xv.shape[-1]
    r = lax.rsqrt(jnp.mean(xv * xv, axis=-1, keepdims=True) + EPS)
    dxn = dy * gv
    proj = jnp.sum(dxn * xv, axis=-1, keepdims=True) * (1.0 / d)
    dx = r * dxn - xv * (r * r * r) * proj
    return dx, dy * xv * r


def _rms_bwd(name, h, g, dy, dres):
    s, d = h.shape
    tr = _row_tile(s, 256)

    def body(h_ref, g_ref, dy_ref, dres_ref, dh_ref, dhb_ref, dg_ref):
        dx, dgp = _rms_bwd_rows(h_ref[...], g_ref[...], dy_ref[...].astype(F32))
        dh = dres_ref[...] + dx
        dh_ref[...] = dh
        dhb_ref[...] = dh.astype(BF16)

        @pl.when(pl.program_id(0) == 0)
        def _():
            dg_ref[...] = jnp.zeros_like(dg_ref)

        dg_ref[...] += jnp.sum(dgp, axis=0, keepdims=True)

    row = pl.BlockSpec((tr, d), lambda i: (i, 0))
    vec = pl.BlockSpec((1, d), lambda i: (0, 0))
    return pl.pallas_call(
        body, name=name, grid=(s // tr,), in_specs=[row, vec, row, row], out_specs=[row, row, vec],
        out_shape=[jax.ShapeDtypeStruct((s, d), F32), jax.ShapeDtypeStruct((s, d), BF16),
                   jax.ShapeDtypeStruct((1, d), F32)],
        compiler_params=_cp(("arbitrary",)),
    )(h, g, dy, dres)


def _loss_head(name, h, g, target):
    s, d = h.shape
    tr = _row_tile(s, 256)

    def body(h_ref, g_ref, t_ref, loss_ref, dh_ref, dhb_ref, dg_ref):
        xv = h_ref[...]
        gv = g_ref[...]
        r = lax.rsqrt(jnp.mean(xv * xv, axis=-1, keepdims=True) + EPS)
        err = xv * r * gv - t_ref[...]
        part = 0.5 * jnp.sum(jnp.sum(err * err, axis=-1, keepdims=True) * (1.0 / d), axis=0, keepdims=True)
        dx, dgp = _rms_bwd_rows(xv, gv, err * (1.0 / d))
        dh_ref[...] = dx
        dhb_ref[...] = dx.astype(BF16)

        @pl.when(pl.program_id(0) == 0)
        def _():
            dg_ref[...] = jnp.zeros_like(dg_ref)
            loss_ref[...] = jnp.zeros_like(loss_ref)

        dg_ref[...] += jnp.sum(dgp, axis=0, keepdims=True)
        loss_ref[...] += part

    row = pl.BlockSpec((tr, d), lambda i: (i, 0))
    vec = pl.BlockSpec((1, d), lambda i: (0, 0))
    one = pl.BlockSpec((1, 1), lambda i: (0, 0))
    return pl.pallas_call(
        body, name=name, grid=(s // tr,), in_specs=[row, vec, row], out_specs=[one, row, row, vec],
        out_shape=[jax.ShapeDtypeStruct((1, 1), F32), jax.ShapeDtypeStruct((s, d), F32),
                   jax.ShapeDtypeStruct((s, d), BF16), jax.ShapeDtypeStruct((1, d), F32)],
        compiler_params=_cp(("arbitrary",)),
    )(h, g, target)


NN = (((1,), (0,)), ((), ()))
NT = (((1,), (1,)), ((), ()))
TN = (((0,), (0,)), ((), ()))


def _matmul(name, grid, operands, in_specs, pairs, n_acc, acc_shape, epilogue, out_specs, out_shape, after=()):
    operands = list(operands) + list(after)
    in_specs = list(in_specs) + [ANY_SPEC] * len(after)
    n_in = len(operands)
    n_out = len(out_shape)
    nk = grid[2]

    def body(*refs):
        ins, outs, accs = refs[:n_in], refs[n_in:n_in + n_out], refs[n_in + n_out:]
        k = pl.program_id(2)

        @pl.when(k == 0)
        def _():
            for acc in accs:
                acc[...] = jnp.zeros_like(acc)

        for li, ri, ai, dims in pairs:
            accs[ai][...] += lax.dot_general(ins[li][...], ins[ri][...], dims, preferred_element_type=F32)

        @pl.when(k == nk - 1)
        def _():
            epilogue([acc[...] for acc in accs], ins, outs)

    return pl.pallas_call(
        body, name=name, grid=grid, in_specs=in_specs, out_specs=out_specs, out_shape=out_shape,
        scratch_shapes=[pltpu.VMEM(acc_shape, F32)] * n_acc,
        compiler_params=_cp(("parallel", "parallel", "arbitrary")),
    )(*operands)


def _ep_store(dtype):
    def ep(accs, ins, outs):
        outs[0][...] = accs[0].astype(dtype)
    return ep


def _w_spec_nn(wg, kind, tk, tn):
    _, a, b = wg.shape
    if kind == "col":
        kph, bps = a // tk, b // tn
        return pl.BlockSpec((None, tk, tn), lambda i, j, k: (2 * (j // bps) + k // kph, k % kph, j % bps))
    kps, nph = a // tk, b // tn
    return pl.BlockSpec((None, tk, tn), lambda i, j, k: (2 * (k // kps) + j // nph, k % kps, j % nph))


def _w_spec_nt(wg, kind, to, tc):
    _, a, b = wg.shape
    if kind == "col":
        oph, cps = a // to, b // tc
        return pl.BlockSpec((None, to, tc), lambda i, jo, kc: (2 * (kc // cps) + jo // oph, jo % oph, kc % cps))
    ops, cph = a // to, b // tc
    return pl.BlockSpec((None, to, tc), lambda i, jo, kc: (2 * (jo // ops) + kc // cph, jo % ops, kc % cph))


def _mm_nn(name, x, wg, kind, tiles, epilogue=None, extra=(), extra_specs=(), out_specs=None, out_shape=None):
    m, kdim = x.shape
    _, a, b = wg.shape
    n = 4 * b if kind == "col" else 2 * b
    tm, tn, tk = tiles
    tm = min(tm, m)
    grid = (m // tm, n // tn, kdim // tk)
    in_specs = [pl.BlockSpec((tm, tk), lambda i, j, k: (i, k)), _w_spec_nn(wg, kind, tk, tn)] + list(extra_specs)
    if out_shape is None:
        out_shape = [jax.ShapeDtypeStruct((m, n), F32)]
        out_specs = [pl.BlockSpec((tm, tn), lambda i, j, k: (i, j))]
        epilogue = _ep_store(F32)
    return _matmul(name, grid, [x, wg] + list(extra), in_specs, [(0, 1, 0, NN)], 1, (tm, tn), epilogue,
                   out_specs, out_shape)


SUB_COLS = 256


def _matmul_wide(name, grid, operands, in_specs, groups, tn, epilogue, out_specs, out_shape, after=()):
    operands = list(operands) + list(after)
    in_specs = list(in_specs) + [ANY_SPEC] * len(after)
    n_in = len(operands)

    def body(*refs):
        ins, outs = refs[:n_in], refs[n_in:]
        for s0 in range(0, tn, SUB_COLS):
            cols = slice(s0, min(s0 + SUB_COLS, tn))
            accs = []
            for group in groups:
                acc = None
                for li, (c0, cw), ri, dims in group:
                    rhs = ins[ri][:, cols] if dims == NN else ins[ri][cols, :]
                    part = lax.dot_general(ins[li][:, c0:c0 + cw], rhs, dims, preferred_element_type=F32)
                    acc = part if acc is None else acc + part
                accs.append(acc)
            epilogue(accs, ins, outs, cols)

    return pl.pallas_call(
        body, name=name, grid=grid, in_specs=in_specs, out_specs=out_specs, out_shape=out_shape,
        compiler_params=_cp(("parallel", "parallel")),
    )(*operands)


def _wide_store(dtype):
    def ep(accs, ins, outs, cols):
        outs[0][:, cols] = accs[0].astype(dtype)
    return ep


def _wide_nn_weights(wg, kind, tn, first):
    _, a, b = wg.shape
    per = b // tn
    if kind == "col":
        specs = [pl.BlockSpec((None, a, tn), lambda i, j, hf=hf: (2 * (j // per) + hf, 0, j % per)) for hf in range(2)]
    else:
        specs = [pl.BlockSpec((None, a, tn), lambda i, j, ch=ch: (2 * ch + j // per, 0, j % per)) for ch in range(4)]
    dots = [(0, (q * a, a), first + q, NN) for q in range(len(specs))]
    return [wg] * len(specs), specs, dots


def _wide_nt_weights(wg, kind, to, first):
    _, a, b = wg.shape
    per = a // to
    if kind == "col":
        specs = [pl.BlockSpec((None, to, b), lambda i, j, ch=ch: (2 * ch + j // per, j % per, 0)) for ch in range(4)]
    else:
        specs = [pl.BlockSpec((None, to, b), lambda i, j, hf=hf: (2 * (j // per) + hf, j % per, 0)) for hf in range(2)]
    dots = [(0, (q * b, b), first + q, NT) for q in range(len(specs))]
    return [wg] * len(specs), specs, dots


def _mm_nn_wide(name, x, wg, kind, tm, tn, epilogue=None, extra=(), extra_specs=(), out_specs=None, out_shape=None):
    m, kdim = x.shape
    _, a, b = wg.shape
    n = 4 * b if kind == "col" else 2 * b
    tm = min(tm, m)
    ws, wspecs, dots = _wide_nn_weights(wg, kind, tn, 1)
    if out_shape is None:
        out_shape = [jax.ShapeDtypeStruct((m, n), F32)]
        out_specs = [pl.BlockSpec((tm, tn), lambda i, j: (i, j))]
        epilogue = _wide_store(F32)
    return _matmul_wide(name, (m // tm, n // tn), [x] + ws + list(extra),
                        [pl.BlockSpec((tm, kdim), lambda i, j: (i, 0))] + wspecs + list(extra_specs),
                        [dots], tn, epilogue, out_specs, out_shape)


def _mm_nt_wide(name, dy, wg, kind, tm, to, out_dtype=BF16, after=()):
    m, n = dy.shape
    _, a, b = wg.shape
    kdim = 2 * a if kind == "col" else 4 * a
    tm = min(tm, m)
    ws, wspecs, dots = _wide_nt_weights(wg, kind, to, 1)
    return _matmul_wide(name, (m // tm, kdim // to), [dy] + ws,
                        [pl.BlockSpec((tm, n), lambda i, j: (i, 0))] + wspecs, [dots], to, _wide_store(out_dtype),
                        [pl.BlockSpec((tm, to), lambda i, j: (i, j))],
                        [jax.ShapeDtypeStruct((m, kdim), out_dtype)], after=after)[0]


def _mm_tn(name, a, dy, kind, tiles):
    m, kdim = a.shape
    _, n = dy.shape
    tr, tn, tmm = tiles
    tmm = min(tmm, m)
    grid = (kdim // tr, n // tn, m // tmm)
    in_specs = [pl.BlockSpec((tmm, tr), lambda r, j, k: (k, r)), pl.BlockSpec((tmm, tn), lambda r, j, k: (k, j))]
    if kind == "col":
        ns = n // 4
        bps = ns // tn
        out_shape = [jax.ShapeDtypeStruct((4, kdim, ns), BF16)]
        out_specs = [pl.BlockSpec((None, tr, tn), lambda r, j, k: (j // bps, r, j % bps))]
    else:
        rs = kdim // 4
        rps = rs // tr
        out_shape = [jax.ShapeDtypeStruct((4, rs, n), BF16)]
        out_specs = [pl.BlockSpec((None, tr, tn), lambda r, j, k: (r // rps, r % rps, j))]
    return _matmul(name, grid, [a, dy], in_specs, [(0, 1, 0, TN)], 1, (tr, tn), _ep_store(BF16),
                   out_specs, out_shape)[0]


def _zero_halo(pad_ref, s):
    z = jnp.zeros((HALO, pad_ref.shape[1]), F32)
    pad_ref[pl.ds(0, HALO), :] = z
    pad_ref[pl.ds(HALO + s, HALO), :] = z


def _window(pad_ref, r0, rows):
    return pad_ref[pl.ds(r0, rows + 2 * HALO), :]


def _delayed(win, k, rows):
    if k == 0:
        return win[HALO:HALO + rows]
    return pltpu.roll(win, k, axis=0)[HALO:HALO + rows]


def _advanced(win, k, rows):
    if k == 0:
        return win[HALO:HALO + rows]
    return pltpu.roll(win, win.shape[0] - k, axis=0)[HALO:HALO + rows]


def _fold8(x):
    return jnp.sum(x.reshape(x.shape[0] // 8, 8, x.shape[1]), axis=0)


def _chunks(s):
    rows = min(CHUNK_ROWS, s)
    return rows, s // rows


def _col_spec(s, first_block):
    return pl.BlockSpec((s, HEAD), lambda j: (0, first_block + j))


def _sgu_fwd(name, z, ln_g, ln_b, w, b):
    s = z.shape[0]
    tr = _row_tile(s, 1024)
    nh = GROUP // HEAD

    def body(u_ref, v_ref, lg_ref, lb_ref, w_ref, b_ref, o_ref):
        row = lax.broadcasted_iota(jnp.int32, (HEAD, HEAD), 0)
        col = lax.broadcasted_iota(jnp.int32, (HEAD, HEAD), 1)
        wm = jnp.where(row >= col, w_ref[...], 0.0).astype(BF16)
        for ck in range(tr // HEAD):
            rs = pl.ds(ck * HEAD, HEAD)
            u = _gelu(u_ref[rs, :])
            v = _gelu(v_ref[rs, :])
            mu = jnp.mean(v, axis=-1, keepdims=True)
            vc = v - mu
            var = jnp.mean(vc * vc, axis=-1, keepdims=True)
            vln = vc * lax.rsqrt(var + EPS) * lg_ref[...] + lb_ref[...]
            sp = jnp.dot(wm, vln.astype(BF16), preferred_element_type=F32) + b_ref[...]
            o_ref[rs, :] = (u * sp).astype(BF16)

    head_vec = pl.BlockSpec((None, 1, HEAD), lambda h, i: (h, 0, 0))
    return pl.pallas_call(
        body, name=name, grid=(nh, s // tr),
        in_specs=[pl.BlockSpec((tr, HEAD), lambda h, i: (i, h)), pl.BlockSpec((tr, HEAD), lambda h, i: (i, nh + h)),
                  head_vec, head_vec, pl.BlockSpec((None, HEAD, HEAD), lambda h, i: (h, 0, 0)),
                  pl.BlockSpec((None, HEAD, 1), lambda h, i: (h, 0, 0))],
        out_specs=pl.BlockSpec((tr, HEAD), lambda h, i: (i, h)),
        out_shape=jax.ShapeDtypeStruct((s, GROUP), BF16), compiler_params=_cp(("parallel", "parallel")),
    )(z, z, ln_g, ln_b, w, b)


def _sgu_bwd(name, z, d_o, ln_g, ln_b, w, b):
    s = z.shape[0]
    tr = _row_tile(s, 1024)
    nh = GROUP // HEAD

    def body(u_ref, v_ref, do_ref, lg_ref, lb_ref, w_ref, b_ref, du_ref, dv_ref, dlg_ref, dlb_ref, dw_ref, db_ref,
             dsp_acc):
        row = lax.broadcasted_iota(jnp.int32, (HEAD, HEAD), 0)
        col = lax.broadcasted_iota(jnp.int32, (HEAD, HEAD), 1)
        tril = row >= col
        wm = jnp.where(tril, w_ref[...], 0.0).astype(BF16)
        i = pl.program_id(1)

        @pl.when(i == 0)
        def _():
            dlg_ref[...] = jnp.zeros_like(dlg_ref)
            dlb_ref[...] = jnp.zeros_like(dlb_ref)
            dw_ref[...] = jnp.zeros_like(dw_ref)
            dsp_acc[...] = jnp.zeros_like(dsp_acc)

        dlg = jnp.zeros((1, HEAD), F32)
        dlb = jnp.zeros((1, HEAD), F32)
        dw = jnp.zeros((HEAD, HEAD), F32)
        dsp_sum = jnp.zeros((HEAD, HEAD), F32)
        for ck in range(tr // HEAD):
            rs = pl.ds(ck * HEAD, HEAD)
            zu = u_ref[rs, :]
            zv = v_ref[rs, :]
            u = _gelu(zu)
            v = _gelu(zv)
            mu = jnp.mean(v, axis=-1, keepdims=True)
            vc = v - mu
            var = jnp.mean(vc * vc, axis=-1, keepdims=True)
            rstd = lax.rsqrt(var + EPS)
            xh = vc * rstd
            vln = (xh * lg_ref[...] + lb_ref[...]).astype(BF16)
            sp = jnp.dot(wm, vln, preferred_element_type=F32) + b_ref[...]
            d_oa = do_ref[rs, :].astype(F32)
            du = d_oa * sp
            dsp = d_oa * u
            dsp_b = dsp.astype(BF16)
            dvln = lax.dot_general(wm, dsp_b, TN, preferred_element_type=F32)
            dw = dw + lax.dot_general(dsp_b, vln, NT, preferred_element_type=F32)
            dsp_sum = dsp_sum + dsp
            dlg = dlg + jnp.sum(dvln * xh, axis=0, keepdims=True)
            dlb = dlb + jnp.sum(dvln, axis=0, keepdims=True)
            dxh = dvln * lg_ref[...]
            dv = rstd * (dxh - jnp.mean(dxh, axis=-1, keepdims=True)
                         - xh * jnp.mean(dxh * xh, axis=-1, keepdims=True))
            du_ref[rs, :] = (du * _gelu_grad(zu)).astype(BF16)
            dv_ref[rs, :] = (dv * _gelu_grad(zv)).astype(BF16)
        dlg_ref[...] += dlg
        dlb_ref[...] += dlb
        dw_ref[...] += jnp.where(tril, dw, 0.0)
        dsp_acc[...] += dsp_sum

        @pl.when(i == pl.num_programs(1) - 1)
        def _():
            db_ref[...] = jnp.sum(dsp_acc[...], axis=1, keepdims=True)

    head_vec = pl.BlockSpec((None, 1, HEAD), lambda h, i: (h, 0, 0))
    head_mat = pl.BlockSpec((None, HEAD, HEAD), lambda h, i: (h, 0, 0))
    head_col = pl.BlockSpec((None, HEAD, 1), lambda h, i: (h, 0, 0))
    return pl.pallas_call(
        body, name=name, grid=(nh, s // tr),
        in_specs=[pl.BlockSpec((tr, HEAD), lambda h, i: (i, h)), pl.BlockSpec((tr, HEAD), lambda h, i: (i, nh + h)),
                  pl.BlockSpec((tr, HEAD), lambda h, i: (i, h)), head_vec, head_vec, head_mat, head_col],
        out_specs=[pl.BlockSpec((tr, HEAD), lambda h, i: (i, h)), pl.BlockSpec((tr, HEAD), lambda h, i: (i, h)),
                   head_vec, head_vec, head_mat, head_col],
        out_shape=[jax.ShapeDtypeStruct((s, GROUP), BF16), jax.ShapeDtypeStruct((s, GROUP), BF16),
                   jax.ShapeDtypeStruct((nh, 1, HEAD), F32), jax.ShapeDtypeStruct((nh, 1, HEAD), F32),
                   jax.ShapeDtypeStruct((nh, HEAD, HEAD), F32), jax.ShapeDtypeStruct((nh, HEAD, 1), F32)],
        scratch_shapes=[pltpu.VMEM((HEAD, HEAD), F32)],
        compiler_params=_cp(("parallel", "arbitrary")),
    )(z, z, d_o, ln_g, ln_b, w, b)


def _shortconv_fwd(name, z, w):
    s = z.shape[0]
    kw = w.shape[0]
    rows, nchunk = _chunks(s)
    nb = GROUP // HEAD

    def body(h_ref, bg_ref, cg_ref, w_ref, o_ref, pad):
        _zero_halo(pad, s)

        def fill(ci, carry):
            r0 = pl.multiple_of(ci * rows, rows)
            pad[pl.ds(pl.multiple_of(HALO + r0, 8), rows), :] = cg_ref[pl.ds(r0, rows), :] * h_ref[pl.ds(r0, rows), :]
            return carry

        lax.fori_loop(0, nchunk, fill, 0)

        def step(ci, carry):
            r0 = pl.multiple_of(ci * rows, rows)
            win = _window(pad, r0, rows)
            cv = jnp.zeros((rows, HEAD), F32)
            for k in range(kw):
                cv = cv + w_ref[k:k + 1, :] * _delayed(win, kw - 1 - k, rows)
            o_ref[pl.ds(r0, rows), :] = (bg_ref[pl.ds(r0, rows), :] * cv).astype(BF16)
            return carry

        lax.fori_loop(0, nchunk, step, 0)

    return pl.pallas_call(
        body, name=name, grid=(nb,),
        in_specs=[_col_spec(s, 8), _col_spec(s, 12), _col_spec(s, 16), pl.BlockSpec((kw, HEAD), lambda j: (0, j))],
        out_specs=_col_spec(s, 0),
        out_shape=jax.ShapeDtypeStruct((s, GROUP), BF16),
        scratch_shapes=[pltpu.VMEM((s + 2 * HALO, HEAD), F32)],
        compiler_params=_cp(("parallel",)),
    )(z, z, z, w)


def _shortconv_bwd(name, z, d_o, w):
    s = z.shape[0]
    kw = w.shape[0]
    rows, nchunk = _chunks(s)
    nb = GROUP // HEAD

    def body(h_ref, bg_ref, cg_ref, do_ref, w_ref, dh_ref, dbg_ref, dcg_ref, dw_ref, pad_q, pad_d, acc):
        _zero_halo(pad_q, s)
        _zero_halo(pad_d, s)
        acc[...] = jnp.zeros_like(acc)

        def fill(ci, carry):
            r0 = pl.multiple_of(ci * rows, rows)
            rs = pl.ds(r0, rows)
            ps = pl.ds(pl.multiple_of(HALO + r0, 8), rows)
            pad_q[ps, :] = cg_ref[rs, :] * h_ref[rs, :]
            pad_d[ps, :] = do_ref[rs, :].astype(F32) * bg_ref[rs, :]
            return carry

        lax.fori_loop(0, nchunk, fill, 0)

        def step(ci, carry):
            r0 = pl.multiple_of(ci * rows, rows)
            rs = pl.ds(r0, rows)
            wq = _window(pad_q, r0, rows)
            wd = _window(pad_d, r0, rows)
            dcv = wd[HALO:HALO + rows]
            cv = jnp.zeros((rows, HEAD), F32)
            dq = jnp.zeros((rows, HEAD), F32)
            for k in range(kw):
                qk = _delayed(wq, kw - 1 - k, rows)
                cv = cv + w_ref[k:k + 1, :] * qk
                dq = dq + w_ref[k:k + 1, :] * _advanced(wd, kw - 1 - k, rows)
                acc[k] += _fold8(dcv * qk)
            dbg_ref[rs, :] = (do_ref[rs, :].astype(F32) * cv).astype(BF16)
            dcg_ref[rs, :] = (dq * h_ref[rs, :]).astype(BF16)
            dh_ref[rs, :] = (dq * cg_ref[rs, :]).astype(BF16)
            return carry

        lax.fori_loop(0, nchunk, step, 0)
        for k in range(kw):
            dw_ref[k:k + 1, :] = jnp.sum(acc[k], axis=0, keepdims=True)

    col = _col_spec(s, 0)
    return pl.pallas_call(
        body, name=name, grid=(nb,),
        in_specs=[_col_spec(s, 8), _col_spec(s, 12), _col_spec(s, 16), col, pl.BlockSpec((kw, HEAD), lambda j: (0, j))],
        out_specs=[col, col, col, pl.BlockSpec((kw, HEAD), lambda j: (0, j))],
        out_shape=[jax.ShapeDtypeStruct((s, GROUP), BF16)] * 3 + [jax.ShapeDtypeStruct((kw, GROUP), F32)],
        scratch_shapes=[pltpu.VMEM((s + 2 * HALO, HEAD), F32), pltpu.VMEM((s + 2 * HALO, HEAD), F32),
                        pltpu.VMEM((kw, 8, HEAD), F32)],
        compiler_params=_cp(("parallel",)),
    )(z, z, z, d_o, w)


def _conformer_conv_fwd(name, z, w, bias):
    s = z.shape[0]
    kw = w.shape[0]
    rows, nchunk = _chunks(s)
    nb = GROUP // HEAD

    def body(a_ref, g_ref, w_ref, b_ref, o_ref, pad):
        _zero_halo(pad, s)

        def fill(ci, carry):
            r0 = pl.multiple_of(ci * rows, rows)
            rs = pl.ds(r0, rows)
            pad[pl.ds(pl.multiple_of(HALO + r0, 8), rows), :] = a_ref[rs, :] * _sigmoid(g_ref[rs, :])
            return carry

        lax.fori_loop(0, nchunk, fill, 0)

        def step(ci, carry):
            r0 = pl.multiple_of(ci * rows, rows)
            win = _window(pad, r0, rows)
            cc = jnp.zeros((rows, HEAD), F32)
            for k in range(kw):
                cc = cc + w_ref[k:k + 1, :] * _delayed(win, kw - 1 - k, rows)
            o_ref[pl.ds(r0, rows), :] = cc + b_ref[...]
            return carry

        lax.fori_loop(0, nchunk, step, 0)

    return pl.pallas_call(
        body, name=name, grid=(nb,),
        in_specs=[_col_spec(s, 20), _col_spec(s, 24), pl.BlockSpec((kw, HEAD), lambda j: (0, j)),
                  pl.BlockSpec((1, HEAD), lambda j: (0, j))],
        out_specs=_col_spec(s, 0),
        out_shape=jax.ShapeDtypeStruct((s, GROUP), F32),
        scratch_shapes=[pltpu.VMEM((s + 2 * HALO, HEAD), F32)],
        compiler_params=_cp(("parallel",)),
    )(z, z, w, bias)


def _ln_rows(cc, g, b):
    mu = jnp.mean(cc, axis=-1, keepdims=True)
    xc = cc - mu
    var = jnp.mean(xc * xc, axis=-1, keepdims=True)
    rstd = lax.rsqrt(var + EPS)
    xh = xc * rstd
    return xh, rstd, xh * g + b


def _conformer_ln_fwd(name, cc, g, b):
    s, d = cc.shape
    tr = _row_tile(s, 512)

    def body(c_ref, g_ref, b_ref, o_ref):
        _, _, l = _ln_rows(c_ref[...], g_ref[...], b_ref[...])
        o_ref[...] = (l * _sigmoid(l)).astype(BF16)

    row = pl.BlockSpec((tr, d), lambda i: (i, 0))
    vec = pl.BlockSpec((1, d), lambda i: (0, 0))
    return pl.pallas_call(
        body, name=name, grid=(s // tr,), in_specs=[row, vec, vec], out_specs=row,
        out_shape=jax.ShapeDtypeStruct((s, d), BF16), compiler_params=_cp(("parallel",)),
    )(cc, g, b)


def _conformer_ln_bwd(name, cc, d_o, g, b):
    s, d = cc.shape
    tr = _row_tile(s, 512)

    def body(c_ref, do_ref, g_ref, b_ref, dcc_ref, dg_ref, db_ref, dcb_ref):
        xh, rstd, l = _ln_rows(c_ref[...], g_ref[...], b_ref[...])
        sg = _sigmoid(l)
        dl = do_ref[...].astype(F32) * sg * (1.0 + l * (1.0 - sg))
        dxh = dl * g_ref[...]
        dcc = rstd * (dxh - jnp.mean(dxh, axis=-1, keepdims=True) - xh * jnp.mean(dxh * xh, axis=-1, keepdims=True))
        dcc_ref[...] = dcc

        @pl.when(pl.program_id(0) == 0)
        def _():
            dg_ref[...] = jnp.zeros_like(dg_ref)
            db_ref[...] = jnp.zeros_like(db_ref)
            dcb_ref[...] = jnp.zeros_like(dcb_ref)

        dg_ref[...] += jnp.sum(dl * xh, axis=0, keepdims=True)
        db_ref[...] += jnp.sum(dl, axis=0, keepdims=True)
        dcb_ref[...] += jnp.sum(dcc, axis=0, keepdims=True)

    row = pl.BlockSpec((tr, d), lambda i: (i, 0))
    vec = pl.BlockSpec((1, d), lambda i: (0, 0))
    return pl.pallas_call(
        body, name=name, grid=(s // tr,), in_specs=[row, row, vec, vec], out_specs=[row, vec, vec, vec],
        out_shape=[jax.ShapeDtypeStruct((s, d), F32)] + [jax.ShapeDtypeStruct((1, d), F32)] * 3,
        compiler_params=_cp(("arbitrary",)),
    )(cc, d_o, g, b)


def _conformer_conv_bwd(name, z, dcc, w):
    s = z.shape[0]
    kw = w.shape[0]
    rows, nchunk = _chunks(s)
    nb = GROUP // HEAD

    def body(a_ref, g_ref, d_ref, w_ref, da_ref, dg_ref, dw_ref, pad_h, pad_d, acc):
        _zero_halo(pad_h, s)
        _zero_halo(pad_d, s)
        acc[...] = jnp.zeros_like(acc)

        def fill(ci, carry):
            r0 = pl.multiple_of(ci * rows, rows)
            rs = pl.ds(r0, rows)
            ps = pl.ds(pl.multiple_of(HALO + r0, 8), rows)
            pad_h[ps, :] = a_ref[rs, :] * _sigmoid(g_ref[rs, :])
            pad_d[ps, :] = d_ref[rs, :]
            return carry

        lax.fori_loop(0, nchunk, fill, 0)

        def step(ci, carry):
            r0 = pl.multiple_of(ci * rows, rows)
            rs = pl.ds(r0, rows)
            wh = _window(pad_h, r0, rows)
            wd = _window(pad_d, r0, rows)
            dcc_c = wd[HALO:HALO + rows]
            dhc = jnp.zeros((rows, HEAD), F32)
            for k in range(kw):
                dhc = dhc + w_ref[k:k + 1, :] * _advanced(wd, kw - 1 - k, rows)
                acc[k] += _fold8(dcc_c * _delayed(wh, kw - 1 - k, rows))
            sg = _sigmoid(g_ref[rs, :])
            da_ref[rs, :] = (dhc * sg).astype(BF16)
            dg_ref[rs, :] = (dhc * a_ref[rs, :] * sg * (1.0 - sg)).astype(BF16)
            return carry

        lax.fori_loop(0, nchunk, step, 0)
        for k in range(kw):
            dw_ref[k:k + 1, :] = jnp.sum(acc[k], axis=0, keepdims=True)

    col = _col_spec(s, 0)
    return pl.pallas_call(
        body, name=name, grid=(nb,),
        in_specs=[_col_spec(s, 20), _col_spec(s, 24), col, pl.BlockSpec((kw, HEAD), lambda j: (0, j))],
        out_specs=[col, col, pl.BlockSpec((kw, HEAD), lambda j: (0, j))],
        out_shape=[jax.ShapeDtypeStruct((s, GROUP), BF16)] * 2 + [jax.ShapeDtypeStruct((kw, GROUP), F32)],
        scratch_shapes=[pltpu.VMEM((s + 2 * HALO, HEAD), F32), pltpu.VMEM((s + 2 * HALO, HEAD), F32),
                        pltpu.VMEM((kw, 8, HEAD), F32)],
        compiler_params=_cp(("parallel",)),
    )(z, z, dcc, w)


def _pool_window_sum(win, level, rows, shift):
    n = win.shape[0]

    def moved(v, k):
        return pltpu.roll(v, k if shift is _delayed else n - k, axis=0)

    s2 = win + moved(win, 1)
    s4 = s2 + moved(s2, 2)
    s8 = s4 + moved(s4, 4)
    s16 = s8 + moved(s8, 8)
    sel = jnp.where(level == 0, s2, jnp.where(level == 1, s4, jnp.where(level == 2, s8, s16)))
    return sel[HALO:HALO + rows]


def _pool_count(level, r0, rows):
    t = r0 + lax.broadcasted_iota(jnp.int32, (rows, 1), 0)
    width = jnp.left_shift(jnp.int32(2), level)
    return jnp.minimum(t + 1, width).astype(F32)


def _pool_fwd(name, z, pool_w, scale):
    s = z.shape[0]
    rows, nchunk = _chunks(s)

    def body(z_ref, w_ref, sc_ref, o_ref, pad):
        level = pl.program_id(0)
        _zero_halo(pad, s)

        def fill(ci, carry):
            r0 = pl.multiple_of(ci * rows, rows)
            pad[pl.ds(pl.multiple_of(HALO + r0, 8), rows), :] = z_ref[pl.ds(r0, rows), :]
            return carry

        lax.fori_loop(0, nchunk, fill, 0)
        wb = w_ref[...].astype(BF16)

        def step(ci, carry):
            r0 = pl.multiple_of(ci * rows, rows)
            win = _window(pad, r0, rows)
            pm = _pool_window_sum(win, level, rows, _delayed) / _pool_count(level, r0, rows) - win[HALO:HALO + rows]
            r = jnp.dot(pm.astype(BF16), wb, preferred_element_type=F32)
            o_ref[pl.ds(r0, rows), :] = (r * sc_ref[...]).astype(BF16)
            return carry

        lax.fori_loop(0, nchunk, step, 0)

    return pl.pallas_call(
        body, name=name, grid=(POOL_LEVELS,),
        in_specs=[_col_spec(s, 28), pl.BlockSpec((None, HEAD, HEAD), lambda j: (j, 0, 0)),
                  pl.BlockSpec((1, HEAD), lambda j: (0, j))],
        out_specs=_col_spec(s, 0),
        out_shape=jax.ShapeDtypeStruct((s, GROUP), BF16),
        scratch_shapes=[pltpu.VMEM((s + 2 * HALO, HEAD), F32)],
        compiler_params=_cp(("parallel",)),
    )(z, pool_w, scale)


def _pool_bwd(name, z, d_o, pool_w, scale):
    s = z.shape[0]
    rows, nchunk = _chunks(s)

    def body(z_ref, do_ref, w_ref, sc_ref, dz_ref, dw_ref, dsc_ref, pad, pad_q, dw_acc, dsc_acc):
        level = pl.program_id(0)
        _zero_halo(pad, s)
        _zero_halo(pad_q, s)
        dw_acc[...] = jnp.zeros_like(dw_acc)
        dsc_acc[...] = jnp.zeros_like(dsc_acc)

        def fill(ci, carry):
            r0 = pl.multiple_of(ci * rows, rows)
            pad[pl.ds(pl.multiple_of(HALO + r0, 8), rows), :] = z_ref[pl.ds(r0, rows), :]
            return carry

        lax.fori_loop(0, nchunk, fill, 0)
        wb = w_ref[...].astype(BF16)

        def first(ci, carry):
            r0 = pl.multiple_of(ci * rows, rows)
            win = _window(pad, r0, rows)
            cnt = _pool_count(level, r0, rows)
            pm = (_pool_window_sum(win, level, rows, _delayed) / cnt - win[HALO:HALO + rows]).astype(BF16)
            r = jnp.dot(pm, wb, preferred_element_type=F32)
            d_od = do_ref[pl.ds(r0, rows), :].astype(F32)
            dsc_acc[...] += _fold8(d_od * r)
            dr = (d_od * sc_ref[...]).astype(BF16)
            dw_acc[...] += lax.dot_general(pm, dr, TN, preferred_element_type=F32)
            dpm = lax.dot_general(dr, wb, NT, preferred_element_type=F32)
            pad_q[pl.ds(pl.multiple_of(HALO + r0, 8), rows), :] = dpm / cnt
            return carry

        lax.fori_loop(0, nchunk, first, 0)

        def second(ci, carry):
            r0 = pl.multiple_of(ci * rows, rows)
            wq = _window(pad_q, r0, rows)
            dpm = wq[HALO:HALO + rows] * _pool_count(level, r0, rows)
            dz_ref[pl.ds(r0, rows), :] = (_pool_window_sum(wq, level, rows, _advanced) - dpm).astype(BF16)
            return carry

        lax.fori_loop(0, nchunk, second, 0)
        dw_ref[...] = dw_acc[...]
        dsc_ref[...] = jnp.sum(dsc_acc[...], axis=0, keepdims=True)

    col = _col_spec(s, 0)
    mat = pl.BlockSpec((None, HEAD, HEAD), lambda j: (j, 0, 0))
    vec = pl.BlockSpec((1, HEAD), lambda j: (0, j))
    return pl.pallas_call(
        body, name=name, grid=(POOL_LEVELS,),
        in_specs=[_col_spec(s, 28), col, mat, vec], out_specs=[col, mat, vec],
        out_shape=[jax.ShapeDtypeStruct((s, GROUP), BF16), jax.ShapeDtypeStruct((POOL_LEVELS, HEAD, HEAD), F32),
                   jax.ShapeDtypeStruct((1, GROUP), F32)],
        scratch_shapes=[pltpu.VMEM((s + 2 * HALO, HEAD), F32), pltpu.VMEM((s + 2 * HALO, HEAD), F32),
                        pltpu.VMEM((HEAD, HEAD), F32), pltpu.VMEM((8, HEAD), F32)],
        compiler_params=_cp(("parallel",)),
    )(z, d_o, pool_w, scale)


def _ep_residual(accs, ins, outs):
    outs[0][...] = ins[2][...] + accs[0]


def _mm_residual(name, x, wg, h, tm, tn):
    m, d = h.shape
    tm = min(tm, m)

    def ep(accs, ins, outs, cols):
        outs[0][:, cols] = ins[5][:, cols] + accs[0]

    tile = pl.BlockSpec((tm, tn), lambda i, j: (i, j))
    return _mm_nn_wide(name, x, wg, "row", tm, tn, epilogue=ep, extra=[h], extra_specs=[tile], out_specs=[tile],
                       out_shape=[jax.ShapeDtypeStruct((m, d), F32)])[0]


def _swiglu_fwd(name, y, wg_gate, wg_up, tm):
    m, kdim = y.shape
    _, a, b = wg_gate.shape
    tm = min(tm, m)

    def ep(accs, ins, outs, cols):
        gt, up = accs
        outs[0][:, cols] = (gt * _sigmoid(gt) * up).astype(BF16)
        outs[1][:, cols] = gt.astype(BF16)
        outs[2][:, cols] = up.astype(BF16)

    gws, gspecs, gdots = _wide_nn_weights(wg_gate, "col", b, 1)
    uws, uspecs, udots = _wide_nn_weights(wg_up, "col", b, 1 + len(gws))
    out = pl.BlockSpec((tm, b), lambda i, j: (i, j))
    return _matmul_wide(name, (m // tm, 4), [y] + gws + uws,
                        [pl.BlockSpec((tm, kdim), lambda i, j: (i, 0))] + gspecs + uspecs, [gdots, udots], b, ep,
                        [out] * 3, [jax.ShapeDtypeStruct((m, 4 * b), BF16)] * 3)


def _swiglu_bwd(name, dh, wg_down, gate, up, tm):
    m, n = dh.shape
    _, a, b = wg_down.shape
    tm = min(tm, m)

    def ep(accs, ins, outs, cols):
        d_act = accs[0]
        gt = ins[3][:, cols].astype(F32)
        upv = ins[4][:, cols].astype(F32)
        sg = _sigmoid(gt)
        outs[0][:, cols] = (d_act * upv * sg * (1.0 + gt * (1.0 - sg))).astype(BF16)
        outs[1][:, cols] = (d_act * gt * sg).astype(BF16)
        outs[2][:, cols] = (gt * sg * upv).astype(BF16)

    ws, wspecs, dots = _wide_nt_weights(wg_down, "row", a, 1)
    tile = pl.BlockSpec((tm, a), lambda i, j: (i, j))
    return _matmul_wide(name, (m // tm, 4), [dh] + ws + [gate, up],
                        [pl.BlockSpec((tm, n), lambda i, j: (i, 0))] + wspecs + [tile, tile], [dots], a, ep,
                        [tile] * 3, [jax.ShapeDtypeStruct((m, 4 * a), BF16)] * 3)


def _ffn_dy(name, d_gate, d_up, wg_gate, wg_up, tiles, after=()):
    m, n = d_gate.shape
    _, a, b = wg_gate.shape
    kdim = 2 * a
    tm, to, tc = tiles
    tm = min(tm, m)
    grid = (m // tm, kdim // to, n // tc)
    lhs = pl.BlockSpec((tm, tc), lambda i, j, k: (i, k))
    wspec = _w_spec_nt(wg_gate, "col", to, tc)
    return _matmul(name, grid, [d_gate, d_up, wg_gate, wg_up], [lhs, lhs, wspec, wspec],
                   [(0, 2, 0, NT), (1, 3, 0, NT)], 1, (tm, to), _ep_store(BF16),
                   [pl.BlockSpec((tm, to), lambda i, j, k: (i, j))], [jax.ShapeDtypeStruct((m, kdim), BF16)],
                   after=after)[0]


def _ple_fwd(name, y, wg, h, pp, tm, tn):
    m, d = h.shape
    tm = min(tm, m)

    def ep(accs, ins, outs, cols):
        pg = accs[0]
        outs[0][:, cols] = ins[5][:, cols] + _sigmoid(pg) * ins[6][:, cols].astype(F32)
        outs[1][:, cols] = pg.astype(BF16)

    tile = pl.BlockSpec((tm, tn), lambda i, j: (i, j))
    return _mm_nn_wide(name, y, wg, "row", tm, tn, epilogue=ep, extra=[h, pp], extra_specs=[tile, tile],
                       out_specs=[tile, tile],
                       out_shape=[jax.ShapeDtypeStruct((m, d), F32), jax.ShapeDtypeStruct((m, d), BF16)])


def _ple_bwd(name, dh, pg, pp, after=()):
    s, d = dh.shape
    tr = _row_tile(s, 512)

    def body(dh_ref, pg_ref, pp_ref, *rest):
        dpp_ref, dpg_ref = rest[-2:]
        dhv = dh_ref[...]
        sg = _sigmoid(pg_ref[...].astype(F32))
        dpp_ref[...] = (dhv * sg).astype(BF16)
        dpg_ref[...] = (dhv * pp_ref[...].astype(F32) * sg * (1.0 - sg)).astype(BF16)

    row = pl.BlockSpec((tr, d), lambda i: (i, 0))
    return pl.pallas_call(
        body, name=name, grid=(s // tr,), in_specs=[row] * 3 + [ANY_SPEC] * len(after), out_specs=[row] * 2,
        out_shape=[jax.ShapeDtypeStruct((s, d), BF16)] * 2, compiler_params=_cp(("parallel",)),
    )(dh, pg, pp, *after)


BIG = ["w_in", "w_out", "w_gate", "w_up", "w_down", "w_ple_gate", "w_ple_proj"]
KIND = {"w_in": "col", "w_out": "row", "w_gate": "col", "w_up": "col", "w_down": "row", "w_ple_gate": "row",
        "w_ple_proj": "col"}
GATHER_GROUPS = (("w_in", "w_out"), ("w_gate", "w_up"), ("w_down", "w_ple_gate", "w_ple_proj"))
RS_GROUPS = (("w_ple_gate", "w_ple_proj", "w_down", "w_gate", "w_up"), ("w_out", "w_in"))
SMALL = ["norm_mix_g", "sgu_ln_g", "sgu_ln_b", "sgu_w", "sgu_b", "sc_conv_w", "cf_conv_w", "cf_conv_b", "cf_ln_g",
         "cf_ln_b", "pool_w", "pool_scale", "norm_ffn_g", "norm_ple_g", "final_norm_g"]
CHIP_SPLIT = ["sc_conv_w", "cf_conv_w"]
WEIGHTS = ['norm_mix_g', 'w_in', 'sgu_ln_g', 'sgu_ln_b', 'sgu_w', 'sgu_b', 'sc_conv_w', 'cf_conv_w', 'cf_conv_b',
           'cf_ln_g', 'cf_ln_b', 'pool_w', 'pool_scale', 'w_out', 'norm_ffn_g', 'w_gate', 'w_up', 'w_down',
           'norm_ple_g', 'w_ple_gate', 'w_ple_proj', 'final_norm_g']


def _tile(n, want):
    if n <= want:
        return n
    t = (want // 128) * 128
    while n % t:
        t -= 128
    return t


def _pack_rows(vecs):
    flat = jnp.concatenate([v.reshape(-1) for v in vecs])
    n = flat.shape[0]
    quantum = PACK_ROWS * 128
    padded = ((n + quantum - 1) // quantum) * quantum
    return jnp.pad(flat, (0, padded - n)).reshape(padded // 128, 128), n


def _unpack(flat, shapes):
    out, off = [], 0
    for shp in shapes:
        size = math.prod(shp)
        out.append(flat[off:off + size].reshape(shp))
        off += size
    return out


def kernel(x, p, norm_mix_g, w_in, sgu_ln_g, sgu_ln_b, sgu_w, sgu_b, sc_conv_w, cf_conv_w, cf_conv_b, cf_ln_g, cf_ln_b, pool_w, pool_scale, w_out, norm_ffn_g, w_gate, w_up, w_down, norm_ple_g, w_ple_gate, w_ple_proj, final_norm_g, loss_target, m_norm_mix_g, m_w_in, m_sgu_ln_g, m_sgu_ln_b, m_sgu_w, m_sgu_b, m_sc_conv_w, m_cf_conv_w, m_cf_conv_b, m_cf_ln_g, m_cf_ln_b, m_pool_w, m_pool_scale, m_w_out, m_norm_ffn_g, m_w_gate, m_w_up, m_w_down, m_norm_ple_g, m_w_ple_gate, m_w_ple_proj, m_final_norm_g, v_norm_mix_g, v_w_in, v_sgu_ln_g, v_sgu_ln_b, v_sgu_w, v_sgu_b, v_sc_conv_w, v_cf_conv_w, v_cf_conv_b, v_cf_ln_g, v_cf_ln_b, v_pool_w, v_pool_scale, v_w_out, v_norm_ffn_g, v_w_gate, v_w_up, v_w_down, v_norm_ple_g, v_w_ple_gate, v_w_ple_proj, v_final_norm_g):
    args = dict(locals())
    w = {n: args[n] for n in WEIGHTS}
    mom = {n: args["m_" + n] for n in WEIGHTS}
    var = {n: args["v_" + n] for n in WEIGHTS}
    depth = w_in.shape[0]
    s, d = x.shape[1], x.shape[2]
    f_dim = 4 * w_gate.shape[2]
    xi, yi, ci = lax.axis_index("x"), lax.axis_index("y"), lax.axis_index("c")
    c_idx = ci.astype(jnp.int32).reshape(1)

    chip_idx = (2 * xi + yi).astype(jnp.int32).reshape(1)

    def start_gathers(l, dep):
        pending = []
        for gi, names in enumerate(GATHER_GROUPS):
            lands = [_cast_into_landing(f"cast_{n}", w[n], l, 0 if KIND[n] == "col" else 1, chip_idx) for n in names]
            pending.append(_gather_start(f"gather_start_{l}_{gi}", lands, dep))
            dep = pending[-1][3]
        return pending, dep

    def finish_gather(l, gi, pending, after):
        send_sems, recv_sems, lands, _ = pending
        lands = _gather_wait(f"gather_wait_{l}_{gi}", send_sems, recv_sems, lands, after)
        return dict(zip(GATHER_GROUPS[gi], _gather_share(f"gather_share_{l}_{gi}", lands)))

    conv_pack = jnp.concatenate([sc_conv_w, cf_conv_w], axis=1)
    taps = conv_pack.shape[1]
    rows_pad = ((depth * taps + 7) // 8) * 8
    conv_rows = jnp.pad(conv_pack.reshape(depth * taps, HEAD), ((0, rows_pad - depth * taps), (0, 0)))
    conv_all = _allgather8("gather_conv_weights", [conv_rows])[0]
    conv_full = conv_all[0::2, :depth * taps].reshape(4, depth, taps, HEAD)
    conv_full = jnp.transpose(conv_full, (1, 2, 0, 3)).reshape(depth, taps, GROUP)
    sc_w_full, cf_w_full = conv_full[:, :3], conv_full[:, 3:]

    pending, _ = start_gathers(0, conv_all)
    h = x[0]
    saved = []
    gathered = []
    for l in range(depth):
        wg = finish_gather(l, 0, pending[0], h)
        gathered.append(wg)
        started = ()
        this_layer = pending
        if l + 1 < depth:
            pending, token = start_gathers(l + 1, wg["w_in"])
            started = (token,)
        sv = {"h0": h}
        y1 = _rms_fwd("rms_mix", h, norm_mix_g[l:l + 1], after=started)
        z = _mm_nn_wide("mm_in", y1, wg["w_in"], "col", 1024, 1024)[0]
        lg, lb = sgu_ln_g[l][:, None, :], sgu_ln_b[l][:, None, :]
        sb = sgu_b[l][:, :, None]
        oa = _sgu_fwd("sgu_fwd", z, lg, lb, sgu_w[l], sb)
        ob = _shortconv_fwd("shortconv_fwd", z, sc_w_full[l])
        cc = _conformer_conv_fwd("conformer_conv_fwd", z, cf_w_full[l], cf_conv_b[l:l + 1])
        oc = _conformer_ln_fwd("conformer_ln_fwd", cc, cf_ln_g[l:l + 1], cf_ln_b[l:l + 1])
        od = _pool_fwd("pool_fwd", z, pool_w[l], pool_scale[l:l + 1])
        o = jnp.concatenate([oa, ob, oc, od], axis=1)
        tile = pl.BlockSpec((min(1024, s), 1024), lambda i, j, k: (i, j))
        h1 = _mm_residual("mm_out", o, wg["w_out"], h, 1024, 1024)
        wg.update(finish_gather(l, 1, this_layer[1], h1))
        y2 = _rms_fwd("rms_ffn", h1, norm_ffn_g[l:l + 1])
        fs = f_dim // 4
        act, gt, up = _swiglu_fwd("mm_swiglu", y2, wg["w_gate"], wg["w_up"], 512)
        wg.update(finish_gather(l, 2, this_layer[2], act))
        h2 = _mm_nn("mm_down", act, wg["w_down"], "row", (1024, 1024, fs), epilogue=_ep_residual, extra=[h1],
                    extra_specs=[tile], out_specs=[tile], out_shape=[jax.ShapeDtypeStruct((s, d), F32)])[0]
        y3 = _rms_fwd("rms_ple", h2, norm_ple_g[l:l + 1])
        pb = p[l, 0].astype(BF16)
        ptile = pl.BlockSpec((min(1024, s), 512), lambda i, j: (i, j))
        pp = _mm_nn_wide("mm_ple_proj", pb, wg["w_ple_proj"], "col", 1024, 512, epilogue=_wide_store(BF16),
                         out_specs=[ptile], out_shape=[jax.ShapeDtypeStruct((s, d), BF16)])[0]
        h3, pg = _ple_fwd("mm_ple_gate", y3, wg["w_ple_gate"], h2, pp, 1024, 1024)
        sv.update(y1=y1, z=z, cc=cc, o=o, h1=h1, y2=y2, gt=gt, up=up, h2=h2, y3=y3, pb=pb, pp=pp, pg=pg)
        saved.append(sv)
        h = h3

    loss_part, dh, dhb, d_final_g = _loss_head("loss_head", h, final_norm_g[None, :], loss_target[0])

    small_grads = [None] * depth
    where = jnp.stack([2 * xi + yi, 2 * (1 - xi) + yi, 2 * xi + (1 - yi), 2 * (1 - xi) + (1 - yi), ci]).astype(jnp.int32)
    grad_bufs = {n: lax.empty((depth, 2, w[n].shape[1] // 2, w[n].shape[2]), F32) for n in BIG}
    exchanges = [None] * len(RS_GROUPS)

    def halves(g):
        return g.reshape(4, 2, g.shape[1] // 2, g.shape[2])

    def start_exchange(layer, gi, sibling, after):
        send_sems, recv_sems, gs, lands, _ = sibling
        gs, lands = _sibling_wait(f"rs_sibling_wait_{layer}_{gi}", send_sems, recv_sems, gs, lands, after)
        chip_sums = _add_halves("rs_add", gs, lands, c_idx)
        return _chip_exchange_start(f"rs_chips_start_{layer}_{gi}", chip_sums)

    def finish_exchange(layer, gi, after):
        send_sems, recv_sems, parts, lands, _ = exchanges[gi]
        parts, lands = _chip_exchange_wait(f"rs_chips_wait_{layer}_{gi}", send_sems, recv_sems, parts, lands, after)
        names = RS_GROUPS[gi]
        sums = _sum_parts("rs_sum", parts, lands, [grad_bufs[n] for n in names], layer, where)
        grad_bufs.update(zip(names, sums))

    for l in reversed(range(depth)):
        wg = gathered[l]
        sv = saved[l]
        fs = f_dim // 4
        started = () if exchanges[1] is None else (exchanges[1][4],)
        d_pp, d_pg = _ple_bwd("ple_bwd", dh, sv["pg"], sv["pp"], after=started)
        g_ple_proj = _mm_tn("dw_ple_proj", sv["pb"], d_pp, "col", (w_ple_proj.shape[1], 512, 1024))
        g_ple_gate = _mm_tn("dw_ple_gate", sv["y3"], d_pg, "row", (512, 1024, 1024))
        dy3 = _mm_nt_wide("dx_ple_gate", d_pg, wg["w_ple_gate"], "row", 1024, 512)
        dh, dhb, dg_ple = _rms_bwd("rms_ple_bwd", sv["h2"], norm_ple_g[l:l + 1], dy3, dh)

        d_gt, d_up, act = _swiglu_bwd("dx_down_swiglu", dhb, wg["w_down"], sv["gt"], sv["up"], 512)
        g_down = _mm_tn("dw_down", act, dhb, "row", (fs, 1024, 1024))
        g_gate = _mm_tn("dw_gate", sv["y2"], d_gt, "col", (1024, fs, 1024))
        g_up = _mm_tn("dw_up", sv["y2"], d_up, "col", (1024, fs, 1024))
        big = dict(w_gate=g_gate, w_up=g_up, w_down=g_down, w_ple_gate=g_ple_gate, w_ple_proj=g_ple_proj)
        sibling = _sibling_start(f"rs_sibling_start_{l}_0", [halves(big[n]) for n in RS_GROUPS[0]])
        dy2 = _ffn_dy("dx_gate_up", d_gt, d_up, wg["w_gate"], wg["w_up"], (1024, 1024, fs), after=(sibling[4],))
        dh, dhb, dg_ffn = _rms_bwd("rms_ffn_bwd", sv["h1"], norm_ffn_g[l:l + 1], dy2, dh)

        g_out = _mm_tn("dw_out", sv["o"], dhb, "row", (512, 1024, 1024))
        if exchanges[0] is not None:
            finish_exchange(l + 1, 0, g_out)
        exchanges[0] = start_exchange(l, 0, sibling, g_out)
        d_o = _mm_nt_wide("dx_out", dhb, wg["w_out"], "row", 1024, 512, after=(exchanges[0][4],))
        z = sv["z"]
        lg, lb = sgu_ln_g[l][:, None, :], sgu_ln_b[l][:, None, :]
        sb = sgu_b[l][:, :, None]
        dzu, dzv, d_lg, d_lb, d_sw, d_sb = _sgu_bwd("sgu_bwd", z, d_o[:, 0:GROUP], lg, lb, sgu_w[l], sb)
        dzh, dzbg, dzcg, d_scw = _shortconv_bwd("shortconv_bwd", z, d_o[:, GROUP:2 * GROUP], sc_w_full[l])
        dcc, d_cflg, d_cflb, d_cfb = _conformer_ln_bwd("conformer_ln_bwd", sv["cc"], d_o[:, 2 * GROUP:3 * GROUP],
                                                       cf_ln_g[l:l + 1], cf_ln_b[l:l + 1])
        dza, dzg, d_cfw = _conformer_conv_bwd("conformer_conv_bwd", z, dcc, cf_w_full[l])
        dzd, d_pw, d_psc = _pool_bwd("pool_bwd", z, d_o[:, 3 * GROUP:], pool_w[l], pool_scale[l:l + 1])
        dz = jnp.concatenate([dzu, dzv, dzh, dzbg, dzcg, dza, dzg, dzd], axis=1)
        g_in = _mm_tn("dw_in", sv["y1"], dz, "col", (1024, 1024, 1024))
        big.update(w_out=g_out, w_in=g_in)
        sibling = _sibling_start(f"rs_sibling_start_{l}_1", [halves(big[n]) for n in RS_GROUPS[1]])
        dy1 = _mm_nt_wide("dx_in", dz, wg["w_in"], "col", 1024, 512, after=(sibling[4],))
        dh, dhb, dg_mix = _rms_bwd("rms_mix_bwd", sv["h0"], norm_mix_g[l:l + 1], dy1, dh)
        if exchanges[1] is not None:
            finish_exchange(l + 1, 1, dh)
        exchanges[1] = start_exchange(l, 1, sibling, dh)

        small_grads[l] = dict(norm_mix_g=dg_mix, sgu_ln_g=d_lg, sgu_ln_b=d_lb, sgu_w=d_sw, sgu_b=d_sb,
                              sc_conv_w=d_scw, cf_conv_w=d_cfw, cf_conv_b=d_cfb, cf_ln_g=d_cflg, cf_ln_b=d_cflb,
                              pool_w=d_pw, pool_scale=d_psc, norm_ffn_g=dg_ffn, norm_ple_g=dg_ple)
    grad_x = dh[None]
    finish_exchange(0, 0, dh)
    finish_exchange(0, 1, dh)
    joined = _sibling_join_halves("rs_join", [grad_bufs[n] for n in BIG])
    grads = {n: joined[a].reshape(w[n].shape) for a, n in enumerate(BIG)}

    per_layer = [n for n in SMALL if n != "final_norm_g"]
    vecs = [small_grads[l][n] for l in range(depth) for n in per_layer] + [d_final_g, loss_part]
    packed, _ = _pack_rows(vecs)
    total = _sum8("sum_small", _allgather8("gather_small", [packed])[0]).reshape(-1)
    full_shapes = {n: (w[n].shape[1:] if n not in CHIP_SPLIT else (w[n].shape[1], GROUP)) for n in per_layer}
    pieces = _unpack(total, [full_shapes[n] for _ in range(depth) for n in per_layer] + [(d,), ()])
    loss = pieces[-1]
    grads["final_norm_g"] = pieces[-2]
    chip_off = (2 * xi + yi) * HEAD
    for j, n in enumerate(per_layer):
        g = jnp.stack([pieces[l * len(per_layer) + j] for l in range(depth)])
        if n in CHIP_SPLIT:
            g = lax.dynamic_slice_in_dim(g, chip_off, HEAD, axis=2)
        grads[n] = g

    delta, new_m, new_v = {}, {}, {}
    for n in BIG:
        shp = w[n].shape
        two_d = (shp[0] * shp[1], shp[2])
        dl, mn, vn = _adamw(f"adamw_{n}", w[n].reshape(two_d), grads[n].reshape(two_d), mom[n].reshape(two_d),
                            var[n].reshape(two_d))
        delta[n], new_m[n], new_v[n] = dl.reshape(shp), mn.reshape(shp), vn.reshape(shp)
    small_shapes = [w[n].shape for n in SMALL]
    pw, _ = _pack_rows([w[n] for n in SMALL])
    pg_, _ = _pack_rows([grads[n] for n in SMALL])
    pm, _ = _pack_rows([mom[n] for n in SMALL])
    pv, _ = _pack_rows([var[n] for n in SMALL])
    dl, mn, vn = _adamw("adamw_small", pw, pg_, pm, pv)
    for n, a, b, cc_ in zip(SMALL, _unpack(dl.reshape(-1), small_shapes), _unpack(mn.reshape(-1), small_shapes),
                            _unpack(vn.reshape(-1), small_shapes)):
        delta[n], new_m[n], new_v[n] = a, b, cc_

    return (loss, grad_x, *[grads[n] for n in WEIGHTS], *[delta[n] for n in WEIGHTS],
            *[new_m[n] for n in WEIGHTS], *[new_v[n] for n in WEIGHTS])
```

```python
import functools
import math

import jax
import jax.numpy as jnp
from jax import lax
from jax.experimental import pallas as pl
from jax.experimental.pallas import tpu as pltpu

F32 = jnp.float32
BF16 = jnp.bfloat16
MESH = pl.DeviceIdType.MESH

HEAD = 128
GROUP = 512
EPS = 1e-6
HALO = 32
CHUNK_ROWS = 128
POOL_LEVELS = 4
PACK_ROWS = 512

ADAM_LR = 0.001
ADAM_B1 = 0.9
ADAM_B2 = 0.999
ADAM_EPS = 1e-08
ADAM_WD = 0.01
ADAM_STEP = 10

VMEM_LIMIT = 56 * 1024 * 1024


def _cp(sem=None, vmem=VMEM_LIMIT):
    return pltpu.CompilerParams(dimension_semantics=sem, vmem_limit_bytes=vmem)


def _sigmoid(x):
    return 0.5 * jnp.tanh(0.5 * x) + 0.5


_GELU_K = math.sqrt(2.0 / math.pi)
_GELU_C = 0.044715


def _gelu(x):
    t = jnp.tanh(_GELU_K * (x + _GELU_C * x * x * x))
    return 0.5 * x * (1.0 + t)


def _gelu_grad(x):
    t = jnp.tanh(_GELU_K * (x + _GELU_C * x * x * x))
    return 0.5 * (1.0 + t) + 0.5 * x * (1.0 - t * t) * _GELU_K * (1.0 + 3.0 * _GELU_C * x * x)


def _mesh_pos():
    return lax.axis_index("x"), lax.axis_index("y"), lax.axis_index("c")


def _any_specs(n):
    return [pl.BlockSpec(memory_space=pl.ANY)] * n


def _allgather8(name, blocks):
    n = len(blocks)

    def body(*refs):
        ins, outs = refs[:n], refs[n:2 * n]
        send_sems, recv_sems, local_sems = refs[2 * n:]
        x, y, c = _mesh_pos()
        me, sibling = (x, y, c), (x, y, 1 - c)
        chips = [(1 - x, y), (x, 1 - y), (1 - x, 1 - y)]

        def slot(a, dev):
            return outs[a].at[4 * dev[0] + 2 * dev[1] + dev[2]]

        def copy(a, k, block, to, src=None):
            dst = slot(a, block)
            return pltpu.make_async_remote_copy(
                src_ref=dst if src is None else src, dst_ref=dst,
                send_sem=send_sems.at[7 * a + k], recv_sem=recv_sems.at[7 * a + k],
                device_id=to, device_id_type=MESH)

        mine, first, passed = [], [], []
        for a in range(n):
            cp = pltpu.make_async_copy(ins[a], slot(a, me), local_sems.at[a])
            cp.start()
            mine.append(cp)
            cps = [copy(a, 0, me, sibling, src=ins[a])]
            cps += [copy(a, 1 + j, me, (*chip, c), src=ins[a]) for j, chip in enumerate(chips)]
            for cp in cps:
                cp.start()
            first += cps
        for j, chip in enumerate(chips):
            for a in range(n):
                copy(a, 1 + j, (*chip, c), me).wait_recv()
                cp = copy(a, 4 + j, (*chip, c), sibling)
                cp.start()
                passed.append(cp)
        for a in range(n):
            copy(a, 0, sibling, me).wait_recv()
            for j, chip in enumerate(chips):
                copy(a, 4 + j, (*chip, 1 - c), me).wait_recv()
        for cp in first + passed:
            cp.wait_send()
        for cp in mine:
            cp.wait()

    return pl.pallas_call(
        body, name=name,
        out_shape=[jax.ShapeDtypeStruct((8,) + b.shape, b.dtype) for b in blocks],
        in_specs=_any_specs(n), out_specs=_any_specs(n),
        scratch_shapes=[pltpu.SemaphoreType.DMA((7 * n,)), pltpu.SemaphoreType.DMA((7 * n,)),
                        pltpu.SemaphoreType.DMA((n,))],
    )(*blocks)


HBM_SPEC = pl.BlockSpec(memory_space=pltpu.HBM)
SEM_SPEC = pl.BlockSpec(memory_space=pltpu.SEMAPHORE)
ANY_SPEC = pl.BlockSpec(memory_space=pl.ANY)
SPLIT_COPY = pltpu.CompilerParams(has_side_effects=pltpu.SideEffectType.DATAFLOW_SIDE_EFFECTING)


def _hbm(x):
    return pltpu.with_memory_space_constraint(x, pltpu.HBM)


def _other_chips(x, y):
    return [(1 - x, y), (x, 1 - y), (1 - x, 1 - y)]


def _dev_slot(ref, dev):
    return ref.at[4 * dev[0] + 2 * dev[1] + dev[2]]


def _cast_into_landing(name, w, layer, ax, chip_idx):
    _, r, cc = w.shape
    r2, c2 = (r // 2, cc) if ax == 0 else (r, cc // 2)
    tr = _row_tile(r2, 512 if c2 <= 1536 else 256)
    nt = r2 // tr

    def body(chip_ref, w_ref, o_ref):
        o_ref[...] = w_ref[...].astype(BF16)

    if ax == 0:
        in_spec = pl.BlockSpec((None, tr, c2), lambda hf, i, chip_ref: (layer, hf * nt + i, 0))
    else:
        in_spec = pl.BlockSpec((None, tr, c2), lambda hf, i, chip_ref: (layer, i, hf))
    return pl.pallas_call(
        body, name=name,
        grid_spec=pltpu.PrefetchScalarGridSpec(
            num_scalar_prefetch=1, grid=(2, nt), in_specs=[in_spec],
            out_specs=pl.BlockSpec((None, tr, c2), lambda hf, i, chip_ref: (2 * chip_ref[0] + hf, i, 0))),
        out_shape=jax.ShapeDtypeStruct((8, r2, c2), BF16), compiler_params=_cp(("parallel", "parallel")),
    )(chip_idx, w)


def _gather_ici_copies(lands, send_sems, recv_sems):
    x, y, c = _mesh_pos()
    pairs = []
    for a in range(len(lands)):
        for j, chip in enumerate(_other_chips(x, y)):
            def copy(dev):
                return pltpu.make_async_remote_copy(
                    src_ref=_dev_slot(lands[a], dev), dst_ref=_dev_slot(lands[a], dev),
                    send_sem=send_sems.at[3 * a + j], recv_sem=recv_sems.at[3 * a + j],
                    device_id=(*chip, c), device_id_type=MESH)
            pairs.append((copy((x, y, c)), copy((*chip, c))))
    return pairs


def _exchange_ici_copies(ins, lands, send_sems, recv_sems):
    x, y, c = _mesh_pos()
    pairs = []
    for a in range(len(ins)):
        for k, chip in enumerate(_other_chips(x, y)):
            there = 2 * chip[0] + chip[1]
            def copy(dst_entry):
                return pltpu.make_async_remote_copy(
                    src_ref=ins[a].at[there], dst_ref=lands[a].at[dst_entry],
                    send_sem=send_sems.at[3 * a + k], recv_sem=recv_sems.at[3 * a + k],
                    device_id=(*chip, c), device_id_type=MESH)
            pairs.append((copy(2 * x + y), copy(there)))
    return pairs


def _gather_start(name, lands, dep):
    n = len(lands)

    def body(*refs):
        token = refs[-1]
        for send, _ in _gather_ici_copies(refs[:n], refs[n + 1], refs[n + 2]):
            send.start()
        token[...] = jnp.zeros_like(token)

    out = pl.pallas_call(
        body, name=name,
        out_shape=(pltpu.SemaphoreType.DMA((3 * n,)), pltpu.SemaphoreType.DMA((3 * n,)),
                   *[pltpu.HBM(l.shape, l.dtype) for l in lands], jax.ShapeDtypeStruct((8, 128), F32)),
        in_specs=[HBM_SPEC] * n + [ANY_SPEC],
        out_specs=(SEM_SPEC, SEM_SPEC, *[HBM_SPEC] * n, pl.BlockSpec(memory_space=pltpu.VMEM)),
        input_output_aliases={a: 2 + a for a in range(n)},
        compiler_params=SPLIT_COPY,
    )(*[_hbm(l) for l in lands], dep)
    return out[0], out[1], list(out[2:2 + n]), out[-1]


def _gather_wait(name, send_sems, recv_sems, lands, after):
    n = len(lands)

    def body(*refs):
        for send, recv in _gather_ici_copies(refs[:n], refs[n], refs[n + 1]):
            send.wait_send()
            recv.wait_recv()

    out = pl.pallas_call(
        body, name=name,
        out_shape=[pltpu.HBM(l.shape, l.dtype) for l in lands],
        in_specs=[HBM_SPEC] * n + [SEM_SPEC, SEM_SPEC, ANY_SPEC],
        out_specs=[HBM_SPEC] * n,
        input_output_aliases={a: a for a in range(n)},
        compiler_params=SPLIT_COPY,
    )(*lands, send_sems, recv_sems, after)
    return list(out)


def _gather_share(name, lands):
    n = len(lands)

    def body(*refs):
        outs = refs[n:2 * n]
        send_sems, recv_sems = refs[2 * n:]
        x, y, c = _mesh_pos()

        def copy(a, j, dev):
            slot = _dev_slot(outs[a], dev)
            return pltpu.make_async_remote_copy(
                src_ref=slot, dst_ref=slot, send_sem=send_sems.at[3 * a + j], recv_sem=recv_sems.at[3 * a + j],
                device_id=(x, y, 1 - c), device_id_type=MESH)

        chips = _other_chips(x, y)
        for a in range(n):
            for j, chip in enumerate(chips):
                copy(a, j, (*chip, c)).start()
        for a in range(n):
            for j, chip in enumerate(chips):
                copy(a, j, (*chip, c)).wait_send()
                copy(a, j, (*chip, 1 - c)).wait_recv()

    return pl.pallas_call(
        body, name=name,
        out_shape=[jax.ShapeDtypeStruct(l.shape, l.dtype) for l in lands],
        in_specs=_any_specs(n), out_specs=_any_specs(n),
        input_output_aliases={a: a for a in range(n)},
        scratch_shapes=[pltpu.SemaphoreType.DMA((3 * n,)), pltpu.SemaphoreType.DMA((3 * n,))],
    )(*lands)


def _sibling_copies(ins, lands, send_sems, recv_sems):
    x, y, c = _mesh_pos()
    return [pltpu.make_async_remote_copy(
        src_ref=ins[a].at[j, 1 - c], dst_ref=lands[a].at[j],
        send_sem=send_sems.at[4 * a + j], recv_sem=recv_sems.at[4 * a + j],
        device_id=(x, y, 1 - c), device_id_type=MESH) for a in range(len(ins)) for j in range(4)]


def _sibling_start(name, grads):
    n = len(grads)

    def body(*refs):
        token = refs[-1]
        for cp in _sibling_copies(refs[:n], refs[n:2 * n], refs[2 * n], refs[2 * n + 1]):
            cp.start()
        token[...] = jnp.zeros_like(token)

    lands = [_hbm(lax.empty((4,) + g.shape[2:], g.dtype)) for g in grads]
    out = pl.pallas_call(
        body, name=name,
        out_shape=(pltpu.SemaphoreType.DMA((4 * n,)), pltpu.SemaphoreType.DMA((4 * n,)),
                   *[pltpu.HBM(g.shape, g.dtype) for g in grads], *[pltpu.HBM(l.shape, l.dtype) for l in lands],
                   jax.ShapeDtypeStruct((8, 128), F32)),
        in_specs=[HBM_SPEC] * (2 * n),
        out_specs=(SEM_SPEC, SEM_SPEC, *[HBM_SPEC] * (2 * n), pl.BlockSpec(memory_space=pltpu.VMEM)),
        input_output_aliases={i: 2 + i for i in range(2 * n)},
        compiler_params=SPLIT_COPY,
    )(*[_hbm(g) for g in grads], *lands)
    return out[0], out[1], list(out[2:2 + n]), list(out[2 + n:2 + 2 * n]), out[-1]


def _sibling_wait(name, send_sems, recv_sems, grads, lands, after):
    n = len(grads)

    def body(*refs):
        for cp in _sibling_copies(refs[:n], refs[n:2 * n], refs[2 * n], refs[2 * n + 1]):
            cp.wait_send()
            cp.wait_recv()

    out = pl.pallas_call(
        body, name=name,
        out_shape=[pltpu.HBM(g.shape, g.dtype) for g in grads] + [pltpu.HBM(l.shape, l.dtype) for l in lands],
        in_specs=[HBM_SPEC] * (2 * n) + [SEM_SPEC, SEM_SPEC, ANY_SPEC],
        out_specs=[HBM_SPEC] * (2 * n),
        input_output_aliases={i: i for i in range(2 * n)},
        compiler_params=SPLIT_COPY,
    )(*grads, *lands, send_sems, recv_sems, after)
    return list(out[:n]), list(out[n:])


def _chip_exchange_start(name, parts):
    n = len(parts)

    def body(*refs):
        ins, lands = refs[:n], refs[n:2 * n]
        token = refs[-1]
        for send, _ in _exchange_ici_copies(ins, lands, refs[2 * n], refs[2 * n + 1]):
            send.start()
        token[...] = jnp.zeros_like(token)

    lands = [_hbm(lax.empty(p.shape, p.dtype)) for p in parts]
    out = pl.pallas_call(
        body, name=name,
        out_shape=(pltpu.SemaphoreType.DMA((3 * n,)), pltpu.SemaphoreType.DMA((3 * n,)),
                   *[pltpu.HBM(p.shape, p.dtype) for p in parts], *[pltpu.HBM(p.shape, p.dtype) for p in parts],
                   jax.ShapeDtypeStruct((8, 128), F32)),
        in_specs=[HBM_SPEC] * (2 * n),
        out_specs=(SEM_SPEC, SEM_SPEC, *[HBM_SPEC] * (2 * n), pl.BlockSpec(memory_space=pltpu.VMEM)),
        input_output_aliases={i: 2 + i for i in range(2 * n)},
        compiler_params=SPLIT_COPY,
    )(*[_hbm(p) for p in parts], *lands)
    return out[0], out[1], list(out[2:2 + n]), list(out[2 + n:2 + 2 * n]), out[-1]


def _chip_exchange_wait(name, send_sems, recv_sems, parts, lands, after):
    n = len(parts)

    def body(*refs):
        ins, lands_in = refs[:n], refs[n:2 * n]
        for send, recv in _exchange_ici_copies(ins, lands_in, refs[2 * n], refs[2 * n + 1]):
            send.wait_send()
            recv.wait_recv()

    out = pl.pallas_call(
        body, name=name,
        out_shape=[pltpu.HBM(p.shape, p.dtype) for p in parts] * 2,
        in_specs=[HBM_SPEC] * (2 * n) + [SEM_SPEC, SEM_SPEC, ANY_SPEC],
        out_specs=[HBM_SPEC] * (2 * n),
        input_output_aliases={i: i for i in range(2 * n)},
        compiler_params=SPLIT_COPY,
    )(*parts, *lands, send_sems, recv_sems, after)
    return list(out[:n]), list(out[n:])


def _sibling_join_halves(name, bufs, after=()):
    n = len(bufs)
    depth = bufs[0].shape[0]

    def body(*refs):
        outs = refs[n + len(after):2 * n + len(after)]
        send_sems, recv_sems = refs[2 * n + len(after):]
        x, y, c = _mesh_pos()

        def copy(a, l, half):
            return pltpu.make_async_remote_copy(
                src_ref=outs[a].at[l, half], dst_ref=outs[a].at[l, half],
                send_sem=send_sems.at[depth * a + l], recv_sem=recv_sems.at[depth * a + l],
                device_id=(x, y, 1 - c), device_id_type=MESH)

        for a in range(n):
            for l in range(depth):
                copy(a, l, c).start()
        for a in range(n):
            for l in range(depth):
                copy(a, l, c).wait_send()
                copy(a, l, 1 - c).wait_recv()

    return pl.pallas_call(
        body, name=name, out_shape=[jax.ShapeDtypeStruct(b.shape, b.dtype) for b in bufs],
        in_specs=_any_specs(n + len(after)), out_specs=_any_specs(n),
        input_output_aliases={a: a for a in range(n)},
        scratch_shapes=[pltpu.SemaphoreType.DMA((depth * n,)), pltpu.SemaphoreType.DMA((depth * n,))],
    )(*bufs, *after)


def _row_tile(rows, want):
    t = 1
    while t * 2 <= min(rows, want):
        t *= 2
    while rows % t:
        t //= 2
    return t


def _add_halves(name, grads, recvd, c_idx):
    outs = []
    for a, (g, r) in enumerate(zip(grads, recvd)):
        _, _, r2, cc = g.shape
        tr = _row_tile(r2, 512)

        def body(c_ref, g_ref, r_ref, o_ref):
            o_ref[...] = (g_ref[...].astype(F32) + r_ref[...].astype(F32)).astype(BF16)

        outs.append(pl.pallas_call(
            body, name=f"{name}_{a}",
            grid_spec=pltpu.PrefetchScalarGridSpec(
                num_scalar_prefetch=1, grid=(4, r2 // tr),
                in_specs=[pl.BlockSpec((None, None, tr, cc), lambda j, i, c_ref: (j, c_ref[0], i, 0)),
                          pl.BlockSpec((None, tr, cc), lambda j, i, c_ref: (j, i, 0))],
                out_specs=pl.BlockSpec((None, tr, cc), lambda j, i, c_ref: (j, i, 0))),
            out_shape=jax.ShapeDtypeStruct(r.shape, BF16), compiler_params=_cp(("parallel", "parallel")),
        )(c_idx, g, r))
    return outs


def _sum_parts(name, parts, recvd, bufs, layer, where):
    outs = []
    for a, (p, r, buf) in enumerate(zip(parts, recvd, bufs)):
        _, r2, cc = p.shape
        tr = _row_tile(r2, 256)

        def body(where_ref, own_ref, r1_ref, r2_ref, r3_ref, buf_ref, o_ref):
            acc = own_ref[...].astype(F32)
            for ref in (r1_ref, r2_ref, r3_ref):
                acc = acc + ref[...].astype(F32)
            o_ref[...] = acc

        def entry(k):
            return pl.BlockSpec((None, tr, cc), lambda i, where_ref: (where_ref[k], i, 0))

        outs.append(pl.pallas_call(
            body, name=f"{name}_{a}",
            grid_spec=pltpu.PrefetchScalarGridSpec(
                num_scalar_prefetch=1, grid=(r2 // tr,),
                in_specs=[entry(0), entry(1), entry(2), entry(3), ANY_SPEC],
                out_specs=pl.BlockSpec((None, None, tr, cc), lambda i, where_ref: (layer, where_ref[4], i, 0))),
            out_shape=jax.ShapeDtypeStruct(buf.shape, F32), input_output_aliases={5: 0},
            compiler_params=_cp(("parallel",)),
        )(where, p, r, r, r, buf))
    return outs


def _sum8(name, gathered):
    _, r, cc = gathered.shape
    tr = _row_tile(r, 512)

    def body(g_ref, o_ref):
        acc = g_ref[0]
        for i in range(1, 8):
            acc = acc + g_ref[i]
        o_ref[...] = acc

    return pl.pallas_call(
        body, name=name, grid=(r // tr,),
        in_specs=[pl.BlockSpec((8, tr, cc), lambda i: (0, i, 0))],
        out_specs=pl.BlockSpec((tr, cc), lambda i: (i, 0)),
        out_shape=jax.ShapeDtypeStruct((r, cc), F32), compiler_params=_cp(("parallel",)),
    )(gathered)


def _adamw(name, w, g, m, v):
    r, cc = w.shape
    tr = _row_tile(r, max(8, (256 * 1024) // cc))

    def body(w_ref, g_ref, m_ref, v_ref, d_ref, mo_ref, vo_ref):
        gg = g_ref[...]
        mn = ADAM_B1 * m_ref[...] + (1.0 - ADAM_B1) * gg
        vn = ADAM_B2 * v_ref[...] + (1.0 - ADAM_B2) * (gg * gg)
        m_hat = mn / (1.0 - ADAM_B1 ** ADAM_STEP)
        v_hat = vn / (1.0 - ADAM_B2 ** ADAM_STEP)
        d_ref[...] = -ADAM_LR * (m_hat / (jnp.sqrt(v_hat) + ADAM_EPS) + ADAM_WD * w_ref[...])
        mo_ref[...] = mn
        vo_ref[...] = vn

    spec = pl.BlockSpec((tr, cc), lambda i: (i, 0))
    return pl.pallas_call(
        body, name=name, grid=(r // tr,), in_specs=[spec] * 4, out_specs=[spec] * 3,
        out_shape=[jax.ShapeDtypeStruct(w.shape, F32)] * 3, compiler_params=_cp(("parallel",)),
    )(w, g, m, v)


def _rms_fwd(name, h, g, after=()):
    s, d = h.shape
    tr = _row_tile(s, 512)

    def body(h_ref, g_ref, *rest):
        y_ref = rest[-1]
        xv = h_ref[...]
        r = lax.rsqrt(jnp.mean(xv * xv, axis=-1, keepdims=True) + EPS)
        y_ref[...] = (xv * r * g_ref[...]).astype(BF16)

    return pl.pallas_call(
        body, name=name, grid=(s // tr,),
        in_specs=[pl.BlockSpec((tr, d), lambda i: (i, 0)), pl.BlockSpec((1, d), lambda i: (0, 0))]
        + [ANY_SPEC] * len(after),
        out_specs=pl.BlockSpec((tr, d), lambda i: (i, 0)),
        out_shape=jax.ShapeDtypeStruct((s, d), BF16), compiler_params=_cp(("parallel",)),
    )(h, g, *after)


def _rms_bwd_rows(xv, gv, dy):
    d = xv.shape[-1]
    r = lax.rsqrt(jnp.mean(xv * xv, axis=-1, keepdims=True) + EPS)
    dxn = dy * gv
    proj = jnp.sum(dxn * xv, axis=-1, keepdims=True) * (1.0 / d)
    dx = r * dxn - xv * (r * r * r) * proj
    return dx, dy * xv * r


def _rms_bwd(name, h, g, dy, dres):
    s, d = h.shape
    tr = _row_tile(s, 256)

    def body(h_ref, g_ref, dy_ref, dres_ref, dh_ref, dhb_ref, dg_ref):
        dx, dgp = _rms_bwd_rows(h_ref[...], g_ref[...], dy_ref[...].astype(F32))
        dh = dres_ref[...] + dx
        dh_ref[...] = dh
        dhb_ref[...] = dh.astype(BF16)

        @pl.when(pl.program_id(0) == 0)
        def _():
            dg_ref[...] = jnp.zeros_like(dg_ref)

        dg_ref[...] += jnp.sum(dgp, axis=0, keepdims=True)

    row = pl.BlockSpec((tr, d), lambda i: (i, 0))
    vec = pl.BlockSpec((1, d), lambda i: (0, 0))
    return pl.pallas_call(
        body, name=name, grid=(s // tr,), in_specs=[row, vec, row, row], out_specs=[row, row, vec],
        out_shape=[jax.ShapeDtypeStruct((s, d), F32), jax.ShapeDtypeStruct((s, d), BF16),
                   jax.ShapeDtypeStruct((1, d), F32)],
        compiler_params=_cp(("arbitrary",)),
    )(h, g, dy, dres)


def _loss_head(name, h, g, target):
    s, d = h.shape
    tr = _row_tile(s, 256)

    def body(h_ref, g_ref, t_ref, loss_ref, dh_ref, dhb_ref, dg_ref):
        xv = h_ref[...]
        gv = g_ref[...]
        r = lax.rsqrt(jnp.mean(xv * xv, axis=-1, keepdims=True) + EPS)
        err = xv * r * gv - t_ref[...]
        part = 0.5 * jnp.sum(jnp.sum(err * err, axis=-1, keepdims=True) * (1.0 / d), axis=0, keepdims=True)
        dx, dgp = _rms_bwd_rows(xv, gv, err * (1.0 / d))
        dh_ref[...] = dx
        dhb_ref[...] = dx.astype(BF16)

        @pl.when(pl.program_id(0) == 0)
        def _():
            dg_ref[...] = jnp.zeros_like(dg_ref)
            loss_ref[...] = jnp.zeros_like(loss_ref)

        dg_ref[...] += jnp.sum(dgp, axis=0, keepdims=True)
        loss_ref[...] += part

    row = pl.BlockSpec((tr, d), lambda i: (i, 0))
    vec = pl.BlockSpec((1, d), lambda i: (0, 0))
    one = pl.BlockSpec((1, 1), lambda i: (0, 0))
    return pl.pallas_call(
        body, name=name, grid=(s // tr,), in_specs=[row, vec, row], out_specs=[one, row, row, vec],
        out_shape=[jax.ShapeDtypeStruct((1, 1), F32), jax.ShapeDtypeStruct((s, d), F32),
                   jax.ShapeDtypeStruct((s, d), BF16), jax.ShapeDtypeStruct((1, d), F32)],
        compiler_params=_cp(("arbitrary",)),
    )(h, g, target)


NN = (((1,), (0,)), ((), ()))
NT = (((1,), (1,)), ((), ()))
TN = (((0,), (0,)), ((), ()))


def _matmul(name, grid, operands, in_specs, pairs, n_acc, acc_shape, epilogue, out_specs, out_shape, after=()):
    operands = list(operands) + list(after)
    in_specs = list(in_specs) + [ANY_SPEC] * len(after)
    n_in = len(operands)
    n_out = len(out_shape)
    nk = grid[2]

    def body(*refs):
        ins, outs, accs = refs[:n_in], refs[n_in:n_in + n_out], refs[n_in + n_out:]
        k = pl.program_id(2)

        @pl.when(k == 0)
        def _():
            for acc in accs:
                acc[...] = jnp.zeros_like(acc)

        for li, ri, ai, dims in pairs:
            accs[ai][...] += lax.dot_general(ins[li][...], ins[ri][...], dims, preferred_element_type=F32)

        @pl.when(k == nk - 1)
        def _():
            epilogue([acc[...] for acc in accs], ins, outs)

    return pl.pallas_call(
        body, name=name, grid=grid, in_specs=in_specs, out_specs=out_specs, out_shape=out_shape,
        scratch_shapes=[pltpu.VMEM(acc_shape, F32)] * n_acc,
        compiler_params=_cp(("parallel", "parallel", "arbitrary")),
    )(*operands)


def _ep_store(dtype):
    def ep(accs, ins, outs):
        outs[0][...] = accs[0].astype(dtype)
    return ep


def _w_spec_nn(wg, kind, tk, tn):
    _, a, b = wg.shape
    if kind == "col":
        kph, bps = a // tk, b // tn
        return pl.BlockSpec((None, tk, tn), lambda i, j, k: (2 * (j // bps) + k // kph, k % kph, j % bps))
    kps, nph = a // tk, b // tn
    return pl.BlockSpec((None, tk, tn), lambda i, j, k: (2 * (k // kps) + j // nph, k % kps, j % nph))


def _w_spec_nt(wg, kind, to, tc):
    _, a, b = wg.shape
    if kind == "col":
        oph, cps = a // to, b // tc
        return pl.BlockSpec((None, to, tc), lambda i, jo, kc: (2 * (kc // cps) + jo // oph, jo % oph, kc % cps))
    ops, cph = a // to, b // tc
    return pl.BlockSpec((None, to, tc), lambda i, jo, kc: (2 * (jo // ops) + kc // cph, jo % ops, kc % cph))


def _mm_nn(name, x, wg, kind, tiles, epilogue=None, extra=(), extra_specs=(), out_specs=None, out_shape=None):
    m, kdim = x.shape
    _, a, b = wg.shape
    n = 4 * b if kind == "col" else 2 * b
    tm, tn, tk = tiles
    tm = min(tm, m)
    grid = (m // tm, n // tn, kdim // tk)
    in_specs = [pl.BlockSpec((tm, tk), lambda i, j, k: (i, k)), _w_spec_nn(wg, kind, tk, tn)] + list(extra_specs)
    if out_shape is None:
        out_shape = [jax.ShapeDtypeStruct((m, n), F32)]
        out_specs = [pl.BlockSpec((tm, tn), lambda i, j, k: (i, j))]
        epilogue = _ep_store(F32)
    return _matmul(name, grid, [x, wg] + list(extra), in_specs, [(0, 1, 0, NN)], 1, (tm, tn), epilogue,
                   out_specs, out_shape)


SUB_COLS = 256


def _matmul_wide(name, grid, operands, in_specs, groups, tn, epilogue, out_specs, out_shape, after=()):
    operands = list(operands) + list(after)
    in_specs = list(in_specs) + [ANY_SPEC] * len(after)
    n_in = len(operands)

    def body(*refs):
        ins, outs = refs[:n_in], refs[n_in:]
        for s0 in range(0, tn, SUB_COLS):
            cols = slice(s0, min(s0 + SUB_COLS, tn))
            accs = []
            for group in groups:
                acc = None
                for li, (c0, cw), ri, dims in group:
                    rhs = ins[ri][:, cols] if dims == NN else ins[ri][cols, :]
                    part = lax.dot_general(ins[li][:, c0:c0 + cw], rhs, dims, preferred_element_type=F32)
                    acc = part if acc is None else acc + part
                accs.append(acc)
            epilogue(accs, ins, outs, cols)

    return pl.pallas_call(
        body, name=name, grid=grid, in_specs=in_specs, out_specs=out_specs, out_shape=out_shape,
        compiler_params=_cp(("parallel", "parallel")),
    )(*operands)


def _wide_store(dtype):
    def ep(accs, ins, outs, cols):
        outs[0][:, cols] = accs[0].astype(dtype)
    return ep


def _wide_nn_weights(wg, kind, tn, first):
    _, a, b = wg.shape
    per = b // tn
    if kind == "col":
        specs = [pl.BlockSpec((None, a, tn), lambda i, j, hf=hf: (2 * (j // per) + hf, 0, j % per)) for hf in range(2)]
    else:
        specs = [pl.BlockSpec((None, a, tn), lambda i, j, ch=ch: (2 * ch + j // per, 0, j % per)) for ch in range(4)]
    dots = [(0, (q * a, a), first + q, NN) for q in range(len(specs))]
    return [wg] * len(specs), specs, dots


def _wide_nt_weights(wg, kind, to, first):
    _, a, b = wg.shape
    per = a // to
    if kind == "col":
        specs = [pl.BlockSpec((None, to, b), lambda i, j, ch=ch: (2 * ch + j // per, j % per, 0)) for ch in range(4)]
    else:
        specs = [pl.BlockSpec((None, to, b), lambda i, j, hf=hf: (2 * (j // per) + hf, j % per, 0)) for hf in range(2)]
    dots = [(0, (q * b, b), first + q, NT) for q in range(len(specs))]
    return [wg] * len(specs), specs, dots


def _mm_nn_wide(name, x, wg, kind, tm, tn, epilogue=None, extra=(), extra_specs=(), out_specs=None, out_shape=None):
    m, kdim = x.shape
    _, a, b = wg.shape
    n = 4 * b if kind == "col" else 2 * b
    tm = min(tm, m)
    ws, wspecs, dots = _wide_nn_weights(wg, kind, tn, 1)
    if out_shape is None:
        out_shape = [jax.ShapeDtypeStruct((m, n), F32)]
        out_specs = [pl.BlockSpec((tm, tn), lambda i, j: (i, j))]
        epilogue = _wide_store(F32)
    return _matmul_wide(name, (m // tm, n // tn), [x] + ws + list(extra),
                        [pl.BlockSpec((tm, kdim), lambda i, j: (i, 0))] + wspecs + list(extra_specs),
                        [dots], tn, epilogue, out_specs, out_shape)


def _mm_nt_wide(name, dy, wg, kind, tm, to, out_dtype=BF16, after=()):
    m, n = dy.shape
    _, a, b = wg.shape
    kdim = 2 * a if kind == "col" else 4 * a
    tm = min(tm, m)
    ws, wspecs, dots = _wide_nt_weights(wg, kind, to, 1)
    return _matmul_wide(name, (m // tm, kdim // to), [dy] + ws,
                        [pl.BlockSpec((tm, n), lambda i, j: (i, 0))] + wspecs, [dots], to, _wide_store(out_dtype),
                        [pl.BlockSpec((tm, to), lambda i, j: (i, j))],
                        [jax.ShapeDtypeStruct((m, kdim), out_dtype)], after=after)[0]


def _mm_tn(name, a, dy, kind, tiles):
    m, kdim = a.shape
    _, n = dy.shape
    tr, tn, tmm = tiles
    tmm = min(tmm, m)
    grid = (kdim // tr, n // tn, m // tmm)
    in_specs = [pl.BlockSpec((tmm, tr), lambda r, j, k: (k, r)), pl.BlockSpec((tmm, tn), lambda r, j, k: (k, j))]
    if kind == "col":
        ns = n // 4
        bps = ns // tn
        out_shape = [jax.ShapeDtypeStruct((4, kdim, ns), BF16)]
        out_specs = [pl.BlockSpec((None, tr, tn), lambda r, j, k: (j // bps, r, j % bps))]
    else:
        rs = kdim // 4
        rps = rs // tr
        out_shape = [jax.ShapeDtypeStruct((4, rs, n), BF16)]
        out_specs = [pl.BlockSpec((None, tr, tn), lambda r, j, k: (r // rps, r % rps, j))]
    return _matmul(name, grid, [a, dy], in_specs, [(0, 1, 0, TN)], 1, (tr, tn), _ep_store(BF16),
                   out_specs, out_shape)[0]


def _zero_halo(pad_ref, s):
    z = jnp.zeros((HALO, pad_ref.shape[1]), F32)
    pad_ref[pl.ds(0, HALO), :] = z
    pad_ref[pl.ds(HALO + s, HALO), :] = z


def _window(pad_ref, r0, rows):
    return pad_ref[pl.ds(r0, rows + 2 * HALO), :]


def _delayed(win, k, rows):
    if k == 0:
        return win[HALO:HALO + rows]
    return pltpu.roll(win, k, axis=0)[HALO:HALO + rows]


def _advanced(win, k, rows):
    if k == 0:
        return win[HALO:HALO + rows]
    return pltpu.roll(win, win.shape[0] - k, axis=0)[HALO:HALO + rows]


def _fold8(x):
    return jnp.sum(x.reshape(x.shape[0] // 8, 8, x.shape[1]), axis=0)


def _chunks(s):
    rows = min(CHUNK_ROWS, s)
    return rows, s // rows


def _col_spec(s, first_block):
    return pl.BlockSpec((s, HEAD), lambda j: (0, first_block + j))


def _sgu_fwd(name, z, ln_g, ln_b, w, b):
    s = z.shape[0]
    tr = _row_tile(s, 1024)
    nh = GROUP // HEAD

    def body(u_ref, v_ref, lg_ref, lb_ref, w_ref, b_ref, o_ref):
        row = lax.broadcasted_iota(jnp.int32, (HEAD, HEAD), 0)
        col = lax.broadcasted_iota(jnp.int32, (HEAD, HEAD), 1)
        wm = jnp.where(row >= col, w_ref[...], 0.0).astype(BF16)
        for ck in range(tr // HEAD):
            rs = pl.ds(ck * HEAD, HEAD)
            u = _gelu(u_ref[rs, :])
            v = _gelu(v_ref[rs, :])
            mu = jnp.mean(v, axis=-1, keepdims=True)
            vc = v - mu
            var = jnp.mean(vc * vc, axis=-1, keepdims=True)
            vln = vc * lax.rsqrt(var + EPS) * lg_ref[...] + lb_ref[...]
            sp = jnp.dot(wm, vln.astype(BF16), preferred_element_type=F32) + b_ref[...]
            o_ref[rs, :] = (u * sp).astype(BF16)

    head_vec = pl.BlockSpec((None, 1, HEAD), lambda h, i: (h, 0, 0))
    return pl.pallas_call(
        body, name=name, grid=(nh, s // tr),
        in_specs=[pl.BlockSpec((tr, HEAD), lambda h, i: (i, h)), pl.BlockSpec((tr, HEAD), lambda h, i: (i, nh + h)),
                  head_vec, head_vec, pl.BlockSpec((None, HEAD, HEAD), lambda h, i: (h, 0, 0)),
                  pl.BlockSpec((None, HEAD, 1), lambda h, i: (h, 0, 0))],
        out_specs=pl.BlockSpec((tr, HEAD), lambda h, i: (i, h)),
        out_shape=jax.ShapeDtypeStruct((s, GROUP), BF16), compiler_params=_cp(("parallel", "parallel")),
    )(z, z, ln_g, ln_b, w, b)


def _sgu_bwd(name, z, d_o, ln_g, ln_b, w, b):
    s = z.shape[0]
    tr = _row_tile(s, 1024)
    nh = GROUP // HEAD

    def body(u_ref, v_ref, do_ref, lg_ref, lb_ref, w_ref, b_ref, du_ref, dv_ref, dlg_ref, dlb_ref, dw_ref, db_ref,
             dsp_acc):
        row = lax.broadcasted_iota(jnp.int32, (HEAD, HEAD), 0)
        col = lax.broadcasted_iota(jnp.int32, (HEAD, HEAD), 1)
        tril = row >= col
        wm = jnp.where(tril, w_ref[...], 0.0).astype(BF16)
        i = pl.program_id(1)

        @pl.when(i == 0)
        def _():
            dlg_ref[...] = jnp.zeros_like(dlg_ref)
            dlb_ref[...] = jnp.zeros_like(dlb_ref)
            dw_ref[...] = jnp.zeros_like(dw_ref)
            dsp_acc[...] = jnp.zeros_like(dsp_acc)

        dlg = jnp.zeros((1, HEAD), F32)
        dlb = jnp.zeros((1, HEAD), F32)
        dw = jnp.zeros((HEAD, HEAD), F32)
        dsp_sum = jnp.zeros((HEAD, HEAD), F32)
        for ck in range(tr // HEAD):
            rs = pl.ds(ck * HEAD, HEAD)
            zu = u_ref[rs, :]
            zv = v_ref[rs, :]
            u = _gelu(zu)
            v = _gelu(zv)
            mu = jnp.mean(v, axis=-1, keepdims=True)
            vc = v - mu
            var = jnp.mean(vc * vc, axis=-1, keepdims=True)
            rstd = lax.rsqrt(var + EPS)
            xh = vc * rstd
            vln = (xh * lg_ref[...] + lb_ref[...]).astype(BF16)
            sp = jnp.dot(wm, vln, preferred_element_type=F32) + b_ref[...]
            d_oa = do_ref[rs, :].astype(F32)
            du = d_oa * sp
            dsp = d_oa * u
            dsp_b = dsp.astype(BF16)
            dvln = lax.dot_general(wm, dsp_b, TN, preferred_element_type=F32)
            dw = dw + lax.dot_general(dsp_b, vln, NT, preferred_element_type=F32)
            dsp_sum = dsp_sum + dsp
            dlg = dlg + jnp.sum(dvln * xh, axis=0, keepdims=True)
            dlb = dlb + jnp.sum(dvln, axis=0, keepdims=True)
            dxh = dvln * lg_ref[...]
            dv = rstd * (dxh - jnp.mean(dxh, axis=-1, keepdims=True)
                         - xh * jnp.mean(dxh * xh, axis=-1, keepdims=True))
            du_ref[rs, :] = (du * _gelu_grad(zu)).astype(BF16)
            dv_ref[rs, :] = (dv * _gelu_grad(zv)).astype(BF16)
        dlg_ref[...] += dlg
        dlb_ref[...] += dlb
        dw_ref[...] += jnp.where(tril, dw, 0.0)
        dsp_acc[...] += dsp_sum

        @pl.when(i == pl.num_programs(1) - 1)
        def _():
            db_ref[...] = jnp.sum(dsp_acc[...], axis=1, keepdims=True)

    head_vec = pl.BlockSpec((None, 1, HEAD), lambda h, i: (h, 0, 0))
    head_mat = pl.BlockSpec((None, HEAD, HEAD), lambda h, i: (h, 0, 0))
    head_col = pl.BlockSpec((None, HEAD, 1), lambda h, i: (h, 0, 0))
    return pl.pallas_call(
        body, name=name, grid=(nh, s // tr),
        in_specs=[pl.BlockSpec((tr, HEAD), lambda h, i: (i, h)), pl.BlockSpec((tr, HEAD), lambda h, i: (i, nh + h)),
                  pl.BlockSpec((tr, HEAD), lambda h, i: (i, h)), head_vec, head_vec, head_mat, head_col],
        out_specs=[pl.BlockSpec((tr, HEAD), lambda h, i: (i, h)), pl.BlockSpec((tr, HEAD), lambda h, i: (i, h)),
                   head_vec, head_vec, head_mat, head_col],
        out_shape=[jax.ShapeDtypeStruct((s, GROUP), BF16), jax.ShapeDtypeStruct((s, GROUP), BF16),
                   jax.ShapeDtypeStruct((nh, 1, HEAD), F32), jax.ShapeDtypeStruct((nh, 1, HEAD), F32),
                   jax.ShapeDtypeStruct((nh, HEAD, HEAD), F32), jax.ShapeDtypeStruct((nh, HEAD, 1), F32)],
        scratch_shapes=[pltpu.VMEM((HEAD, HEAD), F32)],
        compiler_params=_cp(("parallel", "arbitrary")),
    )(z, z, d_o, ln_g, ln_b, w, b)


def _shortconv_fwd(name, z, w):
    s = z.shape[0]
    kw = w.shape[0]
    rows, nchunk = _chunks(s)
    nb = GROUP // HEAD

    def body(h_ref, bg_ref, cg_ref, w_ref, o_ref, pad):
        _zero_halo(pad, s)

        def fill(ci, carry):
            r0 = pl.multiple_of(ci * rows, rows)
            pad[pl.ds(pl.multiple_of(HALO + r0, 8), rows), :] = cg_ref[pl.ds(r0, rows), :] * h_ref[pl.ds(r0, rows), :]
            return carry

        lax.fori_loop(0, nchunk, fill, 0)

        def step(ci, carry):
            r0 = pl.multiple_of(ci * rows, rows)
            win = _window(pad, r0, rows)
            cv = jnp.zeros((rows, HEAD), F32)
            for k in range(kw):
                cv = cv + w_ref[k:k + 1, :] * _delayed(win, kw - 1 - k, rows)
            o_ref[pl.ds(r0, rows), :] = (bg_ref[pl.ds(r0, rows), :] * cv).astype(BF16)
            return carry

        lax.fori_loop(0, nchunk, step, 0)

    return pl.pallas_call(
        body, name=name, grid=(nb,),
        in_specs=[_col_spec(s, 8), _col_spec(s, 12), _col_spec(s, 16), pl.BlockSpec((kw, HEAD), lambda j: (0, j))],
        out_specs=_col_spec(s, 0),
        out_shape=jax.ShapeDtypeStruct((s, GROUP), BF16),
        scratch_shapes=[pltpu.VMEM((s + 2 * HALO, HEAD), F32)],
        compiler_params=_cp(("parallel",)),
    )(z, z, z, w)


def _shortconv_bwd(name, z, d_o, w):
    s = z.shape[0]
    kw = w.shape[0]
    rows, nchunk = _chunks(s)
    nb = GROUP // HEAD

    def body(h_ref, bg_ref, cg_ref, do_ref, w_ref, dh_ref, dbg_ref, dcg_ref, dw_ref, pad_q, pad_d, acc):
        _zero_halo(pad_q, s)
        _zero_halo(pad_d, s)
        acc[...] = jnp.zeros_like(acc)

        def fill(ci, carry):
            r0 = pl.multiple_of(ci * rows, rows)
            rs = pl.ds(r0, rows)
            ps = pl.ds(pl.multiple_of(HALO + r0, 8), rows)
            pad_q[ps, :] = cg_ref[rs, :] * h_ref[rs, :]
            pad_d[ps, :] = do_ref[rs, :].astype(F32) * bg_ref[rs, :]
            return carry

        lax.fori_loop(0, nchunk, fill, 0)

        def step(ci, carry):
            r0 = pl.multiple_of(ci * rows, rows)
            rs = pl.ds(r0, rows)
            wq = _window(pad_q, r0, rows)
            wd = _window(pad_d, r0, rows)
            dcv = wd[HALO:HALO + rows]
            cv = jnp.zeros((rows, HEAD), F32)
            dq = jnp.zeros((rows, HEAD), F32)
            for k in range(kw):
                qk = _delayed(wq, kw - 1 - k, rows)
                cv = cv + w_ref[k:k + 1, :] * qk
                dq = dq + w_ref[k:k + 1, :] * _advanced(wd, kw - 1 - k, rows)
                acc[k] += _fold8(dcv * qk)
            dbg_ref[rs, :] = (do_ref[rs, :].astype(F32) * cv).astype(BF16)
            dcg_ref[rs, :] = (dq * h_ref[rs, :]).astype(BF16)
            dh_ref[rs, :] = (dq * cg_ref[rs, :]).astype(BF16)
            return carry

        lax.fori_loop(0, nchunk, step, 0)
        for k in range(kw):
            dw_ref[k:k + 1, :] = jnp.sum(acc[k], axis=0, keepdims=True)

    col = _col_spec(s, 0)
    return pl.pallas_call(
        body, name=name, grid=(nb,),
        in_specs=[_col_spec(s, 8), _col_spec(s, 12), _col_spec(s, 16), col, pl.BlockSpec((kw, HEAD), lambda j: (0, j))],
        out_specs=[col, col, col, pl.BlockSpec((kw, HEAD), lambda j: (0, j))],
        out_shape=[jax.ShapeDtypeStruct((s, GROUP), BF16)] * 3 + [jax.ShapeDtypeStruct((kw, GROUP), F32)],
        scratch_shapes=[pltpu.VMEM((s + 2 * HALO, HEAD), F32), pltpu.VMEM((s + 2 * HALO, HEAD), F32),
                        pltpu.VMEM((kw, 8, HEAD), F32)],
        compiler_params=_cp(("parallel",)),
    )(z, z, z, d_o, w)


def _conformer_conv_fwd(name, z, w, bias):
    s = z.shape[0]
    kw = w.shape[0]
    rows, nchunk = _chunks(s)
    nb = GROUP // HEAD

    def body(a_ref, g_ref, w_ref, b_ref, o_ref, pad):
        _zero_halo(pad, s)

        def fill(ci, carry):
            r0 = pl.multiple_of(ci * rows, rows)
            rs = pl.ds(r0, rows)
            pad[pl.ds(pl.multiple_of(HALO + r0, 8), rows), :] = a_ref[rs, :] * _sigmoid(g_ref[rs, :])
            return carry

        lax.fori_loop(0, nchunk, fill, 0)

        def step(ci, carry):
            r0 = pl.multiple_of(ci * rows, rows)
            win = _window(pad, r0, rows)
            cc = jnp.zeros((rows, HEAD), F32)
            for k in range(kw):
                cc = cc + w_ref[k:k + 1, :] * _delayed(win, kw - 1 - k, rows)
            o_ref[pl.ds(r0, rows), :] = cc + b_ref[...]
            return carry

        lax.fori_loop(0, nchunk, step, 0)

    return pl.pallas_call(
        body, name=name, grid=(nb,),
        in_specs=[_col_spec(s, 20), _col_spec(s, 24), pl.BlockSpec((kw, HEAD), lambda j: (0, j)),
                  pl.BlockSpec((1, HEAD), lambda j: (0, j))],
        out_specs=_col_spec(s, 0),
        out_shape=jax.ShapeDtypeStruct((s, GROUP), F32),
        scratch_shapes=[pltpu.VMEM((s + 2 * HALO, HEAD), F32)],
        compiler_params=_cp(("parallel",)),
    )(z, z, w, bias)


def _ln_rows(cc, g, b):
    mu = jnp.mean(cc, axis=-1, keepdims=True)
    xc = cc - mu
    var = jnp.mean(xc * xc, axis=-1, keepdims=True)
    rstd = lax.rsqrt(var + EPS)
    xh = xc * rstd
    return xh, rstd, xh * g + b


def _conformer_ln_fwd(name, cc, g, b):
    s, d = cc.shape
    tr = _row_tile(s, 512)

    def body(c_ref, g_ref, b_ref, o_ref):
        _, _, l = _ln_rows(c_ref[...], g_ref[...], b_ref[...])
        o_ref[...] = (l * _sigmoid(l)).astype(BF16)

    row = pl.BlockSpec((tr, d), lambda i: (i, 0))
    vec = pl.BlockSpec((1, d), lambda i: (0, 0))
    return pl.pallas_call(
        body, name=name, grid=(s // tr,), in_specs=[row, vec, vec], out_specs=row,
        out_shape=jax.ShapeDtypeStruct((s, d), BF16), compiler_params=_cp(("parallel",)),
    )(cc, g, b)


def _conformer_ln_bwd(name, cc, d_o, g, b):
    s, d = cc.shape
    tr = _row_tile(s, 512)

    def body(c_ref, do_ref, g_ref, b_ref, dcc_ref, dg_ref, db_ref, dcb_ref):
        xh, rstd, l = _ln_rows(c_ref[...], g_ref[...], b_ref[...])
        sg = _sigmoid(l)
        dl = do_ref[...].astype(F32) * sg * (1.0 + l * (1.0 - sg))
        dxh = dl * g_ref[...]
        dcc = rstd * (dxh - jnp.mean(dxh, axis=-1, keepdims=True) - xh * jnp.mean(dxh * xh, axis=-1, keepdims=True))
        dcc_ref[...] = dcc

        @pl.when(pl.program_id(0) == 0)
        def _():
            dg_ref[...] = jnp.zeros_like(dg_ref)
            db_ref[...] = jnp.zeros_like(db_ref)
            dcb_ref[...] = jnp.zeros_like(dcb_ref)

        dg_ref[...] += jnp.sum(dl * xh, axis=0, keepdims=True)
        db_ref[...] += jnp.sum(dl, axis=0, keepdims=True)
        dcb_ref[...] += jnp.sum(dcc, axis=0, keepdims=True)

    row = pl.BlockSpec((tr, d), lambda i: (i, 0))
    vec = pl.BlockSpec((1, d), lambda i: (0, 0))
    return pl.pallas_call(
        body, name=name, grid=(s // tr,), in_specs=[row, row, vec, vec], out_specs=[row, vec, vec, vec],
        out_shape=[jax.ShapeDtypeStruct((s, d), F32)] + [jax.ShapeDtypeStruct((1, d), F32)] * 3,
        compiler_params=_cp(("arbitrary",)),
    )(cc, d_o, g, b)


def _conformer_conv_bwd(name, z, dcc, w):
    s = z.shape[0]
    kw = w.shape[0]
    rows, nchunk = _chunks(s)
    nb = GROUP // HEAD

    def body(a_ref, g_ref, d_ref, w_ref, da_ref, dg_ref, dw_ref, pad_h, pad_d, acc):
        _zero_halo(pad_h, s)
        _zero_halo(pad_d, s)
        acc[...] = jnp.zeros_like(acc)

        def fill(ci, carry):
            r0 = pl.multiple_of(ci * rows, rows)
            rs = pl.ds(r0, rows)
            ps = pl.ds(pl.multiple_of(HALO + r0, 8), rows)
            pad_h[ps, :] = a_ref[rs, :] * _sigmoid(g_ref[rs, :])
            pad_d[ps, :] = d_ref[rs, :]
            return carry

        lax.fori_loop(0, nchunk, fill, 0)

        def step(ci, carry):
            r0 = pl.multiple_of(ci * rows, rows)
            rs = pl.ds(r0, rows)
            wh = _window(pad_h, r0, rows)
            wd = _window(pad_d, r0, rows)
            dcc_c = wd[HALO:HALO + rows]
            dhc = jnp.zeros((rows, HEAD), F32)
            for k in range(kw):
                dhc = dhc + w_ref[k:k + 1, :] * _advanced(wd, kw - 1 - k, rows)
                acc[k] += _fold8(dcc_c * _delayed(wh, kw - 1 - k, rows))
            sg = _sigmoid(g_ref[rs, :])
            da_ref[rs, :] = (dhc * sg).astype(BF16)
            dg_ref[rs, :] = (dhc * a_ref[rs, :] * sg * (1.0 - sg)).astype(BF16)
            return carry

        lax.fori_loop(0, nchunk, step, 0)
        for k in range(kw):
            dw_ref[k:k + 1, :] = jnp.sum(acc[k], axis=0, keepdims=True)

    col = _col_spec(s, 0)
    return pl.pallas_call(
        body, name=name, grid=(nb,),
        in_specs=[_col_spec(s, 20), _col_spec(s, 24), col, pl.BlockSpec((kw, HEAD), lambda j: (0, j))],
        out_specs=[col, col, pl.BlockSpec((kw, HEAD), lambda j: (0, j))],
        out_shape=[jax.ShapeDtypeStruct((s, GROUP), BF16)] * 2 + [jax.ShapeDtypeStruct((kw, GROUP), F32)],
        scratch_shapes=[pltpu.VMEM((s + 2 * HALO, HEAD), F32), pltpu.VMEM((s + 2 * HALO, HEAD), F32),
                        pltpu.VMEM((kw, 8, HEAD), F32)],
        compiler_params=_cp(("parallel",)),
    )(z, z, dcc, w)


def _pool_window_sum(win, level, rows, shift):
    n = win.shape[0]

    def moved(v, k):
        return pltpu.roll(v, k if shift is _delayed else n - k, axis=0)

    s2 = win + moved(win, 1)
    s4 = s2 + moved(s2, 2)
    s8 = s4 + moved(s4, 4)
    s16 = s8 + moved(s8, 8)
    sel = jnp.where(level == 0, s2, jnp.where(level == 1, s4, jnp.where(level == 2, s8, s16)))
    return sel[HALO:HALO + rows]


def _pool_count(level, r0, rows):
    t = r0 + lax.broadcasted_iota(jnp.int32, (rows, 1), 0)
    width = jnp.left_shift(jnp.int32(2), level)
    return jnp.minimum(t + 1, width).astype(F32)


def _pool_fwd(name, z, pool_w, scale):
    s = z.shape[0]
    rows, nchunk = _chunks(s)

    def body(z_ref, w_ref, sc_ref, o_ref, pad):
        level = pl.program_id(0)
        _zero_halo(pad, s)

        def fill(ci, carry):
            r0 = pl.multiple_of(ci * rows, rows)
            pad[pl.ds(pl.multiple_of(HALO + r0, 8), rows), :] = z_ref[pl.ds(r0, rows), :]
            return carry

        lax.fori_loop(0, nchunk, fill, 0)
        wb = w_ref[...].astype(BF16)

        def step(ci, carry):
            r0 = pl.multiple_of(ci * rows, rows)
            win = _window(pad, r0, rows)
            pm = _pool_window_sum(win, level, rows, _delayed) / _pool_count(level, r0, rows) - win[HALO:HALO + rows]
            r = jnp.dot(pm.astype(BF16), wb, preferred_element_type=F32)
            o_ref[pl.ds(r0, rows), :] = (r * sc_ref[...]).astype(BF16)
            return carry

        lax.fori_loop(0, nchunk, step, 0)

    return pl.pallas_call(
        body, name=name, grid=(POOL_LEVELS,),
        in_specs=[_col_spec(s, 28), pl.BlockSpec((None, HEAD, HEAD), lambda j: (j, 0, 0)),
                  pl.BlockSpec((1, HEAD), lambda j: (0, j))],
        out_specs=_col_spec(s, 0),
        out_shape=jax.ShapeDtypeStruct((s, GROUP), BF16),
        scratch_shapes=[pltpu.VMEM((s + 2 * HALO, HEAD), F32)],
        compiler_params=_cp(("parallel",)),
    )(z, pool_w, scale)


def _pool_bwd(name, z, d_o, pool_w, scale):
    s = z.shape[0]
    rows, nchunk = _chunks(s)

    def body(z_ref, do_ref, w_ref, sc_ref, dz_ref, dw_ref, dsc_ref, pad, pad_q, dw_acc, dsc_acc):
        level = pl.program_id(0)
        _zero_halo(pad, s)
        _zero_halo(pad_q, s)
        dw_acc[...] = jnp.zeros_like(dw_acc)
        dsc_acc[...] = jnp.zeros_like(dsc_acc)

        def fill(ci, carry):
            r0 = pl.multiple_of(ci * rows, rows)
            pad[pl.ds(pl.multiple_of(HALO + r0, 8), rows), :] = z_ref[pl.ds(r0, rows), :]
            return carry

        lax.fori_loop(0, nchunk, fill, 0)
        wb = w_ref[...].astype(BF16)

        def first(ci, carry):
            r0 = pl.multiple_of(ci * rows, rows)
            win = _window(pad, r0, rows)
            cnt = _pool_count(level, r0, rows)
            pm = (_pool_window_sum(win, level, rows, _delayed) / cnt - win[HALO:HALO + rows]).astype(BF16)
            r = jnp.dot(pm, wb, preferred_element_type=F32)
            d_od = do_ref[pl.ds(r0, rows), :].astype(F32)
            dsc_acc[...] += _fold8(d_od * r)
            dr = (d_od * sc_ref[...]).astype(BF16)
            dw_acc[...] += lax.dot_general(pm, dr, TN, preferred_element_type=F32)
            dpm = lax.dot_general(dr, wb, NT, preferred_element_type=F32)
            pad_q[pl.ds(pl.multiple_of(HALO + r0, 8), rows), :] = dpm / cnt
            return carry

        lax.fori_loop(0, nchunk, first, 0)

        def second(ci, carry):
            r0 = pl.multiple_of(ci * rows, rows)
            wq = _window(pad_q, r0, rows)
            dpm = wq[HALO:HALO + rows] * _pool_count(level, r0, rows)
            dz_ref[pl.ds(r0, rows), :] = (_pool_window_sum(wq, level, rows, _advanced) - dpm).astype(BF16)
            return carry

        lax.fori_loop(0, nchunk, second, 0)
        dw_ref[...] = dw_acc[...]
        dsc_ref[...] = jnp.sum(dsc_acc[...], axis=0, keepdims=True)

    col = _col_spec(s, 0)
    mat = pl.BlockSpec((None, HEAD, HEAD), lambda j: (j, 0, 0))
    vec = pl.BlockSpec((1, HEAD), lambda j: (0, j))
    return pl.pallas_call(
        body, name=name, grid=(POOL_LEVELS,),
        in_specs=[_col_spec(s, 28), col, mat, vec], out_specs=[col, mat, vec],
        out_shape=[jax.ShapeDtypeStruct((s, GROUP), BF16), jax.ShapeDtypeStruct((POOL_LEVELS, HEAD, HEAD), F32),
                   jax.ShapeDtypeStruct((1, GROUP), F32)],
        scratch_shapes=[pltpu.VMEM((s + 2 * HALO, HEAD), F32), pltpu.VMEM((s + 2 * HALO, HEAD), F32),
                        pltpu.VMEM((HEAD, HEAD), F32), pltpu.VMEM((8, HEAD), F32)],
        compiler_params=_cp(("parallel",)),
    )(z, d_o, pool_w, scale)


def _ep_residual(accs, ins, outs):
    outs[0][...] = ins[2][...] + accs[0]


def _mm_residual(name, x, wg, h, tm, tn):
    m, d = h.shape
    tm = min(tm, m)

    def ep(accs, ins, outs, cols):
        outs[0][:, cols] = ins[5][:, cols] + accs[0]

    tile = pl.BlockSpec((tm, tn), lambda i, j: (i, j))
    return _mm_nn_wide(name, x, wg, "row", tm, tn, epilogue=ep, extra=[h], extra_specs=[tile], out_specs=[tile],
                       out_shape=[jax.ShapeDtypeStruct((m, d), F32)])[0]


def _swiglu_fwd(name, y, wg_gate, wg_up, tm):
    m, kdim = y.shape
    _, a, b = wg_gate.shape
    tm = min(tm, m)

    def ep(accs, ins, outs, cols):
        gt, up = accs
        outs[0][:, cols] = (gt * _sigmoid(gt) * up).astype(BF16)
        outs[1][:, cols] = gt.astype(BF16)
        outs[2][:, cols] = up.astype(BF16)

    gws, gspecs, gdots = _wide_nn_weights(wg_gate, "col", b, 1)
    uws, uspecs, udots = _wide_nn_weights(wg_up, "col", b, 1 + len(gws))
    out = pl.BlockSpec((tm, b), lambda i, j: (i, j))
    return _matmul_wide(name, (m // tm, 4), [y] + gws + uws,
                        [pl.BlockSpec((tm, kdim), lambda i, j: (i, 0))] + gspecs + uspecs, [gdots, udots], b, ep,
                        [out] * 3, [jax.ShapeDtypeStruct((m, 4 * b), BF16)] * 3)


def _swiglu_bwd(name, dh, wg_down, gate, up, tm):
    m, n = dh.shape
    _, a, b = wg_down.shape
    tm = min(tm, m)

    def ep(accs, ins, outs, cols):
        d_act = accs[0]
        gt = ins[3][:, cols].astype(F32)
        upv = ins[4][:, cols].astype(F32)
        sg = _sigmoid(gt)
        outs[0][:, cols] = (d_act * upv * sg * (1.0 + gt * (1.0 - sg))).astype(BF16)
        outs[1][:, cols] = (d_act * gt * sg).astype(BF16)
        outs[2][:, cols] = (gt * sg * upv).astype(BF16)

    ws, wspecs, dots = _wide_nt_weights(wg_down, "row", a, 1)
    tile = pl.BlockSpec((tm, a), lambda i, j: (i, j))
    return _matmul_wide(name, (m // tm, 4), [dh] + ws + [gate, up],
                        [pl.BlockSpec((tm, n), lambda i, j: (i, 0))] + wspecs + [tile, tile], [dots], a, ep,
                        [tile] * 3, [jax.ShapeDtypeStruct((m, 4 * a), BF16)] * 3)


def _ffn_dy(name, d_gate, d_up, wg_gate, wg_up, tiles, after=()):
    m, n = d_gate.shape
    _, a, b = wg_gate.shape
    kdim = 2 * a
    tm, to, tc = tiles
    tm = min(tm, m)
    grid = (m // tm, kdim // to, n // tc)
    lhs = pl.BlockSpec((tm, tc), lambda i, j, k: (i, k))
    wspec = _w_spec_nt(wg_gate, "col", to, tc)
    return _matmul(name, grid, [d_gate, d_up, wg_gate, wg_up], [lhs, lhs, wspec, wspec],
                   [(0, 2, 0, NT), (1, 3, 0, NT)], 1, (tm, to), _ep_store(BF16),
                   [pl.BlockSpec((tm, to), lambda i, j, k: (i, j))], [jax.ShapeDtypeStruct((m, kdim), BF16)],
                   after=after)[0]


def _ple_fwd(name, y, wg, h, pp, tm, tn):
    m, d = h.shape
    tm = min(tm, m)

    def ep(accs, ins, outs, cols):
        pg = accs[0]
        outs[0][:, cols] = ins[5][:, cols] + _sigmoid(pg) * ins[6][:, cols].astype(F32)
        outs[1][:, cols] = pg.astype(BF16)

    tile = pl.BlockSpec((tm, tn), lambda i, j: (i, j))
    return _mm_nn_wide(name, y, wg, "row", tm, tn, epilogue=ep, extra=[h, pp], extra_specs=[tile, tile],
                       out_specs=[tile, tile],
                       out_shape=[jax.ShapeDtypeStruct((m, d), F32), jax.ShapeDtypeStruct((m, d), BF16)])


def _ple_bwd(name, dh, pg, pp, after=()):
    s, d = dh.shape
    tr = _row_tile(s, 512)

    def body(dh_ref, pg_ref, pp_ref, *rest):
        dpp_ref, dpg_ref = rest[-2:]
        dhv = dh_ref[...]
        sg = _sigmoid(pg_ref[...].astype(F32))
        dpp_ref[...] = (dhv * sg).astype(BF16)
        dpg_ref[...] = (dhv * pp_ref[...].astype(F32) * sg * (1.0 - sg)).astype(BF16)

    row = pl.BlockSpec((tr, d), lambda i: (i, 0))
    return pl.pallas_call(
        body, name=name, grid=(s // tr,), in_specs=[row] * 3 + [ANY_SPEC] * len(after), out_specs=[row] * 2,
        out_shape=[jax.ShapeDtypeStruct((s, d), BF16)] * 2, compiler_params=_cp(("parallel",)),
    )(dh, pg, pp, *after)


BIG = ["w_in", "w_out", "w_gate", "w_up", "w_down", "w_ple_gate", "w_ple_proj"]
KIND = {"w_in": "col", "w_out": "row", "w_gate": "col", "w_up": "col", "w_down": "row", "w_ple_gate": "row",
        "w_ple_proj": "col"}
GATHER_GROUPS = (("w_in", "w_out"), ("w_gate", "w_up"), ("w_down", "w_ple_gate", "w_ple_proj"))
RS_GROUPS = (("w_ple_gate", "w_ple_proj", "w_down", "w_gate", "w_up"), ("w_out", "w_in"))
SMALL = ["norm_mix_g", "sgu_ln_g", "sgu_ln_b", "sgu_w", "sgu_b", "sc_conv_w", "cf_conv_w", "cf_conv_b", "cf_ln_g",
         "cf_ln_b", "pool_w", "pool_scale", "norm_ffn_g", "norm_ple_g", "final_norm_g"]
CHIP_SPLIT = ["sc_conv_w", "cf_conv_w"]
WEIGHTS = ['norm_mix_g', 'w_in', 'sgu_ln_g', 'sgu_ln_b', 'sgu_w', 'sgu_b', 'sc_conv_w', 'cf_conv_w', 'cf_conv_b',
           'cf_ln_g', 'cf_ln_b', 'pool_w', 'pool_scale', 'w_out', 'norm_ffn_g', 'w_gate', 'w_up', 'w_down',
           'norm_ple_g', 'w_ple_gate', 'w_ple_proj', 'final_norm_g']


def _tile(n, want):
    if n <= want:
        return n
    t = (want // 128) * 128
    while n % t:
        t -= 128
    return t


def _pack_rows(vecs):
    flat = jnp.concatenate([v.reshape(-1) for v in vecs])
    n = flat.shape[0]
    quantum = PACK_ROWS * 128
    padded = ((n + quantum - 1) // quantum) * quantum
    return jnp.pad(flat, (0, padded - n)).reshape(padded // 128, 128), n


def _unpack(flat, shapes):
    out, off = [], 0
    for shp in shapes:
        size = math.prod(shp)
        out.append(flat[off:off + size].reshape(shp))
        off += size
    return out


def kernel(x, p, norm_mix_g, w_in, sgu_ln_g, sgu_ln_b, sgu_w, sgu_b, sc_conv_w, cf_conv_w, cf_conv_b, cf_ln_g, cf_ln_b, pool_w, pool_scale, w_out, norm_ffn_g, w_gate, w_up, w_down, norm_ple_g, w_ple_gate, w_ple_proj, final_norm_g, loss_target, m_norm_mix_g, m_w_in, m_sgu_ln_g, m_sgu_ln_b, m_sgu_w, m_sgu_b, m_sc_conv_w, m_cf_conv_w, m_cf_conv_b, m_cf_ln_g, m_cf_ln_b, m_pool_w, m_pool_scale, m_w_out, m_norm_ffn_g, m_w_gate, m_w_up, m_w_down, m_norm_ple_g, m_w_ple_gate, m_w_ple_proj, m_final_norm_g, v_norm_mix_g, v_w_in, v_sgu_ln_g, v_sgu_ln_b, v_sgu_w, v_sgu_b, v_sc_conv_w, v_cf_conv_w, v_cf_conv_b, v_cf_ln_g, v_cf_ln_b, v_pool_w, v_pool_scale, v_w_out, v_norm_ffn_g, v_w_gate, v_w_up, v_w_down, v_norm_ple_g, v_w_ple_gate, v_w_ple_proj, v_final_norm_g):
    args = dict(locals())
    w = {n: args[n] for n in WEIGHTS}
    mom = {n: args["m_" + n] for n in WEIGHTS}
    var = {n: args["v_" + n] for n in WEIGHTS}
    depth = w_in.shape[0]
    s, d = x.shape[1], x.shape[2]
    f_dim = 4 * w_gate.shape[2]
    xi, yi, ci = lax.axis_index("x"), lax.axis_index("y"), lax.axis_index("c")
    c_idx = ci.astype(jnp.int32).reshape(1)

    chip_idx = (2 * xi + yi).astype(jnp.int32).reshape(1)

    def start_gathers(l, dep):
        pending = []
        for gi, names in enumerate(GATHER_GROUPS):
            lands = [_cast_into_landing(f"cast_{n}", w[n], l, 0 if KIND[n] == "col" else 1, chip_idx) for n in names]
            pending.append(_gather_start(f"gather_start_{l}_{gi}", lands, dep))
            dep = pending[-1][3]
        return pending, dep

    def finish_gather(l, gi, pending, after):
        send_sems, recv_sems, lands, _ = pending
        lands = _gather_wait(f"gather_wait_{l}_{gi}", send_sems, recv_sems, lands, after)
        return dict(zip(GATHER_GROUPS[gi], _gather_share(f"gather_share_{l}_{gi}", lands)))

    conv_pack = jnp.concatenate([sc_conv_w, cf_conv_w], axis=1)
    taps = conv_pack.shape[1]
    rows_pad = ((depth * taps + 7) // 8) * 8
    conv_rows = jnp.pad(conv_pack.reshape(depth * taps, HEAD), ((0, rows_pad - depth * taps), (0, 0)))
    conv_all = _allgather8("gather_conv_weights", [conv_rows])[0]
    conv_full = conv_all[0::2, :depth * taps].reshape(4, depth, taps, HEAD)
    conv_full = jnp.transpose(conv_full, (1, 2, 0, 3)).reshape(depth, taps, GROUP)
    sc_w_full, cf_w_full = conv_full[:, :3], conv_full[:, 3:]

    pending, token = start_gathers(0, conv_all)
    h = x[0]
    saved = []
    gathered = []
    for l in range(depth):
        wg = finish_gather(l, 0, pending[0], token if l == 0 else h)
        gathered.append(wg)
        started = ()
        this_layer = pending
        if l + 1 < depth:
            pending, token = start_gathers(l + 1, wg["w_in"])
            started = (token,)
        sv = {"h0": h}
        y1 = _rms_fwd("rms_mix", h, norm_mix_g[l:l + 1], after=started)
        z = _mm_nn_wide("mm_in", y1, wg["w_in"], "col", 1024, 1024)[0]
        lg, lb = sgu_ln_g[l][:, None, :], sgu_ln_b[l][:, None, :]
        sb = sgu_b[l][:, :, None]
        oa = _sgu_fwd("sgu_fwd", z, lg, lb, sgu_w[l], sb)
        ob = _shortconv_fwd("shortconv_fwd", z, sc_w_full[l])
        cc = _conformer_conv_fwd("conformer_conv_fwd", z, cf_w_full[l], cf_conv_b[l:l + 1])
        oc = _conformer_ln_fwd("conformer_ln_fwd", cc, cf_ln_g[l:l + 1], cf_ln_b[l:l + 1])
        od = _pool_fwd("pool_fwd", z, pool_w[l], pool_scale[l:l + 1])
        o = jnp.concatenate([oa, ob, oc, od], axis=1)
        tile = pl.BlockSpec((min(1024, s), 1024), lambda i, j, k: (i, j))
        h1 = _mm_residual("mm_out", o, wg["w_out"], h, 1024, 1024)
        wg.update(finish_gather(l, 1, this_layer[1], h1))
        y2 = _rms_fwd("rms_ffn", h1, norm_ffn_g[l:l + 1])
        fs = f_dim // 4
        act, gt, up = _swiglu_fwd("mm_swiglu", y2, wg["w_gate"], wg["w_up"], 512)
        wg.update(finish_gather(l, 2, this_layer[2], act))
        h2 = _mm_nn("mm_down", act, wg["w_down"], "row", (1024, 1024, fs), epilogue=_ep_residual, extra=[h1],
                    extra_specs=[tile], out_specs=[tile], out_shape=[jax.ShapeDtypeStruct((s, d), F32)])[0]
        y3 = _rms_fwd("rms_ple", h2, norm_ple_g[l:l + 1])
        pb = p[l, 0].astype(BF16)
        ptile = pl.BlockSpec((min(1024, s), 512), lambda i, j: (i, j))
        pp = _mm_nn_wide("mm_ple_proj", pb, wg["w_ple_proj"], "col", 1024, 512, epilogue=_wide_store(BF16),
                         out_specs=[ptile], out_shape=[jax.ShapeDtypeStruct((s, d), BF16)])[0]
        h3, pg = _ple_fwd("mm_ple_gate", y3, wg["w_ple_gate"], h2, pp, 1024, 1024)
        sv.update(y1=y1, z=z, cc=cc, o=o, h1=h1, y2=y2, gt=gt, up=up, h2=h2, y3=y3, pb=pb, pp=pp, pg=pg)
        saved.append(sv)
        h = h3

    loss_part, dh, dhb, d_final_g = _loss_head("loss_head", h, final_norm_g[None, :], loss_target[0])

    small_grads = [None] * depth
    where = jnp.stack([2 * xi + yi, 2 * (1 - xi) + yi, 2 * xi + (1 - yi), 2 * (1 - xi) + (1 - yi), ci]).astype(jnp.int32)
    grad_bufs = {n: lax.empty((depth, 2, w[n].shape[1] // 2, w[n].shape[2]), F32) for n in BIG}
    exchanges = [None] * len(RS_GROUPS)

    def halves(g):
        return g.reshape(4, 2, g.shape[1] // 2, g.shape[2])

    def start_exchange(layer, gi, sibling, after):
        send_sems, recv_sems, gs, lands, _ = sibling
        gs, lands = _sibling_wait(f"rs_sibling_wait_{layer}_{gi}", send_sems, recv_sems, gs, lands, after)
        chip_sums = _add_halves("rs_add", gs, lands, c_idx)
        return _chip_exchange_start(f"rs_chips_start_{layer}_{gi}", chip_sums)

    def finish_exchange(layer, gi, after):
        send_sems, recv_sems, parts, lands, _ = exchanges[gi]
        parts, lands = _chip_exchange_wait(f"rs_chips_wait_{layer}_{gi}", send_sems, recv_sems, parts, lands, after)
        names = RS_GROUPS[gi]
        sums = _sum_parts("rs_sum", parts, lands, [grad_bufs[n] for n in names], layer, where)
        grad_bufs.update(zip(names, sums))

    for l in reversed(range(depth)):
        wg = gathered[l]
        sv = saved[l]
        fs = f_dim // 4
        started = () if exchanges[1] is None else (exchanges[1][4],)
        d_pp, d_pg = _ple_bwd("ple_bwd", dh, sv["pg"], sv["pp"], after=started)
        g_ple_proj = _mm_tn("dw_ple_proj", sv["pb"], d_pp, "col", (w_ple_proj.shape[1], 512, 1024))
        g_ple_gate = _mm_tn("dw_ple_gate", sv["y3"], d_pg, "row", (512, 1024, 1024))
        dy3 = _mm_nt_wide("dx_ple_gate", d_pg, wg["w_ple_gate"], "row", 1024, 512)
        dh, dhb, dg_ple = _rms_bwd("rms_ple_bwd", sv["h2"], norm_ple_g[l:l + 1], dy3, dh)

        d_gt, d_up, act = _swiglu_bwd("dx_down_swiglu", dhb, wg["w_down"], sv["gt"], sv["up"], 512)
        g_down = _mm_tn("dw_down", act, dhb, "row", (fs, 1024, 1024))
        g_gate = _mm_tn("dw_gate", sv["y2"], d_gt, "col", (1024, fs, 1024))
        g_up = _mm_tn("dw_up", sv["y2"], d_up, "col", (1024, fs, 1024))
        big = dict(w_gate=g_gate, w_up=g_up, w_down=g_down, w_ple_gate=g_ple_gate, w_ple_proj=g_ple_proj)
        sibling = _sibling_start(f"rs_sibling_start_{l}_0", [halves(big[n]) for n in RS_GROUPS[0]])
        dy2 = _ffn_dy("dx_gate_up", d_gt, d_up, wg["w_gate"], wg["w_up"], (1024, 1024, fs), after=(sibling[4],))
        dh, dhb, dg_ffn = _rms_bwd("rms_ffn_bwd", sv["h1"], norm_ffn_g[l:l + 1], dy2, dh)

        g_out = _mm_tn("dw_out", sv["o"], dhb, "row", (512, 1024, 1024))
        if exchanges[0] is not None:
            finish_exchange(l + 1, 0, g_out)
        exchanges[0] = start_exchange(l, 0, sibling, g_out)
        d_o = _mm_nt_wide("dx_out", dhb, wg["w_out"], "row", 1024, 512, after=(exchanges[0][4],))
        z = sv["z"]
        lg, lb = sgu_ln_g[l][:, None, :], sgu_ln_b[l][:, None, :]
        sb = sgu_b[l][:, :, None]
        dzu, dzv, d_lg, d_lb, d_sw, d_sb = _sgu_bwd("sgu_bwd", z, d_o[:, 0:GROUP], lg, lb, sgu_w[l], sb)
        dzh, dzbg, dzcg, d_scw = _shortconv_bwd("shortconv_bwd", z, d_o[:, GROUP:2 * GROUP], sc_w_full[l])
        dcc, d_cflg, d_cflb, d_cfb = _conformer_ln_bwd("conformer_ln_bwd", sv["cc"], d_o[:, 2 * GROUP:3 * GROUP],
                                                       cf_ln_g[l:l + 1], cf_ln_b[l:l + 1])
        dza, dzg, d_cfw = _conformer_conv_bwd("conformer_conv_bwd", z, dcc, cf_w_full[l])
        dzd, d_pw, d_psc = _pool_bwd("pool_bwd", z, d_o[:, 3 * GROUP:], pool_w[l], pool_scale[l:l + 1])
        dz = jnp.concatenate([dzu, dzv, dzh, dzbg, dzcg, dza, dzg, dzd], axis=1)
        g_in = _mm_tn("dw_in", sv["y1"], dz, "col", (1024, 1024, 1024))
        big.update(w_out=g_out, w_in=g_in)
        sibling = _sibling_start(f"rs_sibling_start_{l}_1", [halves(big[n]) for n in RS_GROUPS[1]])
        dy1 = _mm_nt_wide("dx_in", dz, wg["w_in"], "col", 1024, 512, after=(sibling[4],))
        dh, dhb, dg_mix = _rms_bwd("rms_mix_bwd", sv["h0"], norm_mix_g[l:l + 1], dy1, dh)
        if exchanges[1] is not None:
            finish_exchange(l + 1, 1, dh)
        exchanges[1] = start_exchange(l, 1, sibling, dh)

        small_grads[l] = dict(norm_mix_g=dg_mix, sgu_ln_g=d_lg, sgu_ln_b=d_lb, sgu_w=d_sw, sgu_b=d_sb,
                              sc_conv_w=d_scw, cf_conv_w=d_cfw, cf_conv_b=d_cfb, cf_ln_g=d_cflg, cf_ln_b=d_cflb,
                              pool_w=d_pw, pool_scale=d_psc, norm_ffn_g=dg_ffn, norm_ple_g=dg_ple)
    grad_x = dh[None]

    grads, delta, new_m, new_v = {}, {}, {}, {}
    per_layer = [n for n in SMALL if n != "final_norm_g"]
    vecs = [small_grads[l][n] for n in per_layer for l in range(depth)] + [d_final_g, loss_part]
    packed, _ = _pack_rows(vecs)
    total = _sum8("sum_small", _allgather8("gather_small", [packed])[0]).reshape(-1)
    stacked_shapes = [(depth,) + (w[n].shape[1:] if n not in CHIP_SPLIT else (w[n].shape[1], GROUP)) for n in per_layer]
    pieces = _unpack(total, stacked_shapes + [(d,), ()])
    loss = pieces[-1]
    grads["final_norm_g"] = pieces[-2]
    chip_off = (2 * xi + yi) * HEAD
    for n, g in zip(per_layer, pieces):
        grads[n] = lax.dynamic_slice_in_dim(g, chip_off, HEAD, axis=2) if n in CHIP_SPLIT else g

    finish_exchange(0, 0, dh)
    behind = (exchanges[1][4],)
    for gi, names in enumerate(RS_GROUPS):
        if gi == 1:
            finish_exchange(0, 1, behind[0])
        joined = _sibling_join_halves(f"rs_join_{gi}", [grad_bufs[n] for n in names], after=behind)
        for n, g in zip(names, joined):
            shp = w[n].shape
            two_d = (shp[0] * shp[1], shp[2])
            grads[n] = g.reshape(shp)
            dl, mn, vn = _adamw(f"adamw_{n}", w[n].reshape(two_d), g.reshape(two_d), mom[n].reshape(two_d),
                                var[n].reshape(two_d))
            delta[n], new_m[n], new_v[n] = dl.reshape(shp), mn.reshape(shp), vn.reshape(shp)
            behind = (dl,)
    small_shapes = [w[n].shape for n in SMALL]
    pw, _ = _pack_rows([w[n] for n in SMALL])
    pg_, _ = _pack_rows([grads[n] for n in SMALL])
    pm, _ = _pack_rows([mom[n] for n in SMALL])
    pv, _ = _pack_rows([var[n] for n in SMALL])
    dl, mn, vn = _adamw("adamw_small", pw, pg_, pm, pv)
    for n, a, b, cc_ in zip(SMALL, _unpack(dl.reshape(-1), small_shapes), _unpack(mn.reshape(-1), small_shapes),
                            _unpack(vn.reshape(-1), small_shapes)):
        delta[n], new_m[n], new_v[n] = a, b, cc_

    return (loss, grad_x, *[grads[n] for n in WEIGHTS], *[delta[n] for n in WEIGHTS],
            *[new_m[n] for n in WEIGHTS], *[new_v[n] for n in WEIGHTS])
```

```python
import functools
import math

import jax
import jax.numpy as jnp
from jax import lax
from jax.experimental import pallas as pl
from jax.experimental.pallas import tpu as pltpu

F32 = jnp.float32
BF16 = jnp.bfloat16
MESH = pl.DeviceIdType.MESH

HEAD = 128
GROUP = 512
EPS = 1e-6
HALO = 32
CHUNK_ROWS = 128
POOL_LEVELS = 4
PACK_ROWS = 512

ADAM_LR = 0.001
ADAM_B1 = 0.9
ADAM_B2 = 0.999
ADAM_EPS = 1e-08
ADAM_WD = 0.01
ADAM_STEP = 10

VMEM_LIMIT = 56 * 1024 * 1024


def _cp(sem=None, vmem=VMEM_LIMIT):
    return pltpu.CompilerParams(dimension_semantics=sem, vmem_limit_bytes=vmem)


def _sigmoid(x):
    return 0.5 * jnp.tanh(0.5 * x) + 0.5


_GELU_K = math.sqrt(2.0 / math.pi)
_GELU_C = 0.044715


def _gelu(x):
    t = jnp.tanh(_GELU_K * (x + _GELU_C * x * x * x))
    return 0.5 * x * (1.0 + t)


def _gelu_grad(x):
    t = jnp.tanh(_GELU_K * (x + _GELU_C * x * x * x))
    return 0.5 * (1.0 + t) + 0.5 * x * (1.0 - t * t) * _GELU_K * (1.0 + 3.0 * _GELU_C * x * x)


def _mesh_pos():
    return lax.axis_index("x"), lax.axis_index("y"), lax.axis_index("c")


def _any_specs(n):
    return [pl.BlockSpec(memory_space=pl.ANY)] * n


def _allgather8(name, blocks):
    n = len(blocks)

    def body(*refs):
        ins, outs = refs[:n], refs[n:2 * n]
        send_sems, recv_sems, local_sems = refs[2 * n:]
        x, y, c = _mesh_pos()
        me, sibling = (x, y, c), (x, y, 1 - c)
        chips = [(1 - x, y), (x, 1 - y), (1 - x, 1 - y)]

        def slot(a, dev):
            return outs[a].at[4 * dev[0] + 2 * dev[1] + dev[2]]

        def copy(a, k, block, to, src=None):
            dst = slot(a, block)
            return pltpu.make_async_remote_copy(
                src_ref=dst if src is None else src, dst_ref=dst,
                send_sem=send_sems.at[7 * a + k], recv_sem=recv_sems.at[7 * a + k],
                device_id=to, device_id_type=MESH)

        mine, first, passed = [], [], []
        for a in range(n):
            cp = pltpu.make_async_copy(ins[a], slot(a, me), local_sems.at[a])
            cp.start()
            mine.append(cp)
            cps = [copy(a, 0, me, sibling, src=ins[a])]
            cps += [copy(a, 1 + j, me, (*chip, c), src=ins[a]) for j, chip in enumerate(chips)]
            for cp in cps:
                cp.start()
            first += cps
        for j, chip in enumerate(chips):
            for a in range(n):
                copy(a, 1 + j, (*chip, c), me).wait_recv()
                cp = copy(a, 4 + j, (*chip, c), sibling)
                cp.start()
                passed.append(cp)
        for a in range(n):
            copy(a, 0, sibling, me).wait_recv()
            for j, chip in enumerate(chips):
                copy(a, 4 + j, (*chip, 1 - c), me).wait_recv()
        for cp in first + passed:
            cp.wait_send()
        for cp in mine:
            cp.wait()

    return pl.pallas_call(
        body, name=name,
        out_shape=[jax.ShapeDtypeStruct((8,) + b.shape, b.dtype) for b in blocks],
        in_specs=_any_specs(n), out_specs=_any_specs(n),
        scratch_shapes=[pltpu.SemaphoreType.DMA((7 * n,)), pltpu.SemaphoreType.DMA((7 * n,)),
                        pltpu.SemaphoreType.DMA((n,))],
    )(*blocks)


HBM_SPEC = pl.BlockSpec(memory_space=pltpu.HBM)
SEM_SPEC = pl.BlockSpec(memory_space=pltpu.SEMAPHORE)
ANY_SPEC = pl.BlockSpec(memory_space=pl.ANY)
SPLIT_COPY = pltpu.CompilerParams(has_side_effects=pltpu.SideEffectType.DATAFLOW_SIDE_EFFECTING)


def _hbm(x):
    return pltpu.with_memory_space_constraint(x, pltpu.HBM)


def _other_chips(x, y):
    return [(1 - x, y), (x, 1 - y), (1 - x, 1 - y)]


def _dev_slot(ref, dev):
    return ref.at[4 * dev[0] + 2 * dev[1] + dev[2]]


def _cast_into_landing(name, w, layer, ax, chip_idx):
    _, r, cc = w.shape
    r2, c2 = (r // 2, cc) if ax == 0 else (r, cc // 2)
    tr = _row_tile(r2, 512 if c2 <= 1536 else 256)
    nt = r2 // tr

    def body(chip_ref, w_ref, o_ref):
        o_ref[...] = w_ref[...].astype(BF16)

    if ax == 0:
        in_spec = pl.BlockSpec((None, tr, c2), lambda hf, i, chip_ref: (layer, hf * nt + i, 0))
    else:
        in_spec = pl.BlockSpec((None, tr, c2), lambda hf, i, chip_ref: (layer, i, hf))
    return pl.pallas_call(
        body, name=name,
        grid_spec=pltpu.PrefetchScalarGridSpec(
            num_scalar_prefetch=1, grid=(2, nt), in_specs=[in_spec],
            out_specs=pl.BlockSpec((None, tr, c2), lambda hf, i, chip_ref: (2 * chip_ref[0] + hf, i, 0))),
        out_shape=jax.ShapeDtypeStruct((8, r2, c2), BF16), compiler_params=_cp(("parallel", "parallel")),
    )(chip_idx, w)


def _gather_ici_copies(lands, send_sems, recv_sems):
    x, y, c = _mesh_pos()
    pairs = []
    for a in range(len(lands)):
        for j, chip in enumerate(_other_chips(x, y)):
            def copy(dev):
                return pltpu.make_async_remote_copy(
                    src_ref=_dev_slot(lands[a], dev), dst_ref=_dev_slot(lands[a], dev),
                    send_sem=send_sems.at[3 * a + j], recv_sem=recv_sems.at[3 * a + j],
                    device_id=(*chip, c), device_id_type=MESH)
            pairs.append((copy((x, y, c)), copy((*chip, c))))
    return pairs


def _exchange_ici_copies(ins, lands, send_sems, recv_sems):
    x, y, c = _mesh_pos()
    pairs = []
    for a in range(len(ins)):
        for k, chip in enumerate(_other_chips(x, y)):
            there = 2 * chip[0] + chip[1]
            def copy(dst_entry):
                return pltpu.make_async_remote_copy(
                    src_ref=ins[a].at[there], dst_ref=lands[a].at[dst_entry],
                    send_sem=send_sems.at[3 * a + k], recv_sem=recv_sems.at[3 * a + k],
                    device_id=(*chip, c), device_id_type=MESH)
            pairs.append((copy(2 * x + y), copy(there)))
    return pairs


def _gather_start(name, lands, dep):
    n = len(lands)

    def body(*refs):
        token = refs[-1]
        for send, _ in _gather_ici_copies(refs[:n], refs[n + 1], refs[n + 2]):
            send.start()
        token[...] = jnp.zeros_like(token)

    out = pl.pallas_call(
        body, name=name,
        out_shape=(pltpu.SemaphoreType.DMA((3 * n,)), pltpu.SemaphoreType.DMA((3 * n,)),
                   *[pltpu.HBM(l.shape, l.dtype) for l in lands], jax.ShapeDtypeStruct((8, 128), F32)),
        in_specs=[HBM_SPEC] * n + [ANY_SPEC],
        out_specs=(SEM_SPEC, SEM_SPEC, *[HBM_SPEC] * n, pl.BlockSpec(memory_space=pltpu.VMEM)),
        input_output_aliases={a: 2 + a for a in range(n)},
        compiler_params=SPLIT_COPY,
    )(*[_hbm(l) for l in lands], dep)
    return out[0], out[1], list(out[2:2 + n]), out[-1]


def _gather_wait(name, send_sems, recv_sems, lands, after):
    n = len(lands)

    def body(*refs):
        for send, recv in _gather_ici_copies(refs[:n], refs[n], refs[n + 1]):
            send.wait_send()
            recv.wait_recv()

    out = pl.pallas_call(
        body, name=name,
        out_shape=[pltpu.HBM(l.shape, l.dtype) for l in lands],
        in_specs=[HBM_SPEC] * n + [SEM_SPEC, SEM_SPEC, ANY_SPEC],
        out_specs=[HBM_SPEC] * n,
        input_output_aliases={a: a for a in range(n)},
        compiler_params=SPLIT_COPY,
    )(*lands, send_sems, recv_sems, after)
    return list(out)


def _share_copies(lands, send_sems, recv_sems):
    x, y, c = _mesh_pos()
    pairs = []
    for a in range(len(lands)):
        for j, chip in enumerate(_other_chips(x, y)):
            def copy(dev):
                slot = _dev_slot(lands[a], dev)
                return pltpu.make_async_remote_copy(
                    src_ref=slot, dst_ref=slot, send_sem=send_sems.at[3 * a + j], recv_sem=recv_sems.at[3 * a + j],
                    device_id=(x, y, 1 - c), device_id_type=MESH)
            pairs.append((copy((*chip, c)), copy((*chip, 1 - c))))
    return pairs


def _share_start(name, lands):
    n = len(lands)

    def body(*refs):
        token = refs[-1]
        for send, _ in _share_copies(refs[:n], refs[n], refs[n + 1]):
            send.start()
        token[...] = jnp.zeros_like(token)

    out = pl.pallas_call(
        body, name=name,
        out_shape=(pltpu.SemaphoreType.DMA((3 * n,)), pltpu.SemaphoreType.DMA((3 * n,)),
                   *[pltpu.HBM(l.shape, l.dtype) for l in lands], jax.ShapeDtypeStruct((8, 128), F32)),
        in_specs=[HBM_SPEC] * n,
        out_specs=(SEM_SPEC, SEM_SPEC, *[HBM_SPEC] * n, pl.BlockSpec(memory_space=pltpu.VMEM)),
        input_output_aliases={a: 2 + a for a in range(n)},
        compiler_params=SPLIT_COPY,
    )(*lands)
    return out[0], out[1], list(out[2:2 + n]), out[-1]


def _share_wait(name, send_sems, recv_sems, lands, after):
    n = len(lands)

    def body(*refs):
        for send, recv in _share_copies(refs[:n], refs[n], refs[n + 1]):
            send.wait_send()
            recv.wait_recv()

    out = pl.pallas_call(
        body, name=name,
        out_shape=[pltpu.HBM(l.shape, l.dtype) for l in lands],
        in_specs=[HBM_SPEC] * n + [SEM_SPEC, SEM_SPEC, ANY_SPEC],
        out_specs=[HBM_SPEC] * n,
        input_output_aliases={a: a for a in range(n)},
        compiler_params=SPLIT_COPY,
    )(*lands, send_sems, recv_sems, after)
    return list(out)


def _sibling_copies(ins, lands, send_sems, recv_sems):
    x, y, c = _mesh_pos()
    return [pltpu.make_async_remote_copy(
        src_ref=ins[a].at[j, 1 - c], dst_ref=lands[a].at[j],
        send_sem=send_sems.at[4 * a + j], recv_sem=recv_sems.at[4 * a + j],
        device_id=(x, y, 1 - c), device_id_type=MESH) for a in range(len(ins)) for j in range(4)]


def _sibling_start(name, grads):
    n = len(grads)

    def body(*refs):
        token = refs[-1]
        for cp in _sibling_copies(refs[:n], refs[n:2 * n], refs[2 * n], refs[2 * n + 1]):
            cp.start()
        token[...] = jnp.zeros_like(token)

    lands = [_hbm(lax.empty((4,) + g.shape[2:], g.dtype)) for g in grads]
    out = pl.pallas_call(
        body, name=name,
        out_shape=(pltpu.SemaphoreType.DMA((4 * n,)), pltpu.SemaphoreType.DMA((4 * n,)),
                   *[pltpu.HBM(g.shape, g.dtype) for g in grads], *[pltpu.HBM(l.shape, l.dtype) for l in lands],
                   jax.ShapeDtypeStruct((8, 128), F32)),
        in_specs=[HBM_SPEC] * (2 * n),
        out_specs=(SEM_SPEC, SEM_SPEC, *[HBM_SPEC] * (2 * n), pl.BlockSpec(memory_space=pltpu.VMEM)),
        input_output_aliases={i: 2 + i for i in range(2 * n)},
        compiler_params=SPLIT_COPY,
    )(*[_hbm(g) for g in grads], *lands)
    return out[0], out[1], list(out[2:2 + n]), list(out[2 + n:2 + 2 * n]), out[-1]


def _sibling_wait(name, send_sems, recv_sems, grads, lands, after):
    n = len(grads)

    def body(*refs):
        for cp in _sibling_copies(refs[:n], refs[n:2 * n], refs[2 * n], refs[2 * n + 1]):
            cp.wait_send()
            cp.wait_recv()

    out = pl.pallas_call(
        body, name=name,
        out_shape=[pltpu.HBM(g.shape, g.dtype) for g in grads] + [pltpu.HBM(l.shape, l.dtype) for l in lands],
        in_specs=[HBM_SPEC] * (2 * n) + [SEM_SPEC, SEM_SPEC, ANY_SPEC],
        out_specs=[HBM_SPEC] * (2 * n),
        input_output_aliases={i: i for i in range(2 * n)},
        compiler_params=SPLIT_COPY,
    )(*grads, *lands, send_sems, recv_sems, after)
    return list(out[:n]), list(out[n:])


def _chip_exchange_start(name, parts):
    n = len(parts)

    def body(*refs):
        ins, lands = refs[:n], refs[n:2 * n]
        token = refs[-1]
        for send, _ in _exchange_ici_copies(ins, lands, refs[2 * n], refs[2 * n + 1]):
            send.start()
        token[...] = jnp.zeros_like(token)

    lands = [_hbm(lax.empty(p.shape, p.dtype)) for p in parts]
    out = pl.pallas_call(
        body, name=name,
        out_shape=(pltpu.SemaphoreType.DMA((3 * n,)), pltpu.SemaphoreType.DMA((3 * n,)),
                   *[pltpu.HBM(p.shape, p.dtype) for p in parts], *[pltpu.HBM(p.shape, p.dtype) for p in parts],
                   jax.ShapeDtypeStruct((8, 128), F32)),
        in_specs=[HBM_SPEC] * (2 * n),
        out_specs=(SEM_SPEC, SEM_SPEC, *[HBM_SPEC] * (2 * n), pl.BlockSpec(memory_space=pltpu.VMEM)),
        input_output_aliases={i: 2 + i for i in range(2 * n)},
        compiler_params=SPLIT_COPY,
    )(*[_hbm(p) for p in parts], *lands)
    return out[0], out[1], list(out[2:2 + n]), list(out[2 + n:2 + 2 * n]), out[-1]


def _chip_exchange_wait(name, send_sems, recv_sems, parts, lands, after):
    n = len(parts)

    def body(*refs):
        ins, lands_in = refs[:n], refs[n:2 * n]
        for send, recv in _exchange_ici_copies(ins, lands_in, refs[2 * n], refs[2 * n + 1]):
            send.wait_send()
            recv.wait_recv()

    out = pl.pallas_call(
        body, name=name,
        out_shape=[pltpu.HBM(p.shape, p.dtype) for p in parts] * 2,
        in_specs=[HBM_SPEC] * (2 * n) + [SEM_SPEC, SEM_SPEC, ANY_SPEC],
        out_specs=[HBM_SPEC] * (2 * n),
        input_output_aliases={i: i for i in range(2 * n)},
        compiler_params=SPLIT_COPY,
    )(*parts, *lands, send_sems, recv_sems, after)
    return list(out[:n]), list(out[n:])


def _sibling_join_halves(name, bufs, after=()):
    n = len(bufs)
    depth = bufs[0].shape[0]

    def body(*refs):
        outs = refs[n + len(after):2 * n + len(after)]
        send_sems, recv_sems = refs[2 * n + len(after):]
        x, y, c = _mesh_pos()

        def copy(a, l, half):
            return pltpu.make_async_remote_copy(
                src_ref=outs[a].at[l, half], dst_ref=outs[a].at[l, half],
                send_sem=send_sems.at[depth * a + l], recv_sem=recv_sems.at[depth * a + l],
                device_id=(x, y, 1 - c), device_id_type=MESH)

        for a in range(n):
            for l in range(depth):
                copy(a, l, c).start()
        for a in range(n):
            for l in range(depth):
                copy(a, l, c).wait_send()
                copy(a, l, 1 - c).wait_recv()

    return pl.pallas_call(
        body, name=name, out_shape=[jax.ShapeDtypeStruct(b.shape, b.dtype) for b in bufs],
        in_specs=_any_specs(n + len(after)), out_specs=_any_specs(n),
        input_output_aliases={a: a for a in range(n)},
        scratch_shapes=[pltpu.SemaphoreType.DMA((depth * n,)), pltpu.SemaphoreType.DMA((depth * n,))],
    )(*bufs, *after)


def _row_tile(rows, want):
    t = 1
    while t * 2 <= min(rows, want):
        t *= 2
    while rows % t:
        t //= 2
    return t


def _add_halves(name, grads, recvd, c_idx):
    outs = []
    for a, (g, r) in enumerate(zip(grads, recvd)):
        _, _, r2, cc = g.shape
        tr = _row_tile(r2, 512)

        def body(c_ref, g_ref, r_ref, o_ref):
            o_ref[...] = (g_ref[...].astype(F32) + r_ref[...].astype(F32)).astype(BF16)

        outs.append(pl.pallas_call(
            body, name=f"{name}_{a}",
            grid_spec=pltpu.PrefetchScalarGridSpec(
                num_scalar_prefetch=1, grid=(4, r2 // tr),
                in_specs=[pl.BlockSpec((None, None, tr, cc), lambda j, i, c_ref: (j, c_ref[0], i, 0)),
                          pl.BlockSpec((None, tr, cc), lambda j, i, c_ref: (j, i, 0))],
                out_specs=pl.BlockSpec((None, tr, cc), lambda j, i, c_ref: (j, i, 0))),
            out_shape=jax.ShapeDtypeStruct(r.shape, BF16), compiler_params=_cp(("parallel", "parallel")),
        )(c_idx, g, r))
    return outs


def _sum_parts(name, parts, recvd, bufs, layer, where):
    outs = []
    for a, (p, r, buf) in enumerate(zip(parts, recvd, bufs)):
        _, r2, cc = p.shape
        tr = _row_tile(r2, 256)

        def body(where_ref, own_ref, r1_ref, r2_ref, r3_ref, buf_ref, o_ref):
            acc = own_ref[...].astype(F32)
            for ref in (r1_ref, r2_ref, r3_ref):
                acc = acc + ref[...].astype(F32)
            o_ref[...] = acc

        def entry(k):
            return pl.BlockSpec((None, tr, cc), lambda i, where_ref: (where_ref[k], i, 0))

        outs.append(pl.pallas_call(
            body, name=f"{name}_{a}",
            grid_spec=pltpu.PrefetchScalarGridSpec(
                num_scalar_prefetch=1, grid=(r2 // tr,),
                in_specs=[entry(0), entry(1), entry(2), entry(3), ANY_SPEC],
                out_specs=pl.BlockSpec((None, None, tr, cc), lambda i, where_ref: (layer, where_ref[4], i, 0))),
            out_shape=jax.ShapeDtypeStruct(buf.shape, F32), input_output_aliases={5: 0},
            compiler_params=_cp(("parallel",)),
        )(where, p, r, r, r, buf))
    return outs


def _sum8(name, gathered):
    _, r, cc = gathered.shape
    tr = _row_tile(r, 512)

    def body(g_ref, o_ref):
        acc = g_ref[0]
        for i in range(1, 8):
            acc = acc + g_ref[i]
        o_ref[...] = acc

    return pl.pallas_call(
        body, name=name, grid=(r // tr,),
        in_specs=[pl.BlockSpec((8, tr, cc), lambda i: (0, i, 0))],
        out_specs=pl.BlockSpec((tr, cc), lambda i: (i, 0)),
        out_shape=jax.ShapeDtypeStruct((r, cc), F32), compiler_params=_cp(("parallel",)),
    )(gathered)


def _adamw(name, w, g, m, v):
    r, cc = w.shape
    tr = _row_tile(r, max(8, (256 * 1024) // cc))

    def body(w_ref, g_ref, m_ref, v_ref, d_ref, mo_ref, vo_ref):
        gg = g_ref[...]
        mn = ADAM_B1 * m_ref[...] + (1.0 - ADAM_B1) * gg
        vn = ADAM_B2 * v_ref[...] + (1.0 - ADAM_B2) * (gg * gg)
        m_hat = mn / (1.0 - ADAM_B1 ** ADAM_STEP)
        v_hat = vn / (1.0 - ADAM_B2 ** ADAM_STEP)
        d_ref[...] = -ADAM_LR * (m_hat / (jnp.sqrt(v_hat) + ADAM_EPS) + ADAM_WD * w_ref[...])
        mo_ref[...] = mn
        vo_ref[...] = vn

    spec = pl.BlockSpec((tr, cc), lambda i: (i, 0))
    return pl.pallas_call(
        body, name=name, grid=(r // tr,), in_specs=[spec] * 4, out_specs=[spec] * 3,
        out_shape=[jax.ShapeDtypeStruct(w.shape, F32)] * 3, compiler_params=_cp(("parallel",)),
    )(w, g, m, v)


def _rms_fwd(name, h, g, after=()):
    s, d = h.shape
    tr = _row_tile(s, 512)

    def body(h_ref, g_ref, *rest):
        y_ref = rest[-1]
        xv = h_ref[...]
        r = lax.rsqrt(jnp.mean(xv * xv, axis=-1, keepdims=True) + EPS)
        y_ref[...] = (xv * r * g_ref[...]).astype(BF16)

    return pl.pallas_call(
        body, name=name, grid=(s // tr,),
        in_specs=[pl.BlockSpec((tr, d), lambda i: (i, 0)), pl.BlockSpec((1, d), lambda i: (0, 0))]
        + [ANY_SPEC] * len(after),
        out_specs=pl.BlockSpec((tr, d), lambda i: (i, 0)),
        out_shape=jax.ShapeDtypeStruct((s, d), BF16), compiler_params=_cp(("parallel",)),
    )(h, g, *after)


def _rms_bwd_rows(xv, gv, dy):
    d = xv.shape[-1]
    r = lax.rsqrt(jnp.mean(xv * xv, axis=-1, keepdims=True) + EPS)
    dxn = dy * gv
    proj = jnp.sum(dxn * xv, axis=-1, keepdims=True) * (1.0 / d)
    dx = r * dxn - xv * (r * r * r) * proj
    return dx, dy * xv * r


def _rms_bwd(name, h, g, dy, dres):
    s, d = h.shape
    tr = _row_tile(s, 256)

    def body(h_ref, g_ref, dy_ref, dres_ref, dh_ref, dhb_ref, dg_ref):
        dx, dgp = _rms_bwd_rows(h_ref[...], g_ref[...], dy_ref[...].astype(F32))
        dh = dres_ref[...] + dx
        dh_ref[...] = dh
        dhb_ref[...] = dh.astype(BF16)

        @pl.when(pl.program_id(0) == 0)
        def _():
            dg_ref[...] = jnp.zeros_like(dg_ref)

        dg_ref[...] += jnp.sum(dgp, axis=0, keepdims=True)

    row = pl.BlockSpec((tr, d), lambda i: (i, 0))
    vec = pl.BlockSpec((1, d), lambda i: (0, 0))
    return pl.pallas_call(
        body, name=name, grid=(s // tr,), in_specs=[row, vec, row, row], out_specs=[row, row, vec],
        out_shape=[jax.ShapeDtypeStruct((s, d), F32), jax.ShapeDtypeStruct((s, d), BF16),
                   jax.ShapeDtypeStruct((1, d), F32)],
        compiler_params=_cp(("arbitrary",)),
    )(h, g, dy, dres)


def _loss_head(name, h, g, target):
    s, d = h.shape
    tr = _row_tile(s, 256)

    def body(h_ref, g_ref, t_ref, loss_ref, dh_ref, dhb_ref, dg_ref):
        xv = h_ref[...]
        gv = g_ref[...]
        r = lax.rsqrt(jnp.mean(xv * xv, axis=-1, keepdims=True) + EPS)
        err = xv * r * gv - t_ref[...]
        part = 0.5 * jnp.sum(jnp.sum(err * err, axis=-1, keepdims=True) * (1.0 / d), axis=0, keepdims=True)
        dx, dgp = _rms_bwd_rows(xv, gv, err * (1.0 / d))
        dh_ref[...] = dx
        dhb_ref[...] = dx.astype(BF16)

        @pl.when(pl.program_id(0) == 0)
        def _():
            dg_ref[...] = jnp.zeros_like(dg_ref)
            loss_ref[...] = jnp.zeros_like(loss_ref)

        dg_ref[...] += jnp.sum(dgp, axis=0, keepdims=True)
        loss_ref[...] += part

    row = pl.BlockSpec((tr, d), lambda i: (i, 0))
    vec = pl.BlockSpec((1, d), lambda i: (0, 0))
    one = pl.BlockSpec((1, 1), lambda i: (0, 0))
    return pl.pallas_call(
        body, name=name, grid=(s // tr,), in_specs=[row, vec, row], out_specs=[one, row, row, vec],
        out_shape=[jax.ShapeDtypeStruct((1, 1), F32), jax.ShapeDtypeStruct((s, d), F32),
                   jax.ShapeDtypeStruct((s, d), BF16), jax.ShapeDtypeStruct((1, d), F32)],
        compiler_params=_cp(("arbitrary",)),
    )(h, g, target)


NN = (((1,), (0,)), ((), ()))
NT = (((1,), (1,)), ((), ()))
TN = (((0,), (0,)), ((), ()))


def _matmul(name, grid, operands, in_specs, pairs, n_acc, acc_shape, epilogue, out_specs, out_shape, after=()):
    operands = list(operands) + list(after)
    in_specs = list(in_specs) + [ANY_SPEC] * len(after)
    n_in = len(operands)
    n_out = len(out_shape)
    nk = grid[2]

    def body(*refs):
        ins, outs, accs = refs[:n_in], refs[n_in:n_in + n_out], refs[n_in + n_out:]
        k = pl.program_id(2)

        @pl.when(k == 0)
        def _():
            for acc in accs:
                acc[...] = jnp.zeros_like(acc)

        for li, ri, ai, dims in pairs:
            accs[ai][...] += lax.dot_general(ins[li][...], ins[ri][...], dims, preferred_element_type=F32)

        @pl.when(k == nk - 1)
        def _():
            epilogue([acc[...] for acc in accs], ins, outs)

    return pl.pallas_call(
        body, name=name, grid=grid, in_specs=in_specs, out_specs=out_specs, out_shape=out_shape,
        scratch_shapes=[pltpu.VMEM(acc_shape, F32)] * n_acc,
        compiler_params=_cp(("parallel", "parallel", "arbitrary")),
    )(*operands)


def _ep_store(dtype):
    def ep(accs, ins, outs):
        outs[0][...] = accs[0].astype(dtype)
    return ep


def _w_spec_nt(wg, kind, to, tc):
    _, a, b = wg.shape
    if kind == "col":
        oph, cps = a // to, b // tc
        return pl.BlockSpec((None, to, tc), lambda i, jo, kc: (2 * (kc // cps) + jo // oph, jo % oph, kc % cps))
    ops, cph = a // to, b // tc
    return pl.BlockSpec((None, to, tc), lambda i, jo, kc: (2 * (jo // ops) + kc // cph, jo % ops, kc % cph))


SUB_COLS = 256


def _matmul_wide(name, grid, operands, in_specs, groups, tn, epilogue, out_specs, out_shape, after=()):
    operands = list(operands) + list(after)
    in_specs = list(in_specs) + [ANY_SPEC] * len(after)
    n_in = len(operands)

    def body(*refs):
        ins, outs = refs[:n_in], refs[n_in:]
        for s0 in range(0, tn, SUB_COLS):
            cols = slice(s0, min(s0 + SUB_COLS, tn))
            accs = []
            for group in groups:
                acc = None
                for li, (c0, cw), ri, dims in group:
                    rhs = ins[ri][:, cols] if dims == NN else ins[ri][cols, :]
                    part = lax.dot_general(ins[li][:, c0:c0 + cw], rhs, dims, preferred_element_type=F32)
                    acc = part if acc is None else acc + part
                accs.append(acc)
            epilogue(accs, ins, outs, cols)

    return pl.pallas_call(
        body, name=name, grid=grid, in_specs=in_specs, out_specs=out_specs, out_shape=out_shape,
        compiler_params=_cp(("parallel", "parallel")),
    )(*operands)


def _wide_store(dtype):
    def ep(accs, ins, outs, cols):
        outs[0][:, cols] = accs[0].astype(dtype)
    return ep


def _wide_nn_weights(wg, kind, tn, first):
    _, a, b = wg.shape
    per = b // tn
    if kind == "col":
        specs = [pl.BlockSpec((None, a, tn), lambda i, j, hf=hf: (2 * (j // per) + hf, 0, j % per)) for hf in range(2)]
    else:
        specs = [pl.BlockSpec((None, a, tn), lambda i, j, ch=ch: (2 * ch + j // per, 0, j % per)) for ch in range(4)]
    dots = [(0, (q * a, a), first + q, NN) for q in range(len(specs))]
    return [wg] * len(specs), specs, dots


def _wide_nt_weights(wg, kind, to, first):
    _, a, b = wg.shape
    per = a // to
    if kind == "col":
        specs = [pl.BlockSpec((None, to, b), lambda i, j, ch=ch: (2 * ch + j // per, j % per, 0)) for ch in range(4)]
    else:
        specs = [pl.BlockSpec((None, to, b), lambda i, j, hf=hf: (2 * (j // per) + hf, j % per, 0)) for hf in range(2)]
    dots = [(0, (q * b, b), first + q, NT) for q in range(len(specs))]
    return [wg] * len(specs), specs, dots


def _mm_nn_wide(name, x, wg, kind, tm, tn, epilogue=None, extra=(), extra_specs=(), out_specs=None, out_shape=None,
                after=()):
    m, kdim = x.shape
    _, a, b = wg.shape
    n = 4 * b if kind == "col" else 2 * b
    tm = min(tm, m)
    ws, wspecs, dots = _wide_nn_weights(wg, kind, tn, 1)
    if out_shape is None:
        out_shape = [jax.ShapeDtypeStruct((m, n), F32)]
        out_specs = [pl.BlockSpec((tm, tn), lambda i, j: (i, j))]
        epilogue = _wide_store(F32)
    return _matmul_wide(name, (m // tm, n // tn), [x] + ws + list(extra),
                        [pl.BlockSpec((tm, kdim), lambda i, j: (i, 0))] + wspecs + list(extra_specs),
                        [dots], tn, epilogue, out_specs, out_shape, after=after)


def _mm_nt_wide(name, dy, wg, kind, tm, to, out_dtype=BF16, after=()):
    m, n = dy.shape
    _, a, b = wg.shape
    kdim = 2 * a if kind == "col" else 4 * a
    tm = min(tm, m)
    ws, wspecs, dots = _wide_nt_weights(wg, kind, to, 1)
    return _matmul_wide(name, (m // tm, kdim // to), [dy] + ws,
                        [pl.BlockSpec((tm, n), lambda i, j: (i, 0))] + wspecs, [dots], to, _wide_store(out_dtype),
                        [pl.BlockSpec((tm, to), lambda i, j: (i, j))],
                        [jax.ShapeDtypeStruct((m, kdim), out_dtype)], after=after)[0]


def _mm_tn_wide(name, a, dy, kind, tr, tn):
    m, kdim = a.shape
    _, n = dy.shape

    def body(a_ref, dy_ref, o_ref):
        for s0 in range(0, tn, SUB_COLS):
            cols = slice(s0, min(s0 + SUB_COLS, tn))
            o_ref[:, cols] = lax.dot_general(a_ref[...], dy_ref[:, cols], TN, preferred_element_type=F32).astype(BF16)

    if kind == "col":
        ns = n // 4
        per = ns // tn
        out_shape = jax.ShapeDtypeStruct((4, kdim, ns), BF16)
        out_spec = pl.BlockSpec((None, tr, tn), lambda r, j: (j // per, r, j % per))
    else:
        rs = kdim // 4
        per = rs // tr
        out_shape = jax.ShapeDtypeStruct((4, rs, n), BF16)
        out_spec = pl.BlockSpec((None, tr, tn), lambda r, j: (r // per, r % per, j))
    return pl.pallas_call(
        body, name=name, grid=(kdim // tr, n // tn),
        in_specs=[pl.BlockSpec((m, tr), lambda r, j: (0, r)), pl.BlockSpec((m, tn), lambda r, j: (0, j))],
        out_specs=out_spec, out_shape=out_shape, compiler_params=_cp(("parallel", "parallel")),
    )(a, dy)


def _mm_tn(name, a, dy, kind, tiles):
    m, kdim = a.shape
    _, n = dy.shape
    tr, tn, tmm = tiles
    tmm = min(tmm, m)
    grid = (kdim // tr, n // tn, m // tmm)
    in_specs = [pl.BlockSpec((tmm, tr), lambda r, j, k: (k, r)), pl.BlockSpec((tmm, tn), lambda r, j, k: (k, j))]
    if kind == "col":
        ns = n // 4
        bps = ns // tn
        out_shape = [jax.ShapeDtypeStruct((4, kdim, ns), BF16)]
        out_specs = [pl.BlockSpec((None, tr, tn), lambda r, j, k: (j // bps, r, j % bps))]
    else:
        rs = kdim // 4
        rps = rs // tr
        out_shape = [jax.ShapeDtypeStruct((4, rs, n), BF16)]
        out_specs = [pl.BlockSpec((None, tr, tn), lambda r, j, k: (r // rps, r % rps, j))]
    return _matmul(name, grid, [a, dy], in_specs, [(0, 1, 0, TN)], 1, (tr, tn), _ep_store(BF16),
                   out_specs, out_shape)[0]


def _zero_halo(pad_ref, s):
    z = jnp.zeros((HALO, pad_ref.shape[1]), F32)
    pad_ref[pl.ds(0, HALO), :] = z
    pad_ref[pl.ds(HALO + s, HALO), :] = z


def _window(pad_ref, r0, rows):
    return pad_ref[pl.ds(r0, rows + 2 * HALO), :]


def _delayed(win, k, rows):
    if k == 0:
        return win[HALO:HALO + rows]
    return pltpu.roll(win, k, axis=0)[HALO:HALO + rows]


def _advanced(win, k, rows):
    if k == 0:
        return win[HALO:HALO + rows]
    return pltpu.roll(win, win.shape[0] - k, axis=0)[HALO:HALO + rows]


def _fold8(x):
    return jnp.sum(x.reshape(x.shape[0] // 8, 8, x.shape[1]), axis=0)


def _chunks(s):
    rows = min(CHUNK_ROWS, s)
    return rows, s // rows


def _col_spec(s, first_block):
    return pl.BlockSpec((s, HEAD), lambda j: (0, first_block + j))


def _sgu_fwd(name, z, ln_g, ln_b, w, b):
    s = z.shape[0]
    tr = _row_tile(s, 1024)
    nh = GROUP // HEAD

    def body(u_ref, v_ref, lg_ref, lb_ref, w_ref, b_ref, o_ref):
        row = lax.broadcasted_iota(jnp.int32, (HEAD, HEAD), 0)
        col = lax.broadcasted_iota(jnp.int32, (HEAD, HEAD), 1)
        wm = jnp.where(row >= col, w_ref[...], 0.0).astype(BF16)
        for ck in range(tr // HEAD):
            rs = pl.ds(ck * HEAD, HEAD)
            u = _gelu(u_ref[rs, :])
            v = _gelu(v_ref[rs, :])
            mu = jnp.mean(v, axis=-1, keepdims=True)
            vc = v - mu
            var = jnp.mean(vc * vc, axis=-1, keepdims=True)
            vln = vc * lax.rsqrt(var + EPS) * lg_ref[...] + lb_ref[...]
            sp = jnp.dot(wm, vln.astype(BF16), preferred_element_type=F32) + b_ref[...]
            o_ref[rs, :] = (u * sp).astype(BF16)

    head_vec = pl.BlockSpec((None, 1, HEAD), lambda h, i: (h, 0, 0))
    return pl.pallas_call(
        body, name=name, grid=(nh, s // tr),
        in_specs=[pl.BlockSpec((tr, HEAD), lambda h, i: (i, h)), pl.BlockSpec((tr, HEAD), lambda h, i: (i, nh + h)),
                  head_vec, head_vec, pl.BlockSpec((None, HEAD, HEAD), lambda h, i: (h, 0, 0)),
                  pl.BlockSpec((None, HEAD, 1), lambda h, i: (h, 0, 0))],
        out_specs=pl.BlockSpec((tr, HEAD), lambda h, i: (i, h)),
        out_shape=jax.ShapeDtypeStruct((s, GROUP), BF16), compiler_params=_cp(("parallel", "parallel")),
    )(z, z, ln_g, ln_b, w, b)


def _sgu_bwd(name, z, d_o, ln_g, ln_b, w, b):
    s = z.shape[0]
    tr = _row_tile(s, 1024)
    nh = GROUP // HEAD

    def body(u_ref, v_ref, do_ref, lg_ref, lb_ref, w_ref, b_ref, du_ref, dv_ref, dlg_ref, dlb_ref, dw_ref, db_ref,
             dsp_acc):
        row = lax.broadcasted_iota(jnp.int32, (HEAD, HEAD), 0)
        col = lax.broadcasted_iota(jnp.int32, (HEAD, HEAD), 1)
        tril = row >= col
        wm = jnp.where(tril, w_ref[...], 0.0).astype(BF16)
        i = pl.program_id(1)

        @pl.when(i == 0)
        def _():
            dlg_ref[...] = jnp.zeros_like(dlg_ref)
            dlb_ref[...] = jnp.zeros_like(dlb_ref)
            dw_ref[...] = jnp.zeros_like(dw_ref)
            dsp_acc[...] = jnp.zeros_like(dsp_acc)

        dlg = jnp.zeros((1, HEAD), F32)
        dlb = jnp.zeros((1, HEAD), F32)
        dw = jnp.zeros((HEAD, HEAD), F32)
        dsp_sum = jnp.zeros((HEAD, HEAD), F32)
        for ck in range(tr // HEAD):
            rs = pl.ds(ck * HEAD, HEAD)
            zu = u_ref[rs, :]
            zv = v_ref[rs, :]
            u = _gelu(zu)
            v = _gelu(zv)
            mu = jnp.mean(v, axis=-1, keepdims=True)
            vc = v - mu
            var = jnp.mean(vc * vc, axis=-1, keepdims=True)
            rstd = lax.rsqrt(var + EPS)
            xh = vc * rstd
            vln = (xh * lg_ref[...] + lb_ref[...]).astype(BF16)
            sp = jnp.dot(wm, vln, preferred_element_type=F32) + b_ref[...]
            d_oa = do_ref[rs, :].astype(F32)
            du = d_oa * sp
            dsp = d_oa * u
            dsp_b = dsp.astype(BF16)
            dvln = lax.dot_general(wm, dsp_b, TN, preferred_element_type=F32)
            dw = dw + lax.dot_general(dsp_b, vln, NT, preferred_element_type=F32)
            dsp_sum = dsp_sum + dsp
            dlg = dlg + jnp.sum(dvln * xh, axis=0, keepdims=True)
            dlb = dlb + jnp.sum(dvln, axis=0, keepdims=True)
            dxh = dvln * lg_ref[...]
            dv = rstd * (dxh - jnp.mean(dxh, axis=-1, keepdims=True)
                         - xh * jnp.mean(dxh * xh, axis=-1, keepdims=True))
            du_ref[rs, :] = (du * _gelu_grad(zu)).astype(BF16)
            dv_ref[rs, :] = (dv * _gelu_grad(zv)).astype(BF16)
        dlg_ref[...] += dlg
        dlb_ref[...] += dlb
        dw_ref[...] += jnp.where(tril, dw, 0.0)
        dsp_acc[...] += dsp_sum

        @pl.when(i == pl.num_programs(1) - 1)
        def _():
            db_ref[...] = jnp.sum(dsp_acc[...], axis=1, keepdims=True)

    head_vec = pl.BlockSpec((None, 1, HEAD), lambda h, i: (h, 0, 0))
    head_mat = pl.BlockSpec((None, HEAD, HEAD), lambda h, i: (h, 0, 0))
    head_col = pl.BlockSpec((None, HEAD, 1), lambda h, i: (h, 0, 0))
    return pl.pallas_call(
        body, name=name, grid=(nh, s // tr),
        in_specs=[pl.BlockSpec((tr, HEAD), lambda h, i: (i, h)), pl.BlockSpec((tr, HEAD), lambda h, i: (i, nh + h)),
                  pl.BlockSpec((tr, HEAD), lambda h, i: (i, h)), head_vec, head_vec, head_mat, head_col],
        out_specs=[pl.BlockSpec((tr, HEAD), lambda h, i: (i, h)), pl.BlockSpec((tr, HEAD), lambda h, i: (i, h)),
                   head_vec, head_vec, head_mat, head_col],
        out_shape=[jax.ShapeDtypeStruct((s, GROUP), BF16), jax.ShapeDtypeStruct((s, GROUP), BF16),
                   jax.ShapeDtypeStruct((nh, 1, HEAD), F32), jax.ShapeDtypeStruct((nh, 1, HEAD), F32),
                   jax.ShapeDtypeStruct((nh, HEAD, HEAD), F32), jax.ShapeDtypeStruct((nh, HEAD, 1), F32)],
        scratch_shapes=[pltpu.VMEM((HEAD, HEAD), F32)],
        compiler_params=_cp(("parallel", "arbitrary")),
    )(z, z, d_o, ln_g, ln_b, w, b)


def _shortconv_fwd(name, z, w):
    s = z.shape[0]
    kw = w.shape[0]
    rows, nchunk = _chunks(s)
    nb = GROUP // HEAD

    def body(h_ref, bg_ref, cg_ref, w_ref, o_ref, pad):
        _zero_halo(pad, s)

        def fill(ci, carry):
            r0 = pl.multiple_of(ci * rows, rows)
            pad[pl.ds(pl.multiple_of(HALO + r0, 8), rows), :] = cg_ref[pl.ds(r0, rows), :] * h_ref[pl.ds(r0, rows), :]
            return carry

        lax.fori_loop(0, nchunk, fill, 0)

        def step(ci, carry):
            r0 = pl.multiple_of(ci * rows, rows)
            win = _window(pad, r0, rows)
            cv = jnp.zeros((rows, HEAD), F32)
            for k in range(kw):
                cv = cv + w_ref[k:k + 1, :] * _delayed(win, kw - 1 - k, rows)
            o_ref[pl.ds(r0, rows), :] = (bg_ref[pl.ds(r0, rows), :] * cv).astype(BF16)
            return carry

        lax.fori_loop(0, nchunk, step, 0)

    return pl.pallas_call(
        body, name=name, grid=(nb,),
        in_specs=[_col_spec(s, 8), _col_spec(s, 12), _col_spec(s, 16), pl.BlockSpec((kw, HEAD), lambda j: (0, j))],
        out_specs=_col_spec(s, 0),
        out_shape=jax.ShapeDtypeStruct((s, GROUP), BF16),
        scratch_shapes=[pltpu.VMEM((s + 2 * HALO, HEAD), F32)],
        compiler_params=_cp(("parallel",)),
    )(z, z, z, w)


def _shortconv_bwd(name, z, d_o, w):
    s = z.shape[0]
    kw = w.shape[0]
    rows, nchunk = _chunks(s)
    nb = GROUP // HEAD

    def body(h_ref, bg_ref, cg_ref, do_ref, w_ref, dh_ref, dbg_ref, dcg_ref, dw_ref, pad_q, pad_d, acc):
        _zero_halo(pad_q, s)
        _zero_halo(pad_d, s)
        acc[...] = jnp.zeros_like(acc)

        def fill(ci, carry):
            r0 = pl.multiple_of(ci * rows, rows)
            rs = pl.ds(r0, rows)
            ps = pl.ds(pl.multiple_of(HALO + r0, 8), rows)
            pad_q[ps, :] = cg_ref[rs, :] * h_ref[rs, :]
            pad_d[ps, :] = do_ref[rs, :].astype(F32) * bg_ref[rs, :]
            return carry

        lax.fori_loop(0, nchunk, fill, 0)

        def step(ci, carry):
            r0 = pl.multiple_of(ci * rows, rows)
            rs = pl.ds(r0, rows)
            wq = _window(pad_q, r0, rows)
            wd = _window(pad_d, r0, rows)
            dcv = wd[HALO:HALO + rows]
            cv = jnp.zeros((rows, HEAD), F32)
            dq = jnp.zeros((rows, HEAD), F32)
            for k in range(kw):
                qk = _delayed(wq, kw - 1 - k, rows)
                cv = cv + w_ref[k:k + 1, :] * qk
                dq = dq + w_ref[k:k + 1, :] * _advanced(wd, kw - 1 - k, rows)
                acc[k] += _fold8(dcv * qk)
            dbg_ref[rs, :] = (do_ref[rs, :].astype(F32) * cv).astype(BF16)
            dcg_ref[rs, :] = (dq * h_ref[rs, :]).astype(BF16)
            dh_ref[rs, :] = (dq * cg_ref[rs, :]).astype(BF16)
            return carry

        lax.fori_loop(0, nchunk, step, 0)
        for k in range(kw):
            dw_ref[k:k + 1, :] = jnp.sum(acc[k], axis=0, keepdims=True)

    col = _col_spec(s, 0)
    return pl.pallas_call(
        body, name=name, grid=(nb,),
        in_specs=[_col_spec(s, 8), _col_spec(s, 12), _col_spec(s, 16), col, pl.BlockSpec((kw, HEAD), lambda j: (0, j))],
        out_specs=[col, col, col, pl.BlockSpec((kw, HEAD), lambda j: (0, j))],
        out_shape=[jax.ShapeDtypeStruct((s, GROUP), BF16)] * 3 + [jax.ShapeDtypeStruct((kw, GROUP), F32)],
        scratch_shapes=[pltpu.VMEM((s + 2 * HALO, HEAD), F32), pltpu.VMEM((s + 2 * HALO, HEAD), F32),
                        pltpu.VMEM((kw, 8, HEAD), F32)],
        compiler_params=_cp(("parallel",)),
    )(z, z, z, d_o, w)


def _conformer_conv_fwd(name, z, w, bias):
    s = z.shape[0]
    kw = w.shape[0]
    rows, nchunk = _chunks(s)
    nb = GROUP // HEAD

    def body(a_ref, g_ref, w_ref, b_ref, o_ref, pad):
        _zero_halo(pad, s)

        def fill(ci, carry):
            r0 = pl.multiple_of(ci * rows, rows)
            rs = pl.ds(r0, rows)
            pad[pl.ds(pl.multiple_of(HALO + r0, 8), rows), :] = a_ref[rs, :] * _sigmoid(g_ref[rs, :])
            return carry

        lax.fori_loop(0, nchunk, fill, 0)

        def step(ci, carry):
            r0 = pl.multiple_of(ci * rows, rows)
            win = _window(pad, r0, rows)
            cc = jnp.zeros((rows, HEAD), F32)
            for k in range(kw):
                cc = cc + w_ref[k:k + 1, :] * _delayed(win, kw - 1 - k, rows)
            o_ref[pl.ds(r0, rows), :] = cc + b_ref[...]
            return carry

        lax.fori_loop(0, nchunk, step, 0)

    return pl.pallas_call(
        body, name=name, grid=(nb,),
        in_specs=[_col_spec(s, 20), _col_spec(s, 24), pl.BlockSpec((kw, HEAD), lambda j: (0, j)),
                  pl.BlockSpec((1, HEAD), lambda j: (0, j))],
        out_specs=_col_spec(s, 0),
        out_shape=jax.ShapeDtypeStruct((s, GROUP), F32),
        scratch_shapes=[pltpu.VMEM((s + 2 * HALO, HEAD), F32)],
        compiler_params=_cp(("parallel",)),
    )(z, z, w, bias)


def _ln_rows(cc, g, b):
    mu = jnp.mean(cc, axis=-1, keepdims=True)
    xc = cc - mu
    var = jnp.mean(xc * xc, axis=-1, keepdims=True)
    rstd = lax.rsqrt(var + EPS)
    xh = xc * rstd
    return xh, rstd, xh * g + b


def _conformer_ln_fwd(name, cc, g, b):
    s, d = cc.shape
    tr = _row_tile(s, 512)

    def body(c_ref, g_ref, b_ref, o_ref):
        _, _, l = _ln_rows(c_ref[...], g_ref[...], b_ref[...])
        o_ref[...] = (l * _sigmoid(l)).astype(BF16)

    row = pl.BlockSpec((tr, d), lambda i: (i, 0))
    vec = pl.BlockSpec((1, d), lambda i: (0, 0))
    return pl.pallas_call(
        body, name=name, grid=(s // tr,), in_specs=[row, vec, vec], out_specs=row,
        out_shape=jax.ShapeDtypeStruct((s, d), BF16), compiler_params=_cp(("parallel",)),
    )(cc, g, b)


def _conformer_ln_bwd(name, cc, d_o, g, b):
    s, d = cc.shape
    tr = _row_tile(s, 512)

    def body(c_ref, do_ref, g_ref, b_ref, dcc_ref, dg_ref, db_ref, dcb_ref):
        xh, rstd, l = _ln_rows(c_ref[...], g_ref[...], b_ref[...])
        sg = _sigmoid(l)
        dl = do_ref[...].astype(F32) * sg * (1.0 + l * (1.0 - sg))
        dxh = dl * g_ref[...]
        dcc = rstd * (dxh - jnp.mean(dxh, axis=-1, keepdims=True) - xh * jnp.mean(dxh * xh, axis=-1, keepdims=True))
        dcc_ref[...] = dcc

        @pl.when(pl.program_id(0) == 0)
        def _():
            dg_ref[...] = jnp.zeros_like(dg_ref)
            db_ref[...] = jnp.zeros_like(db_ref)
            dcb_ref[...] = jnp.zeros_like(dcb_ref)

        dg_ref[...] += jnp.sum(dl * xh, axis=0, keepdims=True)
        db_ref[...] += jnp.sum(dl, axis=0, keepdims=True)
        dcb_ref[...] += jnp.sum(dcc, axis=0, keepdims=True)

    row = pl.BlockSpec((tr, d), lambda i: (i, 0))
    vec = pl.BlockSpec((1, d), lambda i: (0, 0))
    return pl.pallas_call(
        body, name=name, grid=(s // tr,), in_specs=[row, row, vec, vec], out_specs=[row, vec, vec, vec],
        out_shape=[jax.ShapeDtypeStruct((s, d), F32)] + [jax.ShapeDtypeStruct((1, d), F32)] * 3,
        compiler_params=_cp(("arbitrary",)),
    )(cc, d_o, g, b)


def _conformer_conv_bwd(name, z, dcc, w):
    s = z.shape[0]
    kw = w.shape[0]
    rows, nchunk = _chunks(s)
    nb = GROUP // HEAD

    def body(a_ref, g_ref, d_ref, w_ref, da_ref, dg_ref, dw_ref, pad_h, pad_d, acc):
        _zero_halo(pad_h, s)
        _zero_halo(pad_d, s)
        acc[...] = jnp.zeros_like(acc)

        def fill(ci, carry):
            r0 = pl.multiple_of(ci * rows, rows)
            rs = pl.ds(r0, rows)
            ps = pl.ds(pl.multiple_of(HALO + r0, 8), rows)
            pad_h[ps, :] = a_ref[rs, :] * _sigmoid(g_ref[rs, :])
            pad_d[ps, :] = d_ref[rs, :]
            return carry

        lax.fori_loop(0, nchunk, fill, 0)

        def step(ci, carry):
            r0 = pl.multiple_of(ci * rows, rows)
            rs = pl.ds(r0, rows)
            wh = _window(pad_h, r0, rows)
            wd = _window(pad_d, r0, rows)
            dcc_c = wd[HALO:HALO + rows]
            dhc = jnp.zeros((rows, HEAD), F32)
            for k in range(kw):
                dhc = dhc + w_ref[k:k + 1, :] * _advanced(wd, kw - 1 - k, rows)
                acc[k] += _fold8(dcc_c * _delayed(wh, kw - 1 - k, rows))
            sg = _sigmoid(g_ref[rs, :])
            da_ref[rs, :] = (dhc * sg).astype(BF16)
            dg_ref[rs, :] = (dhc * a_ref[rs, :] * sg * (1.0 - sg)).astype(BF16)
            return carry

        lax.fori_loop(0, nchunk, step, 0)
        for k in range(kw):
            dw_ref[k:k + 1, :] = jnp.sum(acc[k], axis=0, keepdims=True)

    col = _col_spec(s, 0)
    return pl.pallas_call(
        body, name=name, grid=(nb,),
        in_specs=[_col_spec(s, 20), _col_spec(s, 24), col, pl.BlockSpec((kw, HEAD), lambda j: (0, j))],
        out_specs=[col, col, pl.BlockSpec((kw, HEAD), lambda j: (0, j))],
        out_shape=[jax.ShapeDtypeStruct((s, GROUP), BF16)] * 2 + [jax.ShapeDtypeStruct((kw, GROUP), F32)],
        scratch_shapes=[pltpu.VMEM((s + 2 * HALO, HEAD), F32), pltpu.VMEM((s + 2 * HALO, HEAD), F32),
                        pltpu.VMEM((kw, 8, HEAD), F32)],
        compiler_params=_cp(("parallel",)),
    )(z, z, dcc, w)


def _pool_window_sum(win, level, rows, shift):
    n = win.shape[0]

    def moved(v, k):
        return pltpu.roll(v, k if shift is _delayed else n - k, axis=0)

    s2 = win + moved(win, 1)
    s4 = s2 + moved(s2, 2)
    s8 = s4 + moved(s4, 4)
    s16 = s8 + moved(s8, 8)
    sel = jnp.where(level == 0, s2, jnp.where(level == 1, s4, jnp.where(level == 2, s8, s16)))
    return sel[HALO:HALO + rows]


def _pool_count(level, r0, rows):
    t = r0 + lax.broadcasted_iota(jnp.int32, (rows, 1), 0)
    width = jnp.left_shift(jnp.int32(2), level)
    return jnp.minimum(t + 1, width).astype(F32)


def _pool_fwd(name, z, pool_w, scale):
    s = z.shape[0]
    rows, nchunk = _chunks(s)

    def body(z_ref, w_ref, sc_ref, o_ref, pad):
        level = pl.program_id(0)
        _zero_halo(pad, s)

        def fill(ci, carry):
            r0 = pl.multiple_of(ci * rows, rows)
            pad[pl.ds(pl.multiple_of(HALO + r0, 8), rows), :] = z_ref[pl.ds(r0, rows), :]
            return carry

        lax.fori_loop(0, nchunk, fill, 0)
        wb = w_ref[...].astype(BF16)

        def step(ci, carry):
            r0 = pl.multiple_of(ci * rows, rows)
            win = _window(pad, r0, rows)
            pm = _pool_window_sum(win, level, rows, _delayed) / _pool_count(level, r0, rows) - win[HALO:HALO + rows]
            r = jnp.dot(pm.astype(BF16), wb, preferred_element_type=F32)
            o_ref[pl.ds(r0, rows), :] = (r * sc_ref[...]).astype(BF16)
            return carry

        lax.fori_loop(0, nchunk, step, 0)

    return pl.pallas_call(
        body, name=name, grid=(POOL_LEVELS,),
        in_specs=[_col_spec(s, 28), pl.BlockSpec((None, HEAD, HEAD), lambda j: (j, 0, 0)),
                  pl.BlockSpec((1, HEAD), lambda j: (0, j))],
        out_specs=_col_spec(s, 0),
        out_shape=jax.ShapeDtypeStruct((s, GROUP), BF16),
        scratch_shapes=[pltpu.VMEM((s + 2 * HALO, HEAD), F32)],
        compiler_params=_cp(("parallel",)),
    )(z, pool_w, scale)


def _pool_bwd(name, z, d_o, pool_w, scale):
    s = z.shape[0]
    rows, nchunk = _chunks(s)

    def body(z_ref, do_ref, w_ref, sc_ref, dz_ref, dw_ref, dsc_ref, pad, pad_q, dw_acc, dsc_acc):
        level = pl.program_id(0)
        _zero_halo(pad, s)
        _zero_halo(pad_q, s)
        dw_acc[...] = jnp.zeros_like(dw_acc)
        dsc_acc[...] = jnp.zeros_like(dsc_acc)

        def fill(ci, carry):
            r0 = pl.multiple_of(ci * rows, rows)
            pad[pl.ds(pl.multiple_of(HALO + r0, 8), rows), :] = z_ref[pl.ds(r0, rows), :]
            return carry

        lax.fori_loop(0, nchunk, fill, 0)
        wb = w_ref[...].astype(BF16)

        def first(ci, carry):
            r0 = pl.multiple_of(ci * rows, rows)
            win = _window(pad, r0, rows)
            cnt = _pool_count(level, r0, rows)
            pm = (_pool_window_sum(win, level, rows, _delayed) / cnt - win[HALO:HALO + rows]).astype(BF16)
            r = jnp.dot(pm, wb, preferred_element_type=F32)
            d_od = do_ref[pl.ds(r0, rows), :].astype(F32)
            dsc_acc[...] += _fold8(d_od * r)
            dr = (d_od * sc_ref[...]).astype(BF16)
            dw_acc[...] += lax.dot_general(pm, dr, TN, preferred_element_type=F32)
            dpm = lax.dot_general(dr, wb, NT, preferred_element_type=F32)
            pad_q[pl.ds(pl.multiple_of(HALO + r0, 8), rows), :] = dpm / cnt
            return carry

        lax.fori_loop(0, nchunk, first, 0)

        def second(ci, carry):
            r0 = pl.multiple_of(ci * rows, rows)
            wq = _window(pad_q, r0, rows)
            dpm = wq[HALO:HALO + rows] * _pool_count(level, r0, rows)
            dz_ref[pl.ds(r0, rows), :] = (_pool_window_sum(wq, level, rows, _advanced) - dpm).astype(BF16)
            return carry

        lax.fori_loop(0, nchunk, second, 0)
        dw_ref[...] = dw_acc[...]
        dsc_ref[...] = jnp.sum(dsc_acc[...], axis=0, keepdims=True)

    col = _col_spec(s, 0)
    mat = pl.BlockSpec((None, HEAD, HEAD), lambda j: (j, 0, 0))
    vec = pl.BlockSpec((1, HEAD), lambda j: (0, j))
    return pl.pallas_call(
        body, name=name, grid=(POOL_LEVELS,),
        in_specs=[_col_spec(s, 28), col, mat, vec], out_specs=[col, mat, vec],
        out_shape=[jax.ShapeDtypeStruct((s, GROUP), BF16), jax.ShapeDtypeStruct((POOL_LEVELS, HEAD, HEAD), F32),
                   jax.ShapeDtypeStruct((1, GROUP), F32)],
        scratch_shapes=[pltpu.VMEM((s + 2 * HALO, HEAD), F32), pltpu.VMEM((s + 2 * HALO, HEAD), F32),
                        pltpu.VMEM((HEAD, HEAD), F32), pltpu.VMEM((8, HEAD), F32)],
        compiler_params=_cp(("parallel",)),
    )(z, d_o, pool_w, scale)


def _mm_residual(name, x, wg, h, tm, tn):
    m, d = h.shape
    tm = min(tm, m)

    def ep(accs, ins, outs, cols):
        outs[0][:, cols] = ins[5][:, cols] + accs[0]

    tile = pl.BlockSpec((tm, tn), lambda i, j: (i, j))
    return _mm_nn_wide(name, x, wg, "row", tm, tn, epilogue=ep, extra=[h], extra_specs=[tile], out_specs=[tile],
                       out_shape=[jax.ShapeDtypeStruct((m, d), F32)])[0]


def _swiglu_fwd(name, y, wg_gate, wg_up, tm):
    m, kdim = y.shape
    _, a, b = wg_gate.shape
    tm = min(tm, m)

    def ep(accs, ins, outs, cols):
        gt, up = accs
        outs[0][:, cols] = (gt * _sigmoid(gt) * up).astype(BF16)
        outs[1][:, cols] = gt.astype(BF16)
        outs[2][:, cols] = up.astype(BF16)

    gws, gspecs, gdots = _wide_nn_weights(wg_gate, "col", b, 1)
    uws, uspecs, udots = _wide_nn_weights(wg_up, "col", b, 1 + len(gws))
    out = pl.BlockSpec((tm, b), lambda i, j: (i, j))
    return _matmul_wide(name, (m // tm, 4), [y] + gws + uws,
                        [pl.BlockSpec((tm, kdim), lambda i, j: (i, 0))] + gspecs + uspecs, [gdots, udots], b, ep,
                        [out] * 3, [jax.ShapeDtypeStruct((m, 4 * b), BF16)] * 3)


def _swiglu_bwd(name, dh, wg_down, gate, up, tm):
    m, n = dh.shape
    _, a, b = wg_down.shape
    tm = min(tm, m)

    def ep(accs, ins, outs, cols):
        d_act = accs[0]
        gt = ins[3][:, cols].astype(F32)
        upv = ins[4][:, cols].astype(F32)
        sg = _sigmoid(gt)
        outs[0][:, cols] = (d_act * upv * sg * (1.0 + gt * (1.0 - sg))).astype(BF16)
        outs[1][:, cols] = (d_act * gt * sg).astype(BF16)
        outs[2][:, cols] = (gt * sg * upv).astype(BF16)

    ws, wspecs, dots = _wide_nt_weights(wg_down, "row", a, 1)
    tile = pl.BlockSpec((tm, a), lambda i, j: (i, j))
    return _matmul_wide(name, (m // tm, 4), [dh] + ws + [gate, up],
                        [pl.BlockSpec((tm, n), lambda i, j: (i, 0))] + wspecs + [tile, tile], [dots], a, ep,
                        [tile] * 3, [jax.ShapeDtypeStruct((m, 4 * a), BF16)] * 3)


def _ffn_dy(name, d_gate, d_up, wg_gate, wg_up, tiles, after=()):
    m, n = d_gate.shape
    _, a, b = wg_gate.shape
    kdim = 2 * a
    tm, to, tc = tiles
    tm = min(tm, m)
    grid = (m // tm, kdim // to, n // tc)
    lhs = pl.BlockSpec((tm, tc), lambda i, j, k: (i, k))
    wspec = _w_spec_nt(wg_gate, "col", to, tc)
    return _matmul(name, grid, [d_gate, d_up, wg_gate, wg_up], [lhs, lhs, wspec, wspec],
                   [(0, 2, 0, NT), (1, 3, 0, NT)], 1, (tm, to), _ep_store(BF16),
                   [pl.BlockSpec((tm, to), lambda i, j, k: (i, j))], [jax.ShapeDtypeStruct((m, kdim), BF16)],
                   after=after)[0]


def _ple_fwd(name, y, wg, h, pp, tm, tn, after=()):
    m, d = h.shape
    tm = min(tm, m)

    def ep(accs, ins, outs, cols):
        pg = accs[0]
        outs[0][:, cols] = ins[5][:, cols] + _sigmoid(pg) * ins[6][:, cols].astype(F32)
        outs[1][:, cols] = pg.astype(BF16)

    tile = pl.BlockSpec((tm, tn), lambda i, j: (i, j))
    return _mm_nn_wide(name, y, wg, "row", tm, tn, epilogue=ep, extra=[h, pp], extra_specs=[tile, tile],
                       out_specs=[tile, tile],
                       out_shape=[jax.ShapeDtypeStruct((m, d), F32), jax.ShapeDtypeStruct((m, d), BF16)], after=after)


def _ple_bwd(name, dh, pg, pp, after=()):
    s, d = dh.shape
    tr = _row_tile(s, 512)

    def body(dh_ref, pg_ref, pp_ref, *rest):
        dpp_ref, dpg_ref = rest[-2:]
        dhv = dh_ref[...]
        sg = _sigmoid(pg_ref[...].astype(F32))
        dpp_ref[...] = (dhv * sg).astype(BF16)
        dpg_ref[...] = (dhv * pp_ref[...].astype(F32) * sg * (1.0 - sg)).astype(BF16)

    row = pl.BlockSpec((tr, d), lambda i: (i, 0))
    return pl.pallas_call(
        body, name=name, grid=(s // tr,), in_specs=[row] * 3 + [ANY_SPEC] * len(after), out_specs=[row] * 2,
        out_shape=[jax.ShapeDtypeStruct((s, d), BF16)] * 2, compiler_params=_cp(("parallel",)),
    )(dh, pg, pp, *after)


BIG = ["w_in", "w_out", "w_gate", "w_up", "w_down", "w_ple_gate", "w_ple_proj"]
KIND = {"w_in": "col", "w_out": "row", "w_gate": "col", "w_up": "col", "w_down": "row", "w_ple_gate": "row",
        "w_ple_proj": "col"}
GATHER_GROUPS = (("w_in", "w_out"), ("w_gate", "w_up"), ("w_down", "w_ple_gate", "w_ple_proj"))
RS_GROUPS = (("w_ple_gate", "w_ple_proj", "w_down", "w_gate", "w_up"), ("w_out", "w_in"))
SMALL = ["norm_mix_g", "sgu_ln_g", "sgu_ln_b", "sgu_w", "sgu_b", "sc_conv_w", "cf_conv_w", "cf_conv_b", "cf_ln_g",
         "cf_ln_b", "pool_w", "pool_scale", "norm_ffn_g", "norm_ple_g", "final_norm_g"]
CHIP_SPLIT = ["sc_conv_w", "cf_conv_w"]
WEIGHTS = ['norm_mix_g', 'w_in', 'sgu_ln_g', 'sgu_ln_b', 'sgu_w', 'sgu_b', 'sc_conv_w', 'cf_conv_w', 'cf_conv_b',
           'cf_ln_g', 'cf_ln_b', 'pool_w', 'pool_scale', 'w_out', 'norm_ffn_g', 'w_gate', 'w_up', 'w_down',
           'norm_ple_g', 'w_ple_gate', 'w_ple_proj', 'final_norm_g']


def _tile(n, want):
    if n <= want:
        return n
    t = (want // 128) * 128
    while n % t:
        t -= 128
    return t


def _pack_rows(vecs):
    flat = jnp.concatenate([v.reshape(-1) for v in vecs])
    n = flat.shape[0]
    quantum = PACK_ROWS * 128
    padded = ((n + quantum - 1) // quantum) * quantum
    return jnp.pad(flat, (0, padded - n)).reshape(padded // 128, 128), n


def _unpack(flat, shapes):
    out, off = [], 0
    for shp in shapes:
        size = math.prod(shp)
        out.append(flat[off:off + size].reshape(shp))
        off += size
    return out


def kernel(x, p, norm_mix_g, w_in, sgu_ln_g, sgu_ln_b, sgu_w, sgu_b, sc_conv_w, cf_conv_w, cf_conv_b, cf_ln_g, cf_ln_b, pool_w, pool_scale, w_out, norm_ffn_g, w_gate, w_up, w_down, norm_ple_g, w_ple_gate, w_ple_proj, final_norm_g, loss_target, m_norm_mix_g, m_w_in, m_sgu_ln_g, m_sgu_ln_b, m_sgu_w, m_sgu_b, m_sc_conv_w, m_cf_conv_w, m_cf_conv_b, m_cf_ln_g, m_cf_ln_b, m_pool_w, m_pool_scale, m_w_out, m_norm_ffn_g, m_w_gate, m_w_up, m_w_down, m_norm_ple_g, m_w_ple_gate, m_w_ple_proj, m_final_norm_g, v_norm_mix_g, v_w_in, v_sgu_ln_g, v_sgu_ln_b, v_sgu_w, v_sgu_b, v_sc_conv_w, v_cf_conv_w, v_cf_conv_b, v_cf_ln_g, v_cf_ln_b, v_pool_w, v_pool_scale, v_w_out, v_norm_ffn_g, v_w_gate, v_w_up, v_w_down, v_norm_ple_g, v_w_ple_gate, v_w_ple_proj, v_final_norm_g):
    args = dict(locals())
    w = {n: args[n] for n in WEIGHTS}
    mom = {n: args["m_" + n] for n in WEIGHTS}
    var = {n: args["v_" + n] for n in WEIGHTS}
    depth = w_in.shape[0]
    s, d = x.shape[1], x.shape[2]
    f_dim = 4 * w_gate.shape[2]
    xi, yi, ci = lax.axis_index("x"), lax.axis_index("y"), lax.axis_index("c")
    c_idx = ci.astype(jnp.int32).reshape(1)

    chip_idx = (2 * xi + yi).astype(jnp.int32).reshape(1)

    def start_gathers(l, dep):
        pending = []
        for gi, names in enumerate(GATHER_GROUPS):
            lands = [_cast_into_landing(f"cast_{n}", w[n], l, 0 if KIND[n] == "col" else 1, chip_idx) for n in names]
            pending.append(_gather_start(f"gather_start_{l}_{gi}", lands, dep))
            dep = pending[-1][3]
        return pending, dep

    def receive(l, gi, pending, after):
        send_sems, recv_sems, lands, _ = pending
        lands = _gather_wait(f"gather_wait_{l}_{gi}", send_sems, recv_sems, lands, after)
        return _share_start(f"gather_share_start_{l}_{gi}", lands)

    def complete(l, gi, share, after):
        send_sems, recv_sems, lands, _ = share
        lands = _share_wait(f"gather_share_wait_{l}_{gi}", send_sems, recv_sems, lands, after)
        return dict(zip(GATHER_GROUPS[gi], lands))

    conv_pack = jnp.concatenate([sc_conv_w, cf_conv_w], axis=1)
    taps = conv_pack.shape[1]
    rows_pad = ((depth * taps + 7) // 8) * 8
    conv_rows = jnp.pad(conv_pack.reshape(depth * taps, HEAD), ((0, rows_pad - depth * taps), (0, 0)))
    conv_all = _allgather8("gather_conv_weights", [conv_rows])[0]
    conv_full = conv_all[0::2, :depth * taps].reshape(4, depth, taps, HEAD)
    conv_full = jnp.transpose(conv_full, (1, 2, 0, 3)).reshape(depth, taps, GROUP)
    sc_w_full, cf_w_full = conv_full[:, :3], conv_full[:, 3:]

    pending, token = start_gathers(0, conv_all)
    h = x[0]
    saved = []
    gathered = []
    shares = [None] * len(GATHER_GROUPS)
    for l in range(depth):
        if l == 0:
            shares[0] = receive(0, 0, pending[0], token)
        wg = complete(l, 0, shares[0], h)
        gathered.append(wg)
        started = ()
        this_layer = pending
        if l + 1 < depth:
            pending, token = start_gathers(l + 1, wg["w_in"])
            started = (token,)
        sv = {"h0": h}
        y1 = _rms_fwd("rms_mix", h, norm_mix_g[l:l + 1], after=started)
        z = _mm_nn_wide("mm_in", y1, wg["w_in"], "col", 1024, 1024)[0]
        lg, lb = sgu_ln_g[l][:, None, :], sgu_ln_b[l][:, None, :]
        sb = sgu_b[l][:, :, None]
        oa = _sgu_fwd("sgu_fwd", z, lg, lb, sgu_w[l], sb)
        ob = _shortconv_fwd("shortconv_fwd", z, sc_w_full[l])
        cc = _conformer_conv_fwd("conformer_conv_fwd", z, cf_w_full[l], cf_conv_b[l:l + 1])
        oc = _conformer_ln_fwd("conformer_ln_fwd", cc, cf_ln_g[l:l + 1], cf_ln_b[l:l + 1])
        od = _pool_fwd("pool_fwd", z, pool_w[l], pool_scale[l:l + 1])
        o = jnp.concatenate([oa, ob, oc, od], axis=1)
        h1 = _mm_residual("mm_out", o, wg["w_out"], h, 1024, 1024)
        if l == 0:
            shares[1] = receive(0, 1, this_layer[1], h1)
        wg.update(complete(l, 1, shares[1], h1))
        y2 = _rms_fwd("rms_ffn", h1, norm_ffn_g[l:l + 1])
        act, gt, up = _swiglu_fwd("mm_swiglu", y2, wg["w_gate"], wg["w_up"], 512)
        if l == 0:
            shares[2] = receive(0, 2, this_layer[2], act)
        wg.update(complete(l, 2, shares[2], act))
        h2 = _mm_residual("mm_down", act, wg["w_down"], h1, 1024, 512)
        started = ()
        if l + 1 < depth:
            shares[0] = receive(l + 1, 0, pending[0], h2)
            shares[1] = receive(l + 1, 1, pending[1], shares[0][3])
            started = (shares[1][3],)
        y3 = _rms_fwd("rms_ple", h2, norm_ple_g[l:l + 1], after=started)
        pb = p[l, 0].astype(BF16)
        ptile = pl.BlockSpec((min(1024, s), 512), lambda i, j: (i, j))
        pp = _mm_nn_wide("mm_ple_proj", pb, wg["w_ple_proj"], "col", 1024, 512, epilogue=_wide_store(BF16),
                         out_specs=[ptile], out_shape=[jax.ShapeDtypeStruct((s, d), BF16)])[0]
        started = ()
        if l + 1 < depth:
            shares[2] = receive(l + 1, 2, pending[2], pp)
            started = (shares[2][3],)
        h3, pg = _ple_fwd("mm_ple_gate", y3, wg["w_ple_gate"], h2, pp, 1024, 1024, after=started)
        sv.update(y1=y1, z=z, cc=cc, o=o, h1=h1, y2=y2, gt=gt, up=up, h2=h2, y3=y3, pb=pb, pp=pp, pg=pg)
        saved.append(sv)
        h = h3

    loss_part, dh, dhb, d_final_g = _loss_head("loss_head", h, final_norm_g[None, :], loss_target[0])

    small_grads = [None] * depth
    where = jnp.stack([2 * xi + yi, 2 * (1 - xi) + yi, 2 * xi + (1 - yi), 2 * (1 - xi) + (1 - yi), ci]).astype(jnp.int32)
    grad_bufs = {n: lax.empty((depth, 2, w[n].shape[1] // 2, w[n].shape[2]), F32) for n in BIG}
    exchanges = [None] * len(RS_GROUPS)

    def halves(g):
        return g.reshape(4, 2, g.shape[1] // 2, g.shape[2])

    def start_exchange(layer, gi, sibling, after):
        send_sems, recv_sems, gs, lands, _ = sibling
        gs, lands = _sibling_wait(f"rs_sibling_wait_{layer}_{gi}", send_sems, recv_sems, gs, lands, after)
        chip_sums = _add_halves("rs_add", gs, lands, c_idx)
        return _chip_exchange_start(f"rs_chips_start_{layer}_{gi}", chip_sums)

    def finish_exchange(layer, gi, after):
        send_sems, recv_sems, parts, lands, _ = exchanges[gi]
        parts, lands = _chip_exchange_wait(f"rs_chips_wait_{layer}_{gi}", send_sems, recv_sems, parts, lands, after)
        names = RS_GROUPS[gi]
        sums = _sum_parts("rs_sum", parts, lands, [grad_bufs[n] for n in names], layer, where)
        grad_bufs.update(zip(names, sums))

    for l in reversed(range(depth)):
        wg = gathered[l]
        sv = saved[l]
        fs = f_dim // 4
        started = () if exchanges[1] is None else (exchanges[1][4],)
        d_pp, d_pg = _ple_bwd("ple_bwd", dh, sv["pg"], sv["pp"], after=started)
        g_ple_proj = _mm_tn_wide("dw_ple_proj", sv["pb"], d_pp, "col", w_ple_proj.shape[1], 512)
        g_ple_gate = _mm_tn_wide("dw_ple_gate", sv["y3"], d_pg, "row", 512, 1024)
        dy3 = _mm_nt_wide("dx_ple_gate", d_pg, wg["w_ple_gate"], "row", 1024, 512)
        dh, dhb, dg_ple = _rms_bwd("rms_ple_bwd", sv["h2"], norm_ple_g[l:l + 1], dy3, dh)

        d_gt, d_up, act = _swiglu_bwd("dx_down_swiglu", dhb, wg["w_down"], sv["gt"], sv["up"], 512)
        g_down = _mm_tn("dw_down", act, dhb, "row", (fs, 1024, 1024))
        g_gate = _mm_tn("dw_gate", sv["y2"], d_gt, "col", (1024, fs, 1024))
        g_up = _mm_tn("dw_up", sv["y2"], d_up, "col", (1024, fs, 1024))
        big = dict(w_gate=g_gate, w_up=g_up, w_down=g_down, w_ple_gate=g_ple_gate, w_ple_proj=g_ple_proj)
        sibling = _sibling_start(f"rs_sibling_start_{l}_0", [halves(big[n]) for n in RS_GROUPS[0]])
        dy2 = _ffn_dy("dx_gate_up", d_gt, d_up, wg["w_gate"], wg["w_up"], (1024, 1024, fs), after=(sibling[4],))
        dh, dhb, dg_ffn = _rms_bwd("rms_ffn_bwd", sv["h1"], norm_ffn_g[l:l + 1], dy2, dh)

        g_out = _mm_tn_wide("dw_out", sv["o"], dhb, "row", 512, 1024)
        if exchanges[0] is not None:
            finish_exchange(l + 1, 0, g_out)
        exchanges[0] = start_exchange(l, 0, sibling, g_out)
        d_o = _mm_nt_wide("dx_out", dhb, wg["w_out"], "row", 1024, 512, after=(exchanges[0][4],))
        z = sv["z"]
        lg, lb = sgu_ln_g[l][:, None, :], sgu_ln_b[l][:, None, :]
        sb = sgu_b[l][:, :, None]
        dzu, dzv, d_lg, d_lb, d_sw, d_sb = _sgu_bwd("sgu_bwd", z, d_o[:, 0:GROUP], lg, lb, sgu_w[l], sb)
        dzh, dzbg, dzcg, d_scw = _shortconv_bwd("shortconv_bwd", z, d_o[:, GROUP:2 * GROUP], sc_w_full[l])
        dcc, d_cflg, d_cflb, d_cfb = _conformer_ln_bwd("conformer_ln_bwd", sv["cc"], d_o[:, 2 * GROUP:3 * GROUP],
                                                       cf_ln_g[l:l + 1], cf_ln_b[l:l + 1])
        dza, dzg, d_cfw = _conformer_conv_bwd("conformer_conv_bwd", z, dcc, cf_w_full[l])
        dzd, d_pw, d_psc = _pool_bwd("pool_bwd", z, d_o[:, 3 * GROUP:], pool_w[l], pool_scale[l:l + 1])
        dz = jnp.concatenate([dzu, dzv, dzh, dzbg, dzcg, dza, dzg, dzd], axis=1)
        g_in = _mm_tn("dw_in", sv["y1"], dz, "col", (1024, 1024, 1024))
        big.update(w_out=g_out, w_in=g_in)
        sibling = _sibling_start(f"rs_sibling_start_{l}_1", [halves(big[n]) for n in RS_GROUPS[1]])
        dy1 = _mm_nt_wide("dx_in", dz, wg["w_in"], "col", 1024, 512, after=(sibling[4],))
        dh, dhb, dg_mix = _rms_bwd("rms_mix_bwd", sv["h0"], norm_mix_g[l:l + 1], dy1, dh)
        if exchanges[1] is not None:
            finish_exchange(l + 1, 1, dh)
        exchanges[1] = start_exchange(l, 1, sibling, dh)

        small_grads[l] = dict(norm_mix_g=dg_mix, sgu_ln_g=d_lg, sgu_ln_b=d_lb, sgu_w=d_sw, sgu_b=d_sb,
                              sc_conv_w=d_scw, cf_conv_w=d_cfw, cf_conv_b=d_cfb, cf_ln_g=d_cflg, cf_ln_b=d_cflb,
                              pool_w=d_pw, pool_scale=d_psc, norm_ffn_g=dg_ffn, norm_ple_g=dg_ple)
    grad_x = dh[None]

    grads, delta, new_m, new_v = {}, {}, {}, {}
    per_layer = [n for n in SMALL if n != "final_norm_g"]
    vecs = [small_grads[l][n] for n in per_layer for l in range(depth)] + [d_final_g, loss_part]
    packed, _ = _pack_rows(vecs)
    total = _sum8("sum_small", _allgather8("gather_small", [packed])[0]).reshape(-1)
    stacked_shapes = [(depth,) + (w[n].shape[1:] if n not in CHIP_SPLIT else (w[n].shape[1], GROUP)) for n in per_layer]
    pieces = _unpack(total, stacked_shapes + [(d,), ()])
    loss = pieces[-1]
    grads["final_norm_g"] = pieces[-2]
    chip_off = (2 * xi + yi) * HEAD
    for n, g in zip(per_layer, pieces):
        grads[n] = lax.dynamic_slice_in_dim(g, chip_off, HEAD, axis=2) if n in CHIP_SPLIT else g

    finish_exchange(0, 0, dh)
    behind = (exchanges[1][4],)
    for gi, names in enumerate(RS_GROUPS):
        if gi == 1:
            finish_exchange(0, 1, behind[0])
        joined = _sibling_join_halves(f"rs_join_{gi}", [grad_bufs[n] for n in names], after=behind)
        for n, g in zip(names, joined):
            shp = w[n].shape
            two_d = (shp[0] * shp[1], shp[2])
            grads[n] = g.reshape(shp)
            dl, mn, vn = _adamw(f"adamw_{n}", w[n].reshape(two_d), g.reshape(two_d), mom[n].reshape(two_d),
                                var[n].reshape(two_d))
            delta[n], new_m[n], new_v[n] = dl.reshape(shp), mn.reshape(shp), vn.reshape(shp)
            behind = (dl,)
    small_shapes = [w[n].shape for n in SMALL]
    pw, _ = _pack_rows([w[n] for n in SMALL])
    pg_, _ = _pack_rows([grads[n] for n in SMALL])
    pm, _ = _pack_rows([mom[n] for n in SMALL])
    pv, _ = _pack_rows([var[n] for n in SMALL])
    dl, mn, vn = _adamw("adamw_small", pw, pg_, pm, pv)
    for n, a, b, cc_ in zip(SMALL, _unpack(dl.reshape(-1), small_shapes), _unpack(mn.reshape(-1), small_shapes),
                            _unpack(vn.reshape(-1), small_shapes)):
        delta[n], new_m[n], new_v[n] = a, b, cc_

    return (loss, grad_x, *[grads[n] for n in WEIGHTS], *[delta[n] for n in WEIGHTS],
            *[new_m[n] for n in WEIGHTS], *[new_v[n] for n in WEIGHTS])
```

```python
import functools
import math

import jax
import jax.numpy as jnp
from jax import lax
from jax.experimental import pallas as pl
from jax.experimental.pallas import tpu as pltpu

F32 = jnp.float32
BF16 = jnp.bfloat16
MESH = pl.DeviceIdType.MESH

HEAD = 128
GROUP = 512
EPS = 1e-6
HALO = 32
CHUNK_ROWS = 128
POOL_LEVELS = 4
PACK_ROWS = 512

ADAM_LR = 0.001
ADAM_B1 = 0.9
ADAM_B2 = 0.999
ADAM_EPS = 1e-08
ADAM_WD = 0.01
ADAM_STEP = 10

VMEM_LIMIT = 56 * 1024 * 1024


def _cp(sem=None, vmem=VMEM_LIMIT):
    return pltpu.CompilerParams(dimension_semantics=sem, vmem_limit_bytes=vmem)


def _sigmoid(x):
    return 0.5 * jnp.tanh(0.5 * x) + 0.5


_GELU_K = math.sqrt(2.0 / math.pi)
_GELU_C = 0.044715


def _gelu(x):
    t = jnp.tanh(_GELU_K * (x + _GELU_C * x * x * x))
    return 0.5 * x * (1.0 + t)


def _gelu_grad(x):
    t = jnp.tanh(_GELU_K * (x + _GELU_C * x * x * x))
    return 0.5 * (1.0 + t) + 0.5 * x * (1.0 - t * t) * _GELU_K * (1.0 + 3.0 * _GELU_C * x * x)


def _mesh_pos():
    return lax.axis_index("x"), lax.axis_index("y"), lax.axis_index("c")


def _any_specs(n):
    return [pl.BlockSpec(memory_space=pl.ANY)] * n


def _allgather8(name, blocks):
    n = len(blocks)

    def body(*refs):
        ins, outs = refs[:n], refs[n:2 * n]
        send_sems, recv_sems, local_sems = refs[2 * n:]
        x, y, c = _mesh_pos()
        me, sibling = (x, y, c), (x, y, 1 - c)
        chips = [(1 - x, y), (x, 1 - y), (1 - x, 1 - y)]

        def slot(a, dev):
            return outs[a].at[4 * dev[0] + 2 * dev[1] + dev[2]]

        def copy(a, k, block, to, src=None):
            dst = slot(a, block)
            return pltpu.make_async_remote_copy(
                src_ref=dst if src is None else src, dst_ref=dst,
                send_sem=send_sems.at[7 * a + k], recv_sem=recv_sems.at[7 * a + k],
                device_id=to, device_id_type=MESH)

        mine, first, passed = [], [], []
        for a in range(n):
            cp = pltpu.make_async_copy(ins[a], slot(a, me), local_sems.at[a])
            cp.start()
            mine.append(cp)
            cps = [copy(a, 0, me, sibling, src=ins[a])]
            cps += [copy(a, 1 + j, me, (*chip, c), src=ins[a]) for j, chip in enumerate(chips)]
            for cp in cps:
                cp.start()
            first += cps
        for j, chip in enumerate(chips):
            for a in range(n):
                copy(a, 1 + j, (*chip, c), me).wait_recv()
                cp = copy(a, 4 + j, (*chip, c), sibling)
                cp.start()
                passed.append(cp)
        for a in range(n):
            copy(a, 0, sibling, me).wait_recv()
            for j, chip in enumerate(chips):
                copy(a, 4 + j, (*chip, 1 - c), me).wait_recv()
        for cp in first + passed:
            cp.wait_send()
        for cp in mine:
            cp.wait()

    return pl.pallas_call(
        body, name=name,
        out_shape=[jax.ShapeDtypeStruct((8,) + b.shape, b.dtype) for b in blocks],
        in_specs=_any_specs(n), out_specs=_any_specs(n),
        scratch_shapes=[pltpu.SemaphoreType.DMA((7 * n,)), pltpu.SemaphoreType.DMA((7 * n,)),
                        pltpu.SemaphoreType.DMA((n,))],
    )(*blocks)


HBM_SPEC = pl.BlockSpec(memory_space=pltpu.HBM)
SEM_SPEC = pl.BlockSpec(memory_space=pltpu.SEMAPHORE)
ANY_SPEC = pl.BlockSpec(memory_space=pl.ANY)
SPLIT_COPY = pltpu.CompilerParams(has_side_effects=pltpu.SideEffectType.DATAFLOW_SIDE_EFFECTING)


def _hbm(x):
    return pltpu.with_memory_space_constraint(x, pltpu.HBM)


def _other_chips(x, y):
    return [(1 - x, y), (x, 1 - y), (1 - x, 1 - y)]


def _dev_slot(ref, dev):
    return ref.at[4 * dev[0] + 2 * dev[1] + dev[2]]


def _cast_into_landing(name, w, layer, ax, chip_idx):
    _, r, cc = w.shape
    r2, c2 = (r // 2, cc) if ax == 0 else (r, cc // 2)
    tr = _row_tile(r2, 512 if c2 <= 1536 else 256)
    nt = r2 // tr

    def body(chip_ref, w_ref, o_ref):
        o_ref[...] = w_ref[...].astype(BF16)

    if ax == 0:
        in_spec = pl.BlockSpec((None, tr, c2), lambda hf, i, chip_ref: (layer, hf * nt + i, 0))
    else:
        in_spec = pl.BlockSpec((None, tr, c2), lambda hf, i, chip_ref: (layer, i, hf))
    return pl.pallas_call(
        body, name=name,
        grid_spec=pltpu.PrefetchScalarGridSpec(
            num_scalar_prefetch=1, grid=(2, nt), in_specs=[in_spec],
            out_specs=pl.BlockSpec((None, tr, c2), lambda hf, i, chip_ref: (2 * chip_ref[0] + hf, i, 0))),
        out_shape=jax.ShapeDtypeStruct((8, r2, c2), BF16), compiler_params=_cp(("parallel", "parallel")),
    )(chip_idx, w)


def _gather_ici_copies(lands, send_sems, recv_sems):
    x, y, c = _mesh_pos()
    pairs = []
    for a in range(len(lands)):
        for j, chip in enumerate(_other_chips(x, y)):
            def copy(dev):
                return pltpu.make_async_remote_copy(
                    src_ref=_dev_slot(lands[a], dev), dst_ref=_dev_slot(lands[a], dev),
                    send_sem=send_sems.at[3 * a + j], recv_sem=recv_sems.at[3 * a + j],
                    device_id=(*chip, c), device_id_type=MESH)
            pairs.append((copy((x, y, c)), copy((*chip, c))))
    return pairs


def _exchange_ici_copies(ins, lands, send_sems, recv_sems):
    x, y, c = _mesh_pos()
    pairs = []
    for a in range(len(ins)):
        for k, chip in enumerate(_other_chips(x, y)):
            there = 2 * chip[0] + chip[1]
            def copy(dst_entry):
                return pltpu.make_async_remote_copy(
                    src_ref=ins[a].at[there], dst_ref=lands[a].at[dst_entry],
                    send_sem=send_sems.at[3 * a + k], recv_sem=recv_sems.at[3 * a + k],
                    device_id=(*chip, c), device_id_type=MESH)
            pairs.append((copy(2 * x + y), copy(there)))
    return pairs


def _gather_start(name, lands, dep):
    n = len(lands)

    def body(*refs):
        token = refs[-1]
        for send, _ in _gather_ici_copies(refs[:n], refs[n + 1], refs[n + 2]):
            send.start()
        token[...] = jnp.zeros_like(token)

    out = pl.pallas_call(
        body, name=name,
        out_shape=(pltpu.SemaphoreType.DMA((3 * n,)), pltpu.SemaphoreType.DMA((3 * n,)),
                   *[pltpu.HBM(l.shape, l.dtype) for l in lands], jax.ShapeDtypeStruct((8, 128), F32)),
        in_specs=[HBM_SPEC] * n + [ANY_SPEC],
        out_specs=(SEM_SPEC, SEM_SPEC, *[HBM_SPEC] * n, pl.BlockSpec(memory_space=pltpu.VMEM)),
        input_output_aliases={a: 2 + a for a in range(n)},
        compiler_params=SPLIT_COPY,
    )(*[_hbm(l) for l in lands], dep)
    return out[0], out[1], list(out[2:2 + n]), out[-1]


def _gather_wait(name, send_sems, recv_sems, lands, after):
    n = len(lands)

    def body(*refs):
        for send, recv in _gather_ici_copies(refs[:n], refs[n], refs[n + 1]):
            send.wait_send()
            recv.wait_recv()

    out = pl.pallas_call(
        body, name=name,
        out_shape=[pltpu.HBM(l.shape, l.dtype) for l in lands],
        in_specs=[HBM_SPEC] * n + [SEM_SPEC, SEM_SPEC, ANY_SPEC],
        out_specs=[HBM_SPEC] * n,
        input_output_aliases={a: a for a in range(n)},
        compiler_params=SPLIT_COPY,
    )(*lands, send_sems, recv_sems, after)
    return list(out)


def _share_copies(lands, send_sems, recv_sems):
    x, y, c = _mesh_pos()
    pairs = []
    for a in range(len(lands)):
        for j, chip in enumerate(_other_chips(x, y)):
            def copy(dev):
                slot = _dev_slot(lands[a], dev)
                return pltpu.make_async_remote_copy(
                    src_ref=slot, dst_ref=slot, send_sem=send_sems.at[3 * a + j], recv_sem=recv_sems.at[3 * a + j],
                    device_id=(x, y, 1 - c), device_id_type=MESH)
            pairs.append((copy((*chip, c)), copy((*chip, 1 - c))))
    return pairs


def _share_start(name, lands):
    n = len(lands)

    def body(*refs):
        token = refs[-1]
        for send, _ in _share_copies(refs[:n], refs[n], refs[n + 1]):
            send.start()
        token[...] = jnp.zeros_like(token)

    out = pl.pallas_call(
        body, name=name,
        out_shape=(pltpu.SemaphoreType.DMA((3 * n,)), pltpu.SemaphoreType.DMA((3 * n,)),
                   *[pltpu.HBM(l.shape, l.dtype) for l in lands], jax.ShapeDtypeStruct((8, 128), F32)),
        in_specs=[HBM_SPEC] * n,
        out_specs=(SEM_SPEC, SEM_SPEC, *[HBM_SPEC] * n, pl.BlockSpec(memory_space=pltpu.VMEM)),
        input_output_aliases={a: 2 + a for a in range(n)},
        compiler_params=SPLIT_COPY,
    )(*lands)
    return out[0], out[1], list(out[2:2 + n]), out[-1]


def _share_wait(name, send_sems, recv_sems, lands, after):
    n = len(lands)

    def body(*refs):
        for send, recv in _share_copies(refs[:n], refs[n], refs[n + 1]):
            send.wait_send()
            recv.wait_recv()

    out = pl.pallas_call(
        body, name=name,
        out_shape=[pltpu.HBM(l.shape, l.dtype) for l in lands],
        in_specs=[HBM_SPEC] * n + [SEM_SPEC, SEM_SPEC, ANY_SPEC],
        out_specs=[HBM_SPEC] * n,
        input_output_aliases={a: a for a in range(n)},
        compiler_params=SPLIT_COPY,
    )(*lands, send_sems, recv_sems, after)
    return list(out)


def _sibling_copies(ins, lands, send_sems, recv_sems):
    x, y, c = _mesh_pos()
    return [pltpu.make_async_remote_copy(
        src_ref=ins[a].at[j, 1 - c], dst_ref=lands[a].at[j],
        send_sem=send_sems.at[4 * a + j], recv_sem=recv_sems.at[4 * a + j],
        device_id=(x, y, 1 - c), device_id_type=MESH) for a in range(len(ins)) for j in range(4)]


def _sibling_start(name, grads):
    n = len(grads)

    def body(*refs):
        token = refs[-1]
        for cp in _sibling_copies(refs[:n], refs[n:2 * n], refs[2 * n], refs[2 * n + 1]):
            cp.start()
        token[...] = jnp.zeros_like(token)

    lands = [_hbm(lax.empty((4,) + g.shape[2:], g.dtype)) for g in grads]
    out = pl.pallas_call(
        body, name=name,
        out_shape=(pltpu.SemaphoreType.DMA((4 * n,)), pltpu.SemaphoreType.DMA((4 * n,)),
                   *[pltpu.HBM(g.shape, g.dtype) for g in grads], *[pltpu.HBM(l.shape, l.dtype) for l in lands],
                   jax.ShapeDtypeStruct((8, 128), F32)),
        in_specs=[HBM_SPEC] * (2 * n),
        out_specs=(SEM_SPEC, SEM_SPEC, *[HBM_SPEC] * (2 * n), pl.BlockSpec(memory_space=pltpu.VMEM)),
        input_output_aliases={i: 2 + i for i in range(2 * n)},
        compiler_params=SPLIT_COPY,
    )(*[_hbm(g) for g in grads], *lands)
    return out[0], out[1], list(out[2:2 + n]), list(out[2 + n:2 + 2 * n]), out[-1]


def _sibling_wait(name, send_sems, recv_sems, grads, lands, after):
    n = len(grads)

    def body(*refs):
        for cp in _sibling_copies(refs[:n], refs[n:2 * n], refs[2 * n], refs[2 * n + 1]):
            cp.wait_send()
            cp.wait_recv()

    out = pl.pallas_call(
        body, name=name,
        out_shape=[pltpu.HBM(g.shape, g.dtype) for g in grads] + [pltpu.HBM(l.shape, l.dtype) for l in lands],
        in_specs=[HBM_SPEC] * (2 * n) + [SEM_SPEC, SEM_SPEC, ANY_SPEC],
        out_specs=[HBM_SPEC] * (2 * n),
        input_output_aliases={i: i for i in range(2 * n)},
        compiler_params=SPLIT_COPY,
    )(*grads, *lands, send_sems, recv_sems, after)
    return list(out[:n]), list(out[n:])


def _chip_exchange_start(name, parts):
    n = len(parts)

    def body(*refs):
        ins, lands = refs[:n], refs[n:2 * n]
        token = refs[-1]
        for send, _ in _exchange_ici_copies(ins, lands, refs[2 * n], refs[2 * n + 1]):
            send.start()
        token[...] = jnp.zeros_like(token)

    lands = [_hbm(lax.empty(p.shape, p.dtype)) for p in parts]
    out = pl.pallas_call(
        body, name=name,
        out_shape=(pltpu.SemaphoreType.DMA((3 * n,)), pltpu.SemaphoreType.DMA((3 * n,)),
                   *[pltpu.HBM(p.shape, p.dtype) for p in parts], *[pltpu.HBM(p.shape, p.dtype) for p in parts],
                   jax.ShapeDtypeStruct((8, 128), F32)),
        in_specs=[HBM_SPEC] * (2 * n),
        out_specs=(SEM_SPEC, SEM_SPEC, *[HBM_SPEC] * (2 * n), pl.BlockSpec(memory_space=pltpu.VMEM)),
        input_output_aliases={i: 2 + i for i in range(2 * n)},
        compiler_params=SPLIT_COPY,
    )(*[_hbm(p) for p in parts], *lands)
    return out[0], out[1], list(out[2:2 + n]), list(out[2 + n:2 + 2 * n]), out[-1]


def _chip_exchange_wait(name, send_sems, recv_sems, parts, lands, after):
    n = len(parts)

    def body(*refs):
        ins, lands_in = refs[:n], refs[n:2 * n]
        for send, recv in _exchange_ici_copies(ins, lands_in, refs[2 * n], refs[2 * n + 1]):
            send.wait_send()
            recv.wait_recv()

    out = pl.pallas_call(
        body, name=name,
        out_shape=[pltpu.HBM(p.shape, p.dtype) for p in parts] * 2,
        in_specs=[HBM_SPEC] * (2 * n) + [SEM_SPEC, SEM_SPEC, ANY_SPEC],
        out_specs=[HBM_SPEC] * (2 * n),
        input_output_aliases={i: i for i in range(2 * n)},
        compiler_params=SPLIT_COPY,
    )(*parts, *lands, send_sems, recv_sems, after)
    return list(out[:n]), list(out[n:])


def _join_copies(bufs, send_sems, recv_sems, layer):
    x, y, c = _mesh_pos()

    def copy(a, half):
        return pltpu.make_async_remote_copy(
            src_ref=bufs[a].at[layer, half], dst_ref=bufs[a].at[layer, half],
            send_sem=send_sems.at[a], recv_sem=recv_sems.at[a], device_id=(x, y, 1 - c), device_id_type=MESH)

    return [(copy(a, c), copy(a, 1 - c)) for a in range(len(bufs))]


def _sibling_join_halves(name, bufs, layer, after=()):
    n = len(bufs)

    def body(*refs):
        outs = refs[n + len(after):2 * n + len(after)]
        pairs = _join_copies(outs, refs[-2], refs[-1], layer)
        for send, _ in pairs:
            send.start()
        for send, recv in pairs:
            send.wait_send()
            recv.wait_recv()

    return pl.pallas_call(
        body, name=name, out_shape=[jax.ShapeDtypeStruct(b.shape, b.dtype) for b in bufs],
        in_specs=_any_specs(n + len(after)), out_specs=_any_specs(n),
        input_output_aliases={a: a for a in range(n)},
        scratch_shapes=[pltpu.SemaphoreType.DMA((n,)), pltpu.SemaphoreType.DMA((n,))],
    )(*bufs, *after)


def _join_start(name, bufs, layer):
    n = len(bufs)

    def body(*refs):
        token = refs[-1]
        for send, _ in _join_copies(refs[:n], refs[n], refs[n + 1], layer):
            send.start()
        token[...] = jnp.zeros_like(token)

    out = pl.pallas_call(
        body, name=name,
        out_shape=(pltpu.SemaphoreType.DMA((n,)), pltpu.SemaphoreType.DMA((n,)),
                   *[pltpu.HBM(b.shape, b.dtype) for b in bufs], jax.ShapeDtypeStruct((8, 128), F32)),
        in_specs=[HBM_SPEC] * n,
        out_specs=(SEM_SPEC, SEM_SPEC, *[HBM_SPEC] * n, pl.BlockSpec(memory_space=pltpu.VMEM)),
        input_output_aliases={a: 2 + a for a in range(n)},
        compiler_params=SPLIT_COPY,
    )(*[_hbm(b) for b in bufs])
    return out[0], out[1], list(out[2:2 + n]), out[-1]


def _join_wait(name, send_sems, recv_sems, bufs, layer, after):
    n = len(bufs)

    def body(*refs):
        for send, recv in _join_copies(refs[:n], refs[n], refs[n + 1], layer):
            send.wait_send()
            recv.wait_recv()

    out = pl.pallas_call(
        body, name=name,
        out_shape=[pltpu.HBM(b.shape, b.dtype) for b in bufs],
        in_specs=[HBM_SPEC] * n + [SEM_SPEC, SEM_SPEC, ANY_SPEC],
        out_specs=[HBM_SPEC] * n,
        input_output_aliases={a: a for a in range(n)},
        compiler_params=SPLIT_COPY,
    )(*bufs, send_sems, recv_sems, after)
    return list(out)


def _row_tile(rows, want):
    t = 1
    while t * 2 <= min(rows, want):
        t *= 2
    while rows % t:
        t //= 2
    return t


def _add_halves(name, grads, recvd, c_idx):
    outs = []
    for a, (g, r) in enumerate(zip(grads, recvd)):
        _, _, r2, cc = g.shape
        tr = _row_tile(r2, 512)

        def body(c_ref, g_ref, r_ref, o_ref):
            o_ref[...] = (g_ref[...].astype(F32) + r_ref[...].astype(F32)).astype(BF16)

        outs.append(pl.pallas_call(
            body, name=f"{name}_{a}",
            grid_spec=pltpu.PrefetchScalarGridSpec(
                num_scalar_prefetch=1, grid=(4, r2 // tr),
                in_specs=[pl.BlockSpec((None, None, tr, cc), lambda j, i, c_ref: (j, c_ref[0], i, 0)),
                          pl.BlockSpec((None, tr, cc), lambda j, i, c_ref: (j, i, 0))],
                out_specs=pl.BlockSpec((None, tr, cc), lambda j, i, c_ref: (j, i, 0))),
            out_shape=jax.ShapeDtypeStruct(r.shape, BF16), compiler_params=_cp(("parallel", "parallel")),
        )(c_idx, g, r))
    return outs


def _sum_parts(name, parts, recvd, bufs, layer, where):
    outs = []
    for a, (p, r, buf) in enumerate(zip(parts, recvd, bufs)):
        _, r2, cc = p.shape
        tr = _row_tile(r2, 256)

        def body(where_ref, own_ref, r1_ref, r2_ref, r3_ref, buf_ref, o_ref):
            acc = own_ref[...].astype(F32)
            for ref in (r1_ref, r2_ref, r3_ref):
                acc = acc + ref[...].astype(F32)
            o_ref[...] = acc

        def entry(k):
            return pl.BlockSpec((None, tr, cc), lambda i, where_ref: (where_ref[k], i, 0))

        outs.append(pl.pallas_call(
            body, name=f"{name}_{a}",
            grid_spec=pltpu.PrefetchScalarGridSpec(
                num_scalar_prefetch=1, grid=(r2 // tr,),
                in_specs=[entry(0), entry(1), entry(2), entry(3), ANY_SPEC],
                out_specs=pl.BlockSpec((None, None, tr, cc), lambda i, where_ref: (layer, where_ref[4], i, 0))),
            out_shape=jax.ShapeDtypeStruct(buf.shape, F32), input_output_aliases={5: 0},
            compiler_params=_cp(("parallel",)),
        )(where, p, r, r, r, buf))
    return outs


def _sum8(name, gathered):
    _, r, cc = gathered.shape
    tr = _row_tile(r, 512)

    def body(g_ref, o_ref):
        acc = g_ref[0]
        for i in range(1, 8):
            acc = acc + g_ref[i]
        o_ref[...] = acc

    return pl.pallas_call(
        body, name=name, grid=(r // tr,),
        in_specs=[pl.BlockSpec((8, tr, cc), lambda i: (0, i, 0))],
        out_specs=pl.BlockSpec((tr, cc), lambda i: (i, 0)),
        out_shape=jax.ShapeDtypeStruct((r, cc), F32), compiler_params=_cp(("parallel",)),
    )(gathered)


def _adamw(name, w, g, m, v):
    r, cc = w.shape
    tr = _row_tile(r, max(8, (256 * 1024) // cc))

    def body(w_ref, g_ref, m_ref, v_ref, d_ref, mo_ref, vo_ref):
        gg = g_ref[...]
        mn = ADAM_B1 * m_ref[...] + (1.0 - ADAM_B1) * gg
        vn = ADAM_B2 * v_ref[...] + (1.0 - ADAM_B2) * (gg * gg)
        m_hat = mn / (1.0 - ADAM_B1 ** ADAM_STEP)
        v_hat = vn / (1.0 - ADAM_B2 ** ADAM_STEP)
        d_ref[...] = -ADAM_LR * (m_hat / (jnp.sqrt(v_hat) + ADAM_EPS) + ADAM_WD * w_ref[...])
        mo_ref[...] = mn
        vo_ref[...] = vn

    spec = pl.BlockSpec((tr, cc), lambda i: (i, 0))
    return pl.pallas_call(
        body, name=name, grid=(r // tr,), in_specs=[spec] * 4, out_specs=[spec] * 3,
        out_shape=[jax.ShapeDtypeStruct(w.shape, F32)] * 3, compiler_params=_cp(("parallel",)),
    )(w, g, m, v)


def _rms_fwd(name, h, g, after=()):
    s, d = h.shape
    tr = _row_tile(s, 512)

    def body(h_ref, g_ref, *rest):
        y_ref = rest[-1]
        xv = h_ref[...]
        r = lax.rsqrt(jnp.mean(xv * xv, axis=-1, keepdims=True) + EPS)
        y_ref[...] = (xv * r * g_ref[...]).astype(BF16)

    return pl.pallas_call(
        body, name=name, grid=(s // tr,),
        in_specs=[pl.BlockSpec((tr, d), lambda i: (i, 0)), pl.BlockSpec((1, d), lambda i: (0, 0))]
        + [ANY_SPEC] * len(after),
        out_specs=pl.BlockSpec((tr, d), lambda i: (i, 0)),
        out_shape=jax.ShapeDtypeStruct((s, d), BF16), compiler_params=_cp(("parallel",)),
    )(h, g, *after)


def _rms_bwd_rows(xv, gv, dy):
    d = xv.shape[-1]
    r = lax.rsqrt(jnp.mean(xv * xv, axis=-1, keepdims=True) + EPS)
    dxn = dy * gv
    proj = jnp.sum(dxn * xv, axis=-1, keepdims=True) * (1.0 / d)
    dx = r * dxn - xv * (r * r * r) * proj
    return dx, dy * xv * r


def _rms_bwd(name, h, g, dy, dres):
    s, d = h.shape
    tr = _row_tile(s, 256)

    def body(h_ref, g_ref, dy_ref, dres_ref, dh_ref, dhb_ref, dg_ref):
        dx, dgp = _rms_bwd_rows(h_ref[...], g_ref[...], dy_ref[...].astype(F32))
        dh = dres_ref[...] + dx
        dh_ref[...] = dh
        dhb_ref[...] = dh.astype(BF16)

        @pl.when(pl.program_id(0) == 0)
        def _():
            dg_ref[...] = jnp.zeros_like(dg_ref)

        dg_ref[...] += jnp.sum(dgp, axis=0, keepdims=True)

    row = pl.BlockSpec((tr, d), lambda i: (i, 0))
    vec = pl.BlockSpec((1, d), lambda i: (0, 0))
    return pl.pallas_call(
        body, name=name, grid=(s // tr,), in_specs=[row, vec, row, row], out_specs=[row, row, vec],
        out_shape=[jax.ShapeDtypeStruct((s, d), F32), jax.ShapeDtypeStruct((s, d), BF16),
                   jax.ShapeDtypeStruct((1, d), F32)],
        compiler_params=_cp(("arbitrary",)),
    )(h, g, dy, dres)


def _loss_head(name, h, g, target):
    s, d = h.shape
    tr = _row_tile(s, 256)

    def body(h_ref, g_ref, t_ref, loss_ref, dh_ref, dhb_ref, dg_ref):
        xv = h_ref[...]
        gv = g_ref[...]
        r = lax.rsqrt(jnp.mean(xv * xv, axis=-1, keepdims=True) + EPS)
        err = xv * r * gv - t_ref[...]
        part = 0.5 * jnp.sum(jnp.sum(err * err, axis=-1, keepdims=True) * (1.0 / d), axis=0, keepdims=True)
        dx, dgp = _rms_bwd_rows(xv, gv, err * (1.0 / d))
        dh_ref[...] = dx
        dhb_ref[...] = dx.astype(BF16)

        @pl.when(pl.program_id(0) == 0)
        def _():
            dg_ref[...] = jnp.zeros_like(dg_ref)
            loss_ref[...] = jnp.zeros_like(loss_ref)

        dg_ref[...] += jnp.sum(dgp, axis=0, keepdims=True)
        loss_ref[...] += part

    row = pl.BlockSpec((tr, d), lambda i: (i, 0))
    vec = pl.BlockSpec((1, d), lambda i: (0, 0))
    one = pl.BlockSpec((1, 1), lambda i: (0, 0))
    return pl.pallas_call(
        body, name=name, grid=(s // tr,), in_specs=[row, vec, row], out_specs=[one, row, row, vec],
        out_shape=[jax.ShapeDtypeStruct((1, 1), F32), jax.ShapeDtypeStruct((s, d), F32),
                   jax.ShapeDtypeStruct((s, d), BF16), jax.ShapeDtypeStruct((1, d), F32)],
        compiler_params=_cp(("arbitrary",)),
    )(h, g, target)


NN = (((1,), (0,)), ((), ()))
NT = (((1,), (1,)), ((), ()))
TN = (((0,), (0,)), ((), ()))


def _matmul(name, grid, operands, in_specs, pairs, n_acc, acc_shape, epilogue, out_specs, out_shape, after=()):
    operands = list(operands) + list(after)
    in_specs = list(in_specs) + [ANY_SPEC] * len(after)
    n_in = len(operands)
    n_out = len(out_shape)
    nk = grid[2]

    def body(*refs):
        ins, outs, accs = refs[:n_in], refs[n_in:n_in + n_out], refs[n_in + n_out:]
        k = pl.program_id(2)

        @pl.when(k == 0)
        def _():
            for acc in accs:
                acc[...] = jnp.zeros_like(acc)

        for li, ri, ai, dims in pairs:
            accs[ai][...] += lax.dot_general(ins[li][...], ins[ri][...], dims, preferred_element_type=F32)

        @pl.when(k == nk - 1)
        def _():
            epilogue([acc[...] for acc in accs], ins, outs)

    return pl.pallas_call(
        body, name=name, grid=grid, in_specs=in_specs, out_specs=out_specs, out_shape=out_shape,
        scratch_shapes=[pltpu.VMEM(acc_shape, F32)] * n_acc,
        compiler_params=_cp(("parallel", "parallel", "arbitrary")),
    )(*operands)


def _ep_store(dtype):
    def ep(accs, ins, outs):
        outs[0][...] = accs[0].astype(dtype)
    return ep


def _w_spec_nt(wg, kind, to, tc):
    _, a, b = wg.shape
    if kind == "col":
        oph, cps = a // to, b // tc
        return pl.BlockSpec((None, to, tc), lambda i, jo, kc: (2 * (kc // cps) + jo // oph, jo % oph, kc % cps))
    ops, cph = a // to, b // tc
    return pl.BlockSpec((None, to, tc), lambda i, jo, kc: (2 * (jo // ops) + kc // cph, jo % ops, kc % cph))


SUB_COLS = 256


def _matmul_wide(name, grid, operands, in_specs, groups, tn, epilogue, out_specs, out_shape, after=()):
    operands = list(operands) + list(after)
    in_specs = list(in_specs) + [ANY_SPEC] * len(after)
    n_in = len(operands)

    def body(*refs):
        ins, outs = refs[:n_in], refs[n_in:]
        for s0 in range(0, tn, SUB_COLS):
            cols = slice(s0, min(s0 + SUB_COLS, tn))
            accs = []
            for group in groups:
                acc = None
                for li, (c0, cw), ri, dims in group:
                    rhs = ins[ri][:, cols] if dims == NN else ins[ri][cols, :]
                    part = lax.dot_general(ins[li][:, c0:c0 + cw], rhs, dims, preferred_element_type=F32)
                    acc = part if acc is None else acc + part
                accs.append(acc)
            epilogue(accs, ins, outs, cols)

    return pl.pallas_call(
        body, name=name, grid=grid, in_specs=in_specs, out_specs=out_specs, out_shape=out_shape,
        compiler_params=_cp(("parallel", "parallel")),
    )(*operands)


def _wide_store(dtype):
    def ep(accs, ins, outs, cols):
        outs[0][:, cols] = accs[0].astype(dtype)
    return ep


def _wide_nn_weights(wg, kind, tn, first):
    _, a, b = wg.shape
    per = b // tn
    if kind == "col":
        specs = [pl.BlockSpec((None, a, tn), lambda i, j, hf=hf: (2 * (j // per) + hf, 0, j % per)) for hf in range(2)]
    else:
        specs = [pl.BlockSpec((None, a, tn), lambda i, j, ch=ch: (2 * ch + j // per, 0, j % per)) for ch in range(4)]
    dots = [(0, (q * a, a), first + q, NN) for q in range(len(specs))]
    return [wg] * len(specs), specs, dots


def _wide_nt_weights(wg, kind, to, first):
    _, a, b = wg.shape
    per = a // to
    if kind == "col":
        specs = [pl.BlockSpec((None, to, b), lambda i, j, ch=ch: (2 * ch + j // per, j % per, 0)) for ch in range(4)]
    else:
        specs = [pl.BlockSpec((None, to, b), lambda i, j, hf=hf: (2 * (j // per) + hf, j % per, 0)) for hf in range(2)]
    dots = [(0, (q * b, b), first + q, NT) for q in range(len(specs))]
    return [wg] * len(specs), specs, dots


def _mm_nn_wide(name, x, wg, kind, tm, tn, epilogue=None, extra=(), extra_specs=(), out_specs=None, out_shape=None,
                after=()):
    m, kdim = x.shape
    _, a, b = wg.shape
    n = 4 * b if kind == "col" else 2 * b
    tm = min(tm, m)
    ws, wspecs, dots = _wide_nn_weights(wg, kind, tn, 1)
    if out_shape is None:
        out_shape = [jax.ShapeDtypeStruct((m, n), F32)]
        out_specs = [pl.BlockSpec((tm, tn), lambda i, j: (i, j))]
        epilogue = _wide_store(F32)
    return _matmul_wide(name, (m // tm, n // tn), [x] + ws + list(extra),
                        [pl.BlockSpec((tm, kdim), lambda i, j: (i, 0))] + wspecs + list(extra_specs),
                        [dots], tn, epilogue, out_specs, out_shape, after=after)


def _mm_nt_wide(name, dy, wg, kind, tm, to, out_dtype=BF16, after=()):
    m, n = dy.shape
    _, a, b = wg.shape
    kdim = 2 * a if kind == "col" else 4 * a
    tm = min(tm, m)
    ws, wspecs, dots = _wide_nt_weights(wg, kind, to, 1)
    return _matmul_wide(name, (m // tm, kdim // to), [dy] + ws,
                        [pl.BlockSpec((tm, n), lambda i, j: (i, 0))] + wspecs, [dots], to, _wide_store(out_dtype),
                        [pl.BlockSpec((tm, to), lambda i, j: (i, j))],
                        [jax.ShapeDtypeStruct((m, kdim), out_dtype)], after=after)[0]


def _mm_tn_wide(name, a, dy, kind, tr, tn):
    m, kdim = a.shape
    _, n = dy.shape

    def body(a_ref, dy_ref, o_ref):
        for s0 in range(0, tn, SUB_COLS):
            cols = slice(s0, min(s0 + SUB_COLS, tn))
            o_ref[:, cols] = lax.dot_general(a_ref[...], dy_ref[:, cols], TN, preferred_element_type=F32).astype(BF16)

    if kind == "col":
        ns = n // 4
        per = ns // tn
        out_shape = jax.ShapeDtypeStruct((4, kdim, ns), BF16)
        out_spec = pl.BlockSpec((None, tr, tn), lambda r, j: (j // per, r, j % per))
    else:
        rs = kdim // 4
        per = rs // tr
        out_shape = jax.ShapeDtypeStruct((4, rs, n), BF16)
        out_spec = pl.BlockSpec((None, tr, tn), lambda r, j: (r // per, r % per, j))
    return pl.pallas_call(
        body, name=name, grid=(kdim // tr, n // tn),
        in_specs=[pl.BlockSpec((m, tr), lambda r, j: (0, r)), pl.BlockSpec((m, tn), lambda r, j: (0, j))],
        out_specs=out_spec, out_shape=out_shape, compiler_params=_cp(("parallel", "parallel")),
    )(a, dy)


def _mm_tn(name, a, dy, kind, tiles):
    m, kdim = a.shape
    _, n = dy.shape
    tr, tn, tmm = tiles
    tmm = min(tmm, m)
    grid = (kdim // tr, n // tn, m // tmm)
    in_specs = [pl.BlockSpec((tmm, tr), lambda r, j, k: (k, r)), pl.BlockSpec((tmm, tn), lambda r, j, k: (k, j))]
    if kind == "col":
        ns = n // 4
        bps = ns // tn
        out_shape = [jax.ShapeDtypeStruct((4, kdim, ns), BF16)]
        out_specs = [pl.BlockSpec((None, tr, tn), lambda r, j, k: (j // bps, r, j % bps))]
    else:
        rs = kdim // 4
        rps = rs // tr
        out_shape = [jax.ShapeDtypeStruct((4, rs, n), BF16)]
        out_specs = [pl.BlockSpec((None, tr, tn), lambda r, j, k: (r // rps, r % rps, j))]
    return _matmul(name, grid, [a, dy], in_specs, [(0, 1, 0, TN)], 1, (tr, tn), _ep_store(BF16),
                   out_specs, out_shape)[0]


def _zero_halo(pad_ref, s):
    z = jnp.zeros((HALO, pad_ref.shape[1]), F32)
    pad_ref[pl.ds(0, HALO), :] = z
    pad_ref[pl.ds(HALO + s, HALO), :] = z


def _window(pad_ref, r0, rows):
    return pad_ref[pl.ds(r0, rows + 2 * HALO), :]


def _delayed(win, k, rows):
    if k == 0:
        return win[HALO:HALO + rows]
    return pltpu.roll(win, k, axis=0)[HALO:HALO + rows]


def _advanced(win, k, rows):
    if k == 0:
        return win[HALO:HALO + rows]
    return pltpu.roll(win, win.shape[0] - k, axis=0)[HALO:HALO + rows]


def _fold8(x):
    return jnp.sum(x.reshape(x.shape[0] // 8, 8, x.shape[1]), axis=0)


def _chunks(s):
    rows = min(CHUNK_ROWS, s)
    return rows, s // rows


def _col_spec(s, first_block):
    return pl.BlockSpec((s, HEAD), lambda j: (0, first_block + j))


def _sgu_fwd(name, z, ln_g, ln_b, w, b):
    s = z.shape[0]
    tr = _row_tile(s, 1024)
    nh = GROUP // HEAD

    def body(u_ref, v_ref, lg_ref, lb_ref, w_ref, b_ref, o_ref):
        row = lax.broadcasted_iota(jnp.int32, (HEAD, HEAD), 0)
        col = lax.broadcasted_iota(jnp.int32, (HEAD, HEAD), 1)
        wm = jnp.where(row >= col, w_ref[...], 0.0).astype(BF16)
        for ck in range(tr // HEAD):
            rs = pl.ds(ck * HEAD, HEAD)
            u = _gelu(u_ref[rs, :])
            v = _gelu(v_ref[rs, :])
            mu = jnp.mean(v, axis=-1, keepdims=True)
            vc = v - mu
            var = jnp.mean(vc * vc, axis=-1, keepdims=True)
            vln = vc * lax.rsqrt(var + EPS) * lg_ref[...] + lb_ref[...]
            sp = jnp.dot(wm, vln.astype(BF16), preferred_element_type=F32) + b_ref[...]
            o_ref[rs, :] = (u * sp).astype(BF16)

    head_vec = pl.BlockSpec((None, 1, HEAD), lambda h, i: (h, 0, 0))
    return pl.pallas_call(
        body, name=name, grid=(nh, s // tr),
        in_specs=[pl.BlockSpec((tr, HEAD), lambda h, i: (i, h)), pl.BlockSpec((tr, HEAD), lambda h, i: (i, nh + h)),
                  head_vec, head_vec, pl.BlockSpec((None, HEAD, HEAD), lambda h, i: (h, 0, 0)),
                  pl.BlockSpec((None, HEAD, 1), lambda h, i: (h, 0, 0))],
        out_specs=pl.BlockSpec((tr, HEAD), lambda h, i: (i, h)),
        out_shape=jax.ShapeDtypeStruct((s, GROUP), BF16), compiler_params=_cp(("parallel", "parallel")),
    )(z, z, ln_g, ln_b, w, b)


def _sgu_bwd(name, z, d_o, ln_g, ln_b, w, b):
    s = z.shape[0]
    tr = _row_tile(s, 1024)
    nh = GROUP // HEAD

    def body(u_ref, v_ref, do_ref, lg_ref, lb_ref, w_ref, b_ref, du_ref, dv_ref, dlg_ref, dlb_ref, dw_ref, db_ref,
             dsp_acc):
        row = lax.broadcasted_iota(jnp.int32, (HEAD, HEAD), 0)
        col = lax.broadcasted_iota(jnp.int32, (HEAD, HEAD), 1)
        tril = row >= col
        wm = jnp.where(tril, w_ref[...], 0.0).astype(BF16)
        i = pl.program_id(1)

        @pl.when(i == 0)
        def _():
            dlg_ref[...] = jnp.zeros_like(dlg_ref)
            dlb_ref[...] = jnp.zeros_like(dlb_ref)
            dw_ref[...] = jnp.zeros_like(dw_ref)
            dsp_acc[...] = jnp.zeros_like(dsp_acc)

        dlg = jnp.zeros((1, HEAD), F32)
        dlb = jnp.zeros((1, HEAD), F32)
        dw = jnp.zeros((HEAD, HEAD), F32)
        dsp_sum = jnp.zeros((HEAD, HEAD), F32)
        for ck in range(tr // HEAD):
            rs = pl.ds(ck * HEAD, HEAD)
            zu = u_ref[rs, :]
            zv = v_ref[rs, :]
            u = _gelu(zu)
            v = _gelu(zv)
            mu = jnp.mean(v, axis=-1, keepdims=True)
            vc = v - mu
            var = jnp.mean(vc * vc, axis=-1, keepdims=True)
            rstd = lax.rsqrt(var + EPS)
            xh = vc * rstd
            vln = (xh * lg_ref[...] + lb_ref[...]).astype(BF16)
            sp = jnp.dot(wm, vln, preferred_element_type=F32) + b_ref[...]
            d_oa = do_ref[rs, :].astype(F32)
            du = d_oa * sp
            dsp = d_oa * u
            dsp_b = dsp.astype(BF16)
            dvln = lax.dot_general(wm, dsp_b, TN, preferred_element_type=F32)
            dw = dw + lax.dot_general(dsp_b, vln, NT, preferred_element_type=F32)
            dsp_sum = dsp_sum + dsp
            dlg = dlg + jnp.sum(dvln * xh, axis=0, keepdims=True)
            dlb = dlb + jnp.sum(dvln, axis=0, keepdims=True)
            dxh = dvln * lg_ref[...]
            dv = rstd * (dxh - jnp.mean(dxh, axis=-1, keepdims=True)
                         - xh * jnp.mean(dxh * xh, axis=-1, keepdims=True))
            du_ref[rs, :] = (du * _gelu_grad(zu)).astype(BF16)
            dv_ref[rs, :] = (dv * _gelu_grad(zv)).astype(BF16)
        dlg_ref[...] += dlg
        dlb_ref[...] += dlb
        dw_ref[...] += jnp.where(tril, dw, 0.0)
        dsp_acc[...] += dsp_sum

        @pl.when(i == pl.num_programs(1) - 1)
        def _():
            db_ref[...] = jnp.sum(dsp_acc[...], axis=1, keepdims=True)

    head_vec = pl.BlockSpec((None, 1, HEAD), lambda h, i: (h, 0, 0))
    head_mat = pl.BlockSpec((None, HEAD, HEAD), lambda h, i: (h, 0, 0))
    head_col = pl.BlockSpec((None, HEAD, 1), lambda h, i: (h, 0, 0))
    return pl.pallas_call(
        body, name=name, grid=(nh, s // tr),
        in_specs=[pl.BlockSpec((tr, HEAD), lambda h, i: (i, h)), pl.BlockSpec((tr, HEAD), lambda h, i: (i, nh + h)),
                  pl.BlockSpec((tr, HEAD), lambda h, i: (i, h)), head_vec, head_vec, head_mat, head_col],
        out_specs=[pl.BlockSpec((tr, HEAD), lambda h, i: (i, h)), pl.BlockSpec((tr, HEAD), lambda h, i: (i, h)),
                   head_vec, head_vec, head_mat, head_col],
        out_shape=[jax.ShapeDtypeStruct((s, GROUP), BF16), jax.ShapeDtypeStruct((s, GROUP), BF16),
                   jax.ShapeDtypeStruct((nh, 1, HEAD), F32), jax.ShapeDtypeStruct((nh, 1, HEAD), F32),
                   jax.ShapeDtypeStruct((nh, HEAD, HEAD), F32), jax.ShapeDtypeStruct((nh, HEAD, 1), F32)],
        scratch_shapes=[pltpu.VMEM((HEAD, HEAD), F32)],
        compiler_params=_cp(("parallel", "arbitrary")),
    )(z, z, d_o, ln_g, ln_b, w, b)


def _shortconv_fwd(name, z, w):
    s = z.shape[0]
    kw = w.shape[0]
    rows, nchunk = _chunks(s)
    nb = GROUP // HEAD

    def body(h_ref, bg_ref, cg_ref, w_ref, o_ref, pad):
        _zero_halo(pad, s)

        def fill(ci, carry):
            r0 = pl.multiple_of(ci * rows, rows)
            pad[pl.ds(pl.multiple_of(HALO + r0, 8), rows), :] = cg_ref[pl.ds(r0, rows), :] * h_ref[pl.ds(r0, rows), :]
            return carry

        lax.fori_loop(0, nchunk, fill, 0)

        def step(ci, carry):
            r0 = pl.multiple_of(ci * rows, rows)
            win = _window(pad, r0, rows)
            cv = jnp.zeros((rows, HEAD), F32)
            for k in range(kw):
                cv = cv + w_ref[k:k + 1, :] * _delayed(win, kw - 1 - k, rows)
            o_ref[pl.ds(r0, rows), :] = (bg_ref[pl.ds(r0, rows), :] * cv).astype(BF16)
            return carry

        lax.fori_loop(0, nchunk, step, 0)

    return pl.pallas_call(
        body, name=name, grid=(nb,),
        in_specs=[_col_spec(s, 8), _col_spec(s, 12), _col_spec(s, 16), pl.BlockSpec((kw, HEAD), lambda j: (0, j))],
        out_specs=_col_spec(s, 0),
        out_shape=jax.ShapeDtypeStruct((s, GROUP), BF16),
        scratch_shapes=[pltpu.VMEM((s + 2 * HALO, HEAD), F32)],
        compiler_params=_cp(("parallel",)),
    )(z, z, z, w)


def _shortconv_bwd(name, z, d_o, w):
    s = z.shape[0]
    kw = w.shape[0]
    rows, nchunk = _chunks(s)
    nb = GROUP // HEAD

    def body(h_ref, bg_ref, cg_ref, do_ref, w_ref, dh_ref, dbg_ref, dcg_ref, dw_ref, pad_q, pad_d, acc):
        _zero_halo(pad_q, s)
        _zero_halo(pad_d, s)
        acc[...] = jnp.zeros_like(acc)

        def fill(ci, carry):
            r0 = pl.multiple_of(ci * rows, rows)
            rs = pl.ds(r0, rows)
            ps = pl.ds(pl.multiple_of(HALO + r0, 8), rows)
            pad_q[ps, :] = cg_ref[rs, :] * h_ref[rs, :]
            pad_d[ps, :] = do_ref[rs, :].astype(F32) * bg_ref[rs, :]
            return carry

        lax.fori_loop(0, nchunk, fill, 0)

        def step(ci, carry):
            r0 = pl.multiple_of(ci * rows, rows)
            rs = pl.ds(r0, rows)
            wq = _window(pad_q, r0, rows)
            wd = _window(pad_d, r0, rows)
            dcv = wd[HALO:HALO + rows]
            cv = jnp.zeros((rows, HEAD), F32)
            dq = jnp.zeros((rows, HEAD), F32)
            for k in range(kw):
                qk = _delayed(wq, kw - 1 - k, rows)
                cv = cv + w_ref[k:k + 1, :] * qk
                dq = dq + w_ref[k:k + 1, :] * _advanced(wd, kw - 1 - k, rows)
                acc[k] += _fold8(dcv * qk)
            dbg_ref[rs, :] = (do_ref[rs, :].astype(F32) * cv).astype(BF16)
            dcg_ref[rs, :] = (dq * h_ref[rs, :]).astype(BF16)
            dh_ref[rs, :] = (dq * cg_ref[rs, :]).astype(BF16)
            return carry

        lax.fori_loop(0, nchunk, step, 0)
        for k in range(kw):
            dw_ref[k:k + 1, :] = jnp.sum(acc[k], axis=0, keepdims=True)

    col = _col_spec(s, 0)
    return pl.pallas_call(
        body, name=name, grid=(nb,),
        in_specs=[_col_spec(s, 8), _col_spec(s, 12), _col_spec(s, 16), col, pl.BlockSpec((kw, HEAD), lambda j: (0, j))],
        out_specs=[col, col, col, pl.BlockSpec((kw, HEAD), lambda j: (0, j))],
        out_shape=[jax.ShapeDtypeStruct((s, GROUP), BF16)] * 3 + [jax.ShapeDtypeStruct((kw, GROUP), F32)],
        scratch_shapes=[pltpu.VMEM((s + 2 * HALO, HEAD), F32), pltpu.VMEM((s + 2 * HALO, HEAD), F32),
                        pltpu.VMEM((kw, 8, HEAD), F32)],
        compiler_params=_cp(("parallel",)),
    )(z, z, z, d_o, w)


def _conformer_conv_fwd(name, z, w, bias):
    s = z.shape[0]
    kw = w.shape[0]
    rows, nchunk = _chunks(s)
    nb = GROUP // HEAD

    def body(a_ref, g_ref, w_ref, b_ref, o_ref, pad):
        _zero_halo(pad, s)

        def fill(ci, carry):
            r0 = pl.multiple_of(ci * rows, rows)
            rs = pl.ds(r0, rows)
            pad[pl.ds(pl.multiple_of(HALO + r0, 8), rows), :] = a_ref[rs, :] * _sigmoid(g_ref[rs, :])
            return carry

        lax.fori_loop(0, nchunk, fill, 0)

        def step(ci, carry):
            r0 = pl.multiple_of(ci * rows, rows)
            win = _window(pad, r0, rows)
            cc = jnp.zeros((rows, HEAD), F32)
            for k in range(kw):
                cc = cc + w_ref[k:k + 1, :] * _delayed(win, kw - 1 - k, rows)
            o_ref[pl.ds(r0, rows), :] = cc + b_ref[...]
            return carry

        lax.fori_loop(0, nchunk, step, 0)

    return pl.pallas_call(
        body, name=name, grid=(nb,),
        in_specs=[_col_spec(s, 20), _col_spec(s, 24), pl.BlockSpec((kw, HEAD), lambda j: (0, j)),
                  pl.BlockSpec((1, HEAD), lambda j: (0, j))],
        out_specs=_col_spec(s, 0),
        out_shape=jax.ShapeDtypeStruct((s, GROUP), F32),
        scratch_shapes=[pltpu.VMEM((s + 2 * HALO, HEAD), F32)],
        compiler_params=_cp(("parallel",)),
    )(z, z, w, bias)


def _ln_rows(cc, g, b):
    mu = jnp.mean(cc, axis=-1, keepdims=True)
    xc = cc - mu
    var = jnp.mean(xc * xc, axis=-1, keepdims=True)
    rstd = lax.rsqrt(var + EPS)
    xh = xc * rstd
    return xh, rstd, xh * g + b


def _conformer_ln_fwd(name, cc, g, b):
    s, d = cc.shape
    tr = _row_tile(s, 512)

    def body(c_ref, g_ref, b_ref, o_ref):
        _, _, l = _ln_rows(c_ref[...], g_ref[...], b_ref[...])
        o_ref[...] = (l * _sigmoid(l)).astype(BF16)

    row = pl.BlockSpec((tr, d), lambda i: (i, 0))
    vec = pl.BlockSpec((1, d), lambda i: (0, 0))
    return pl.pallas_call(
        body, name=name, grid=(s // tr,), in_specs=[row, vec, vec], out_specs=row,
        out_shape=jax.ShapeDtypeStruct((s, d), BF16), compiler_params=_cp(("parallel",)),
    )(cc, g, b)


def _conformer_ln_bwd(name, cc, d_o, g, b):
    s, d = cc.shape
    tr = _row_tile(s, 512)

    def body(c_ref, do_ref, g_ref, b_ref, dcc_ref, dg_ref, db_ref, dcb_ref):
        xh, rstd, l = _ln_rows(c_ref[...], g_ref[...], b_ref[...])
        sg = _sigmoid(l)
        dl = do_ref[...].astype(F32) * sg * (1.0 + l * (1.0 - sg))
        dxh = dl * g_ref[...]
        dcc = rstd * (dxh - jnp.mean(dxh, axis=-1, keepdims=True) - xh * jnp.mean(dxh * xh, axis=-1, keepdims=True))
        dcc_ref[...] = dcc

        @pl.when(pl.program_id(0) == 0)
        def _():
            dg_ref[...] = jnp.zeros_like(dg_ref)
            db_ref[...] = jnp.zeros_like(db_ref)
            dcb_ref[...] = jnp.zeros_like(dcb_ref)

        dg_ref[...] += jnp.sum(dl * xh, axis=0, keepdims=True)
        db_ref[...] += jnp.sum(dl, axis=0, keepdims=True)
        dcb_ref[...] += jnp.sum(dcc, axis=0, keepdims=True)

    row = pl.BlockSpec((tr, d), lambda i: (i, 0))
    vec = pl.BlockSpec((1, d), lambda i: (0, 0))
    return pl.pallas_call(
        body, name=name, grid=(s // tr,), in_specs=[row, row, vec, vec], out_specs=[row, vec, vec, vec],
        out_shape=[jax.ShapeDtypeStruct((s, d), F32)] + [jax.ShapeDtypeStruct((1, d), F32)] * 3,
        compiler_params=_cp(("arbitrary",)),
    )(cc, d_o, g, b)


def _conformer_conv_bwd(name, z, dcc, w):
    s = z.shape[0]
    kw = w.shape[0]
    rows, nchunk = _chunks(s)
    nb = GROUP // HEAD

    def body(a_ref, g_ref, d_ref, w_ref, da_ref, dg_ref, dw_ref, pad_h, pad_d, acc):
        _zero_halo(pad_h, s)
        _zero_halo(pad_d, s)
        acc[...] = jnp.zeros_like(acc)

        def fill(ci, carry):
            r0 = pl.multiple_of(ci * rows, rows)
            rs = pl.ds(r0, rows)
            ps = pl.ds(pl.multiple_of(HALO + r0, 8), rows)
            pad_h[ps, :] = a_ref[rs, :] * _sigmoid(g_ref[rs, :])
            pad_d[ps, :] = d_ref[rs, :]
            return carry

        lax.fori_loop(0, nchunk, fill, 0)

        def step(ci, carry):
            r0 = pl.multiple_of(ci * rows, rows)
            rs = pl.ds(r0, rows)
            wh = _window(pad_h, r0, rows)
            wd = _window(pad_d, r0, rows)
            dcc_c = wd[HALO:HALO + rows]
            dhc = jnp.zeros((rows, HEAD), F32)
            for k in range(kw):
                dhc = dhc + w_ref[k:k + 1, :] * _advanced(wd, kw - 1 - k, rows)
                acc[k] += _fold8(dcc_c * _delayed(wh, kw - 1 - k, rows))
            sg = _sigmoid(g_ref[rs, :])
            da_ref[rs, :] = (dhc * sg).astype(BF16)
            dg_ref[rs, :] = (dhc * a_ref[rs, :] * sg * (1.0 - sg)).astype(BF16)
            return carry

        lax.fori_loop(0, nchunk, step, 0)
        for k in range(kw):
            dw_ref[k:k + 1, :] = jnp.sum(acc[k], axis=0, keepdims=True)

    col = _col_spec(s, 0)
    return pl.pallas_call(
        body, name=name, grid=(nb,),
        in_specs=[_col_spec(s, 20), _col_spec(s, 24), col, pl.BlockSpec((kw, HEAD), lambda j: (0, j))],
        out_specs=[col, col, pl.BlockSpec((kw, HEAD), lambda j: (0, j))],
        out_shape=[jax.ShapeDtypeStruct((s, GROUP), BF16)] * 2 + [jax.ShapeDtypeStruct((kw, GROUP), F32)],
        scratch_shapes=[pltpu.VMEM((s + 2 * HALO, HEAD), F32), pltpu.VMEM((s + 2 * HALO, HEAD), F32),
                        pltpu.VMEM((kw, 8, HEAD), F32)],
        compiler_params=_cp(("parallel",)),
    )(z, z, dcc, w)


def _pool_window_sum(win, level, rows, shift):
    n = win.shape[0]

    def moved(v, k):
        return pltpu.roll(v, k if shift is _delayed else n - k, axis=0)

    s2 = win + moved(win, 1)
    s4 = s2 + moved(s2, 2)
    s8 = s4 + moved(s4, 4)
    s16 = s8 + moved(s8, 8)
    sel = jnp.where(level == 0, s2, jnp.where(level == 1, s4, jnp.where(level == 2, s8, s16)))
    return sel[HALO:HALO + rows]


def _pool_count(level, r0, rows):
    t = r0 + lax.broadcasted_iota(jnp.int32, (rows, 1), 0)
    width = jnp.left_shift(jnp.int32(2), level)
    return jnp.minimum(t + 1, width).astype(F32)


def _pool_fwd(name, z, pool_w, scale):
    s = z.shape[0]
    rows, nchunk = _chunks(s)

    def body(z_ref, w_ref, sc_ref, o_ref, pad):
        level = pl.program_id(0)
        _zero_halo(pad, s)

        def fill(ci, carry):
            r0 = pl.multiple_of(ci * rows, rows)
            pad[pl.ds(pl.multiple_of(HALO + r0, 8), rows), :] = z_ref[pl.ds(r0, rows), :]
            return carry

        lax.fori_loop(0, nchunk, fill, 0)
        wb = w_ref[...].astype(BF16)

        def step(ci, carry):
            r0 = pl.multiple_of(ci * rows, rows)
            win = _window(pad, r0, rows)
            pm = _pool_window_sum(win, level, rows, _delayed) / _pool_count(level, r0, rows) - win[HALO:HALO + rows]
            r = jnp.dot(pm.astype(BF16), wb, preferred_element_type=F32)
            o_ref[pl.ds(r0, rows), :] = (r * sc_ref[...]).astype(BF16)
            return carry

        lax.fori_loop(0, nchunk, step, 0)

    return pl.pallas_call(
        body, name=name, grid=(POOL_LEVELS,),
        in_specs=[_col_spec(s, 28), pl.BlockSpec((None, HEAD, HEAD), lambda j: (j, 0, 0)),
                  pl.BlockSpec((1, HEAD), lambda j: (0, j))],
        out_specs=_col_spec(s, 0),
        out_shape=jax.ShapeDtypeStruct((s, GROUP), BF16),
        scratch_shapes=[pltpu.VMEM((s + 2 * HALO, HEAD), F32)],
        compiler_params=_cp(("parallel",)),
    )(z, pool_w, scale)


def _pool_bwd(name, z, d_o, pool_w, scale):
    s = z.shape[0]
    rows, nchunk = _chunks(s)

    def body(z_ref, do_ref, w_ref, sc_ref, dz_ref, dw_ref, dsc_ref, pad, pad_q, dw_acc, dsc_acc):
        level = pl.program_id(0)
        _zero_halo(pad, s)
        _zero_halo(pad_q, s)
        dw_acc[...] = jnp.zeros_like(dw_acc)
        dsc_acc[...] = jnp.zeros_like(dsc_acc)

        def fill(ci, carry):
            r0 = pl.multiple_of(ci * rows, rows)
            pad[pl.ds(pl.multiple_of(HALO + r0, 8), rows), :] = z_ref[pl.ds(r0, rows), :]
            return carry

        lax.fori_loop(0, nchunk, fill, 0)
        wb = w_ref[...].astype(BF16)

        def first(ci, carry):
            r0 = pl.multiple_of(ci * rows, rows)
            win = _window(pad, r0, rows)
            cnt = _pool_count(level, r0, rows)
            pm = (_pool_window_sum(win, level, rows, _delayed) / cnt - win[HALO:HALO + rows]).astype(BF16)
            r = jnp.dot(pm, wb, preferred_element_type=F32)
            d_od = do_ref[pl.ds(r0, rows), :].astype(F32)
            dsc_acc[...] += _fold8(d_od * r)
            dr = (d_od * sc_ref[...]).astype(BF16)
            dw_acc[...] += lax.dot_general(pm, dr, TN, preferred_element_type=F32)
            dpm = lax.dot_general(dr, wb, NT, preferred_element_type=F32)
            pad_q[pl.ds(pl.multiple_of(HALO + r0, 8), rows), :] = dpm / cnt
            return carry

        lax.fori_loop(0, nchunk, first, 0)

        def second(ci, carry):
            r0 = pl.multiple_of(ci * rows, rows)
            wq = _window(pad_q, r0, rows)
            dpm = wq[HALO:HALO + rows] * _pool_count(level, r0, rows)
            dz_ref[pl.ds(r0, rows), :] = (_pool_window_sum(wq, level, rows, _advanced) - dpm).astype(BF16)
            return carry

        lax.fori_loop(0, nchunk, second, 0)
        dw_ref[...] = dw_acc[...]
        dsc_ref[...] = jnp.sum(dsc_acc[...], axis=0, keepdims=True)

    col = _col_spec(s, 0)
    mat = pl.BlockSpec((None, HEAD, HEAD), lambda j: (j, 0, 0))
    vec = pl.BlockSpec((1, HEAD), lambda j: (0, j))
    return pl.pallas_call(
        body, name=name, grid=(POOL_LEVELS,),
        in_specs=[_col_spec(s, 28), col, mat, vec], out_specs=[col, mat, vec],
        out_shape=[jax.ShapeDtypeStruct((s, GROUP), BF16), jax.ShapeDtypeStruct((POOL_LEVELS, HEAD, HEAD), F32),
                   jax.ShapeDtypeStruct((1, GROUP), F32)],
        scratch_shapes=[pltpu.VMEM((s + 2 * HALO, HEAD), F32), pltpu.VMEM((s + 2 * HALO, HEAD), F32),
                        pltpu.VMEM((HEAD, HEAD), F32), pltpu.VMEM((8, HEAD), F32)],
        compiler_params=_cp(("parallel",)),
    )(z, d_o, pool_w, scale)


def _mm_residual(name, x, wg, h, tm, tn):
    m, d = h.shape
    tm = min(tm, m)

    def ep(accs, ins, outs, cols):
        outs[0][:, cols] = ins[5][:, cols] + accs[0]

    tile = pl.BlockSpec((tm, tn), lambda i, j: (i, j))
    return _mm_nn_wide(name, x, wg, "row", tm, tn, epilogue=ep, extra=[h], extra_specs=[tile], out_specs=[tile],
                       out_shape=[jax.ShapeDtypeStruct((m, d), F32)])[0]


def _swiglu_fwd(name, y, wg_gate, wg_up, tm):
    m, kdim = y.shape
    _, a, b = wg_gate.shape
    tm = min(tm, m)

    def ep(accs, ins, outs, cols):
        gt, up = accs
        outs[0][:, cols] = (gt * _sigmoid(gt) * up).astype(BF16)
        outs[1][:, cols] = gt.astype(BF16)
        outs[2][:, cols] = up.astype(BF16)

    gws, gspecs, gdots = _wide_nn_weights(wg_gate, "col", b, 1)
    uws, uspecs, udots = _wide_nn_weights(wg_up, "col", b, 1 + len(gws))
    out = pl.BlockSpec((tm, b), lambda i, j: (i, j))
    return _matmul_wide(name, (m // tm, 4), [y] + gws + uws,
                        [pl.BlockSpec((tm, kdim), lambda i, j: (i, 0))] + gspecs + uspecs, [gdots, udots], b, ep,
                        [out] * 3, [jax.ShapeDtypeStruct((m, 4 * b), BF16)] * 3)


def _swiglu_bwd(name, dh, wg_down, gate, up, tm):
    m, n = dh.shape
    _, a, b = wg_down.shape
    tm = min(tm, m)

    def ep(accs, ins, outs, cols):
        d_act = accs[0]
        gt = ins[3][:, cols].astype(F32)
        upv = ins[4][:, cols].astype(F32)
        sg = _sigmoid(gt)
        outs[0][:, cols] = (d_act * upv * sg * (1.0 + gt * (1.0 - sg))).astype(BF16)
        outs[1][:, cols] = (d_act * gt * sg).astype(BF16)
        outs[2][:, cols] = (gt * sg * upv).astype(BF16)

    ws, wspecs, dots = _wide_nt_weights(wg_down, "row", a, 1)
    tile = pl.BlockSpec((tm, a), lambda i, j: (i, j))
    return _matmul_wide(name, (m // tm, 4), [dh] + ws + [gate, up],
                        [pl.BlockSpec((tm, n), lambda i, j: (i, 0))] + wspecs + [tile, tile], [dots], a, ep,
                        [tile] * 3, [jax.ShapeDtypeStruct((m, 4 * a), BF16)] * 3)


def _ffn_dy(name, d_gate, d_up, wg_gate, wg_up, tiles, after=()):
    m, n = d_gate.shape
    _, a, b = wg_gate.shape
    kdim = 2 * a
    tm, to, tc = tiles
    tm = min(tm, m)
    grid = (m // tm, kdim // to, n // tc)
    lhs = pl.BlockSpec((tm, tc), lambda i, j, k: (i, k))
    wspec = _w_spec_nt(wg_gate, "col", to, tc)
    return _matmul(name, grid, [d_gate, d_up, wg_gate, wg_up], [lhs, lhs, wspec, wspec],
                   [(0, 2, 0, NT), (1, 3, 0, NT)], 1, (tm, to), _ep_store(BF16),
                   [pl.BlockSpec((tm, to), lambda i, j, k: (i, j))], [jax.ShapeDtypeStruct((m, kdim), BF16)],
                   after=after)[0]


def _ple_fwd(name, y, wg, h, pp, tm, tn, after=()):
    m, d = h.shape
    tm = min(tm, m)

    def ep(accs, ins, outs, cols):
        pg = accs[0]
        outs[0][:, cols] = ins[5][:, cols] + _sigmoid(pg) * ins[6][:, cols].astype(F32)
        outs[1][:, cols] = pg.astype(BF16)

    tile = pl.BlockSpec((tm, tn), lambda i, j: (i, j))
    return _mm_nn_wide(name, y, wg, "row", tm, tn, epilogue=ep, extra=[h, pp], extra_specs=[tile, tile],
                       out_specs=[tile, tile],
                       out_shape=[jax.ShapeDtypeStruct((m, d), F32), jax.ShapeDtypeStruct((m, d), BF16)], after=after)


def _ple_bwd(name, dh, pg, pp, after=()):
    s, d = dh.shape
    tr = _row_tile(s, 512)

    def body(dh_ref, pg_ref, pp_ref, *rest):
        dpp_ref, dpg_ref = rest[-2:]
        dhv = dh_ref[...]
        sg = _sigmoid(pg_ref[...].astype(F32))
        dpp_ref[...] = (dhv * sg).astype(BF16)
        dpg_ref[...] = (dhv * pp_ref[...].astype(F32) * sg * (1.0 - sg)).astype(BF16)

    row = pl.BlockSpec((tr, d), lambda i: (i, 0))
    return pl.pallas_call(
        body, name=name, grid=(s // tr,), in_specs=[row] * 3 + [ANY_SPEC] * len(after), out_specs=[row] * 2,
        out_shape=[jax.ShapeDtypeStruct((s, d), BF16)] * 2, compiler_params=_cp(("parallel",)),
    )(dh, pg, pp, *after)


BIG = ["w_in", "w_out", "w_gate", "w_up", "w_down", "w_ple_gate", "w_ple_proj"]
KIND = {"w_in": "col", "w_out": "row", "w_gate": "col", "w_up": "col", "w_down": "row", "w_ple_gate": "row",
        "w_ple_proj": "col"}
GATHER_GROUPS = (("w_in", "w_out"), ("w_gate", "w_up"), ("w_down", "w_ple_gate", "w_ple_proj"))
RS_GROUPS = (("w_ple_gate", "w_ple_proj", "w_down", "w_gate", "w_up"), ("w_out", "w_in"))
SMALL = ["norm_mix_g", "sgu_ln_g", "sgu_ln_b", "sgu_w", "sgu_b", "sc_conv_w", "cf_conv_w", "cf_conv_b", "cf_ln_g",
         "cf_ln_b", "pool_w", "pool_scale", "norm_ffn_g", "norm_ple_g", "final_norm_g"]
CHIP_SPLIT = ["sc_conv_w", "cf_conv_w"]
WEIGHTS = ['norm_mix_g', 'w_in', 'sgu_ln_g', 'sgu_ln_b', 'sgu_w', 'sgu_b', 'sc_conv_w', 'cf_conv_w', 'cf_conv_b',
           'cf_ln_g', 'cf_ln_b', 'pool_w', 'pool_scale', 'w_out', 'norm_ffn_g', 'w_gate', 'w_up', 'w_down',
           'norm_ple_g', 'w_ple_gate', 'w_ple_proj', 'final_norm_g']


def _tile(n, want):
    if n <= want:
        return n
    t = (want // 128) * 128
    while n % t:
        t -= 128
    return t


def _pack_rows(vecs):
    flat = jnp.concatenate([v.reshape(-1) for v in vecs])
    n = flat.shape[0]
    quantum = PACK_ROWS * 128
    padded = ((n + quantum - 1) // quantum) * quantum
    return jnp.pad(flat, (0, padded - n)).reshape(padded // 128, 128), n


def _unpack(flat, shapes):
    out, off = [], 0
    for shp in shapes:
        size = math.prod(shp)
        out.append(flat[off:off + size].reshape(shp))
        off += size
    return out


def kernel(x, p, norm_mix_g, w_in, sgu_ln_g, sgu_ln_b, sgu_w, sgu_b, sc_conv_w, cf_conv_w, cf_conv_b, cf_ln_g, cf_ln_b, pool_w, pool_scale, w_out, norm_ffn_g, w_gate, w_up, w_down, norm_ple_g, w_ple_gate, w_ple_proj, final_norm_g, loss_target, m_norm_mix_g, m_w_in, m_sgu_ln_g, m_sgu_ln_b, m_sgu_w, m_sgu_b, m_sc_conv_w, m_cf_conv_w, m_cf_conv_b, m_cf_ln_g, m_cf_ln_b, m_pool_w, m_pool_scale, m_w_out, m_norm_ffn_g, m_w_gate, m_w_up, m_w_down, m_norm_ple_g, m_w_ple_gate, m_w_ple_proj, m_final_norm_g, v_norm_mix_g, v_w_in, v_sgu_ln_g, v_sgu_ln_b, v_sgu_w, v_sgu_b, v_sc_conv_w, v_cf_conv_w, v_cf_conv_b, v_cf_ln_g, v_cf_ln_b, v_pool_w, v_pool_scale, v_w_out, v_norm_ffn_g, v_w_gate, v_w_up, v_w_down, v_norm_ple_g, v_w_ple_gate, v_w_ple_proj, v_final_norm_g):
    args = dict(locals())
    w = {n: args[n] for n in WEIGHTS}
    mom = {n: args["m_" + n] for n in WEIGHTS}
    var = {n: args["v_" + n] for n in WEIGHTS}
    depth = w_in.shape[0]
    s, d = x.shape[1], x.shape[2]
    f_dim = 4 * w_gate.shape[2]
    xi, yi, ci = lax.axis_index("x"), lax.axis_index("y"), lax.axis_index("c")
    c_idx = ci.astype(jnp.int32).reshape(1)

    chip_idx = (2 * xi + yi).astype(jnp.int32).reshape(1)

    def start_gathers(l, dep):
        pending = []
        for gi, names in enumerate(GATHER_GROUPS):
            lands = [_cast_into_landing(f"cast_{n}", w[n], l, 0 if KIND[n] == "col" else 1, chip_idx) for n in names]
            pending.append(_gather_start(f"gather_start_{l}_{gi}", lands, dep))
            dep = pending[-1][3]
        return pending, dep

    def receive(l, gi, pending, after):
        send_sems, recv_sems, lands, _ = pending
        lands = _gather_wait(f"gather_wait_{l}_{gi}", send_sems, recv_sems, lands, after)
        return _share_start(f"gather_share_start_{l}_{gi}", lands)

    def complete(l, gi, share, after):
        send_sems, recv_sems, lands, _ = share
        lands = _share_wait(f"gather_share_wait_{l}_{gi}", send_sems, recv_sems, lands, after)
        return dict(zip(GATHER_GROUPS[gi], lands))

    conv_pack = jnp.concatenate([sc_conv_w, cf_conv_w], axis=1)
    taps = conv_pack.shape[1]
    rows_pad = ((depth * taps + 7) // 8) * 8
    conv_rows = jnp.pad(conv_pack.reshape(depth * taps, HEAD), ((0, rows_pad - depth * taps), (0, 0)))
    conv_all = _allgather8("gather_conv_weights", [conv_rows])[0]
    conv_full = conv_all[0::2, :depth * taps].reshape(4, depth, taps, HEAD)
    conv_full = jnp.transpose(conv_full, (1, 2, 0, 3)).reshape(depth, taps, GROUP)
    sc_w_full, cf_w_full = conv_full[:, :3], conv_full[:, 3:]

    pending, token = start_gathers(0, conv_all)
    h = x[0]
    saved = []
    gathered = []
    shares = [None] * len(GATHER_GROUPS)
    for l in range(depth):
        just_in_time = l < 2
        ahead = l + 1 < depth and l + 1 >= 2
        if just_in_time:
            shares[0] = receive(l, 0, pending[0], token if l == 0 else h)
        wg = complete(l, 0, shares[0], h)
        gathered.append(wg)
        started = ()
        this_layer = pending
        if l + 1 < depth:
            pending, token = start_gathers(l + 1, wg["w_in"])
            started = (token,)
        sv = {"h0": h}
        y1 = _rms_fwd("rms_mix", h, norm_mix_g[l:l + 1], after=started)
        z = _mm_nn_wide("mm_in", y1, wg["w_in"], "col", 1024, 1024)[0]
        lg, lb = sgu_ln_g[l][:, None, :], sgu_ln_b[l][:, None, :]
        sb = sgu_b[l][:, :, None]
        oa = _sgu_fwd("sgu_fwd", z, lg, lb, sgu_w[l], sb)
        ob = _shortconv_fwd("shortconv_fwd", z, sc_w_full[l])
        cc = _conformer_conv_fwd("conformer_conv_fwd", z, cf_w_full[l], cf_conv_b[l:l + 1])
        oc = _conformer_ln_fwd("conformer_ln_fwd", cc, cf_ln_g[l:l + 1], cf_ln_b[l:l + 1])
        od = _pool_fwd("pool_fwd", z, pool_w[l], pool_scale[l:l + 1])
        o = jnp.concatenate([oa, ob, oc, od], axis=1)
        h1 = _mm_residual("mm_out", o, wg["w_out"], h, 1024, 1024)
        if just_in_time:
            shares[1] = receive(l, 1, this_layer[1], h1)
        wg.update(complete(l, 1, shares[1], h1))
        y2 = _rms_fwd("rms_ffn", h1, norm_ffn_g[l:l + 1])
        act, gt, up = _swiglu_fwd("mm_swiglu", y2, wg["w_gate"], wg["w_up"], 512)
        if just_in_time:
            shares[2] = receive(l, 2, this_layer[2], act)
        wg.update(complete(l, 2, shares[2], act))
        h2 = _mm_residual("mm_down", act, wg["w_down"], h1, 1024, 512)
        started = ()
        if ahead:
            shares[0] = receive(l + 1, 0, pending[0], h2)
            shares[1] = receive(l + 1, 1, pending[1], shares[0][3])
            started = (shares[1][3],)
        y3 = _rms_fwd("rms_ple", h2, norm_ple_g[l:l + 1], after=started)
        pb = p[l, 0].astype(BF16)
        ptile = pl.BlockSpec((min(1024, s), 512), lambda i, j: (i, j))
        pp = _mm_nn_wide("mm_ple_proj", pb, wg["w_ple_proj"], "col", 1024, 512, epilogue=_wide_store(BF16),
                         out_specs=[ptile], out_shape=[jax.ShapeDtypeStruct((s, d), BF16)])[0]
        started = ()
        if ahead:
            shares[2] = receive(l + 1, 2, pending[2], pp)
            started = (shares[2][3],)
        h3, pg = _ple_fwd("mm_ple_gate", y3, wg["w_ple_gate"], h2, pp, 1024, 1024, after=started)
        sv.update(y1=y1, z=z, cc=cc, o=o, h1=h1, y2=y2, gt=gt, up=up, h2=h2, y3=y3, pb=pb, pp=pp, pg=pg)
        saved.append(sv)
        h = h3

    loss_part, dh, dhb, d_final_g = _loss_head("loss_head", h, final_norm_g[None, :], loss_target[0])

    small_grads = [None] * depth
    where = jnp.stack([2 * xi + yi, 2 * (1 - xi) + yi, 2 * xi + (1 - yi), 2 * (1 - xi) + (1 - yi), ci]).astype(jnp.int32)
    grad_bufs = {n: lax.empty((depth, 2, w[n].shape[1] // 2, w[n].shape[2]), F32) for n in BIG}
    exchanges = [None] * len(RS_GROUPS)
    joins = []
    behind_join = ()

    def halves(g):
        return g.reshape(4, 2, g.shape[1] // 2, g.shape[2])

    def start_exchange(layer, gi, sibling, after):
        send_sems, recv_sems, gs, lands, _ = sibling
        gs, lands = _sibling_wait(f"rs_sibling_wait_{layer}_{gi}", send_sems, recv_sems, gs, lands, after)
        chip_sums = _add_halves("rs_add", gs, lands, c_idx)
        return _chip_exchange_start(f"rs_chips_start_{layer}_{gi}", chip_sums)

    def finish_exchange(layer, gi, after):
        send_sems, recv_sems, parts, lands, _ = exchanges[gi]
        parts, lands = _chip_exchange_wait(f"rs_chips_wait_{layer}_{gi}", send_sems, recv_sems, parts, lands, after)
        names = RS_GROUPS[gi]
        sums = _sum_parts("rs_sum", parts, lands, [grad_bufs[n] for n in names], layer, where)
        grad_bufs.update(zip(names, sums))

    for l in reversed(range(depth)):
        wg = gathered[l]
        sv = saved[l]
        fs = f_dim // 4
        started = () if exchanges[1] is None else (exchanges[1][4],)
        d_pp, d_pg = _ple_bwd("ple_bwd", dh, sv["pg"], sv["pp"], after=started + behind_join)
        g_ple_proj = _mm_tn_wide("dw_ple_proj", sv["pb"], d_pp, "col", w_ple_proj.shape[1], 512)
        g_ple_gate = _mm_tn_wide("dw_ple_gate", sv["y3"], d_pg, "row", 512, 1024)
        dy3 = _mm_nt_wide("dx_ple_gate", d_pg, wg["w_ple_gate"], "row", 1024, 512)
        dh, dhb, dg_ple = _rms_bwd("rms_ple_bwd", sv["h2"], norm_ple_g[l:l + 1], dy3, dh)

        d_gt, d_up, act = _swiglu_bwd("dx_down_swiglu", dhb, wg["w_down"], sv["gt"], sv["up"], 512)
        g_down = _mm_tn("dw_down", act, dhb, "row", (fs, 1024, 1024))
        g_gate = _mm_tn("dw_gate", sv["y2"], d_gt, "col", (1024, fs, 1024))
        g_up = _mm_tn("dw_up", sv["y2"], d_up, "col", (1024, fs, 1024))
        big = dict(w_gate=g_gate, w_up=g_up, w_down=g_down, w_ple_gate=g_ple_gate, w_ple_proj=g_ple_proj)
        sibling = _sibling_start(f"rs_sibling_start_{l}_0", [halves(big[n]) for n in RS_GROUPS[0]])
        dy2 = _ffn_dy("dx_gate_up", d_gt, d_up, wg["w_gate"], wg["w_up"], (1024, 1024, fs), after=(sibling[4],))
        dh, dhb, dg_ffn = _rms_bwd("rms_ffn_bwd", sv["h1"], norm_ffn_g[l:l + 1], dy2, dh)

        g_out = _mm_tn_wide("dw_out", sv["o"], dhb, "row", 512, 1024)
        if exchanges[0] is not None:
            finish_exchange(l + 1, 0, g_out)
        exchanges[0] = start_exchange(l, 0, sibling, g_out)
        d_o = _mm_nt_wide("dx_out", dhb, wg["w_out"], "row", 1024, 512, after=(exchanges[0][4],))
        z = sv["z"]
        lg, lb = sgu_ln_g[l][:, None, :], sgu_ln_b[l][:, None, :]
        sb = sgu_b[l][:, :, None]
        dzu, dzv, d_lg, d_lb, d_sw, d_sb = _sgu_bwd("sgu_bwd", z, d_o[:, 0:GROUP], lg, lb, sgu_w[l], sb)
        dzh, dzbg, dzcg, d_scw = _shortconv_bwd("shortconv_bwd", z, d_o[:, GROUP:2 * GROUP], sc_w_full[l])
        dcc, d_cflg, d_cflb, d_cfb = _conformer_ln_bwd("conformer_ln_bwd", sv["cc"], d_o[:, 2 * GROUP:3 * GROUP],
                                                       cf_ln_g[l:l + 1], cf_ln_b[l:l + 1])
        dza, dzg, d_cfw = _conformer_conv_bwd("conformer_conv_bwd", z, dcc, cf_w_full[l])
        dzd, d_pw, d_psc = _pool_bwd("pool_bwd", z, d_o[:, 3 * GROUP:], pool_w[l], pool_scale[l:l + 1])
        dz = jnp.concatenate([dzu, dzv, dzh, dzbg, dzcg, dza, dzg, dzd], axis=1)
        g_in = _mm_tn("dw_in", sv["y1"], dz, "col", (1024, 1024, 1024))
        big.update(w_out=g_out, w_in=g_in)
        sibling = _sibling_start(f"rs_sibling_start_{l}_1", [halves(big[n]) for n in RS_GROUPS[1]])
        dy1 = _mm_nt_wide("dx_in", dz, wg["w_in"], "col", 1024, 512, after=(sibling[4],))
        dh, dhb, dg_mix = _rms_bwd("rms_mix_bwd", sv["h0"], norm_mix_g[l:l + 1], dy1, dh)
        if exchanges[1] is not None:
            finish_exchange(l + 1, 1, dh)
            send_sems, recv_sems, bufs, join_token = _join_start(f"rs_join_start_{l + 1}",
                                                                 [grad_bufs[n] for n in BIG], l + 1)
            grad_bufs.update(zip(BIG, bufs))
            joins.append((l + 1, send_sems, recv_sems))
            behind_join = (join_token,)
        exchanges[1] = start_exchange(l, 1, sibling, dh)

        small_grads[l] = dict(norm_mix_g=dg_mix, sgu_ln_g=d_lg, sgu_ln_b=d_lb, sgu_w=d_sw, sgu_b=d_sb,
                              sc_conv_w=d_scw, cf_conv_w=d_cfw, cf_conv_b=d_cfb, cf_ln_g=d_cflg, cf_ln_b=d_cflb,
                              pool_w=d_pw, pool_scale=d_psc, norm_ffn_g=dg_ffn, norm_ple_g=dg_ple)
    grad_x = dh[None]

    grads, delta, new_m, new_v = {}, {}, {}, {}
    per_layer = [n for n in SMALL if n != "final_norm_g"]
    vecs = [small_grads[l][n] for n in per_layer for l in range(depth)] + [d_final_g, loss_part]
    packed, _ = _pack_rows(vecs)
    total = _sum8("sum_small", _allgather8("gather_small", [packed])[0]).reshape(-1)
    stacked_shapes = [(depth,) + (w[n].shape[1:] if n not in CHIP_SPLIT else (w[n].shape[1], GROUP)) for n in per_layer]
    pieces = _unpack(total, stacked_shapes + [(d,), ()])
    loss = pieces[-1]
    grads["final_norm_g"] = pieces[-2]
    chip_off = (2 * xi + yi) * HEAD
    for n, g in zip(per_layer, pieces):
        grads[n] = lax.dynamic_slice_in_dim(g, chip_off, HEAD, axis=2) if n in CHIP_SPLIT else g

    finish_exchange(0, 0, dh)
    for layer, send_sems, recv_sems in joins:
        bufs = _join_wait(f"rs_join_wait_{layer}", send_sems, recv_sems, [grad_bufs[n] for n in BIG], layer, total)
        grad_bufs.update(zip(BIG, bufs))
    behind = (exchanges[1][4],)
    for gi, names in enumerate(RS_GROUPS):
        if gi == 1:
            finish_exchange(0, 1, behind[0])
        joined = _sibling_join_halves(f"rs_join_{gi}", [grad_bufs[n] for n in names], 0, after=behind)
        for n, g in zip(names, joined):
            shp = w[n].shape
            two_d = (shp[0] * shp[1], shp[2])
            grads[n] = g.reshape(shp)
            dl, mn, vn = _adamw(f"adamw_{n}", w[n].reshape(two_d), g.reshape(two_d), mom[n].reshape(two_d),
                                var[n].reshape(two_d))
            delta[n], new_m[n], new_v[n] = dl.reshape(shp), mn.reshape(shp), vn.reshape(shp)
            behind = (dl,)
    small_shapes = [w[n].shape for n in SMALL]
    pw, _ = _pack_rows([w[n] for n in SMALL])
    pg_, _ = _pack_rows([grads[n] for n in SMALL])
    pm, _ = _pack_rows([mom[n] for n in SMALL])
    pv, _ = _pack_rows([var[n] for n in SMALL])
    dl, mn, vn = _adamw("adamw_small", pw, pg_, pm, pv)
    for n, a, b, cc_ in zip(SMALL, _unpack(dl.reshape(-1), small_shapes), _unpack(mn.reshape(-1), small_shapes),
                            _unpack(vn.reshape(-1), small_shapes)):
        delta[n], new_m[n], new_v[n] = a, b, cc_

    return (loss, grad_x, *[grads[n] for n in WEIGHTS], *[delta[n] for n in WEIGHTS],
            *[new_m[n] for n in WEIGHTS], *[new_v[n] for n in WEIGHTS])
```

```python
import functools
import math

import jax
import jax.numpy as jnp
from jax import lax
from jax.experimental import pallas as pl
from jax.experimental.pallas import tpu as pltpu

F32 = jnp.float32
BF16 = jnp.bfloat16
MESH = pl.DeviceIdType.MESH

HEAD = 128
GROUP = 512
EPS = 1e-6
HALO = 32
CHUNK_ROWS = 128
POOL_LEVELS = 4
PACK_ROWS = 512

ADAM_LR = 0.001
ADAM_B1 = 0.9
ADAM_B2 = 0.999
ADAM_EPS = 1e-08
ADAM_WD = 0.01
ADAM_STEP = 10

VMEM_LIMIT = 56 * 1024 * 1024


def _cp(sem=None, vmem=VMEM_LIMIT):
    return pltpu.CompilerParams(dimension_semantics=sem, vmem_limit_bytes=vmem)


def _sigmoid(x):
    return 0.5 * jnp.tanh(0.5 * x) + 0.5


_GELU_K = math.sqrt(2.0 / math.pi)
_GELU_C = 0.044715


def _gelu(x):
    t = jnp.tanh(_GELU_K * (x + _GELU_C * x * x * x))
    return 0.5 * x * (1.0 + t)


def _gelu_grad(x):
    t = jnp.tanh(_GELU_K * (x + _GELU_C * x * x * x))
    return 0.5 * (1.0 + t) + 0.5 * x * (1.0 - t * t) * _GELU_K * (1.0 + 3.0 * _GELU_C * x * x)


def _mesh_pos():
    return lax.axis_index("x"), lax.axis_index("y"), lax.axis_index("c")


def _any_specs(n):
    return [pl.BlockSpec(memory_space=pl.ANY)] * n


def _allgather8(name, blocks):
    n = len(blocks)

    def body(*refs):
        ins, outs = refs[:n], refs[n:2 * n]
        send_sems, recv_sems, local_sems = refs[2 * n:]
        x, y, c = _mesh_pos()
        me, sibling = (x, y, c), (x, y, 1 - c)
        chips = [(1 - x, y), (x, 1 - y), (1 - x, 1 - y)]

        def slot(a, dev):
            return outs[a].at[4 * dev[0] + 2 * dev[1] + dev[2]]

        def copy(a, k, block, to, src=None):
            dst = slot(a, block)
            return pltpu.make_async_remote_copy(
                src_ref=dst if src is None else src, dst_ref=dst,
                send_sem=send_sems.at[7 * a + k], recv_sem=recv_sems.at[7 * a + k],
                device_id=to, device_id_type=MESH)

        mine, first, passed = [], [], []
        for a in range(n):
            cp = pltpu.make_async_copy(ins[a], slot(a, me), local_sems.at[a])
            cp.start()
            mine.append(cp)
            cps = [copy(a, 0, me, sibling, src=ins[a])]
            cps += [copy(a, 1 + j, me, (*chip, c), src=ins[a]) for j, chip in enumerate(chips)]
            for cp in cps:
                cp.start()
            first += cps
        for j, chip in enumerate(chips):
            for a in range(n):
                copy(a, 1 + j, (*chip, c), me).wait_recv()
                cp = copy(a, 4 + j, (*chip, c), sibling)
                cp.start()
                passed.append(cp)
        for a in range(n):
            copy(a, 0, sibling, me).wait_recv()
            for j, chip in enumerate(chips):
                copy(a, 4 + j, (*chip, 1 - c), me).wait_recv()
        for cp in first + passed:
            cp.wait_send()
        for cp in mine:
            cp.wait()

    return pl.pallas_call(
        body, name=name,
        out_shape=[jax.ShapeDtypeStruct((8,) + b.shape, b.dtype) for b in blocks],
        in_specs=_any_specs(n), out_specs=_any_specs(n),
        scratch_shapes=[pltpu.SemaphoreType.DMA((7 * n,)), pltpu.SemaphoreType.DMA((7 * n,)),
                        pltpu.SemaphoreType.DMA((n,))],
    )(*blocks)


HBM_SPEC = pl.BlockSpec(memory_space=pltpu.HBM)
SEM_SPEC = pl.BlockSpec(memory_space=pltpu.SEMAPHORE)
ANY_SPEC = pl.BlockSpec(memory_space=pl.ANY)
SPLIT_COPY = pltpu.CompilerParams(has_side_effects=pltpu.SideEffectType.DATAFLOW_SIDE_EFFECTING)


def _hbm(x):
    return pltpu.with_memory_space_constraint(x, pltpu.HBM)


def _other_chips(x, y):
    return [(1 - x, y), (x, 1 - y), (1 - x, 1 - y)]


def _dev_slot(ref, dev):
    return ref.at[4 * dev[0] + 2 * dev[1] + dev[2]]


def _cast_into_landing(name, w, layer, ax, chip_idx):
    _, r, cc = w.shape
    r2, c2 = (r // 2, cc) if ax == 0 else (r, cc // 2)
    tr = _row_tile(r2, 512 if c2 <= 1536 else 256)
    nt = r2 // tr

    def body(chip_ref, w_ref, o_ref):
        o_ref[...] = w_ref[...].astype(BF16)

    if ax == 0:
        in_spec = pl.BlockSpec((None, tr, c2), lambda hf, i, chip_ref: (layer, hf * nt + i, 0))
    else:
        in_spec = pl.BlockSpec((None, tr, c2), lambda hf, i, chip_ref: (layer, i, hf))
    return pl.pallas_call(
        body, name=name,
        grid_spec=pltpu.PrefetchScalarGridSpec(
            num_scalar_prefetch=1, grid=(2, nt), in_specs=[in_spec],
            out_specs=pl.BlockSpec((None, tr, c2), lambda hf, i, chip_ref: (2 * chip_ref[0] + hf, i, 0))),
        out_shape=jax.ShapeDtypeStruct((8, r2, c2), BF16), compiler_params=_cp(("parallel", "parallel")),
    )(chip_idx, w)


def _gather_ici_copies(lands, send_sems, recv_sems):
    x, y, c = _mesh_pos()
    pairs = []
    for a in range(len(lands)):
        for j, chip in enumerate(_other_chips(x, y)):
            def copy(dev):
                return pltpu.make_async_remote_copy(
                    src_ref=_dev_slot(lands[a], dev), dst_ref=_dev_slot(lands[a], dev),
                    send_sem=send_sems.at[3 * a + j], recv_sem=recv_sems.at[3 * a + j],
                    device_id=(*chip, c), device_id_type=MESH)
            pairs.append((copy((x, y, c)), copy((*chip, c))))
    return pairs


def _exchange_ici_copies(ins, lands, send_sems, recv_sems):
    x, y, c = _mesh_pos()
    pairs = []
    for a in range(len(ins)):
        for k, chip in enumerate(_other_chips(x, y)):
            there = 2 * chip[0] + chip[1]
            def copy(dst_entry):
                return pltpu.make_async_remote_copy(
                    src_ref=ins[a].at[there], dst_ref=lands[a].at[dst_entry],
                    send_sem=send_sems.at[3 * a + k], recv_sem=recv_sems.at[3 * a + k],
                    device_id=(*chip, c), device_id_type=MESH)
            pairs.append((copy(2 * x + y), copy(there)))
    return pairs


def _gather_start(name, lands, dep):
    n = len(lands)

    def body(*refs):
        token = refs[-1]
        for send, _ in _gather_ici_copies(refs[:n], refs[n + 1], refs[n + 2]):
            send.start()
        token[...] = jnp.zeros_like(token)

    out = pl.pallas_call(
        body, name=name,
        out_shape=(pltpu.SemaphoreType.DMA((3 * n,)), pltpu.SemaphoreType.DMA((3 * n,)),
                   *[pltpu.HBM(l.shape, l.dtype) for l in lands], jax.ShapeDtypeStruct((8, 128), F32)),
        in_specs=[HBM_SPEC] * n + [ANY_SPEC],
        out_specs=(SEM_SPEC, SEM_SPEC, *[HBM_SPEC] * n, pl.BlockSpec(memory_space=pltpu.VMEM)),
        input_output_aliases={a: 2 + a for a in range(n)},
        compiler_params=SPLIT_COPY,
    )(*[_hbm(l) for l in lands], dep)
    return out[0], out[1], list(out[2:2 + n]), out[-1]


def _gather_wait(name, send_sems, recv_sems, lands, after):
    n = len(lands)

    def body(*refs):
        for send, recv in _gather_ici_copies(refs[:n], refs[n], refs[n + 1]):
            send.wait_send()
            recv.wait_recv()

    out = pl.pallas_call(
        body, name=name,
        out_shape=[pltpu.HBM(l.shape, l.dtype) for l in lands],
        in_specs=[HBM_SPEC] * n + [SEM_SPEC, SEM_SPEC, ANY_SPEC],
        out_specs=[HBM_SPEC] * n,
        input_output_aliases={a: a for a in range(n)},
        compiler_params=SPLIT_COPY,
    )(*lands, send_sems, recv_sems, after)
    return list(out)


def _share_copies(lands, send_sems, recv_sems):
    x, y, c = _mesh_pos()
    pairs = []
    for a in range(len(lands)):
        for j, chip in enumerate(_other_chips(x, y)):
            def copy(dev):
                slot = _dev_slot(lands[a], dev)
                return pltpu.make_async_remote_copy(
                    src_ref=slot, dst_ref=slot, send_sem=send_sems.at[3 * a + j], recv_sem=recv_sems.at[3 * a + j],
                    device_id=(x, y, 1 - c), device_id_type=MESH)
            pairs.append((copy((*chip, c)), copy((*chip, 1 - c))))
    return pairs


def _share_start(name, lands):
    n = len(lands)

    def body(*refs):
        token = refs[-1]
        for send, _ in _share_copies(refs[:n], refs[n], refs[n + 1]):
            send.start()
        token[...] = jnp.zeros_like(token)

    out = pl.pallas_call(
        body, name=name,
        out_shape=(pltpu.SemaphoreType.DMA((3 * n,)), pltpu.SemaphoreType.DMA((3 * n,)),
                   *[pltpu.HBM(l.shape, l.dtype) for l in lands], jax.ShapeDtypeStruct((8, 128), F32)),
        in_specs=[HBM_SPEC] * n,
        out_specs=(SEM_SPEC, SEM_SPEC, *[HBM_SPEC] * n, pl.BlockSpec(memory_space=pltpu.VMEM)),
        input_output_aliases={a: 2 + a for a in range(n)},
        compiler_params=SPLIT_COPY,
    )(*lands)
    return out[0], out[1], list(out[2:2 + n]), out[-1]


def _share_wait(name, send_sems, recv_sems, lands, after):
    n = len(lands)

    def body(*refs):
        for send, recv in _share_copies(refs[:n], refs[n], refs[n + 1]):
            send.wait_send()
            recv.wait_recv()

    out = pl.pallas_call(
        body, name=name,
        out_shape=[pltpu.HBM(l.shape, l.dtype) for l in lands],
        in_specs=[HBM_SPEC] * n + [SEM_SPEC, SEM_SPEC, ANY_SPEC],
        out_specs=[HBM_SPEC] * n,
        input_output_aliases={a: a for a in range(n)},
        compiler_params=SPLIT_COPY,
    )(*lands, send_sems, recv_sems, after)
    return list(out)


def _sibling_copies(ins, lands, send_sems, recv_sems):
    x, y, c = _mesh_pos()
    return [pltpu.make_async_remote_copy(
        src_ref=ins[a].at[j, 1 - c], dst_ref=lands[a].at[j],
        send_sem=send_sems.at[4 * a + j], recv_sem=recv_sems.at[4 * a + j],
        device_id=(x, y, 1 - c), device_id_type=MESH) for a in range(len(ins)) for j in range(4)]


def _sibling_start(name, grads):
    n = len(grads)

    def body(*refs):
        token = refs[-1]
        for cp in _sibling_copies(refs[:n], refs[n:2 * n], refs[2 * n], refs[2 * n + 1]):
            cp.start()
        token[...] = jnp.zeros_like(token)

    lands = [_hbm(lax.empty((4,) + g.shape[2:], g.dtype)) for g in grads]
    out = pl.pallas_call(
        body, name=name,
        out_shape=(pltpu.SemaphoreType.DMA((4 * n,)), pltpu.SemaphoreType.DMA((4 * n,)),
                   *[pltpu.HBM(g.shape, g.dtype) for g in grads], *[pltpu.HBM(l.shape, l.dtype) for l in lands],
                   jax.ShapeDtypeStruct((8, 128), F32)),
        in_specs=[HBM_SPEC] * (2 * n),
        out_specs=(SEM_SPEC, SEM_SPEC, *[HBM_SPEC] * (2 * n), pl.BlockSpec(memory_space=pltpu.VMEM)),
        input_output_aliases={i: 2 + i for i in range(2 * n)},
        compiler_params=SPLIT_COPY,
    )(*[_hbm(g) for g in grads], *lands)
    return out[0], out[1], list(out[2:2 + n]), list(out[2 + n:2 + 2 * n]), out[-1]


def _sibling_wait(name, send_sems, recv_sems, grads, lands, after):
    n = len(grads)

    def body(*refs):
        for cp in _sibling_copies(refs[:n], refs[n:2 * n], refs[2 * n], refs[2 * n + 1]):
            cp.wait_send()
            cp.wait_recv()

    out = pl.pallas_call(
        body, name=name,
        out_shape=[pltpu.HBM(g.shape, g.dtype) for g in grads] + [pltpu.HBM(l.shape, l.dtype) for l in lands],
        in_specs=[HBM_SPEC] * (2 * n) + [SEM_SPEC, SEM_SPEC, ANY_SPEC],
        out_specs=[HBM_SPEC] * (2 * n),
        input_output_aliases={i: i for i in range(2 * n)},
        compiler_params=SPLIT_COPY,
    )(*grads, *lands, send_sems, recv_sems, after)
    return list(out[:n]), list(out[n:])


def _chip_exchange_start(name, parts):
    n = len(parts)

    def body(*refs):
        ins, lands = refs[:n], refs[n:2 * n]
        token = refs[-1]
        for send, _ in _exchange_ici_copies(ins, lands, refs[2 * n], refs[2 * n + 1]):
            send.start()
        token[...] = jnp.zeros_like(token)

    lands = [_hbm(lax.empty(p.shape, p.dtype)) for p in parts]
    out = pl.pallas_call(
        body, name=name,
        out_shape=(pltpu.SemaphoreType.DMA((3 * n,)), pltpu.SemaphoreType.DMA((3 * n,)),
                   *[pltpu.HBM(p.shape, p.dtype) for p in parts], *[pltpu.HBM(p.shape, p.dtype) for p in parts],
                   jax.ShapeDtypeStruct((8, 128), F32)),
        in_specs=[HBM_SPEC] * (2 * n),
        out_specs=(SEM_SPEC, SEM_SPEC, *[HBM_SPEC] * (2 * n), pl.BlockSpec(memory_space=pltpu.VMEM)),
        input_output_aliases={i: 2 + i for i in range(2 * n)},
        compiler_params=SPLIT_COPY,
    )(*[_hbm(p) for p in parts], *lands)
    return out[0], out[1], list(out[2:2 + n]), list(out[2 + n:2 + 2 * n]), out[-1]


def _chip_exchange_wait(name, send_sems, recv_sems, parts, lands, after):
    n = len(parts)

    def body(*refs):
        ins, lands_in = refs[:n], refs[n:2 * n]
        for send, recv in _exchange_ici_copies(ins, lands_in, refs[2 * n], refs[2 * n + 1]):
            send.wait_send()
            recv.wait_recv()

    out = pl.pallas_call(
        body, name=name,
        out_shape=[pltpu.HBM(p.shape, p.dtype) for p in parts] * 2,
        in_specs=[HBM_SPEC] * (2 * n) + [SEM_SPEC, SEM_SPEC, ANY_SPEC],
        out_specs=[HBM_SPEC] * (2 * n),
        input_output_aliases={i: i for i in range(2 * n)},
        compiler_params=SPLIT_COPY,
    )(*parts, *lands, send_sems, recv_sems, after)
    return list(out[:n]), list(out[n:])


def _join_copies(bufs, send_sems, recv_sems, layer):
    x, y, c = _mesh_pos()

    def copy(a, half):
        return pltpu.make_async_remote_copy(
            src_ref=bufs[a].at[layer, half], dst_ref=bufs[a].at[layer, half],
            send_sem=send_sems.at[a], recv_sem=recv_sems.at[a], device_id=(x, y, 1 - c), device_id_type=MESH)

    return [(copy(a, c), copy(a, 1 - c)) for a in range(len(bufs))]


def _sibling_join_halves(name, bufs, layer, after=()):
    n = len(bufs)

    def body(*refs):
        outs = refs[n + len(after):2 * n + len(after)]
        pairs = _join_copies(outs, refs[-2], refs[-1], layer)
        for send, _ in pairs:
            send.start()
        for send, recv in pairs:
            send.wait_send()
            recv.wait_recv()

    return pl.pallas_call(
        body, name=name, out_shape=[jax.ShapeDtypeStruct(b.shape, b.dtype) for b in bufs],
        in_specs=_any_specs(n + len(after)), out_specs=_any_specs(n),
        input_output_aliases={a: a for a in range(n)},
        scratch_shapes=[pltpu.SemaphoreType.DMA((n,)), pltpu.SemaphoreType.DMA((n,))],
    )(*bufs, *after)


def _join_start(name, bufs, layer):
    n = len(bufs)

    def body(*refs):
        token = refs[-1]
        for send, _ in _join_copies(refs[:n], refs[n], refs[n + 1], layer):
            send.start()
        token[...] = jnp.zeros_like(token)

    out = pl.pallas_call(
        body, name=name,
        out_shape=(pltpu.SemaphoreType.DMA((n,)), pltpu.SemaphoreType.DMA((n,)),
                   *[pltpu.HBM(b.shape, b.dtype) for b in bufs], jax.ShapeDtypeStruct((8, 128), F32)),
        in_specs=[HBM_SPEC] * n,
        out_specs=(SEM_SPEC, SEM_SPEC, *[HBM_SPEC] * n, pl.BlockSpec(memory_space=pltpu.VMEM)),
        input_output_aliases={a: 2 + a for a in range(n)},
        compiler_params=SPLIT_COPY,
    )(*[_hbm(b) for b in bufs])
    return out[0], out[1], list(out[2:2 + n]), out[-1]


def _join_wait(name, send_sems, recv_sems, bufs, layer, after):
    n = len(bufs)

    def body(*refs):
        for send, recv in _join_copies(refs[:n], refs[n], refs[n + 1], layer):
            send.wait_send()
            recv.wait_recv()

    out = pl.pallas_call(
        body, name=name,
        out_shape=[pltpu.HBM(b.shape, b.dtype) for b in bufs],
        in_specs=[HBM_SPEC] * n + [SEM_SPEC, SEM_SPEC, ANY_SPEC],
        out_specs=[HBM_SPEC] * n,
        input_output_aliases={a: a for a in range(n)},
        compiler_params=SPLIT_COPY,
    )(*bufs, send_sems, recv_sems, after)
    return list(out)


def _row_tile(rows, want):
    t = 1
    while t * 2 <= min(rows, want):
        t *= 2
    while rows % t:
        t //= 2
    return t


def _fit_rows(rows, cols, max_elems):
    for k in range(1, rows + 1):
        if rows % k == 0 and (rows // k) % 16 == 0 and (rows // k) * cols <= max_elems:
            return rows // k
    return _row_tile(rows, max(8, max_elems // cols))


def _add_halves(name, grads, recvd, c_idx):
    outs = []
    for a, (g, r) in enumerate(zip(grads, recvd)):
        _, _, r2, cc = g.shape
        tr = _fit_rows(r2, cc, 1536 * 1024)

        def body(c_ref, g_ref, r_ref, o_ref):
            o_ref[...] = (g_ref[...].astype(F32) + r_ref[...].astype(F32)).astype(BF16)

        outs.append(pl.pallas_call(
            body, name=f"{name}_{a}",
            grid_spec=pltpu.PrefetchScalarGridSpec(
                num_scalar_prefetch=1, grid=(4, r2 // tr),
                in_specs=[pl.BlockSpec((None, None, tr, cc), lambda j, i, c_ref: (j, c_ref[0], i, 0)),
                          pl.BlockSpec((None, tr, cc), lambda j, i, c_ref: (j, i, 0))],
                out_specs=pl.BlockSpec((None, tr, cc), lambda j, i, c_ref: (j, i, 0))),
            out_shape=jax.ShapeDtypeStruct(r.shape, BF16), compiler_params=_cp(("parallel", "parallel")),
        )(c_idx, g, r))
    return outs


def _sum_parts(name, parts, recvd, bufs, layer, where):
    outs = []
    for a, (p, r, buf) in enumerate(zip(parts, recvd, bufs)):
        _, r2, cc = p.shape
        tr = _fit_rows(r2, cc, 768 * 1024)

        def body(where_ref, own_ref, r1_ref, r2_ref, r3_ref, buf_ref, o_ref):
            acc = own_ref[...].astype(F32)
            for ref in (r1_ref, r2_ref, r3_ref):
                acc = acc + ref[...].astype(F32)
            o_ref[...] = acc

        def entry(k):
            return pl.BlockSpec((None, tr, cc), lambda i, where_ref: (where_ref[k], i, 0))

        outs.append(pl.pallas_call(
            body, name=f"{name}_{a}",
            grid_spec=pltpu.PrefetchScalarGridSpec(
                num_scalar_prefetch=1, grid=(r2 // tr,),
                in_specs=[entry(0), entry(1), entry(2), entry(3), ANY_SPEC],
                out_specs=pl.BlockSpec((None, None, tr, cc), lambda i, where_ref: (layer, where_ref[4], i, 0))),
            out_shape=jax.ShapeDtypeStruct(buf.shape, F32), input_output_aliases={5: 0},
            compiler_params=_cp(("parallel",)),
        )(where, p, r, r, r, buf))
    return outs


def _sum8(name, gathered):
    _, r, cc = gathered.shape
    tr = _row_tile(r, 512)

    def body(g_ref, o_ref):
        acc = g_ref[0]
        for i in range(1, 8):
            acc = acc + g_ref[i]
        o_ref[...] = acc

    return pl.pallas_call(
        body, name=name, grid=(r // tr,),
        in_specs=[pl.BlockSpec((8, tr, cc), lambda i: (0, i, 0))],
        out_specs=pl.BlockSpec((tr, cc), lambda i: (i, 0)),
        out_shape=jax.ShapeDtypeStruct((r, cc), F32), compiler_params=_cp(("parallel",)),
    )(gathered)


def _adamw(name, w, g, m, v):
    r, cc = w.shape
    tr = _row_tile(r, max(8, (512 * 1024) // cc))

    def body(w_ref, g_ref, m_ref, v_ref, d_ref, mo_ref, vo_ref):
        gg = g_ref[...]
        mn = ADAM_B1 * m_ref[...] + (1.0 - ADAM_B1) * gg
        vn = ADAM_B2 * v_ref[...] + (1.0 - ADAM_B2) * (gg * gg)
        m_hat = mn / (1.0 - ADAM_B1 ** ADAM_STEP)
        v_hat = vn / (1.0 - ADAM_B2 ** ADAM_STEP)
        d_ref[...] = -ADAM_LR * (m_hat / (jnp.sqrt(v_hat) + ADAM_EPS) + ADAM_WD * w_ref[...])
        mo_ref[...] = mn
        vo_ref[...] = vn

    spec = pl.BlockSpec((tr, cc), lambda i: (i, 0))
    return pl.pallas_call(
        body, name=name, grid=(r // tr,), in_specs=[spec] * 4, out_specs=[spec] * 3,
        out_shape=[jax.ShapeDtypeStruct(w.shape, F32)] * 3, compiler_params=_cp(("parallel",)),
    )(w, g, m, v)


def _rms_fwd(name, h, g, after=()):
    s, d = h.shape
    tr = _row_tile(s, 512)

    def body(h_ref, g_ref, *rest):
        y_ref = rest[-1]
        xv = h_ref[...]
        r = lax.rsqrt(jnp.mean(xv * xv, axis=-1, keepdims=True) + EPS)
        y_ref[...] = (xv * r * g_ref[...]).astype(BF16)

    return pl.pallas_call(
        body, name=name, grid=(s // tr,),
        in_specs=[pl.BlockSpec((tr, d), lambda i: (i, 0)), pl.BlockSpec((1, d), lambda i: (0, 0))]
        + [ANY_SPEC] * len(after),
        out_specs=pl.BlockSpec((tr, d), lambda i: (i, 0)),
        out_shape=jax.ShapeDtypeStruct((s, d), BF16), compiler_params=_cp(("parallel",)),
    )(h, g, *after)


def _rms_bwd_rows(xv, gv, dy):
    d = xv.shape[-1]
    r = lax.rsqrt(jnp.mean(xv * xv, axis=-1, keepdims=True) + EPS)
    dxn = dy * gv
    proj = jnp.sum(dxn * xv, axis=-1, keepdims=True) * (1.0 / d)
    dx = r * dxn - xv * (r * r * r) * proj
    return dx, dy * xv * r


def _rms_bwd(name, h, g, dy, dres):
    s, d = h.shape
    tr = _row_tile(s, 256)

    def body(h_ref, g_ref, dy_ref, dres_ref, dh_ref, dhb_ref, dg_ref):
        dx, dgp = _rms_bwd_rows(h_ref[...], g_ref[...], dy_ref[...].astype(F32))
        dh = dres_ref[...] + dx
        dh_ref[...] = dh
        dhb_ref[...] = dh.astype(BF16)

        @pl.when(pl.program_id(0) == 0)
        def _():
            dg_ref[...] = jnp.zeros_like(dg_ref)

        dg_ref[...] += jnp.sum(dgp, axis=0, keepdims=True)

    row = pl.BlockSpec((tr, d), lambda i: (i, 0))
    vec = pl.BlockSpec((1, d), lambda i: (0, 0))
    return pl.pallas_call(
        body, name=name, grid=(s // tr,), in_specs=[row, vec, row, row], out_specs=[row, row, vec],
        out_shape=[jax.ShapeDtypeStruct((s, d), F32), jax.ShapeDtypeStruct((s, d), BF16),
                   jax.ShapeDtypeStruct((1, d), F32)],
        compiler_params=_cp(("arbitrary",)),
    )(h, g, dy, dres)


def _loss_head(name, h, g, target):
    s, d = h.shape
    tr = _row_tile(s, 256)

    def body(h_ref, g_ref, t_ref, loss_ref, dh_ref, dhb_ref, dg_ref):
        xv = h_ref[...]
        gv = g_ref[...]
        r = lax.rsqrt(jnp.mean(xv * xv, axis=-1, keepdims=True) + EPS)
        err = xv * r * gv - t_ref[...]
        part = 0.5 * jnp.sum(jnp.sum(err * err, axis=-1, keepdims=True) * (1.0 / d), axis=0, keepdims=True)
        dx, dgp = _rms_bwd_rows(xv, gv, err * (1.0 / d))
        dh_ref[...] = dx
        dhb_ref[...] = dx.astype(BF16)

        @pl.when(pl.program_id(0) == 0)
        def _():
            dg_ref[...] = jnp.zeros_like(dg_ref)
            loss_ref[...] = jnp.zeros_like(loss_ref)

        dg_ref[...] += jnp.sum(dgp, axis=0, keepdims=True)
        loss_ref[...] += part

    row = pl.BlockSpec((tr, d), lambda i: (i, 0))
    vec = pl.BlockSpec((1, d), lambda i: (0, 0))
    one = pl.BlockSpec((1, 1), lambda i: (0, 0))
    return pl.pallas_call(
        body, name=name, grid=(s // tr,), in_specs=[row, vec, row], out_specs=[one, row, row, vec],
        out_shape=[jax.ShapeDtypeStruct((1, 1), F32), jax.ShapeDtypeStruct((s, d), F32),
                   jax.ShapeDtypeStruct((s, d), BF16), jax.ShapeDtypeStruct((1, d), F32)],
        compiler_params=_cp(("arbitrary",)),
    )(h, g, target)


NN = (((1,), (0,)), ((), ()))
NT = (((1,), (1,)), ((), ()))
TN = (((0,), (0,)), ((), ()))


def _matmul(name, grid, operands, in_specs, pairs, n_acc, acc_shape, epilogue, out_specs, out_shape, after=()):
    operands = list(operands) + list(after)
    in_specs = list(in_specs) + [ANY_SPEC] * len(after)
    n_in = len(operands)
    n_out = len(out_shape)
    nk = grid[2]

    def body(*refs):
        ins, outs, accs = refs[:n_in], refs[n_in:n_in + n_out], refs[n_in + n_out:]
        k = pl.program_id(2)

        @pl.when(k == 0)
        def _():
            for acc in accs:
                acc[...] = jnp.zeros_like(acc)

        for li, ri, ai, dims in pairs:
            accs[ai][...] += lax.dot_general(ins[li][...], ins[ri][...], dims, preferred_element_type=F32)

        @pl.when(k == nk - 1)
        def _():
            epilogue([acc[...] for acc in accs], ins, outs)

    return pl.pallas_call(
        body, name=name, grid=grid, in_specs=in_specs, out_specs=out_specs, out_shape=out_shape,
        scratch_shapes=[pltpu.VMEM(acc_shape, F32)] * n_acc,
        compiler_params=_cp(("parallel", "parallel", "arbitrary")),
    )(*operands)


def _ep_store(dtype):
    def ep(accs, ins, outs):
        outs[0][...] = accs[0].astype(dtype)
    return ep


def _w_spec_nt(wg, kind, to, tc):
    _, a, b = wg.shape
    if kind == "col":
        oph, cps = a // to, b // tc
        return pl.BlockSpec((None, to, tc), lambda i, jo, kc: (2 * (kc // cps) + jo // oph, jo % oph, kc % cps))
    ops, cph = a // to, b // tc
    return pl.BlockSpec((None, to, tc), lambda i, jo, kc: (2 * (jo // ops) + kc // cph, jo % ops, kc % cph))


SUB_COLS = 256


def _matmul_wide(name, grid, operands, in_specs, groups, tn, epilogue, out_specs, out_shape, after=()):
    operands = list(operands) + list(after)
    in_specs = list(in_specs) + [ANY_SPEC] * len(after)
    n_in = len(operands)

    def body(*refs):
        ins, outs = refs[:n_in], refs[n_in:]
        for s0 in range(0, tn, SUB_COLS):
            cols = slice(s0, min(s0 + SUB_COLS, tn))
            accs = []
            for group in groups:
                acc = None
                for li, (c0, cw), ri, dims in group:
                    rhs = ins[ri][:, cols] if dims == NN else ins[ri][cols, :]
                    part = lax.dot_general(ins[li][:, c0:c0 + cw], rhs, dims, preferred_element_type=F32)
                    acc = part if acc is None else acc + part
                accs.append(acc)
            epilogue(accs, ins, outs, cols)

    return pl.pallas_call(
        body, name=name, grid=grid, in_specs=in_specs, out_specs=out_specs, out_shape=out_shape,
        compiler_params=_cp(("parallel", "parallel")),
    )(*operands)


def _wide_store(dtype):
    def ep(accs, ins, outs, cols):
        outs[0][:, cols] = accs[0].astype(dtype)
    return ep


def _wide_nn_weights(wg, kind, tn, first):
    _, a, b = wg.shape
    per = b // tn
    if kind == "col":
        specs = [pl.BlockSpec((None, a, tn), lambda i, j, hf=hf: (2 * (j // per) + hf, 0, j % per)) for hf in range(2)]
    else:
        specs = [pl.BlockSpec((None, a, tn), lambda i, j, ch=ch: (2 * ch + j // per, 0, j % per)) for ch in range(4)]
    dots = [(0, (q * a, a), first + q, NN) for q in range(len(specs))]
    return [wg] * len(specs), specs, dots


def _wide_nt_weights(wg, kind, to, first):
    _, a, b = wg.shape
    per = a // to
    if kind == "col":
        specs = [pl.BlockSpec((None, to, b), lambda i, j, ch=ch: (2 * ch + j // per, j % per, 0)) for ch in range(4)]
    else:
        specs = [pl.BlockSpec((None, to, b), lambda i, j, hf=hf: (2 * (j // per) + hf, j % per, 0)) for hf in range(2)]
    dots = [(0, (q * b, b), first + q, NT) for q in range(len(specs))]
    return [wg] * len(specs), specs, dots


def _mm_nn_wide(name, x, wg, kind, tm, tn, epilogue=None, extra=(), extra_specs=(), out_specs=None, out_shape=None,
                after=()):
    m, kdim = x.shape
    _, a, b = wg.shape
    n = 4 * b if kind == "col" else 2 * b
    tm = min(tm, m)
    ws, wspecs, dots = _wide_nn_weights(wg, kind, tn, 1)
    if out_shape is None:
        out_shape = [jax.ShapeDtypeStruct((m, n), F32)]
        out_specs = [pl.BlockSpec((tm, tn), lambda i, j: (i, j))]
        epilogue = _wide_store(F32)
    return _matmul_wide(name, (m // tm, n // tn), [x] + ws + list(extra),
                        [pl.BlockSpec((tm, kdim), lambda i, j: (i, 0))] + wspecs + list(extra_specs),
                        [dots], tn, epilogue, out_specs, out_shape, after=after)


def _mm_nt_wide(name, dy, wg, kind, tm, to, out_dtype=BF16, after=()):
    m, n = dy.shape
    _, a, b = wg.shape
    kdim = 2 * a if kind == "col" else 4 * a
    tm = min(tm, m)
    ws, wspecs, dots = _wide_nt_weights(wg, kind, to, 1)
    return _matmul_wide(name, (m // tm, kdim // to), [dy] + ws,
                        [pl.BlockSpec((tm, n), lambda i, j: (i, 0))] + wspecs, [dots], to, _wide_store(out_dtype),
                        [pl.BlockSpec((tm, to), lambda i, j: (i, j))],
                        [jax.ShapeDtypeStruct((m, kdim), out_dtype)], after=after)[0]


def _mm_tn_wide(name, a, dy, kind, tr, tn):
    m, kdim = a.shape
    _, n = dy.shape

    def body(a_ref, dy_ref, o_ref):
        for s0 in range(0, tn, 2 * SUB_COLS):
            cols = slice(s0, min(s0 + 2 * SUB_COLS, tn))
            o_ref[:, cols] = lax.dot_general(a_ref[...], dy_ref[:, cols], TN, preferred_element_type=F32).astype(BF16)

    if kind == "col":
        ns = n // 4
        per = ns // tn
        out_shape = jax.ShapeDtypeStruct((4, kdim, ns), BF16)
        out_spec = pl.BlockSpec((None, tr, tn), lambda r, j: (j // per, r, j % per))
    else:
        rs = kdim // 4
        per = rs // tr
        out_shape = jax.ShapeDtypeStruct((4, rs, n), BF16)
        out_spec = pl.BlockSpec((None, tr, tn), lambda r, j: (r // per, r % per, j))
    return pl.pallas_call(
        body, name=name, grid=(kdim // tr, n // tn),
        in_specs=[pl.BlockSpec((m, tr), lambda r, j: (0, r)), pl.BlockSpec((m, tn), lambda r, j: (0, j))],
        out_specs=out_spec, out_shape=out_shape, compiler_params=_cp(("parallel", "parallel")),
    )(a, dy)


def _zero_halo(pad_ref, s):
    z = jnp.zeros((HALO, pad_ref.shape[1]), F32)
    pad_ref[pl.ds(0, HALO), :] = z
    pad_ref[pl.ds(HALO + s, HALO), :] = z


def _window(pad_ref, r0, rows):
    return pad_ref[pl.ds(r0, rows + 2 * HALO), :]


def _delayed(win, k, rows):
    if k == 0:
        return win[HALO:HALO + rows]
    return pltpu.roll(win, k, axis=0)[HALO:HALO + rows]


def _advanced(win, k, rows):
    if k == 0:
        return win[HALO:HALO + rows]
    return pltpu.roll(win, win.shape[0] - k, axis=0)[HALO:HALO + rows]


def _fold8(x):
    return jnp.sum(x.reshape(x.shape[0] // 8, 8, x.shape[1]), axis=0)


def _chunks(s):
    rows = min(CHUNK_ROWS, s)
    return rows, s // rows


def _col_spec(s, first_block):
    return pl.BlockSpec((s, HEAD), lambda j: (0, first_block + j))


def _sgu_fwd(name, z, ln_g, ln_b, w, b):
    s = z.shape[0]
    tr = _row_tile(s, 1024)
    nh = GROUP // HEAD

    def body(u_ref, v_ref, lg_ref, lb_ref, w_ref, b_ref, o_ref):
        row = lax.broadcasted_iota(jnp.int32, (HEAD, HEAD), 0)
        col = lax.broadcasted_iota(jnp.int32, (HEAD, HEAD), 1)
        wm = jnp.where(row >= col, w_ref[...], 0.0).astype(BF16)
        for ck in range(tr // HEAD):
            rs = pl.ds(ck * HEAD, HEAD)
            u = _gelu(u_ref[rs, :])
            v = _gelu(v_ref[rs, :])
            mu = jnp.mean(v, axis=-1, keepdims=True)
            vc = v - mu
            var = jnp.mean(vc * vc, axis=-1, keepdims=True)
            vln = vc * lax.rsqrt(var + EPS) * lg_ref[...] + lb_ref[...]
            sp = jnp.dot(wm, vln.astype(BF16), preferred_element_type=F32) + b_ref[...]
            o_ref[rs, :] = (u * sp).astype(BF16)

    head_vec = pl.BlockSpec((None, 1, HEAD), lambda h, i: (h, 0, 0))
    return pl.pallas_call(
        body, name=name, grid=(nh, s // tr),
        in_specs=[pl.BlockSpec((tr, HEAD), lambda h, i: (i, h)), pl.BlockSpec((tr, HEAD), lambda h, i: (i, nh + h)),
                  head_vec, head_vec, pl.BlockSpec((None, HEAD, HEAD), lambda h, i: (h, 0, 0)),
                  pl.BlockSpec((None, HEAD, 1), lambda h, i: (h, 0, 0))],
        out_specs=pl.BlockSpec((tr, HEAD), lambda h, i: (i, h)),
        out_shape=jax.ShapeDtypeStruct((s, GROUP), BF16), compiler_params=_cp(("parallel", "parallel")),
    )(z, z, ln_g, ln_b, w, b)


def _sgu_bwd(name, z, d_o, ln_g, ln_b, w, b):
    s = z.shape[0]
    tr = _row_tile(s, 1024)
    nh = GROUP // HEAD

    def body(u_ref, v_ref, do_ref, lg_ref, lb_ref, w_ref, b_ref, du_ref, dv_ref, dlg_ref, dlb_ref, dw_ref, db_ref,
             dsp_acc):
        row = lax.broadcasted_iota(jnp.int32, (HEAD, HEAD), 0)
        col = lax.broadcasted_iota(jnp.int32, (HEAD, HEAD), 1)
        tril = row >= col
        wm = jnp.where(tril, w_ref[...], 0.0).astype(BF16)
        i = pl.program_id(1)

        @pl.when(i == 0)
        def _():
            dlg_ref[...] = jnp.zeros_like(dlg_ref)
            dlb_ref[...] = jnp.zeros_like(dlb_ref)
            dw_ref[...] = jnp.zeros_like(dw_ref)
            dsp_acc[...] = jnp.zeros_like(dsp_acc)

        dlg = jnp.zeros((1, HEAD), F32)
        dlb = jnp.zeros((1, HEAD), F32)
        dw = jnp.zeros((HEAD, HEAD), F32)
        dsp_sum = jnp.zeros((HEAD, HEAD), F32)
        for ck in range(tr // HEAD):
            rs = pl.ds(ck * HEAD, HEAD)
            zu = u_ref[rs, :]
            zv = v_ref[rs, :]
            u = _gelu(zu)
            v = _gelu(zv)
            mu = jnp.mean(v, axis=-1, keepdims=True)
            vc = v - mu
            var = jnp.mean(vc * vc, axis=-1, keepdims=True)
            rstd = lax.rsqrt(var + EPS)
            xh = vc * rstd
            vln = (xh * lg_ref[...] + lb_ref[...]).astype(BF16)
            sp = jnp.dot(wm, vln, preferred_element_type=F32) + b_ref[...]
            d_oa = do_ref[rs, :].astype(F32)
            du = d_oa * sp
            dsp = d_oa * u
            dsp_b = dsp.astype(BF16)
            dvln = lax.dot_general(wm, dsp_b, TN, preferred_element_type=F32)
            dw = dw + lax.dot_general(dsp_b, vln, NT, preferred_element_type=F32)
            dsp_sum = dsp_sum + dsp
            dlg = dlg + jnp.sum(dvln * xh, axis=0, keepdims=True)
            dlb = dlb + jnp.sum(dvln, axis=0, keepdims=True)
            dxh = dvln * lg_ref[...]
            dv = rstd * (dxh - jnp.mean(dxh, axis=-1, keepdims=True)
                         - xh * jnp.mean(dxh * xh, axis=-1, keepdims=True))
            du_ref[rs, :] = (du * _gelu_grad(zu)).astype(BF16)
            dv_ref[rs, :] = (dv * _gelu_grad(zv)).astype(BF16)
        dlg_ref[...] += dlg
        dlb_ref[...] += dlb
        dw_ref[...] += jnp.where(tril, dw, 0.0)
        dsp_acc[...] += dsp_sum

        @pl.when(i == pl.num_programs(1) - 1)
        def _():
            db_ref[...] = jnp.sum(dsp_acc[...], axis=1, keepdims=True)

    head_vec = pl.BlockSpec((None, 1, HEAD), lambda h, i: (h, 0, 0))
    head_mat = pl.BlockSpec((None, HEAD, HEAD), lambda h, i: (h, 0, 0))
    head_col = pl.BlockSpec((None, HEAD, 1), lambda h, i: (h, 0, 0))
    return pl.pallas_call(
        body, name=name, grid=(nh, s // tr),
        in_specs=[pl.BlockSpec((tr, HEAD), lambda h, i: (i, h)), pl.BlockSpec((tr, HEAD), lambda h, i: (i, nh + h)),
                  pl.BlockSpec((tr, HEAD), lambda h, i: (i, h)), head_vec, head_vec, head_mat, head_col],
        out_specs=[pl.BlockSpec((tr, HEAD), lambda h, i: (i, h)), pl.BlockSpec((tr, HEAD), lambda h, i: (i, h)),
                   head_vec, head_vec, head_mat, head_col],
        out_shape=[jax.ShapeDtypeStruct((s, GROUP), BF16), jax.ShapeDtypeStruct((s, GROUP), BF16),
                   jax.ShapeDtypeStruct((nh, 1, HEAD), F32), jax.ShapeDtypeStruct((nh, 1, HEAD), F32),
                   jax.ShapeDtypeStruct((nh, HEAD, HEAD), F32), jax.ShapeDtypeStruct((nh, HEAD, 1), F32)],
        scratch_shapes=[pltpu.VMEM((HEAD, HEAD), F32)],
        compiler_params=_cp(("parallel", "arbitrary")),
    )(z, z, d_o, ln_g, ln_b, w, b)


def _shortconv_fwd(name, z, w):
    s = z.shape[0]
    kw = w.shape[0]
    rows, nchunk = _chunks(s)
    nb = GROUP // HEAD

    def body(h_ref, bg_ref, cg_ref, w_ref, o_ref, pad):
        _zero_halo(pad, s)

        def fill(ci, carry):
            r0 = pl.multiple_of(ci * rows, rows)
            pad[pl.ds(pl.multiple_of(HALO + r0, 8), rows), :] = cg_ref[pl.ds(r0, rows), :] * h_ref[pl.ds(r0, rows), :]
            return carry

        lax.fori_loop(0, nchunk, fill, 0)

        def step(ci, carry):
            r0 = pl.multiple_of(ci * rows, rows)
            win = _window(pad, r0, rows)
            cv = jnp.zeros((rows, HEAD), F32)
            for k in range(kw):
                cv = cv + w_ref[k:k + 1, :] * _delayed(win, kw - 1 - k, rows)
            o_ref[pl.ds(r0, rows), :] = (bg_ref[pl.ds(r0, rows), :] * cv).astype(BF16)
            return carry

        lax.fori_loop(0, nchunk, step, 0)

    return pl.pallas_call(
        body, name=name, grid=(nb,),
        in_specs=[_col_spec(s, 8), _col_spec(s, 12), _col_spec(s, 16), pl.BlockSpec((kw, HEAD), lambda j: (0, j))],
        out_specs=_col_spec(s, 0),
        out_shape=jax.ShapeDtypeStruct((s, GROUP), BF16),
        scratch_shapes=[pltpu.VMEM((s + 2 * HALO, HEAD), F32)],
        compiler_params=_cp(("parallel",)),
    )(z, z, z, w)


def _shortconv_bwd(name, z, d_o, w):
    s = z.shape[0]
    kw = w.shape[0]
    rows, nchunk = _chunks(s)
    nb = GROUP // HEAD

    def body(h_ref, bg_ref, cg_ref, do_ref, w_ref, dh_ref, dbg_ref, dcg_ref, dw_ref, pad_q, pad_d, acc):
        _zero_halo(pad_q, s)
        _zero_halo(pad_d, s)
        acc[...] = jnp.zeros_like(acc)

        def fill(ci, carry):
            r0 = pl.multiple_of(ci * rows, rows)
            rs = pl.ds(r0, rows)
            ps = pl.ds(pl.multiple_of(HALO + r0, 8), rows)
            pad_q[ps, :] = cg_ref[rs, :] * h_ref[rs, :]
            pad_d[ps, :] = do_ref[rs, :].astype(F32) * bg_ref[rs, :]
            return carry

        lax.fori_loop(0, nchunk, fill, 0)

        def step(ci, carry):
            r0 = pl.multiple_of(ci * rows, rows)
            rs = pl.ds(r0, rows)
            wq = _window(pad_q, r0, rows)
            wd = _window(pad_d, r0, rows)
            dcv = wd[HALO:HALO + rows]
            cv = jnp.zeros((rows, HEAD), F32)
            dq = jnp.zeros((rows, HEAD), F32)
            for k in range(kw):
                qk = _delayed(wq, kw - 1 - k, rows)
                cv = cv + w_ref[k:k + 1, :] * qk
                dq = dq + w_ref[k:k + 1, :] * _advanced(wd, kw - 1 - k, rows)
                acc[k] += _fold8(dcv * qk)
            dbg_ref[rs, :] = (do_ref[rs, :].astype(F32) * cv).astype(BF16)
            dcg_ref[rs, :] = (dq * h_ref[rs, :]).astype(BF16)
            dh_ref[rs, :] = (dq * cg_ref[rs, :]).astype(BF16)
            return carry

        lax.fori_loop(0, nchunk, step, 0)
        for k in range(kw):
            dw_ref[k:k + 1, :] = jnp.sum(acc[k], axis=0, keepdims=True)

    col = _col_spec(s, 0)
    return pl.pallas_call(
        body, name=name, grid=(nb,),
        in_specs=[_col_spec(s, 8), _col_spec(s, 12), _col_spec(s, 16), col, pl.BlockSpec((kw, HEAD), lambda j: (0, j))],
        out_specs=[col, col, col, pl.BlockSpec((kw, HEAD), lambda j: (0, j))],
        out_shape=[jax.ShapeDtypeStruct((s, GROUP), BF16)] * 3 + [jax.ShapeDtypeStruct((kw, GROUP), F32)],
        scratch_shapes=[pltpu.VMEM((s + 2 * HALO, HEAD), F32), pltpu.VMEM((s + 2 * HALO, HEAD), F32),
                        pltpu.VMEM((kw, 8, HEAD), F32)],
        compiler_params=_cp(("parallel",)),
    )(z, z, z, d_o, w)


def _conformer_conv_fwd(name, z, w, bias):
    s = z.shape[0]
    kw = w.shape[0]
    rows, nchunk = _chunks(s)
    nb = GROUP // HEAD

    def body(a_ref, g_ref, w_ref, b_ref, o_ref, pad):
        _zero_halo(pad, s)

        def fill(ci, carry):
            r0 = pl.multiple_of(ci * rows, rows)
            rs = pl.ds(r0, rows)
            pad[pl.ds(pl.multiple_of(HALO + r0, 8), rows), :] = a_ref[rs, :] * _sigmoid(g_ref[rs, :])
            return carry

        lax.fori_loop(0, nchunk, fill, 0)

        def step(ci, carry):
            r0 = pl.multiple_of(ci * rows, rows)
            win = _window(pad, r0, rows)
            cc = jnp.zeros((rows, HEAD), F32)
            for k in range(kw):
                cc = cc + w_ref[k:k + 1, :] * _delayed(win, kw - 1 - k, rows)
            o_ref[pl.ds(r0, rows), :] = cc + b_ref[...]
            return carry

        lax.fori_loop(0, nchunk, step, 0)

    return pl.pallas_call(
        body, name=name, grid=(nb,),
        in_specs=[_col_spec(s, 20), _col_spec(s, 24), pl.BlockSpec((kw, HEAD), lambda j: (0, j)),
                  pl.BlockSpec((1, HEAD), lambda j: (0, j))],
        out_specs=_col_spec(s, 0),
        out_shape=jax.ShapeDtypeStruct((s, GROUP), F32),
        scratch_shapes=[pltpu.VMEM((s + 2 * HALO, HEAD), F32)],
        compiler_params=_cp(("parallel",)),
    )(z, z, w, bias)


def _ln_rows(cc, g, b):
    mu = jnp.mean(cc, axis=-1, keepdims=True)
    xc = cc - mu
    var = jnp.mean(xc * xc, axis=-1, keepdims=True)
    rstd = lax.rsqrt(var + EPS)
    xh = xc * rstd
    return xh, rstd, xh * g + b


def _conformer_ln_fwd(name, cc, g, b):
    s, d = cc.shape
    tr = _row_tile(s, 512)

    def body(c_ref, g_ref, b_ref, o_ref):
        _, _, l = _ln_rows(c_ref[...], g_ref[...], b_ref[...])
        o_ref[...] = (l * _sigmoid(l)).astype(BF16)

    row = pl.BlockSpec((tr, d), lambda i: (i, 0))
    vec = pl.BlockSpec((1, d), lambda i: (0, 0))
    return pl.pallas_call(
        body, name=name, grid=(s // tr,), in_specs=[row, vec, vec], out_specs=row,
        out_shape=jax.ShapeDtypeStruct((s, d), BF16), compiler_params=_cp(("parallel",)),
    )(cc, g, b)


def _conformer_ln_bwd(name, cc, d_o, g, b):
    s, d = cc.shape
    tr = _row_tile(s, 512)

    def body(c_ref, do_ref, g_ref, b_ref, dcc_ref, dg_ref, db_ref, dcb_ref):
        xh, rstd, l = _ln_rows(c_ref[...], g_ref[...], b_ref[...])
        sg = _sigmoid(l)
        dl = do_ref[...].astype(F32) * sg * (1.0 + l * (1.0 - sg))
        dxh = dl * g_ref[...]
        dcc = rstd * (dxh - jnp.mean(dxh, axis=-1, keepdims=True) - xh * jnp.mean(dxh * xh, axis=-1, keepdims=True))
        dcc_ref[...] = dcc

        @pl.when(pl.program_id(0) == 0)
        def _():
            dg_ref[...] = jnp.zeros_like(dg_ref)
            db_ref[...] = jnp.zeros_like(db_ref)
            dcb_ref[...] = jnp.zeros_like(dcb_ref)

        dg_ref[...] += jnp.sum(dl * xh, axis=0, keepdims=True)
        db_ref[...] += jnp.sum(dl, axis=0, keepdims=True)
        dcb_ref[...] += jnp.sum(dcc, axis=0, keepdims=True)

    row = pl.BlockSpec((tr, d), lambda i: (i, 0))
    vec = pl.BlockSpec((1, d), lambda i: (0, 0))
    return pl.pallas_call(
        body, name=name, grid=(s // tr,), in_specs=[row, row, vec, vec], out_specs=[row, vec, vec, vec],
        out_shape=[jax.ShapeDtypeStruct((s, d), F32)] + [jax.ShapeDtypeStruct((1, d), F32)] * 3,
        compiler_params=_cp(("arbitrary",)),
    )(cc, d_o, g, b)


def _conformer_conv_bwd(name, z, dcc, w):
    s = z.shape[0]
    kw = w.shape[0]
    rows, nchunk = _chunks(s)
    nb = GROUP // HEAD

    def body(a_ref, g_ref, d_ref, w_ref, da_ref, dg_ref, dw_ref, pad_h, pad_d, acc):
        _zero_halo(pad_h, s)
        _zero_halo(pad_d, s)
        acc[...] = jnp.zeros_like(acc)

        def fill(ci, carry):
            r0 = pl.multiple_of(ci * rows, rows)
            rs = pl.ds(r0, rows)
            ps = pl.ds(pl.multiple_of(HALO + r0, 8), rows)
            pad_h[ps, :] = a_ref[rs, :] * _sigmoid(g_ref[rs, :])
            pad_d[ps, :] = d_ref[rs, :]
            return carry

        lax.fori_loop(0, nchunk, fill, 0)

        def step(ci, carry):
            r0 = pl.multiple_of(ci * rows, rows)
            rs = pl.ds(r0, rows)
            wh = _window(pad_h, r0, rows)
            wd = _window(pad_d, r0, rows)
            dcc_c = wd[HALO:HALO + rows]
            dhc = jnp.zeros((rows, HEAD), F32)
            for k in range(kw):
                dhc = dhc + w_ref[k:k + 1, :] * _advanced(wd, kw - 1 - k, rows)
                acc[k] += _fold8(dcc_c * _delayed(wh, kw - 1 - k, rows))
            sg = _sigmoid(g_ref[rs, :])
            da_ref[rs, :] = (dhc * sg).astype(BF16)
            dg_ref[rs, :] = (dhc * a_ref[rs, :] * sg * (1.0 - sg)).astype(BF16)
            return carry

        lax.fori_loop(0, nchunk, step, 0)
        for k in range(kw):
            dw_ref[k:k + 1, :] = jnp.sum(acc[k], axis=0, keepdims=True)

    col = _col_spec(s, 0)
    return pl.pallas_call(
        body, name=name, grid=(nb,),
        in_specs=[_col_spec(s, 20), _col_spec(s, 24), col, pl.BlockSpec((kw, HEAD), lambda j: (0, j))],
        out_specs=[col, col, pl.BlockSpec((kw, HEAD), lambda j: (0, j))],
        out_shape=[jax.ShapeDtypeStruct((s, GROUP), BF16)] * 2 + [jax.ShapeDtypeStruct((kw, GROUP), F32)],
        scratch_shapes=[pltpu.VMEM((s + 2 * HALO, HEAD), F32), pltpu.VMEM((s + 2 * HALO, HEAD), F32),
                        pltpu.VMEM((kw, 8, HEAD), F32)],
        compiler_params=_cp(("parallel",)),
    )(z, z, dcc, w)


def _pool_window_sum(win, level, rows, shift):
    n = win.shape[0]

    def moved(v, k):
        return pltpu.roll(v, k if shift is _delayed else n - k, axis=0)

    s2 = win + moved(win, 1)
    s4 = s2 + moved(s2, 2)
    s8 = s4 + moved(s4, 4)
    s16 = s8 + moved(s8, 8)
    sel = jnp.where(level == 0, s2, jnp.where(level == 1, s4, jnp.where(level == 2, s8, s16)))
    return sel[HALO:HALO + rows]


def _pool_count(level, r0, rows):
    t = r0 + lax.broadcasted_iota(jnp.int32, (rows, 1), 0)
    width = jnp.left_shift(jnp.int32(2), level)
    return jnp.minimum(t + 1, width).astype(F32)


def _pool_fwd(name, z, pool_w, scale):
    s = z.shape[0]
    rows, nchunk = _chunks(s)

    def body(z_ref, w_ref, sc_ref, o_ref, pad):
        level = pl.program_id(0)
        _zero_halo(pad, s)

        def fill(ci, carry):
            r0 = pl.multiple_of(ci * rows, rows)
            pad[pl.ds(pl.multiple_of(HALO + r0, 8), rows), :] = z_ref[pl.ds(r0, rows), :]
            return carry

        lax.fori_loop(0, nchunk, fill, 0)
        wb = w_ref[...].astype(BF16)

        def step(ci, carry):
            r0 = pl.multiple_of(ci * rows, rows)
            win = _window(pad, r0, rows)
            pm = _pool_window_sum(win, level, rows, _delayed) / _pool_count(level, r0, rows) - win[HALO:HALO + rows]
            r = jnp.dot(pm.astype(BF16), wb, preferred_element_type=F32)
            o_ref[pl.ds(r0, rows), :] = (r * sc_ref[...]).astype(BF16)
            return carry

        lax.fori_loop(0, nchunk, step, 0)

    return pl.pallas_call(
        body, name=name, grid=(POOL_LEVELS,),
        in_specs=[_col_spec(s, 28), pl.BlockSpec((None, HEAD, HEAD), lambda j: (j, 0, 0)),
                  pl.BlockSpec((1, HEAD), lambda j: (0, j))],
        out_specs=_col_spec(s, 0),
        out_shape=jax.ShapeDtypeStruct((s, GROUP), BF16),
        scratch_shapes=[pltpu.VMEM((s + 2 * HALO, HEAD), F32)],
        compiler_params=_cp(("parallel",)),
    )(z, pool_w, scale)


def _pool_bwd(name, z, d_o, pool_w, scale):
    s = z.shape[0]
    rows, nchunk = _chunks(s)

    def body(z_ref, do_ref, w_ref, sc_ref, dz_ref, dw_ref, dsc_ref, pad, pad_q, dw_acc, dsc_acc):
        level = pl.program_id(0)
        _zero_halo(pad, s)
        _zero_halo(pad_q, s)
        dw_acc[...] = jnp.zeros_like(dw_acc)
        dsc_acc[...] = jnp.zeros_like(dsc_acc)

        def fill(ci, carry):
            r0 = pl.multiple_of(ci * rows, rows)
            pad[pl.ds(pl.multiple_of(HALO + r0, 8), rows), :] = z_ref[pl.ds(r0, rows), :]
            return carry

        lax.fori_loop(0, nchunk, fill, 0)
        wb = w_ref[...].astype(BF16)

        def first(ci, carry):
            r0 = pl.multiple_of(ci * rows, rows)
            win = _window(pad, r0, rows)
            cnt = _pool_count(level, r0, rows)
            pm = (_pool_window_sum(win, level, rows, _delayed) / cnt - win[HALO:HALO + rows]).astype(BF16)
            r = jnp.dot(pm, wb, preferred_element_type=F32)
            d_od = do_ref[pl.ds(r0, rows), :].astype(F32)
            dsc_acc[...] += _fold8(d_od * r)
            dr = (d_od * sc_ref[...]).astype(BF16)
            dw_acc[...] += lax.dot_general(pm, dr, TN, preferred_element_type=F32)
            dpm = lax.dot_general(dr, wb, NT, preferred_element_type=F32)
            pad_q[pl.ds(pl.multiple_of(HALO + r0, 8), rows), :] = dpm / cnt
            return carry

        lax.fori_loop(0, nchunk, first, 0)

        def second(ci, carry):
            r0 = pl.multiple_of(ci * rows, rows)
            wq = _window(pad_q, r0, rows)
            dpm = wq[HALO:HALO + rows] * _pool_count(level, r0, rows)
            dz_ref[pl.ds(r0, rows), :] = (_pool_window_sum(wq, level, rows, _advanced) - dpm).astype(BF16)
            return carry

        lax.fori_loop(0, nchunk, second, 0)
        dw_ref[...] = dw_acc[...]
        dsc_ref[...] = jnp.sum(dsc_acc[...], axis=0, keepdims=True)

    col = _col_spec(s, 0)
    mat = pl.BlockSpec((None, HEAD, HEAD), lambda j: (j, 0, 0))
    vec = pl.BlockSpec((1, HEAD), lambda j: (0, j))
    return pl.pallas_call(
        body, name=name, grid=(POOL_LEVELS,),
        in_specs=[_col_spec(s, 28), col, mat, vec], out_specs=[col, mat, vec],
        out_shape=[jax.ShapeDtypeStruct((s, GROUP), BF16), jax.ShapeDtypeStruct((POOL_LEVELS, HEAD, HEAD), F32),
                   jax.ShapeDtypeStruct((1, GROUP), F32)],
        scratch_shapes=[pltpu.VMEM((s + 2 * HALO, HEAD), F32), pltpu.VMEM((s + 2 * HALO, HEAD), F32),
                        pltpu.VMEM((HEAD, HEAD), F32), pltpu.VMEM((8, HEAD), F32)],
        compiler_params=_cp(("parallel",)),
    )(z, d_o, pool_w, scale)


def _mm_residual(name, x, wg, h, tm, tn):
    m, d = h.shape
    tm = min(tm, m)

    def ep(accs, ins, outs, cols):
        outs[0][:, cols] = ins[5][:, cols] + accs[0]

    tile = pl.BlockSpec((tm, tn), lambda i, j: (i, j))
    return _mm_nn_wide(name, x, wg, "row", tm, tn, epilogue=ep, extra=[h], extra_specs=[tile], out_specs=[tile],
                       out_shape=[jax.ShapeDtypeStruct((m, d), F32)])[0]


def _swiglu_fwd(name, y, wg_gate, wg_up, tm):
    m, kdim = y.shape
    _, a, b = wg_gate.shape
    tm = min(tm, m)

    def ep(accs, ins, outs, cols):
        gt, up = accs
        outs[0][:, cols] = (gt * _sigmoid(gt) * up).astype(BF16)
        outs[1][:, cols] = gt.astype(BF16)
        outs[2][:, cols] = up.astype(BF16)

    gws, gspecs, gdots = _wide_nn_weights(wg_gate, "col", b, 1)
    uws, uspecs, udots = _wide_nn_weights(wg_up, "col", b, 1 + len(gws))
    out = pl.BlockSpec((tm, b), lambda i, j: (i, j))
    return _matmul_wide(name, (m // tm, 4), [y] + gws + uws,
                        [pl.BlockSpec((tm, kdim), lambda i, j: (i, 0))] + gspecs + uspecs, [gdots, udots], b, ep,
                        [out] * 3, [jax.ShapeDtypeStruct((m, 4 * b), BF16)] * 3)


def _swiglu_bwd(name, dh, wg_down, gate, up, tm):
    m, n = dh.shape
    _, a, b = wg_down.shape
    tm = min(tm, m)

    def ep(accs, ins, outs, cols):
        d_act = accs[0]
        gt = ins[3][:, cols].astype(F32)
        upv = ins[4][:, cols].astype(F32)
        sg = _sigmoid(gt)
        outs[0][:, cols] = (d_act * upv * sg * (1.0 + gt * (1.0 - sg))).astype(BF16)
        outs[1][:, cols] = (d_act * gt * sg).astype(BF16)
        outs[2][:, cols] = (gt * sg * upv).astype(BF16)

    ws, wspecs, dots = _wide_nt_weights(wg_down, "row", a, 1)
    tile = pl.BlockSpec((tm, a), lambda i, j: (i, j))
    return _matmul_wide(name, (m // tm, 4), [dh] + ws + [gate, up],
                        [pl.BlockSpec((tm, n), lambda i, j: (i, 0))] + wspecs + [tile, tile], [dots], a, ep,
                        [tile] * 3, [jax.ShapeDtypeStruct((m, 4 * a), BF16)] * 3)


def _ffn_dy(name, d_gate, d_up, wg_gate, wg_up, tiles, after=()):
    m, n = d_gate.shape
    _, a, b = wg_gate.shape
    kdim = 2 * a
    tm, to, tc = tiles
    tm = min(tm, m)
    grid = (m // tm, kdim // to, n // tc)
    lhs = pl.BlockSpec((tm, tc), lambda i, j, k: (i, k))
    wspec = _w_spec_nt(wg_gate, "col", to, tc)
    return _matmul(name, grid, [d_gate, d_up, wg_gate, wg_up], [lhs, lhs, wspec, wspec],
                   [(0, 2, 0, NT), (1, 3, 0, NT)], 1, (tm, to), _ep_store(BF16),
                   [pl.BlockSpec((tm, to), lambda i, j, k: (i, j))], [jax.ShapeDtypeStruct((m, kdim), BF16)],
                   after=after)[0]


def _ple_fwd(name, y, wg, h, pp, tm, tn, after=()):
    m, d = h.shape
    tm = min(tm, m)

    def ep(accs, ins, outs, cols):
        pg = accs[0]
        outs[0][:, cols] = ins[5][:, cols] + _sigmoid(pg) * ins[6][:, cols].astype(F32)
        outs[1][:, cols] = pg.astype(BF16)

    tile = pl.BlockSpec((tm, tn), lambda i, j: (i, j))
    return _mm_nn_wide(name, y, wg, "row", tm, tn, epilogue=ep, extra=[h, pp], extra_specs=[tile, tile],
                       out_specs=[tile, tile],
                       out_shape=[jax.ShapeDtypeStruct((m, d), F32), jax.ShapeDtypeStruct((m, d), BF16)], after=after)


def _ple_bwd(name, dh, pg, pp, after=()):
    s, d = dh.shape
    tr = _row_tile(s, 512)

    def body(dh_ref, pg_ref, pp_ref, *rest):
        dpp_ref, dpg_ref = rest[-2:]
        dhv = dh_ref[...]
        sg = _sigmoid(pg_ref[...].astype(F32))
        dpp_ref[...] = (dhv * sg).astype(BF16)
        dpg_ref[...] = (dhv * pp_ref[...].astype(F32) * sg * (1.0 - sg)).astype(BF16)

    row = pl.BlockSpec((tr, d), lambda i: (i, 0))
    return pl.pallas_call(
        body, name=name, grid=(s // tr,), in_specs=[row] * 3 + [ANY_SPEC] * len(after), out_specs=[row] * 2,
        out_shape=[jax.ShapeDtypeStruct((s, d), BF16)] * 2, compiler_params=_cp(("parallel",)),
    )(dh, pg, pp, *after)


BIG = ["w_in", "w_out", "w_gate", "w_up", "w_down", "w_ple_gate", "w_ple_proj"]
KIND = {"w_in": "col", "w_out": "row", "w_gate": "col", "w_up": "col", "w_down": "row", "w_ple_gate": "row",
        "w_ple_proj": "col"}
GATHER_GROUPS = (("w_in", "w_out"), ("w_gate", "w_up"), ("w_down", "w_ple_gate", "w_ple_proj"))
RS_GROUPS = (("w_ple_gate", "w_ple_proj", "w_down", "w_gate", "w_up"), ("w_out", "w_in"))
SMALL = ["norm_mix_g", "sgu_ln_g", "sgu_ln_b", "sgu_w", "sgu_b", "sc_conv_w", "cf_conv_w", "cf_conv_b", "cf_ln_g",
         "cf_ln_b", "pool_w", "pool_scale", "norm_ffn_g", "norm_ple_g", "final_norm_g"]
CHIP_SPLIT = ["sc_conv_w", "cf_conv_w"]
WEIGHTS = ['norm_mix_g', 'w_in', 'sgu_ln_g', 'sgu_ln_b', 'sgu_w', 'sgu_b', 'sc_conv_w', 'cf_conv_w', 'cf_conv_b',
           'cf_ln_g', 'cf_ln_b', 'pool_w', 'pool_scale', 'w_out', 'norm_ffn_g', 'w_gate', 'w_up', 'w_down',
           'norm_ple_g', 'w_ple_gate', 'w_ple_proj', 'final_norm_g']


def _tile(n, want):
    if n <= want:
        return n
    t = (want // 128) * 128
    while n % t:
        t -= 128
    return t


def _pack_rows(vecs):
    flat = jnp.concatenate([v.reshape(-1) for v in vecs])
    n = flat.shape[0]
    quantum = PACK_ROWS * 128
    padded = ((n + quantum - 1) // quantum) * quantum
    return jnp.pad(flat, (0, padded - n)).reshape(padded // 128, 128), n


def _unpack(flat, shapes):
    out, off = [], 0
    for shp in shapes:
        size = math.prod(shp)
        out.append(flat[off:off + size].reshape(shp))
        off += size
    return out


def kernel(x, p, norm_mix_g, w_in, sgu_ln_g, sgu_ln_b, sgu_w, sgu_b, sc_conv_w, cf_conv_w, cf_conv_b, cf_ln_g, cf_ln_b, pool_w, pool_scale, w_out, norm_ffn_g, w_gate, w_up, w_down, norm_ple_g, w_ple_gate, w_ple_proj, final_norm_g, loss_target, m_norm_mix_g, m_w_in, m_sgu_ln_g, m_sgu_ln_b, m_sgu_w, m_sgu_b, m_sc_conv_w, m_cf_conv_w, m_cf_conv_b, m_cf_ln_g, m_cf_ln_b, m_pool_w, m_pool_scale, m_w_out, m_norm_ffn_g, m_w_gate, m_w_up, m_w_down, m_norm_ple_g, m_w_ple_gate, m_w_ple_proj, m_final_norm_g, v_norm_mix_g, v_w_in, v_sgu_ln_g, v_sgu_ln_b, v_sgu_w, v_sgu_b, v_sc_conv_w, v_cf_conv_w, v_cf_conv_b, v_cf_ln_g, v_cf_ln_b, v_pool_w, v_pool_scale, v_w_out, v_norm_ffn_g, v_w_gate, v_w_up, v_w_down, v_norm_ple_g, v_w_ple_gate, v_w_ple_proj, v_final_norm_g):
    args = dict(locals())
    w = {n: args[n] for n in WEIGHTS}
    mom = {n: args["m_" + n] for n in WEIGHTS}
    var = {n: args["v_" + n] for n in WEIGHTS}
    depth = w_in.shape[0]
    s, d = x.shape[1], x.shape[2]
    f_dim = 4 * w_gate.shape[2]
    xi, yi, ci = lax.axis_index("x"), lax.axis_index("y"), lax.axis_index("c")
    c_idx = ci.astype(jnp.int32).reshape(1)

    chip_idx = (2 * xi + yi).astype(jnp.int32).reshape(1)

    def start_gathers(l, dep):
        pending = []
        for gi, names in enumerate(GATHER_GROUPS):
            lands = [_cast_into_landing(f"cast_{n}", w[n], l, 0 if KIND[n] == "col" else 1, chip_idx) for n in names]
            pending.append(_gather_start(f"gather_start_{l}_{gi}", lands, dep))
            dep = pending[-1][3]
        return pending, dep

    def receive(l, gi, pending, after):
        send_sems, recv_sems, lands, _ = pending
        lands = _gather_wait(f"gather_wait_{l}_{gi}", send_sems, recv_sems, lands, after)
        return _share_start(f"gather_share_start_{l}_{gi}", lands)

    def complete(l, gi, share, after):
        send_sems, recv_sems, lands, _ = share
        lands = _share_wait(f"gather_share_wait_{l}_{gi}", send_sems, recv_sems, lands, after)
        return dict(zip(GATHER_GROUPS[gi], lands))

    conv_pack = jnp.concatenate([sc_conv_w, cf_conv_w], axis=1)
    taps = conv_pack.shape[1]
    rows_pad = ((depth * taps + 7) // 8) * 8
    conv_rows = jnp.pad(conv_pack.reshape(depth * taps, HEAD), ((0, rows_pad - depth * taps), (0, 0)))
    conv_all = _allgather8("gather_conv_weights", [conv_rows])[0]
    conv_full = conv_all[0::2, :depth * taps].reshape(4, depth, taps, HEAD)
    conv_full = jnp.transpose(conv_full, (1, 2, 0, 3)).reshape(depth, taps, GROUP)
    sc_w_full, cf_w_full = conv_full[:, :3], conv_full[:, 3:]

    pending, token = start_gathers(0, conv_all)
    h = x[0]
    saved = []
    gathered = []
    shares = [None] * len(GATHER_GROUPS)
    for l in range(depth):
        just_in_time = l < 2
        ahead = l + 1 < depth and l + 1 >= 2
        if just_in_time:
            shares[0] = receive(l, 0, pending[0], token if l == 0 else h)
        wg = complete(l, 0, shares[0], h)
        gathered.append(wg)
        started = ()
        this_layer = pending
        if l + 1 < depth:
            pending, token = start_gathers(l + 1, wg["w_in"])
            started = (token,)
        sv = {"h0": h}
        y1 = _rms_fwd("rms_mix", h, norm_mix_g[l:l + 1], after=started)
        z = _mm_nn_wide("mm_in", y1, wg["w_in"], "col", 1024, 1024)[0]
        lg, lb = sgu_ln_g[l][:, None, :], sgu_ln_b[l][:, None, :]
        sb = sgu_b[l][:, :, None]
        oa = _sgu_fwd("sgu_fwd", z, lg, lb, sgu_w[l], sb)
        ob = _shortconv_fwd("shortconv_fwd", z, sc_w_full[l])
        cc = _conformer_conv_fwd("conformer_conv_fwd", z, cf_w_full[l], cf_conv_b[l:l + 1])
        oc = _conformer_ln_fwd("conformer_ln_fwd", cc, cf_ln_g[l:l + 1], cf_ln_b[l:l + 1])
        od = _pool_fwd("pool_fwd", z, pool_w[l], pool_scale[l:l + 1])
        o = jnp.concatenate([oa, ob, oc, od], axis=1)
        h1 = _mm_residual("mm_out", o, wg["w_out"], h, 1024, 1024)
        if just_in_time:
            shares[1] = receive(l, 1, this_layer[1], h1)
        wg.update(complete(l, 1, shares[1], h1))
        y2 = _rms_fwd("rms_ffn", h1, norm_ffn_g[l:l + 1])
        act, gt, up = _swiglu_fwd("mm_swiglu", y2, wg["w_gate"], wg["w_up"], 512)
        if just_in_time:
            shares[2] = receive(l, 2, this_layer[2], act)
        wg.update(complete(l, 2, shares[2], act))
        h2 = _mm_residual("mm_down", act, wg["w_down"], h1, 1024, 512)
        started = ()
        if ahead:
            shares[0] = receive(l + 1, 0, pending[0], h2)
            shares[1] = receive(l + 1, 1, pending[1], shares[0][3])
            started = (shares[1][3],)
        y3 = _rms_fwd("rms_ple", h2, norm_ple_g[l:l + 1], after=started)
        pb = p[l, 0].astype(BF16)
        ptile = pl.BlockSpec((min(1024, s), 512), lambda i, j: (i, j))
        pp = _mm_nn_wide("mm_ple_proj", pb, wg["w_ple_proj"], "col", 1024, 512, epilogue=_wide_store(BF16),
                         out_specs=[ptile], out_shape=[jax.ShapeDtypeStruct((s, d), BF16)])[0]
        started = ()
        if ahead:
            shares[2] = receive(l + 1, 2, pending[2], pp)
            started = (shares[2][3],)
        h3, pg = _ple_fwd("mm_ple_gate", y3, wg["w_ple_gate"], h2, pp, 1024, 1024, after=started)
        sv.update(y1=y1, z=z, cc=cc, o=o, h1=h1, y2=y2, gt=gt, up=up, h2=h2, y3=y3, pb=pb, pp=pp, pg=pg)
        saved.append(sv)
        h = h3

    loss_part, dh, dhb, d_final_g = _loss_head("loss_head", h, final_norm_g[None, :], loss_target[0])

    small_grads = [None] * depth
    where = jnp.stack([2 * xi + yi, 2 * (1 - xi) + yi, 2 * xi + (1 - yi), 2 * (1 - xi) + (1 - yi), ci]).astype(jnp.int32)
    grad_bufs = {n: lax.empty((depth, 2, w[n].shape[1] // 2, w[n].shape[2]), F32) for n in BIG}
    exchanges = [None] * len(RS_GROUPS)
    joins = []
    behind_join = ()

    def halves(g):
        return g.reshape(4, 2, g.shape[1] // 2, g.shape[2])

    def start_exchange(layer, gi, sibling, after):
        send_sems, recv_sems, gs, lands, _ = sibling
        gs, lands = _sibling_wait(f"rs_sibling_wait_{layer}_{gi}", send_sems, recv_sems, gs, lands, after)
        chip_sums = _add_halves("rs_add", gs, lands, c_idx)
        return _chip_exchange_start(f"rs_chips_start_{layer}_{gi}", chip_sums)

    def finish_exchange(layer, gi, after):
        send_sems, recv_sems, parts, lands, _ = exchanges[gi]
        parts, lands = _chip_exchange_wait(f"rs_chips_wait_{layer}_{gi}", send_sems, recv_sems, parts, lands, after)
        names = RS_GROUPS[gi]
        sums = _sum_parts("rs_sum", parts, lands, [grad_bufs[n] for n in names], layer, where)
        grad_bufs.update(zip(names, sums))

    for l in reversed(range(depth)):
        wg = gathered[l]
        sv = saved[l]
        fs = f_dim // 4
        started = () if exchanges[1] is None else (exchanges[1][4],)
        d_pp, d_pg = _ple_bwd("ple_bwd", dh, sv["pg"], sv["pp"], after=started + behind_join)
        g_ple_proj = _mm_tn_wide("dw_ple_proj", sv["pb"], d_pp, "col", w_ple_proj.shape[1], 512)
        g_ple_gate = _mm_tn_wide("dw_ple_gate", sv["y3"], d_pg, "row", 512, 1024)
        dy3 = _mm_nt_wide("dx_ple_gate", d_pg, wg["w_ple_gate"], "row", 1024, 512)
        dh, dhb, dg_ple = _rms_bwd("rms_ple_bwd", sv["h2"], norm_ple_g[l:l + 1], dy3, dh)

        d_gt, d_up, act = _swiglu_bwd("dx_down_swiglu", dhb, wg["w_down"], sv["gt"], sv["up"], 512)
        g_down = _mm_tn_wide("dw_down", act, dhb, "row", fs, 512)
        g_gate = _mm_tn_wide("dw_gate", sv["y2"], d_gt, "col", 512, fs)
        g_up = _mm_tn_wide("dw_up", sv["y2"], d_up, "col", 512, fs)
        big = dict(w_gate=g_gate, w_up=g_up, w_down=g_down, w_ple_gate=g_ple_gate, w_ple_proj=g_ple_proj)
        sibling = _sibling_start(f"rs_sibling_start_{l}_0", [halves(big[n]) for n in RS_GROUPS[0]])
        dy2 = _ffn_dy("dx_gate_up", d_gt, d_up, wg["w_gate"], wg["w_up"], (1024, 1024, fs), after=(sibling[4],))
        dh, dhb, dg_ffn = _rms_bwd("rms_ffn_bwd", sv["h1"], norm_ffn_g[l:l + 1], dy2, dh)

        g_out = _mm_tn_wide("dw_out", sv["o"], dhb, "row", 512, 1024)
        if exchanges[0] is not None:
            finish_exchange(l + 1, 0, g_out)
        exchanges[0] = start_exchange(l, 0, sibling, g_out)
        d_o = _mm_nt_wide("dx_out", dhb, wg["w_out"], "row", 1024, 512, after=(exchanges[0][4],))
        z = sv["z"]
        lg, lb = sgu_ln_g[l][:, None, :], sgu_ln_b[l][:, None, :]
        sb = sgu_b[l][:, :, None]
        dzu, dzv, d_lg, d_lb, d_sw, d_sb = _sgu_bwd("sgu_bwd", z, d_o[:, 0:GROUP], lg, lb, sgu_w[l], sb)
        dzh, dzbg, dzcg, d_scw = _shortconv_bwd("shortconv_bwd", z, d_o[:, GROUP:2 * GROUP], sc_w_full[l])
        dcc, d_cflg, d_cflb, d_cfb = _conformer_ln_bwd("conformer_ln_bwd", sv["cc"], d_o[:, 2 * GROUP:3 * GROUP],
                                                       cf_ln_g[l:l + 1], cf_ln_b[l:l + 1])
        dza, dzg, d_cfw = _conformer_conv_bwd("conformer_conv_bwd", z, dcc, cf_w_full[l])
        dzd, d_pw, d_psc = _pool_bwd("pool_bwd", z, d_o[:, 3 * GROUP:], pool_w[l], pool_scale[l:l + 1])
        dz = jnp.concatenate([dzu, dzv, dzh, dzbg, dzcg, dza, dzg, dzd], axis=1)
        g_in = _mm_tn_wide("dw_in", sv["y1"], dz, "col", 512, 1024)
        big.update(w_out=g_out, w_in=g_in)
        sibling = _sibling_start(f"rs_sibling_start_{l}_1", [halves(big[n]) for n in RS_GROUPS[1]])
        dy1 = _mm_nt_wide("dx_in", dz, wg["w_in"], "col", 1024, 512, after=(sibling[4],))
        dh, dhb, dg_mix = _rms_bwd("rms_mix_bwd", sv["h0"], norm_mix_g[l:l + 1], dy1, dh)
        if exchanges[1] is not None:
            finish_exchange(l + 1, 1, dh)
            send_sems, recv_sems, bufs, join_token = _join_start(f"rs_join_start_{l + 1}",
                                                                 [grad_bufs[n] for n in BIG], l + 1)
            grad_bufs.update(zip(BIG, bufs))
            joins.append((l + 1, send_sems, recv_sems))
            behind_join = (join_token,)
        exchanges[1] = start_exchange(l, 1, sibling, dh)

        small_grads[l] = dict(norm_mix_g=dg_mix, sgu_ln_g=d_lg, sgu_ln_b=d_lb, sgu_w=d_sw, sgu_b=d_sb,
                              sc_conv_w=d_scw, cf_conv_w=d_cfw, cf_conv_b=d_cfb, cf_ln_g=d_cflg, cf_ln_b=d_cflb,
                              pool_w=d_pw, pool_scale=d_psc, norm_ffn_g=dg_ffn, norm_ple_g=dg_ple)
    grad_x = dh[None]

    grads, delta, new_m, new_v = {}, {}, {}, {}
    per_layer = [n for n in SMALL if n != "final_norm_g"]
    vecs = [small_grads[l][n] for n in per_layer for l in range(depth)] + [d_final_g, loss_part]
    packed, _ = _pack_rows(vecs)
    total = _sum8("sum_small", _allgather8("gather_small", [packed])[0]).reshape(-1)
    stacked_shapes = [(depth,) + (w[n].shape[1:] if n not in CHIP_SPLIT else (w[n].shape[1], GROUP)) for n in per_layer]
    pieces = _unpack(total, stacked_shapes + [(d,), ()])
    loss = pieces[-1]
    grads["final_norm_g"] = pieces[-2]
    chip_off = (2 * xi + yi) * HEAD
    for n, g in zip(per_layer, pieces):
        grads[n] = lax.dynamic_slice_in_dim(g, chip_off, HEAD, axis=2) if n in CHIP_SPLIT else g

    finish_exchange(0, 0, dh)
    for layer, send_sems, recv_sems in joins:
        bufs = _join_wait(f"rs_join_wait_{layer}", send_sems, recv_sems, [grad_bufs[n] for n in BIG], layer, total)
        grad_bufs.update(zip(BIG, bufs))
    behind = (exchanges[1][4],)
    for gi, names in enumerate(RS_GROUPS):
        if gi == 1:
            finish_exchange(0, 1, behind[0])
        joined = _sibling_join_halves(f"rs_join_{gi}", [grad_bufs[n] for n in names], 0, after=behind)
        for n, g in zip(names, joined):
            shp = w[n].shape
            two_d = (shp[0] * shp[1], shp[2])
            grads[n] = g.reshape(shp)
            dl, mn, vn = _adamw(f"adamw_{n}", w[n].reshape(two_d), g.reshape(two_d), mom[n].reshape(two_d),
                                var[n].reshape(two_d))
            delta[n], new_m[n], new_v[n] = dl.reshape(shp), mn.reshape(shp), vn.reshape(shp)
            behind = (dl,)
    small_shapes = [w[n].shape for n in SMALL]
    pw, _ = _pack_rows([w[n] for n in SMALL])
    pg_, _ = _pack_rows([grads[n] for n in SMALL])
    pm, _ = _pack_rows([mom[n] for n in SMALL])
    pv, _ = _pack_rows([var[n] for n in SMALL])
    dl, mn, vn = _adamw("adamw_small", pw, pg_, pm, pv)
    for n, a, b, cc_ in zip(SMALL, _unpack(dl.reshape(-1), small_shapes), _unpack(mn.reshape(-1), small_shapes),
                            _unpack(vn.reshape(-1), small_shapes)):
        delta[n], new_m[n], new_v[n] = a, b, cc_

    return (loss, grad_x, *[grads[n] for n in WEIGHTS], *[delta[n] for n in WEIGHTS],
            *[new_m[n] for n in WEIGHTS], *[new_v[n] for n in WEIGHTS])
```

```python
import functools
import math

import jax
import jax.numpy as jnp
from jax import lax
from jax.experimental import pallas as pl
from jax.experimental.pallas import tpu as pltpu

F32 = jnp.float32
BF16 = jnp.bfloat16
MESH = pl.DeviceIdType.MESH

HEAD = 128
GROUP = 512
EPS = 1e-6
HALO = 32
CHUNK_ROWS = 128
POOL_LEVELS = 4
PACK_ROWS = 512

ADAM_LR = 0.001
ADAM_B1 = 0.9
ADAM_B2 = 0.999
ADAM_EPS = 1e-08
ADAM_WD = 0.01
ADAM_STEP = 10

VMEM_LIMIT = 56 * 1024 * 1024


def _cp(sem=None, vmem=VMEM_LIMIT):
    return pltpu.CompilerParams(dimension_semantics=sem, vmem_limit_bytes=vmem)


def _sigmoid(x):
    return 0.5 * jnp.tanh(0.5 * x) + 0.5


_GELU_K = math.sqrt(2.0 / math.pi)
_GELU_C = 0.044715


def _gelu(x):
    t = jnp.tanh(_GELU_K * (x + _GELU_C * x * x * x))
    return 0.5 * x * (1.0 + t)


def _gelu_grad(x):
    t = jnp.tanh(_GELU_K * (x + _GELU_C * x * x * x))
    return 0.5 * (1.0 + t) + 0.5 * x * (1.0 - t * t) * _GELU_K * (1.0 + 3.0 * _GELU_C * x * x)


def _mesh_pos():
    return lax.axis_index("x"), lax.axis_index("y"), lax.axis_index("c")


def _any_specs(n):
    return [pl.BlockSpec(memory_space=pl.ANY)] * n


def _allgather8(name, blocks):
    n = len(blocks)

    def body(*refs):
        ins, outs = refs[:n], refs[n:2 * n]
        send_sems, recv_sems, local_sems = refs[2 * n:]
        x, y, c = _mesh_pos()
        me, sibling = (x, y, c), (x, y, 1 - c)
        chips = [(1 - x, y), (x, 1 - y), (1 - x, 1 - y)]

        def slot(a, dev):
            return outs[a].at[4 * dev[0] + 2 * dev[1] + dev[2]]

        def copy(a, k, block, to, src=None):
            dst = slot(a, block)
            return pltpu.make_async_remote_copy(
                src_ref=dst if src is None else src, dst_ref=dst,
                send_sem=send_sems.at[7 * a + k], recv_sem=recv_sems.at[7 * a + k],
                device_id=to, device_id_type=MESH)

        mine, first, passed = [], [], []
        for a in range(n):
            cp = pltpu.make_async_copy(ins[a], slot(a, me), local_sems.at[a])
            cp.start()
            mine.append(cp)
            cps = [copy(a, 0, me, sibling, src=ins[a])]
            cps += [copy(a, 1 + j, me, (*chip, c), src=ins[a]) for j, chip in enumerate(chips)]
            for cp in cps:
                cp.start()
            first += cps
        for j, chip in enumerate(chips):
            for a in range(n):
                copy(a, 1 + j, (*chip, c), me).wait_recv()
                cp = copy(a, 4 + j, (*chip, c), sibling)
                cp.start()
                passed.append(cp)
        for a in range(n):
            copy(a, 0, sibling, me).wait_recv()
            for j, chip in enumerate(chips):
                copy(a, 4 + j, (*chip, 1 - c), me).wait_recv()
        for cp in first + passed:
            cp.wait_send()
        for cp in mine:
            cp.wait()

    return pl.pallas_call(
        body, name=name,
        out_shape=[jax.ShapeDtypeStruct((8,) + b.shape, b.dtype) for b in blocks],
        in_specs=_any_specs(n), out_specs=_any_specs(n),
        scratch_shapes=[pltpu.SemaphoreType.DMA((7 * n,)), pltpu.SemaphoreType.DMA((7 * n,)),
                        pltpu.SemaphoreType.DMA((n,))],
    )(*blocks)


HBM_SPEC = pl.BlockSpec(memory_space=pltpu.HBM)
SEM_SPEC = pl.BlockSpec(memory_space=pltpu.SEMAPHORE)
ANY_SPEC = pl.BlockSpec(memory_space=pl.ANY)
SPLIT_COPY = pltpu.CompilerParams(has_side_effects=pltpu.SideEffectType.DATAFLOW_SIDE_EFFECTING)


def _hbm(x):
    return pltpu.with_memory_space_constraint(x, pltpu.HBM)


def _other_chips(x, y):
    return [(1 - x, y), (x, 1 - y), (1 - x, 1 - y)]


def _dev_slot(ref, dev):
    return ref.at[4 * dev[0] + 2 * dev[1] + dev[2]]


def _cast_into_landing(name, w, layer, ax, chip_idx):
    _, r, cc = w.shape
    r2, c2 = (r // 2, cc) if ax == 0 else (r, cc // 2)
    tr = _row_tile(r2, 512 if c2 <= 1536 else 256)
    nt = r2 // tr

    def body(chip_ref, w_ref, o_ref):
        o_ref[...] = w_ref[...].astype(BF16)

    if ax == 0:
        in_spec = pl.BlockSpec((None, tr, c2), lambda hf, i, chip_ref: (layer, hf * nt + i, 0))
    else:
        in_spec = pl.BlockSpec((None, tr, c2), lambda hf, i, chip_ref: (layer, i, hf))
    return pl.pallas_call(
        body, name=name,
        grid_spec=pltpu.PrefetchScalarGridSpec(
            num_scalar_prefetch=1, grid=(2, nt), in_specs=[in_spec],
            out_specs=pl.BlockSpec((None, tr, c2), lambda hf, i, chip_ref: (2 * chip_ref[0] + hf, i, 0))),
        out_shape=jax.ShapeDtypeStruct((8, r2, c2), BF16), compiler_params=_cp(("parallel", "parallel")),
    )(chip_idx, w)


def _gather_ici_copies(lands, send_sems, recv_sems):
    x, y, c = _mesh_pos()
    pairs = []
    for a in range(len(lands)):
        for j, chip in enumerate(_other_chips(x, y)):
            def copy(dev):
                return pltpu.make_async_remote_copy(
                    src_ref=_dev_slot(lands[a], dev), dst_ref=_dev_slot(lands[a], dev),
                    send_sem=send_sems.at[3 * a + j], recv_sem=recv_sems.at[3 * a + j],
                    device_id=(*chip, c), device_id_type=MESH)
            pairs.append((copy((x, y, c)), copy((*chip, c))))
    return pairs


def _exchange_ici_copies(ins, lands, send_sems, recv_sems):
    x, y, c = _mesh_pos()
    pairs = []
    for a in range(len(ins)):
        for k, chip in enumerate(_other_chips(x, y)):
            there = 2 * chip[0] + chip[1]
            def copy(dst_entry):
                return pltpu.make_async_remote_copy(
                    src_ref=ins[a].at[there], dst_ref=lands[a].at[dst_entry],
                    send_sem=send_sems.at[3 * a + k], recv_sem=recv_sems.at[3 * a + k],
                    device_id=(*chip, c), device_id_type=MESH)
            pairs.append((copy(2 * x + y), copy(there)))
    return pairs


def _gather_start(name, lands, dep):
    n = len(lands)

    def body(*refs):
        token = refs[-1]
        for send, _ in _gather_ici_copies(refs[:n], refs[n + 1], refs[n + 2]):
            send.start()
        token[...] = jnp.zeros_like(token)

    out = pl.pallas_call(
        body, name=name,
        out_shape=(pltpu.SemaphoreType.DMA((3 * n,)), pltpu.SemaphoreType.DMA((3 * n,)),
                   *[pltpu.HBM(l.shape, l.dtype) for l in lands], jax.ShapeDtypeStruct((8, 128), F32)),
        in_specs=[HBM_SPEC] * n + [ANY_SPEC],
        out_specs=(SEM_SPEC, SEM_SPEC, *[HBM_SPEC] * n, pl.BlockSpec(memory_space=pltpu.VMEM)),
        input_output_aliases={a: 2 + a for a in range(n)},
        compiler_params=SPLIT_COPY,
    )(*[_hbm(l) for l in lands], dep)
    return out[0], out[1], list(out[2:2 + n]), out[-1]


def _gather_wait(name, send_sems, recv_sems, lands, after):
    n = len(lands)

    def body(*refs):
        for send, recv in _gather_ici_copies(refs[:n], refs[n], refs[n + 1]):
            send.wait_send()
            recv.wait_recv()

    out = pl.pallas_call(
        body, name=name,
        out_shape=[pltpu.HBM(l.shape, l.dtype) for l in lands],
        in_specs=[HBM_SPEC] * n + [SEM_SPEC, SEM_SPEC, ANY_SPEC],
        out_specs=[HBM_SPEC] * n,
        input_output_aliases={a: a for a in range(n)},
        compiler_params=SPLIT_COPY,
    )(*lands, send_sems, recv_sems, after)
    return list(out)


def _share_copies(lands, send_sems, recv_sems):
    x, y, c = _mesh_pos()
    pairs = []
    for a in range(len(lands)):
        for j, chip in enumerate(_other_chips(x, y)):
            def copy(dev):
                slot = _dev_slot(lands[a], dev)
                return pltpu.make_async_remote_copy(
                    src_ref=slot, dst_ref=slot, send_sem=send_sems.at[3 * a + j], recv_sem=recv_sems.at[3 * a + j],
                    device_id=(x, y, 1 - c), device_id_type=MESH)
            pairs.append((copy((*chip, c)), copy((*chip, 1 - c))))
    return pairs


def _share_start(name, lands):
    n = len(lands)

    def body(*refs):
        token = refs[-1]
        for send, _ in _share_copies(refs[:n], refs[n], refs[n + 1]):
            send.start()
        token[...] = jnp.zeros_like(token)

    out = pl.pallas_call(
        body, name=name,
        out_shape=(pltpu.SemaphoreType.DMA((3 * n,)), pltpu.SemaphoreType.DMA((3 * n,)),
                   *[pltpu.HBM(l.shape, l.dtype) for l in lands], jax.ShapeDtypeStruct((8, 128), F32)),
        in_specs=[HBM_SPEC] * n,
        out_specs=(SEM_SPEC, SEM_SPEC, *[HBM_SPEC] * n, pl.BlockSpec(memory_space=pltpu.VMEM)),
        input_output_aliases={a: 2 + a for a in range(n)},
        compiler_params=SPLIT_COPY,
    )(*lands)
    return out[0], out[1], list(out[2:2 + n]), out[-1]


def _share_wait(name, send_sems, recv_sems, lands, after):
    n = len(lands)

    def body(*refs):
        for send, recv in _share_copies(refs[:n], refs[n], refs[n + 1]):
            send.wait_send()
            recv.wait_recv()

    out = pl.pallas_call(
        body, name=name,
        out_shape=[pltpu.HBM(l.shape, l.dtype) for l in lands],
        in_specs=[HBM_SPEC] * n + [SEM_SPEC, SEM_SPEC, ANY_SPEC],
        out_specs=[HBM_SPEC] * n,
        input_output_aliases={a: a for a in range(n)},
        compiler_params=SPLIT_COPY,
    )(*lands, send_sems, recv_sems, after)
    return list(out)


def _sibling_copies(ins, lands, send_sems, recv_sems):
    x, y, c = _mesh_pos()
    return [pltpu.make_async_remote_copy(
        src_ref=ins[a].at[j, 1 - c], dst_ref=lands[a].at[j],
        send_sem=send_sems.at[4 * a + j], recv_sem=recv_sems.at[4 * a + j],
        device_id=(x, y, 1 - c), device_id_type=MESH) for a in range(len(ins)) for j in range(4)]


def _sibling_start(name, grads):
    n = len(grads)

    def body(*refs):
        token = refs[-1]
        for cp in _sibling_copies(refs[:n], refs[n:2 * n], refs[2 * n], refs[2 * n + 1]):
            cp.start()
        token[...] = jnp.zeros_like(token)

    lands = [_hbm(lax.empty((4,) + g.shape[2:], g.dtype)) for g in grads]
    out = pl.pallas_call(
        body, name=name,
        out_shape=(pltpu.SemaphoreType.DMA((4 * n,)), pltpu.SemaphoreType.DMA((4 * n,)),
                   *[pltpu.HBM(g.shape, g.dtype) for g in grads], *[pltpu.HBM(l.shape, l.dtype) for l in lands],
                   jax.ShapeDtypeStruct((8, 128), F32)),
        in_specs=[HBM_SPEC] * (2 * n),
        out_specs=(SEM_SPEC, SEM_SPEC, *[HBM_SPEC] * (2 * n), pl.BlockSpec(memory_space=pltpu.VMEM)),
        input_output_aliases={i: 2 + i for i in range(2 * n)},
        compiler_params=SPLIT_COPY,
    )(*[_hbm(g) for g in grads], *lands)
    return out[0], out[1], list(out[2:2 + n]), list(out[2 + n:2 + 2 * n]), out[-1]


def _sibling_wait(name, send_sems, recv_sems, grads, lands, after):
    n = len(grads)

    def body(*refs):
        for cp in _sibling_copies(refs[:n], refs[n:2 * n], refs[2 * n], refs[2 * n + 1]):
            cp.wait_send()
            cp.wait_recv()

    out = pl.pallas_call(
        body, name=name,
        out_shape=[pltpu.HBM(g.shape, g.dtype) for g in grads] + [pltpu.HBM(l.shape, l.dtype) for l in lands],
        in_specs=[HBM_SPEC] * (2 * n) + [SEM_SPEC, SEM_SPEC, ANY_SPEC],
        out_specs=[HBM_SPEC] * (2 * n),
        input_output_aliases={i: i for i in range(2 * n)},
        compiler_params=SPLIT_COPY,
    )(*grads, *lands, send_sems, recv_sems, after)
    return list(out[:n]), list(out[n:])


def _chip_exchange_start(name, parts):
    n = len(parts)

    def body(*refs):
        ins, lands = refs[:n], refs[n:2 * n]
        token = refs[-1]
        for send, _ in _exchange_ici_copies(ins, lands, refs[2 * n], refs[2 * n + 1]):
            send.start()
        token[...] = jnp.zeros_like(token)

    lands = [_hbm(lax.empty(p.shape, p.dtype)) for p in parts]
    out = pl.pallas_call(
        body, name=name,
        out_shape=(pltpu.SemaphoreType.DMA((3 * n,)), pltpu.SemaphoreType.DMA((3 * n,)),
                   *[pltpu.HBM(p.shape, p.dtype) for p in parts], *[pltpu.HBM(p.shape, p.dtype) for p in parts],
                   jax.ShapeDtypeStruct((8, 128), F32)),
        in_specs=[HBM_SPEC] * (2 * n),
        out_specs=(SEM_SPEC, SEM_SPEC, *[HBM_SPEC] * (2 * n), pl.BlockSpec(memory_space=pltpu.VMEM)),
        input_output_aliases={i: 2 + i for i in range(2 * n)},
        compiler_params=SPLIT_COPY,
    )(*[_hbm(p) for p in parts], *lands)
    return out[0], out[1], list(out[2:2 + n]), list(out[2 + n:2 + 2 * n]), out[-1]


def _chip_exchange_wait(name, send_sems, recv_sems, parts, lands, after):
    n = len(parts)

    def body(*refs):
        ins, lands_in = refs[:n], refs[n:2 * n]
        for send, recv in _exchange_ici_copies(ins, lands_in, refs[2 * n], refs[2 * n + 1]):
            send.wait_send()
            recv.wait_recv()

    out = pl.pallas_call(
        body, name=name,
        out_shape=[pltpu.HBM(p.shape, p.dtype) for p in parts] * 2,
        in_specs=[HBM_SPEC] * (2 * n) + [SEM_SPEC, SEM_SPEC, ANY_SPEC],
        out_specs=[HBM_SPEC] * (2 * n),
        input_output_aliases={i: i for i in range(2 * n)},
        compiler_params=SPLIT_COPY,
    )(*parts, *lands, send_sems, recv_sems, after)
    return list(out[:n]), list(out[n:])


def _join_copies(bufs, send_sems, recv_sems, layer):
    x, y, c = _mesh_pos()

    def copy(a, half):
        return pltpu.make_async_remote_copy(
            src_ref=bufs[a].at[layer, half], dst_ref=bufs[a].at[layer, half],
            send_sem=send_sems.at[a], recv_sem=recv_sems.at[a], device_id=(x, y, 1 - c), device_id_type=MESH)

    return [(copy(a, c), copy(a, 1 - c)) for a in range(len(bufs))]


def _sibling_join_halves(name, bufs, layer, after=()):
    n = len(bufs)

    def body(*refs):
        outs = refs[n + len(after):2 * n + len(after)]
        pairs = _join_copies(outs, refs[-2], refs[-1], layer)
        for send, _ in pairs:
            send.start()
        for send, recv in pairs:
            send.wait_send()
            recv.wait_recv()

    return pl.pallas_call(
        body, name=name, out_shape=[jax.ShapeDtypeStruct(b.shape, b.dtype) for b in bufs],
        in_specs=_any_specs(n + len(after)), out_specs=_any_specs(n),
        input_output_aliases={a: a for a in range(n)},
        scratch_shapes=[pltpu.SemaphoreType.DMA((n,)), pltpu.SemaphoreType.DMA((n,))],
    )(*bufs, *after)


def _join_start(name, bufs, layer):
    n = len(bufs)

    def body(*refs):
        token = refs[-1]
        for send, _ in _join_copies(refs[:n], refs[n], refs[n + 1], layer):
            send.start()
        token[...] = jnp.zeros_like(token)

    out = pl.pallas_call(
        body, name=name,
        out_shape=(pltpu.SemaphoreType.DMA((n,)), pltpu.SemaphoreType.DMA((n,)),
                   *[pltpu.HBM(b.shape, b.dtype) for b in bufs], jax.ShapeDtypeStruct((8, 128), F32)),
        in_specs=[HBM_SPEC] * n,
        out_specs=(SEM_SPEC, SEM_SPEC, *[HBM_SPEC] * n, pl.BlockSpec(memory_space=pltpu.VMEM)),
        input_output_aliases={a: 2 + a for a in range(n)},
        compiler_params=SPLIT_COPY,
    )(*[_hbm(b) for b in bufs])
    return out[0], out[1], list(out[2:2 + n]), out[-1]


def _join_wait(name, send_sems, recv_sems, bufs, layer, after):
    n = len(bufs)

    def body(*refs):
        for send, recv in _join_copies(refs[:n], refs[n], refs[n + 1], layer):
            send.wait_send()
            recv.wait_recv()

    out = pl.pallas_call(
        body, name=name,
        out_shape=[pltpu.HBM(b.shape, b.dtype) for b in bufs],
        in_specs=[HBM_SPEC] * n + [SEM_SPEC, SEM_SPEC, ANY_SPEC],
        out_specs=[HBM_SPEC] * n,
        input_output_aliases={a: a for a in range(n)},
        compiler_params=SPLIT_COPY,
    )(*bufs, send_sems, recv_sems, after)
    return list(out)


def _row_tile(rows, want):
    t = 1
    while t * 2 <= min(rows, want):
        t *= 2
    while rows % t:
        t //= 2
    return t


def _fit_rows(rows, cols, max_elems):
    for k in range(1, rows + 1):
        if rows % k == 0 and (rows // k) % 16 == 0 and (rows // k) * cols <= max_elems:
            return rows // k
    return _row_tile(rows, max(8, max_elems // cols))


def _add_halves(name, grads, recvd, c_idx):
    outs = []
    for a, (g, r) in enumerate(zip(grads, recvd)):
        _, _, r2, cc = g.shape
        tr = _fit_rows(r2, cc, 1536 * 1024)

        def body(c_ref, g_ref, r_ref, o_ref):
            o_ref[...] = (g_ref[...].astype(F32) + r_ref[...].astype(F32)).astype(BF16)

        outs.append(pl.pallas_call(
            body, name=f"{name}_{a}",
            grid_spec=pltpu.PrefetchScalarGridSpec(
                num_scalar_prefetch=1, grid=(4, r2 // tr),
                in_specs=[pl.BlockSpec((None, None, tr, cc), lambda j, i, c_ref: (j, c_ref[0], i, 0)),
                          pl.BlockSpec((None, tr, cc), lambda j, i, c_ref: (j, i, 0))],
                out_specs=pl.BlockSpec((None, tr, cc), lambda j, i, c_ref: (j, i, 0))),
            out_shape=jax.ShapeDtypeStruct(r.shape, BF16), compiler_params=_cp(("parallel", "parallel")),
        )(c_idx, g, r))
    return outs


def _sum_parts(name, parts, recvd, bufs, layer, where):
    outs = []
    for a, (p, r, buf) in enumerate(zip(parts, recvd, bufs)):
        _, r2, cc = p.shape
        tr = _fit_rows(r2, cc, 768 * 1024)

        def body(where_ref, own_ref, r1_ref, r2_ref, r3_ref, buf_ref, o_ref):
            acc = own_ref[...].astype(F32)
            for ref in (r1_ref, r2_ref, r3_ref):
                acc = acc + ref[...].astype(F32)
            o_ref[...] = acc

        def entry(k):
            return pl.BlockSpec((None, tr, cc), lambda i, where_ref: (where_ref[k], i, 0))

        outs.append(pl.pallas_call(
            body, name=f"{name}_{a}",
            grid_spec=pltpu.PrefetchScalarGridSpec(
                num_scalar_prefetch=1, grid=(r2 // tr,),
                in_specs=[entry(0), entry(1), entry(2), entry(3), ANY_SPEC],
                out_specs=pl.BlockSpec((None, None, tr, cc), lambda i, where_ref: (layer, where_ref[4], i, 0))),
            out_shape=jax.ShapeDtypeStruct(buf.shape, F32), input_output_aliases={5: 0},
            compiler_params=_cp(("parallel",)),
        )(where, p, r, r, r, buf))
    return outs


def _sum8(name, gathered):
    _, r, cc = gathered.shape
    tr = _row_tile(r, 512)

    def body(g_ref, o_ref):
        acc = g_ref[0]
        for i in range(1, 8):
            acc = acc + g_ref[i]
        o_ref[...] = acc

    return pl.pallas_call(
        body, name=name, grid=(r // tr,),
        in_specs=[pl.BlockSpec((8, tr, cc), lambda i: (0, i, 0))],
        out_specs=pl.BlockSpec((tr, cc), lambda i: (i, 0)),
        out_shape=jax.ShapeDtypeStruct((r, cc), F32), compiler_params=_cp(("parallel",)),
    )(gathered)


def _adamw(name, w, g, m, v):
    r, cc = w.shape
    tr = _row_tile(r, max(8, (512 * 1024) // cc))

    def body(w_ref, g_ref, m_ref, v_ref, d_ref, mo_ref, vo_ref):
        gg = g_ref[...]
        mn = ADAM_B1 * m_ref[...] + (1.0 - ADAM_B1) * gg
        vn = ADAM_B2 * v_ref[...] + (1.0 - ADAM_B2) * (gg * gg)
        m_hat = mn / (1.0 - ADAM_B1 ** ADAM_STEP)
        v_hat = vn / (1.0 - ADAM_B2 ** ADAM_STEP)
        d_ref[...] = -ADAM_LR * (m_hat / (jnp.sqrt(v_hat) + ADAM_EPS) + ADAM_WD * w_ref[...])
        mo_ref[...] = mn
        vo_ref[...] = vn

    spec = pl.BlockSpec((tr, cc), lambda i: (i, 0))
    return pl.pallas_call(
        body, name=name, grid=(r // tr,), in_specs=[spec] * 4, out_specs=[spec] * 3,
        out_shape=[jax.ShapeDtypeStruct(w.shape, F32)] * 3, compiler_params=_cp(("parallel",)),
    )(w, g, m, v)


def _rms_fwd(name, h, g, after=()):
    s, d = h.shape
    tr = _row_tile(s, 512)

    def body(h_ref, g_ref, *rest):
        y_ref = rest[-1]
        xv = h_ref[...]
        r = lax.rsqrt(jnp.mean(xv * xv, axis=-1, keepdims=True) + EPS)
        y_ref[...] = (xv * r * g_ref[...]).astype(BF16)

    return pl.pallas_call(
        body, name=name, grid=(s // tr,),
        in_specs=[pl.BlockSpec((tr, d), lambda i: (i, 0)), pl.BlockSpec((1, d), lambda i: (0, 0))]
        + [ANY_SPEC] * len(after),
        out_specs=pl.BlockSpec((tr, d), lambda i: (i, 0)),
        out_shape=jax.ShapeDtypeStruct((s, d), BF16), compiler_params=_cp(("parallel",)),
    )(h, g, *after)


def _rms_bwd_rows(xv, gv, dy):
    d = xv.shape[-1]
    r = lax.rsqrt(jnp.mean(xv * xv, axis=-1, keepdims=True) + EPS)
    dxn = dy * gv
    proj = jnp.sum(dxn * xv, axis=-1, keepdims=True) * (1.0 / d)
    dx = r * dxn - xv * (r * r * r) * proj
    return dx, dy * xv * r


def _rms_bwd(name, h, g, dy, dres):
    s, d = h.shape
    tr = _row_tile(s, 512)
    band = 16

    def body(h_ref, g_ref, dy_ref, dres_ref, dh_ref, dhb_ref, dg_ref, acc):
        acc[...] = jnp.zeros_like(acc)

        def step(i, carry):
            rows = pl.ds(pl.multiple_of(i * band, band), band)
            dx, dgp = _rms_bwd_rows(h_ref[rows, :], g_ref[...], dy_ref[rows, :].astype(F32))
            dh = dres_ref[rows, :] + dx
            dh_ref[rows, :] = dh
            dhb_ref[rows, :] = dh.astype(BF16)
            acc[...] += _fold8(dgp)
            return carry

        lax.fori_loop(0, tr // band, step, 0, unroll=4)

        @pl.when(pl.program_id(0) == 0)
        def _():
            dg_ref[...] = jnp.zeros_like(dg_ref)

        dg_ref[...] += jnp.sum(acc[...], axis=0, keepdims=True)

    row = pl.BlockSpec((tr, d), lambda i: (i, 0))
    vec = pl.BlockSpec((1, d), lambda i: (0, 0))
    return pl.pallas_call(
        body, name=name, grid=(s // tr,), in_specs=[row, vec, row, row], out_specs=[row, row, vec],
        out_shape=[jax.ShapeDtypeStruct((s, d), F32), jax.ShapeDtypeStruct((s, d), BF16),
                   jax.ShapeDtypeStruct((1, d), F32)],
        scratch_shapes=[pltpu.VMEM((8, d), F32)],
        compiler_params=_cp(("arbitrary",)),
    )(h, g, dy, dres)


def _loss_head(name, h, g, target):
    s, d = h.shape
    tr = _row_tile(s, 256)

    def body(h_ref, g_ref, t_ref, loss_ref, dh_ref, dhb_ref, dg_ref):
        xv = h_ref[...]
        gv = g_ref[...]
        r = lax.rsqrt(jnp.mean(xv * xv, axis=-1, keepdims=True) + EPS)
        err = xv * r * gv - t_ref[...]
        part = 0.5 * jnp.sum(jnp.sum(err * err, axis=-1, keepdims=True) * (1.0 / d), axis=0, keepdims=True)
        dx, dgp = _rms_bwd_rows(xv, gv, err * (1.0 / d))
        dh_ref[...] = dx
        dhb_ref[...] = dx.astype(BF16)

        @pl.when(pl.program_id(0) == 0)
        def _():
            dg_ref[...] = jnp.zeros_like(dg_ref)
            loss_ref[...] = jnp.zeros_like(loss_ref)

        dg_ref[...] += jnp.sum(dgp, axis=0, keepdims=True)
        loss_ref[...] += part

    row = pl.BlockSpec((tr, d), lambda i: (i, 0))
    vec = pl.BlockSpec((1, d), lambda i: (0, 0))
    one = pl.BlockSpec((1, 1), lambda i: (0, 0))
    return pl.pallas_call(
        body, name=name, grid=(s // tr,), in_specs=[row, vec, row], out_specs=[one, row, row, vec],
        out_shape=[jax.ShapeDtypeStruct((1, 1), F32), jax.ShapeDtypeStruct((s, d), F32),
                   jax.ShapeDtypeStruct((s, d), BF16), jax.ShapeDtypeStruct((1, d), F32)],
        compiler_params=_cp(("arbitrary",)),
    )(h, g, target)


NN = (((1,), (0,)), ((), ()))
NT = (((1,), (1,)), ((), ()))
TN = (((0,), (0,)), ((), ()))


def _matmul(name, grid, operands, in_specs, pairs, n_acc, acc_shape, epilogue, out_specs, out_shape, after=()):
    operands = list(operands) + list(after)
    in_specs = list(in_specs) + [ANY_SPEC] * len(after)
    n_in = len(operands)
    n_out = len(out_shape)
    nk = grid[2]

    def body(*refs):
        ins, outs, accs = refs[:n_in], refs[n_in:n_in + n_out], refs[n_in + n_out:]
        k = pl.program_id(2)

        @pl.when(k == 0)
        def _():
            for acc in accs:
                acc[...] = jnp.zeros_like(acc)

        for li, ri, ai, dims in pairs:
            accs[ai][...] += lax.dot_general(ins[li][...], ins[ri][...], dims, preferred_element_type=F32)

        @pl.when(k == nk - 1)
        def _():
            epilogue([acc[...] for acc in accs], ins, outs)

    return pl.pallas_call(
        body, name=name, grid=grid, in_specs=in_specs, out_specs=out_specs, out_shape=out_shape,
        scratch_shapes=[pltpu.VMEM(acc_shape, F32)] * n_acc,
        compiler_params=_cp(("parallel", "parallel", "arbitrary")),
    )(*operands)


def _ep_store(dtype):
    def ep(accs, ins, outs):
        outs[0][...] = accs[0].astype(dtype)
    return ep


def _w_spec_nt(wg, kind, to, tc):
    _, a, b = wg.shape
    if kind == "col":
        oph, cps = a // to, b // tc
        return pl.BlockSpec((None, to, tc), lambda i, jo, kc: (2 * (kc // cps) + jo // oph, jo % oph, kc % cps))
    ops, cph = a // to, b // tc
    return pl.BlockSpec((None, to, tc), lambda i, jo, kc: (2 * (jo // ops) + kc // cph, jo % ops, kc % cph))


SUB_COLS = 256


def _matmul_wide(name, grid, operands, in_specs, groups, tn, epilogue, out_specs, out_shape, after=()):
    operands = list(operands) + list(after)
    in_specs = list(in_specs) + [ANY_SPEC] * len(after)
    n_in = len(operands)

    def body(*refs):
        ins, outs = refs[:n_in], refs[n_in:]
        for s0 in range(0, tn, SUB_COLS):
            cols = slice(s0, min(s0 + SUB_COLS, tn))
            accs = []
            for group in groups:
                acc = None
                for li, (c0, cw), ri, dims in group:
                    rhs = ins[ri][:, cols] if dims == NN else ins[ri][cols, :]
                    part = lax.dot_general(ins[li][:, c0:c0 + cw], rhs, dims, preferred_element_type=F32)
                    acc = part if acc is None else acc + part
                accs.append(acc)
            epilogue(accs, ins, outs, cols)

    return pl.pallas_call(
        body, name=name, grid=grid, in_specs=in_specs, out_specs=out_specs, out_shape=out_shape,
        compiler_params=_cp(("parallel", "parallel")),
    )(*operands)


def _wide_store(dtype):
    def ep(accs, ins, outs, cols):
        outs[0][:, cols] = accs[0].astype(dtype)
    return ep


def _wide_nn_weights(wg, kind, tn, first):
    _, a, b = wg.shape
    per = b // tn
    if kind == "col":
        specs = [pl.BlockSpec((None, a, tn), lambda i, j, hf=hf: (2 * (j // per) + hf, 0, j % per)) for hf in range(2)]
    else:
        specs = [pl.BlockSpec((None, a, tn), lambda i, j, ch=ch: (2 * ch + j // per, 0, j % per)) for ch in range(4)]
    dots = [(0, (q * a, a), first + q, NN) for q in range(len(specs))]
    return [wg] * len(specs), specs, dots


def _wide_nt_weights(wg, kind, to, first):
    _, a, b = wg.shape
    per = a // to
    if kind == "col":
        specs = [pl.BlockSpec((None, to, b), lambda i, j, ch=ch: (2 * ch + j // per, j % per, 0)) for ch in range(4)]
    else:
        specs = [pl.BlockSpec((None, to, b), lambda i, j, hf=hf: (2 * (j // per) + hf, j % per, 0)) for hf in range(2)]
    dots = [(0, (q * b, b), first + q, NT) for q in range(len(specs))]
    return [wg] * len(specs), specs, dots


def _mm_nn_wide(name, x, wg, kind, tm, tn, epilogue=None, extra=(), extra_specs=(), out_specs=None, out_shape=None,
                after=()):
    m, kdim = x.shape
    _, a, b = wg.shape
    n = 4 * b if kind == "col" else 2 * b
    tm = min(tm, m)
    ws, wspecs, dots = _wide_nn_weights(wg, kind, tn, 1)
    if out_shape is None:
        out_shape = [jax.ShapeDtypeStruct((m, n), F32)]
        out_specs = [pl.BlockSpec((tm, tn), lambda i, j: (i, j))]
        epilogue = _wide_store(F32)
    return _matmul_wide(name, (m // tm, n // tn), [x] + ws + list(extra),
                        [pl.BlockSpec((tm, kdim), lambda i, j: (i, 0))] + wspecs + list(extra_specs),
                        [dots], tn, epilogue, out_specs, out_shape, after=after)


def _mm_nt_wide(name, dy, wg, kind, tm, to, out_dtype=BF16, after=()):
    m, n = dy.shape
    _, a, b = wg.shape
    kdim = 2 * a if kind == "col" else 4 * a
    tm = min(tm, m)
    ws, wspecs, dots = _wide_nt_weights(wg, kind, to, 1)
    return _matmul_wide(name, (m // tm, kdim // to), [dy] + ws,
                        [pl.BlockSpec((tm, n), lambda i, j: (i, 0))] + wspecs, [dots], to, _wide_store(out_dtype),
                        [pl.BlockSpec((tm, to), lambda i, j: (i, j))],
                        [jax.ShapeDtypeStruct((m, kdim), out_dtype)], after=after)[0]


def _mm_tn_wide(name, a, dy, kind, tr, tn):
    m, kdim = a.shape
    _, n = dy.shape

    def body(a_ref, dy_ref, o_ref):
        for s0 in range(0, tn, 2 * SUB_COLS):
            cols = slice(s0, min(s0 + 2 * SUB_COLS, tn))
            o_ref[:, cols] = lax.dot_general(a_ref[...], dy_ref[:, cols], TN, preferred_element_type=F32).astype(BF16)

    if kind == "col":
        ns = n // 4
        per = ns // tn
        out_shape = jax.ShapeDtypeStruct((4, kdim, ns), BF16)
        out_spec = pl.BlockSpec((None, tr, tn), lambda r, j: (j // per, r, j % per))
    else:
        rs = kdim // 4
        per = rs // tr
        out_shape = jax.ShapeDtypeStruct((4, rs, n), BF16)
        out_spec = pl.BlockSpec((None, tr, tn), lambda r, j: (r // per, r % per, j))
    return pl.pallas_call(
        body, name=name, grid=(kdim // tr, n // tn),
        in_specs=[pl.BlockSpec((m, tr), lambda r, j: (0, r)), pl.BlockSpec((m, tn), lambda r, j: (0, j))],
        out_specs=out_spec, out_shape=out_shape, compiler_params=_cp(("parallel", "parallel")),
    )(a, dy)


def _zero_halo(pad_ref, s):
    z = jnp.zeros((HALO, pad_ref.shape[1]), F32)
    pad_ref[pl.ds(0, HALO), :] = z
    pad_ref[pl.ds(HALO + s, HALO), :] = z


def _window(pad_ref, r0, rows):
    return pad_ref[pl.ds(r0, rows + 2 * HALO), :]


def _delayed(win, k, rows):
    if k == 0:
        return win[HALO:HALO + rows]
    return pltpu.roll(win, k, axis=0)[HALO:HALO + rows]


def _advanced(win, k, rows):
    if k == 0:
        return win[HALO:HALO + rows]
    return pltpu.roll(win, win.shape[0] - k, axis=0)[HALO:HALO + rows]


def _fold8(x):
    return jnp.sum(x.reshape(x.shape[0] // 8, 8, x.shape[1]), axis=0)


def _chunks(s):
    rows = min(CHUNK_ROWS, s)
    return rows, s // rows


def _col_spec(s, first_block):
    return pl.BlockSpec((s, HEAD), lambda j: (0, first_block + j))


def _sgu_fwd(name, z, ln_g, ln_b, w, b):
    s = z.shape[0]
    tr = _row_tile(s, 1024)
    nh = GROUP // HEAD

    def body(u_ref, v_ref, lg_ref, lb_ref, w_ref, b_ref, o_ref):
        row = lax.broadcasted_iota(jnp.int32, (HEAD, HEAD), 0)
        col = lax.broadcasted_iota(jnp.int32, (HEAD, HEAD), 1)
        wm = jnp.where(row >= col, w_ref[...], 0.0).astype(BF16)
        for ck in range(tr // HEAD):
            rs = pl.ds(ck * HEAD, HEAD)
            u = _gelu(u_ref[rs, :])
            v = _gelu(v_ref[rs, :])
            mu = jnp.mean(v, axis=-1, keepdims=True)
            vc = v - mu
            var = jnp.mean(vc * vc, axis=-1, keepdims=True)
            vln = vc * lax.rsqrt(var + EPS) * lg_ref[...] + lb_ref[...]
            sp = jnp.dot(wm, vln.astype(BF16), preferred_element_type=F32) + b_ref[...]
            o_ref[rs, :] = (u * sp).astype(BF16)

    head_vec = pl.BlockSpec((None, 1, HEAD), lambda h, i: (h, 0, 0))
    return pl.pallas_call(
        body, name=name, grid=(nh, s // tr),
        in_specs=[pl.BlockSpec((tr, HEAD), lambda h, i: (i, h)), pl.BlockSpec((tr, HEAD), lambda h, i: (i, nh + h)),
                  head_vec, head_vec, pl.BlockSpec((None, HEAD, HEAD), lambda h, i: (h, 0, 0)),
                  pl.BlockSpec((None, HEAD, 1), lambda h, i: (h, 0, 0))],
        out_specs=pl.BlockSpec((tr, HEAD), lambda h, i: (i, h)),
        out_shape=jax.ShapeDtypeStruct((s, GROUP), BF16), compiler_params=_cp(("parallel", "parallel")),
    )(z, z, ln_g, ln_b, w, b)


def _sgu_bwd(name, z, d_o, ln_g, ln_b, w, b):
    s = z.shape[0]
    tr = _row_tile(s, 1024)
    nh = GROUP // HEAD

    def body(u_ref, v_ref, do_ref, lg_ref, lb_ref, w_ref, b_ref, du_ref, dv_ref, dlg_ref, dlb_ref, dw_ref, db_ref,
             dsp_acc):
        row = lax.broadcasted_iota(jnp.int32, (HEAD, HEAD), 0)
        col = lax.broadcasted_iota(jnp.int32, (HEAD, HEAD), 1)
        tril = row >= col
        wm = jnp.where(tril, w_ref[...], 0.0).astype(BF16)
        i = pl.program_id(1)

        @pl.when(i == 0)
        def _():
            dlg_ref[...] = jnp.zeros_like(dlg_ref)
            dlb_ref[...] = jnp.zeros_like(dlb_ref)
            dw_ref[...] = jnp.zeros_like(dw_ref)
            dsp_acc[...] = jnp.zeros_like(dsp_acc)

        dlg = jnp.zeros((1, HEAD), F32)
        dlb = jnp.zeros((1, HEAD), F32)
        dw = jnp.zeros((HEAD, HEAD), F32)
        dsp_sum = jnp.zeros((HEAD, HEAD), F32)
        for ck in range(tr // HEAD):
            rs = pl.ds(ck * HEAD, HEAD)
            zu = u_ref[rs, :]
            zv = v_ref[rs, :]
            u = _gelu(zu)
            v = _gelu(zv)
            mu = jnp.mean(v, axis=-1, keepdims=True)
            vc = v - mu
            var = jnp.mean(vc * vc, axis=-1, keepdims=True)
            rstd = lax.rsqrt(var + EPS)
            xh = vc * rstd
            vln = (xh * lg_ref[...] + lb_ref[...]).astype(BF16)
            sp = jnp.dot(wm, vln, preferred_element_type=F32) + b_ref[...]
            d_oa = do_ref[rs, :].astype(F32)
            du = d_oa * sp
            dsp = d_oa * u
            dsp_b = dsp.astype(BF16)
            dvln = lax.dot_general(wm, dsp_b, TN, preferred_element_type=F32)
            dw = dw + lax.dot_general(dsp_b, vln, NT, preferred_element_type=F32)
            dsp_sum = dsp_sum + dsp
            dlg = dlg + jnp.sum(dvln * xh, axis=0, keepdims=True)
            dlb = dlb + jnp.sum(dvln, axis=0, keepdims=True)
            dxh = dvln * lg_ref[...]
            dv = rstd * (dxh - jnp.mean(dxh, axis=-1, keepdims=True)
                         - xh * jnp.mean(dxh * xh, axis=-1, keepdims=True))
            du_ref[rs, :] = (du * _gelu_grad(zu)).astype(BF16)
            dv_ref[rs, :] = (dv * _gelu_grad(zv)).astype(BF16)
        dlg_ref[...] += dlg
        dlb_ref[...] += dlb
        dw_ref[...] += jnp.where(tril, dw, 0.0)
        dsp_acc[...] += dsp_sum

        @pl.when(i == pl.num_programs(1) - 1)
        def _():
            db_ref[...] = jnp.sum(dsp_acc[...], axis=1, keepdims=True)

    head_vec = pl.BlockSpec((None, 1, HEAD), lambda h, i: (h, 0, 0))
    head_mat = pl.BlockSpec((None, HEAD, HEAD), lambda h, i: (h, 0, 0))
    head_col = pl.BlockSpec((None, HEAD, 1), lambda h, i: (h, 0, 0))
    return pl.pallas_call(
        body, name=name, grid=(nh, s // tr),
        in_specs=[pl.BlockSpec((tr, HEAD), lambda h, i: (i, h)), pl.BlockSpec((tr, HEAD), lambda h, i: (i, nh + h)),
                  pl.BlockSpec((tr, HEAD), lambda h, i: (i, h)), head_vec, head_vec, head_mat, head_col],
        out_specs=[pl.BlockSpec((tr, HEAD), lambda h, i: (i, h)), pl.BlockSpec((tr, HEAD), lambda h, i: (i, h)),
                   head_vec, head_vec, head_mat, head_col],
        out_shape=[jax.ShapeDtypeStruct((s, GROUP), BF16), jax.ShapeDtypeStruct((s, GROUP), BF16),
                   jax.ShapeDtypeStruct((nh, 1, HEAD), F32), jax.ShapeDtypeStruct((nh, 1, HEAD), F32),
                   jax.ShapeDtypeStruct((nh, HEAD, HEAD), F32), jax.ShapeDtypeStruct((nh, HEAD, 1), F32)],
        scratch_shapes=[pltpu.VMEM((HEAD, HEAD), F32)],
        compiler_params=_cp(("parallel", "arbitrary")),
    )(z, z, d_o, ln_g, ln_b, w, b)


def _shortconv_fwd(name, z, w):
    s = z.shape[0]
    kw = w.shape[0]
    rows, nchunk = _chunks(s)
    nb = GROUP // HEAD

    def body(h_ref, bg_ref, cg_ref, w_ref, o_ref, pad):
        _zero_halo(pad, s)

        def fill(ci, carry):
            r0 = pl.multiple_of(ci * rows, rows)
            pad[pl.ds(pl.multiple_of(HALO + r0, 8), rows), :] = cg_ref[pl.ds(r0, rows), :] * h_ref[pl.ds(r0, rows), :]
            return carry

        lax.fori_loop(0, nchunk, fill, 0)

        def step(ci, carry):
            r0 = pl.multiple_of(ci * rows, rows)
            win = _window(pad, r0, rows)
            cv = jnp.zeros((rows, HEAD), F32)
            for k in range(kw):
                cv = cv + w_ref[k:k + 1, :] * _delayed(win, kw - 1 - k, rows)
            o_ref[pl.ds(r0, rows), :] = (bg_ref[pl.ds(r0, rows), :] * cv).astype(BF16)
            return carry

        lax.fori_loop(0, nchunk, step, 0)

    return pl.pallas_call(
        body, name=name, grid=(nb,),
        in_specs=[_col_spec(s, 8), _col_spec(s, 12), _col_spec(s, 16), pl.BlockSpec((kw, HEAD), lambda j: (0, j))],
        out_specs=_col_spec(s, 0),
        out_shape=jax.ShapeDtypeStruct((s, GROUP), BF16),
        scratch_shapes=[pltpu.VMEM((s + 2 * HALO, HEAD), F32)],
        compiler_params=_cp(("parallel",)),
    )(z, z, z, w)


def _shortconv_bwd(name, z, d_o, w):
    s = z.shape[0]
    kw = w.shape[0]
    rows, nchunk = _chunks(s)
    nb = GROUP // HEAD

    def body(h_ref, bg_ref, cg_ref, do_ref, w_ref, dh_ref, dbg_ref, dcg_ref, dw_ref, pad_q, pad_d, acc):
        _zero_halo(pad_q, s)
        _zero_halo(pad_d, s)
        acc[...] = jnp.zeros_like(acc)

        def fill(ci, carry):
            r0 = pl.multiple_of(ci * rows, rows)
            rs = pl.ds(r0, rows)
            ps = pl.ds(pl.multiple_of(HALO + r0, 8), rows)
            pad_q[ps, :] = cg_ref[rs, :] * h_ref[rs, :]
            pad_d[ps, :] = do_ref[rs, :].astype(F32) * bg_ref[rs, :]
            return carry

        lax.fori_loop(0, nchunk, fill, 0)

        def step(ci, carry):
            r0 = pl.multiple_of(ci * rows, rows)
            rs = pl.ds(r0, rows)
            wq = _window(pad_q, r0, rows)
            wd = _window(pad_d, r0, rows)
            dcv = wd[HALO:HALO + rows]
            cv = jnp.zeros((rows, HEAD), F32)
            dq = jnp.zeros((rows, HEAD), F32)
            for k in range(kw):
                qk = _delayed(wq, kw - 1 - k, rows)
                cv = cv + w_ref[k:k + 1, :] * qk
                dq = dq + w_ref[k:k + 1, :] * _advanced(wd, kw - 1 - k, rows)
                acc[k] += _fold8(dcv * qk)
            dbg_ref[rs, :] = (do_ref[rs, :].astype(F32) * cv).astype(BF16)
            dcg_ref[rs, :] = (dq * h_ref[rs, :]).astype(BF16)
            dh_ref[rs, :] = (dq * cg_ref[rs, :]).astype(BF16)
            return carry

        lax.fori_loop(0, nchunk, step, 0)
        for k in range(kw):
            dw_ref[k:k + 1, :] = jnp.sum(acc[k], axis=0, keepdims=True)

    col = _col_spec(s, 0)
    return pl.pallas_call(
        body, name=name, grid=(nb,),
        in_specs=[_col_spec(s, 8), _col_spec(s, 12), _col_spec(s, 16), col, pl.BlockSpec((kw, HEAD), lambda j: (0, j))],
        out_specs=[col, col, col, pl.BlockSpec((kw, HEAD), lambda j: (0, j))],
        out_shape=[jax.ShapeDtypeStruct((s, GROUP), BF16)] * 3 + [jax.ShapeDtypeStruct((kw, GROUP), F32)],
        scratch_shapes=[pltpu.VMEM((s + 2 * HALO, HEAD), F32), pltpu.VMEM((s + 2 * HALO, HEAD), F32),
                        pltpu.VMEM((kw, 8, HEAD), F32)],
        compiler_params=_cp(("parallel",)),
    )(z, z, z, d_o, w)


def _conformer_conv_fwd(name, z, w, bias):
    s = z.shape[0]
    kw = w.shape[0]
    rows, nchunk = _chunks(s)
    nb = GROUP // HEAD

    def body(a_ref, g_ref, w_ref, b_ref, o_ref, pad):
        _zero_halo(pad, s)

        def fill(ci, carry):
            r0 = pl.multiple_of(ci * rows, rows)
            rs = pl.ds(r0, rows)
            pad[pl.ds(pl.multiple_of(HALO + r0, 8), rows), :] = a_ref[rs, :] * _sigmoid(g_ref[rs, :])
            return carry

        lax.fori_loop(0, nchunk, fill, 0)

        def step(ci, carry):
            r0 = pl.multiple_of(ci * rows, rows)
            win = _window(pad, r0, rows)
            cc = jnp.zeros((rows, HEAD), F32)
            for k in range(kw):
                cc = cc + w_ref[k:k + 1, :] * _delayed(win, kw - 1 - k, rows)
            o_ref[pl.ds(r0, rows), :] = cc + b_ref[...]
            return carry

        lax.fori_loop(0, nchunk, step, 0)

    return pl.pallas_call(
        body, name=name, grid=(nb,),
        in_specs=[_col_spec(s, 20), _col_spec(s, 24), pl.BlockSpec((kw, HEAD), lambda j: (0, j)),
                  pl.BlockSpec((1, HEAD), lambda j: (0, j))],
        out_specs=_col_spec(s, 0),
        out_shape=jax.ShapeDtypeStruct((s, GROUP), F32),
        scratch_shapes=[pltpu.VMEM((s + 2 * HALO, HEAD), F32)],
        compiler_params=_cp(("parallel",)),
    )(z, z, w, bias)


def _ln_rows(cc, g, b):
    mu = jnp.mean(cc, axis=-1, keepdims=True)
    xc = cc - mu
    var = jnp.mean(xc * xc, axis=-1, keepdims=True)
    rstd = lax.rsqrt(var + EPS)
    xh = xc * rstd
    return xh, rstd, xh * g + b


def _conformer_ln_fwd(name, cc, g, b):
    s, d = cc.shape
    tr = _row_tile(s, 512)

    def body(c_ref, g_ref, b_ref, o_ref):
        _, _, l = _ln_rows(c_ref[...], g_ref[...], b_ref[...])
        o_ref[...] = (l * _sigmoid(l)).astype(BF16)

    row = pl.BlockSpec((tr, d), lambda i: (i, 0))
    vec = pl.BlockSpec((1, d), lambda i: (0, 0))
    return pl.pallas_call(
        body, name=name, grid=(s // tr,), in_specs=[row, vec, vec], out_specs=row,
        out_shape=jax.ShapeDtypeStruct((s, d), BF16), compiler_params=_cp(("parallel",)),
    )(cc, g, b)


def _conformer_ln_bwd(name, cc, d_o, g, b):
    s, d = cc.shape
    tr = _row_tile(s, 512)

    def body(c_ref, do_ref, g_ref, b_ref, dcc_ref, dg_ref, db_ref, dcb_ref):
        xh, rstd, l = _ln_rows(c_ref[...], g_ref[...], b_ref[...])
        sg = _sigmoid(l)
        dl = do_ref[...].astype(F32) * sg * (1.0 + l * (1.0 - sg))
        dxh = dl * g_ref[...]
        dcc = rstd * (dxh - jnp.mean(dxh, axis=-1, keepdims=True) - xh * jnp.mean(dxh * xh, axis=-1, keepdims=True))
        dcc_ref[...] = dcc

        @pl.when(pl.program_id(0) == 0)
        def _():
            dg_ref[...] = jnp.zeros_like(dg_ref)
            db_ref[...] = jnp.zeros_like(db_ref)
            dcb_ref[...] = jnp.zeros_like(dcb_ref)

        dg_ref[...] += jnp.sum(dl * xh, axis=0, keepdims=True)
        db_ref[...] += jnp.sum(dl, axis=0, keepdims=True)
        dcb_ref[...] += jnp.sum(dcc, axis=0, keepdims=True)

    row = pl.BlockSpec((tr, d), lambda i: (i, 0))
    vec = pl.BlockSpec((1, d), lambda i: (0, 0))
    return pl.pallas_call(
        body, name=name, grid=(s // tr,), in_specs=[row, row, vec, vec], out_specs=[row, vec, vec, vec],
        out_shape=[jax.ShapeDtypeStruct((s, d), F32)] + [jax.ShapeDtypeStruct((1, d), F32)] * 3,
        compiler_params=_cp(("arbitrary",)),
    )(cc, d_o, g, b)


def _conformer_conv_bwd(name, z, dcc, w):
    s = z.shape[0]
    kw = w.shape[0]
    rows, nchunk = _chunks(s)
    nb = GROUP // HEAD

    def body(a_ref, g_ref, d_ref, w_ref, da_ref, dg_ref, dw_ref, pad_h, pad_d, acc):
        _zero_halo(pad_h, s)
        _zero_halo(pad_d, s)
        acc[...] = jnp.zeros_like(acc)

        def fill(ci, carry):
            r0 = pl.multiple_of(ci * rows, rows)
            rs = pl.ds(r0, rows)
            ps = pl.ds(pl.multiple_of(HALO + r0, 8), rows)
            pad_h[ps, :] = a_ref[rs, :] * _sigmoid(g_ref[rs, :])
            pad_d[ps, :] = d_ref[rs, :]
            return carry

        lax.fori_loop(0, nchunk, fill, 0)

        def step(ci, carry):
            r0 = pl.multiple_of(ci * rows, rows)
            rs = pl.ds(r0, rows)
            wh = _window(pad_h, r0, rows)
            wd = _window(pad_d, r0, rows)
            dcc_c = wd[HALO:HALO + rows]
            dhc = jnp.zeros((rows, HEAD), F32)
            for k in range(kw):
                dhc = dhc + w_ref[k:k + 1, :] * _advanced(wd, kw - 1 - k, rows)
                acc[k] += _fold8(dcc_c * _delayed(wh, kw - 1 - k, rows))
            sg = _sigmoid(g_ref[rs, :])
            da_ref[rs, :] = (dhc * sg).astype(BF16)
            dg_ref[rs, :] = (dhc * a_ref[rs, :] * sg * (1.0 - sg)).astype(BF16)
            return carry

        lax.fori_loop(0, nchunk, step, 0)
        for k in range(kw):
            dw_ref[k:k + 1, :] = jnp.sum(acc[k], axis=0, keepdims=True)

    col = _col_spec(s, 0)
    return pl.pallas_call(
        body, name=name, grid=(nb,),
        in_specs=[_col_spec(s, 20), _col_spec(s, 24), col, pl.BlockSpec((kw, HEAD), lambda j: (0, j))],
        out_specs=[col, col, pl.BlockSpec((kw, HEAD), lambda j: (0, j))],
        out_shape=[jax.ShapeDtypeStruct((s, GROUP), BF16)] * 2 + [jax.ShapeDtypeStruct((kw, GROUP), F32)],
        scratch_shapes=[pltpu.VMEM((s + 2 * HALO, HEAD), F32), pltpu.VMEM((s + 2 * HALO, HEAD), F32),
                        pltpu.VMEM((kw, 8, HEAD), F32)],
        compiler_params=_cp(("parallel",)),
    )(z, z, dcc, w)


def _pool_window_sum(win, level, rows, shift):
    n = win.shape[0]

    def moved(v, k):
        return pltpu.roll(v, k if shift is _delayed else n - k, axis=0)

    s2 = win + moved(win, 1)
    s4 = s2 + moved(s2, 2)
    s8 = s4 + moved(s4, 4)
    s16 = s8 + moved(s8, 8)
    sel = jnp.where(level == 0, s2, jnp.where(level == 1, s4, jnp.where(level == 2, s8, s16)))
    return sel[HALO:HALO + rows]


def _pool_count(level, r0, rows):
    t = r0 + lax.broadcasted_iota(jnp.int32, (rows, 1), 0)
    width = jnp.left_shift(jnp.int32(2), level)
    return jnp.minimum(t + 1, width).astype(F32)


def _pool_fwd(name, z, pool_w, scale):
    s = z.shape[0]
    rows, nchunk = _chunks(s)

    def body(z_ref, w_ref, sc_ref, o_ref, pad):
        level = pl.program_id(0)
        _zero_halo(pad, s)

        def fill(ci, carry):
            r0 = pl.multiple_of(ci * rows, rows)
            pad[pl.ds(pl.multiple_of(HALO + r0, 8), rows), :] = z_ref[pl.ds(r0, rows), :]
            return carry

        lax.fori_loop(0, nchunk, fill, 0)
        wb = w_ref[...].astype(BF16)

        def step(ci, carry):
            r0 = pl.multiple_of(ci * rows, rows)
            win = _window(pad, r0, rows)
            pm = _pool_window_sum(win, level, rows, _delayed) / _pool_count(level, r0, rows) - win[HALO:HALO + rows]
            r = jnp.dot(pm.astype(BF16), wb, preferred_element_type=F32)
            o_ref[pl.ds(r0, rows), :] = (r * sc_ref[...]).astype(BF16)
            return carry

        lax.fori_loop(0, nchunk, step, 0, unroll=2)

    return pl.pallas_call(
        body, name=name, grid=(POOL_LEVELS,),
        in_specs=[_col_spec(s, 28), pl.BlockSpec((None, HEAD, HEAD), lambda j: (j, 0, 0)),
                  pl.BlockSpec((1, HEAD), lambda j: (0, j))],
        out_specs=_col_spec(s, 0),
        out_shape=jax.ShapeDtypeStruct((s, GROUP), BF16),
        scratch_shapes=[pltpu.VMEM((s + 2 * HALO, HEAD), F32)],
        compiler_params=_cp(("parallel",)),
    )(z, pool_w, scale)


def _pool_bwd(name, z, d_o, pool_w, scale):
    s = z.shape[0]
    rows, nchunk = _chunks(s)

    def body(z_ref, do_ref, w_ref, sc_ref, dz_ref, dw_ref, dsc_ref, pad, pad_q, dw_acc, dsc_acc):
        level = pl.program_id(0)
        _zero_halo(pad, s)
        _zero_halo(pad_q, s)
        dw_acc[...] = jnp.zeros_like(dw_acc)
        dsc_acc[...] = jnp.zeros_like(dsc_acc)

        def fill(ci, carry):
            r0 = pl.multiple_of(ci * rows, rows)
            pad[pl.ds(pl.multiple_of(HALO + r0, 8), rows), :] = z_ref[pl.ds(r0, rows), :]
            return carry

        lax.fori_loop(0, nchunk, fill, 0)
        wb = w_ref[...].astype(BF16)

        def first(ci, carry):
            r0 = pl.multiple_of(ci * rows, rows)
            win = _window(pad, r0, rows)
            cnt = _pool_count(level, r0, rows)
            pm = (_pool_window_sum(win, level, rows, _delayed) / cnt - win[HALO:HALO + rows]).astype(BF16)
            r = jnp.dot(pm, wb, preferred_element_type=F32)
            d_od = do_ref[pl.ds(r0, rows), :].astype(F32)
            dsc_acc[...] += _fold8(d_od * r)
            dr = (d_od * sc_ref[...]).astype(BF16)
            dw_acc[...] += lax.dot_general(pm, dr, TN, preferred_element_type=F32)
            dpm = lax.dot_general(dr, wb, NT, preferred_element_type=F32)
            pad_q[pl.ds(pl.multiple_of(HALO + r0, 8), rows), :] = dpm / cnt
            return carry

        lax.fori_loop(0, nchunk, first, 0, unroll=2)

        def second(ci, carry):
            r0 = pl.multiple_of(ci * rows, rows)
            wq = _window(pad_q, r0, rows)
            dpm = wq[HALO:HALO + rows] * _pool_count(level, r0, rows)
            dz_ref[pl.ds(r0, rows), :] = (_pool_window_sum(wq, level, rows, _advanced) - dpm).astype(BF16)
            return carry

        lax.fori_loop(0, nchunk, second, 0)
        dw_ref[...] = dw_acc[...]
        dsc_ref[...] = jnp.sum(dsc_acc[...], axis=0, keepdims=True)

    col = _col_spec(s, 0)
    mat = pl.BlockSpec((None, HEAD, HEAD), lambda j: (j, 0, 0))
    vec = pl.BlockSpec((1, HEAD), lambda j: (0, j))
    return pl.pallas_call(
        body, name=name, grid=(POOL_LEVELS,),
        in_specs=[_col_spec(s, 28), col, mat, vec], out_specs=[col, mat, vec],
        out_shape=[jax.ShapeDtypeStruct((s, GROUP), BF16), jax.ShapeDtypeStruct((POOL_LEVELS, HEAD, HEAD), F32),
                   jax.ShapeDtypeStruct((1, GROUP), F32)],
        scratch_shapes=[pltpu.VMEM((s + 2 * HALO, HEAD), F32), pltpu.VMEM((s + 2 * HALO, HEAD), F32),
                        pltpu.VMEM((HEAD, HEAD), F32), pltpu.VMEM((8, HEAD), F32)],
        compiler_params=_cp(("parallel",)),
    )(z, d_o, pool_w, scale)


def _mm_residual(name, x, wg, h, tm, tn):
    m, d = h.shape
    tm = min(tm, m)

    def ep(accs, ins, outs, cols):
        outs[0][:, cols] = ins[5][:, cols] + accs[0]

    tile = pl.BlockSpec((tm, tn), lambda i, j: (i, j))
    return _mm_nn_wide(name, x, wg, "row", tm, tn, epilogue=ep, extra=[h], extra_specs=[tile], out_specs=[tile],
                       out_shape=[jax.ShapeDtypeStruct((m, d), F32)])[0]


def _swiglu_fwd(name, y, wg_gate, wg_up, tm):
    m, kdim = y.shape
    _, a, b = wg_gate.shape
    tm = min(tm, m)

    def ep(accs, ins, outs, cols):
        gt, up = accs
        outs[0][:, cols] = (gt * _sigmoid(gt) * up).astype(BF16)
        outs[1][:, cols] = gt.astype(BF16)
        outs[2][:, cols] = up.astype(BF16)

    gws, gspecs, gdots = _wide_nn_weights(wg_gate, "col", b, 1)
    uws, uspecs, udots = _wide_nn_weights(wg_up, "col", b, 1 + len(gws))
    out = pl.BlockSpec((tm, b), lambda i, j: (i, j))
    return _matmul_wide(name, (m // tm, 4), [y] + gws + uws,
                        [pl.BlockSpec((tm, kdim), lambda i, j: (i, 0))] + gspecs + uspecs, [gdots, udots], b, ep,
                        [out] * 3, [jax.ShapeDtypeStruct((m, 4 * b), BF16)] * 3)


def _swiglu_bwd(name, dh, wg_down, gate, up, tm):
    m, n = dh.shape
    _, a, b = wg_down.shape
    tm = min(tm, m)

    def ep(accs, ins, outs, cols):
        d_act = accs[0]
        gt = ins[3][:, cols].astype(F32)
        upv = ins[4][:, cols].astype(F32)
        sg = _sigmoid(gt)
        outs[0][:, cols] = (d_act * upv * sg * (1.0 + gt * (1.0 - sg))).astype(BF16)
        outs[1][:, cols] = (d_act * gt * sg).astype(BF16)
        outs[2][:, cols] = (gt * sg * upv).astype(BF16)

    ws, wspecs, dots = _wide_nt_weights(wg_down, "row", a, 1)
    tile = pl.BlockSpec((tm, a), lambda i, j: (i, j))
    return _matmul_wide(name, (m // tm, 4), [dh] + ws + [gate, up],
                        [pl.BlockSpec((tm, n), lambda i, j: (i, 0))] + wspecs + [tile, tile], [dots], a, ep,
                        [tile] * 3, [jax.ShapeDtypeStruct((m, 4 * a), BF16)] * 3)


def _ffn_dy(name, d_gate, d_up, wg_gate, wg_up, tiles, after=()):
    m, n = d_gate.shape
    _, a, b = wg_gate.shape
    kdim = 2 * a
    tm, to, tc = tiles
    tm = min(tm, m)
    grid = (m // tm, kdim // to, n // tc)
    lhs = pl.BlockSpec((tm, tc), lambda i, j, k: (i, k))
    wspec = _w_spec_nt(wg_gate, "col", to, tc)
    return _matmul(name, grid, [d_gate, d_up, wg_gate, wg_up], [lhs, lhs, wspec, wspec],
                   [(0, 2, 0, NT), (1, 3, 0, NT)], 1, (tm, to), _ep_store(BF16),
                   [pl.BlockSpec((tm, to), lambda i, j, k: (i, j))], [jax.ShapeDtypeStruct((m, kdim), BF16)],
                   after=after)[0]


def _ple_fwd(name, y, wg, h, pp, tm, tn, after=()):
    m, d = h.shape
    tm = min(tm, m)

    def ep(accs, ins, outs, cols):
        pg = accs[0]
        outs[0][:, cols] = ins[5][:, cols] + _sigmoid(pg) * ins[6][:, cols].astype(F32)
        outs[1][:, cols] = pg.astype(BF16)

    tile = pl.BlockSpec((tm, tn), lambda i, j: (i, j))
    return _mm_nn_wide(name, y, wg, "row", tm, tn, epilogue=ep, extra=[h, pp], extra_specs=[tile, tile],
                       out_specs=[tile, tile],
                       out_shape=[jax.ShapeDtypeStruct((m, d), F32), jax.ShapeDtypeStruct((m, d), BF16)], after=after)


def _ple_bwd(name, dh, pg, pp, after=()):
    s, d = dh.shape
    tr = _row_tile(s, 512)

    def body(dh_ref, pg_ref, pp_ref, *rest):
        dpp_ref, dpg_ref = rest[-2:]
        dhv = dh_ref[...]
        sg = _sigmoid(pg_ref[...].astype(F32))
        dpp_ref[...] = (dhv * sg).astype(BF16)
        dpg_ref[...] = (dhv * pp_ref[...].astype(F32) * sg * (1.0 - sg)).astype(BF16)

    row = pl.BlockSpec((tr, d), lambda i: (i, 0))
    return pl.pallas_call(
        body, name=name, grid=(s // tr,), in_specs=[row] * 3 + [ANY_SPEC] * len(after), out_specs=[row] * 2,
        out_shape=[jax.ShapeDtypeStruct((s, d), BF16)] * 2, compiler_params=_cp(("parallel",)),
    )(dh, pg, pp, *after)


BIG = ["w_in", "w_out", "w_gate", "w_up", "w_down", "w_ple_gate", "w_ple_proj"]
KIND = {"w_in": "col", "w_out": "row", "w_gate": "col", "w_up": "col", "w_down": "row", "w_ple_gate": "row",
        "w_ple_proj": "col"}
GATHER_GROUPS = (("w_in", "w_out"), ("w_gate", "w_up"), ("w_down", "w_ple_gate", "w_ple_proj"))
RS_GROUPS = (("w_ple_gate", "w_ple_proj", "w_down", "w_gate", "w_up"), ("w_out", "w_in"))
SMALL = ["norm_mix_g", "sgu_ln_g", "sgu_ln_b", "sgu_w", "sgu_b", "sc_conv_w", "cf_conv_w", "cf_conv_b", "cf_ln_g",
         "cf_ln_b", "pool_w", "pool_scale", "norm_ffn_g", "norm_ple_g", "final_norm_g"]
CHIP_SPLIT = ["sc_conv_w", "cf_conv_w"]
WEIGHTS = ['norm_mix_g', 'w_in', 'sgu_ln_g', 'sgu_ln_b', 'sgu_w', 'sgu_b', 'sc_conv_w', 'cf_conv_w', 'cf_conv_b',
           'cf_ln_g', 'cf_ln_b', 'pool_w', 'pool_scale', 'w_out', 'norm_ffn_g', 'w_gate', 'w_up', 'w_down',
           'norm_ple_g', 'w_ple_gate', 'w_ple_proj', 'final_norm_g']


def _tile(n, want):
    if n <= want:
        return n
    t = (want // 128) * 128
    while n % t:
        t -= 128
    return t


def _pack_rows(vecs):
    flat = jnp.concatenate([v.reshape(-1) for v in vecs])
    n = flat.shape[0]
    quantum = PACK_ROWS * 128
    padded = ((n + quantum - 1) // quantum) * quantum
    return jnp.pad(flat, (0, padded - n)).reshape(padded // 128, 128), n


def _unpack(flat, shapes):
    out, off = [], 0
    for shp in shapes:
        size = math.prod(shp)
        out.append(flat[off:off + size].reshape(shp))
        off += size
    return out


def kernel(x, p, norm_mix_g, w_in, sgu_ln_g, sgu_ln_b, sgu_w, sgu_b, sc_conv_w, cf_conv_w, cf_conv_b, cf_ln_g, cf_ln_b, pool_w, pool_scale, w_out, norm_ffn_g, w_gate, w_up, w_down, norm_ple_g, w_ple_gate, w_ple_proj, final_norm_g, loss_target, m_norm_mix_g, m_w_in, m_sgu_ln_g, m_sgu_ln_b, m_sgu_w, m_sgu_b, m_sc_conv_w, m_cf_conv_w, m_cf_conv_b, m_cf_ln_g, m_cf_ln_b, m_pool_w, m_pool_scale, m_w_out, m_norm_ffn_g, m_w_gate, m_w_up, m_w_down, m_norm_ple_g, m_w_ple_gate, m_w_ple_proj, m_final_norm_g, v_norm_mix_g, v_w_in, v_sgu_ln_g, v_sgu_ln_b, v_sgu_w, v_sgu_b, v_sc_conv_w, v_cf_conv_w, v_cf_conv_b, v_cf_ln_g, v_cf_ln_b, v_pool_w, v_pool_scale, v_w_out, v_norm_ffn_g, v_w_gate, v_w_up, v_w_down, v_norm_ple_g, v_w_ple_gate, v_w_ple_proj, v_final_norm_g):
    args = dict(locals())
    w = {n: args[n] for n in WEIGHTS}
    mom = {n: args["m_" + n] for n in WEIGHTS}
    var = {n: args["v_" + n] for n in WEIGHTS}
    depth = w_in.shape[0]
    s, d = x.shape[1], x.shape[2]
    f_dim = 4 * w_gate.shape[2]
    xi, yi, ci = lax.axis_index("x"), lax.axis_index("y"), lax.axis_index("c")
    c_idx = ci.astype(jnp.int32).reshape(1)

    chip_idx = (2 * xi + yi).astype(jnp.int32).reshape(1)

    def start_gathers(l, dep):
        pending = []
        for gi, names in enumerate(GATHER_GROUPS):
            lands = [_cast_into_landing(f"cast_{n}", w[n], l, 0 if KIND[n] == "col" else 1, chip_idx) for n in names]
            pending.append(_gather_start(f"gather_start_{l}_{gi}", lands, dep))
            dep = pending[-1][3]
        return pending, dep

    def receive(l, gi, pending, after):
        send_sems, recv_sems, lands, _ = pending
        lands = _gather_wait(f"gather_wait_{l}_{gi}", send_sems, recv_sems, lands, after)
        return _share_start(f"gather_share_start_{l}_{gi}", lands)

    def complete(l, gi, share, after):
        send_sems, recv_sems, lands, _ = share
        lands = _share_wait(f"gather_share_wait_{l}_{gi}", send_sems, recv_sems, lands, after)
        return dict(zip(GATHER_GROUPS[gi], lands))

    conv_pack = jnp.concatenate([sc_conv_w, cf_conv_w], axis=1)
    taps = conv_pack.shape[1]
    rows_pad = ((depth * taps + 7) // 8) * 8
    conv_rows = jnp.pad(conv_pack.reshape(depth * taps, HEAD), ((0, rows_pad - depth * taps), (0, 0)))
    conv_all = _allgather8("gather_conv_weights", [conv_rows])[0]
    conv_full = conv_all[0::2, :depth * taps].reshape(4, depth, taps, HEAD)
    conv_full = jnp.transpose(conv_full, (1, 2, 0, 3)).reshape(depth, taps, GROUP)
    sc_w_full, cf_w_full = conv_full[:, :3], conv_full[:, 3:]

    pending, token = start_gathers(0, conv_all)
    h = x[0]
    saved = []
    gathered = []
    shares = [None] * len(GATHER_GROUPS)
    for l in range(depth):
        just_in_time = l < 2
        ahead = l + 1 < depth and l + 1 >= 2
        if just_in_time:
            shares[0] = receive(l, 0, pending[0], token if l == 0 else h)
        wg = complete(l, 0, shares[0], h)
        gathered.append(wg)
        started = ()
        this_layer = pending
        if l + 1 < depth:
            pending, token = start_gathers(l + 1, wg["w_in"])
            started = (token,)
        sv = {"h0": h}
        y1 = _rms_fwd("rms_mix", h, norm_mix_g[l:l + 1], after=started)
        z = _mm_nn_wide("mm_in", y1, wg["w_in"], "col", 1024, 1024)[0]
        lg, lb = sgu_ln_g[l][:, None, :], sgu_ln_b[l][:, None, :]
        sb = sgu_b[l][:, :, None]
        oa = _sgu_fwd("sgu_fwd", z, lg, lb, sgu_w[l], sb)
        ob = _shortconv_fwd("shortconv_fwd", z, sc_w_full[l])
        cc = _conformer_conv_fwd("conformer_conv_fwd", z, cf_w_full[l], cf_conv_b[l:l + 1])
        oc = _conformer_ln_fwd("conformer_ln_fwd", cc, cf_ln_g[l:l + 1], cf_ln_b[l:l + 1])
        od = _pool_fwd("pool_fwd", z, pool_w[l], pool_scale[l:l + 1])
        o = jnp.concatenate([oa, ob, oc, od], axis=1)
        h1 = _mm_residual("mm_out", o, wg["w_out"], h, 1024, 1024)
        if just_in_time:
            shares[1] = receive(l, 1, this_layer[1], h1)
        wg.update(complete(l, 1, shares[1], h1))
        y2 = _rms_fwd("rms_ffn", h1, norm_ffn_g[l:l + 1])
        act, gt, up = _swiglu_fwd("mm_swiglu", y2, wg["w_gate"], wg["w_up"], 512)
        if just_in_time:
            shares[2] = receive(l, 2, this_layer[2], act)
        wg.update(complete(l, 2, shares[2], act))
        h2 = _mm_residual("mm_down", act, wg["w_down"], h1, 1024, 512)
        started = ()
        if ahead:
            shares[0] = receive(l + 1, 0, pending[0], h2)
            shares[1] = receive(l + 1, 1, pending[1], shares[0][3])
            started = (shares[1][3],)
        y3 = _rms_fwd("rms_ple", h2, norm_ple_g[l:l + 1], after=started)
        pb = p[l, 0].astype(BF16)
        ptile = pl.BlockSpec((min(1024, s), 512), lambda i, j: (i, j))
        pp = _mm_nn_wide("mm_ple_proj", pb, wg["w_ple_proj"], "col", 1024, 512, epilogue=_wide_store(BF16),
                         out_specs=[ptile], out_shape=[jax.ShapeDtypeStruct((s, d), BF16)])[0]
        started = ()
        if ahead:
            shares[2] = receive(l + 1, 2, pending[2], pp)
            started = (shares[2][3],)
        h3, pg = _ple_fwd("mm_ple_gate", y3, wg["w_ple_gate"], h2, pp, 1024, 1024, after=started)
        sv.update(y1=y1, z=z, cc=cc, o=o, h1=h1, y2=y2, gt=gt, up=up, h2=h2, y3=y3, pb=pb, pp=pp, pg=pg)
        saved.append(sv)
        h = h3

    loss_part, dh, dhb, d_final_g = _loss_head("loss_head", h, final_norm_g[None, :], loss_target[0])

    small_grads = [None] * depth
    where = jnp.stack([2 * xi + yi, 2 * (1 - xi) + yi, 2 * xi + (1 - yi), 2 * (1 - xi) + (1 - yi), ci]).astype(jnp.int32)
    grad_bufs = {n: lax.empty((depth, 2, w[n].shape[1] // 2, w[n].shape[2]), F32) for n in BIG}
    exchanges = [None] * len(RS_GROUPS)
    joins = []
    behind_join = ()

    def halves(g):
        return g.reshape(4, 2, g.shape[1] // 2, g.shape[2])

    def start_exchange(layer, gi, sibling, after):
        send_sems, recv_sems, gs, lands, _ = sibling
        gs, lands = _sibling_wait(f"rs_sibling_wait_{layer}_{gi}", send_sems, recv_sems, gs, lands, after)
        chip_sums = _add_halves("rs_add", gs, lands, c_idx)
        return _chip_exchange_start(f"rs_chips_start_{layer}_{gi}", chip_sums)

    def finish_exchange(layer, gi, after):
        send_sems, recv_sems, parts, lands, _ = exchanges[gi]
        parts, lands = _chip_exchange_wait(f"rs_chips_wait_{layer}_{gi}", send_sems, recv_sems, parts, lands, after)
        names = RS_GROUPS[gi]
        sums = _sum_parts("rs_sum", parts, lands, [grad_bufs[n] for n in names], layer, where)
        grad_bufs.update(zip(names, sums))

    for l in reversed(range(depth)):
        wg = gathered[l]
        sv = saved[l]
        fs = f_dim // 4
        started = () if exchanges[1] is None else (exchanges[1][4],)
        d_pp, d_pg = _ple_bwd("ple_bwd", dh, sv["pg"], sv["pp"], after=started + behind_join)
        g_ple_proj = _mm_tn_wide("dw_ple_proj", sv["pb"], d_pp, "col", w_ple_proj.shape[1], 512)
        g_ple_gate = _mm_tn_wide("dw_ple_gate", sv["y3"], d_pg, "row", 512, 1024)
        dy3 = _mm_nt_wide("dx_ple_gate", d_pg, wg["w_ple_gate"], "row", 1024, 512)
        dh, dhb, dg_ple = _rms_bwd("rms_ple_bwd", sv["h2"], norm_ple_g[l:l + 1], dy3, dh)

        d_gt, d_up, act = _swiglu_bwd("dx_down_swiglu", dhb, wg["w_down"], sv["gt"], sv["up"], 512)
        g_down = _mm_tn_wide("dw_down", act, dhb, "row", fs, 512)
        g_gate = _mm_tn_wide("dw_gate", sv["y2"], d_gt, "col", 512, fs)
        g_up = _mm_tn_wide("dw_up", sv["y2"], d_up, "col", 512, fs)
        big = dict(w_gate=g_gate, w_up=g_up, w_down=g_down, w_ple_gate=g_ple_gate, w_ple_proj=g_ple_proj)
        sibling = _sibling_start(f"rs_sibling_start_{l}_0", [halves(big[n]) for n in RS_GROUPS[0]])
        dy2 = _ffn_dy("dx_gate_up", d_gt, d_up, wg["w_gate"], wg["w_up"], (1024, 1024, fs), after=(sibling[4],))
        dh, dhb, dg_ffn = _rms_bwd("rms_ffn_bwd", sv["h1"], norm_ffn_g[l:l + 1], dy2, dh)

        g_out = _mm_tn_wide("dw_out", sv["o"], dhb, "row", 512, 1024)
        if exchanges[0] is not None:
            finish_exchange(l + 1, 0, g_out)
        exchanges[0] = start_exchange(l, 0, sibling, g_out)
        d_o = _mm_nt_wide("dx_out", dhb, wg["w_out"], "row", 1024, 512, after=(exchanges[0][4],))
        z = sv["z"]
        lg, lb = sgu_ln_g[l][:, None, :], sgu_ln_b[l][:, None, :]
        sb = sgu_b[l][:, :, None]
        dzu, dzv, d_lg, d_lb, d_sw, d_sb = _sgu_bwd("sgu_bwd", z, d_o[:, 0:GROUP], lg, lb, sgu_w[l], sb)
        dzh, dzbg, dzcg, d_scw = _shortconv_bwd("shortconv_bwd", z, d_o[:, GROUP:2 * GROUP], sc_w_full[l])
        dcc, d_cflg, d_cflb, d_cfb = _conformer_ln_bwd("conformer_ln_bwd", sv["cc"], d_o[:, 2 * GROUP:3 * GROUP],
                                                       cf_ln_g[l:l + 1], cf_ln_b[l:l + 1])
        dza, dzg, d_cfw = _conformer_conv_bwd("conformer_conv_bwd", z, dcc, cf_w_full[l])
        dzd, d_pw, d_psc = _pool_bwd("pool_bwd", z, d_o[:, 3 * GROUP:], pool_w[l], pool_scale[l:l + 1])
        dz = jnp.concatenate([dzu, dzv, dzh, dzbg, dzcg, dza, dzg, dzd], axis=1)
        g_in = _mm_tn_wide("dw_in", sv["y1"], dz, "col", 512, 1024)
        big.update(w_out=g_out, w_in=g_in)
        sibling = _sibling_start(f"rs_sibling_start_{l}_1", [halves(big[n]) for n in RS_GROUPS[1]])
        dy1 = _mm_nt_wide("dx_in", dz, wg["w_in"], "col", 1024, 512, after=(sibling[4],))
        dh, dhb, dg_mix = _rms_bwd("rms_mix_bwd", sv["h0"], norm_mix_g[l:l + 1], dy1, dh)
        if exchanges[1] is not None:
            finish_exchange(l + 1, 1, dh)
            send_sems, recv_sems, bufs, join_token = _join_start(f"rs_join_start_{l + 1}",
                                                                 [grad_bufs[n] for n in BIG], l + 1)
            grad_bufs.update(zip(BIG, bufs))
            joins.append((l + 1, send_sems, recv_sems))
            behind_join = (join_token,)
        exchanges[1] = start_exchange(l, 1, sibling, dh)

        small_grads[l] = dict(norm_mix_g=dg_mix, sgu_ln_g=d_lg, sgu_ln_b=d_lb, sgu_w=d_sw, sgu_b=d_sb,
                              sc_conv_w=d_scw, cf_conv_w=d_cfw, cf_conv_b=d_cfb, cf_ln_g=d_cflg, cf_ln_b=d_cflb,
                              pool_w=d_pw, pool_scale=d_psc, norm_ffn_g=dg_ffn, norm_ple_g=dg_ple)
    grad_x = dh[None]

    grads, delta, new_m, new_v = {}, {}, {}, {}
    per_layer = [n for n in SMALL if n != "final_norm_g"]
    vecs = [small_grads[l][n] for n in per_layer for l in range(depth)] + [d_final_g, loss_part]
    packed, _ = _pack_rows(vecs)
    total = _sum8("sum_small", _allgather8("gather_small", [packed])[0]).reshape(-1)
    stacked_shapes = [(depth,) + (w[n].shape[1:] if n not in CHIP_SPLIT else (w[n].shape[1], GROUP)) for n in per_layer]
    pieces = _unpack(total, stacked_shapes + [(d,), ()])
    loss = pieces[-1]
    grads["final_norm_g"] = pieces[-2]
    chip_off = (2 * xi + yi) * HEAD
    for n, g in zip(per_layer, pieces):
        grads[n] = lax.dynamic_slice_in_dim(g, chip_off, HEAD, axis=2) if n in CHIP_SPLIT else g

    finish_exchange(0, 0, dh)
    for layer, send_sems, recv_sems in joins:
        bufs = _join_wait(f"rs_join_wait_{layer}", send_sems, recv_sems, [grad_bufs[n] for n in BIG], layer, total)
        grad_bufs.update(zip(BIG, bufs))
    behind = (exchanges[1][4],)
    for gi, names in enumerate(RS_GROUPS):
        if gi == 1:
            finish_exchange(0, 1, behind[0])
        joined = _sibling_join_halves(f"rs_join_{gi}", [grad_bufs[n] for n in names], 0, after=behind)
        for n, g in zip(names, joined):
            shp = w[n].shape
            two_d = (shp[0] * shp[1], shp[2])
            grads[n] = g.reshape(shp)
            dl, mn, vn = _adamw(f"adamw_{n}", w[n].reshape(two_d), g.reshape(two_d), mom[n].reshape(two_d),
                                var[n].reshape(two_d))
            delta[n], new_m[n], new_v[n] = dl.reshape(shp), mn.reshape(shp), vn.reshape(shp)
            behind = (dl,)
    small_shapes = [w[n].shape for n in SMALL]
    pw, _ = _pack_rows([w[n] for n in SMALL])
    pg_, _ = _pack_rows([grads[n] for n in SMALL])
    pm, _ = _pack_rows([mom[n] for n in SMALL])
    pv, _ = _pack_rows([var[n] for n in SMALL])
    dl, mn, vn = _adamw("adamw_small", pw, pg_, pm, pv)
    for n, a, b, cc_ in zip(SMALL, _unpack(dl.reshape(-1), small_shapes), _unpack(mn.reshape(-1), small_shapes),
                            _unpack(vn.reshape(-1), small_shapes)):
        delta[n], new_m[n], new_v[n] = a, b, cc_

    return (loss, grad_x, *[grads[n] for n in WEIGHTS], *[delta[n] for n in WEIGHTS],
            *[new_m[n] for n in WEIGHTS], *[new_v[n] for n in WEIGHTS])
```

```python
import functools
import math

import jax
import jax.numpy as jnp
from jax import lax
from jax.experimental import pallas as pl
from jax.experimental.pallas import tpu as pltpu

F32 = jnp.float32
BF16 = jnp.bfloat16
MESH = pl.DeviceIdType.MESH

HEAD = 128
GROUP = 512
EPS = 1e-6
HALO = 32
CHUNK_ROWS = 128
POOL_LEVELS = 4
PACK_ROWS = 512

ADAM_LR = 0.001
ADAM_B1 = 0.9
ADAM_B2 = 0.999
ADAM_EPS = 1e-08
ADAM_WD = 0.01
ADAM_STEP = 10

VMEM_LIMIT = 56 * 1024 * 1024


def _cp(sem=None, vmem=VMEM_LIMIT):
    return pltpu.CompilerParams(dimension_semantics=sem, vmem_limit_bytes=vmem)


def _sigmoid(x):
    return 0.5 * jnp.tanh(0.5 * x) + 0.5


_GELU_K = math.sqrt(2.0 / math.pi)
_GELU_C = 0.044715


def _gelu(x):
    t = jnp.tanh(_GELU_K * (x + _GELU_C * x * x * x))
    return 0.5 * x * (1.0 + t)


def _gelu_grad(x):
    t = jnp.tanh(_GELU_K * (x + _GELU_C * x * x * x))
    return 0.5 * (1.0 + t) + 0.5 * x * (1.0 - t * t) * _GELU_K * (1.0 + 3.0 * _GELU_C * x * x)


def _mesh_pos():
    return lax.axis_index("x"), lax.axis_index("y"), lax.axis_index("c")


def _any_specs(n):
    return [pl.BlockSpec(memory_space=pl.ANY)] * n


def _allgather8(name, blocks):
    n = len(blocks)

    def body(*refs):
        ins, outs = refs[:n], refs[n:2 * n]
        send_sems, recv_sems, local_sems = refs[2 * n:]
        x, y, c = _mesh_pos()
        me, sibling = (x, y, c), (x, y, 1 - c)
        chips = [(1 - x, y), (x, 1 - y), (1 - x, 1 - y)]

        def slot(a, dev):
            return outs[a].at[4 * dev[0] + 2 * dev[1] + dev[2]]

        def copy(a, k, block, to, src=None):
            dst = slot(a, block)
            return pltpu.make_async_remote_copy(
                src_ref=dst if src is None else src, dst_ref=dst,
                send_sem=send_sems.at[7 * a + k], recv_sem=recv_sems.at[7 * a + k],
                device_id=to, device_id_type=MESH)

        mine, first, passed = [], [], []
        for a in range(n):
            cp = pltpu.make_async_copy(ins[a], slot(a, me), local_sems.at[a])
            cp.start()
            mine.append(cp)
            cps = [copy(a, 0, me, sibling, src=ins[a])]
            cps += [copy(a, 1 + j, me, (*chip, c), src=ins[a]) for j, chip in enumerate(chips)]
            for cp in cps:
                cp.start()
            first += cps
        for j, chip in enumerate(chips):
            for a in range(n):
                copy(a, 1 + j, (*chip, c), me).wait_recv()
                cp = copy(a, 4 + j, (*chip, c), sibling)
                cp.start()
                passed.append(cp)
        for a in range(n):
            copy(a, 0, sibling, me).wait_recv()
            for j, chip in enumerate(chips):
                copy(a, 4 + j, (*chip, 1 - c), me).wait_recv()
        for cp in first + passed:
            cp.wait_send()
        for cp in mine:
            cp.wait()

    return pl.pallas_call(
        body, name=name,
        out_shape=[jax.ShapeDtypeStruct((8,) + b.shape, b.dtype) for b in blocks],
        in_specs=_any_specs(n), out_specs=_any_specs(n),
        scratch_shapes=[pltpu.SemaphoreType.DMA((7 * n,)), pltpu.SemaphoreType.DMA((7 * n,)),
                        pltpu.SemaphoreType.DMA((n,))],
    )(*blocks)


HBM_SPEC = pl.BlockSpec(memory_space=pltpu.HBM)
SEM_SPEC = pl.BlockSpec(memory_space=pltpu.SEMAPHORE)
ANY_SPEC = pl.BlockSpec(memory_space=pl.ANY)
SPLIT_COPY = pltpu.CompilerParams(has_side_effects=pltpu.SideEffectType.DATAFLOW_SIDE_EFFECTING)


def _hbm(x):
    return pltpu.with_memory_space_constraint(x, pltpu.HBM)


def _other_chips(x, y):
    return [(1 - x, y), (x, 1 - y), (1 - x, 1 - y)]


def _dev_slot(ref, dev):
    return ref.at[4 * dev[0] + 2 * dev[1] + dev[2]]


def _cast_into_landing(name, w, layer, ax, chip_idx):
    _, r, cc = w.shape
    r2, c2 = (r // 2, cc) if ax == 0 else (r, cc // 2)
    tr = _fit_rows(r2, c2, 512 * 1024)
    nt = r2 // tr

    def body(chip_ref, w_ref, o_ref):
        o_ref[...] = w_ref[...].astype(BF16)

    if ax == 0:
        in_spec = pl.BlockSpec((None, tr, c2), lambda hf, i, chip_ref: (layer, hf * nt + i, 0))
    else:
        in_spec = pl.BlockSpec((None, tr, c2), lambda hf, i, chip_ref: (layer, i, hf))
    return pl.pallas_call(
        body, name=name,
        grid_spec=pltpu.PrefetchScalarGridSpec(
            num_scalar_prefetch=1, grid=(2, nt), in_specs=[in_spec],
            out_specs=pl.BlockSpec((None, tr, c2), lambda hf, i, chip_ref: (2 * chip_ref[0] + hf, i, 0))),
        out_shape=jax.ShapeDtypeStruct((8, r2, c2), BF16), compiler_params=_cp(("parallel", "parallel")),
    )(chip_idx, w)


def _gather_ici_copies(lands, send_sems, recv_sems):
    x, y, c = _mesh_pos()
    pairs = []
    for a in range(len(lands)):
        for j, chip in enumerate(_other_chips(x, y)):
            def copy(dev):
                return pltpu.make_async_remote_copy(
                    src_ref=_dev_slot(lands[a], dev), dst_ref=_dev_slot(lands[a], dev),
                    send_sem=send_sems.at[3 * a + j], recv_sem=recv_sems.at[3 * a + j],
                    device_id=(*chip, c), device_id_type=MESH)
            pairs.append((copy((x, y, c)), copy((*chip, c))))
    return pairs


def _exchange_ici_copies(ins, lands, send_sems, recv_sems):
    x, y, c = _mesh_pos()
    pairs = []
    for a in range(len(ins)):
        for k, chip in enumerate(_other_chips(x, y)):
            there = 2 * chip[0] + chip[1]
            def copy(dst_entry):
                return pltpu.make_async_remote_copy(
                    src_ref=ins[a].at[there], dst_ref=lands[a].at[dst_entry],
                    send_sem=send_sems.at[3 * a + k], recv_sem=recv_sems.at[3 * a + k],
                    device_id=(*chip, c), device_id_type=MESH)
            pairs.append((copy(2 * x + y), copy(there)))
    return pairs


def _gather_start(name, lands, dep):
    n = len(lands)

    def body(*refs):
        token = refs[-1]
        for send, _ in _gather_ici_copies(refs[:n], refs[n + 1], refs[n + 2]):
            send.start()
        token[...] = jnp.zeros_like(token)

    out = pl.pallas_call(
        body, name=name,
        out_shape=(pltpu.SemaphoreType.DMA((3 * n,)), pltpu.SemaphoreType.DMA((3 * n,)),
                   *[pltpu.HBM(l.shape, l.dtype) for l in lands], jax.ShapeDtypeStruct((8, 128), F32)),
        in_specs=[HBM_SPEC] * n + [ANY_SPEC],
        out_specs=(SEM_SPEC, SEM_SPEC, *[HBM_SPEC] * n, pl.BlockSpec(memory_space=pltpu.VMEM)),
        input_output_aliases={a: 2 + a for a in range(n)},
        compiler_params=SPLIT_COPY,
    )(*[_hbm(l) for l in lands], dep)
    return out[0], out[1], list(out[2:2 + n]), out[-1]


def _gather_wait(name, send_sems, recv_sems, lands, after):
    n = len(lands)

    def body(*refs):
        for send, recv in _gather_ici_copies(refs[:n], refs[n], refs[n + 1]):
            send.wait_send()
            recv.wait_recv()

    out = pl.pallas_call(
        body, name=name,
        out_shape=[pltpu.HBM(l.shape, l.dtype) for l in lands],
        in_specs=[HBM_SPEC] * n + [SEM_SPEC, SEM_SPEC, ANY_SPEC],
        out_specs=[HBM_SPEC] * n,
        input_output_aliases={a: a for a in range(n)},
        compiler_params=SPLIT_COPY,
    )(*lands, send_sems, recv_sems, after)
    return list(out)


def _share_copies(lands, send_sems, recv_sems):
    x, y, c = _mesh_pos()
    pairs = []
    for a in range(len(lands)):
        for j, chip in enumerate(_other_chips(x, y)):
            def copy(dev):
                slot = _dev_slot(lands[a], dev)
                return pltpu.make_async_remote_copy(
                    src_ref=slot, dst_ref=slot, send_sem=send_sems.at[3 * a + j], recv_sem=recv_sems.at[3 * a + j],
                    device_id=(x, y, 1 - c), device_id_type=MESH)
            pairs.append((copy((*chip, c)), copy((*chip, 1 - c))))
    return pairs


def _share_start(name, lands):
    n = len(lands)

    def body(*refs):
        token = refs[-1]
        for send, _ in _share_copies(refs[:n], refs[n], refs[n + 1]):
            send.start()
        token[...] = jnp.zeros_like(token)

    out = pl.pallas_call(
        body, name=name,
        out_shape=(pltpu.SemaphoreType.DMA((3 * n,)), pltpu.SemaphoreType.DMA((3 * n,)),
                   *[pltpu.HBM(l.shape, l.dtype) for l in lands], jax.ShapeDtypeStruct((8, 128), F32)),
        in_specs=[HBM_SPEC] * n,
        out_specs=(SEM_SPEC, SEM_SPEC, *[HBM_SPEC] * n, pl.BlockSpec(memory_space=pltpu.VMEM)),
        input_output_aliases={a: 2 + a for a in range(n)},
        compiler_params=SPLIT_COPY,
    )(*lands)
    return out[0], out[1], list(out[2:2 + n]), out[-1]


def _share_wait(name, send_sems, recv_sems, lands, after):
    n = len(lands)

    def body(*refs):
        for send, recv in _share_copies(refs[:n], refs[n], refs[n + 1]):
            send.wait_send()
            recv.wait_recv()

    out = pl.pallas_call(
        body, name=name,
        out_shape=[pltpu.HBM(l.shape, l.dtype) for l in lands],
        in_specs=[HBM_SPEC] * n + [SEM_SPEC, SEM_SPEC, ANY_SPEC],
        out_specs=[HBM_SPEC] * n,
        input_output_aliases={a: a for a in range(n)},
        compiler_params=SPLIT_COPY,
    )(*lands, send_sems, recv_sems, after)
    return list(out)


def _sibling_copies(ins, lands, send_sems, recv_sems):
    x, y, c = _mesh_pos()
    return [pltpu.make_async_remote_copy(
        src_ref=ins[a].at[j, 1 - c], dst_ref=lands[a].at[j],
        send_sem=send_sems.at[4 * a + j], recv_sem=recv_sems.at[4 * a + j],
        device_id=(x, y, 1 - c), device_id_type=MESH) for a in range(len(ins)) for j in range(4)]


def _sibling_start(name, grads):
    n = len(grads)

    def body(*refs):
        token = refs[-1]
        for cp in _sibling_copies(refs[:n], refs[n:2 * n], refs[2 * n], refs[2 * n + 1]):
            cp.start()
        token[...] = jnp.zeros_like(token)

    lands = [_hbm(lax.empty((4,) + g.shape[2:], g.dtype)) for g in grads]
    out = pl.pallas_call(
        body, name=name,
        out_shape=(pltpu.SemaphoreType.DMA((4 * n,)), pltpu.SemaphoreType.DMA((4 * n,)),
                   *[pltpu.HBM(g.shape, g.dtype) for g in grads], *[pltpu.HBM(l.shape, l.dtype) for l in lands],
                   jax.ShapeDtypeStruct((8, 128), F32)),
        in_specs=[HBM_SPEC] * (2 * n),
        out_specs=(SEM_SPEC, SEM_SPEC, *[HBM_SPEC] * (2 * n), pl.BlockSpec(memory_space=pltpu.VMEM)),
        input_output_aliases={i: 2 + i for i in range(2 * n)},
        compiler_params=SPLIT_COPY,
    )(*[_hbm(g) for g in grads], *lands)
    return out[0], out[1], list(out[2:2 + n]), list(out[2 + n:2 + 2 * n]), out[-1]


def _sibling_wait(name, send_sems, recv_sems, grads, lands, after):
    n = len(grads)

    def body(*refs):
        for cp in _sibling_copies(refs[:n], refs[n:2 * n], refs[2 * n], refs[2 * n + 1]):
            cp.wait_send()
            cp.wait_recv()

    out = pl.pallas_call(
        body, name=name,
        out_shape=[pltpu.HBM(g.shape, g.dtype) for g in grads] + [pltpu.HBM(l.shape, l.dtype) for l in lands],
        in_specs=[HBM_SPEC] * (2 * n) + [SEM_SPEC, SEM_SPEC, ANY_SPEC],
        out_specs=[HBM_SPEC] * (2 * n),
        input_output_aliases={i: i for i in range(2 * n)},
        compiler_params=SPLIT_COPY,
    )(*grads, *lands, send_sems, recv_sems, after)
    return list(out[:n]), list(out[n:])


def _chip_exchange_start(name, parts):
    n = len(parts)

    def body(*refs):
        ins, lands = refs[:n], refs[n:2 * n]
        token = refs[-1]
        for send, _ in _exchange_ici_copies(ins, lands, refs[2 * n], refs[2 * n + 1]):
            send.start()
        token[...] = jnp.zeros_like(token)

    lands = [_hbm(lax.empty(p.shape, p.dtype)) for p in parts]
    out = pl.pallas_call(
        body, name=name,
        out_shape=(pltpu.SemaphoreType.DMA((3 * n,)), pltpu.SemaphoreType.DMA((3 * n,)),
                   *[pltpu.HBM(p.shape, p.dtype) for p in parts], *[pltpu.HBM(p.shape, p.dtype) for p in parts],
                   jax.ShapeDtypeStruct((8, 128), F32)),
        in_specs=[HBM_SPEC] * (2 * n),
        out_specs=(SEM_SPEC, SEM_SPEC, *[HBM_SPEC] * (2 * n), pl.BlockSpec(memory_space=pltpu.VMEM)),
        input_output_aliases={i: 2 + i for i in range(2 * n)},
        compiler_params=SPLIT_COPY,
    )(*[_hbm(p) for p in parts], *lands)
    return out[0], out[1], list(out[2:2 + n]), list(out[2 + n:2 + 2 * n]), out[-1]


def _chip_exchange_wait(name, send_sems, recv_sems, parts, lands, after):
    n = len(parts)

    def body(*refs):
        ins, lands_in = refs[:n], refs[n:2 * n]
        for send, recv in _exchange_ici_copies(ins, lands_in, refs[2 * n], refs[2 * n + 1]):
            send.wait_send()
            recv.wait_recv()

    out = pl.pallas_call(
        body, name=name,
        out_shape=[pltpu.HBM(p.shape, p.dtype) for p in parts] * 2,
        in_specs=[HBM_SPEC] * (2 * n) + [SEM_SPEC, SEM_SPEC, ANY_SPEC],
        out_specs=[HBM_SPEC] * (2 * n),
        input_output_aliases={i: i for i in range(2 * n)},
        compiler_params=SPLIT_COPY,
    )(*parts, *lands, send_sems, recv_sems, after)
    return list(out[:n]), list(out[n:])


def _join_copies(bufs, send_sems, recv_sems, layer):
    x, y, c = _mesh_pos()

    def copy(a, half):
        return pltpu.make_async_remote_copy(
            src_ref=bufs[a].at[layer, half], dst_ref=bufs[a].at[layer, half],
            send_sem=send_sems.at[a], recv_sem=recv_sems.at[a], device_id=(x, y, 1 - c), device_id_type=MESH)

    return [(copy(a, c), copy(a, 1 - c)) for a in range(len(bufs))]


def _sibling_join_halves(name, bufs, layer, after=()):
    n = len(bufs)

    def body(*refs):
        outs = refs[n + len(after):2 * n + len(after)]
        pairs = _join_copies(outs, refs[-2], refs[-1], layer)
        for send, _ in pairs:
            send.start()
        for send, recv in pairs:
            send.wait_send()
            recv.wait_recv()

    return pl.pallas_call(
        body, name=name, out_shape=[jax.ShapeDtypeStruct(b.shape, b.dtype) for b in bufs],
        in_specs=_any_specs(n + len(after)), out_specs=_any_specs(n),
        input_output_aliases={a: a for a in range(n)},
        scratch_shapes=[pltpu.SemaphoreType.DMA((n,)), pltpu.SemaphoreType.DMA((n,))],
    )(*bufs, *after)


def _join_start(name, bufs, layer):
    n = len(bufs)

    def body(*refs):
        token = refs[-1]
        for send, _ in _join_copies(refs[:n], refs[n], refs[n + 1], layer):
            send.start()
        token[...] = jnp.zeros_like(token)

    out = pl.pallas_call(
        body, name=name,
        out_shape=(pltpu.SemaphoreType.DMA((n,)), pltpu.SemaphoreType.DMA((n,)),
                   *[pltpu.HBM(b.shape, b.dtype) for b in bufs], jax.ShapeDtypeStruct((8, 128), F32)),
        in_specs=[HBM_SPEC] * n,
        out_specs=(SEM_SPEC, SEM_SPEC, *[HBM_SPEC] * n, pl.BlockSpec(memory_space=pltpu.VMEM)),
        input_output_aliases={a: 2 + a for a in range(n)},
        compiler_params=SPLIT_COPY,
    )(*[_hbm(b) for b in bufs])
    return out[0], out[1], list(out[2:2 + n]), out[-1]


def _join_wait(name, send_sems, recv_sems, bufs, layer, after):
    n = len(bufs)

    def body(*refs):
        for send, recv in _join_copies(refs[:n], refs[n], refs[n + 1], layer):
            send.wait_send()
            recv.wait_recv()

    out = pl.pallas_call(
        body, name=name,
        out_shape=[pltpu.HBM(b.shape, b.dtype) for b in bufs],
        in_specs=[HBM_SPEC] * n + [SEM_SPEC, SEM_SPEC, ANY_SPEC],
        out_specs=[HBM_SPEC] * n,
        input_output_aliases={a: a for a in range(n)},
        compiler_params=SPLIT_COPY,
    )(*bufs, send_sems, recv_sems, after)
    return list(out)


def _row_tile(rows, want):
    t = 1
    while t * 2 <= min(rows, want):
        t *= 2
    while rows % t:
        t //= 2
    return t


def _fit_rows(rows, cols, max_elems):
    for k in range(1, rows + 1):
        if rows % k == 0 and (rows // k) % 16 == 0 and (rows // k) * cols <= max_elems:
            return rows // k
    return _row_tile(rows, max(8, max_elems // cols))


def _add_halves(name, grads, recvd, c_idx):
    outs = []
    for a, (g, r) in enumerate(zip(grads, recvd)):
        _, _, r2, cc = g.shape
        tr = _fit_rows(r2, cc, 1536 * 1024)

        def body(c_ref, g_ref, r_ref, o_ref):
            o_ref[...] = (g_ref[...].astype(F32) + r_ref[...].astype(F32)).astype(BF16)

        outs.append(pl.pallas_call(
            body, name=f"{name}_{a}",
            grid_spec=pltpu.PrefetchScalarGridSpec(
                num_scalar_prefetch=1, grid=(4, r2 // tr),
                in_specs=[pl.BlockSpec((None, None, tr, cc), lambda j, i, c_ref: (j, c_ref[0], i, 0)),
                          pl.BlockSpec((None, tr, cc), lambda j, i, c_ref: (j, i, 0))],
                out_specs=pl.BlockSpec((None, tr, cc), lambda j, i, c_ref: (j, i, 0))),
            out_shape=jax.ShapeDtypeStruct(r.shape, BF16), compiler_params=_cp(("parallel", "parallel")),
        )(c_idx, g, r))
    return outs


def _sum_parts(name, parts, recvd, bufs, layer, where):
    outs = []
    for a, (p, r, buf) in enumerate(zip(parts, recvd, bufs)):
        _, r2, cc = p.shape
        tr = _fit_rows(r2, cc, 768 * 1024)

        def body(where_ref, own_ref, r1_ref, r2_ref, r3_ref, buf_ref, o_ref):
            acc = own_ref[...].astype(F32)
            for ref in (r1_ref, r2_ref, r3_ref):
                acc = acc + ref[...].astype(F32)
            o_ref[...] = acc

        def entry(k):
            return pl.BlockSpec((None, tr, cc), lambda i, where_ref: (where_ref[k], i, 0))

        outs.append(pl.pallas_call(
            body, name=f"{name}_{a}",
            grid_spec=pltpu.PrefetchScalarGridSpec(
                num_scalar_prefetch=1, grid=(r2 // tr,),
                in_specs=[entry(0), entry(1), entry(2), entry(3), ANY_SPEC],
                out_specs=pl.BlockSpec((None, None, tr, cc), lambda i, where_ref: (layer, where_ref[4], i, 0))),
            out_shape=jax.ShapeDtypeStruct(buf.shape, F32), input_output_aliases={5: 0},
            compiler_params=_cp(("parallel",)),
        )(where, p, r, r, r, buf))
    return outs


def _sum8(name, gathered):
    _, r, cc = gathered.shape
    tr = _row_tile(r, 512)

    def body(g_ref, o_ref):
        acc = g_ref[0].astype(F32)
        for i in range(1, 8):
            acc = acc + g_ref[i].astype(F32)
        o_ref[...] = acc

    return pl.pallas_call(
        body, name=name, grid=(r // tr,),
        in_specs=[pl.BlockSpec((8, tr, cc), lambda i: (0, i, 0))],
        out_specs=pl.BlockSpec((tr, cc), lambda i: (i, 0)),
        out_shape=jax.ShapeDtypeStruct((r, cc), F32), compiler_params=_cp(("parallel",)),
    )(gathered)


def _adamw(name, w, g, m, v):
    r, cc = w.shape
    tr = _row_tile(r, max(8, (512 * 1024) // cc))

    def body(w_ref, g_ref, m_ref, v_ref, d_ref, mo_ref, vo_ref):
        gg = g_ref[...]
        mn = ADAM_B1 * m_ref[...] + (1.0 - ADAM_B1) * gg
        vn = ADAM_B2 * v_ref[...] + (1.0 - ADAM_B2) * (gg * gg)
        m_hat = mn / (1.0 - ADAM_B1 ** ADAM_STEP)
        v_hat = vn / (1.0 - ADAM_B2 ** ADAM_STEP)
        d_ref[...] = -ADAM_LR * (m_hat / (jnp.sqrt(v_hat) + ADAM_EPS) + ADAM_WD * w_ref[...])
        mo_ref[...] = mn
        vo_ref[...] = vn

    spec = pl.BlockSpec((tr, cc), lambda i: (i, 0))
    return pl.pallas_call(
        body, name=name, grid=(r // tr,), in_specs=[spec] * 4, out_specs=[spec] * 3,
        out_shape=[jax.ShapeDtypeStruct(w.shape, F32)] * 3, compiler_params=_cp(("parallel",)),
    )(w, g, m, v)


def _rms_fwd(name, h, g, after=()):
    s, d = h.shape
    tr = _row_tile(s, 512)

    def body(h_ref, g_ref, *rest):
        y_ref = rest[-1]
        xv = h_ref[...]
        r = lax.rsqrt(jnp.mean(xv * xv, axis=-1, keepdims=True) + EPS)
        y_ref[...] = (xv * r * g_ref[...]).astype(BF16)

    return pl.pallas_call(
        body, name=name, grid=(s // tr,),
        in_specs=[pl.BlockSpec((tr, d), lambda i: (i, 0)), pl.BlockSpec((1, d), lambda i: (0, 0))]
        + [ANY_SPEC] * len(after),
        out_specs=pl.BlockSpec((tr, d), lambda i: (i, 0)),
        out_shape=jax.ShapeDtypeStruct((s, d), BF16), compiler_params=_cp(("parallel",)),
    )(h, g, *after)


def _rms_bwd_rows(xv, gv, dy):
    d = xv.shape[-1]
    r = lax.rsqrt(jnp.mean(xv * xv, axis=-1, keepdims=True) + EPS)
    dxn = dy * gv
    proj = jnp.sum(dxn * xv, axis=-1, keepdims=True) * (1.0 / d)
    dx = r * dxn - xv * (r * r * r) * proj
    return dx, dy * xv * r


def _rms_bwd(name, h, g, dy, dres):
    s, d = h.shape
    tr = _row_tile(s, 512)
    band = 16

    def body(h_ref, g_ref, dy_ref, dres_ref, dh_ref, dhb_ref, dg_ref, acc):
        acc[...] = jnp.zeros_like(acc)

        def step(i, carry):
            rows = pl.ds(pl.multiple_of(i * band, band), band)
            dx, dgp = _rms_bwd_rows(h_ref[rows, :], g_ref[...], dy_ref[rows, :].astype(F32))
            dh = dres_ref[rows, :] + dx
            dh_ref[rows, :] = dh
            dhb_ref[rows, :] = dh.astype(BF16)
            acc[...] += _fold8(dgp)
            return carry

        lax.fori_loop(0, tr // band, step, 0, unroll=4)

        @pl.when(pl.program_id(0) == 0)
        def _():
            dg_ref[...] = jnp.zeros_like(dg_ref)

        dg_ref[...] += jnp.sum(acc[...], axis=0, keepdims=True)

    row = pl.BlockSpec((tr, d), lambda i: (i, 0))
    vec = pl.BlockSpec((1, d), lambda i: (0, 0))
    return pl.pallas_call(
        body, name=name, grid=(s // tr,), in_specs=[row, vec, row, row], out_specs=[row, row, vec],
        out_shape=[jax.ShapeDtypeStruct((s, d), F32), jax.ShapeDtypeStruct((s, d), BF16),
                   jax.ShapeDtypeStruct((1, d), F32)],
        scratch_shapes=[pltpu.VMEM((8, d), F32)],
        compiler_params=_cp(("arbitrary",)),
    )(h, g, dy, dres)


def _loss_head(name, h, g, target):
    s, d = h.shape
    tr = _row_tile(s, 256)

    def body(h_ref, g_ref, t_ref, loss_ref, dh_ref, dhb_ref, dg_ref):
        xv = h_ref[...]
        gv = g_ref[...]
        r = lax.rsqrt(jnp.mean(xv * xv, axis=-1, keepdims=True) + EPS)
        err = xv * r * gv - t_ref[...]
        part = 0.5 * jnp.sum(jnp.sum(err * err, axis=-1, keepdims=True) * (1.0 / d), axis=0, keepdims=True)
        dx, dgp = _rms_bwd_rows(xv, gv, err * (1.0 / d))
        dh_ref[...] = dx
        dhb_ref[...] = dx.astype(BF16)

        @pl.when(pl.program_id(0) == 0)
        def _():
            dg_ref[...] = jnp.zeros_like(dg_ref)
            loss_ref[...] = jnp.zeros_like(loss_ref)

        dg_ref[...] += jnp.sum(dgp, axis=0, keepdims=True)
        loss_ref[...] += part

    row = pl.BlockSpec((tr, d), lambda i: (i, 0))
    vec = pl.BlockSpec((1, d), lambda i: (0, 0))
    one = pl.BlockSpec((1, 1), lambda i: (0, 0))
    return pl.pallas_call(
        body, name=name, grid=(s // tr,), in_specs=[row, vec, row], out_specs=[one, row, row, vec],
        out_shape=[jax.ShapeDtypeStruct((1, 1), F32), jax.ShapeDtypeStruct((s, d), F32),
                   jax.ShapeDtypeStruct((s, d), BF16), jax.ShapeDtypeStruct((1, d), F32)],
        compiler_params=_cp(("arbitrary",)),
    )(h, g, target)


NN = (((1,), (0,)), ((), ()))
NT = (((1,), (1,)), ((), ()))
TN = (((0,), (0,)), ((), ()))


def _matmul(name, grid, operands, in_specs, pairs, n_acc, acc_shape, epilogue, out_specs, out_shape, after=()):
    operands = list(operands) + list(after)
    in_specs = list(in_specs) + [ANY_SPEC] * len(after)
    n_in = len(operands)
    n_out = len(out_shape)
    nk = grid[2]

    def body(*refs):
        ins, outs, accs = refs[:n_in], refs[n_in:n_in + n_out], refs[n_in + n_out:]
        k = pl.program_id(2)

        @pl.when(k == 0)
        def _():
            for acc in accs:
                acc[...] = jnp.zeros_like(acc)

        for li, ri, ai, dims in pairs:
            accs[ai][...] += lax.dot_general(ins[li][...], ins[ri][...], dims, preferred_element_type=F32)

        @pl.when(k == nk - 1)
        def _():
            epilogue([acc[...] for acc in accs], ins, outs)

    return pl.pallas_call(
        body, name=name, grid=grid, in_specs=in_specs, out_specs=out_specs, out_shape=out_shape,
        scratch_shapes=[pltpu.VMEM(acc_shape, F32)] * n_acc,
        compiler_params=_cp(("parallel", "parallel", "arbitrary")),
    )(*operands)


def _ep_store(dtype):
    def ep(accs, ins, outs):
        outs[0][...] = accs[0].astype(dtype)
    return ep


def _w_spec_nt(wg, kind, to, tc):
    _, a, b = wg.shape
    if kind == "col":
        oph, cps = a // to, b // tc
        return pl.BlockSpec((None, to, tc), lambda i, jo, kc: (2 * (kc // cps) + jo // oph, jo % oph, kc % cps))
    ops, cph = a // to, b // tc
    return pl.BlockSpec((None, to, tc), lambda i, jo, kc: (2 * (jo // ops) + kc // cph, jo % ops, kc % cph))


SUB_COLS = 256


def _matmul_wide(name, grid, operands, in_specs, groups, tn, epilogue, out_specs, out_shape, after=()):
    operands = list(operands) + list(after)
    in_specs = list(in_specs) + [ANY_SPEC] * len(after)
    n_in = len(operands)

    def body(*refs):
        ins, outs = refs[:n_in], refs[n_in:]
        for s0 in range(0, tn, SUB_COLS):
            cols = slice(s0, min(s0 + SUB_COLS, tn))
            accs = []
            for group in groups:
                acc = None
                for li, (c0, cw), ri, dims in group:
                    rhs = ins[ri][:, cols] if dims == NN else ins[ri][cols, :]
                    part = lax.dot_general(ins[li][:, c0:c0 + cw], rhs, dims, preferred_element_type=F32)
                    acc = part if acc is None else acc + part
                accs.append(acc)
            epilogue(accs, ins, outs, cols)

    return pl.pallas_call(
        body, name=name, grid=grid, in_specs=in_specs, out_specs=out_specs, out_shape=out_shape,
        compiler_params=_cp(("parallel", "parallel")),
    )(*operands)


def _wide_store(dtype):
    def ep(accs, ins, outs, cols):
        outs[0][:, cols] = accs[0].astype(dtype)
    return ep


def _wide_nn_weights(wg, kind, tn, first):
    _, a, b = wg.shape
    per = b // tn
    if kind == "col":
        specs = [pl.BlockSpec((None, a, tn), lambda i, j, hf=hf: (2 * (j // per) + hf, 0, j % per)) for hf in range(2)]
    else:
        specs = [pl.BlockSpec((None, a, tn), lambda i, j, ch=ch: (2 * ch + j // per, 0, j % per)) for ch in range(4)]
    dots = [(0, (q * a, a), first + q, NN) for q in range(len(specs))]
    return [wg] * len(specs), specs, dots


def _wide_nt_weights(wg, kind, to, first):
    _, a, b = wg.shape
    per = a // to
    if kind == "col":
        specs = [pl.BlockSpec((None, to, b), lambda i, j, ch=ch: (2 * ch + j // per, j % per, 0)) for ch in range(4)]
    else:
        specs = [pl.BlockSpec((None, to, b), lambda i, j, hf=hf: (2 * (j // per) + hf, j % per, 0)) for hf in range(2)]
    dots = [(0, (q * b, b), first + q, NT) for q in range(len(specs))]
    return [wg] * len(specs), specs, dots


def _mm_nn_wide(name, x, wg, kind, tm, tn, epilogue=None, extra=(), extra_specs=(), out_specs=None, out_shape=None,
                after=()):
    m, kdim = x.shape
    _, a, b = wg.shape
    n = 4 * b if kind == "col" else 2 * b
    tm = min(tm, m)
    ws, wspecs, dots = _wide_nn_weights(wg, kind, tn, 1)
    if out_shape is None:
        out_shape = [jax.ShapeDtypeStruct((m, n), F32)]
        out_specs = [pl.BlockSpec((tm, tn), lambda i, j: (i, j))]
        epilogue = _wide_store(F32)
    return _matmul_wide(name, (m // tm, n // tn), [x] + ws + list(extra),
                        [pl.BlockSpec((tm, kdim), lambda i, j: (i, 0))] + wspecs + list(extra_specs),
                        [dots], tn, epilogue, out_specs, out_shape, after=after)


def _mm_nt_wide(name, dy, wg, kind, tm, to, out_dtype=BF16, after=()):
    m, n = dy.shape
    _, a, b = wg.shape
    kdim = 2 * a if kind == "col" else 4 * a
    tm = min(tm, m)
    ws, wspecs, dots = _wide_nt_weights(wg, kind, to, 1)
    return _matmul_wide(name, (m // tm, kdim // to), [dy] + ws,
                        [pl.BlockSpec((tm, n), lambda i, j: (i, 0))] + wspecs, [dots], to, _wide_store(out_dtype),
                        [pl.BlockSpec((tm, to), lambda i, j: (i, j))],
                        [jax.ShapeDtypeStruct((m, kdim), out_dtype)], after=after)[0]


def _mm_tn_wide(name, a, dy, kind, tr, tn):
    m, kdim = a.shape
    _, n = dy.shape

    def body(a_ref, dy_ref, o_ref):
        for s0 in range(0, tn, 2 * SUB_COLS):
            cols = slice(s0, min(s0 + 2 * SUB_COLS, tn))
            o_ref[:, cols] = lax.dot_general(a_ref[...], dy_ref[:, cols], TN, preferred_element_type=F32).astype(BF16)

    if kind == "col":
        ns = n // 4
        per = ns // tn
        out_shape = jax.ShapeDtypeStruct((4, kdim, ns), BF16)
        out_spec = pl.BlockSpec((None, tr, tn), lambda r, j: (j // per, r, j % per))
    else:
        rs = kdim // 4
        per = rs // tr
        out_shape = jax.ShapeDtypeStruct((4, rs, n), BF16)
        out_spec = pl.BlockSpec((None, tr, tn), lambda r, j: (r // per, r % per, j))
    return pl.pallas_call(
        body, name=name, grid=(kdim // tr, n // tn),
        in_specs=[pl.BlockSpec((m, tr), lambda r, j: (0, r)), pl.BlockSpec((m, tn), lambda r, j: (0, j))],
        out_specs=out_spec, out_shape=out_shape, compiler_params=_cp(("parallel", "parallel")),
    )(a, dy)


def _zero_halo(pad_ref, s):
    z = jnp.zeros((HALO, pad_ref.shape[1]), F32)
    pad_ref[pl.ds(0, HALO), :] = z
    pad_ref[pl.ds(HALO + s, HALO), :] = z


def _window(pad_ref, r0, rows):
    return pad_ref[pl.ds(r0, rows + 2 * HALO), :]


def _delayed(win, k, rows):
    if k == 0:
        return win[HALO:HALO + rows]
    return pltpu.roll(win, k, axis=0)[HALO:HALO + rows]


def _advanced(win, k, rows):
    if k == 0:
        return win[HALO:HALO + rows]
    return pltpu.roll(win, win.shape[0] - k, axis=0)[HALO:HALO + rows]


def _fold8(x):
    return jnp.sum(x.reshape(x.shape[0] // 8, 8, x.shape[1]), axis=0)


def _chunks(s):
    rows = min(CHUNK_ROWS, s)
    return rows, s // rows


def _col_spec(s, first_block):
    return pl.BlockSpec((s, HEAD), lambda j: (0, first_block + j))


def _sgu_fwd(name, z, ln_g, ln_b, w, b):
    s = z.shape[0]
    tr = _row_tile(s, 1024)
    nh = GROUP // HEAD

    def body(u_ref, v_ref, lg_ref, lb_ref, w_ref, b_ref, o_ref):
        row = lax.broadcasted_iota(jnp.int32, (HEAD, HEAD), 0)
        col = lax.broadcasted_iota(jnp.int32, (HEAD, HEAD), 1)
        wm = jnp.where(row >= col, w_ref[...], 0.0).astype(BF16)
        for ck in range(tr // HEAD):
            rs = pl.ds(ck * HEAD, HEAD)
            u = _gelu(u_ref[rs, :])
            v = _gelu(v_ref[rs, :])
            mu = jnp.mean(v, axis=-1, keepdims=True)
            vc = v - mu
            var = jnp.mean(vc * vc, axis=-1, keepdims=True)
            vln = vc * lax.rsqrt(var + EPS) * lg_ref[...] + lb_ref[...]
            sp = jnp.dot(wm, vln.astype(BF16), preferred_element_type=F32) + b_ref[...]
            o_ref[rs, :] = (u * sp).astype(BF16)

    head_vec = pl.BlockSpec((None, 1, HEAD), lambda h, i: (h, 0, 0))
    return pl.pallas_call(
        body, name=name, grid=(nh, s // tr),
        in_specs=[pl.BlockSpec((tr, HEAD), lambda h, i: (i, h)), pl.BlockSpec((tr, HEAD), lambda h, i: (i, nh + h)),
                  head_vec, head_vec, pl.BlockSpec((None, HEAD, HEAD), lambda h, i: (h, 0, 0)),
                  pl.BlockSpec((None, HEAD, 1), lambda h, i: (h, 0, 0))],
        out_specs=pl.BlockSpec((tr, HEAD), lambda h, i: (i, h)),
        out_shape=jax.ShapeDtypeStruct((s, GROUP), BF16), compiler_params=_cp(("parallel", "parallel")),
    )(z, z, ln_g, ln_b, w, b)


def _sgu_bwd(name, z, d_o, ln_g, ln_b, w, b):
    s = z.shape[0]
    tr = _row_tile(s, 1024)
    nh = GROUP // HEAD

    def body(u_ref, v_ref, do_ref, lg_ref, lb_ref, w_ref, b_ref, du_ref, dv_ref, dlg_ref, dlb_ref, dw_ref, db_ref,
             dsp_acc):
        row = lax.broadcasted_iota(jnp.int32, (HEAD, HEAD), 0)
        col = lax.broadcasted_iota(jnp.int32, (HEAD, HEAD), 1)
        tril = row >= col
        wm = jnp.where(tril, w_ref[...], 0.0).astype(BF16)
        i = pl.program_id(1)

        @pl.when(i == 0)
        def _():
            dlg_ref[...] = jnp.zeros_like(dlg_ref)
            dlb_ref[...] = jnp.zeros_like(dlb_ref)
            dw_ref[...] = jnp.zeros_like(dw_ref)
            dsp_acc[...] = jnp.zeros_like(dsp_acc)

        dlg = jnp.zeros((1, HEAD), F32)
        dlb = jnp.zeros((1, HEAD), F32)
        dw = jnp.zeros((HEAD, HEAD), F32)
        dsp_sum = jnp.zeros((HEAD, HEAD), F32)
        for ck in range(tr // HEAD):
            rs = pl.ds(ck * HEAD, HEAD)
            zu = u_ref[rs, :]
            zv = v_ref[rs, :]
            u = _gelu(zu)
            v = _gelu(zv)
            mu = jnp.mean(v, axis=-1, keepdims=True)
            vc = v - mu
            var = jnp.mean(vc * vc, axis=-1, keepdims=True)
            rstd = lax.rsqrt(var + EPS)
            xh = vc * rstd
            vln = (xh * lg_ref[...] + lb_ref[...]).astype(BF16)
            sp = jnp.dot(wm, vln, preferred_element_type=F32) + b_ref[...]
            d_oa = do_ref[rs, :].astype(F32)
            du = d_oa * sp
            dsp = d_oa * u
            dsp_b = dsp.astype(BF16)
            dvln = lax.dot_general(wm, dsp_b, TN, preferred_element_type=F32)
            dw = dw + lax.dot_general(dsp_b, vln, NT, preferred_element_type=F32)
            dsp_sum = dsp_sum + dsp
            dlg = dlg + jnp.sum(dvln * xh, axis=0, keepdims=True)
            dlb = dlb + jnp.sum(dvln, axis=0, keepdims=True)
            dxh = dvln * lg_ref[...]
            dv = rstd * (dxh - jnp.mean(dxh, axis=-1, keepdims=True)
                         - xh * jnp.mean(dxh * xh, axis=-1, keepdims=True))
            du_ref[rs, :] = (du * _gelu_grad(zu)).astype(BF16)
            dv_ref[rs, :] = (dv * _gelu_grad(zv)).astype(BF16)
        dlg_ref[...] += dlg
        dlb_ref[...] += dlb
        dw_ref[...] += jnp.where(tril, dw, 0.0)
        dsp_acc[...] += dsp_sum

        @pl.when(i == pl.num_programs(1) - 1)
        def _():
            db_ref[...] = jnp.sum(dsp_acc[...], axis=1, keepdims=True)

    head_vec = pl.BlockSpec((None, 1, HEAD), lambda h, i: (h, 0, 0))
    head_mat = pl.BlockSpec((None, HEAD, HEAD), lambda h, i: (h, 0, 0))
    head_col = pl.BlockSpec((None, HEAD, 1), lambda h, i: (h, 0, 0))
    return pl.pallas_call(
        body, name=name, grid=(nh, s // tr),
        in_specs=[pl.BlockSpec((tr, HEAD), lambda h, i: (i, h)), pl.BlockSpec((tr, HEAD), lambda h, i: (i, nh + h)),
                  pl.BlockSpec((tr, HEAD), lambda h, i: (i, h)), head_vec, head_vec, head_mat, head_col],
        out_specs=[pl.BlockSpec((tr, HEAD), lambda h, i: (i, h)), pl.BlockSpec((tr, HEAD), lambda h, i: (i, h)),
                   head_vec, head_vec, head_mat, head_col],
        out_shape=[jax.ShapeDtypeStruct((s, GROUP), BF16), jax.ShapeDtypeStruct((s, GROUP), BF16),
                   jax.ShapeDtypeStruct((nh, 1, HEAD), F32), jax.ShapeDtypeStruct((nh, 1, HEAD), F32),
                   jax.ShapeDtypeStruct((nh, HEAD, HEAD), F32), jax.ShapeDtypeStruct((nh, HEAD, 1), F32)],
        scratch_shapes=[pltpu.VMEM((HEAD, HEAD), F32)],
        compiler_params=_cp(("parallel", "arbitrary")),
    )(z, z, d_o, ln_g, ln_b, w, b)


def _shortconv_fwd(name, z, w):
    s = z.shape[0]
    kw = w.shape[0]
    rows, nchunk = _chunks(s)
    nb = GROUP // HEAD

    def body(h_ref, bg_ref, cg_ref, w_ref, o_ref, pad):
        _zero_halo(pad, s)

        def fill(ci, carry):
            r0 = pl.multiple_of(ci * rows, rows)
            pad[pl.ds(pl.multiple_of(HALO + r0, 8), rows), :] = cg_ref[pl.ds(r0, rows), :] * h_ref[pl.ds(r0, rows), :]
            return carry

        lax.fori_loop(0, nchunk, fill, 0)

        def step(ci, carry):
            r0 = pl.multiple_of(ci * rows, rows)
            win = _window(pad, r0, rows)
            cv = jnp.zeros((rows, HEAD), F32)
            for k in range(kw):
                cv = cv + w_ref[k:k + 1, :] * _delayed(win, kw - 1 - k, rows)
            o_ref[pl.ds(r0, rows), :] = (bg_ref[pl.ds(r0, rows), :] * cv).astype(BF16)
            return carry

        lax.fori_loop(0, nchunk, step, 0)

    return pl.pallas_call(
        body, name=name, grid=(nb,),
        in_specs=[_col_spec(s, 8), _col_spec(s, 12), _col_spec(s, 16), pl.BlockSpec((kw, HEAD), lambda j: (0, j))],
        out_specs=_col_spec(s, 0),
        out_shape=jax.ShapeDtypeStruct((s, GROUP), BF16),
        scratch_shapes=[pltpu.VMEM((s + 2 * HALO, HEAD), F32)],
        compiler_params=_cp(("parallel",)),
    )(z, z, z, w)


def _shortconv_bwd(name, z, d_o, w):
    s = z.shape[0]
    kw = w.shape[0]
    rows, nchunk = _chunks(s)
    nb = GROUP // HEAD

    def body(h_ref, bg_ref, cg_ref, do_ref, w_ref, dh_ref, dbg_ref, dcg_ref, dw_ref, pad_q, pad_d, acc):
        _zero_halo(pad_q, s)
        _zero_halo(pad_d, s)
        acc[...] = jnp.zeros_like(acc)

        def fill(ci, carry):
            r0 = pl.multiple_of(ci * rows, rows)
            rs = pl.ds(r0, rows)
            ps = pl.ds(pl.multiple_of(HALO + r0, 8), rows)
            pad_q[ps, :] = cg_ref[rs, :] * h_ref[rs, :]
            pad_d[ps, :] = do_ref[rs, :].astype(F32) * bg_ref[rs, :]
            return carry

        lax.fori_loop(0, nchunk, fill, 0)

        def step(ci, carry):
            r0 = pl.multiple_of(ci * rows, rows)
            rs = pl.ds(r0, rows)
            wq = _window(pad_q, r0, rows)
            wd = _window(pad_d, r0, rows)
            dcv = wd[HALO:HALO + rows]
            cv = jnp.zeros((rows, HEAD), F32)
            dq = jnp.zeros((rows, HEAD), F32)
            for k in range(kw):
                qk = _delayed(wq, kw - 1 - k, rows)
                cv = cv + w_ref[k:k + 1, :] * qk
                dq = dq + w_ref[k:k + 1, :] * _advanced(wd, kw - 1 - k, rows)
                acc[k] += _fold8(dcv * qk)
            dbg_ref[rs, :] = (do_ref[rs, :].astype(F32) * cv).astype(BF16)
            dcg_ref[rs, :] = (dq * h_ref[rs, :]).astype(BF16)
            dh_ref[rs, :] = (dq * cg_ref[rs, :]).astype(BF16)
            return carry

        lax.fori_loop(0, nchunk, step, 0)
        for k in range(kw):
            dw_ref[k:k + 1, :] = jnp.sum(acc[k], axis=0, keepdims=True)

    col = _col_spec(s, 0)
    return pl.pallas_call(
        body, name=name, grid=(nb,),
        in_specs=[_col_spec(s, 8), _col_spec(s, 12), _col_spec(s, 16), col, pl.BlockSpec((kw, HEAD), lambda j: (0, j))],
        out_specs=[col, col, col, pl.BlockSpec((kw, HEAD), lambda j: (0, j))],
        out_shape=[jax.ShapeDtypeStruct((s, GROUP), BF16)] * 3 + [jax.ShapeDtypeStruct((kw, GROUP), F32)],
        scratch_shapes=[pltpu.VMEM((s + 2 * HALO, HEAD), F32), pltpu.VMEM((s + 2 * HALO, HEAD), F32),
                        pltpu.VMEM((kw, 8, HEAD), F32)],
        compiler_params=_cp(("parallel",)),
    )(z, z, z, d_o, w)


def _conformer_conv_fwd(name, z, w, bias):
    s = z.shape[0]
    kw = w.shape[0]
    rows, nchunk = _chunks(s)
    nb = GROUP // HEAD

    def body(a_ref, g_ref, w_ref, b_ref, o_ref, pad):
        _zero_halo(pad, s)

        def fill(ci, carry):
            r0 = pl.multiple_of(ci * rows, rows)
            rs = pl.ds(r0, rows)
            pad[pl.ds(pl.multiple_of(HALO + r0, 8), rows), :] = a_ref[rs, :] * _sigmoid(g_ref[rs, :])
            return carry

        lax.fori_loop(0, nchunk, fill, 0)

        def step(ci, carry):
            r0 = pl.multiple_of(ci * rows, rows)
            win = _window(pad, r0, rows)
            cc = jnp.zeros((rows, HEAD), F32)
            for k in range(kw):
                cc = cc + w_ref[k:k + 1, :] * _delayed(win, kw - 1 - k, rows)
            o_ref[pl.ds(r0, rows), :] = cc + b_ref[...]
            return carry

        lax.fori_loop(0, nchunk, step, 0)

    return pl.pallas_call(
        body, name=name, grid=(nb,),
        in_specs=[_col_spec(s, 20), _col_spec(s, 24), pl.BlockSpec((kw, HEAD), lambda j: (0, j)),
                  pl.BlockSpec((1, HEAD), lambda j: (0, j))],
        out_specs=_col_spec(s, 0),
        out_shape=jax.ShapeDtypeStruct((s, GROUP), F32),
        scratch_shapes=[pltpu.VMEM((s + 2 * HALO, HEAD), F32)],
        compiler_params=_cp(("parallel",)),
    )(z, z, w, bias)


def _ln_rows(cc, g, b):
    mu = jnp.mean(cc, axis=-1, keepdims=True)
    xc = cc - mu
    var = jnp.mean(xc * xc, axis=-1, keepdims=True)
    rstd = lax.rsqrt(var + EPS)
    xh = xc * rstd
    return xh, rstd, xh * g + b


def _conformer_ln_fwd(name, cc, g, b):
    s, d = cc.shape
    tr = _row_tile(s, 512)

    def body(c_ref, g_ref, b_ref, o_ref):
        _, _, l = _ln_rows(c_ref[...], g_ref[...], b_ref[...])
        o_ref[...] = (l * _sigmoid(l)).astype(BF16)

    row = pl.BlockSpec((tr, d), lambda i: (i, 0))
    vec = pl.BlockSpec((1, d), lambda i: (0, 0))
    return pl.pallas_call(
        body, name=name, grid=(s // tr,), in_specs=[row, vec, vec], out_specs=row,
        out_shape=jax.ShapeDtypeStruct((s, d), BF16), compiler_params=_cp(("parallel",)),
    )(cc, g, b)


def _conformer_ln_bwd(name, cc, d_o, g, b):
    s, d = cc.shape
    tr = _row_tile(s, 512)

    def body(c_ref, do_ref, g_ref, b_ref, dcc_ref, dg_ref, db_ref, dcb_ref):
        xh, rstd, l = _ln_rows(c_ref[...], g_ref[...], b_ref[...])
        sg = _sigmoid(l)
        dl = do_ref[...].astype(F32) * sg * (1.0 + l * (1.0 - sg))
        dxh = dl * g_ref[...]
        dcc = rstd * (dxh - jnp.mean(dxh, axis=-1, keepdims=True) - xh * jnp.mean(dxh * xh, axis=-1, keepdims=True))
        dcc_ref[...] = dcc

        @pl.when(pl.program_id(0) == 0)
        def _():
            dg_ref[...] = jnp.zeros_like(dg_ref)
            db_ref[...] = jnp.zeros_like(db_ref)
            dcb_ref[...] = jnp.zeros_like(dcb_ref)

        dg_ref[...] += jnp.sum(dl * xh, axis=0, keepdims=True)
        db_ref[...] += jnp.sum(dl, axis=0, keepdims=True)
        dcb_ref[...] += jnp.sum(dcc, axis=0, keepdims=True)

    row = pl.BlockSpec((tr, d), lambda i: (i, 0))
    vec = pl.BlockSpec((1, d), lambda i: (0, 0))
    return pl.pallas_call(
        body, name=name, grid=(s // tr,), in_specs=[row, row, vec, vec], out_specs=[row, vec, vec, vec],
        out_shape=[jax.ShapeDtypeStruct((s, d), F32)] + [jax.ShapeDtypeStruct((1, d), F32)] * 3,
        compiler_params=_cp(("arbitrary",)),
    )(cc, d_o, g, b)


def _conformer_conv_bwd(name, z, dcc, w):
    s = z.shape[0]
    kw = w.shape[0]
    rows, nchunk = _chunks(s)
    nb = GROUP // HEAD

    def body(a_ref, g_ref, d_ref, w_ref, da_ref, dg_ref, dw_ref, pad_h, pad_d, acc):
        _zero_halo(pad_h, s)
        _zero_halo(pad_d, s)
        acc[...] = jnp.zeros_like(acc)

        def fill(ci, carry):
            r0 = pl.multiple_of(ci * rows, rows)
            rs = pl.ds(r0, rows)
            ps = pl.ds(pl.multiple_of(HALO + r0, 8), rows)
            pad_h[ps, :] = a_ref[rs, :] * _sigmoid(g_ref[rs, :])
            pad_d[ps, :] = d_ref[rs, :]
            return carry

        lax.fori_loop(0, nchunk, fill, 0)

        def step(ci, carry):
            r0 = pl.multiple_of(ci * rows, rows)
            rs = pl.ds(r0, rows)
            wh = _window(pad_h, r0, rows)
            wd = _window(pad_d, r0, rows)
            dcc_c = wd[HALO:HALO + rows]
            dhc = jnp.zeros((rows, HEAD), F32)
            for k in range(kw):
                dhc = dhc + w_ref[k:k + 1, :] * _advanced(wd, kw - 1 - k, rows)
                acc[k] += _fold8(dcc_c * _delayed(wh, kw - 1 - k, rows))
            sg = _sigmoid(g_ref[rs, :])
            da_ref[rs, :] = (dhc * sg).astype(BF16)
            dg_ref[rs, :] = (dhc * a_ref[rs, :] * sg * (1.0 - sg)).astype(BF16)
            return carry

        lax.fori_loop(0, nchunk, step, 0)
        for k in range(kw):
            dw_ref[k:k + 1, :] = jnp.sum(acc[k], axis=0, keepdims=True)

    col = _col_spec(s, 0)
    return pl.pallas_call(
        body, name=name, grid=(nb,),
        in_specs=[_col_spec(s, 20), _col_spec(s, 24), col, pl.BlockSpec((kw, HEAD), lambda j: (0, j))],
        out_specs=[col, col, pl.BlockSpec((kw, HEAD), lambda j: (0, j))],
        out_shape=[jax.ShapeDtypeStruct((s, GROUP), BF16)] * 2 + [jax.ShapeDtypeStruct((kw, GROUP), F32)],
        scratch_shapes=[pltpu.VMEM((s + 2 * HALO, HEAD), F32), pltpu.VMEM((s + 2 * HALO, HEAD), F32),
                        pltpu.VMEM((kw, 8, HEAD), F32)],
        compiler_params=_cp(("parallel",)),
    )(z, z, dcc, w)


def _pool_window_sum(win, level, rows, shift):
    n = win.shape[0]

    def moved(v, k):
        return pltpu.roll(v, k if shift is _delayed else n - k, axis=0)

    s2 = win + moved(win, 1)
    s4 = s2 + moved(s2, 2)
    s8 = s4 + moved(s4, 4)
    s16 = s8 + moved(s8, 8)
    sel = jnp.where(level == 0, s2, jnp.where(level == 1, s4, jnp.where(level == 2, s8, s16)))
    return sel[HALO:HALO + rows]


def _pool_count(level, r0, rows):
    t = r0 + lax.broadcasted_iota(jnp.int32, (rows, 1), 0)
    width = jnp.left_shift(jnp.int32(2), level)
    return jnp.minimum(t + 1, width).astype(F32)


def _pool_fwd(name, z, pool_w, scale):
    s = z.shape[0]
    rows, nchunk = _chunks(s)

    def body(z_ref, w_ref, sc_ref, o_ref, pad):
        level = pl.program_id(0)
        _zero_halo(pad, s)

        def fill(ci, carry):
            r0 = pl.multiple_of(ci * rows, rows)
            pad[pl.ds(pl.multiple_of(HALO + r0, 8), rows), :] = z_ref[pl.ds(r0, rows), :]
            return carry

        lax.fori_loop(0, nchunk, fill, 0)
        wb = w_ref[...].astype(BF16)

        def step(ci, carry):
            r0 = pl.multiple_of(ci * rows, rows)
            win = _window(pad, r0, rows)
            pm = _pool_window_sum(win, level, rows, _delayed) / _pool_count(level, r0, rows) - win[HALO:HALO + rows]
            r = jnp.dot(pm.astype(BF16), wb, preferred_element_type=F32)
            o_ref[pl.ds(r0, rows), :] = (r * sc_ref[...]).astype(BF16)
            return carry

        lax.fori_loop(0, nchunk, step, 0, unroll=min(4, nchunk))

    return pl.pallas_call(
        body, name=name, grid=(POOL_LEVELS,),
        in_specs=[_col_spec(s, 28), pl.BlockSpec((None, HEAD, HEAD), lambda j: (j, 0, 0)),
                  pl.BlockSpec((1, HEAD), lambda j: (0, j))],
        out_specs=_col_spec(s, 0),
        out_shape=jax.ShapeDtypeStruct((s, GROUP), BF16),
        scratch_shapes=[pltpu.VMEM((s + 2 * HALO, HEAD), F32)],
        compiler_params=_cp(("parallel",)),
    )(z, pool_w, scale)


def _pool_bwd(name, z, d_o, pool_w, scale):
    s = z.shape[0]
    rows, nchunk = _chunks(s)

    def body(z_ref, do_ref, w_ref, sc_ref, dz_ref, dw_ref, dsc_ref, pad, pad_q, dw_acc, dsc_acc):
        level = pl.program_id(0)
        _zero_halo(pad, s)
        _zero_halo(pad_q, s)
        dw_acc[...] = jnp.zeros_like(dw_acc)
        dsc_acc[...] = jnp.zeros_like(dsc_acc)

        def fill(ci, carry):
            r0 = pl.multiple_of(ci * rows, rows)
            pad[pl.ds(pl.multiple_of(HALO + r0, 8), rows), :] = z_ref[pl.ds(r0, rows), :]
            return carry

        lax.fori_loop(0, nchunk, fill, 0)
        wb = w_ref[...].astype(BF16)

        def first(ci, carry):
            r0 = pl.multiple_of(ci * rows, rows)
            win = _window(pad, r0, rows)
            cnt = _pool_count(level, r0, rows)
            pm = (_pool_window_sum(win, level, rows, _delayed) / cnt - win[HALO:HALO + rows]).astype(BF16)
            r = jnp.dot(pm, wb, preferred_element_type=F32)
            d_od = do_ref[pl.ds(r0, rows), :].astype(F32)
            dsc_acc[...] += _fold8(d_od * r)
            dr = (d_od * sc_ref[...]).astype(BF16)
            dw_acc[...] += lax.dot_general(pm, dr, TN, preferred_element_type=F32)
            dpm = lax.dot_general(dr, wb, NT, preferred_element_type=F32)
            pad_q[pl.ds(pl.multiple_of(HALO + r0, 8), rows), :] = dpm / cnt
            return carry

        lax.fori_loop(0, nchunk, first, 0, unroll=min(4, nchunk))

        def second(ci, carry):
            r0 = pl.multiple_of(ci * rows, rows)
            wq = _window(pad_q, r0, rows)
            dpm = wq[HALO:HALO + rows] * _pool_count(level, r0, rows)
            dz_ref[pl.ds(r0, rows), :] = (_pool_window_sum(wq, level, rows, _advanced) - dpm).astype(BF16)
            return carry

        lax.fori_loop(0, nchunk, second, 0)
        dw_ref[...] = dw_acc[...]
        dsc_ref[...] = jnp.sum(dsc_acc[...], axis=0, keepdims=True)

    col = _col_spec(s, 0)
    mat = pl.BlockSpec((None, HEAD, HEAD), lambda j: (j, 0, 0))
    vec = pl.BlockSpec((1, HEAD), lambda j: (0, j))
    return pl.pallas_call(
        body, name=name, grid=(POOL_LEVELS,),
        in_specs=[_col_spec(s, 28), col, mat, vec], out_specs=[col, mat, vec],
        out_shape=[jax.ShapeDtypeStruct((s, GROUP), BF16), jax.ShapeDtypeStruct((POOL_LEVELS, HEAD, HEAD), F32),
                   jax.ShapeDtypeStruct((1, GROUP), F32)],
        scratch_shapes=[pltpu.VMEM((s + 2 * HALO, HEAD), F32), pltpu.VMEM((s + 2 * HALO, HEAD), F32),
                        pltpu.VMEM((HEAD, HEAD), F32), pltpu.VMEM((8, HEAD), F32)],
        compiler_params=_cp(("parallel",)),
    )(z, d_o, pool_w, scale)


def _mm_residual(name, x, wg, h, tm, tn):
    m, d = h.shape
    tm = min(tm, m)

    def ep(accs, ins, outs, cols):
        outs[0][:, cols] = ins[5][:, cols] + accs[0]

    tile = pl.BlockSpec((tm, tn), lambda i, j: (i, j))
    return _mm_nn_wide(name, x, wg, "row", tm, tn, epilogue=ep, extra=[h], extra_specs=[tile], out_specs=[tile],
                       out_shape=[jax.ShapeDtypeStruct((m, d), F32)])[0]


def _swiglu_fwd(name, y, wg_gate, wg_up, tm):
    m, kdim = y.shape
    _, a, b = wg_gate.shape
    tm = min(tm, m)

    def ep(accs, ins, outs, cols):
        gt, up = accs
        outs[0][:, cols] = (gt * _sigmoid(gt) * up).astype(BF16)
        outs[1][:, cols] = gt.astype(BF16)
        outs[2][:, cols] = up.astype(BF16)

    gws, gspecs, gdots = _wide_nn_weights(wg_gate, "col", b, 1)
    uws, uspecs, udots = _wide_nn_weights(wg_up, "col", b, 1 + len(gws))
    out = pl.BlockSpec((tm, b), lambda i, j: (i, j))
    return _matmul_wide(name, (m // tm, 4), [y] + gws + uws,
                        [pl.BlockSpec((tm, kdim), lambda i, j: (i, 0))] + gspecs + uspecs, [gdots, udots], b, ep,
                        [out] * 3, [jax.ShapeDtypeStruct((m, 4 * b), BF16)] * 3)


def _swiglu_bwd(name, dh, wg_down, gate, up, tm):
    m, n = dh.shape
    _, a, b = wg_down.shape
    tm = min(tm, m)

    def ep(accs, ins, outs, cols):
        d_act = accs[0]
        gt = ins[3][:, cols].astype(F32)
        upv = ins[4][:, cols].astype(F32)
        sg = _sigmoid(gt)
        outs[0][:, cols] = (d_act * upv * sg * (1.0 + gt * (1.0 - sg))).astype(BF16)
        outs[1][:, cols] = (d_act * gt * sg).astype(BF16)
        outs[2][:, cols] = (gt * sg * upv).astype(BF16)

    ws, wspecs, dots = _wide_nt_weights(wg_down, "row", a, 1)
    tile = pl.BlockSpec((tm, a), lambda i, j: (i, j))
    return _matmul_wide(name, (m // tm, 4), [dh] + ws + [gate, up],
                        [pl.BlockSpec((tm, n), lambda i, j: (i, 0))] + wspecs + [tile, tile], [dots], a, ep,
                        [tile] * 3, [jax.ShapeDtypeStruct((m, 4 * a), BF16)] * 3)


def _ffn_dy(name, d_gate, d_up, wg_gate, wg_up, tiles, after=()):
    m, n = d_gate.shape
    _, a, b = wg_gate.shape
    kdim = 2 * a
    tm, to, tc = tiles
    tm = min(tm, m)
    grid = (m // tm, kdim // to, n // tc)
    lhs = pl.BlockSpec((tm, tc), lambda i, j, k: (i, k))
    wspec = _w_spec_nt(wg_gate, "col", to, tc)
    return _matmul(name, grid, [d_gate, d_up, wg_gate, wg_up], [lhs, lhs, wspec, wspec],
                   [(0, 2, 0, NT), (1, 3, 0, NT)], 1, (tm, to), _ep_store(BF16),
                   [pl.BlockSpec((tm, to), lambda i, j, k: (i, j))], [jax.ShapeDtypeStruct((m, kdim), BF16)],
                   after=after)[0]


def _ple_fwd(name, y, wg, h, pp, tm, tn, after=()):
    m, d = h.shape
    tm = min(tm, m)

    def ep(accs, ins, outs, cols):
        pg = accs[0]
        outs[0][:, cols] = ins[5][:, cols] + _sigmoid(pg) * ins[6][:, cols].astype(F32)
        outs[1][:, cols] = pg.astype(BF16)

    tile = pl.BlockSpec((tm, tn), lambda i, j: (i, j))
    return _mm_nn_wide(name, y, wg, "row", tm, tn, epilogue=ep, extra=[h, pp], extra_specs=[tile, tile],
                       out_specs=[tile, tile],
                       out_shape=[jax.ShapeDtypeStruct((m, d), F32), jax.ShapeDtypeStruct((m, d), BF16)], after=after)


def _ple_bwd(name, dh, pg, pp, after=()):
    s, d = dh.shape
    tr = _row_tile(s, 512)

    def body(dh_ref, pg_ref, pp_ref, *rest):
        dpp_ref, dpg_ref = rest[-2:]
        dhv = dh_ref[...]
        sg = _sigmoid(pg_ref[...].astype(F32))
        dpp_ref[...] = (dhv * sg).astype(BF16)
        dpg_ref[...] = (dhv * pp_ref[...].astype(F32) * sg * (1.0 - sg)).astype(BF16)

    row = pl.BlockSpec((tr, d), lambda i: (i, 0))
    return pl.pallas_call(
        body, name=name, grid=(s // tr,), in_specs=[row] * 3 + [ANY_SPEC] * len(after), out_specs=[row] * 2,
        out_shape=[jax.ShapeDtypeStruct((s, d), BF16)] * 2, compiler_params=_cp(("parallel",)),
    )(dh, pg, pp, *after)


BIG = ["w_in", "w_out", "w_gate", "w_up", "w_down", "w_ple_gate", "w_ple_proj"]
KIND = {"w_in": "col", "w_out": "row", "w_gate": "col", "w_up": "col", "w_down": "row", "w_ple_gate": "row",
        "w_ple_proj": "col"}
GATHER_GROUPS = (("w_in", "w_out"), ("w_gate", "w_up"), ("w_down", "w_ple_gate", "w_ple_proj"))
RS_GROUPS = (("w_ple_gate", "w_ple_proj", "w_down", "w_gate", "w_up"), ("w_out", "w_in"))
SMALL = ["norm_mix_g", "sgu_ln_g", "sgu_ln_b", "sgu_w", "sgu_b", "sc_conv_w", "cf_conv_w", "cf_conv_b", "cf_ln_g",
         "cf_ln_b", "pool_w", "pool_scale", "norm_ffn_g", "norm_ple_g", "final_norm_g"]
CHIP_SPLIT = ["sc_conv_w", "cf_conv_w"]
MATRIX_SMALL = ["sgu_w", "pool_w"]
WEIGHTS = ['norm_mix_g', 'w_in', 'sgu_ln_g', 'sgu_ln_b', 'sgu_w', 'sgu_b', 'sc_conv_w', 'cf_conv_w', 'cf_conv_b',
           'cf_ln_g', 'cf_ln_b', 'pool_w', 'pool_scale', 'w_out', 'norm_ffn_g', 'w_gate', 'w_up', 'w_down',
           'norm_ple_g', 'w_ple_gate', 'w_ple_proj', 'final_norm_g']


def _tile(n, want):
    if n <= want:
        return n
    t = (want // 128) * 128
    while n % t:
        t -= 128
    return t


def _pack_rows(vecs):
    flat = jnp.concatenate([v.reshape(-1) for v in vecs])
    n = flat.shape[0]
    quantum = PACK_ROWS * 128
    padded = ((n + quantum - 1) // quantum) * quantum
    return jnp.pad(flat, (0, padded - n)).reshape(padded // 128, 128), n


def _unpack(flat, shapes):
    out, off = [], 0
    for shp in shapes:
        size = math.prod(shp)
        out.append(flat[off:off + size].reshape(shp))
        off += size
    return out


def kernel(x, p, norm_mix_g, w_in, sgu_ln_g, sgu_ln_b, sgu_w, sgu_b, sc_conv_w, cf_conv_w, cf_conv_b, cf_ln_g, cf_ln_b, pool_w, pool_scale, w_out, norm_ffn_g, w_gate, w_up, w_down, norm_ple_g, w_ple_gate, w_ple_proj, final_norm_g, loss_target, m_norm_mix_g, m_w_in, m_sgu_ln_g, m_sgu_ln_b, m_sgu_w, m_sgu_b, m_sc_conv_w, m_cf_conv_w, m_cf_conv_b, m_cf_ln_g, m_cf_ln_b, m_pool_w, m_pool_scale, m_w_out, m_norm_ffn_g, m_w_gate, m_w_up, m_w_down, m_norm_ple_g, m_w_ple_gate, m_w_ple_proj, m_final_norm_g, v_norm_mix_g, v_w_in, v_sgu_ln_g, v_sgu_ln_b, v_sgu_w, v_sgu_b, v_sc_conv_w, v_cf_conv_w, v_cf_conv_b, v_cf_ln_g, v_cf_ln_b, v_pool_w, v_pool_scale, v_w_out, v_norm_ffn_g, v_w_gate, v_w_up, v_w_down, v_norm_ple_g, v_w_ple_gate, v_w_ple_proj, v_final_norm_g):
    args = dict(locals())
    w = {n: args[n] for n in WEIGHTS}
    mom = {n: args["m_" + n] for n in WEIGHTS}
    var = {n: args["v_" + n] for n in WEIGHTS}
    depth = w_in.shape[0]
    s, d = x.shape[1], x.shape[2]
    f_dim = 4 * w_gate.shape[2]
    xi, yi, ci = lax.axis_index("x"), lax.axis_index("y"), lax.axis_index("c")
    c_idx = ci.astype(jnp.int32).reshape(1)

    chip_idx = (2 * xi + yi).astype(jnp.int32).reshape(1)

    def start_gathers(l, dep):
        pending = []
        for gi, names in enumerate(GATHER_GROUPS):
            lands = [_cast_into_landing(f"cast_{n}", w[n], l, 0 if KIND[n] == "col" else 1, chip_idx) for n in names]
            pending.append(_gather_start(f"gather_start_{l}_{gi}", lands, dep))
            dep = pending[-1][3]
        return pending, dep

    def receive(l, gi, pending, after):
        send_sems, recv_sems, lands, _ = pending
        lands = _gather_wait(f"gather_wait_{l}_{gi}", send_sems, recv_sems, lands, after)
        return _share_start(f"gather_share_start_{l}_{gi}", lands)

    def complete(l, gi, share, after):
        send_sems, recv_sems, lands, _ = share
        lands = _share_wait(f"gather_share_wait_{l}_{gi}", send_sems, recv_sems, lands, after)
        return dict(zip(GATHER_GROUPS[gi], lands))

    conv_pack = jnp.concatenate([sc_conv_w, cf_conv_w], axis=1)
    taps = conv_pack.shape[1]
    rows_pad = ((depth * taps + 7) // 8) * 8
    conv_rows = jnp.pad(conv_pack.reshape(depth * taps, HEAD), ((0, rows_pad - depth * taps), (0, 0)))
    conv_all = _allgather8("gather_conv_weights", [conv_rows])[0]
    conv_full = conv_all[0::2, :depth * taps].reshape(4, depth, taps, HEAD)
    conv_full = jnp.transpose(conv_full, (1, 2, 0, 3)).reshape(depth, taps, GROUP)
    sc_w_full, cf_w_full = conv_full[:, :3], conv_full[:, 3:]

    pending, token = start_gathers(0, conv_all)
    h = x[0]
    saved = []
    gathered = []
    shares = [None] * len(GATHER_GROUPS)
    for l in range(depth):
        just_in_time = l < 2
        ahead = l + 1 < depth and l + 1 >= 2
        if just_in_time:
            shares[0] = receive(l, 0, pending[0], token if l == 0 else h)
        wg = complete(l, 0, shares[0], h)
        gathered.append(wg)
        started = ()
        this_layer = pending
        if l + 1 < depth:
            pending, token = start_gathers(l + 1, wg["w_in"])
            started = (token,)
        sv = {"h0": h}
        y1 = _rms_fwd("rms_mix", h, norm_mix_g[l:l + 1], after=started)
        z = _mm_nn_wide("mm_in", y1, wg["w_in"], "col", 1024, 1024)[0]
        lg, lb = sgu_ln_g[l][:, None, :], sgu_ln_b[l][:, None, :]
        sb = sgu_b[l][:, :, None]
        oa = _sgu_fwd("sgu_fwd", z, lg, lb, sgu_w[l], sb)
        ob = _shortconv_fwd("shortconv_fwd", z, sc_w_full[l])
        cc = _conformer_conv_fwd("conformer_conv_fwd", z, cf_w_full[l], cf_conv_b[l:l + 1])
        oc = _conformer_ln_fwd("conformer_ln_fwd", cc, cf_ln_g[l:l + 1], cf_ln_b[l:l + 1])
        od = _pool_fwd("pool_fwd", z, pool_w[l], pool_scale[l:l + 1])
        o = jnp.concatenate([oa, ob, oc, od], axis=1)
        h1 = _mm_residual("mm_out", o, wg["w_out"], h, 1024, 1024)
        if just_in_time:
            shares[1] = receive(l, 1, this_layer[1], h1)
        wg.update(complete(l, 1, shares[1], h1))
        y2 = _rms_fwd("rms_ffn", h1, norm_ffn_g[l:l + 1])
        act, gt, up = _swiglu_fwd("mm_swiglu", y2, wg["w_gate"], wg["w_up"], 512)
        if just_in_time:
            shares[2] = receive(l, 2, this_layer[2], act)
        wg.update(complete(l, 2, shares[2], act))
        h2 = _mm_residual("mm_down", act, wg["w_down"], h1, 1024, 512)
        started = ()
        if ahead:
            shares[0] = receive(l + 1, 0, pending[0], h2)
            shares[1] = receive(l + 1, 1, pending[1], shares[0][3])
            started = (shares[1][3],)
        y3 = _rms_fwd("rms_ple", h2, norm_ple_g[l:l + 1], after=started)
        pb = p[l, 0].astype(BF16)
        ptile = pl.BlockSpec((min(1024, s), 512), lambda i, j: (i, j))
        pp = _mm_nn_wide("mm_ple_proj", pb, wg["w_ple_proj"], "col", 1024, 512, epilogue=_wide_store(BF16),
                         out_specs=[ptile], out_shape=[jax.ShapeDtypeStruct((s, d), BF16)])[0]
        started = ()
        if ahead:
            shares[2] = receive(l + 1, 2, pending[2], pp)
            started = (shares[2][3],)
        h3, pg = _ple_fwd("mm_ple_gate", y3, wg["w_ple_gate"], h2, pp, 1024, 1024, after=started)
        sv.update(y1=y1, z=z, cc=cc, o=o, h1=h1, y2=y2, gt=gt, up=up, h2=h2, y3=y3, pb=pb, pp=pp, pg=pg)
        saved.append(sv)
        h = h3

    loss_part, dh, dhb, d_final_g = _loss_head("loss_head", h, final_norm_g[None, :], loss_target[0])

    small_grads = [None] * depth
    where = jnp.stack([2 * xi + yi, 2 * (1 - xi) + yi, 2 * xi + (1 - yi), 2 * (1 - xi) + (1 - yi), ci]).astype(jnp.int32)
    grad_bufs = {n: lax.empty((depth, 2, w[n].shape[1] // 2, w[n].shape[2]), F32) for n in BIG}
    exchanges = [None] * len(RS_GROUPS)
    joins = []
    behind_join = ()

    def halves(g):
        return g.reshape(4, 2, g.shape[1] // 2, g.shape[2])

    def start_exchange(layer, gi, sibling, after):
        send_sems, recv_sems, gs, lands, _ = sibling
        gs, lands = _sibling_wait(f"rs_sibling_wait_{layer}_{gi}", send_sems, recv_sems, gs, lands, after)
        chip_sums = _add_halves("rs_add", gs, lands, c_idx)
        return _chip_exchange_start(f"rs_chips_start_{layer}_{gi}", chip_sums)

    def finish_exchange(layer, gi, after):
        send_sems, recv_sems, parts, lands, _ = exchanges[gi]
        parts, lands = _chip_exchange_wait(f"rs_chips_wait_{layer}_{gi}", send_sems, recv_sems, parts, lands, after)
        names = RS_GROUPS[gi]
        sums = _sum_parts("rs_sum", parts, lands, [grad_bufs[n] for n in names], layer, where)
        grad_bufs.update(zip(names, sums))

    for l in reversed(range(depth)):
        wg = gathered[l]
        sv = saved[l]
        fs = f_dim // 4
        started = () if exchanges[1] is None else (exchanges[1][4],)
        d_pp, d_pg = _ple_bwd("ple_bwd", dh, sv["pg"], sv["pp"], after=started + behind_join)
        g_ple_proj = _mm_tn_wide("dw_ple_proj", sv["pb"], d_pp, "col", w_ple_proj.shape[1], 512)
        g_ple_gate = _mm_tn_wide("dw_ple_gate", sv["y3"], d_pg, "row", 512, 1024)
        dy3 = _mm_nt_wide("dx_ple_gate", d_pg, wg["w_ple_gate"], "row", 1024, 512)
        dh, dhb, dg_ple = _rms_bwd("rms_ple_bwd", sv["h2"], norm_ple_g[l:l + 1], dy3, dh)

        d_gt, d_up, act = _swiglu_bwd("dx_down_swiglu", dhb, wg["w_down"], sv["gt"], sv["up"], 512)
        g_down = _mm_tn_wide("dw_down", act, dhb, "row", fs, 512)
        g_gate = _mm_tn_wide("dw_gate", sv["y2"], d_gt, "col", 512, fs)
        g_up = _mm_tn_wide("dw_up", sv["y2"], d_up, "col", 512, fs)
        big = dict(w_gate=g_gate, w_up=g_up, w_down=g_down, w_ple_gate=g_ple_gate, w_ple_proj=g_ple_proj)
        sibling = _sibling_start(f"rs_sibling_start_{l}_0", [halves(big[n]) for n in RS_GROUPS[0]])
        dy2 = _ffn_dy("dx_gate_up", d_gt, d_up, wg["w_gate"], wg["w_up"], (1024, 1024, fs), after=(sibling[4],))
        dh, dhb, dg_ffn = _rms_bwd("rms_ffn_bwd", sv["h1"], norm_ffn_g[l:l + 1], dy2, dh)

        g_out = _mm_tn_wide("dw_out", sv["o"], dhb, "row", 512, 1024)
        if exchanges[0] is not None:
            finish_exchange(l + 1, 0, g_out)
        exchanges[0] = start_exchange(l, 0, sibling, g_out)
        d_o = _mm_nt_wide("dx_out", dhb, wg["w_out"], "row", 1024, 512, after=(exchanges[0][4],))
        z = sv["z"]
        lg, lb = sgu_ln_g[l][:, None, :], sgu_ln_b[l][:, None, :]
        sb = sgu_b[l][:, :, None]
        dzu, dzv, d_lg, d_lb, d_sw, d_sb = _sgu_bwd("sgu_bwd", z, d_o[:, 0:GROUP], lg, lb, sgu_w[l], sb)
        dzh, dzbg, dzcg, d_scw = _shortconv_bwd("shortconv_bwd", z, d_o[:, GROUP:2 * GROUP], sc_w_full[l])
        dcc, d_cflg, d_cflb, d_cfb = _conformer_ln_bwd("conformer_ln_bwd", sv["cc"], d_o[:, 2 * GROUP:3 * GROUP],
                                                       cf_ln_g[l:l + 1], cf_ln_b[l:l + 1])
        dza, dzg, d_cfw = _conformer_conv_bwd("conformer_conv_bwd", z, dcc, cf_w_full[l])
        dzd, d_pw, d_psc = _pool_bwd("pool_bwd", z, d_o[:, 3 * GROUP:], pool_w[l], pool_scale[l:l + 1])
        dz = jnp.concatenate([dzu, dzv, dzh, dzbg, dzcg, dza, dzg, dzd], axis=1)
        g_in = _mm_tn_wide("dw_in", sv["y1"], dz, "col", 512, 1024)
        big.update(w_out=g_out, w_in=g_in)
        sibling = _sibling_start(f"rs_sibling_start_{l}_1", [halves(big[n]) for n in RS_GROUPS[1]])
        dy1 = _mm_nt_wide("dx_in", dz, wg["w_in"], "col", 1024, 512, after=(sibling[4],))
        dh, dhb, dg_mix = _rms_bwd("rms_mix_bwd", sv["h0"], norm_mix_g[l:l + 1], dy1, dh)
        if exchanges[1] is not None:
            finish_exchange(l + 1, 1, dh)
            send_sems, recv_sems, bufs, join_token = _join_start(f"rs_join_start_{l + 1}",
                                                                 [grad_bufs[n] for n in BIG], l + 1)
            grad_bufs.update(zip(BIG, bufs))
            joins.append((l + 1, send_sems, recv_sems))
            behind_join = (join_token,)
        exchanges[1] = start_exchange(l, 1, sibling, dh)

        small_grads[l] = dict(norm_mix_g=dg_mix, sgu_ln_g=d_lg, sgu_ln_b=d_lb, sgu_w=d_sw, sgu_b=d_sb,
                              sc_conv_w=d_scw, cf_conv_w=d_cfw, cf_conv_b=d_cfb, cf_ln_g=d_cflg, cf_ln_b=d_cflb,
                              pool_w=d_pw, pool_scale=d_psc, norm_ffn_g=dg_ffn, norm_ple_g=dg_ple)
    grad_x = dh[None]

    grads, delta, new_m, new_v = {}, {}, {}, {}
    per_layer = [n for n in SMALL if n != "final_norm_g"]
    narrow = [n for n in per_layer if n not in MATRIX_SMALL]
    packed, _ = _pack_rows([small_grads[l][n] for n in narrow for l in range(depth)] + [d_final_g, loss_part])
    packed_wide, _ = _pack_rows([small_grads[l][n] for n in MATRIX_SMALL for l in range(depth)])
    everyone = _allgather8("gather_small", [packed, packed_wide.astype(BF16)])
    total = _sum8("sum_small", everyone[0]).reshape(-1)
    total_wide = _sum8("sum_small_wide", everyone[1]).reshape(-1)

    def stacked_shape(n):
        return (depth,) + (w[n].shape[1:] if n not in CHIP_SPLIT else (w[n].shape[1], GROUP))

    pieces = _unpack(total, [stacked_shape(n) for n in narrow] + [(d,), ()])
    loss = pieces[-1]
    grads["final_norm_g"] = pieces[-2]
    chip_off = (2 * xi + yi) * HEAD
    for n, g in zip(narrow, pieces):
        grads[n] = lax.dynamic_slice_in_dim(g, chip_off, HEAD, axis=2) if n in CHIP_SPLIT else g
    grads.update(zip(MATRIX_SMALL, _unpack(total_wide, [stacked_shape(n) for n in MATRIX_SMALL])))

    finish_exchange(0, 0, dh)
    for layer, send_sems, recv_sems in joins:
        bufs = _join_wait(f"rs_join_wait_{layer}", send_sems, recv_sems, [grad_bufs[n] for n in BIG], layer, total)
        grad_bufs.update(zip(BIG, bufs))
    behind = (exchanges[1][4],)
    for gi, names in enumerate(RS_GROUPS):
        if gi == 1:
            finish_exchange(0, 1, behind[0])
        joined = _sibling_join_halves(f"rs_join_{gi}", [grad_bufs[n] for n in names], 0, after=behind)
        for n, g in zip(names, joined):
            shp = w[n].shape
            two_d = (shp[0] * shp[1], shp[2])
            grads[n] = g.reshape(shp)
            dl, mn, vn = _adamw(f"adamw_{n}", w[n].reshape(two_d), g.reshape(two_d), mom[n].reshape(two_d),
                                var[n].reshape(two_d))
            delta[n], new_m[n], new_v[n] = dl.reshape(shp), mn.reshape(shp), vn.reshape(shp)
            behind = (dl,)
    small_shapes = [w[n].shape for n in SMALL]
    pw, _ = _pack_rows([w[n] for n in SMALL])
    pg_, _ = _pack_rows([grads[n] for n in SMALL])
    pm, _ = _pack_rows([mom[n] for n in SMALL])
    pv, _ = _pack_rows([var[n] for n in SMALL])
    dl, mn, vn = _adamw("adamw_small", pw, pg_, pm, pv)
    for n, a, b, cc_ in zip(SMALL, _unpack(dl.reshape(-1), small_shapes), _unpack(mn.reshape(-1), small_shapes),
                            _unpack(vn.reshape(-1), small_shapes)):
        delta[n], new_m[n], new_v[n] = a, b, cc_

    return (loss, grad_x, *[grads[n] for n in WEIGHTS], *[delta[n] for n in WEIGHTS],
            *[new_m[n] for n in WEIGHTS], *[new_v[n] for n in WEIGHTS])
```

```python
import functools
import math

import jax
import jax.numpy as jnp
from jax import lax
from jax.experimental import pallas as pl
from jax.experimental.pallas import tpu as pltpu

F32 = jnp.float32
BF16 = jnp.bfloat16
MESH = pl.DeviceIdType.MESH

HEAD = 128
GROUP = 512
EPS = 1e-6
HALO = 32
CHUNK_ROWS = 128
POOL_LEVELS = 4
PACK_ROWS = 512

ADAM_LR = 0.001
ADAM_B1 = 0.9
ADAM_B2 = 0.999
ADAM_EPS = 1e-08
ADAM_WD = 0.01
ADAM_STEP = 10

VMEM_LIMIT = 56 * 1024 * 1024


def _cp(sem=None, vmem=VMEM_LIMIT):
    return pltpu.CompilerParams(dimension_semantics=sem, vmem_limit_bytes=vmem)


def _sigmoid(x):
    return 0.5 * jnp.tanh(0.5 * x) + 0.5


_GELU_K = math.sqrt(2.0 / math.pi)
_GELU_C = 0.044715


def _gelu(x):
    t = jnp.tanh(_GELU_K * (x + _GELU_C * x * x * x))
    return 0.5 * x * (1.0 + t)


def _gelu_grad(x):
    t = jnp.tanh(_GELU_K * (x + _GELU_C * x * x * x))
    return 0.5 * (1.0 + t) + 0.5 * x * (1.0 - t * t) * _GELU_K * (1.0 + 3.0 * _GELU_C * x * x)


def _mesh_pos():
    return lax.axis_index("x"), lax.axis_index("y"), lax.axis_index("c")


def _any_specs(n):
    return [pl.BlockSpec(memory_space=pl.ANY)] * n


def _allgather8(name, blocks):
    n = len(blocks)

    def body(*refs):
        ins, outs = refs[:n], refs[n:2 * n]
        send_sems, recv_sems, local_sems = refs[2 * n:]
        x, y, c = _mesh_pos()
        me, sibling = (x, y, c), (x, y, 1 - c)
        chips = [(1 - x, y), (x, 1 - y), (1 - x, 1 - y)]

        def slot(a, dev):
            return outs[a].at[4 * dev[0] + 2 * dev[1] + dev[2]]

        def copy(a, k, block, to, src=None):
            dst = slot(a, block)
            return pltpu.make_async_remote_copy(
                src_ref=dst if src is None else src, dst_ref=dst,
                send_sem=send_sems.at[7 * a + k], recv_sem=recv_sems.at[7 * a + k],
                device_id=to, device_id_type=MESH)

        mine, first, passed = [], [], []
        for a in range(n):
            cp = pltpu.make_async_copy(ins[a], slot(a, me), local_sems.at[a])
            cp.start()
            mine.append(cp)
            cps = [copy(a, 0, me, sibling, src=ins[a])]
            cps += [copy(a, 1 + j, me, (*chip, c), src=ins[a]) for j, chip in enumerate(chips)]
            for cp in cps:
                cp.start()
            first += cps
        for j, chip in enumerate(chips):
            for a in range(n):
                copy(a, 1 + j, (*chip, c), me).wait_recv()
                cp = copy(a, 4 + j, (*chip, c), sibling)
                cp.start()
                passed.append(cp)
        for a in range(n):
            copy(a, 0, sibling, me).wait_recv()
            for j, chip in enumerate(chips):
                copy(a, 4 + j, (*chip, 1 - c), me).wait_recv()
        for cp in first + passed:
            cp.wait_send()
        for cp in mine:
            cp.wait()

    return pl.pallas_call(
        body, name=name,
        out_shape=[jax.ShapeDtypeStruct((8,) + b.shape, b.dtype) for b in blocks],
        in_specs=_any_specs(n), out_specs=_any_specs(n),
        scratch_shapes=[pltpu.SemaphoreType.DMA((7 * n,)), pltpu.SemaphoreType.DMA((7 * n,)),
                        pltpu.SemaphoreType.DMA((n,))],
    )(*blocks)


HBM_SPEC = pl.BlockSpec(memory_space=pltpu.HBM)
SEM_SPEC = pl.BlockSpec(memory_space=pltpu.SEMAPHORE)
ANY_SPEC = pl.BlockSpec(memory_space=pl.ANY)
SPLIT_COPY = pltpu.CompilerParams(has_side_effects=pltpu.SideEffectType.DATAFLOW_SIDE_EFFECTING)


def _hbm(x):
    return pltpu.with_memory_space_constraint(x, pltpu.HBM)


def _other_chips(x, y):
    return [(1 - x, y), (x, 1 - y), (1 - x, 1 - y)]


def _dev_slot(ref, dev):
    return ref.at[4 * dev[0] + 2 * dev[1] + dev[2]]


def _cast_into_landing(name, w, layer, ax, chip_idx):
    _, r, cc = w.shape
    r2, c2 = (r // 2, cc) if ax == 0 else (r, cc // 2)
    tr = _fit_rows(r2, c2, 512 * 1024)
    nt = r2 // tr

    def body(chip_ref, w_ref, o_ref):
        o_ref[...] = w_ref[...].astype(BF16)

    if ax == 0:
        in_spec = pl.BlockSpec((None, tr, c2), lambda hf, i, chip_ref: (layer, hf * nt + i, 0))
    else:
        in_spec = pl.BlockSpec((None, tr, c2), lambda hf, i, chip_ref: (layer, i, hf))
    return pl.pallas_call(
        body, name=name,
        grid_spec=pltpu.PrefetchScalarGridSpec(
            num_scalar_prefetch=1, grid=(2, nt), in_specs=[in_spec],
            out_specs=pl.BlockSpec((None, tr, c2), lambda hf, i, chip_ref: (2 * chip_ref[0] + hf, i, 0))),
        out_shape=jax.ShapeDtypeStruct((8, r2, c2), BF16), compiler_params=_cp(("parallel", "parallel")),
    )(chip_idx, w)


def _gather_ici_copies(lands, send_sems, recv_sems):
    x, y, c = _mesh_pos()
    pairs = []
    for a in range(len(lands)):
        for j, chip in enumerate(_other_chips(x, y)):
            def copy(dev):
                return pltpu.make_async_remote_copy(
                    src_ref=_dev_slot(lands[a], dev), dst_ref=_dev_slot(lands[a], dev),
                    send_sem=send_sems.at[3 * a + j], recv_sem=recv_sems.at[3 * a + j],
                    device_id=(*chip, c), device_id_type=MESH)
            pairs.append((copy((x, y, c)), copy((*chip, c))))
    return pairs


def _exchange_ici_copies(ins, lands, send_sems, recv_sems):
    x, y, c = _mesh_pos()
    pairs = []
    for a in range(len(ins)):
        for k, chip in enumerate(_other_chips(x, y)):
            there = 2 * chip[0] + chip[1]
            def copy(dst_entry):
                return pltpu.make_async_remote_copy(
                    src_ref=ins[a].at[there], dst_ref=lands[a].at[dst_entry],
                    send_sem=send_sems.at[3 * a + k], recv_sem=recv_sems.at[3 * a + k],
                    device_id=(*chip, c), device_id_type=MESH)
            pairs.append((copy(2 * x + y), copy(there)))
    return pairs


def _gather_start(name, lands, dep):
    n = len(lands)

    def body(*refs):
        token = refs[-1]
        for send, _ in _gather_ici_copies(refs[:n], refs[n + 1], refs[n + 2]):
            send.start()
        token[...] = jnp.zeros_like(token)

    out = pl.pallas_call(
        body, name=name,
        out_shape=(pltpu.SemaphoreType.DMA((3 * n,)), pltpu.SemaphoreType.DMA((3 * n,)),
                   *[pltpu.HBM(l.shape, l.dtype) for l in lands], jax.ShapeDtypeStruct((8, 128), F32)),
        in_specs=[HBM_SPEC] * n + [ANY_SPEC],
        out_specs=(SEM_SPEC, SEM_SPEC, *[HBM_SPEC] * n, pl.BlockSpec(memory_space=pltpu.VMEM)),
        input_output_aliases={a: 2 + a for a in range(n)},
        compiler_params=SPLIT_COPY,
    )(*[_hbm(l) for l in lands], dep)
    return out[0], out[1], list(out[2:2 + n]), out[-1]


def _gather_wait(name, send_sems, recv_sems, lands, after):
    n = len(lands)

    def body(*refs):
        for send, recv in _gather_ici_copies(refs[:n], refs[n], refs[n + 1]):
            send.wait_send()
            recv.wait_recv()

    out = pl.pallas_call(
        body, name=name,
        out_shape=[pltpu.HBM(l.shape, l.dtype) for l in lands],
        in_specs=[HBM_SPEC] * n + [SEM_SPEC, SEM_SPEC, ANY_SPEC],
        out_specs=[HBM_SPEC] * n,
        input_output_aliases={a: a for a in range(n)},
        compiler_params=SPLIT_COPY,
    )(*lands, send_sems, recv_sems, after)
    return list(out)


def _share_copies(lands, send_sems, recv_sems):
    x, y, c = _mesh_pos()
    pairs = []
    for a in range(len(lands)):
        for j, chip in enumerate(_other_chips(x, y)):
            def copy(dev):
                slot = _dev_slot(lands[a], dev)
                return pltpu.make_async_remote_copy(
                    src_ref=slot, dst_ref=slot, send_sem=send_sems.at[3 * a + j], recv_sem=recv_sems.at[3 * a + j],
                    device_id=(x, y, 1 - c), device_id_type=MESH)
            pairs.append((copy((*chip, c)), copy((*chip, 1 - c))))
    return pairs


def _share_start(name, lands):
    n = len(lands)

    def body(*refs):
        token = refs[-1]
        for send, _ in _share_copies(refs[:n], refs[n], refs[n + 1]):
            send.start()
        token[...] = jnp.zeros_like(token)

    out = pl.pallas_call(
        body, name=name,
        out_shape=(pltpu.SemaphoreType.DMA((3 * n,)), pltpu.SemaphoreType.DMA((3 * n,)),
                   *[pltpu.HBM(l.shape, l.dtype) for l in lands], jax.ShapeDtypeStruct((8, 128), F32)),
        in_specs=[HBM_SPEC] * n,
        out_specs=(SEM_SPEC, SEM_SPEC, *[HBM_SPEC] * n, pl.BlockSpec(memory_space=pltpu.VMEM)),
        input_output_aliases={a: 2 + a for a in range(n)},
        compiler_params=SPLIT_COPY,
    )(*lands)
    return out[0], out[1], list(out[2:2 + n]), out[-1]


def _share_wait(name, send_sems, recv_sems, lands, after):
    n = len(lands)

    def body(*refs):
        for send, recv in _share_copies(refs[:n], refs[n], refs[n + 1]):
            send.wait_send()
            recv.wait_recv()

    out = pl.pallas_call(
        body, name=name,
        out_shape=[pltpu.HBM(l.shape, l.dtype) for l in lands],
        in_specs=[HBM_SPEC] * n + [SEM_SPEC, SEM_SPEC, ANY_SPEC],
        out_specs=[HBM_SPEC] * n,
        input_output_aliases={a: a for a in range(n)},
        compiler_params=SPLIT_COPY,
    )(*lands, send_sems, recv_sems, after)
    return list(out)


def _sibling_copies(ins, lands, send_sems, recv_sems):
    x, y, c = _mesh_pos()
    return [pltpu.make_async_remote_copy(
        src_ref=ins[a].at[j, 1 - c], dst_ref=lands[a].at[j],
        send_sem=send_sems.at[4 * a + j], recv_sem=recv_sems.at[4 * a + j],
        device_id=(x, y, 1 - c), device_id_type=MESH) for a in range(len(ins)) for j in range(4)]


def _sibling_start(name, grads):
    n = len(grads)

    def body(*refs):
        token = refs[-1]
        for cp in _sibling_copies(refs[:n], refs[n:2 * n], refs[2 * n], refs[2 * n + 1]):
            cp.start()
        token[...] = jnp.zeros_like(token)

    lands = [_hbm(lax.empty((4,) + g.shape[2:], g.dtype)) for g in grads]
    out = pl.pallas_call(
        body, name=name,
        out_shape=(pltpu.SemaphoreType.DMA((4 * n,)), pltpu.SemaphoreType.DMA((4 * n,)),
                   *[pltpu.HBM(g.shape, g.dtype) for g in grads], *[pltpu.HBM(l.shape, l.dtype) for l in lands],
                   jax.ShapeDtypeStruct((8, 128), F32)),
        in_specs=[HBM_SPEC] * (2 * n),
        out_specs=(SEM_SPEC, SEM_SPEC, *[HBM_SPEC] * (2 * n), pl.BlockSpec(memory_space=pltpu.VMEM)),
        input_output_aliases={i: 2 + i for i in range(2 * n)},
        compiler_params=SPLIT_COPY,
    )(*[_hbm(g) for g in grads], *lands)
    return out[0], out[1], list(out[2:2 + n]), list(out[2 + n:2 + 2 * n]), out[-1]


def _sibling_wait(name, send_sems, recv_sems, grads, lands, after):
    n = len(grads)

    def body(*refs):
        for cp in _sibling_copies(refs[:n], refs[n:2 * n], refs[2 * n], refs[2 * n + 1]):
            cp.wait_send()
            cp.wait_recv()

    out = pl.pallas_call(
        body, name=name,
        out_shape=[pltpu.HBM(g.shape, g.dtype) for g in grads] + [pltpu.HBM(l.shape, l.dtype) for l in lands],
        in_specs=[HBM_SPEC] * (2 * n) + [SEM_SPEC, SEM_SPEC, ANY_SPEC],
        out_specs=[HBM_SPEC] * (2 * n),
        input_output_aliases={i: i for i in range(2 * n)},
        compiler_params=SPLIT_COPY,
    )(*grads, *lands, send_sems, recv_sems, after)
    return list(out[:n]), list(out[n:])


def _chip_exchange_start(name, parts):
    n = len(parts)

    def body(*refs):
        ins, lands = refs[:n], refs[n:2 * n]
        token = refs[-1]
        for send, _ in _exchange_ici_copies(ins, lands, refs[2 * n], refs[2 * n + 1]):
            send.start()
        token[...] = jnp.zeros_like(token)

    lands = [_hbm(lax.empty(p.shape, p.dtype)) for p in parts]
    out = pl.pallas_call(
        body, name=name,
        out_shape=(pltpu.SemaphoreType.DMA((3 * n,)), pltpu.SemaphoreType.DMA((3 * n,)),
                   *[pltpu.HBM(p.shape, p.dtype) for p in parts], *[pltpu.HBM(p.shape, p.dtype) for p in parts],
                   jax.ShapeDtypeStruct((8, 128), F32)),
        in_specs=[HBM_SPEC] * (2 * n),
        out_specs=(SEM_SPEC, SEM_SPEC, *[HBM_SPEC] * (2 * n), pl.BlockSpec(memory_space=pltpu.VMEM)),
        input_output_aliases={i: 2 + i for i in range(2 * n)},
        compiler_params=SPLIT_COPY,
    )(*[_hbm(p) for p in parts], *lands)
    return out[0], out[1], list(out[2:2 + n]), list(out[2 + n:2 + 2 * n]), out[-1]


def _chip_exchange_wait(name, send_sems, recv_sems, parts, lands, after):
    n = len(parts)

    def body(*refs):
        ins, lands_in = refs[:n], refs[n:2 * n]
        for send, recv in _exchange_ici_copies(ins, lands_in, refs[2 * n], refs[2 * n + 1]):
            send.wait_send()
            recv.wait_recv()

    out = pl.pallas_call(
        body, name=name,
        out_shape=[pltpu.HBM(p.shape, p.dtype) for p in parts] * 2,
        in_specs=[HBM_SPEC] * (2 * n) + [SEM_SPEC, SEM_SPEC, ANY_SPEC],
        out_specs=[HBM_SPEC] * (2 * n),
        input_output_aliases={i: i for i in range(2 * n)},
        compiler_params=SPLIT_COPY,
    )(*parts, *lands, send_sems, recv_sems, after)
    return list(out[:n]), list(out[n:])


def _join_copies(bufs, send_sems, recv_sems, layer):
    x, y, c = _mesh_pos()

    def copy(a, half):
        return pltpu.make_async_remote_copy(
            src_ref=bufs[a].at[layer, half], dst_ref=bufs[a].at[layer, half],
            send_sem=send_sems.at[a], recv_sem=recv_sems.at[a], device_id=(x, y, 1 - c), device_id_type=MESH)

    return [(copy(a, c), copy(a, 1 - c)) for a in range(len(bufs))]


def _sibling_join_halves(name, bufs, layer, after=()):
    n = len(bufs)

    def body(*refs):
        outs = refs[n + len(after):2 * n + len(after)]
        pairs = _join_copies(outs, refs[-2], refs[-1], layer)
        for send, _ in pairs:
            send.start()
        for send, recv in pairs:
            send.wait_send()
            recv.wait_recv()

    return pl.pallas_call(
        body, name=name, out_shape=[jax.ShapeDtypeStruct(b.shape, b.dtype) for b in bufs],
        in_specs=_any_specs(n + len(after)), out_specs=_any_specs(n),
        input_output_aliases={a: a for a in range(n)},
        scratch_shapes=[pltpu.SemaphoreType.DMA((n,)), pltpu.SemaphoreType.DMA((n,))],
    )(*bufs, *after)


def _join_start(name, bufs, layer):
    n = len(bufs)

    def body(*refs):
        token = refs[-1]
        for send, _ in _join_copies(refs[:n], refs[n], refs[n + 1], layer):
            send.start()
        token[...] = jnp.zeros_like(token)

    out = pl.pallas_call(
        body, name=name,
        out_shape=(pltpu.SemaphoreType.DMA((n,)), pltpu.SemaphoreType.DMA((n,)),
                   *[pltpu.HBM(b.shape, b.dtype) for b in bufs], jax.ShapeDtypeStruct((8, 128), F32)),
        in_specs=[HBM_SPEC] * n,
        out_specs=(SEM_SPEC, SEM_SPEC, *[HBM_SPEC] * n, pl.BlockSpec(memory_space=pltpu.VMEM)),
        input_output_aliases={a: 2 + a for a in range(n)},
        compiler_params=SPLIT_COPY,
    )(*[_hbm(b) for b in bufs])
    return out[0], out[1], list(out[2:2 + n]), out[-1]


def _join_wait(name, send_sems, recv_sems, bufs, layer, after):
    n = len(bufs)

    def body(*refs):
        for send, recv in _join_copies(refs[:n], refs[n], refs[n + 1], layer):
            send.wait_send()
            recv.wait_recv()

    out = pl.pallas_call(
        body, name=name,
        out_shape=[pltpu.HBM(b.shape, b.dtype) for b in bufs],
        in_specs=[HBM_SPEC] * n + [SEM_SPEC, SEM_SPEC, ANY_SPEC],
        out_specs=[HBM_SPEC] * n,
        input_output_aliases={a: a for a in range(n)},
        compiler_params=SPLIT_COPY,
    )(*bufs, send_sems, recv_sems, after)
    return list(out)


def _row_tile(rows, want):
    t = 1
    while t * 2 <= min(rows, want):
        t *= 2
    while rows % t:
        t //= 2
    return t


def _fit_rows(rows, cols, max_elems):
    for k in range(1, rows + 1):
        if rows % k == 0 and (rows // k) % 16 == 0 and (rows // k) * cols <= max_elems:
            return rows // k
    return _row_tile(rows, max(8, max_elems // cols))


def _add_halves(name, grads, recvd, c_idx):
    outs = []
    for a, (g, r) in enumerate(zip(grads, recvd)):
        _, _, r2, cc = g.shape
        tr = _fit_rows(r2, cc, 1536 * 1024)

        def body(c_ref, g_ref, r_ref, o_ref):
            o_ref[...] = (g_ref[...].astype(F32) + r_ref[...].astype(F32)).astype(BF16)

        outs.append(pl.pallas_call(
            body, name=f"{name}_{a}",
            grid_spec=pltpu.PrefetchScalarGridSpec(
                num_scalar_prefetch=1, grid=(4, r2 // tr),
                in_specs=[pl.BlockSpec((None, None, tr, cc), lambda j, i, c_ref: (j, c_ref[0], i, 0)),
                          pl.BlockSpec((None, tr, cc), lambda j, i, c_ref: (j, i, 0))],
                out_specs=pl.BlockSpec((None, tr, cc), lambda j, i, c_ref: (j, i, 0))),
            out_shape=jax.ShapeDtypeStruct(r.shape, BF16), compiler_params=_cp(("parallel", "parallel")),
        )(c_idx, g, r))
    return outs


def _sum_parts(name, parts, recvd, bufs, layer, where):
    outs = []
    for a, (p, r, buf) in enumerate(zip(parts, recvd, bufs)):
        _, r2, cc = p.shape
        tr = _fit_rows(r2, cc, 768 * 1024)

        def body(where_ref, own_ref, r1_ref, r2_ref, r3_ref, buf_ref, o_ref):
            acc = own_ref[...].astype(F32)
            for ref in (r1_ref, r2_ref, r3_ref):
                acc = acc + ref[...].astype(F32)
            o_ref[...] = acc

        def entry(k):
            return pl.BlockSpec((None, tr, cc), lambda i, where_ref: (where_ref[k], i, 0))

        outs.append(pl.pallas_call(
            body, name=f"{name}_{a}",
            grid_spec=pltpu.PrefetchScalarGridSpec(
                num_scalar_prefetch=1, grid=(r2 // tr,),
                in_specs=[entry(0), entry(1), entry(2), entry(3), ANY_SPEC],
                out_specs=pl.BlockSpec((None, None, tr, cc), lambda i, where_ref: (layer, where_ref[4], i, 0))),
            out_shape=jax.ShapeDtypeStruct(buf.shape, F32), input_output_aliases={5: 0},
            compiler_params=_cp(("parallel",)),
        )(where, p, r, r, r, buf))
    return outs


def _sum8(name, gathered):
    _, r, cc = gathered.shape
    tr = _row_tile(r, 512)

    def body(g_ref, o_ref):
        acc = g_ref[0].astype(F32)
        for i in range(1, 8):
            acc = acc + g_ref[i].astype(F32)
        o_ref[...] = acc

    return pl.pallas_call(
        body, name=name, grid=(r // tr,),
        in_specs=[pl.BlockSpec((8, tr, cc), lambda i: (0, i, 0))],
        out_specs=pl.BlockSpec((tr, cc), lambda i: (i, 0)),
        out_shape=jax.ShapeDtypeStruct((r, cc), F32), compiler_params=_cp(("parallel",)),
    )(gathered)


def _adamw(name, w, g, m, v):
    r, cc = w.shape
    tr = _row_tile(r, max(8, (512 * 1024) // cc))

    def body(w_ref, g_ref, m_ref, v_ref, d_ref, mo_ref, vo_ref):
        gg = g_ref[...]
        mn = ADAM_B1 * m_ref[...] + (1.0 - ADAM_B1) * gg
        vn = ADAM_B2 * v_ref[...] + (1.0 - ADAM_B2) * (gg * gg)
        m_hat = mn / (1.0 - ADAM_B1 ** ADAM_STEP)
        v_hat = vn / (1.0 - ADAM_B2 ** ADAM_STEP)
        d_ref[...] = -ADAM_LR * (m_hat / (jnp.sqrt(v_hat) + ADAM_EPS) + ADAM_WD * w_ref[...])
        mo_ref[...] = mn
        vo_ref[...] = vn

    spec = pl.BlockSpec((tr, cc), lambda i: (i, 0))
    return pl.pallas_call(
        body, name=name, grid=(r // tr,), in_specs=[spec] * 4, out_specs=[spec] * 3,
        out_shape=[jax.ShapeDtypeStruct(w.shape, F32)] * 3, compiler_params=_cp(("parallel",)),
    )(w, g, m, v)


def _rms_fwd(name, h, g, after=()):
    s, d = h.shape
    tr = _row_tile(s, 512)

    def body(h_ref, g_ref, *rest):
        y_ref = rest[-1]
        xv = h_ref[...]
        r = lax.rsqrt(jnp.mean(xv * xv, axis=-1, keepdims=True) + EPS)
        y_ref[...] = (xv * r * g_ref[...]).astype(BF16)

    return pl.pallas_call(
        body, name=name, grid=(s // tr,),
        in_specs=[pl.BlockSpec((tr, d), lambda i: (i, 0)), pl.BlockSpec((1, d), lambda i: (0, 0))]
        + [ANY_SPEC] * len(after),
        out_specs=pl.BlockSpec((tr, d), lambda i: (i, 0)),
        out_shape=jax.ShapeDtypeStruct((s, d), BF16), compiler_params=_cp(("parallel",)),
    )(h, g, *after)


def _rms_bwd_rows(xv, gv, dy):
    d = xv.shape[-1]
    r = lax.rsqrt(jnp.mean(xv * xv, axis=-1, keepdims=True) + EPS)
    dxn = dy * gv
    proj = jnp.sum(dxn * xv, axis=-1, keepdims=True) * (1.0 / d)
    dx = r * dxn - xv * (r * r * r) * proj
    return dx, dy * xv * r


def _rms_bwd(name, h, g, dy, dres, ple=None):
    s, d = h.shape
    n_ple = 0 if ple is None else 2
    tr = _row_tile(s, 256 if n_ple else 512)
    band = 16

    def body(h_ref, g_ref, dy_ref, dres_ref, *rest):
        ple_refs = rest[:n_ple]
        dh_ref, dhb_ref, dg_ref = rest[n_ple:n_ple + 3]
        ple_outs = rest[n_ple + 3:2 * n_ple + 3]
        acc = rest[-1]
        acc[...] = jnp.zeros_like(acc)

        def step(i, carry):
            rows = pl.ds(pl.multiple_of(i * band, band), band)
            dx, dgp = _rms_bwd_rows(h_ref[rows, :], g_ref[...], dy_ref[rows, :].astype(F32))
            dh = dres_ref[rows, :] + dx
            dh_ref[rows, :] = dh
            dhb_ref[rows, :] = dh.astype(BF16)
            acc[...] += _fold8(dgp)
            if n_ple:
                sg = _sigmoid(ple_refs[0][rows, :].astype(F32))
                ple_outs[0][rows, :] = (dh * sg).astype(BF16)
                ple_outs[1][rows, :] = (dh * ple_refs[1][rows, :].astype(F32) * sg * (1.0 - sg)).astype(BF16)
            return carry

        lax.fori_loop(0, tr // band, step, 0, unroll=4)

        @pl.when(pl.program_id(0) == 0)
        def _():
            dg_ref[...] = jnp.zeros_like(dg_ref)

        dg_ref[...] += jnp.sum(acc[...], axis=0, keepdims=True)

    row = pl.BlockSpec((tr, d), lambda i: (i, 0))
    vec = pl.BlockSpec((1, d), lambda i: (0, 0))
    return pl.pallas_call(
        body, name=name, grid=(s // tr,), in_specs=[row, vec, row, row] + [row] * n_ple,
        out_specs=[row, row, vec] + [row] * n_ple,
        out_shape=[jax.ShapeDtypeStruct((s, d), F32), jax.ShapeDtypeStruct((s, d), BF16),
                   jax.ShapeDtypeStruct((1, d), F32)] + [jax.ShapeDtypeStruct((s, d), BF16)] * n_ple,
        scratch_shapes=[pltpu.VMEM((8, d), F32)],
        compiler_params=_cp(("arbitrary",)),
    )(h, g, dy, dres, *(ple or ()))


def _loss_head(name, h, g, target):
    s, d = h.shape
    tr = _row_tile(s, 256)

    def body(h_ref, g_ref, t_ref, loss_ref, dh_ref, dhb_ref, dg_ref):
        xv = h_ref[...]
        gv = g_ref[...]
        r = lax.rsqrt(jnp.mean(xv * xv, axis=-1, keepdims=True) + EPS)
        err = xv * r * gv - t_ref[...]
        part = 0.5 * jnp.sum(jnp.sum(err * err, axis=-1, keepdims=True) * (1.0 / d), axis=0, keepdims=True)
        dx, dgp = _rms_bwd_rows(xv, gv, err * (1.0 / d))
        dh_ref[...] = dx
        dhb_ref[...] = dx.astype(BF16)

        @pl.when(pl.program_id(0) == 0)
        def _():
            dg_ref[...] = jnp.zeros_like(dg_ref)
            loss_ref[...] = jnp.zeros_like(loss_ref)

        dg_ref[...] += jnp.sum(dgp, axis=0, keepdims=True)
        loss_ref[...] += part

    row = pl.BlockSpec((tr, d), lambda i: (i, 0))
    vec = pl.BlockSpec((1, d), lambda i: (0, 0))
    one = pl.BlockSpec((1, 1), lambda i: (0, 0))
    return pl.pallas_call(
        body, name=name, grid=(s // tr,), in_specs=[row, vec, row], out_specs=[one, row, row, vec],
        out_shape=[jax.ShapeDtypeStruct((1, 1), F32), jax.ShapeDtypeStruct((s, d), F32),
                   jax.ShapeDtypeStruct((s, d), BF16), jax.ShapeDtypeStruct((1, d), F32)],
        compiler_params=_cp(("arbitrary",)),
    )(h, g, target)


NN = (((1,), (0,)), ((), ()))
NT = (((1,), (1,)), ((), ()))
TN = (((0,), (0,)), ((), ()))


def _matmul(name, grid, operands, in_specs, pairs, n_acc, acc_shape, epilogue, out_specs, out_shape, after=()):
    operands = list(operands) + list(after)
    in_specs = list(in_specs) + [ANY_SPEC] * len(after)
    n_in = len(operands)
    n_out = len(out_shape)
    nk = grid[2]

    def body(*refs):
        ins, outs, accs = refs[:n_in], refs[n_in:n_in + n_out], refs[n_in + n_out:]
        k = pl.program_id(2)

        @pl.when(k == 0)
        def _():
            for acc in accs:
                acc[...] = jnp.zeros_like(acc)

        for li, ri, ai, dims in pairs:
            accs[ai][...] += lax.dot_general(ins[li][...], ins[ri][...], dims, preferred_element_type=F32)

        @pl.when(k == nk - 1)
        def _():
            epilogue([acc[...] for acc in accs], ins, outs)

    return pl.pallas_call(
        body, name=name, grid=grid, in_specs=in_specs, out_specs=out_specs, out_shape=out_shape,
        scratch_shapes=[pltpu.VMEM(acc_shape, F32)] * n_acc,
        compiler_params=_cp(("parallel", "parallel", "arbitrary")),
    )(*operands)


def _ep_store(dtype):
    def ep(accs, ins, outs):
        outs[0][...] = accs[0].astype(dtype)
    return ep


def _w_spec_nt(wg, kind, to, tc):
    _, a, b = wg.shape
    if kind == "col":
        oph, cps = a // to, b // tc
        return pl.BlockSpec((None, to, tc), lambda i, jo, kc: (2 * (kc // cps) + jo // oph, jo % oph, kc % cps))
    ops, cph = a // to, b // tc
    return pl.BlockSpec((None, to, tc), lambda i, jo, kc: (2 * (jo // ops) + kc // cph, jo % ops, kc % cph))


SUB_COLS = 256


def _matmul_wide(name, grid, operands, in_specs, groups, tn, epilogue, out_specs, out_shape, after=()):
    operands = list(operands) + list(after)
    in_specs = list(in_specs) + [ANY_SPEC] * len(after)
    n_in = len(operands)

    def body(*refs):
        ins, outs = refs[:n_in], refs[n_in:]
        for s0 in range(0, tn, SUB_COLS):
            cols = slice(s0, min(s0 + SUB_COLS, tn))
            accs = []
            for group in groups:
                acc = None
                for li, (c0, cw), ri, dims in group:
                    rhs = ins[ri][:, cols] if dims == NN else ins[ri][cols, :]
                    part = lax.dot_general(ins[li][:, c0:c0 + cw], rhs, dims, preferred_element_type=F32)
                    acc = part if acc is None else acc + part
                accs.append(acc)
            epilogue(accs, ins, outs, cols)

    return pl.pallas_call(
        body, name=name, grid=grid, in_specs=in_specs, out_specs=out_specs, out_shape=out_shape,
        compiler_params=_cp(("parallel", "parallel")),
    )(*operands)


def _wide_store(dtype):
    def ep(accs, ins, outs, cols):
        outs[0][:, cols] = accs[0].astype(dtype)
    return ep


def _wide_nn_weights(wg, kind, tn, first):
    _, a, b = wg.shape
    per = b // tn
    if kind == "col":
        specs = [pl.BlockSpec((None, a, tn), lambda i, j, hf=hf: (2 * (j // per) + hf, 0, j % per)) for hf in range(2)]
    else:
        specs = [pl.BlockSpec((None, a, tn), lambda i, j, ch=ch: (2 * ch + j // per, 0, j % per)) for ch in range(4)]
    dots = [(0, (q * a, a), first + q, NN) for q in range(len(specs))]
    return [wg] * len(specs), specs, dots


def _wide_nt_weights(wg, kind, to, first):
    _, a, b = wg.shape
    per = a // to
    if kind == "col":
        specs = [pl.BlockSpec((None, to, b), lambda i, j, ch=ch: (2 * ch + j // per, j % per, 0)) for ch in range(4)]
    else:
        specs = [pl.BlockSpec((None, to, b), lambda i, j, hf=hf: (2 * (j // per) + hf, j % per, 0)) for hf in range(2)]
    dots = [(0, (q * b, b), first + q, NT) for q in range(len(specs))]
    return [wg] * len(specs), specs, dots


def _mm_nn_wide(name, x, wg, kind, tm, tn, epilogue=None, extra=(), extra_specs=(), out_specs=None, out_shape=None,
                after=()):
    m, kdim = x.shape
    _, a, b = wg.shape
    n = 4 * b if kind == "col" else 2 * b
    tm = min(tm, m)
    ws, wspecs, dots = _wide_nn_weights(wg, kind, tn, 1)
    if out_shape is None:
        out_shape = [jax.ShapeDtypeStruct((m, n), F32)]
        out_specs = [pl.BlockSpec((tm, tn), lambda i, j: (i, j))]
        epilogue = _wide_store(F32)
    return _matmul_wide(name, (m // tm, n // tn), [x] + ws + list(extra),
                        [pl.BlockSpec((tm, kdim), lambda i, j: (i, 0))] + wspecs + list(extra_specs),
                        [dots], tn, epilogue, out_specs, out_shape, after=after)


def _mm_nt_wide(name, dy, wg, kind, tm, to, out_dtype=BF16, after=()):
    m, n = dy.shape
    _, a, b = wg.shape
    kdim = 2 * a if kind == "col" else 4 * a
    tm = min(tm, m)
    ws, wspecs, dots = _wide_nt_weights(wg, kind, to, 1)
    return _matmul_wide(name, (m // tm, kdim // to), [dy] + ws,
                        [pl.BlockSpec((tm, n), lambda i, j: (i, 0))] + wspecs, [dots], to, _wide_store(out_dtype),
                        [pl.BlockSpec((tm, to), lambda i, j: (i, j))],
                        [jax.ShapeDtypeStruct((m, kdim), out_dtype)], after=after)[0]


def _mm_tn_wide(name, a, dy, kind, tr, tn):
    m, kdim = a.shape
    _, n = dy.shape

    def body(a_ref, dy_ref, o_ref):
        for s0 in range(0, tn, 2 * SUB_COLS):
            cols = slice(s0, min(s0 + 2 * SUB_COLS, tn))
            o_ref[:, cols] = lax.dot_general(a_ref[...], dy_ref[:, cols], TN, preferred_element_type=F32).astype(BF16)

    if kind == "col":
        ns = n // 4
        per = ns // tn
        out_shape = jax.ShapeDtypeStruct((4, kdim, ns), BF16)
        out_spec = pl.BlockSpec((None, tr, tn), lambda r, j: (j // per, r, j % per))
    else:
        rs = kdim // 4
        per = rs // tr
        out_shape = jax.ShapeDtypeStruct((4, rs, n), BF16)
        out_spec = pl.BlockSpec((None, tr, tn), lambda r, j: (r // per, r % per, j))
    return pl.pallas_call(
        body, name=name, grid=(kdim // tr, n // tn),
        in_specs=[pl.BlockSpec((m, tr), lambda r, j: (0, r)), pl.BlockSpec((m, tn), lambda r, j: (0, j))],
        out_specs=out_spec, out_shape=out_shape, compiler_params=_cp(("parallel", "parallel")),
    )(a, dy)


def _zero_halo(pad_ref, s):
    z = jnp.zeros((HALO, pad_ref.shape[1]), F32)
    pad_ref[pl.ds(0, HALO), :] = z
    pad_ref[pl.ds(HALO + s, HALO), :] = z


def _window(pad_ref, r0, rows):
    return pad_ref[pl.ds(r0, rows + 2 * HALO), :]


def _delayed(win, k, rows):
    if k == 0:
        return win[HALO:HALO + rows]
    return pltpu.roll(win, k, axis=0)[HALO:HALO + rows]


def _advanced(win, k, rows):
    if k == 0:
        return win[HALO:HALO + rows]
    return pltpu.roll(win, win.shape[0] - k, axis=0)[HALO:HALO + rows]


def _fold8(x):
    return jnp.sum(x.reshape(x.shape[0] // 8, 8, x.shape[1]), axis=0)


def _chunks(s):
    rows = min(CHUNK_ROWS, s)
    return rows, s // rows


def _col_spec(s, first_block):
    return pl.BlockSpec((s, HEAD), lambda j: (0, first_block + j))


def _sgu_fwd(name, z, ln_g, ln_b, w, b):
    s = z.shape[0]
    tr = _row_tile(s, 1024)
    nh = GROUP // HEAD

    def body(u_ref, v_ref, lg_ref, lb_ref, w_ref, b_ref, o_ref):
        row = lax.broadcasted_iota(jnp.int32, (HEAD, HEAD), 0)
        col = lax.broadcasted_iota(jnp.int32, (HEAD, HEAD), 1)
        wm = jnp.where(row >= col, w_ref[...], 0.0).astype(BF16)
        for ck in range(tr // HEAD):
            rs = pl.ds(ck * HEAD, HEAD)
            u = _gelu(u_ref[rs, :])
            v = _gelu(v_ref[rs, :])
            mu = jnp.mean(v, axis=-1, keepdims=True)
            vc = v - mu
            var = jnp.mean(vc * vc, axis=-1, keepdims=True)
            vln = vc * lax.rsqrt(var + EPS) * lg_ref[...] + lb_ref[...]
            sp = jnp.dot(wm, vln.astype(BF16), preferred_element_type=F32) + b_ref[...]
            o_ref[rs, :] = (u * sp).astype(BF16)

    head_vec = pl.BlockSpec((None, 1, HEAD), lambda h, i: (h, 0, 0))
    return pl.pallas_call(
        body, name=name, grid=(nh, s // tr),
        in_specs=[pl.BlockSpec((tr, HEAD), lambda h, i: (i, h)), pl.BlockSpec((tr, HEAD), lambda h, i: (i, nh + h)),
                  head_vec, head_vec, pl.BlockSpec((None, HEAD, HEAD), lambda h, i: (h, 0, 0)),
                  pl.BlockSpec((None, HEAD, 1), lambda h, i: (h, 0, 0))],
        out_specs=pl.BlockSpec((tr, HEAD), lambda h, i: (i, h)),
        out_shape=jax.ShapeDtypeStruct((s, GROUP), BF16), compiler_params=_cp(("parallel", "parallel")),
    )(z, z, ln_g, ln_b, w, b)


def _sgu_bwd(name, z, d_o, ln_g, ln_b, w, b):
    s = z.shape[0]
    tr = _row_tile(s, 1024)
    nh = GROUP // HEAD

    def body(u_ref, v_ref, do_ref, lg_ref, lb_ref, w_ref, b_ref, du_ref, dv_ref, dlg_ref, dlb_ref, dw_ref, db_ref,
             dsp_acc):
        row = lax.broadcasted_iota(jnp.int32, (HEAD, HEAD), 0)
        col = lax.broadcasted_iota(jnp.int32, (HEAD, HEAD), 1)
        tril = row >= col
        wm = jnp.where(tril, w_ref[...], 0.0).astype(BF16)
        i = pl.program_id(1)

        @pl.when(i == 0)
        def _():
            dlg_ref[...] = jnp.zeros_like(dlg_ref)
            dlb_ref[...] = jnp.zeros_like(dlb_ref)
            dw_ref[...] = jnp.zeros_like(dw_ref)
            dsp_acc[...] = jnp.zeros_like(dsp_acc)

        dlg = jnp.zeros((1, HEAD), F32)
        dlb = jnp.zeros((1, HEAD), F32)
        dw = jnp.zeros((HEAD, HEAD), F32)
        dsp_sum = jnp.zeros((HEAD, HEAD), F32)
        for ck in range(tr // HEAD):
            rs = pl.ds(ck * HEAD, HEAD)
            zu = u_ref[rs, :]
            zv = v_ref[rs, :]
            u = _gelu(zu)
            v = _gelu(zv)
            mu = jnp.mean(v, axis=-1, keepdims=True)
            vc = v - mu
            var = jnp.mean(vc * vc, axis=-1, keepdims=True)
            rstd = lax.rsqrt(var + EPS)
            xh = vc * rstd
            vln = (xh * lg_ref[...] + lb_ref[...]).astype(BF16)
            sp = jnp.dot(wm, vln, preferred_element_type=F32) + b_ref[...]
            d_oa = do_ref[rs, :].astype(F32)
            du = d_oa * sp
            dsp = d_oa * u
            dsp_b = dsp.astype(BF16)
            dvln = lax.dot_general(wm, dsp_b, TN, preferred_element_type=F32)
            dw = dw + lax.dot_general(dsp_b, vln, NT, preferred_element_type=F32)
            dsp_sum = dsp_sum + dsp
            dlg = dlg + jnp.sum(dvln * xh, axis=0, keepdims=True)
            dlb = dlb + jnp.sum(dvln, axis=0, keepdims=True)
            dxh = dvln * lg_ref[...]
            dv = rstd * (dxh - jnp.mean(dxh, axis=-1, keepdims=True)
                         - xh * jnp.mean(dxh * xh, axis=-1, keepdims=True))
            du_ref[rs, :] = (du * _gelu_grad(zu)).astype(BF16)
            dv_ref[rs, :] = (dv * _gelu_grad(zv)).astype(BF16)
        dlg_ref[...] += dlg
        dlb_ref[...] += dlb
        dw_ref[...] += jnp.where(tril, dw, 0.0)
        dsp_acc[...] += dsp_sum

        @pl.when(i == pl.num_programs(1) - 1)
        def _():
            db_ref[...] = jnp.sum(dsp_acc[...], axis=1, keepdims=True)

    head_vec = pl.BlockSpec((None, 1, HEAD), lambda h, i: (h, 0, 0))
    head_mat = pl.BlockSpec((None, HEAD, HEAD), lambda h, i: (h, 0, 0))
    head_col = pl.BlockSpec((None, HEAD, 1), lambda h, i: (h, 0, 0))
    return pl.pallas_call(
        body, name=name, grid=(nh, s // tr),
        in_specs=[pl.BlockSpec((tr, HEAD), lambda h, i: (i, h)), pl.BlockSpec((tr, HEAD), lambda h, i: (i, nh + h)),
                  pl.BlockSpec((tr, HEAD), lambda h, i: (i, h)), head_vec, head_vec, head_mat, head_col],
        out_specs=[pl.BlockSpec((tr, HEAD), lambda h, i: (i, h)), pl.BlockSpec((tr, HEAD), lambda h, i: (i, h)),
                   head_vec, head_vec, head_mat, head_col],
        out_shape=[jax.ShapeDtypeStruct((s, GROUP), BF16), jax.ShapeDtypeStruct((s, GROUP), BF16),
                   jax.ShapeDtypeStruct((nh, 1, HEAD), F32), jax.ShapeDtypeStruct((nh, 1, HEAD), F32),
                   jax.ShapeDtypeStruct((nh, HEAD, HEAD), F32), jax.ShapeDtypeStruct((nh, HEAD, 1), F32)],
        scratch_shapes=[pltpu.VMEM((HEAD, HEAD), F32)],
        compiler_params=_cp(("parallel", "arbitrary")),
    )(z, z, d_o, ln_g, ln_b, w, b)


def _shortconv_fwd(name, z, w):
    s = z.shape[0]
    kw = w.shape[0]
    rows, nchunk = _chunks(s)
    nb = GROUP // HEAD

    def body(h_ref, bg_ref, cg_ref, w_ref, o_ref, pad):
        _zero_halo(pad, s)

        def fill(ci, carry):
            r0 = pl.multiple_of(ci * rows, rows)
            pad[pl.ds(pl.multiple_of(HALO + r0, 8), rows), :] = cg_ref[pl.ds(r0, rows), :] * h_ref[pl.ds(r0, rows), :]
            return carry

        lax.fori_loop(0, nchunk, fill, 0)

        def step(ci, carry):
            r0 = pl.multiple_of(ci * rows, rows)
            win = _window(pad, r0, rows)
            cv = jnp.zeros((rows, HEAD), F32)
            for k in range(kw):
                cv = cv + w_ref[k:k + 1, :] * _delayed(win, kw - 1 - k, rows)
            o_ref[pl.ds(r0, rows), :] = (bg_ref[pl.ds(r0, rows), :] * cv).astype(BF16)
            return carry

        lax.fori_loop(0, nchunk, step, 0)

    return pl.pallas_call(
        body, name=name, grid=(nb,),
        in_specs=[_col_spec(s, 8), _col_spec(s, 12), _col_spec(s, 16), pl.BlockSpec((kw, HEAD), lambda j: (0, j))],
        out_specs=_col_spec(s, 0),
        out_shape=jax.ShapeDtypeStruct((s, GROUP), BF16),
        scratch_shapes=[pltpu.VMEM((s + 2 * HALO, HEAD), F32)],
        compiler_params=_cp(("parallel",)),
    )(z, z, z, w)


def _shortconv_bwd(name, z, d_o, w):
    s = z.shape[0]
    kw = w.shape[0]
    rows, nchunk = _chunks(s)
    nb = GROUP // HEAD

    def body(h_ref, bg_ref, cg_ref, do_ref, w_ref, dh_ref, dbg_ref, dcg_ref, dw_ref, pad_q, pad_d, acc):
        _zero_halo(pad_q, s)
        _zero_halo(pad_d, s)
        acc[...] = jnp.zeros_like(acc)

        def fill(ci, carry):
            r0 = pl.multiple_of(ci * rows, rows)
            rs = pl.ds(r0, rows)
            ps = pl.ds(pl.multiple_of(HALO + r0, 8), rows)
            pad_q[ps, :] = cg_ref[rs, :] * h_ref[rs, :]
            pad_d[ps, :] = do_ref[rs, :].astype(F32) * bg_ref[rs, :]
            return carry

        lax.fori_loop(0, nchunk, fill, 0)

        def step(ci, carry):
            r0 = pl.multiple_of(ci * rows, rows)
            rs = pl.ds(r0, rows)
            wq = _window(pad_q, r0, rows)
            wd = _window(pad_d, r0, rows)
            dcv = wd[HALO:HALO + rows]
            cv = jnp.zeros((rows, HEAD), F32)
            dq = jnp.zeros((rows, HEAD), F32)
            for k in range(kw):
                qk = _delayed(wq, kw - 1 - k, rows)
                cv = cv + w_ref[k:k + 1, :] * qk
                dq = dq + w_ref[k:k + 1, :] * _advanced(wd, kw - 1 - k, rows)
                acc[k] += _fold8(dcv * qk)
            dbg_ref[rs, :] = (do_ref[rs, :].astype(F32) * cv).astype(BF16)
            dcg_ref[rs, :] = (dq * h_ref[rs, :]).astype(BF16)
            dh_ref[rs, :] = (dq * cg_ref[rs, :]).astype(BF16)
            return carry

        lax.fori_loop(0, nchunk, step, 0)
        for k in range(kw):
            dw_ref[k:k + 1, :] = jnp.sum(acc[k], axis=0, keepdims=True)

    col = _col_spec(s, 0)
    return pl.pallas_call(
        body, name=name, grid=(nb,),
        in_specs=[_col_spec(s, 8), _col_spec(s, 12), _col_spec(s, 16), col, pl.BlockSpec((kw, HEAD), lambda j: (0, j))],
        out_specs=[col, col, col, pl.BlockSpec((kw, HEAD), lambda j: (0, j))],
        out_shape=[jax.ShapeDtypeStruct((s, GROUP), BF16)] * 3 + [jax.ShapeDtypeStruct((kw, GROUP), F32)],
        scratch_shapes=[pltpu.VMEM((s + 2 * HALO, HEAD), F32), pltpu.VMEM((s + 2 * HALO, HEAD), F32),
                        pltpu.VMEM((kw, 8, HEAD), F32)],
        compiler_params=_cp(("parallel",)),
    )(z, z, z, d_o, w)


def _conformer_conv_fwd(name, z, w, bias):
    s = z.shape[0]
    kw = w.shape[0]
    rows, nchunk = _chunks(s)
    nb = GROUP // HEAD

    def body(a_ref, g_ref, w_ref, b_ref, o_ref, pad):
        _zero_halo(pad, s)

        def fill(ci, carry):
            r0 = pl.multiple_of(ci * rows, rows)
            rs = pl.ds(r0, rows)
            pad[pl.ds(pl.multiple_of(HALO + r0, 8), rows), :] = a_ref[rs, :] * _sigmoid(g_ref[rs, :])
            return carry

        lax.fori_loop(0, nchunk, fill, 0)

        def step(ci, carry):
            r0 = pl.multiple_of(ci * rows, rows)
            win = _window(pad, r0, rows)
            cc = jnp.zeros((rows, HEAD), F32)
            for k in range(kw):
                cc = cc + w_ref[k:k + 1, :] * _delayed(win, kw - 1 - k, rows)
            o_ref[pl.ds(r0, rows), :] = cc + b_ref[...]
            return carry

        lax.fori_loop(0, nchunk, step, 0)

    return pl.pallas_call(
        body, name=name, grid=(nb,),
        in_specs=[_col_spec(s, 20), _col_spec(s, 24), pl.BlockSpec((kw, HEAD), lambda j: (0, j)),
                  pl.BlockSpec((1, HEAD), lambda j: (0, j))],
        out_specs=_col_spec(s, 0),
        out_shape=jax.ShapeDtypeStruct((s, GROUP), F32),
        scratch_shapes=[pltpu.VMEM((s + 2 * HALO, HEAD), F32)],
        compiler_params=_cp(("parallel",)),
    )(z, z, w, bias)


def _ln_rows(cc, g, b):
    mu = jnp.mean(cc, axis=-1, keepdims=True)
    xc = cc - mu
    var = jnp.mean(xc * xc, axis=-1, keepdims=True)
    rstd = lax.rsqrt(var + EPS)
    xh = xc * rstd
    return xh, rstd, xh * g + b


def _conformer_ln_fwd(name, cc, g, b):
    s, d = cc.shape
    tr = _row_tile(s, 512)

    def body(c_ref, g_ref, b_ref, o_ref):
        _, _, l = _ln_rows(c_ref[...], g_ref[...], b_ref[...])
        o_ref[...] = (l * _sigmoid(l)).astype(BF16)

    row = pl.BlockSpec((tr, d), lambda i: (i, 0))
    vec = pl.BlockSpec((1, d), lambda i: (0, 0))
    return pl.pallas_call(
        body, name=name, grid=(s // tr,), in_specs=[row, vec, vec], out_specs=row,
        out_shape=jax.ShapeDtypeStruct((s, d), BF16), compiler_params=_cp(("parallel",)),
    )(cc, g, b)


def _conformer_ln_bwd(name, cc, d_o, g, b):
    s, d = cc.shape
    tr = _row_tile(s, 512)

    def body(c_ref, do_ref, g_ref, b_ref, dcc_ref, dg_ref, db_ref, dcb_ref):
        xh, rstd, l = _ln_rows(c_ref[...], g_ref[...], b_ref[...])
        sg = _sigmoid(l)
        dl = do_ref[...].astype(F32) * sg * (1.0 + l * (1.0 - sg))
        dxh = dl * g_ref[...]
        dcc = rstd * (dxh - jnp.mean(dxh, axis=-1, keepdims=True) - xh * jnp.mean(dxh * xh, axis=-1, keepdims=True))
        dcc_ref[...] = dcc

        @pl.when(pl.program_id(0) == 0)
        def _():
            dg_ref[...] = jnp.zeros_like(dg_ref)
            db_ref[...] = jnp.zeros_like(db_ref)
            dcb_ref[...] = jnp.zeros_like(dcb_ref)

        dg_ref[...] += jnp.sum(dl * xh, axis=0, keepdims=True)
        db_ref[...] += jnp.sum(dl, axis=0, keepdims=True)
        dcb_ref[...] += jnp.sum(dcc, axis=0, keepdims=True)

    row = pl.BlockSpec((tr, d), lambda i: (i, 0))
    vec = pl.BlockSpec((1, d), lambda i: (0, 0))
    return pl.pallas_call(
        body, name=name, grid=(s // tr,), in_specs=[row, row, vec, vec], out_specs=[row, vec, vec, vec],
        out_shape=[jax.ShapeDtypeStruct((s, d), F32)] + [jax.ShapeDtypeStruct((1, d), F32)] * 3,
        compiler_params=_cp(("arbitrary",)),
    )(cc, d_o, g, b)


def _conformer_conv_bwd(name, z, dcc, w):
    s = z.shape[0]
    kw = w.shape[0]
    rows, nchunk = _chunks(s)
    nb = GROUP // HEAD

    def body(a_ref, g_ref, d_ref, w_ref, da_ref, dg_ref, dw_ref, pad_h, pad_d, acc):
        _zero_halo(pad_h, s)
        _zero_halo(pad_d, s)
        acc[...] = jnp.zeros_like(acc)

        def fill(ci, carry):
            r0 = pl.multiple_of(ci * rows, rows)
            rs = pl.ds(r0, rows)
            ps = pl.ds(pl.multiple_of(HALO + r0, 8), rows)
            pad_h[ps, :] = a_ref[rs, :] * _sigmoid(g_ref[rs, :])
            pad_d[ps, :] = d_ref[rs, :]
            return carry

        lax.fori_loop(0, nchunk, fill, 0)

        def step(ci, carry):
            r0 = pl.multiple_of(ci * rows, rows)
            rs = pl.ds(r0, rows)
            wh = _window(pad_h, r0, rows)
            wd = _window(pad_d, r0, rows)
            dcc_c = wd[HALO:HALO + rows]
            dhc = jnp.zeros((rows, HEAD), F32)
            for k in range(kw):
                dhc = dhc + w_ref[k:k + 1, :] * _advanced(wd, kw - 1 - k, rows)
                acc[k] += _fold8(dcc_c * _delayed(wh, kw - 1 - k, rows))
            sg = _sigmoid(g_ref[rs, :])
            da_ref[rs, :] = (dhc * sg).astype(BF16)
            dg_ref[rs, :] = (dhc * a_ref[rs, :] * sg * (1.0 - sg)).astype(BF16)
            return carry

        lax.fori_loop(0, nchunk, step, 0)
        for k in range(kw):
            dw_ref[k:k + 1, :] = jnp.sum(acc[k], axis=0, keepdims=True)

    col = _col_spec(s, 0)
    return pl.pallas_call(
        body, name=name, grid=(nb,),
        in_specs=[_col_spec(s, 20), _col_spec(s, 24), col, pl.BlockSpec((kw, HEAD), lambda j: (0, j))],
        out_specs=[col, col, pl.BlockSpec((kw, HEAD), lambda j: (0, j))],
        out_shape=[jax.ShapeDtypeStruct((s, GROUP), BF16)] * 2 + [jax.ShapeDtypeStruct((kw, GROUP), F32)],
        scratch_shapes=[pltpu.VMEM((s + 2 * HALO, HEAD), F32), pltpu.VMEM((s + 2 * HALO, HEAD), F32),
                        pltpu.VMEM((kw, 8, HEAD), F32)],
        compiler_params=_cp(("parallel",)),
    )(z, z, dcc, w)


def _pool_window_sum(win, level, rows, shift):
    n = win.shape[0]

    def moved(v, k):
        return pltpu.roll(v, k if shift is _delayed else n - k, axis=0)

    s2 = win + moved(win, 1)
    s4 = s2 + moved(s2, 2)
    s8 = s4 + moved(s4, 4)
    s16 = s8 + moved(s8, 8)
    sel = jnp.where(level == 0, s2, jnp.where(level == 1, s4, jnp.where(level == 2, s8, s16)))
    return sel[HALO:HALO + rows]


def _pool_count(level, r0, rows):
    t = r0 + lax.broadcasted_iota(jnp.int32, (rows, 1), 0)
    width = jnp.left_shift(jnp.int32(2), level)
    return jnp.minimum(t + 1, width).astype(F32)


def _pool_fwd(name, z, pool_w, scale):
    s = z.shape[0]
    rows, nchunk = _chunks(s)

    def body(z_ref, w_ref, sc_ref, o_ref, pad):
        level = pl.program_id(0)
        _zero_halo(pad, s)

        def fill(ci, carry):
            r0 = pl.multiple_of(ci * rows, rows)
            pad[pl.ds(pl.multiple_of(HALO + r0, 8), rows), :] = z_ref[pl.ds(r0, rows), :]
            return carry

        lax.fori_loop(0, nchunk, fill, 0)
        wb = w_ref[...].astype(BF16)

        def step(ci, carry):
            r0 = pl.multiple_of(ci * rows, rows)
            win = _window(pad, r0, rows)
            pm = _pool_window_sum(win, level, rows, _delayed) / _pool_count(level, r0, rows) - win[HALO:HALO + rows]
            r = jnp.dot(pm.astype(BF16), wb, preferred_element_type=F32)
            o_ref[pl.ds(r0, rows), :] = (r * sc_ref[...]).astype(BF16)
            return carry

        lax.fori_loop(0, nchunk, step, 0, unroll=min(4, nchunk))

    return pl.pallas_call(
        body, name=name, grid=(POOL_LEVELS,),
        in_specs=[_col_spec(s, 28), pl.BlockSpec((None, HEAD, HEAD), lambda j: (j, 0, 0)),
                  pl.BlockSpec((1, HEAD), lambda j: (0, j))],
        out_specs=_col_spec(s, 0),
        out_shape=jax.ShapeDtypeStruct((s, GROUP), BF16),
        scratch_shapes=[pltpu.VMEM((s + 2 * HALO, HEAD), F32)],
        compiler_params=_cp(("parallel",)),
    )(z, pool_w, scale)


def _pool_bwd(name, z, d_o, pool_w, scale):
    s = z.shape[0]
    rows, nchunk = _chunks(s)

    def body(z_ref, do_ref, w_ref, sc_ref, dz_ref, dw_ref, dsc_ref, pad, pad_q, dw_acc, dsc_acc):
        level = pl.program_id(0)
        _zero_halo(pad, s)
        _zero_halo(pad_q, s)
        dw_acc[...] = jnp.zeros_like(dw_acc)
        dsc_acc[...] = jnp.zeros_like(dsc_acc)

        def fill(ci, carry):
            r0 = pl.multiple_of(ci * rows, rows)
            pad[pl.ds(pl.multiple_of(HALO + r0, 8), rows), :] = z_ref[pl.ds(r0, rows), :]
            return carry

        lax.fori_loop(0, nchunk, fill, 0)
        wb = w_ref[...].astype(BF16)

        def first(ci, carry):
            r0 = pl.multiple_of(ci * rows, rows)
            win = _window(pad, r0, rows)
            cnt = _pool_count(level, r0, rows)
            pm = (_pool_window_sum(win, level, rows, _delayed) / cnt - win[HALO:HALO + rows]).astype(BF16)
            r = jnp.dot(pm, wb, preferred_element_type=F32)
            d_od = do_ref[pl.ds(r0, rows), :].astype(F32)
            dsc_acc[...] += _fold8(d_od * r)
            dr = (d_od * sc_ref[...]).astype(BF16)
            dw_acc[...] += lax.dot_general(pm, dr, TN, preferred_element_type=F32)
            dpm = lax.dot_general(dr, wb, NT, preferred_element_type=F32)
            pad_q[pl.ds(pl.multiple_of(HALO + r0, 8), rows), :] = dpm / cnt
            return carry

        lax.fori_loop(0, nchunk, first, 0, unroll=min(4, nchunk))

        def second(ci, carry):
            r0 = pl.multiple_of(ci * rows, rows)
            wq = _window(pad_q, r0, rows)
            dpm = wq[HALO:HALO + rows] * _pool_count(level, r0, rows)
            dz_ref[pl.ds(r0, rows), :] = (_pool_window_sum(wq, level, rows, _advanced) - dpm).astype(BF16)
            return carry

        lax.fori_loop(0, nchunk, second, 0)
        dw_ref[...] = dw_acc[...]
        dsc_ref[...] = jnp.sum(dsc_acc[...], axis=0, keepdims=True)

    col = _col_spec(s, 0)
    mat = pl.BlockSpec((None, HEAD, HEAD), lambda j: (j, 0, 0))
    vec = pl.BlockSpec((1, HEAD), lambda j: (0, j))
    return pl.pallas_call(
        body, name=name, grid=(POOL_LEVELS,),
        in_specs=[_col_spec(s, 28), col, mat, vec], out_specs=[col, mat, vec],
        out_shape=[jax.ShapeDtypeStruct((s, GROUP), BF16), jax.ShapeDtypeStruct((POOL_LEVELS, HEAD, HEAD), F32),
                   jax.ShapeDtypeStruct((1, GROUP), F32)],
        scratch_shapes=[pltpu.VMEM((s + 2 * HALO, HEAD), F32), pltpu.VMEM((s + 2 * HALO, HEAD), F32),
                        pltpu.VMEM((HEAD, HEAD), F32), pltpu.VMEM((8, HEAD), F32)],
        compiler_params=_cp(("parallel",)),
    )(z, d_o, pool_w, scale)


def _mm_residual(name, x, wg, h, tm, tn):
    m, d = h.shape
    tm = min(tm, m)

    def ep(accs, ins, outs, cols):
        outs[0][:, cols] = ins[5][:, cols] + accs[0]

    tile = pl.BlockSpec((tm, tn), lambda i, j: (i, j))
    return _mm_nn_wide(name, x, wg, "row", tm, tn, epilogue=ep, extra=[h], extra_specs=[tile], out_specs=[tile],
                       out_shape=[jax.ShapeDtypeStruct((m, d), F32)])[0]


def _swiglu_fwd(name, y, wg_gate, wg_up, tm):
    m, kdim = y.shape
    _, a, b = wg_gate.shape
    tm = min(tm, m)

    def ep(accs, ins, outs, cols):
        gt, up = accs
        outs[0][:, cols] = (gt * _sigmoid(gt) * up).astype(BF16)
        outs[1][:, cols] = gt.astype(BF16)
        outs[2][:, cols] = up.astype(BF16)

    gws, gspecs, gdots = _wide_nn_weights(wg_gate, "col", b, 1)
    uws, uspecs, udots = _wide_nn_weights(wg_up, "col", b, 1 + len(gws))
    out = pl.BlockSpec((tm, b), lambda i, j: (i, j))
    return _matmul_wide(name, (m // tm, 4), [y] + gws + uws,
                        [pl.BlockSpec((tm, kdim), lambda i, j: (i, 0))] + gspecs + uspecs, [gdots, udots], b, ep,
                        [out] * 3, [jax.ShapeDtypeStruct((m, 4 * b), BF16)] * 3)


def _swiglu_bwd(name, dh, wg_down, gate, up, tm):
    m, n = dh.shape
    _, a, b = wg_down.shape
    tm = min(tm, m)

    def ep(accs, ins, outs, cols):
        d_act = accs[0]
        gt = ins[3][:, cols].astype(F32)
        upv = ins[4][:, cols].astype(F32)
        sg = _sigmoid(gt)
        outs[0][:, cols] = (d_act * upv * sg * (1.0 + gt * (1.0 - sg))).astype(BF16)
        outs[1][:, cols] = (d_act * gt * sg).astype(BF16)
        outs[2][:, cols] = (gt * sg * upv).astype(BF16)

    ws, wspecs, dots = _wide_nt_weights(wg_down, "row", a, 1)
    tile = pl.BlockSpec((tm, a), lambda i, j: (i, j))
    return _matmul_wide(name, (m // tm, 4), [dh] + ws + [gate, up],
                        [pl.BlockSpec((tm, n), lambda i, j: (i, 0))] + wspecs + [tile, tile], [dots], a, ep,
                        [tile] * 3, [jax.ShapeDtypeStruct((m, 4 * a), BF16)] * 3)


def _ffn_dy(name, d_gate, d_up, wg_gate, wg_up, tiles, after=()):
    m, n = d_gate.shape
    _, a, b = wg_gate.shape
    kdim = 2 * a
    tm, to, tc = tiles
    tm = min(tm, m)
    grid = (m // tm, kdim // to, n // tc)
    lhs = pl.BlockSpec((tm, tc), lambda i, j, k: (i, k))
    wspec = _w_spec_nt(wg_gate, "col", to, tc)
    return _matmul(name, grid, [d_gate, d_up, wg_gate, wg_up], [lhs, lhs, wspec, wspec],
                   [(0, 2, 0, NT), (1, 3, 0, NT)], 1, (tm, to), _ep_store(BF16),
                   [pl.BlockSpec((tm, to), lambda i, j, k: (i, j))], [jax.ShapeDtypeStruct((m, kdim), BF16)],
                   after=after)[0]


def _ple_fwd(name, y, wg, h, pp, tm, tn, after=()):
    m, d = h.shape
    tm = min(tm, m)

    def ep(accs, ins, outs, cols):
        pg = accs[0]
        outs[0][:, cols] = ins[5][:, cols] + _sigmoid(pg) * ins[6][:, cols].astype(F32)
        outs[1][:, cols] = pg.astype(BF16)

    tile = pl.BlockSpec((tm, tn), lambda i, j: (i, j))
    return _mm_nn_wide(name, y, wg, "row", tm, tn, epilogue=ep, extra=[h, pp], extra_specs=[tile, tile],
                       out_specs=[tile, tile],
                       out_shape=[jax.ShapeDtypeStruct((m, d), F32), jax.ShapeDtypeStruct((m, d), BF16)], after=after)


def _ple_bwd(name, dh, pg, pp, after=()):
    s, d = dh.shape
    tr = _row_tile(s, 512)

    def body(dh_ref, pg_ref, pp_ref, *rest):
        dpp_ref, dpg_ref = rest[-2:]
        dhv = dh_ref[...]
        sg = _sigmoid(pg_ref[...].astype(F32))
        dpp_ref[...] = (dhv * sg).astype(BF16)
        dpg_ref[...] = (dhv * pp_ref[...].astype(F32) * sg * (1.0 - sg)).astype(BF16)

    row = pl.BlockSpec((tr, d), lambda i: (i, 0))
    return pl.pallas_call(
        body, name=name, grid=(s // tr,), in_specs=[row] * 3 + [ANY_SPEC] * len(after), out_specs=[row] * 2,
        out_shape=[jax.ShapeDtypeStruct((s, d), BF16)] * 2, compiler_params=_cp(("parallel",)),
    )(dh, pg, pp, *after)


BIG = ["w_in", "w_out", "w_gate", "w_up", "w_down", "w_ple_gate", "w_ple_proj"]
KIND = {"w_in": "col", "w_out": "row", "w_gate": "col", "w_up": "col", "w_down": "row", "w_ple_gate": "row",
        "w_ple_proj": "col"}
GATHER_GROUPS = (("w_in", "w_out"), ("w_gate", "w_up"), ("w_down", "w_ple_gate", "w_ple_proj"))
RS_GROUPS = (("w_ple_gate", "w_ple_proj", "w_down", "w_gate", "w_up"), ("w_out", "w_in"))
SMALL = ["norm_mix_g", "sgu_ln_g", "sgu_ln_b", "sgu_w", "sgu_b", "sc_conv_w", "cf_conv_w", "cf_conv_b", "cf_ln_g",
         "cf_ln_b", "pool_w", "pool_scale", "norm_ffn_g", "norm_ple_g", "final_norm_g"]
CHIP_SPLIT = ["sc_conv_w", "cf_conv_w"]
MATRIX_SMALL = ["sgu_w", "pool_w"]
WEIGHTS = ['norm_mix_g', 'w_in', 'sgu_ln_g', 'sgu_ln_b', 'sgu_w', 'sgu_b', 'sc_conv_w', 'cf_conv_w', 'cf_conv_b',
           'cf_ln_g', 'cf_ln_b', 'pool_w', 'pool_scale', 'w_out', 'norm_ffn_g', 'w_gate', 'w_up', 'w_down',
           'norm_ple_g', 'w_ple_gate', 'w_ple_proj', 'final_norm_g']


def _tile(n, want):
    if n <= want:
        return n
    t = (want // 128) * 128
    while n % t:
        t -= 128
    return t


def _pack_rows(vecs):
    flat = jnp.concatenate([v.reshape(-1) for v in vecs])
    n = flat.shape[0]
    quantum = PACK_ROWS * 128
    padded = ((n + quantum - 1) // quantum) * quantum
    return jnp.pad(flat, (0, padded - n)).reshape(padded // 128, 128), n


def _unpack(flat, shapes):
    out, off = [], 0
    for shp in shapes:
        size = math.prod(shp)
        out.append(flat[off:off + size].reshape(shp))
        off += size
    return out


def kernel(x, p, norm_mix_g, w_in, sgu_ln_g, sgu_ln_b, sgu_w, sgu_b, sc_conv_w, cf_conv_w, cf_conv_b, cf_ln_g, cf_ln_b, pool_w, pool_scale, w_out, norm_ffn_g, w_gate, w_up, w_down, norm_ple_g, w_ple_gate, w_ple_proj, final_norm_g, loss_target, m_norm_mix_g, m_w_in, m_sgu_ln_g, m_sgu_ln_b, m_sgu_w, m_sgu_b, m_sc_conv_w, m_cf_conv_w, m_cf_conv_b, m_cf_ln_g, m_cf_ln_b, m_pool_w, m_pool_scale, m_w_out, m_norm_ffn_g, m_w_gate, m_w_up, m_w_down, m_norm_ple_g, m_w_ple_gate, m_w_ple_proj, m_final_norm_g, v_norm_mix_g, v_w_in, v_sgu_ln_g, v_sgu_ln_b, v_sgu_w, v_sgu_b, v_sc_conv_w, v_cf_conv_w, v_cf_conv_b, v_cf_ln_g, v_cf_ln_b, v_pool_w, v_pool_scale, v_w_out, v_norm_ffn_g, v_w_gate, v_w_up, v_w_down, v_norm_ple_g, v_w_ple_gate, v_w_ple_proj, v_final_norm_g):
    args = dict(locals())
    w = {n: args[n] for n in WEIGHTS}
    mom = {n: args["m_" + n] for n in WEIGHTS}
    var = {n: args["v_" + n] for n in WEIGHTS}
    depth = w_in.shape[0]
    s, d = x.shape[1], x.shape[2]
    f_dim = 4 * w_gate.shape[2]
    xi, yi, ci = lax.axis_index("x"), lax.axis_index("y"), lax.axis_index("c")
    c_idx = ci.astype(jnp.int32).reshape(1)

    chip_idx = (2 * xi + yi).astype(jnp.int32).reshape(1)

    def start_gathers(l, dep):
        pending = []
        for gi, names in enumerate(GATHER_GROUPS):
            lands = [_cast_into_landing(f"cast_{n}", w[n], l, 0 if KIND[n] == "col" else 1, chip_idx) for n in names]
            pending.append(_gather_start(f"gather_start_{l}_{gi}", lands, dep))
            dep = pending[-1][3]
        return pending, dep

    def receive(l, gi, pending, after):
        send_sems, recv_sems, lands, _ = pending
        lands = _gather_wait(f"gather_wait_{l}_{gi}", send_sems, recv_sems, lands, after)
        return _share_start(f"gather_share_start_{l}_{gi}", lands)

    def complete(l, gi, share, after):
        send_sems, recv_sems, lands, _ = share
        lands = _share_wait(f"gather_share_wait_{l}_{gi}", send_sems, recv_sems, lands, after)
        return dict(zip(GATHER_GROUPS[gi], lands))

    conv_pack = jnp.concatenate([sc_conv_w, cf_conv_w], axis=1)
    taps = conv_pack.shape[1]
    rows_pad = ((depth * taps + 7) // 8) * 8
    conv_rows = jnp.pad(conv_pack.reshape(depth * taps, HEAD), ((0, rows_pad - depth * taps), (0, 0)))
    conv_all = _allgather8("gather_conv_weights", [conv_rows])[0]
    conv_full = conv_all[0::2, :depth * taps].reshape(4, depth, taps, HEAD)
    conv_full = jnp.transpose(conv_full, (1, 2, 0, 3)).reshape(depth, taps, GROUP)
    sc_w_full, cf_w_full = conv_full[:, :3], conv_full[:, 3:]

    pending, token = start_gathers(0, conv_all)
    h = x[0]
    saved = []
    gathered = []
    shares = [None] * len(GATHER_GROUPS)
    for l in range(depth):
        just_in_time = l < 2
        ahead = l + 1 < depth and l + 1 >= 2
        if just_in_time:
            shares[0] = receive(l, 0, pending[0], token if l == 0 else h)
        wg = complete(l, 0, shares[0], h)
        gathered.append(wg)
        started = ()
        this_layer = pending
        if l + 1 < depth:
            pending, token = start_gathers(l + 1, wg["w_in"])
            started = (token,)
        sv = {"h0": h}
        y1 = _rms_fwd("rms_mix", h, norm_mix_g[l:l + 1], after=started)
        z = _mm_nn_wide("mm_in", y1, wg["w_in"], "col", 1024, 1024)[0]
        lg, lb = sgu_ln_g[l][:, None, :], sgu_ln_b[l][:, None, :]
        sb = sgu_b[l][:, :, None]
        oa = _sgu_fwd("sgu_fwd", z, lg, lb, sgu_w[l], sb)
        ob = _shortconv_fwd("shortconv_fwd", z, sc_w_full[l])
        cc = _conformer_conv_fwd("conformer_conv_fwd", z, cf_w_full[l], cf_conv_b[l:l + 1])
        oc = _conformer_ln_fwd("conformer_ln_fwd", cc, cf_ln_g[l:l + 1], cf_ln_b[l:l + 1])
        od = _pool_fwd("pool_fwd", z, pool_w[l], pool_scale[l:l + 1])
        o = jnp.concatenate([oa, ob, oc, od], axis=1)
        h1 = _mm_residual("mm_out", o, wg["w_out"], h, 1024, 1024)
        if just_in_time:
            shares[1] = receive(l, 1, this_layer[1], h1)
        wg.update(complete(l, 1, shares[1], h1))
        y2 = _rms_fwd("rms_ffn", h1, norm_ffn_g[l:l + 1])
        act, gt, up = _swiglu_fwd("mm_swiglu", y2, wg["w_gate"], wg["w_up"], 512)
        if just_in_time:
            shares[2] = receive(l, 2, this_layer[2], act)
        wg.update(complete(l, 2, shares[2], act))
        h2 = _mm_residual("mm_down", act, wg["w_down"], h1, 1024, 512)
        started = ()
        if ahead:
            shares[0] = receive(l + 1, 0, pending[0], h2)
            shares[1] = receive(l + 1, 1, pending[1], shares[0][3])
            started = (shares[1][3],)
        y3 = _rms_fwd("rms_ple", h2, norm_ple_g[l:l + 1], after=started)
        pb = p[l, 0].astype(BF16)
        ptile = pl.BlockSpec((min(1024, s), 512), lambda i, j: (i, j))
        pp = _mm_nn_wide("mm_ple_proj", pb, wg["w_ple_proj"], "col", 1024, 512, epilogue=_wide_store(BF16),
                         out_specs=[ptile], out_shape=[jax.ShapeDtypeStruct((s, d), BF16)])[0]
        started = ()
        if ahead:
            shares[2] = receive(l + 1, 2, pending[2], pp)
            started = (shares[2][3],)
        h3, pg = _ple_fwd("mm_ple_gate", y3, wg["w_ple_gate"], h2, pp, 1024, 1024, after=started)
        sv.update(y1=y1, z=z, cc=cc, o=o, h1=h1, y2=y2, gt=gt, up=up, h2=h2, y3=y3, pb=pb, pp=pp, pg=pg)
        saved.append(sv)
        h = h3

    loss_part, dh, dhb, d_final_g = _loss_head("loss_head", h, final_norm_g[None, :], loss_target[0])

    small_grads = [None] * depth
    where = jnp.stack([2 * xi + yi, 2 * (1 - xi) + yi, 2 * xi + (1 - yi), 2 * (1 - xi) + (1 - yi), ci]).astype(jnp.int32)
    grad_bufs = {n: lax.empty((depth, 2, w[n].shape[1] // 2, w[n].shape[2]), F32) for n in BIG}
    exchanges = [None] * len(RS_GROUPS)
    joins = []
    behind_join = ()
    ple_grads = None

    def halves(g):
        return g.reshape(4, 2, g.shape[1] // 2, g.shape[2])

    def start_exchange(layer, gi, sibling, after):
        send_sems, recv_sems, gs, lands, _ = sibling
        gs, lands = _sibling_wait(f"rs_sibling_wait_{layer}_{gi}", send_sems, recv_sems, gs, lands, after)
        chip_sums = _add_halves("rs_add", gs, lands, c_idx)
        return _chip_exchange_start(f"rs_chips_start_{layer}_{gi}", chip_sums)

    def finish_exchange(layer, gi, after):
        send_sems, recv_sems, parts, lands, _ = exchanges[gi]
        parts, lands = _chip_exchange_wait(f"rs_chips_wait_{layer}_{gi}", send_sems, recv_sems, parts, lands, after)
        names = RS_GROUPS[gi]
        sums = _sum_parts("rs_sum", parts, lands, [grad_bufs[n] for n in names], layer, where)
        grad_bufs.update(zip(names, sums))

    for l in reversed(range(depth)):
        wg = gathered[l]
        sv = saved[l]
        fs = f_dim // 4
        started = () if exchanges[1] is None else (exchanges[1][4],)
        if ple_grads is None:
            ple_grads = _ple_bwd("ple_bwd", dh, sv["pg"], sv["pp"])
        d_pp, d_pg = ple_grads
        g_ple_proj = _mm_tn_wide("dw_ple_proj", sv["pb"], d_pp, "col", w_ple_proj.shape[1], 512)
        g_ple_gate = _mm_tn_wide("dw_ple_gate", sv["y3"], d_pg, "row", 512, 1024)
        dy3 = _mm_nt_wide("dx_ple_gate", d_pg, wg["w_ple_gate"], "row", 1024, 512, after=started + behind_join)
        dh, dhb, dg_ple = _rms_bwd("rms_ple_bwd", sv["h2"], norm_ple_g[l:l + 1], dy3, dh)

        d_gt, d_up, act = _swiglu_bwd("dx_down_swiglu", dhb, wg["w_down"], sv["gt"], sv["up"], 512)
        g_down = _mm_tn_wide("dw_down", act, dhb, "row", fs, 512)
        g_gate = _mm_tn_wide("dw_gate", sv["y2"], d_gt, "col", 512, fs)
        g_up = _mm_tn_wide("dw_up", sv["y2"], d_up, "col", 512, fs)
        big = dict(w_gate=g_gate, w_up=g_up, w_down=g_down, w_ple_gate=g_ple_gate, w_ple_proj=g_ple_proj)
        sibling = _sibling_start(f"rs_sibling_start_{l}_0", [halves(big[n]) for n in RS_GROUPS[0]])
        dy2 = _ffn_dy("dx_gate_up", d_gt, d_up, wg["w_gate"], wg["w_up"], (1024, 1024, fs), after=(sibling[4],))
        dh, dhb, dg_ffn = _rms_bwd("rms_ffn_bwd", sv["h1"], norm_ffn_g[l:l + 1], dy2, dh)

        g_out = _mm_tn_wide("dw_out", sv["o"], dhb, "row", 512, 1024)
        if exchanges[0] is not None:
            finish_exchange(l + 1, 0, g_out)
        exchanges[0] = start_exchange(l, 0, sibling, g_out)
        d_o = _mm_nt_wide("dx_out", dhb, wg["w_out"], "row", 1024, 512, after=(exchanges[0][4],))
        z = sv["z"]
        lg, lb = sgu_ln_g[l][:, None, :], sgu_ln_b[l][:, None, :]
        sb = sgu_b[l][:, :, None]
        dzu, dzv, d_lg, d_lb, d_sw, d_sb = _sgu_bwd("sgu_bwd", z, d_o[:, 0:GROUP], lg, lb, sgu_w[l], sb)
        dzh, dzbg, dzcg, d_scw = _shortconv_bwd("shortconv_bwd", z, d_o[:, GROUP:2 * GROUP], sc_w_full[l])
        dcc, d_cflg, d_cflb, d_cfb = _conformer_ln_bwd("conformer_ln_bwd", sv["cc"], d_o[:, 2 * GROUP:3 * GROUP],
                                                       cf_ln_g[l:l + 1], cf_ln_b[l:l + 1])
        dza, dzg, d_cfw = _conformer_conv_bwd("conformer_conv_bwd", z, dcc, cf_w_full[l])
        dzd, d_pw, d_psc = _pool_bwd("pool_bwd", z, d_o[:, 3 * GROUP:], pool_w[l], pool_scale[l:l + 1])
        dz = jnp.concatenate([dzu, dzv, dzh, dzbg, dzcg, dza, dzg, dzd], axis=1)
        g_in = _mm_tn_wide("dw_in", sv["y1"], dz, "col", 512, 1024)
        big.update(w_out=g_out, w_in=g_in)
        sibling = _sibling_start(f"rs_sibling_start_{l}_1", [halves(big[n]) for n in RS_GROUPS[1]])
        dy1 = _mm_nt_wide("dx_in", dz, wg["w_in"], "col", 1024, 512, after=(sibling[4],))
        below = None if l == 0 else (saved[l - 1]["pg"], saved[l - 1]["pp"])
        dh, dhb, dg_mix, *ple_grads = _rms_bwd("rms_mix_bwd", sv["h0"], norm_mix_g[l:l + 1], dy1, dh, ple=below)
        if exchanges[1] is not None:
            finish_exchange(l + 1, 1, dh)
            send_sems, recv_sems, bufs, join_token = _join_start(f"rs_join_start_{l + 1}",
                                                                 [grad_bufs[n] for n in BIG], l + 1)
            grad_bufs.update(zip(BIG, bufs))
            joins.append((l + 1, send_sems, recv_sems))
            behind_join = (join_token,)
        exchanges[1] = start_exchange(l, 1, sibling, dh)

        small_grads[l] = dict(norm_mix_g=dg_mix, sgu_ln_g=d_lg, sgu_ln_b=d_lb, sgu_w=d_sw, sgu_b=d_sb,
                              sc_conv_w=d_scw, cf_conv_w=d_cfw, cf_conv_b=d_cfb, cf_ln_g=d_cflg, cf_ln_b=d_cflb,
                              pool_w=d_pw, pool_scale=d_psc, norm_ffn_g=dg_ffn, norm_ple_g=dg_ple)
    grad_x = dh[None]

    grads, delta, new_m, new_v = {}, {}, {}, {}
    per_layer = [n for n in SMALL if n != "final_norm_g"]
    narrow = [n for n in per_layer if n not in MATRIX_SMALL]
    packed, _ = _pack_rows([small_grads[l][n] for n in narrow for l in range(depth)] + [d_final_g, loss_part])
    packed_wide, _ = _pack_rows([small_grads[l][n] for n in MATRIX_SMALL for l in range(depth)])
    everyone = _allgather8("gather_small", [packed, packed_wide.astype(BF16)])
    total = _sum8("sum_small", everyone[0]).reshape(-1)
    total_wide = _sum8("sum_small_wide", everyone[1]).reshape(-1)

    def stacked_shape(n):
        return (depth,) + (w[n].shape[1:] if n not in CHIP_SPLIT else (w[n].shape[1], GROUP))

    pieces = _unpack(total, [stacked_shape(n) for n in narrow] + [(d,), ()])
    loss = pieces[-1]
    grads["final_norm_g"] = pieces[-2]
    chip_off = (2 * xi + yi) * HEAD
    for n, g in zip(narrow, pieces):
        grads[n] = lax.dynamic_slice_in_dim(g, chip_off, HEAD, axis=2) if n in CHIP_SPLIT else g
    grads.update(zip(MATRIX_SMALL, _unpack(total_wide, [stacked_shape(n) for n in MATRIX_SMALL])))

    finish_exchange(0, 0, dh)
    for layer, send_sems, recv_sems in joins:
        bufs = _join_wait(f"rs_join_wait_{layer}", send_sems, recv_sems, [grad_bufs[n] for n in BIG], layer, total)
        grad_bufs.update(zip(BIG, bufs))
    behind = (exchanges[1][4],)
    for gi, names in enumerate(RS_GROUPS):
        if gi == 1:
            finish_exchange(0, 1, behind[0])
        joined = _sibling_join_halves(f"rs_join_{gi}", [grad_bufs[n] for n in names], 0, after=behind)
        for n, g in zip(names, joined):
            shp = w[n].shape
            two_d = (shp[0] * shp[1], shp[2])
            grads[n] = g.reshape(shp)
            dl, mn, vn = _adamw(f"adamw_{n}", w[n].reshape(two_d), g.reshape(two_d), mom[n].reshape(two_d),
                                var[n].reshape(two_d))
            delta[n], new_m[n], new_v[n] = dl.reshape(shp), mn.reshape(shp), vn.reshape(shp)
            behind = (dl,)
    small_shapes = [w[n].shape for n in SMALL]
    pw, _ = _pack_rows([w[n] for n in SMALL])
    pg_, _ = _pack_rows([grads[n] for n in SMALL])
    pm, _ = _pack_rows([mom[n] for n in SMALL])
    pv, _ = _pack_rows([var[n] for n in SMALL])
    dl, mn, vn = _adamw("adamw_small", pw, pg_, pm, pv)
    for n, a, b, cc_ in zip(SMALL, _unpack(dl.reshape(-1), small_shapes), _unpack(mn.reshape(-1), small_shapes),
                            _unpack(vn.reshape(-1), small_shapes)):
        delta[n], new_m[n], new_v[n] = a, b, cc_

    return (loss, grad_x, *[grads[n] for n in WEIGHTS], *[delta[n] for n in WEIGHTS],
            *[new_m[n] for n in WEIGHTS], *[new_v[n] for n in WEIGHTS])
```

```python
import functools
import math

import jax
import jax.numpy as jnp
from jax import lax
from jax.experimental import pallas as pl
from jax.experimental.pallas import tpu as pltpu

F32 = jnp.float32
BF16 = jnp.bfloat16
MESH = pl.DeviceIdType.MESH

HEAD = 128
GROUP = 512
EPS = 1e-6
HALO = 32
CHUNK_ROWS = 128
POOL_LEVELS = 4
PACK_ROWS = 512

ADAM_LR = 0.001
ADAM_B1 = 0.9
ADAM_B2 = 0.999
ADAM_EPS = 1e-08
ADAM_WD = 0.01
ADAM_STEP = 10

VMEM_LIMIT = 56 * 1024 * 1024


def _cp(sem=None, vmem=VMEM_LIMIT):
    return pltpu.CompilerParams(dimension_semantics=sem, vmem_limit_bytes=vmem)


def _sigmoid(x):
    return 0.5 * jnp.tanh(0.5 * x) + 0.5


_GELU_K = math.sqrt(2.0 / math.pi)
_GELU_C = 0.044715


def _gelu(x):
    t = jnp.tanh(_GELU_K * (x + _GELU_C * x * x * x))
    return 0.5 * x * (1.0 + t)


def _gelu_grad(x):
    t = jnp.tanh(_GELU_K * (x + _GELU_C * x * x * x))
    return 0.5 * (1.0 + t) + 0.5 * x * (1.0 - t * t) * _GELU_K * (1.0 + 3.0 * _GELU_C * x * x)


def _mesh_pos():
    return lax.axis_index("x"), lax.axis_index("y"), lax.axis_index("c")


def _any_specs(n):
    return [pl.BlockSpec(memory_space=pl.ANY)] * n


def _allgather8(name, blocks):
    n = len(blocks)

    def body(*refs):
        ins, outs = refs[:n], refs[n:2 * n]
        send_sems, recv_sems, local_sems = refs[2 * n:]
        x, y, c = _mesh_pos()
        me, sibling = (x, y, c), (x, y, 1 - c)
        chips = [(1 - x, y), (x, 1 - y), (1 - x, 1 - y)]

        def slot(a, dev):
            return outs[a].at[4 * dev[0] + 2 * dev[1] + dev[2]]

        def copy(a, k, block, to, src=None):
            dst = slot(a, block)
            return pltpu.make_async_remote_copy(
                src_ref=dst if src is None else src, dst_ref=dst,
                send_sem=send_sems.at[7 * a + k], recv_sem=recv_sems.at[7 * a + k],
                device_id=to, device_id_type=MESH)

        mine, first, passed = [], [], []
        for a in range(n):
            cp = pltpu.make_async_copy(ins[a], slot(a, me), local_sems.at[a])
            cp.start()
            mine.append(cp)
            cps = [copy(a, 0, me, sibling, src=ins[a])]
            cps += [copy(a, 1 + j, me, (*chip, c), src=ins[a]) for j, chip in enumerate(chips)]
            for cp in cps:
                cp.start()
            first += cps
        for j, chip in enumerate(chips):
            for a in range(n):
                copy(a, 1 + j, (*chip, c), me).wait_recv()
                cp = copy(a, 4 + j, (*chip, c), sibling)
                cp.start()
                passed.append(cp)
        for a in range(n):
            copy(a, 0, sibling, me).wait_recv()
            for j, chip in enumerate(chips):
                copy(a, 4 + j, (*chip, 1 - c), me).wait_recv()
        for cp in first + passed:
            cp.wait_send()
        for cp in mine:
            cp.wait()

    return pl.pallas_call(
        body, name=name,
        out_shape=[jax.ShapeDtypeStruct((8,) + b.shape, b.dtype) for b in blocks],
        in_specs=_any_specs(n), out_specs=_any_specs(n),
        scratch_shapes=[pltpu.SemaphoreType.DMA((7 * n,)), pltpu.SemaphoreType.DMA((7 * n,)),
                        pltpu.SemaphoreType.DMA((n,))],
    )(*blocks)


HBM_SPEC = pl.BlockSpec(memory_space=pltpu.HBM)
SEM_SPEC = pl.BlockSpec(memory_space=pltpu.SEMAPHORE)
ANY_SPEC = pl.BlockSpec(memory_space=pl.ANY)
SPLIT_COPY = pltpu.CompilerParams(has_side_effects=pltpu.SideEffectType.DATAFLOW_SIDE_EFFECTING)


def _hbm(x):
    return pltpu.with_memory_space_constraint(x, pltpu.HBM)


def _other_chips(x, y):
    return [(1 - x, y), (x, 1 - y), (1 - x, 1 - y)]


def _dev_slot(ref, dev):
    return ref.at[4 * dev[0] + 2 * dev[1] + dev[2]]


def _cast_into_landing(name, w, layer, ax, chip_idx):
    _, r, cc = w.shape
    r2, c2 = (r // 2, cc) if ax == 0 else (r, cc // 2)
    tr = _fit_rows(r2, c2, 512 * 1024)
    nt = r2 // tr

    def body(chip_ref, w_ref, o_ref):
        o_ref[...] = w_ref[...].astype(BF16)

    if ax == 0:
        in_spec = pl.BlockSpec((None, tr, c2), lambda hf, i, chip_ref: (layer, hf * nt + i, 0))
    else:
        in_spec = pl.BlockSpec((None, tr, c2), lambda hf, i, chip_ref: (layer, i, hf))
    return pl.pallas_call(
        body, name=name,
        grid_spec=pltpu.PrefetchScalarGridSpec(
            num_scalar_prefetch=1, grid=(2, nt), in_specs=[in_spec],
            out_specs=pl.BlockSpec((None, tr, c2), lambda hf, i, chip_ref: (2 * chip_ref[0] + hf, i, 0))),
        out_shape=jax.ShapeDtypeStruct((8, r2, c2), BF16), compiler_params=_cp(("parallel", "parallel")),
    )(chip_idx, w)


def _gather_ici_copies(lands, send_sems, recv_sems):
    x, y, c = _mesh_pos()
    pairs = []
    for a in range(len(lands)):
        for j, chip in enumerate(_other_chips(x, y)):
            def copy(dev):
                return pltpu.make_async_remote_copy(
                    src_ref=_dev_slot(lands[a], dev), dst_ref=_dev_slot(lands[a], dev),
                    send_sem=send_sems.at[3 * a + j], recv_sem=recv_sems.at[3 * a + j],
                    device_id=(*chip, c), device_id_type=MESH)
            pairs.append((copy((x, y, c)), copy((*chip, c))))
    return pairs


def _exchange_ici_copies(ins, lands, send_sems, recv_sems):
    x, y, c = _mesh_pos()
    pairs = []
    for a in range(len(ins)):
        for k, chip in enumerate(_other_chips(x, y)):
            there = 2 * chip[0] + chip[1]
            def copy(dst_entry):
                return pltpu.make_async_remote_copy(
                    src_ref=ins[a].at[there], dst_ref=lands[a].at[dst_entry],
                    send_sem=send_sems.at[3 * a + k], recv_sem=recv_sems.at[3 * a + k],
                    device_id=(*chip, c), device_id_type=MESH)
            pairs.append((copy(2 * x + y), copy(there)))
    return pairs


def _gather_start(name, lands, dep):
    n = len(lands)

    def body(*refs):
        token = refs[-1]
        for send, _ in _gather_ici_copies(refs[:n], refs[n + 1], refs[n + 2]):
            send.start()
        token[...] = jnp.zeros_like(token)

    out = pl.pallas_call(
        body, name=name,
        out_shape=(pltpu.SemaphoreType.DMA((3 * n,)), pltpu.SemaphoreType.DMA((3 * n,)),
                   *[pltpu.HBM(l.shape, l.dtype) for l in lands], jax.ShapeDtypeStruct((8, 128), F32)),
        in_specs=[HBM_SPEC] * n + [ANY_SPEC],
        out_specs=(SEM_SPEC, SEM_SPEC, *[HBM_SPEC] * n, pl.BlockSpec(memory_space=pltpu.VMEM)),
        input_output_aliases={a: 2 + a for a in range(n)},
        compiler_params=SPLIT_COPY,
    )(*[_hbm(l) for l in lands], dep)
    return out[0], out[1], list(out[2:2 + n]), out[-1]


def _gather_wait(name, send_sems, recv_sems, lands, after):
    n = len(lands)

    def body(*refs):
        for send, recv in _gather_ici_copies(refs[:n], refs[n], refs[n + 1]):
            send.wait_send()
            recv.wait_recv()

    out = pl.pallas_call(
        body, name=name,
        out_shape=[pltpu.HBM(l.shape, l.dtype) for l in lands],
        in_specs=[HBM_SPEC] * n + [SEM_SPEC, SEM_SPEC, ANY_SPEC],
        out_specs=[HBM_SPEC] * n,
        input_output_aliases={a: a for a in range(n)},
        compiler_params=SPLIT_COPY,
    )(*lands, send_sems, recv_sems, after)
    return list(out)


def _share_copies(lands, send_sems, recv_sems):
    x, y, c = _mesh_pos()
    pairs = []
    for a in range(len(lands)):
        for j, chip in enumerate(_other_chips(x, y)):
            def copy(dev):
                slot = _dev_slot(lands[a], dev)
                return pltpu.make_async_remote_copy(
                    src_ref=slot, dst_ref=slot, send_sem=send_sems.at[3 * a + j], recv_sem=recv_sems.at[3 * a + j],
                    device_id=(x, y, 1 - c), device_id_type=MESH)
            pairs.append((copy((*chip, c)), copy((*chip, 1 - c))))
    return pairs


def _share_start(name, lands):
    n = len(lands)

    def body(*refs):
        token = refs[-1]
        for send, _ in _share_copies(refs[:n], refs[n], refs[n + 1]):
            send.start()
        token[...] = jnp.zeros_like(token)

    out = pl.pallas_call(
        body, name=name,
        out_shape=(pltpu.SemaphoreType.DMA((3 * n,)), pltpu.SemaphoreType.DMA((3 * n,)),
                   *[pltpu.HBM(l.shape, l.dtype) for l in lands], jax.ShapeDtypeStruct((8, 128), F32)),
        in_specs=[HBM_SPEC] * n,
        out_specs=(SEM_SPEC, SEM_SPEC, *[HBM_SPEC] * n, pl.BlockSpec(memory_space=pltpu.VMEM)),
        input_output_aliases={a: 2 + a for a in range(n)},
        compiler_params=SPLIT_COPY,
    )(*lands)
    return out[0], out[1], list(out[2:2 + n]), out[-1]


def _share_wait(name, send_sems, recv_sems, lands, after):
    n = len(lands)

    def body(*refs):
        for send, recv in _share_copies(refs[:n], refs[n], refs[n + 1]):
            send.wait_send()
            recv.wait_recv()

    out = pl.pallas_call(
        body, name=name,
        out_shape=[pltpu.HBM(l.shape, l.dtype) for l in lands],
        in_specs=[HBM_SPEC] * n + [SEM_SPEC, SEM_SPEC, ANY_SPEC],
        out_specs=[HBM_SPEC] * n,
        input_output_aliases={a: a for a in range(n)},
        compiler_params=SPLIT_COPY,
    )(*lands, send_sems, recv_sems, after)
    return list(out)


def _sibling_copies(ins, lands, send_sems, recv_sems):
    x, y, c = _mesh_pos()
    return [pltpu.make_async_remote_copy(
        src_ref=ins[a].at[j, 1 - c], dst_ref=lands[a].at[j],
        send_sem=send_sems.at[4 * a + j], recv_sem=recv_sems.at[4 * a + j],
        device_id=(x, y, 1 - c), device_id_type=MESH) for a in range(len(ins)) for j in range(4)]


def _sibling_start(name, grads):
    n = len(grads)

    def body(*refs):
        token = refs[-1]
        for cp in _sibling_copies(refs[:n], refs[n:2 * n], refs[2 * n], refs[2 * n + 1]):
            cp.start()
        token[...] = jnp.zeros_like(token)

    lands = [_hbm(lax.empty((4,) + g.shape[2:], g.dtype)) for g in grads]
    out = pl.pallas_call(
        body, name=name,
        out_shape=(pltpu.SemaphoreType.DMA((4 * n,)), pltpu.SemaphoreType.DMA((4 * n,)),
                   *[pltpu.HBM(g.shape, g.dtype) for g in grads], *[pltpu.HBM(l.shape, l.dtype) for l in lands],
                   jax.ShapeDtypeStruct((8, 128), F32)),
        in_specs=[HBM_SPEC] * (2 * n),
        out_specs=(SEM_SPEC, SEM_SPEC, *[HBM_SPEC] * (2 * n), pl.BlockSpec(memory_space=pltpu.VMEM)),
        input_output_aliases={i: 2 + i for i in range(2 * n)},
        compiler_params=SPLIT_COPY,
    )(*[_hbm(g) for g in grads], *lands)
    return out[0], out[1], list(out[2:2 + n]), list(out[2 + n:2 + 2 * n]), out[-1]


def _sibling_wait(name, send_sems, recv_sems, grads, lands, after):
    n = len(grads)

    def body(*refs):
        for cp in _sibling_copies(refs[:n], refs[n:2 * n], refs[2 * n], refs[2 * n + 1]):
            cp.wait_send()
            cp.wait_recv()

    out = pl.pallas_call(
        body, name=name,
        out_shape=[pltpu.HBM(g.shape, g.dtype) for g in grads] + [pltpu.HBM(l.shape, l.dtype) for l in lands],
        in_specs=[HBM_SPEC] * (2 * n) + [SEM_SPEC, SEM_SPEC, ANY_SPEC],
        out_specs=[HBM_SPEC] * (2 * n),
        input_output_aliases={i: i for i in range(2 * n)},
        compiler_params=SPLIT_COPY,
    )(*grads, *lands, send_sems, recv_sems, after)
    return list(out[:n]), list(out[n:])


def _chip_exchange_start(name, parts):
    n = len(parts)

    def body(*refs):
        ins, lands = refs[:n], refs[n:2 * n]
        token = refs[-1]
        for send, _ in _exchange_ici_copies(ins, lands, refs[2 * n], refs[2 * n + 1]):
            send.start()
        token[...] = jnp.zeros_like(token)

    lands = [_hbm(lax.empty(p.shape, p.dtype)) for p in parts]
    out = pl.pallas_call(
        body, name=name,
        out_shape=(pltpu.SemaphoreType.DMA((3 * n,)), pltpu.SemaphoreType.DMA((3 * n,)),
                   *[pltpu.HBM(p.shape, p.dtype) for p in parts], *[pltpu.HBM(p.shape, p.dtype) for p in parts],
                   jax.ShapeDtypeStruct((8, 128), F32)),
        in_specs=[HBM_SPEC] * (2 * n),
        out_specs=(SEM_SPEC, SEM_SPEC, *[HBM_SPEC] * (2 * n), pl.BlockSpec(memory_space=pltpu.VMEM)),
        input_output_aliases={i: 2 + i for i in range(2 * n)},
        compiler_params=SPLIT_COPY,
    )(*[_hbm(p) for p in parts], *lands)
    return out[0], out[1], list(out[2:2 + n]), list(out[2 + n:2 + 2 * n]), out[-1]


def _chip_exchange_wait(name, send_sems, recv_sems, parts, lands, after):
    n = len(parts)

    def body(*refs):
        ins, lands_in = refs[:n], refs[n:2 * n]
        for send, recv in _exchange_ici_copies(ins, lands_in, refs[2 * n], refs[2 * n + 1]):
            send.wait_send()
            recv.wait_recv()

    out = pl.pallas_call(
        body, name=name,
        out_shape=[pltpu.HBM(p.shape, p.dtype) for p in parts] * 2,
        in_specs=[HBM_SPEC] * (2 * n) + [SEM_SPEC, SEM_SPEC, ANY_SPEC],
        out_specs=[HBM_SPEC] * (2 * n),
        input_output_aliases={i: i for i in range(2 * n)},
        compiler_params=SPLIT_COPY,
    )(*parts, *lands, send_sems, recv_sems, after)
    return list(out[:n]), list(out[n:])


def _join_copies(bufs, send_sems, recv_sems, layer):
    x, y, c = _mesh_pos()

    def copy(a, half):
        return pltpu.make_async_remote_copy(
            src_ref=bufs[a].at[layer, half], dst_ref=bufs[a].at[layer, half],
            send_sem=send_sems.at[a], recv_sem=recv_sems.at[a], device_id=(x, y, 1 - c), device_id_type=MESH)

    return [(copy(a, c), copy(a, 1 - c)) for a in range(len(bufs))]


def _sibling_join_halves(name, bufs, layer, after=()):
    n = len(bufs)

    def body(*refs):
        outs = refs[n + len(after):2 * n + len(after)]
        pairs = _join_copies(outs, refs[-2], refs[-1], layer)
        for send, _ in pairs:
            send.start()
        for send, recv in pairs:
            send.wait_send()
            recv.wait_recv()

    return pl.pallas_call(
        body, name=name, out_shape=[jax.ShapeDtypeStruct(b.shape, b.dtype) for b in bufs],
        in_specs=_any_specs(n + len(after)), out_specs=_any_specs(n),
        input_output_aliases={a: a for a in range(n)},
        scratch_shapes=[pltpu.SemaphoreType.DMA((n,)), pltpu.SemaphoreType.DMA((n,))],
    )(*bufs, *after)


def _join_start(name, bufs, layer):
    n = len(bufs)

    def body(*refs):
        token = refs[-1]
        for send, _ in _join_copies(refs[:n], refs[n], refs[n + 1], layer):
            send.start()
        token[...] = jnp.zeros_like(token)

    out = pl.pallas_call(
        body, name=name,
        out_shape=(pltpu.SemaphoreType.DMA((n,)), pltpu.SemaphoreType.DMA((n,)),
                   *[pltpu.HBM(b.shape, b.dtype) for b in bufs], jax.ShapeDtypeStruct((8, 128), F32)),
        in_specs=[HBM_SPEC] * n,
        out_specs=(SEM_SPEC, SEM_SPEC, *[HBM_SPEC] * n, pl.BlockSpec(memory_space=pltpu.VMEM)),
        input_output_aliases={a: 2 + a for a in range(n)},
        compiler_params=SPLIT_COPY,
    )(*[_hbm(b) for b in bufs])
    return out[0], out[1], list(out[2:2 + n]), out[-1]


def _join_wait(name, send_sems, recv_sems, bufs, layer, after):
    n = len(bufs)

    def body(*refs):
        for send, recv in _join_copies(refs[:n], refs[n], refs[n + 1], layer):
            send.wait_send()
            recv.wait_recv()

    out = pl.pallas_call(
        body, name=name,
        out_shape=[pltpu.HBM(b.shape, b.dtype) for b in bufs],
        in_specs=[HBM_SPEC] * n + [SEM_SPEC, SEM_SPEC, ANY_SPEC],
        out_specs=[HBM_SPEC] * n,
        input_output_aliases={a: a for a in range(n)},
        compiler_params=SPLIT_COPY,
    )(*bufs, send_sems, recv_sems, after)
    return list(out)


def _row_tile(rows, want):
    t = 1
    while t * 2 <= min(rows, want):
        t *= 2
    while rows % t:
        t //= 2
    return t


def _fit_rows(rows, cols, max_elems):
    for k in range(1, rows + 1):
        if rows % k == 0 and (rows // k) % 16 == 0 and (rows // k) * cols <= max_elems:
            return rows // k
    return _row_tile(rows, max(8, max_elems // cols))


def _add_halves(name, grads, recvd, c_idx):
    outs = []
    for a, (g, r) in enumerate(zip(grads, recvd)):
        _, _, r2, cc = g.shape
        tr = _fit_rows(r2, cc, 1536 * 1024)

        def body(c_ref, g_ref, r_ref, o_ref):
            o_ref[...] = (g_ref[...].astype(F32) + r_ref[...].astype(F32)).astype(BF16)

        outs.append(pl.pallas_call(
            body, name=f"{name}_{a}",
            grid_spec=pltpu.PrefetchScalarGridSpec(
                num_scalar_prefetch=1, grid=(4, r2 // tr),
                in_specs=[pl.BlockSpec((None, None, tr, cc), lambda j, i, c_ref: (j, c_ref[0], i, 0)),
                          pl.BlockSpec((None, tr, cc), lambda j, i, c_ref: (j, i, 0))],
                out_specs=pl.BlockSpec((None, tr, cc), lambda j, i, c_ref: (j, i, 0))),
            out_shape=jax.ShapeDtypeStruct(r.shape, BF16), compiler_params=_cp(("parallel", "parallel")),
        )(c_idx, g, r))
    return outs


def _sum_parts(name, parts, recvd, bufs, layer, where):
    outs = []
    for a, (p, r, buf) in enumerate(zip(parts, recvd, bufs)):
        _, r2, cc = p.shape
        tr = _fit_rows(r2, cc, 768 * 1024)

        def body(where_ref, own_ref, r1_ref, r2_ref, r3_ref, buf_ref, o_ref):
            acc = own_ref[...].astype(F32)
            for ref in (r1_ref, r2_ref, r3_ref):
                acc = acc + ref[...].astype(F32)
            o_ref[...] = acc

        def entry(k):
            return pl.BlockSpec((None, tr, cc), lambda i, where_ref: (where_ref[k], i, 0))

        outs.append(pl.pallas_call(
            body, name=f"{name}_{a}",
            grid_spec=pltpu.PrefetchScalarGridSpec(
                num_scalar_prefetch=1, grid=(r2 // tr,),
                in_specs=[entry(0), entry(1), entry(2), entry(3), ANY_SPEC],
                out_specs=pl.BlockSpec((None, None, tr, cc), lambda i, where_ref: (layer, where_ref[4], i, 0))),
            out_shape=jax.ShapeDtypeStruct(buf.shape, F32), input_output_aliases={5: 0},
            compiler_params=_cp(("parallel",)),
        )(where, p, r, r, r, buf))
    return outs


def _sum8(name, gathered):
    _, r, cc = gathered.shape
    tr = _row_tile(r, 512)

    def body(g_ref, o_ref):
        acc = g_ref[0].astype(F32)
        for i in range(1, 8):
            acc = acc + g_ref[i].astype(F32)
        o_ref[...] = acc

    return pl.pallas_call(
        body, name=name, grid=(r // tr,),
        in_specs=[pl.BlockSpec((8, tr, cc), lambda i: (0, i, 0))],
        out_specs=pl.BlockSpec((tr, cc), lambda i: (i, 0)),
        out_shape=jax.ShapeDtypeStruct((r, cc), F32), compiler_params=_cp(("parallel",)),
    )(gathered)


def _adamw(name, w, g, m, v):
    r, cc = w.shape
    tr = _row_tile(r, max(8, (512 * 1024) // cc))

    def body(w_ref, g_ref, m_ref, v_ref, d_ref, mo_ref, vo_ref):
        gg = g_ref[...]
        mn = ADAM_B1 * m_ref[...] + (1.0 - ADAM_B1) * gg
        vn = ADAM_B2 * v_ref[...] + (1.0 - ADAM_B2) * (gg * gg)
        m_hat = mn / (1.0 - ADAM_B1 ** ADAM_STEP)
        v_hat = vn / (1.0 - ADAM_B2 ** ADAM_STEP)
        d_ref[...] = -ADAM_LR * (m_hat / (jnp.sqrt(v_hat) + ADAM_EPS) + ADAM_WD * w_ref[...])
        mo_ref[...] = mn
        vo_ref[...] = vn

    spec = pl.BlockSpec((tr, cc), lambda i: (i, 0))
    return pl.pallas_call(
        body, name=name, grid=(r // tr,), in_specs=[spec] * 4, out_specs=[spec] * 3,
        out_shape=[jax.ShapeDtypeStruct(w.shape, F32)] * 3, compiler_params=_cp(("parallel",)),
    )(w, g, m, v)


def _rms_fwd(name, h, g, after=()):
    s, d = h.shape
    tr = _row_tile(s, 512)

    def body(h_ref, g_ref, *rest):
        y_ref = rest[-1]
        xv = h_ref[...]
        r = lax.rsqrt(jnp.mean(xv * xv, axis=-1, keepdims=True) + EPS)
        y_ref[...] = (xv * r * g_ref[...]).astype(BF16)

    return pl.pallas_call(
        body, name=name, grid=(s // tr,),
        in_specs=[pl.BlockSpec((tr, d), lambda i: (i, 0)), pl.BlockSpec((1, d), lambda i: (0, 0))]
        + [ANY_SPEC] * len(after),
        out_specs=pl.BlockSpec((tr, d), lambda i: (i, 0)),
        out_shape=jax.ShapeDtypeStruct((s, d), BF16), compiler_params=_cp(("parallel",)),
    )(h, g, *after)


def _rms_bwd_rows(xv, gv, dy):
    d = xv.shape[-1]
    r = lax.rsqrt(jnp.mean(xv * xv, axis=-1, keepdims=True) + EPS)
    dxn = dy * gv
    proj = jnp.sum(dxn * xv, axis=-1, keepdims=True) * (1.0 / d)
    dx = r * dxn - xv * (r * r * r) * proj
    return dx, dy * xv * r


def _rms_bwd(name, h, g, dy, dres, ple=None):
    s, d = h.shape
    n_ple = 0 if ple is None else 2
    tr = _row_tile(s, 256 if n_ple else 512)
    band = 16

    def body(h_ref, g_ref, dy_ref, dres_ref, *rest):
        ple_refs = rest[:n_ple]
        dh_ref, dhb_ref, dg_ref = rest[n_ple:n_ple + 3]
        ple_outs = rest[n_ple + 3:2 * n_ple + 3]
        acc = rest[-1]
        acc[...] = jnp.zeros_like(acc)

        def step(i, carry):
            rows = pl.ds(pl.multiple_of(i * band, band), band)
            dx, dgp = _rms_bwd_rows(h_ref[rows, :], g_ref[...], dy_ref[rows, :].astype(F32))
            dh = dres_ref[rows, :] + dx
            dh_ref[rows, :] = dh
            dhb_ref[rows, :] = dh.astype(BF16)
            acc[...] += _fold8(dgp)
            if n_ple:
                sg = _sigmoid(ple_refs[0][rows, :].astype(F32))
                ple_outs[0][rows, :] = (dh * sg).astype(BF16)
                ple_outs[1][rows, :] = (dh * ple_refs[1][rows, :].astype(F32) * sg * (1.0 - sg)).astype(BF16)
            return carry

        lax.fori_loop(0, tr // band, step, 0, unroll=4)

        @pl.when(pl.program_id(0) == 0)
        def _():
            dg_ref[...] = jnp.zeros_like(dg_ref)

        dg_ref[...] += jnp.sum(acc[...], axis=0, keepdims=True)

    row = pl.BlockSpec((tr, d), lambda i: (i, 0))
    vec = pl.BlockSpec((1, d), lambda i: (0, 0))
    return pl.pallas_call(
        body, name=name, grid=(s // tr,), in_specs=[row, vec, row, row] + [row] * n_ple,
        out_specs=[row, row, vec] + [row] * n_ple,
        out_shape=[jax.ShapeDtypeStruct((s, d), F32), jax.ShapeDtypeStruct((s, d), BF16),
                   jax.ShapeDtypeStruct((1, d), F32)] + [jax.ShapeDtypeStruct((s, d), BF16)] * n_ple,
        scratch_shapes=[pltpu.VMEM((8, d), F32)],
        compiler_params=_cp(("arbitrary",)),
    )(h, g, dy, dres, *(ple or ()))


def _loss_head(name, h, g, target):
    s, d = h.shape
    tr = _row_tile(s, 256)

    def body(h_ref, g_ref, t_ref, loss_ref, dh_ref, dhb_ref, dg_ref):
        xv = h_ref[...]
        gv = g_ref[...]
        r = lax.rsqrt(jnp.mean(xv * xv, axis=-1, keepdims=True) + EPS)
        err = xv * r * gv - t_ref[...]
        part = 0.5 * jnp.sum(jnp.sum(err * err, axis=-1, keepdims=True) * (1.0 / d), axis=0, keepdims=True)
        dx, dgp = _rms_bwd_rows(xv, gv, err * (1.0 / d))
        dh_ref[...] = dx
        dhb_ref[...] = dx.astype(BF16)

        @pl.when(pl.program_id(0) == 0)
        def _():
            dg_ref[...] = jnp.zeros_like(dg_ref)
            loss_ref[...] = jnp.zeros_like(loss_ref)

        dg_ref[...] += jnp.sum(dgp, axis=0, keepdims=True)
        loss_ref[...] += part

    row = pl.BlockSpec((tr, d), lambda i: (i, 0))
    vec = pl.BlockSpec((1, d), lambda i: (0, 0))
    one = pl.BlockSpec((1, 1), lambda i: (0, 0))
    return pl.pallas_call(
        body, name=name, grid=(s // tr,), in_specs=[row, vec, row], out_specs=[one, row, row, vec],
        out_shape=[jax.ShapeDtypeStruct((1, 1), F32), jax.ShapeDtypeStruct((s, d), F32),
                   jax.ShapeDtypeStruct((s, d), BF16), jax.ShapeDtypeStruct((1, d), F32)],
        compiler_params=_cp(("arbitrary",)),
    )(h, g, target)


NN = (((1,), (0,)), ((), ()))
NT = (((1,), (1,)), ((), ()))
TN = (((0,), (0,)), ((), ()))


def _matmul(name, grid, operands, in_specs, pairs, n_acc, acc_shape, epilogue, out_specs, out_shape, after=()):
    operands = list(operands) + list(after)
    in_specs = list(in_specs) + [ANY_SPEC] * len(after)
    n_in = len(operands)
    n_out = len(out_shape)
    nk = grid[2]

    def body(*refs):
        ins, outs, accs = refs[:n_in], refs[n_in:n_in + n_out], refs[n_in + n_out:]
        k = pl.program_id(2)

        @pl.when(k == 0)
        def _():
            for acc in accs:
                acc[...] = jnp.zeros_like(acc)

        for li, ri, ai, dims in pairs:
            accs[ai][...] += lax.dot_general(ins[li][...], ins[ri][...], dims, preferred_element_type=F32)

        @pl.when(k == nk - 1)
        def _():
            epilogue([acc[...] for acc in accs], ins, outs)

    return pl.pallas_call(
        body, name=name, grid=grid, in_specs=in_specs, out_specs=out_specs, out_shape=out_shape,
        scratch_shapes=[pltpu.VMEM(acc_shape, F32)] * n_acc,
        compiler_params=_cp(("parallel", "parallel", "arbitrary")),
    )(*operands)


def _ep_store(dtype):
    def ep(accs, ins, outs):
        outs[0][...] = accs[0].astype(dtype)
    return ep


def _w_spec_nt(wg, kind, to, tc):
    _, a, b = wg.shape
    if kind == "col":
        oph, cps = a // to, b // tc
        return pl.BlockSpec((None, to, tc), lambda i, jo, kc: (2 * (kc // cps) + jo // oph, jo % oph, kc % cps))
    ops, cph = a // to, b // tc
    return pl.BlockSpec((None, to, tc), lambda i, jo, kc: (2 * (jo // ops) + kc // cph, jo % ops, kc % cph))


SUB_COLS = 256


def _matmul_wide(name, grid, operands, in_specs, groups, tn, epilogue, out_specs, out_shape, after=()):
    operands = list(operands) + list(after)
    in_specs = list(in_specs) + [ANY_SPEC] * len(after)
    n_in = len(operands)

    def body(*refs):
        ins, outs = refs[:n_in], refs[n_in:]
        for s0 in range(0, tn, SUB_COLS):
            cols = slice(s0, min(s0 + SUB_COLS, tn))
            accs = []
            for group in groups:
                acc = None
                for li, (c0, cw), ri, dims in group:
                    rhs = ins[ri][:, cols] if dims == NN else ins[ri][cols, :]
                    part = lax.dot_general(ins[li][:, c0:c0 + cw], rhs, dims, preferred_element_type=F32)
                    acc = part if acc is None else acc + part
                accs.append(acc)
            epilogue(accs, ins, outs, cols)

    return pl.pallas_call(
        body, name=name, grid=grid, in_specs=in_specs, out_specs=out_specs, out_shape=out_shape,
        compiler_params=_cp(("parallel", "parallel")),
    )(*operands)


def _wide_store(dtype):
    def ep(accs, ins, outs, cols):
        outs[0][:, cols] = accs[0].astype(dtype)
    return ep


def _wide_nn_weights(wg, kind, tn, first):
    _, a, b = wg.shape
    per = b // tn
    if kind == "col":
        specs = [pl.BlockSpec((None, a, tn), lambda i, j, hf=hf: (2 * (j // per) + hf, 0, j % per)) for hf in range(2)]
    else:
        specs = [pl.BlockSpec((None, a, tn), lambda i, j, ch=ch: (2 * ch + j // per, 0, j % per)) for ch in range(4)]
    dots = [(0, (q * a, a), first + q, NN) for q in range(len(specs))]
    return [wg] * len(specs), specs, dots


def _wide_nt_weights(wg, kind, to, first):
    _, a, b = wg.shape
    per = a // to
    if kind == "col":
        specs = [pl.BlockSpec((None, to, b), lambda i, j, ch=ch: (2 * ch + j // per, j % per, 0)) for ch in range(4)]
    else:
        specs = [pl.BlockSpec((None, to, b), lambda i, j, hf=hf: (2 * (j // per) + hf, j % per, 0)) for hf in range(2)]
    dots = [(0, (q * b, b), first + q, NT) for q in range(len(specs))]
    return [wg] * len(specs), specs, dots


def _mm_nn_wide(name, x, wg, kind, tm, tn, epilogue=None, extra=(), extra_specs=(), out_specs=None, out_shape=None,
                after=()):
    m, kdim = x.shape
    _, a, b = wg.shape
    n = 4 * b if kind == "col" else 2 * b
    tm = min(tm, m)
    ws, wspecs, dots = _wide_nn_weights(wg, kind, tn, 1)
    if out_shape is None:
        out_shape = [jax.ShapeDtypeStruct((m, n), F32)]
        out_specs = [pl.BlockSpec((tm, tn), lambda i, j: (i, j))]
        epilogue = _wide_store(F32)
    return _matmul_wide(name, (m // tm, n // tn), [x] + ws + list(extra),
                        [pl.BlockSpec((tm, kdim), lambda i, j: (i, 0))] + wspecs + list(extra_specs),
                        [dots], tn, epilogue, out_specs, out_shape, after=after)


def _mm_nt_wide(name, dy, wg, kind, tm, to, out_dtype=BF16, after=()):
    m, n = dy.shape
    _, a, b = wg.shape
    kdim = 2 * a if kind == "col" else 4 * a
    tm = min(tm, m)
    ws, wspecs, dots = _wide_nt_weights(wg, kind, to, 1)
    return _matmul_wide(name, (m // tm, kdim // to), [dy] + ws,
                        [pl.BlockSpec((tm, n), lambda i, j: (i, 0))] + wspecs, [dots], to, _wide_store(out_dtype),
                        [pl.BlockSpec((tm, to), lambda i, j: (i, j))],
                        [jax.ShapeDtypeStruct((m, kdim), out_dtype)], after=after)[0]


def _mm_tn_wide(name, a, dy, kind, tr, tn):
    m, kdim = a.shape
    _, n = dy.shape

    def body(a_ref, dy_ref, o_ref):
        for s0 in range(0, tn, 2 * SUB_COLS):
            cols = slice(s0, min(s0 + 2 * SUB_COLS, tn))
            o_ref[:, cols] = lax.dot_general(a_ref[...], dy_ref[:, cols], TN, preferred_element_type=F32).astype(BF16)

    if kind == "col":
        ns = n // 4
        per = ns // tn
        out_shape = jax.ShapeDtypeStruct((4, kdim, ns), BF16)
        out_spec = pl.BlockSpec((None, tr, tn), lambda r, j: (j // per, r, j % per))
    else:
        rs = kdim // 4
        per = rs // tr
        out_shape = jax.ShapeDtypeStruct((4, rs, n), BF16)
        out_spec = pl.BlockSpec((None, tr, tn), lambda r, j: (r // per, r % per, j))
    return pl.pallas_call(
        body, name=name, grid=(kdim // tr, n // tn),
        in_specs=[pl.BlockSpec((m, tr), lambda r, j: (0, r)), pl.BlockSpec((m, tn), lambda r, j: (0, j))],
        out_specs=out_spec, out_shape=out_shape, compiler_params=_cp(("parallel", "parallel")),
    )(a, dy)


def _zero_halo(pad_ref, s):
    z = jnp.zeros((HALO, pad_ref.shape[1]), F32)
    pad_ref[pl.ds(0, HALO), :] = z
    pad_ref[pl.ds(HALO + s, HALO), :] = z


def _window(pad_ref, r0, rows):
    return pad_ref[pl.ds(r0, rows + 2 * HALO), :]


def _delayed(win, k, rows):
    if k == 0:
        return win[HALO:HALO + rows]
    return pltpu.roll(win, k, axis=0)[HALO:HALO + rows]


def _advanced(win, k, rows):
    if k == 0:
        return win[HALO:HALO + rows]
    return pltpu.roll(win, win.shape[0] - k, axis=0)[HALO:HALO + rows]


def _fold8(x):
    return jnp.sum(x.reshape(x.shape[0] // 8, 8, x.shape[1]), axis=0)


def _chunks(s):
    rows = min(CHUNK_ROWS, s)
    return rows, s // rows


def _col_spec(s, first_block):
    return pl.BlockSpec((s, HEAD), lambda j: (0, first_block + j))


def _sgu_fwd(name, z, ln_g, ln_b, w, b):
    s = z.shape[0]
    tr = _row_tile(s, 1024)
    nh = GROUP // HEAD

    def body(u_ref, v_ref, lg_ref, lb_ref, w_ref, b_ref, o_ref):
        row = lax.broadcasted_iota(jnp.int32, (HEAD, HEAD), 0)
        col = lax.broadcasted_iota(jnp.int32, (HEAD, HEAD), 1)
        wm = jnp.where(row >= col, w_ref[...], 0.0).astype(BF16)
        for ck in range(tr // HEAD):
            rs = pl.ds(ck * HEAD, HEAD)
            u = _gelu(u_ref[rs, :])
            v = _gelu(v_ref[rs, :])
            mu = jnp.mean(v, axis=-1, keepdims=True)
            vc = v - mu
            var = jnp.mean(vc * vc, axis=-1, keepdims=True)
            vln = vc * lax.rsqrt(var + EPS) * lg_ref[...] + lb_ref[...]
            sp = jnp.dot(wm, vln.astype(BF16), preferred_element_type=F32) + b_ref[...]
            o_ref[rs, :] = (u * sp).astype(BF16)

    head_vec = pl.BlockSpec((None, 1, HEAD), lambda h, i: (h, 0, 0))
    return pl.pallas_call(
        body, name=name, grid=(nh, s // tr),
        in_specs=[pl.BlockSpec((tr, HEAD), lambda h, i: (i, h)), pl.BlockSpec((tr, HEAD), lambda h, i: (i, nh + h)),
                  head_vec, head_vec, pl.BlockSpec((None, HEAD, HEAD), lambda h, i: (h, 0, 0)),
                  pl.BlockSpec((None, HEAD, 1), lambda h, i: (h, 0, 0))],
        out_specs=pl.BlockSpec((tr, HEAD), lambda h, i: (i, h)),
        out_shape=jax.ShapeDtypeStruct((s, GROUP), BF16), compiler_params=_cp(("parallel", "parallel")),
    )(z, z, ln_g, ln_b, w, b)


def _sgu_bwd(name, z, d_o, ln_g, ln_b, w, b):
    s = z.shape[0]
    tr = _row_tile(s, 1024)
    nh = GROUP // HEAD

    def body(u_ref, v_ref, do_ref, lg_ref, lb_ref, w_ref, b_ref, du_ref, dv_ref, dlg_ref, dlb_ref, dw_ref, db_ref,
             dsp_acc):
        row = lax.broadcasted_iota(jnp.int32, (HEAD, HEAD), 0)
        col = lax.broadcasted_iota(jnp.int32, (HEAD, HEAD), 1)
        tril = row >= col
        wm = jnp.where(tril, w_ref[...], 0.0).astype(BF16)
        i = pl.program_id(1)

        @pl.when(i == 0)
        def _():
            dlg_ref[...] = jnp.zeros_like(dlg_ref)
            dlb_ref[...] = jnp.zeros_like(dlb_ref)
            dw_ref[...] = jnp.zeros_like(dw_ref)
            dsp_acc[...] = jnp.zeros_like(dsp_acc)

        dlg = jnp.zeros((1, HEAD), F32)
        dlb = jnp.zeros((1, HEAD), F32)
        dw = jnp.zeros((HEAD, HEAD), F32)
        dsp_sum = jnp.zeros((HEAD, HEAD), F32)
        for ck in range(tr // HEAD):
            rs = pl.ds(ck * HEAD, HEAD)
            zu = u_ref[rs, :]
            zv = v_ref[rs, :]
            u = _gelu(zu)
            v = _gelu(zv)
            mu = jnp.mean(v, axis=-1, keepdims=True)
            vc = v - mu
            var = jnp.mean(vc * vc, axis=-1, keepdims=True)
            rstd = lax.rsqrt(var + EPS)
            xh = vc * rstd
            vln = (xh * lg_ref[...] + lb_ref[...]).astype(BF16)
            sp = jnp.dot(wm, vln, preferred_element_type=F32) + b_ref[...]
            d_oa = do_ref[rs, :].astype(F32)
            du = d_oa * sp
            dsp = d_oa * u
            dsp_b = dsp.astype(BF16)
            dvln = lax.dot_general(wm, dsp_b, TN, preferred_element_type=F32)
            dw = dw + lax.dot_general(dsp_b, vln, NT, preferred_element_type=F32)
            dsp_sum = dsp_sum + dsp
            dlg = dlg + jnp.sum(dvln * xh, axis=0, keepdims=True)
            dlb = dlb + jnp.sum(dvln, axis=0, keepdims=True)
            dxh = dvln * lg_ref[...]
            dv = rstd * (dxh - jnp.mean(dxh, axis=-1, keepdims=True)
                         - xh * jnp.mean(dxh * xh, axis=-1, keepdims=True))
            du_ref[rs, :] = (du * _gelu_grad(zu)).astype(BF16)
            dv_ref[rs, :] = (dv * _gelu_grad(zv)).astype(BF16)
        dlg_ref[...] += dlg
        dlb_ref[...] += dlb
        dw_ref[...] += jnp.where(tril, dw, 0.0)
        dsp_acc[...] += dsp_sum

        @pl.when(i == pl.num_programs(1) - 1)
        def _():
            db_ref[...] = jnp.sum(dsp_acc[...], axis=1, keepdims=True)

    head_vec = pl.BlockSpec((None, 1, HEAD), lambda h, i: (h, 0, 0))
    head_mat = pl.BlockSpec((None, HEAD, HEAD), lambda h, i: (h, 0, 0))
    head_col = pl.BlockSpec((None, HEAD, 1), lambda h, i: (h, 0, 0))
    return pl.pallas_call(
        body, name=name, grid=(nh, s // tr),
        in_specs=[pl.BlockSpec((tr, HEAD), lambda h, i: (i, h)), pl.BlockSpec((tr, HEAD), lambda h, i: (i, nh + h)),
                  pl.BlockSpec((tr, HEAD), lambda h, i: (i, h)), head_vec, head_vec, head_mat, head_col],
        out_specs=[pl.BlockSpec((tr, HEAD), lambda h, i: (i, h)), pl.BlockSpec((tr, HEAD), lambda h, i: (i, h)),
                   head_vec, head_vec, head_mat, head_col],
        out_shape=[jax.ShapeDtypeStruct((s, GROUP), BF16), jax.ShapeDtypeStruct((s, GROUP), BF16),
                   jax.ShapeDtypeStruct((nh, 1, HEAD), F32), jax.ShapeDtypeStruct((nh, 1, HEAD), F32),
                   jax.ShapeDtypeStruct((nh, HEAD, HEAD), F32), jax.ShapeDtypeStruct((nh, HEAD, 1), F32)],
        scratch_shapes=[pltpu.VMEM((HEAD, HEAD), F32)],
        compiler_params=_cp(("parallel", "arbitrary")),
    )(z, z, d_o, ln_g, ln_b, w, b)


def _shortconv_fwd(name, z, w):
    s = z.shape[0]
    kw = w.shape[0]
    rows, nchunk = _chunks(s)
    nb = GROUP // HEAD

    def body(h_ref, bg_ref, cg_ref, w_ref, o_ref, pad):
        _zero_halo(pad, s)

        def fill(ci, carry):
            r0 = pl.multiple_of(ci * rows, rows)
            pad[pl.ds(pl.multiple_of(HALO + r0, 8), rows), :] = cg_ref[pl.ds(r0, rows), :] * h_ref[pl.ds(r0, rows), :]
            return carry

        lax.fori_loop(0, nchunk, fill, 0)

        def step(ci, carry):
            r0 = pl.multiple_of(ci * rows, rows)
            win = _window(pad, r0, rows)
            cv = jnp.zeros((rows, HEAD), F32)
            for k in range(kw):
                cv = cv + w_ref[k:k + 1, :] * _delayed(win, kw - 1 - k, rows)
            o_ref[pl.ds(r0, rows), :] = (bg_ref[pl.ds(r0, rows), :] * cv).astype(BF16)
            return carry

        lax.fori_loop(0, nchunk, step, 0, unroll=min(4, nchunk))

    return pl.pallas_call(
        body, name=name, grid=(nb,),
        in_specs=[_col_spec(s, 8), _col_spec(s, 12), _col_spec(s, 16), pl.BlockSpec((kw, HEAD), lambda j: (0, j))],
        out_specs=_col_spec(s, 0),
        out_shape=jax.ShapeDtypeStruct((s, GROUP), BF16),
        scratch_shapes=[pltpu.VMEM((s + 2 * HALO, HEAD), F32)],
        compiler_params=_cp(("parallel",)),
    )(z, z, z, w)


def _shortconv_bwd(name, z, d_o, w):
    s = z.shape[0]
    kw = w.shape[0]
    rows, nchunk = _chunks(s)
    nb = GROUP // HEAD

    def body(h_ref, bg_ref, cg_ref, do_ref, w_ref, dh_ref, dbg_ref, dcg_ref, dw_ref, pad_q, pad_d, acc):
        _zero_halo(pad_q, s)
        _zero_halo(pad_d, s)
        acc[...] = jnp.zeros_like(acc)

        def fill(ci, carry):
            r0 = pl.multiple_of(ci * rows, rows)
            rs = pl.ds(r0, rows)
            ps = pl.ds(pl.multiple_of(HALO + r0, 8), rows)
            pad_q[ps, :] = cg_ref[rs, :] * h_ref[rs, :]
            pad_d[ps, :] = do_ref[rs, :].astype(F32) * bg_ref[rs, :]
            return carry

        lax.fori_loop(0, nchunk, fill, 0)

        def step(ci, carry):
            r0 = pl.multiple_of(ci * rows, rows)
            rs = pl.ds(r0, rows)
            wq = _window(pad_q, r0, rows)
            wd = _window(pad_d, r0, rows)
            dcv = wd[HALO:HALO + rows]
            cv = jnp.zeros((rows, HEAD), F32)
            dq = jnp.zeros((rows, HEAD), F32)
            for k in range(kw):
                qk = _delayed(wq, kw - 1 - k, rows)
                cv = cv + w_ref[k:k + 1, :] * qk
                dq = dq + w_ref[k:k + 1, :] * _advanced(wd, kw - 1 - k, rows)
                acc[k] += _fold8(dcv * qk)
            dbg_ref[rs, :] = (do_ref[rs, :].astype(F32) * cv).astype(BF16)
            dcg_ref[rs, :] = (dq * h_ref[rs, :]).astype(BF16)
            dh_ref[rs, :] = (dq * cg_ref[rs, :]).astype(BF16)
            return carry

        lax.fori_loop(0, nchunk, step, 0, unroll=min(2, nchunk))
        for k in range(kw):
            dw_ref[k:k + 1, :] = jnp.sum(acc[k], axis=0, keepdims=True)

    col = _col_spec(s, 0)
    return pl.pallas_call(
        body, name=name, grid=(nb,),
        in_specs=[_col_spec(s, 8), _col_spec(s, 12), _col_spec(s, 16), col, pl.BlockSpec((kw, HEAD), lambda j: (0, j))],
        out_specs=[col, col, col, pl.BlockSpec((kw, HEAD), lambda j: (0, j))],
        out_shape=[jax.ShapeDtypeStruct((s, GROUP), BF16)] * 3 + [jax.ShapeDtypeStruct((kw, GROUP), F32)],
        scratch_shapes=[pltpu.VMEM((s + 2 * HALO, HEAD), F32), pltpu.VMEM((s + 2 * HALO, HEAD), F32),
                        pltpu.VMEM((kw, 8, HEAD), F32)],
        compiler_params=_cp(("parallel",)),
    )(z, z, z, d_o, w)


def _conformer_conv_fwd(name, z, w, bias):
    s = z.shape[0]
    kw = w.shape[0]
    rows, nchunk = _chunks(s)
    nb = GROUP // HEAD

    def body(a_ref, g_ref, w_ref, b_ref, o_ref, pad):
        _zero_halo(pad, s)

        def fill(ci, carry):
            r0 = pl.multiple_of(ci * rows, rows)
            rs = pl.ds(r0, rows)
            pad[pl.ds(pl.multiple_of(HALO + r0, 8), rows), :] = a_ref[rs, :] * _sigmoid(g_ref[rs, :])
            return carry

        lax.fori_loop(0, nchunk, fill, 0)

        def step(ci, carry):
            r0 = pl.multiple_of(ci * rows, rows)
            win = _window(pad, r0, rows)
            cc = jnp.zeros((rows, HEAD), F32)
            for k in range(kw):
                cc = cc + w_ref[k:k + 1, :] * _delayed(win, kw - 1 - k, rows)
            o_ref[pl.ds(r0, rows), :] = cc + b_ref[...]
            return carry

        lax.fori_loop(0, nchunk, step, 0)

    return pl.pallas_call(
        body, name=name, grid=(nb,),
        in_specs=[_col_spec(s, 20), _col_spec(s, 24), pl.BlockSpec((kw, HEAD), lambda j: (0, j)),
                  pl.BlockSpec((1, HEAD), lambda j: (0, j))],
        out_specs=_col_spec(s, 0),
        out_shape=jax.ShapeDtypeStruct((s, GROUP), F32),
        scratch_shapes=[pltpu.VMEM((s + 2 * HALO, HEAD), F32)],
        compiler_params=_cp(("parallel",)),
    )(z, z, w, bias)


def _ln_rows(cc, g, b):
    mu = jnp.mean(cc, axis=-1, keepdims=True)
    xc = cc - mu
    var = jnp.mean(xc * xc, axis=-1, keepdims=True)
    rstd = lax.rsqrt(var + EPS)
    xh = xc * rstd
    return xh, rstd, xh * g + b


def _conformer_ln_fwd(name, cc, g, b):
    s, d = cc.shape
    tr = _row_tile(s, 512)

    def body(c_ref, g_ref, b_ref, o_ref):
        _, _, l = _ln_rows(c_ref[...], g_ref[...], b_ref[...])
        o_ref[...] = (l * _sigmoid(l)).astype(BF16)

    row = pl.BlockSpec((tr, d), lambda i: (i, 0))
    vec = pl.BlockSpec((1, d), lambda i: (0, 0))
    return pl.pallas_call(
        body, name=name, grid=(s // tr,), in_specs=[row, vec, vec], out_specs=row,
        out_shape=jax.ShapeDtypeStruct((s, d), BF16), compiler_params=_cp(("parallel",)),
    )(cc, g, b)


def _conformer_ln_bwd(name, cc, d_o, g, b):
    s, d = cc.shape
    tr = _row_tile(s, 512)

    def body(c_ref, do_ref, g_ref, b_ref, dcc_ref, dg_ref, db_ref, dcb_ref):
        xh, rstd, l = _ln_rows(c_ref[...], g_ref[...], b_ref[...])
        sg = _sigmoid(l)
        dl = do_ref[...].astype(F32) * sg * (1.0 + l * (1.0 - sg))
        dxh = dl * g_ref[...]
        dcc = rstd * (dxh - jnp.mean(dxh, axis=-1, keepdims=True) - xh * jnp.mean(dxh * xh, axis=-1, keepdims=True))
        dcc_ref[...] = dcc

        @pl.when(pl.program_id(0) == 0)
        def _():
            dg_ref[...] = jnp.zeros_like(dg_ref)
            db_ref[...] = jnp.zeros_like(db_ref)
            dcb_ref[...] = jnp.zeros_like(dcb_ref)

        dg_ref[...] += jnp.sum(dl * xh, axis=0, keepdims=True)
        db_ref[...] += jnp.sum(dl, axis=0, keepdims=True)
        dcb_ref[...] += jnp.sum(dcc, axis=0, keepdims=True)

    row = pl.BlockSpec((tr, d), lambda i: (i, 0))
    vec = pl.BlockSpec((1, d), lambda i: (0, 0))
    return pl.pallas_call(
        body, name=name, grid=(s // tr,), in_specs=[row, row, vec, vec], out_specs=[row, vec, vec, vec],
        out_shape=[jax.ShapeDtypeStruct((s, d), F32)] + [jax.ShapeDtypeStruct((1, d), F32)] * 3,
        compiler_params=_cp(("arbitrary",)),
    )(cc, d_o, g, b)


def _conformer_conv_bwd(name, z, dcc, w):
    s = z.shape[0]
    kw = w.shape[0]
    rows, nchunk = _chunks(s)
    nb = GROUP // HEAD

    def body(a_ref, g_ref, d_ref, w_ref, da_ref, dg_ref, dw_ref, pad_h, pad_d, acc):
        _zero_halo(pad_h, s)
        _zero_halo(pad_d, s)
        acc[...] = jnp.zeros_like(acc)

        def fill(ci, carry):
            r0 = pl.multiple_of(ci * rows, rows)
            rs = pl.ds(r0, rows)
            ps = pl.ds(pl.multiple_of(HALO + r0, 8), rows)
            pad_h[ps, :] = a_ref[rs, :] * _sigmoid(g_ref[rs, :])
            pad_d[ps, :] = d_ref[rs, :]
            return carry

        lax.fori_loop(0, nchunk, fill, 0)

        def step(ci, carry):
            r0 = pl.multiple_of(ci * rows, rows)
            rs = pl.ds(r0, rows)
            wh = _window(pad_h, r0, rows)
            wd = _window(pad_d, r0, rows)
            dcc_c = wd[HALO:HALO + rows]
            dhc = jnp.zeros((rows, HEAD), F32)
            for k in range(kw):
                dhc = dhc + w_ref[k:k + 1, :] * _advanced(wd, kw - 1 - k, rows)
                acc[k] += _fold8(dcc_c * _delayed(wh, kw - 1 - k, rows))
            sg = _sigmoid(g_ref[rs, :])
            da_ref[rs, :] = (dhc * sg).astype(BF16)
            dg_ref[rs, :] = (dhc * a_ref[rs, :] * sg * (1.0 - sg)).astype(BF16)
            return carry

        lax.fori_loop(0, nchunk, step, 0)
        for k in range(kw):
            dw_ref[k:k + 1, :] = jnp.sum(acc[k], axis=0, keepdims=True)

    col = _col_spec(s, 0)
    return pl.pallas_call(
        body, name=name, grid=(nb,),
        in_specs=[_col_spec(s, 20), _col_spec(s, 24), col, pl.BlockSpec((kw, HEAD), lambda j: (0, j))],
        out_specs=[col, col, pl.BlockSpec((kw, HEAD), lambda j: (0, j))],
        out_shape=[jax.ShapeDtypeStruct((s, GROUP), BF16)] * 2 + [jax.ShapeDtypeStruct((kw, GROUP), F32)],
        scratch_shapes=[pltpu.VMEM((s + 2 * HALO, HEAD), F32), pltpu.VMEM((s + 2 * HALO, HEAD), F32),
                        pltpu.VMEM((kw, 8, HEAD), F32)],
        compiler_params=_cp(("parallel",)),
    )(z, z, dcc, w)


def _pool_window_sum(win, level, rows, shift):
    n = win.shape[0]

    def moved(v, k):
        return pltpu.roll(v, k if shift is _delayed else n - k, axis=0)

    s2 = win + moved(win, 1)
    s4 = s2 + moved(s2, 2)
    s8 = s4 + moved(s4, 4)
    s16 = s8 + moved(s8, 8)
    sel = jnp.where(level == 0, s2, jnp.where(level == 1, s4, jnp.where(level == 2, s8, s16)))
    return sel[HALO:HALO + rows]


def _pool_count(level, r0, rows):
    t = r0 + lax.broadcasted_iota(jnp.int32, (rows, 1), 0)
    width = jnp.left_shift(jnp.int32(2), level)
    return jnp.minimum(t + 1, width).astype(F32)


def _pool_fwd(name, z, pool_w, scale):
    s = z.shape[0]
    rows, nchunk = _chunks(s)

    def body(z_ref, w_ref, sc_ref, o_ref, pad):
        level = pl.program_id(0)
        _zero_halo(pad, s)

        def fill(ci, carry):
            r0 = pl.multiple_of(ci * rows, rows)
            pad[pl.ds(pl.multiple_of(HALO + r0, 8), rows), :] = z_ref[pl.ds(r0, rows), :]
            return carry

        lax.fori_loop(0, nchunk, fill, 0)
        wb = w_ref[...].astype(BF16)

        def step(ci, carry):
            r0 = pl.multiple_of(ci * rows, rows)
            win = _window(pad, r0, rows)
            pm = _pool_window_sum(win, level, rows, _delayed) / _pool_count(level, r0, rows) - win[HALO:HALO + rows]
            r = jnp.dot(pm.astype(BF16), wb, preferred_element_type=F32)
            o_ref[pl.ds(r0, rows), :] = (r * sc_ref[...]).astype(BF16)
            return carry

        lax.fori_loop(0, nchunk, step, 0, unroll=min(4, nchunk))

    return pl.pallas_call(
        body, name=name, grid=(POOL_LEVELS,),
        in_specs=[_col_spec(s, 28), pl.BlockSpec((None, HEAD, HEAD), lambda j: (j, 0, 0)),
                  pl.BlockSpec((1, HEAD), lambda j: (0, j))],
        out_specs=_col_spec(s, 0),
        out_shape=jax.ShapeDtypeStruct((s, GROUP), BF16),
        scratch_shapes=[pltpu.VMEM((s + 2 * HALO, HEAD), F32)],
        compiler_params=_cp(("parallel",)),
    )(z, pool_w, scale)


def _pool_bwd(name, z, d_o, pool_w, scale):
    s = z.shape[0]
    rows, nchunk = _chunks(s)

    def body(z_ref, do_ref, w_ref, sc_ref, dz_ref, dw_ref, dsc_ref, pad, pad_q, dw_acc, dsc_acc):
        level = pl.program_id(0)
        _zero_halo(pad, s)
        _zero_halo(pad_q, s)
        dw_acc[...] = jnp.zeros_like(dw_acc)
        dsc_acc[...] = jnp.zeros_like(dsc_acc)

        def fill(ci, carry):
            r0 = pl.multiple_of(ci * rows, rows)
            pad[pl.ds(pl.multiple_of(HALO + r0, 8), rows), :] = z_ref[pl.ds(r0, rows), :]
            return carry

        lax.fori_loop(0, nchunk, fill, 0)
        wb = w_ref[...].astype(BF16)

        def first(ci, carry):
            r0 = pl.multiple_of(ci * rows, rows)
            win = _window(pad, r0, rows)
            cnt = _pool_count(level, r0, rows)
            pm = (_pool_window_sum(win, level, rows, _delayed) / cnt - win[HALO:HALO + rows]).astype(BF16)
            r = jnp.dot(pm, wb, preferred_element_type=F32)
            d_od = do_ref[pl.ds(r0, rows), :].astype(F32)
            dsc_acc[...] += _fold8(d_od * r)
            dr = (d_od * sc_ref[...]).astype(BF16)
            dw_acc[...] += lax.dot_general(pm, dr, TN, preferred_element_type=F32)
            dpm = lax.dot_general(dr, wb, NT, preferred_element_type=F32)
            pad_q[pl.ds(pl.multiple_of(HALO + r0, 8), rows), :] = dpm / cnt
            return carry

        lax.fori_loop(0, nchunk, first, 0, unroll=min(4, nchunk))

        def second(ci, carry):
            r0 = pl.multiple_of(ci * rows, rows)
            wq = _window(pad_q, r0, rows)
            dpm = wq[HALO:HALO + rows] * _pool_count(level, r0, rows)
            dz_ref[pl.ds(r0, rows), :] = (_pool_window_sum(wq, level, rows, _advanced) - dpm).astype(BF16)
            return carry

        lax.fori_loop(0, nchunk, second, 0)
        dw_ref[...] = dw_acc[...]
        dsc_ref[...] = jnp.sum(dsc_acc[...], axis=0, keepdims=True)

    col = _col_spec(s, 0)
    mat = pl.BlockSpec((None, HEAD, HEAD), lambda j: (j, 0, 0))
    vec = pl.BlockSpec((1, HEAD), lambda j: (0, j))
    return pl.pallas_call(
        body, name=name, grid=(POOL_LEVELS,),
        in_specs=[_col_spec(s, 28), col, mat, vec], out_specs=[col, mat, vec],
        out_shape=[jax.ShapeDtypeStruct((s, GROUP), BF16), jax.ShapeDtypeStruct((POOL_LEVELS, HEAD, HEAD), F32),
                   jax.ShapeDtypeStruct((1, GROUP), F32)],
        scratch_shapes=[pltpu.VMEM((s + 2 * HALO, HEAD), F32), pltpu.VMEM((s + 2 * HALO, HEAD), F32),
                        pltpu.VMEM((HEAD, HEAD), F32), pltpu.VMEM((8, HEAD), F32)],
        compiler_params=_cp(("parallel",)),
    )(z, d_o, pool_w, scale)


def _mm_residual(name, x, wg, h, tm, tn):
    m, d = h.shape
    tm = min(tm, m)

    def ep(accs, ins, outs, cols):
        outs[0][:, cols] = ins[5][:, cols] + accs[0]

    tile = pl.BlockSpec((tm, tn), lambda i, j: (i, j))
    return _mm_nn_wide(name, x, wg, "row", tm, tn, epilogue=ep, extra=[h], extra_specs=[tile], out_specs=[tile],
                       out_shape=[jax.ShapeDtypeStruct((m, d), F32)])[0]


def _swiglu_fwd(name, y, wg_gate, wg_up, tm):
    m, kdim = y.shape
    _, a, b = wg_gate.shape
    tm = min(tm, m)

    def ep(accs, ins, outs, cols):
        gt, up = accs
        outs[0][:, cols] = (gt * _sigmoid(gt) * up).astype(BF16)
        outs[1][:, cols] = gt.astype(BF16)
        outs[2][:, cols] = up.astype(BF16)

    gws, gspecs, gdots = _wide_nn_weights(wg_gate, "col", b, 1)
    uws, uspecs, udots = _wide_nn_weights(wg_up, "col", b, 1 + len(gws))
    out = pl.BlockSpec((tm, b), lambda i, j: (i, j))
    return _matmul_wide(name, (m // tm, 4), [y] + gws + uws,
                        [pl.BlockSpec((tm, kdim), lambda i, j: (i, 0))] + gspecs + uspecs, [gdots, udots], b, ep,
                        [out] * 3, [jax.ShapeDtypeStruct((m, 4 * b), BF16)] * 3)


def _swiglu_bwd(name, dh, wg_down, gate, up, tm):
    m, n = dh.shape
    _, a, b = wg_down.shape
    tm = min(tm, m)

    def ep(accs, ins, outs, cols):
        d_act = accs[0]
        gt = ins[3][:, cols].astype(F32)
        upv = ins[4][:, cols].astype(F32)
        sg = _sigmoid(gt)
        outs[0][:, cols] = (d_act * upv * sg * (1.0 + gt * (1.0 - sg))).astype(BF16)
        outs[1][:, cols] = (d_act * gt * sg).astype(BF16)
        outs[2][:, cols] = (gt * sg * upv).astype(BF16)

    ws, wspecs, dots = _wide_nt_weights(wg_down, "row", a, 1)
    tile = pl.BlockSpec((tm, a), lambda i, j: (i, j))
    return _matmul_wide(name, (m // tm, 4), [dh] + ws + [gate, up],
                        [pl.BlockSpec((tm, n), lambda i, j: (i, 0))] + wspecs + [tile, tile], [dots], a, ep,
                        [tile] * 3, [jax.ShapeDtypeStruct((m, 4 * a), BF16)] * 3)


def _ffn_dy(name, d_gate, d_up, wg_gate, wg_up, tiles, after=()):
    m, n = d_gate.shape
    _, a, b = wg_gate.shape
    kdim = 2 * a
    tm, to, tc = tiles
    tm = min(tm, m)
    grid = (m // tm, kdim // to, n // tc)
    lhs = pl.BlockSpec((tm, tc), lambda i, j, k: (i, k))
    wspec = _w_spec_nt(wg_gate, "col", to, tc)
    return _matmul(name, grid, [d_gate, d_up, wg_gate, wg_up], [lhs, lhs, wspec, wspec],
                   [(0, 2, 0, NT), (1, 3, 0, NT)], 1, (tm, to), _ep_store(BF16),
                   [pl.BlockSpec((tm, to), lambda i, j, k: (i, j))], [jax.ShapeDtypeStruct((m, kdim), BF16)],
                   after=after)[0]


def _ple_fwd(name, y, wg, h, pp, tm, tn, after=()):
    m, d = h.shape
    tm = min(tm, m)

    def ep(accs, ins, outs, cols):
        pg = accs[0]
        outs[0][:, cols] = ins[5][:, cols] + _sigmoid(pg) * ins[6][:, cols].astype(F32)
        outs[1][:, cols] = pg.astype(BF16)

    tile = pl.BlockSpec((tm, tn), lambda i, j: (i, j))
    return _mm_nn_wide(name, y, wg, "row", tm, tn, epilogue=ep, extra=[h, pp], extra_specs=[tile, tile],
                       out_specs=[tile, tile],
                       out_shape=[jax.ShapeDtypeStruct((m, d), F32), jax.ShapeDtypeStruct((m, d), BF16)], after=after)


def _ple_bwd(name, dh, pg, pp, after=()):
    s, d = dh.shape
    tr = _row_tile(s, 512)

    def body(dh_ref, pg_ref, pp_ref, *rest):
        dpp_ref, dpg_ref = rest[-2:]
        dhv = dh_ref[...]
        sg = _sigmoid(pg_ref[...].astype(F32))
        dpp_ref[...] = (dhv * sg).astype(BF16)
        dpg_ref[...] = (dhv * pp_ref[...].astype(F32) * sg * (1.0 - sg)).astype(BF16)

    row = pl.BlockSpec((tr, d), lambda i: (i, 0))
    return pl.pallas_call(
        body, name=name, grid=(s // tr,), in_specs=[row] * 3 + [ANY_SPEC] * len(after), out_specs=[row] * 2,
        out_shape=[jax.ShapeDtypeStruct((s, d), BF16)] * 2, compiler_params=_cp(("parallel",)),
    )(dh, pg, pp, *after)


BIG = ["w_in", "w_out", "w_gate", "w_up", "w_down", "w_ple_gate", "w_ple_proj"]
KIND = {"w_in": "col", "w_out": "row", "w_gate": "col", "w_up": "col", "w_down": "row", "w_ple_gate": "row",
        "w_ple_proj": "col"}
GATHER_GROUPS = (("w_in", "w_out"), ("w_gate", "w_up"), ("w_down", "w_ple_gate", "w_ple_proj"))
RS_GROUPS = (("w_ple_gate", "w_ple_proj", "w_down", "w_gate", "w_up"), ("w_out", "w_in"))
SMALL = ["norm_mix_g", "sgu_ln_g", "sgu_ln_b", "sgu_w", "sgu_b", "sc_conv_w", "cf_conv_w", "cf_conv_b", "cf_ln_g",
         "cf_ln_b", "pool_w", "pool_scale", "norm_ffn_g", "norm_ple_g", "final_norm_g"]
CHIP_SPLIT = ["sc_conv_w", "cf_conv_w"]
MATRIX_SMALL = ["sgu_w", "pool_w"]
WEIGHTS = ['norm_mix_g', 'w_in', 'sgu_ln_g', 'sgu_ln_b', 'sgu_w', 'sgu_b', 'sc_conv_w', 'cf_conv_w', 'cf_conv_b',
           'cf_ln_g', 'cf_ln_b', 'pool_w', 'pool_scale', 'w_out', 'norm_ffn_g', 'w_gate', 'w_up', 'w_down',
           'norm_ple_g', 'w_ple_gate', 'w_ple_proj', 'final_norm_g']


def _tile(n, want):
    if n <= want:
        return n
    t = (want // 128) * 128
    while n % t:
        t -= 128
    return t


def _pack_rows(vecs):
    flat = jnp.concatenate([v.reshape(-1) for v in vecs])
    n = flat.shape[0]
    quantum = PACK_ROWS * 128
    padded = ((n + quantum - 1) // quantum) * quantum
    return jnp.pad(flat, (0, padded - n)).reshape(padded // 128, 128), n


def _unpack(flat, shapes):
    out, off = [], 0
    for shp in shapes:
        size = math.prod(shp)
        out.append(flat[off:off + size].reshape(shp))
        off += size
    return out


def kernel(x, p, norm_mix_g, w_in, sgu_ln_g, sgu_ln_b, sgu_w, sgu_b, sc_conv_w, cf_conv_w, cf_conv_b, cf_ln_g, cf_ln_b, pool_w, pool_scale, w_out, norm_ffn_g, w_gate, w_up, w_down, norm_ple_g, w_ple_gate, w_ple_proj, final_norm_g, loss_target, m_norm_mix_g, m_w_in, m_sgu_ln_g, m_sgu_ln_b, m_sgu_w, m_sgu_b, m_sc_conv_w, m_cf_conv_w, m_cf_conv_b, m_cf_ln_g, m_cf_ln_b, m_pool_w, m_pool_scale, m_w_out, m_norm_ffn_g, m_w_gate, m_w_up, m_w_down, m_norm_ple_g, m_w_ple_gate, m_w_ple_proj, m_final_norm_g, v_norm_mix_g, v_w_in, v_sgu_ln_g, v_sgu_ln_b, v_sgu_w, v_sgu_b, v_sc_conv_w, v_cf_conv_w, v_cf_conv_b, v_cf_ln_g, v_cf_ln_b, v_pool_w, v_pool_scale, v_w_out, v_norm_ffn_g, v_w_gate, v_w_up, v_w_down, v_norm_ple_g, v_w_ple_gate, v_w_ple_proj, v_final_norm_g):
    args = dict(locals())
    w = {n: args[n] for n in WEIGHTS}
    mom = {n: args["m_" + n] for n in WEIGHTS}
    var = {n: args["v_" + n] for n in WEIGHTS}
    depth = w_in.shape[0]
    s, d = x.shape[1], x.shape[2]
    f_dim = 4 * w_gate.shape[2]
    xi, yi, ci = lax.axis_index("x"), lax.axis_index("y"), lax.axis_index("c")
    c_idx = ci.astype(jnp.int32).reshape(1)

    chip_idx = (2 * xi + yi).astype(jnp.int32).reshape(1)

    def start_gathers(l, dep):
        pending = []
        for gi, names in enumerate(GATHER_GROUPS):
            lands = [_cast_into_landing(f"cast_{n}", w[n], l, 0 if KIND[n] == "col" else 1, chip_idx) for n in names]
            pending.append(_gather_start(f"gather_start_{l}_{gi}", lands, dep))
            dep = pending[-1][3]
        return pending, dep

    def receive(l, gi, pending, after):
        send_sems, recv_sems, lands, _ = pending
        lands = _gather_wait(f"gather_wait_{l}_{gi}", send_sems, recv_sems, lands, after)
        return _share_start(f"gather_share_start_{l}_{gi}", lands)

    def complete(l, gi, share, after):
        send_sems, recv_sems, lands, _ = share
        lands = _share_wait(f"gather_share_wait_{l}_{gi}", send_sems, recv_sems, lands, after)
        return dict(zip(GATHER_GROUPS[gi], lands))

    conv_pack = jnp.concatenate([sc_conv_w, cf_conv_w], axis=1)
    taps = conv_pack.shape[1]
    rows_pad = ((depth * taps + 7) // 8) * 8
    conv_rows = jnp.pad(conv_pack.reshape(depth * taps, HEAD), ((0, rows_pad - depth * taps), (0, 0)))
    conv_all = _allgather8("gather_conv_weights", [conv_rows])[0]
    conv_full = conv_all[0::2, :depth * taps].reshape(4, depth, taps, HEAD)
    conv_full = jnp.transpose(conv_full, (1, 2, 0, 3)).reshape(depth, taps, GROUP)
    sc_w_full, cf_w_full = conv_full[:, :3], conv_full[:, 3:]

    pending, token = start_gathers(0, conv_all)
    h = x[0]
    saved = []
    gathered = []
    shares = [None] * len(GATHER_GROUPS)
    for l in range(depth):
        just_in_time = l < 2
        ahead = l + 1 < depth and l + 1 >= 2
        if just_in_time:
            shares[0] = receive(l, 0, pending[0], token if l == 0 else h)
        wg = complete(l, 0, shares[0], h)
        gathered.append(wg)
        started = ()
        this_layer = pending
        if l + 1 < depth:
            pending, token = start_gathers(l + 1, wg["w_in"])
            started = (token,)
        sv = {"h0": h}
        y1 = _rms_fwd("rms_mix", h, norm_mix_g[l:l + 1], after=started)
        z = _mm_nn_wide("mm_in", y1, wg["w_in"], "col", 1024, 1024)[0]
        lg, lb = sgu_ln_g[l][:, None, :], sgu_ln_b[l][:, None, :]
        sb = sgu_b[l][:, :, None]
        oa = _sgu_fwd("sgu_fwd", z, lg, lb, sgu_w[l], sb)
        ob = _shortconv_fwd("shortconv_fwd", z, sc_w_full[l])
        cc = _conformer_conv_fwd("conformer_conv_fwd", z, cf_w_full[l], cf_conv_b[l:l + 1])
        oc = _conformer_ln_fwd("conformer_ln_fwd", cc, cf_ln_g[l:l + 1], cf_ln_b[l:l + 1])
        od = _pool_fwd("pool_fwd", z, pool_w[l], pool_scale[l:l + 1])
        o = jnp.concatenate([oa, ob, oc, od], axis=1)
        h1 = _mm_residual("mm_out", o, wg["w_out"], h, 1024, 1024)
        if just_in_time:
            shares[1] = receive(l, 1, this_layer[1], h1)
        wg.update(complete(l, 1, shares[1], h1))
        y2 = _rms_fwd("rms_ffn", h1, norm_ffn_g[l:l + 1])
        act, gt, up = _swiglu_fwd("mm_swiglu", y2, wg["w_gate"], wg["w_up"], 512)
        if just_in_time:
            shares[2] = receive(l, 2, this_layer[2], act)
        wg.update(complete(l, 2, shares[2], act))
        h2 = _mm_residual("mm_down", act, wg["w_down"], h1, 1024, 512)
        started = ()
        if ahead:
            shares[0] = receive(l + 1, 0, pending[0], h2)
            shares[1] = receive(l + 1, 1, pending[1], shares[0][3])
            started = (shares[1][3],)
        y3 = _rms_fwd("rms_ple", h2, norm_ple_g[l:l + 1], after=started)
        pb = p[l, 0].astype(BF16)
        ptile = pl.BlockSpec((min(1024, s), 512), lambda i, j: (i, j))
        pp = _mm_nn_wide("mm_ple_proj", pb, wg["w_ple_proj"], "col", 1024, 512, epilogue=_wide_store(BF16),
                         out_specs=[ptile], out_shape=[jax.ShapeDtypeStruct((s, d), BF16)])[0]
        started = ()
        if ahead:
            shares[2] = receive(l + 1, 2, pending[2], pp)
            started = (shares[2][3],)
        h3, pg = _ple_fwd("mm_ple_gate", y3, wg["w_ple_gate"], h2, pp, 1024, 1024, after=started)
        sv.update(y1=y1, z=z, cc=cc, o=o, h1=h1, y2=y2, gt=gt, up=up, h2=h2, y3=y3, pb=pb, pp=pp, pg=pg)
        saved.append(sv)
        h = h3

    loss_part, dh, dhb, d_final_g = _loss_head("loss_head", h, final_norm_g[None, :], loss_target[0])

    small_grads = [None] * depth
    where = jnp.stack([2 * xi + yi, 2 * (1 - xi) + yi, 2 * xi + (1 - yi), 2 * (1 - xi) + (1 - yi), ci]).astype(jnp.int32)
    grad_bufs = {n: lax.empty((depth, 2, w[n].shape[1] // 2, w[n].shape[2]), F32) for n in BIG}
    exchanges = [None] * len(RS_GROUPS)
    joins = []
    behind_join = ()
    ple_grads = None

    def halves(g):
        return g.reshape(4, 2, g.shape[1] // 2, g.shape[2])

    def start_exchange(layer, gi, sibling, after):
        send_sems, recv_sems, gs, lands, _ = sibling
        gs, lands = _sibling_wait(f"rs_sibling_wait_{layer}_{gi}", send_sems, recv_sems, gs, lands, after)
        chip_sums = _add_halves("rs_add", gs, lands, c_idx)
        return _chip_exchange_start(f"rs_chips_start_{layer}_{gi}", chip_sums)

    def finish_exchange(layer, gi, after):
        send_sems, recv_sems, parts, lands, _ = exchanges[gi]
        parts, lands = _chip_exchange_wait(f"rs_chips_wait_{layer}_{gi}", send_sems, recv_sems, parts, lands, after)
        names = RS_GROUPS[gi]
        sums = _sum_parts("rs_sum", parts, lands, [grad_bufs[n] for n in names], layer, where)
        grad_bufs.update(zip(names, sums))

    for l in reversed(range(depth)):
        wg = gathered[l]
        sv = saved[l]
        fs = f_dim // 4
        started = () if exchanges[1] is None else (exchanges[1][4],)
        if ple_grads is None:
            ple_grads = _ple_bwd("ple_bwd", dh, sv["pg"], sv["pp"])
        d_pp, d_pg = ple_grads
        g_ple_proj = _mm_tn_wide("dw_ple_proj", sv["pb"], d_pp, "col", w_ple_proj.shape[1], 512)
        g_ple_gate = _mm_tn_wide("dw_ple_gate", sv["y3"], d_pg, "row", 512, 1024)
        dy3 = _mm_nt_wide("dx_ple_gate", d_pg, wg["w_ple_gate"], "row", 1024, 512, after=started + behind_join)
        dh, dhb, dg_ple = _rms_bwd("rms_ple_bwd", sv["h2"], norm_ple_g[l:l + 1], dy3, dh)

        d_gt, d_up, act = _swiglu_bwd("dx_down_swiglu", dhb, wg["w_down"], sv["gt"], sv["up"], 512)
        g_down = _mm_tn_wide("dw_down", act, dhb, "row", fs, 512)
        g_gate = _mm_tn_wide("dw_gate", sv["y2"], d_gt, "col", 512, fs)
        g_up = _mm_tn_wide("dw_up", sv["y2"], d_up, "col", 512, fs)
        big = dict(w_gate=g_gate, w_up=g_up, w_down=g_down, w_ple_gate=g_ple_gate, w_ple_proj=g_ple_proj)
        sibling = _sibling_start(f"rs_sibling_start_{l}_0", [halves(big[n]) for n in RS_GROUPS[0]])
        dy2 = _ffn_dy("dx_gate_up", d_gt, d_up, wg["w_gate"], wg["w_up"], (1024, 1024, fs), after=(sibling[4],))
        dh, dhb, dg_ffn = _rms_bwd("rms_ffn_bwd", sv["h1"], norm_ffn_g[l:l + 1], dy2, dh)

        g_out = _mm_tn_wide("dw_out", sv["o"], dhb, "row", 512, 1024)
        if exchanges[0] is not None:
            finish_exchange(l + 1, 0, g_out)
        exchanges[0] = start_exchange(l, 0, sibling, g_out)
        d_o = _mm_nt_wide("dx_out", dhb, wg["w_out"], "row", 1024, 512, after=(exchanges[0][4],))
        z = sv["z"]
        lg, lb = sgu_ln_g[l][:, None, :], sgu_ln_b[l][:, None, :]
        sb = sgu_b[l][:, :, None]
        dzu, dzv, d_lg, d_lb, d_sw, d_sb = _sgu_bwd("sgu_bwd", z, d_o[:, 0:GROUP], lg, lb, sgu_w[l], sb)
        dzh, dzbg, dzcg, d_scw = _shortconv_bwd("shortconv_bwd", z, d_o[:, GROUP:2 * GROUP], sc_w_full[l])
        dcc, d_cflg, d_cflb, d_cfb = _conformer_ln_bwd("conformer_ln_bwd", sv["cc"], d_o[:, 2 * GROUP:3 * GROUP],
                                                       cf_ln_g[l:l + 1], cf_ln_b[l:l + 1])
        dza, dzg, d_cfw = _conformer_conv_bwd("conformer_conv_bwd", z, dcc, cf_w_full[l])
        dzd, d_pw, d_psc = _pool_bwd("pool_bwd", z, d_o[:, 3 * GROUP:], pool_w[l], pool_scale[l:l + 1])
        dz = jnp.concatenate([dzu, dzv, dzh, dzbg, dzcg, dza, dzg, dzd], axis=1)
        g_in = _mm_tn_wide("dw_in", sv["y1"], dz, "col", 512, 1024)
        big.update(w_out=g_out, w_in=g_in)
        sibling = _sibling_start(f"rs_sibling_start_{l}_1", [halves(big[n]) for n in RS_GROUPS[1]])
        dy1 = _mm_nt_wide("dx_in", dz, wg["w_in"], "col", 1024, 512, after=(sibling[4],))
        below = None if l == 0 else (saved[l - 1]["pg"], saved[l - 1]["pp"])
        dh, dhb, dg_mix, *ple_grads = _rms_bwd("rms_mix_bwd", sv["h0"], norm_mix_g[l:l + 1], dy1, dh, ple=below)
        if exchanges[1] is not None:
            finish_exchange(l + 1, 1, dh)
            send_sems, recv_sems, bufs, join_token = _join_start(f"rs_join_start_{l + 1}",
                                                                 [grad_bufs[n] for n in BIG], l + 1)
            grad_bufs.update(zip(BIG, bufs))
            joins.append((l + 1, send_sems, recv_sems))
            behind_join = (join_token,)
        exchanges[1] = start_exchange(l, 1, sibling, dh)

        small_grads[l] = dict(norm_mix_g=dg_mix, sgu_ln_g=d_lg, sgu_ln_b=d_lb, sgu_w=d_sw, sgu_b=d_sb,
                              sc_conv_w=d_scw, cf_conv_w=d_cfw, cf_conv_b=d_cfb, cf_ln_g=d_cflg, cf_ln_b=d_cflb,
                              pool_w=d_pw, pool_scale=d_psc, norm_ffn_g=dg_ffn, norm_ple_g=dg_ple)
    grad_x = dh[None]

    grads, delta, new_m, new_v = {}, {}, {}, {}
    per_layer = [n for n in SMALL if n != "final_norm_g"]
    narrow = [n for n in per_layer if n not in MATRIX_SMALL]
    packed, _ = _pack_rows([small_grads[l][n] for n in narrow for l in range(depth)] + [d_final_g, loss_part])
    packed_wide, _ = _pack_rows([small_grads[l][n] for n in MATRIX_SMALL for l in range(depth)])
    everyone = _allgather8("gather_small", [packed, packed_wide.astype(BF16)])
    total = _sum8("sum_small", everyone[0]).reshape(-1)
    total_wide = _sum8("sum_small_wide", everyone[1]).reshape(-1)

    def stacked_shape(n):
        return (depth,) + (w[n].shape[1:] if n not in CHIP_SPLIT else (w[n].shape[1], GROUP))

    pieces = _unpack(total, [stacked_shape(n) for n in narrow] + [(d,), ()])
    loss = pieces[-1]
    grads["final_norm_g"] = pieces[-2]
    chip_off = (2 * xi + yi) * HEAD
    for n, g in zip(narrow, pieces):
        grads[n] = lax.dynamic_slice_in_dim(g, chip_off, HEAD, axis=2) if n in CHIP_SPLIT else g
    grads.update(zip(MATRIX_SMALL, _unpack(total_wide, [stacked_shape(n) for n in MATRIX_SMALL])))

    finish_exchange(0, 0, dh)
    for layer, send_sems, recv_sems in joins:
        bufs = _join_wait(f"rs_join_wait_{layer}", send_sems, recv_sems, [grad_bufs[n] for n in BIG], layer, total)
        grad_bufs.update(zip(BIG, bufs))
    behind = (exchanges[1][4],)
    for gi, names in enumerate(RS_GROUPS):
        if gi == 1:
            finish_exchange(0, 1, behind[0])
        joined = _sibling_join_halves(f"rs_join_{gi}", [grad_bufs[n] for n in names], 0, after=behind)
        for n, g in zip(names, joined):
            shp = w[n].shape
            two_d = (shp[0] * shp[1], shp[2])
            grads[n] = g.reshape(shp)
            dl, mn, vn = _adamw(f"adamw_{n}", w[n].reshape(two_d), g.reshape(two_d), mom[n].reshape(two_d),
                                var[n].reshape(two_d))
            delta[n], new_m[n], new_v[n] = dl.reshape(shp), mn.reshape(shp), vn.reshape(shp)
            behind = (dl,)
    small_shapes = [w[n].shape for n in SMALL]
    pw, _ = _pack_rows([w[n] for n in SMALL])
    pg_, _ = _pack_rows([grads[n] for n in SMALL])
    pm, _ = _pack_rows([mom[n] for n in SMALL])
    pv, _ = _pack_rows([var[n] for n in SMALL])
    dl, mn, vn = _adamw("adamw_small", pw, pg_, pm, pv)
    for n, a, b, cc_ in zip(SMALL, _unpack(dl.reshape(-1), small_shapes), _unpack(mn.reshape(-1), small_shapes),
                            _unpack(vn.reshape(-1), small_shapes)):
        delta[n], new_m[n], new_v[n] = a, b, cc_

    return (loss, grad_x, *[grads[n] for n in WEIGHTS], *[delta[n] for n in WEIGHTS],
            *[new_m[n] for n in WEIGHTS], *[new_v[n] for n in WEIGHTS])
```

```python
import functools
import math

import jax
import jax.numpy as jnp
from jax import lax
from jax.experimental import pallas as pl
from jax.experimental.pallas import tpu as pltpu

F32 = jnp.float32
BF16 = jnp.bfloat16
MESH = pl.DeviceIdType.MESH

HEAD = 128
GROUP = 512
EPS = 1e-6
HALO = 32
CHUNK_ROWS = 128
POOL_LEVELS = 4
PACK_ROWS = 512

ADAM_LR = 0.001
ADAM_B1 = 0.9
ADAM_B2 = 0.999
ADAM_EPS = 1e-08
ADAM_WD = 0.01
ADAM_STEP = 10

VMEM_LIMIT = 56 * 1024 * 1024


def _cp(sem=None, vmem=VMEM_LIMIT):
    return pltpu.CompilerParams(dimension_semantics=sem, vmem_limit_bytes=vmem)


def _sigmoid(x):
    return 0.5 * jnp.tanh(0.5 * x) + 0.5


_GELU_K = math.sqrt(2.0 / math.pi)
_GELU_C = 0.044715


def _gelu(x):
    t = jnp.tanh(_GELU_K * (x + _GELU_C * x * x * x))
    return 0.5 * x * (1.0 + t)


def _gelu_grad(x):
    t = jnp.tanh(_GELU_K * (x + _GELU_C * x * x * x))
    return 0.5 * (1.0 + t) + 0.5 * x * (1.0 - t * t) * _GELU_K * (1.0 + 3.0 * _GELU_C * x * x)


def _mesh_pos():
    return lax.axis_index("x"), lax.axis_index("y"), lax.axis_index("c")


def _any_specs(n):
    return [pl.BlockSpec(memory_space=pl.ANY)] * n


def _allgather8(name, blocks):
    n = len(blocks)

    def body(*refs):
        ins, outs = refs[:n], refs[n:2 * n]
        send_sems, recv_sems, local_sems = refs[2 * n:]
        x, y, c = _mesh_pos()
        me, sibling = (x, y, c), (x, y, 1 - c)
        chips = [(1 - x, y), (x, 1 - y), (1 - x, 1 - y)]

        def slot(a, dev):
            return outs[a].at[4 * dev[0] + 2 * dev[1] + dev[2]]

        def copy(a, k, block, to, src=None):
            dst = slot(a, block)
            return pltpu.make_async_remote_copy(
                src_ref=dst if src is None else src, dst_ref=dst,
                send_sem=send_sems.at[7 * a + k], recv_sem=recv_sems.at[7 * a + k],
                device_id=to, device_id_type=MESH)

        mine, first, passed = [], [], []
        for a in range(n):
            cp = pltpu.make_async_copy(ins[a], slot(a, me), local_sems.at[a])
            cp.start()
            mine.append(cp)
            cps = [copy(a, 0, me, sibling, src=ins[a])]
            cps += [copy(a, 1 + j, me, (*chip, c), src=ins[a]) for j, chip in enumerate(chips)]
            for cp in cps:
                cp.start()
            first += cps
        for j, chip in enumerate(chips):
            for a in range(n):
                copy(a, 1 + j, (*chip, c), me).wait_recv()
                cp = copy(a, 4 + j, (*chip, c), sibling)
                cp.start()
                passed.append(cp)
        for a in range(n):
            copy(a, 0, sibling, me).wait_recv()
            for j, chip in enumerate(chips):
                copy(a, 4 + j, (*chip, 1 - c), me).wait_recv()
        for cp in first + passed:
            cp.wait_send()
        for cp in mine:
            cp.wait()

    return pl.pallas_call(
        body, name=name,
        out_shape=[jax.ShapeDtypeStruct((8,) + b.shape, b.dtype) for b in blocks],
        in_specs=_any_specs(n), out_specs=_any_specs(n),
        scratch_shapes=[pltpu.SemaphoreType.DMA((7 * n,)), pltpu.SemaphoreType.DMA((7 * n,)),
                        pltpu.SemaphoreType.DMA((n,))],
    )(*blocks)


HBM_SPEC = pl.BlockSpec(memory_space=pltpu.HBM)
SEM_SPEC = pl.BlockSpec(memory_space=pltpu.SEMAPHORE)
ANY_SPEC = pl.BlockSpec(memory_space=pl.ANY)
SPLIT_COPY = pltpu.CompilerParams(has_side_effects=pltpu.SideEffectType.DATAFLOW_SIDE_EFFECTING)


def _hbm(x):
    return pltpu.with_memory_space_constraint(x, pltpu.HBM)


def _other_chips(x, y):
    return [(1 - x, y), (x, 1 - y), (1 - x, 1 - y)]


def _dev_slot(ref, dev):
    return ref.at[4 * dev[0] + 2 * dev[1] + dev[2]]


def _cast_into_landing(name, w, layer, ax, chip_idx):
    _, r, cc = w.shape
    r2, c2 = (r // 2, cc) if ax == 0 else (r, cc // 2)
    tr = _fit_rows(r2, c2, 512 * 1024)
    nt = r2 // tr

    def body(chip_ref, w_ref, o_ref):
        o_ref[...] = w_ref[...].astype(BF16)

    if ax == 0:
        in_spec = pl.BlockSpec((None, tr, c2), lambda hf, i, chip_ref: (layer, hf * nt + i, 0))
    else:
        in_spec = pl.BlockSpec((None, tr, c2), lambda hf, i, chip_ref: (layer, i, hf))
    return pl.pallas_call(
        body, name=name,
        grid_spec=pltpu.PrefetchScalarGridSpec(
            num_scalar_prefetch=1, grid=(2, nt), in_specs=[in_spec],
            out_specs=pl.BlockSpec((None, tr, c2), lambda hf, i, chip_ref: (2 * chip_ref[0] + hf, i, 0))),
        out_shape=jax.ShapeDtypeStruct((8, r2, c2), BF16), compiler_params=_cp(("parallel", "parallel")),
    )(chip_idx, w)


def _gather_ici_copies(lands, send_sems, recv_sems):
    x, y, c = _mesh_pos()
    pairs = []
    for a in range(len(lands)):
        for j, chip in enumerate(_other_chips(x, y)):
            def copy(dev):
                return pltpu.make_async_remote_copy(
                    src_ref=_dev_slot(lands[a], dev), dst_ref=_dev_slot(lands[a], dev),
                    send_sem=send_sems.at[3 * a + j], recv_sem=recv_sems.at[3 * a + j],
                    device_id=(*chip, c), device_id_type=MESH)
            pairs.append((copy((x, y, c)), copy((*chip, c))))
    return pairs


def _exchange_ici_copies(ins, lands, send_sems, recv_sems):
    x, y, c = _mesh_pos()
    pairs = []
    for a in range(len(ins)):
        for k, chip in enumerate(_other_chips(x, y)):
            there = 2 * chip[0] + chip[1]
            def copy(dst_entry):
                return pltpu.make_async_remote_copy(
                    src_ref=ins[a].at[there], dst_ref=lands[a].at[dst_entry],
                    send_sem=send_sems.at[3 * a + k], recv_sem=recv_sems.at[3 * a + k],
                    device_id=(*chip, c), device_id_type=MESH)
            pairs.append((copy(2 * x + y), copy(there)))
    return pairs


def _gather_start(name, lands, dep):
    n = len(lands)

    def body(*refs):
        token = refs[-1]
        for send, _ in _gather_ici_copies(refs[:n], refs[n + 1], refs[n + 2]):
            send.start()
        token[...] = jnp.zeros_like(token)

    out = pl.pallas_call(
        body, name=name,
        out_shape=(pltpu.SemaphoreType.DMA((3 * n,)), pltpu.SemaphoreType.DMA((3 * n,)),
                   *[pltpu.HBM(l.shape, l.dtype) for l in lands], jax.ShapeDtypeStruct((8, 128), F32)),
        in_specs=[HBM_SPEC] * n + [ANY_SPEC],
        out_specs=(SEM_SPEC, SEM_SPEC, *[HBM_SPEC] * n, pl.BlockSpec(memory_space=pltpu.VMEM)),
        input_output_aliases={a: 2 + a for a in range(n)},
        compiler_params=SPLIT_COPY,
    )(*[_hbm(l) for l in lands], dep)
    return out[0], out[1], list(out[2:2 + n]), out[-1]


def _gather_wait(name, send_sems, recv_sems, lands, after):
    n = len(lands)

    def body(*refs):
        for send, recv in _gather_ici_copies(refs[:n], refs[n], refs[n + 1]):
            send.wait_send()
            recv.wait_recv()

    out = pl.pallas_call(
        body, name=name,
        out_shape=[pltpu.HBM(l.shape, l.dtype) for l in lands],
        in_specs=[HBM_SPEC] * n + [SEM_SPEC, SEM_SPEC, ANY_SPEC],
        out_specs=[HBM_SPEC] * n,
        input_output_aliases={a: a for a in range(n)},
        compiler_params=SPLIT_COPY,
    )(*lands, send_sems, recv_sems, after)
    return list(out)


def _share_copies(lands, send_sems, recv_sems):
    x, y, c = _mesh_pos()
    pairs = []
    for a in range(len(lands)):
        for j, chip in enumerate(_other_chips(x, y)):
            def copy(dev):
                slot = _dev_slot(lands[a], dev)
                return pltpu.make_async_remote_copy(
                    src_ref=slot, dst_ref=slot, send_sem=send_sems.at[3 * a + j], recv_sem=recv_sems.at[3 * a + j],
                    device_id=(x, y, 1 - c), device_id_type=MESH)
            pairs.append((copy((*chip, c)), copy((*chip, 1 - c))))
    return pairs


def _share_start(name, lands):
    n = len(lands)

    def body(*refs):
        token = refs[-1]
        for send, _ in _share_copies(refs[:n], refs[n], refs[n + 1]):
            send.start()
        token[...] = jnp.zeros_like(token)

    out = pl.pallas_call(
        body, name=name,
        out_shape=(pltpu.SemaphoreType.DMA((3 * n,)), pltpu.SemaphoreType.DMA((3 * n,)),
                   *[pltpu.HBM(l.shape, l.dtype) for l in lands], jax.ShapeDtypeStruct((8, 128), F32)),
        in_specs=[HBM_SPEC] * n,
        out_specs=(SEM_SPEC, SEM_SPEC, *[HBM_SPEC] * n, pl.BlockSpec(memory_space=pltpu.VMEM)),
        input_output_aliases={a: 2 + a for a in range(n)},
        compiler_params=SPLIT_COPY,
    )(*lands)
    return out[0], out[1], list(out[2:2 + n]), out[-1]


def _share_wait(name, send_sems, recv_sems, lands, after):
    n = len(lands)

    def body(*refs):
        for send, recv in _share_copies(refs[:n], refs[n], refs[n + 1]):
            send.wait_send()
            recv.wait_recv()

    out = pl.pallas_call(
        body, name=name,
        out_shape=[pltpu.HBM(l.shape, l.dtype) for l in lands],
        in_specs=[HBM_SPEC] * n + [SEM_SPEC, SEM_SPEC, ANY_SPEC],
        out_specs=[HBM_SPEC] * n,
        input_output_aliases={a: a for a in range(n)},
        compiler_params=SPLIT_COPY,
    )(*lands, send_sems, recv_sems, after)
    return list(out)


def _sibling_copies(ins, lands, send_sems, recv_sems):
    x, y, c = _mesh_pos()
    return [pltpu.make_async_remote_copy(
        src_ref=ins[a].at[j, 1 - c], dst_ref=lands[a].at[j],
        send_sem=send_sems.at[4 * a + j], recv_sem=recv_sems.at[4 * a + j],
        device_id=(x, y, 1 - c), device_id_type=MESH) for a in range(len(ins)) for j in range(4)]


def _sibling_start(name, grads):
    n = len(grads)

    def body(*refs):
        token = refs[-1]
        for cp in _sibling_copies(refs[:n], refs[n:2 * n], refs[2 * n], refs[2 * n + 1]):
            cp.start()
        token[...] = jnp.zeros_like(token)

    lands = [_hbm(lax.empty((4,) + g.shape[2:], g.dtype)) for g in grads]
    out = pl.pallas_call(
        body, name=name,
        out_shape=(pltpu.SemaphoreType.DMA((4 * n,)), pltpu.SemaphoreType.DMA((4 * n,)),
                   *[pltpu.HBM(g.shape, g.dtype) for g in grads], *[pltpu.HBM(l.shape, l.dtype) for l in lands],
                   jax.ShapeDtypeStruct((8, 128), F32)),
        in_specs=[HBM_SPEC] * (2 * n),
        out_specs=(SEM_SPEC, SEM_SPEC, *[HBM_SPEC] * (2 * n), pl.BlockSpec(memory_space=pltpu.VMEM)),
        input_output_aliases={i: 2 + i for i in range(2 * n)},
        compiler_params=SPLIT_COPY,
    )(*[_hbm(g) for g in grads], *lands)
    return out[0], out[1], list(out[2:2 + n]), list(out[2 + n:2 + 2 * n]), out[-1]


def _sibling_wait(name, send_sems, recv_sems, grads, lands, after):
    n = len(grads)

    def body(*refs):
        for cp in _sibling_copies(refs[:n], refs[n:2 * n], refs[2 * n], refs[2 * n + 1]):
            cp.wait_send()
            cp.wait_recv()

    out = pl.pallas_call(
        body, name=name,
        out_shape=[pltpu.HBM(g.shape, g.dtype) for g in grads] + [pltpu.HBM(l.shape, l.dtype) for l in lands],
        in_specs=[HBM_SPEC] * (2 * n) + [SEM_SPEC, SEM_SPEC, ANY_SPEC],
        out_specs=[HBM_SPEC] * (2 * n),
        input_output_aliases={i: i for i in range(2 * n)},
        compiler_params=SPLIT_COPY,
    )(*grads, *lands, send_sems, recv_sems, after)
    return list(out[:n]), list(out[n:])


def _chip_exchange_start(name, parts):
    n = len(parts)

    def body(*refs):
        ins, lands = refs[:n], refs[n:2 * n]
        token = refs[-1]
        for send, _ in _exchange_ici_copies(ins, lands, refs[2 * n], refs[2 * n + 1]):
            send.start()
        token[...] = jnp.zeros_like(token)

    lands = [_hbm(lax.empty(p.shape, p.dtype)) for p in parts]
    out = pl.pallas_call(
        body, name=name,
        out_shape=(pltpu.SemaphoreType.DMA((3 * n,)), pltpu.SemaphoreType.DMA((3 * n,)),
                   *[pltpu.HBM(p.shape, p.dtype) for p in parts], *[pltpu.HBM(p.shape, p.dtype) for p in parts],
                   jax.ShapeDtypeStruct((8, 128), F32)),
        in_specs=[HBM_SPEC] * (2 * n),
        out_specs=(SEM_SPEC, SEM_SPEC, *[HBM_SPEC] * (2 * n), pl.BlockSpec(memory_space=pltpu.VMEM)),
        input_output_aliases={i: 2 + i for i in range(2 * n)},
        compiler_params=SPLIT_COPY,
    )(*[_hbm(p) for p in parts], *lands)
    return out[0], out[1], list(out[2:2 + n]), list(out[2 + n:2 + 2 * n]), out[-1]


def _chip_exchange_wait(name, send_sems, recv_sems, parts, lands, after):
    n = len(parts)

    def body(*refs):
        ins, lands_in = refs[:n], refs[n:2 * n]
        for send, recv in _exchange_ici_copies(ins, lands_in, refs[2 * n], refs[2 * n + 1]):
            send.wait_send()
            recv.wait_recv()

    out = pl.pallas_call(
        body, name=name,
        out_shape=[pltpu.HBM(p.shape, p.dtype) for p in parts] * 2,
        in_specs=[HBM_SPEC] * (2 * n) + [SEM_SPEC, SEM_SPEC, ANY_SPEC],
        out_specs=[HBM_SPEC] * (2 * n),
        input_output_aliases={i: i for i in range(2 * n)},
        compiler_params=SPLIT_COPY,
    )(*parts, *lands, send_sems, recv_sems, after)
    return list(out[:n]), list(out[n:])


def _join_copies(bufs, send_sems, recv_sems, layer):
    x, y, c = _mesh_pos()

    def copy(a, half):
        return pltpu.make_async_remote_copy(
            src_ref=bufs[a].at[layer, half], dst_ref=bufs[a].at[layer, half],
            send_sem=send_sems.at[a], recv_sem=recv_sems.at[a], device_id=(x, y, 1 - c), device_id_type=MESH)

    return [(copy(a, c), copy(a, 1 - c)) for a in range(len(bufs))]


def _sibling_join_halves(name, bufs, layer, after=()):
    n = len(bufs)

    def body(*refs):
        outs = refs[n + len(after):2 * n + len(after)]
        pairs = _join_copies(outs, refs[-2], refs[-1], layer)
        for send, _ in pairs:
            send.start()
        for send, recv in pairs:
            send.wait_send()
            recv.wait_recv()

    return pl.pallas_call(
        body, name=name, out_shape=[jax.ShapeDtypeStruct(b.shape, b.dtype) for b in bufs],
        in_specs=_any_specs(n + len(after)), out_specs=_any_specs(n),
        input_output_aliases={a: a for a in range(n)},
        scratch_shapes=[pltpu.SemaphoreType.DMA((n,)), pltpu.SemaphoreType.DMA((n,))],
    )(*bufs, *after)


def _join_start(name, bufs, layer):
    n = len(bufs)

    def body(*refs):
        token = refs[-1]
        for send, _ in _join_copies(refs[:n], refs[n], refs[n + 1], layer):
            send.start()
        token[...] = jnp.zeros_like(token)

    out = pl.pallas_call(
        body, name=name,
        out_shape=(pltpu.SemaphoreType.DMA((n,)), pltpu.SemaphoreType.DMA((n,)),
                   *[pltpu.HBM(b.shape, b.dtype) for b in bufs], jax.ShapeDtypeStruct((8, 128), F32)),
        in_specs=[HBM_SPEC] * n,
        out_specs=(SEM_SPEC, SEM_SPEC, *[HBM_SPEC] * n, pl.BlockSpec(memory_space=pltpu.VMEM)),
        input_output_aliases={a: 2 + a for a in range(n)},
        compiler_params=SPLIT_COPY,
    )(*[_hbm(b) for b in bufs])
    return out[0], out[1], list(out[2:2 + n]), out[-1]


def _join_wait(name, send_sems, recv_sems, bufs, layer, after):
    n = len(bufs)

    def body(*refs):
        for send, recv in _join_copies(refs[:n], refs[n], refs[n + 1], layer):
            send.wait_send()
            recv.wait_recv()

    out = pl.pallas_call(
        body, name=name,
        out_shape=[pltpu.HBM(b.shape, b.dtype) for b in bufs],
        in_specs=[HBM_SPEC] * n + [SEM_SPEC, SEM_SPEC, ANY_SPEC],
        out_specs=[HBM_SPEC] * n,
        input_output_aliases={a: a for a in range(n)},
        compiler_params=SPLIT_COPY,
    )(*bufs, send_sems, recv_sems, after)
    return list(out)


def _row_tile(rows, want):
    t = 1
    while t * 2 <= min(rows, want):
        t *= 2
    while rows % t:
        t //= 2
    return t


def _fit_rows(rows, cols, max_elems):
    for k in range(1, rows + 1):
        if rows % k == 0 and (rows // k) % 16 == 0 and (rows // k) * cols <= max_elems:
            return rows // k
    return _row_tile(rows, max(8, max_elems // cols))


def _add_halves(name, grads, recvd, c_idx):
    outs = []
    for a, (g, r) in enumerate(zip(grads, recvd)):
        _, _, r2, cc = g.shape
        tr = _fit_rows(r2, cc, 1536 * 1024)

        def body(c_ref, g_ref, r_ref, o_ref):
            o_ref[...] = (g_ref[...].astype(F32) + r_ref[...].astype(F32)).astype(BF16)

        outs.append(pl.pallas_call(
            body, name=f"{name}_{a}",
            grid_spec=pltpu.PrefetchScalarGridSpec(
                num_scalar_prefetch=1, grid=(4, r2 // tr),
                in_specs=[pl.BlockSpec((None, None, tr, cc), lambda j, i, c_ref: (j, c_ref[0], i, 0)),
                          pl.BlockSpec((None, tr, cc), lambda j, i, c_ref: (j, i, 0))],
                out_specs=pl.BlockSpec((None, tr, cc), lambda j, i, c_ref: (j, i, 0))),
            out_shape=jax.ShapeDtypeStruct(r.shape, BF16), compiler_params=_cp(("parallel", "parallel")),
        )(c_idx, g, r))
    return outs


def _sum_parts(name, parts, recvd, bufs, layer, where):
    outs = []
    for a, (p, r, buf) in enumerate(zip(parts, recvd, bufs)):
        _, r2, cc = p.shape
        tr = _fit_rows(r2, cc, 768 * 1024)

        def body(where_ref, own_ref, r1_ref, r2_ref, r3_ref, buf_ref, o_ref):
            acc = own_ref[...].astype(F32)
            for ref in (r1_ref, r2_ref, r3_ref):
                acc = acc + ref[...].astype(F32)
            o_ref[...] = acc

        def entry(k):
            return pl.BlockSpec((None, tr, cc), lambda i, where_ref: (where_ref[k], i, 0))

        outs.append(pl.pallas_call(
            body, name=f"{name}_{a}",
            grid_spec=pltpu.PrefetchScalarGridSpec(
                num_scalar_prefetch=1, grid=(r2 // tr,),
                in_specs=[entry(0), entry(1), entry(2), entry(3), ANY_SPEC],
                out_specs=pl.BlockSpec((None, None, tr, cc), lambda i, where_ref: (layer, where_ref[4], i, 0))),
            out_shape=jax.ShapeDtypeStruct(buf.shape, F32), input_output_aliases={5: 0},
            compiler_params=_cp(("parallel",)),
        )(where, p, r, r, r, buf))
    return outs


def _sum8(name, gathered):
    _, r, cc = gathered.shape
    tr = _row_tile(r, 512)

    def body(g_ref, o_ref):
        acc = g_ref[0].astype(F32)
        for i in range(1, 8):
            acc = acc + g_ref[i].astype(F32)
        o_ref[...] = acc

    return pl.pallas_call(
        body, name=name, grid=(r // tr,),
        in_specs=[pl.BlockSpec((8, tr, cc), lambda i: (0, i, 0))],
        out_specs=pl.BlockSpec((tr, cc), lambda i: (i, 0)),
        out_shape=jax.ShapeDtypeStruct((r, cc), F32), compiler_params=_cp(("parallel",)),
    )(gathered)


def _adamw(name, w, g, m, v):
    r, cc = w.shape
    tr = _row_tile(r, max(8, (512 * 1024) // cc))

    def body(w_ref, g_ref, m_ref, v_ref, d_ref, mo_ref, vo_ref):
        gg = g_ref[...]
        mn = ADAM_B1 * m_ref[...] + (1.0 - ADAM_B1) * gg
        vn = ADAM_B2 * v_ref[...] + (1.0 - ADAM_B2) * (gg * gg)
        m_hat = mn / (1.0 - ADAM_B1 ** ADAM_STEP)
        v_hat = vn / (1.0 - ADAM_B2 ** ADAM_STEP)
        d_ref[...] = -ADAM_LR * (m_hat / (jnp.sqrt(v_hat) + ADAM_EPS) + ADAM_WD * w_ref[...])
        mo_ref[...] = mn
        vo_ref[...] = vn

    spec = pl.BlockSpec((tr, cc), lambda i: (i, 0))
    return pl.pallas_call(
        body, name=name, grid=(r // tr,), in_specs=[spec] * 4, out_specs=[spec] * 3,
        out_shape=[jax.ShapeDtypeStruct(w.shape, F32)] * 3, compiler_params=_cp(("parallel",)),
    )(w, g, m, v)


def _rms_fwd(name, h, g, after=()):
    s, d = h.shape
    tr = _row_tile(s, 512)

    def body(h_ref, g_ref, *rest):
        y_ref = rest[-1]
        xv = h_ref[...]
        r = lax.rsqrt(jnp.mean(xv * xv, axis=-1, keepdims=True) + EPS)
        y_ref[...] = (xv * r * g_ref[...]).astype(BF16)

    return pl.pallas_call(
        body, name=name, grid=(s // tr,),
        in_specs=[pl.BlockSpec((tr, d), lambda i: (i, 0)), pl.BlockSpec((1, d), lambda i: (0, 0))]
        + [ANY_SPEC] * len(after),
        out_specs=pl.BlockSpec((tr, d), lambda i: (i, 0)),
        out_shape=jax.ShapeDtypeStruct((s, d), BF16), compiler_params=_cp(("parallel",)),
    )(h, g, *after)


def _rms_bwd_rows(xv, gv, dy):
    d = xv.shape[-1]
    r = lax.rsqrt(jnp.mean(xv * xv, axis=-1, keepdims=True) + EPS)
    dxn = dy * gv
    proj = jnp.sum(dxn * xv, axis=-1, keepdims=True) * (1.0 / d)
    dx = r * dxn - xv * (r * r * r) * proj
    return dx, dy * xv * r


def _rms_bwd(name, h, g, dy, dres, ple=None):
    s, d = h.shape
    n_ple = 0 if ple is None else 2
    tr = _row_tile(s, 256 if n_ple else 512)
    band = 16

    def body(h_ref, g_ref, dy_ref, dres_ref, *rest):
        ple_refs = rest[:n_ple]
        dh_ref, dhb_ref, dg_ref = rest[n_ple:n_ple + 3]
        ple_outs = rest[n_ple + 3:2 * n_ple + 3]
        acc = rest[-1]
        acc[...] = jnp.zeros_like(acc)

        def step(i, carry):
            rows = pl.ds(pl.multiple_of(i * band, band), band)
            dx, dgp = _rms_bwd_rows(h_ref[rows, :], g_ref[...], dy_ref[rows, :].astype(F32))
            dh = dres_ref[rows, :] + dx
            dh_ref[rows, :] = dh
            dhb_ref[rows, :] = dh.astype(BF16)
            acc[...] += _fold8(dgp)
            if n_ple:
                sg = _sigmoid(ple_refs[0][rows, :].astype(F32))
                ple_outs[0][rows, :] = (dh * sg).astype(BF16)
                ple_outs[1][rows, :] = (dh * ple_refs[1][rows, :].astype(F32) * sg * (1.0 - sg)).astype(BF16)
            return carry

        lax.fori_loop(0, tr // band, step, 0, unroll=4)

        @pl.when(pl.program_id(0) == 0)
        def _():
            dg_ref[...] = jnp.zeros_like(dg_ref)

        dg_ref[...] += jnp.sum(acc[...], axis=0, keepdims=True)

    row = pl.BlockSpec((tr, d), lambda i: (i, 0))
    vec = pl.BlockSpec((1, d), lambda i: (0, 0))
    return pl.pallas_call(
        body, name=name, grid=(s // tr,), in_specs=[row, vec, row, row] + [row] * n_ple,
        out_specs=[row, row, vec] + [row] * n_ple,
        out_shape=[jax.ShapeDtypeStruct((s, d), F32), jax.ShapeDtypeStruct((s, d), BF16),
                   jax.ShapeDtypeStruct((1, d), F32)] + [jax.ShapeDtypeStruct((s, d), BF16)] * n_ple,
        scratch_shapes=[pltpu.VMEM((8, d), F32)],
        compiler_params=_cp(("arbitrary",)),
    )(h, g, dy, dres, *(ple or ()))


def _loss_head(name, h, g, target):
    s, d = h.shape
    tr = _row_tile(s, 256)

    def body(h_ref, g_ref, t_ref, loss_ref, dh_ref, dhb_ref, dg_ref):
        xv = h_ref[...]
        gv = g_ref[...]
        r = lax.rsqrt(jnp.mean(xv * xv, axis=-1, keepdims=True) + EPS)
        err = xv * r * gv - t_ref[...]
        part = 0.5 * jnp.sum(jnp.sum(err * err, axis=-1, keepdims=True) * (1.0 / d), axis=0, keepdims=True)
        dx, dgp = _rms_bwd_rows(xv, gv, err * (1.0 / d))
        dh_ref[...] = dx
        dhb_ref[...] = dx.astype(BF16)

        @pl.when(pl.program_id(0) == 0)
        def _():
            dg_ref[...] = jnp.zeros_like(dg_ref)
            loss_ref[...] = jnp.zeros_like(loss_ref)

        dg_ref[...] += jnp.sum(dgp, axis=0, keepdims=True)
        loss_ref[...] += part

    row = pl.BlockSpec((tr, d), lambda i: (i, 0))
    vec = pl.BlockSpec((1, d), lambda i: (0, 0))
    one = pl.BlockSpec((1, 1), lambda i: (0, 0))
    return pl.pallas_call(
        body, name=name, grid=(s // tr,), in_specs=[row, vec, row], out_specs=[one, row, row, vec],
        out_shape=[jax.ShapeDtypeStruct((1, 1), F32), jax.ShapeDtypeStruct((s, d), F32),
                   jax.ShapeDtypeStruct((s, d), BF16), jax.ShapeDtypeStruct((1, d), F32)],
        compiler_params=_cp(("arbitrary",)),
    )(h, g, target)


NN = (((1,), (0,)), ((), ()))
NT = (((1,), (1,)), ((), ()))
TN = (((0,), (0,)), ((), ()))


def _matmul(name, grid, operands, in_specs, pairs, n_acc, acc_shape, epilogue, out_specs, out_shape, after=()):
    operands = list(operands) + list(after)
    in_specs = list(in_specs) + [ANY_SPEC] * len(after)
    n_in = len(operands)
    n_out = len(out_shape)
    nk = grid[2]

    def body(*refs):
        ins, outs, accs = refs[:n_in], refs[n_in:n_in + n_out], refs[n_in + n_out:]
        k = pl.program_id(2)

        @pl.when(k == 0)
        def _():
            for acc in accs:
                acc[...] = jnp.zeros_like(acc)

        for li, ri, ai, dims in pairs:
            accs[ai][...] += lax.dot_general(ins[li][...], ins[ri][...], dims, preferred_element_type=F32)

        @pl.when(k == nk - 1)
        def _():
            epilogue([acc[...] for acc in accs], ins, outs)

    return pl.pallas_call(
        body, name=name, grid=grid, in_specs=in_specs, out_specs=out_specs, out_shape=out_shape,
        scratch_shapes=[pltpu.VMEM(acc_shape, F32)] * n_acc,
        compiler_params=_cp(("parallel", "parallel", "arbitrary")),
    )(*operands)


def _ep_store(dtype):
    def ep(accs, ins, outs):
        outs[0][...] = accs[0].astype(dtype)
    return ep


def _w_spec_nt(wg, kind, to, tc):
    _, a, b = wg.shape
    if kind == "col":
        oph, cps = a // to, b // tc
        return pl.BlockSpec((None, to, tc), lambda i, jo, kc: (2 * (kc // cps) + jo // oph, jo % oph, kc % cps))
    ops, cph = a // to, b // tc
    return pl.BlockSpec((None, to, tc), lambda i, jo, kc: (2 * (jo // ops) + kc // cph, jo % ops, kc % cph))


SUB_COLS = 256


def _matmul_wide(name, grid, operands, in_specs, groups, tn, epilogue, out_specs, out_shape, after=()):
    operands = list(operands) + list(after)
    in_specs = list(in_specs) + [ANY_SPEC] * len(after)
    n_in = len(operands)

    def body(*refs):
        ins, outs = refs[:n_in], refs[n_in:]
        for s0 in range(0, tn, SUB_COLS):
            cols = slice(s0, min(s0 + SUB_COLS, tn))
            accs = []
            for group in groups:
                acc = None
                for li, (c0, cw), ri, dims in group:
                    rhs = ins[ri][:, cols] if dims == NN else ins[ri][cols, :]
                    part = lax.dot_general(ins[li][:, c0:c0 + cw], rhs, dims, preferred_element_type=F32)
                    acc = part if acc is None else acc + part
                accs.append(acc)
            epilogue(accs, ins, outs, cols)

    return pl.pallas_call(
        body, name=name, grid=grid, in_specs=in_specs, out_specs=out_specs, out_shape=out_shape,
        compiler_params=_cp(("parallel", "parallel")),
    )(*operands)


def _wide_store(dtype):
    def ep(accs, ins, outs, cols):
        outs[0][:, cols] = accs[0].astype(dtype)
    return ep


def _wide_nn_weights(wg, kind, tn, first):
    _, a, b = wg.shape
    per = b // tn
    if kind == "col":
        specs = [pl.BlockSpec((None, a, tn), lambda i, j, hf=hf: (2 * (j // per) + hf, 0, j % per)) for hf in range(2)]
    else:
        specs = [pl.BlockSpec((None, a, tn), lambda i, j, ch=ch: (2 * ch + j // per, 0, j % per)) for ch in range(4)]
    dots = [(0, (q * a, a), first + q, NN) for q in range(len(specs))]
    return [wg] * len(specs), specs, dots


def _wide_nt_weights(wg, kind, to, first):
    _, a, b = wg.shape
    per = a // to
    if kind == "col":
        specs = [pl.BlockSpec((None, to, b), lambda i, j, ch=ch: (2 * ch + j // per, j % per, 0)) for ch in range(4)]
    else:
        specs = [pl.BlockSpec((None, to, b), lambda i, j, hf=hf: (2 * (j // per) + hf, j % per, 0)) for hf in range(2)]
    dots = [(0, (q * b, b), first + q, NT) for q in range(len(specs))]
    return [wg] * len(specs), specs, dots


def _mm_nn_wide(name, x, wg, kind, tm, tn, epilogue=None, extra=(), extra_specs=(), out_specs=None, out_shape=None,
                after=()):
    m, kdim = x.shape
    _, a, b = wg.shape
    n = 4 * b if kind == "col" else 2 * b
    tm = min(tm, m)
    ws, wspecs, dots = _wide_nn_weights(wg, kind, tn, 1)
    if out_shape is None:
        out_shape = [jax.ShapeDtypeStruct((m, n), F32)]
        out_specs = [pl.BlockSpec((tm, tn), lambda i, j: (i, j))]
        epilogue = _wide_store(F32)
    return _matmul_wide(name, (m // tm, n // tn), [x] + ws + list(extra),
                        [pl.BlockSpec((tm, kdim), lambda i, j: (i, 0))] + wspecs + list(extra_specs),
                        [dots], tn, epilogue, out_specs, out_shape, after=after)


def _mm_nt_wide(name, dy, wg, kind, tm, to, out_dtype=BF16, after=()):
    m, n = dy.shape
    _, a, b = wg.shape
    kdim = 2 * a if kind == "col" else 4 * a
    tm = min(tm, m)
    ws, wspecs, dots = _wide_nt_weights(wg, kind, to, 1)
    return _matmul_wide(name, (m // tm, kdim // to), [dy] + ws,
                        [pl.BlockSpec((tm, n), lambda i, j: (i, 0))] + wspecs, [dots], to, _wide_store(out_dtype),
                        [pl.BlockSpec((tm, to), lambda i, j: (i, j))],
                        [jax.ShapeDtypeStruct((m, kdim), out_dtype)], after=after)[0]


def _mm_tn_wide(name, a, dy, kind, tr, tn):
    m, kdim = a.shape
    _, n = dy.shape

    def body(a_ref, dy_ref, o_ref):
        for s0 in range(0, tn, 2 * SUB_COLS):
            cols = slice(s0, min(s0 + 2 * SUB_COLS, tn))
            o_ref[:, cols] = lax.dot_general(a_ref[...], dy_ref[:, cols], TN, preferred_element_type=F32).astype(BF16)

    dy_resident = tn > tr

    def rj(f):
        return (lambda j, r: f(r, j)) if dy_resident else f

    if kind == "col":
        ns = n // 4
        per = ns // tn
        out_shape = jax.ShapeDtypeStruct((4, kdim, ns), BF16)
        out_spec = pl.BlockSpec((None, tr, tn), rj(lambda r, j: (j // per, r, j % per)))
    else:
        rs = kdim // 4
        per = rs // tr
        out_shape = jax.ShapeDtypeStruct((4, rs, n), BF16)
        out_spec = pl.BlockSpec((None, tr, tn), rj(lambda r, j: (r // per, r % per, j)))
    grid = (n // tn, kdim // tr) if dy_resident else (kdim // tr, n // tn)
    return pl.pallas_call(
        body, name=name, grid=grid,
        in_specs=[pl.BlockSpec((m, tr), rj(lambda r, j: (0, r))), pl.BlockSpec((m, tn), rj(lambda r, j: (0, j)))],
        out_specs=out_spec, out_shape=out_shape, compiler_params=_cp(("parallel", "parallel")),
    )(a, dy)


def _zero_halo(pad_ref, s):
    z = jnp.zeros((HALO, pad_ref.shape[1]), F32)
    pad_ref[pl.ds(0, HALO), :] = z
    pad_ref[pl.ds(HALO + s, HALO), :] = z


def _window(pad_ref, r0, rows):
    return pad_ref[pl.ds(r0, rows + 2 * HALO), :]


def _delayed(win, k, rows):
    if k == 0:
        return win[HALO:HALO + rows]
    return pltpu.roll(win, k, axis=0)[HALO:HALO + rows]


def _advanced(win, k, rows):
    if k == 0:
        return win[HALO:HALO + rows]
    return pltpu.roll(win, win.shape[0] - k, axis=0)[HALO:HALO + rows]


def _fold8(x):
    return jnp.sum(x.reshape(x.shape[0] // 8, 8, x.shape[1]), axis=0)


def _chunks(s):
    rows = min(CHUNK_ROWS, s)
    return rows, s // rows


def _col_spec(s, first_block):
    return pl.BlockSpec((s, HEAD), lambda j: (0, first_block + j))


def _sgu_fwd(name, z, ln_g, ln_b, w, b):
    s = z.shape[0]
    tr = _row_tile(s, 1024)
    nh = GROUP // HEAD

    def body(u_ref, v_ref, lg_ref, lb_ref, w_ref, b_ref, o_ref):
        row = lax.broadcasted_iota(jnp.int32, (HEAD, HEAD), 0)
        col = lax.broadcasted_iota(jnp.int32, (HEAD, HEAD), 1)
        wm = jnp.where(row >= col, w_ref[...], 0.0).astype(BF16)
        for ck in range(tr // HEAD):
            rs = pl.ds(ck * HEAD, HEAD)
            u = _gelu(u_ref[rs, :])
            v = _gelu(v_ref[rs, :])
            mu = jnp.mean(v, axis=-1, keepdims=True)
            vc = v - mu
            var = jnp.mean(vc * vc, axis=-1, keepdims=True)
            vln = vc * lax.rsqrt(var + EPS) * lg_ref[...] + lb_ref[...]
            sp = jnp.dot(wm, vln.astype(BF16), preferred_element_type=F32) + b_ref[...]
            o_ref[rs, :] = (u * sp).astype(BF16)

    head_vec = pl.BlockSpec((None, 1, HEAD), lambda h, i: (h, 0, 0))
    return pl.pallas_call(
        body, name=name, grid=(nh, s // tr),
        in_specs=[pl.BlockSpec((tr, HEAD), lambda h, i: (i, h)), pl.BlockSpec((tr, HEAD), lambda h, i: (i, nh + h)),
                  head_vec, head_vec, pl.BlockSpec((None, HEAD, HEAD), lambda h, i: (h, 0, 0)),
                  pl.BlockSpec((None, HEAD, 1), lambda h, i: (h, 0, 0))],
        out_specs=pl.BlockSpec((tr, HEAD), lambda h, i: (i, h)),
        out_shape=jax.ShapeDtypeStruct((s, GROUP), BF16), compiler_params=_cp(("parallel", "parallel")),
    )(z, z, ln_g, ln_b, w, b)


def _sgu_bwd(name, z, d_o, ln_g, ln_b, w, b):
    s = z.shape[0]
    tr = _row_tile(s, 1024)
    nh = GROUP // HEAD

    def body(u_ref, v_ref, do_ref, lg_ref, lb_ref, w_ref, b_ref, du_ref, dv_ref, dlg_ref, dlb_ref, dw_ref, db_ref,
             dsp_acc):
        row = lax.broadcasted_iota(jnp.int32, (HEAD, HEAD), 0)
        col = lax.broadcasted_iota(jnp.int32, (HEAD, HEAD), 1)
        tril = row >= col
        wm = jnp.where(tril, w_ref[...], 0.0).astype(BF16)
        i = pl.program_id(1)

        @pl.when(i == 0)
        def _():
            dlg_ref[...] = jnp.zeros_like(dlg_ref)
            dlb_ref[...] = jnp.zeros_like(dlb_ref)
            dw_ref[...] = jnp.zeros_like(dw_ref)
            dsp_acc[...] = jnp.zeros_like(dsp_acc)

        dlg = jnp.zeros((1, HEAD), F32)
        dlb = jnp.zeros((1, HEAD), F32)
        dw = jnp.zeros((HEAD, HEAD), F32)
        dsp_sum = jnp.zeros((HEAD, HEAD), F32)
        for ck in range(tr // HEAD):
            rs = pl.ds(ck * HEAD, HEAD)
            zu = u_ref[rs, :]
            zv = v_ref[rs, :]
            u = _gelu(zu)
            v = _gelu(zv)
            mu = jnp.mean(v, axis=-1, keepdims=True)
            vc = v - mu
            var = jnp.mean(vc * vc, axis=-1, keepdims=True)
            rstd = lax.rsqrt(var + EPS)
            xh = vc * rstd
            vln = (xh * lg_ref[...] + lb_ref[...]).astype(BF16)
            sp = jnp.dot(wm, vln, preferred_element_type=F32) + b_ref[...]
            d_oa = do_ref[rs, :].astype(F32)
            du = d_oa * sp
            dsp = d_oa * u
            dsp_b = dsp.astype(BF16)
            dvln = lax.dot_general(wm, dsp_b, TN, preferred_element_type=F32)
            dw = dw + lax.dot_general(dsp_b, vln, NT, preferred_element_type=F32)
            dsp_sum = dsp_sum + dsp
            dlg = dlg + jnp.sum(dvln * xh, axis=0, keepdims=True)
            dlb = dlb + jnp.sum(dvln, axis=0, keepdims=True)
            dxh = dvln * lg_ref[...]
            dv = rstd * (dxh - jnp.mean(dxh, axis=-1, keepdims=True)
                         - xh * jnp.mean(dxh * xh, axis=-1, keepdims=True))
            du_ref[rs, :] = (du * _gelu_grad(zu)).astype(BF16)
            dv_ref[rs, :] = (dv * _gelu_grad(zv)).astype(BF16)
        dlg_ref[...] += dlg
        dlb_ref[...] += dlb
        dw_ref[...] += jnp.where(tril, dw, 0.0)
        dsp_acc[...] += dsp_sum

        @pl.when(i == pl.num_programs(1) - 1)
        def _():
            db_ref[...] = jnp.sum(dsp_acc[...], axis=1, keepdims=True)

    head_vec = pl.BlockSpec((None, 1, HEAD), lambda h, i: (h, 0, 0))
    head_mat = pl.BlockSpec((None, HEAD, HEAD), lambda h, i: (h, 0, 0))
    head_col = pl.BlockSpec((None, HEAD, 1), lambda h, i: (h, 0, 0))
    return pl.pallas_call(
        body, name=name, grid=(nh, s // tr),
        in_specs=[pl.BlockSpec((tr, HEAD), lambda h, i: (i, h)), pl.BlockSpec((tr, HEAD), lambda h, i: (i, nh + h)),
                  pl.BlockSpec((tr, HEAD), lambda h, i: (i, h)), head_vec, head_vec, head_mat, head_col],
        out_specs=[pl.BlockSpec((tr, HEAD), lambda h, i: (i, h)), pl.BlockSpec((tr, HEAD), lambda h, i: (i, h)),
                   head_vec, head_vec, head_mat, head_col],
        out_shape=[jax.ShapeDtypeStruct((s, GROUP), BF16), jax.ShapeDtypeStruct((s, GROUP), BF16),
                   jax.ShapeDtypeStruct((nh, 1, HEAD), F32), jax.ShapeDtypeStruct((nh, 1, HEAD), F32),
                   jax.ShapeDtypeStruct((nh, HEAD, HEAD), F32), jax.ShapeDtypeStruct((nh, HEAD, 1), F32)],
        scratch_shapes=[pltpu.VMEM((HEAD, HEAD), F32)],
        compiler_params=_cp(("parallel", "arbitrary")),
    )(z, z, d_o, ln_g, ln_b, w, b)


def _shortconv_fwd(name, z, w):
    s = z.shape[0]
    kw = w.shape[0]
    rows, nchunk = _chunks(s)
    nb = GROUP // HEAD

    def body(h_ref, bg_ref, cg_ref, w_ref, o_ref, pad):
        _zero_halo(pad, s)

        def fill(ci, carry):
            r0 = pl.multiple_of(ci * rows, rows)
            pad[pl.ds(pl.multiple_of(HALO + r0, 8), rows), :] = cg_ref[pl.ds(r0, rows), :] * h_ref[pl.ds(r0, rows), :]
            return carry

        lax.fori_loop(0, nchunk, fill, 0)

        def step(ci, carry):
            r0 = pl.multiple_of(ci * rows, rows)
            win = _window(pad, r0, rows)
            cv = jnp.zeros((rows, HEAD), F32)
            for k in range(kw):
                cv = cv + w_ref[k:k + 1, :] * _delayed(win, kw - 1 - k, rows)
            o_ref[pl.ds(r0, rows), :] = (bg_ref[pl.ds(r0, rows), :] * cv).astype(BF16)
            return carry

        lax.fori_loop(0, nchunk, step, 0, unroll=min(4, nchunk))

    return pl.pallas_call(
        body, name=name, grid=(nb,),
        in_specs=[_col_spec(s, 8), _col_spec(s, 12), _col_spec(s, 16), pl.BlockSpec((kw, HEAD), lambda j: (0, j))],
        out_specs=_col_spec(s, 0),
        out_shape=jax.ShapeDtypeStruct((s, GROUP), BF16),
        scratch_shapes=[pltpu.VMEM((s + 2 * HALO, HEAD), F32)],
        compiler_params=_cp(("parallel",)),
    )(z, z, z, w)


def _shortconv_bwd(name, z, d_o, w):
    s = z.shape[0]
    kw = w.shape[0]
    rows, nchunk = _chunks(s)
    nb = GROUP // HEAD

    def body(h_ref, bg_ref, cg_ref, do_ref, w_ref, dh_ref, dbg_ref, dcg_ref, dw_ref, pad_q, pad_d, acc):
        _zero_halo(pad_q, s)
        _zero_halo(pad_d, s)
        acc[...] = jnp.zeros_like(acc)

        def fill(ci, carry):
            r0 = pl.multiple_of(ci * rows, rows)
            rs = pl.ds(r0, rows)
            ps = pl.ds(pl.multiple_of(HALO + r0, 8), rows)
            pad_q[ps, :] = cg_ref[rs, :] * h_ref[rs, :]
            pad_d[ps, :] = do_ref[rs, :].astype(F32) * bg_ref[rs, :]
            return carry

        lax.fori_loop(0, nchunk, fill, 0)

        def step(ci, carry):
            r0 = pl.multiple_of(ci * rows, rows)
            rs = pl.ds(r0, rows)
            wq = _window(pad_q, r0, rows)
            wd = _window(pad_d, r0, rows)
            dcv = wd[HALO:HALO + rows]
            cv = jnp.zeros((rows, HEAD), F32)
            dq = jnp.zeros((rows, HEAD), F32)
            for k in range(kw):
                qk = _delayed(wq, kw - 1 - k, rows)
                cv = cv + w_ref[k:k + 1, :] * qk
                dq = dq + w_ref[k:k + 1, :] * _advanced(wd, kw - 1 - k, rows)
                acc[k] += _fold8(dcv * qk)
            dbg_ref[rs, :] = (do_ref[rs, :].astype(F32) * cv).astype(BF16)
            dcg_ref[rs, :] = (dq * h_ref[rs, :]).astype(BF16)
            dh_ref[rs, :] = (dq * cg_ref[rs, :]).astype(BF16)
            return carry

        lax.fori_loop(0, nchunk, step, 0, unroll=min(2, nchunk))
        for k in range(kw):
            dw_ref[k:k + 1, :] = jnp.sum(acc[k], axis=0, keepdims=True)

    col = _col_spec(s, 0)
    return pl.pallas_call(
        body, name=name, grid=(nb,),
        in_specs=[_col_spec(s, 8), _col_spec(s, 12), _col_spec(s, 16), col, pl.BlockSpec((kw, HEAD), lambda j: (0, j))],
        out_specs=[col, col, col, pl.BlockSpec((kw, HEAD), lambda j: (0, j))],
        out_shape=[jax.ShapeDtypeStruct((s, GROUP), BF16)] * 3 + [jax.ShapeDtypeStruct((kw, GROUP), F32)],
        scratch_shapes=[pltpu.VMEM((s + 2 * HALO, HEAD), F32), pltpu.VMEM((s + 2 * HALO, HEAD), F32),
                        pltpu.VMEM((kw, 8, HEAD), F32)],
        compiler_params=_cp(("parallel",)),
    )(z, z, z, d_o, w)


def _conformer_conv_fwd(name, z, w, bias):
    s = z.shape[0]
    kw = w.shape[0]
    rows, nchunk = _chunks(s)
    nb = GROUP // HEAD

    def body(a_ref, g_ref, w_ref, b_ref, o_ref, pad):
        _zero_halo(pad, s)

        def fill(ci, carry):
            r0 = pl.multiple_of(ci * rows, rows)
            rs = pl.ds(r0, rows)
            pad[pl.ds(pl.multiple_of(HALO + r0, 8), rows), :] = a_ref[rs, :] * _sigmoid(g_ref[rs, :])
            return carry

        lax.fori_loop(0, nchunk, fill, 0)

        def step(ci, carry):
            r0 = pl.multiple_of(ci * rows, rows)
            win = _window(pad, r0, rows)
            cc = jnp.zeros((rows, HEAD), F32)
            for k in range(kw):
                cc = cc + w_ref[k:k + 1, :] * _delayed(win, kw - 1 - k, rows)
            o_ref[pl.ds(r0, rows), :] = cc + b_ref[...]
            return carry

        lax.fori_loop(0, nchunk, step, 0)

    return pl.pallas_call(
        body, name=name, grid=(nb,),
        in_specs=[_col_spec(s, 20), _col_spec(s, 24), pl.BlockSpec((kw, HEAD), lambda j: (0, j)),
                  pl.BlockSpec((1, HEAD), lambda j: (0, j))],
        out_specs=_col_spec(s, 0),
        out_shape=jax.ShapeDtypeStruct((s, GROUP), F32),
        scratch_shapes=[pltpu.VMEM((s + 2 * HALO, HEAD), F32)],
        compiler_params=_cp(("parallel",)),
    )(z, z, w, bias)


def _ln_rows(cc, g, b):
    mu = jnp.mean(cc, axis=-1, keepdims=True)
    xc = cc - mu
    var = jnp.mean(xc * xc, axis=-1, keepdims=True)
    rstd = lax.rsqrt(var + EPS)
    xh = xc * rstd
    return xh, rstd, xh * g + b


def _conformer_ln_fwd(name, cc, g, b):
    s, d = cc.shape
    tr = _row_tile(s, 512)

    def body(c_ref, g_ref, b_ref, o_ref):
        _, _, l = _ln_rows(c_ref[...], g_ref[...], b_ref[...])
        o_ref[...] = (l * _sigmoid(l)).astype(BF16)

    row = pl.BlockSpec((tr, d), lambda i: (i, 0))
    vec = pl.BlockSpec((1, d), lambda i: (0, 0))
    return pl.pallas_call(
        body, name=name, grid=(s // tr,), in_specs=[row, vec, vec], out_specs=row,
        out_shape=jax.ShapeDtypeStruct((s, d), BF16), compiler_params=_cp(("parallel",)),
    )(cc, g, b)


def _conformer_ln_bwd(name, cc, d_o, g, b):
    s, d = cc.shape
    tr = _row_tile(s, 512)

    def body(c_ref, do_ref, g_ref, b_ref, dcc_ref, dg_ref, db_ref, dcb_ref):
        xh, rstd, l = _ln_rows(c_ref[...], g_ref[...], b_ref[...])
        sg = _sigmoid(l)
        dl = do_ref[...].astype(F32) * sg * (1.0 + l * (1.0 - sg))
        dxh = dl * g_ref[...]
        dcc = rstd * (dxh - jnp.mean(dxh, axis=-1, keepdims=True) - xh * jnp.mean(dxh * xh, axis=-1, keepdims=True))
        dcc_ref[...] = dcc

        @pl.when(pl.program_id(0) == 0)
        def _():
            dg_ref[...] = jnp.zeros_like(dg_ref)
            db_ref[...] = jnp.zeros_like(db_ref)
            dcb_ref[...] = jnp.zeros_like(dcb_ref)

        dg_ref[...] += jnp.sum(dl * xh, axis=0, keepdims=True)
        db_ref[...] += jnp.sum(dl, axis=0, keepdims=True)
        dcb_ref[...] += jnp.sum(dcc, axis=0, keepdims=True)

    row = pl.BlockSpec((tr, d), lambda i: (i, 0))
    vec = pl.BlockSpec((1, d), lambda i: (0, 0))
    return pl.pallas_call(
        body, name=name, grid=(s // tr,), in_specs=[row, row, vec, vec], out_specs=[row, vec, vec, vec],
        out_shape=[jax.ShapeDtypeStruct((s, d), F32)] + [jax.ShapeDtypeStruct((1, d), F32)] * 3,
        compiler_params=_cp(("arbitrary",)),
    )(cc, d_o, g, b)


def _conformer_conv_bwd(name, z, dcc, w):
    s = z.shape[0]
    kw = w.shape[0]
    rows, nchunk = _chunks(s)
    nb = GROUP // HEAD

    def body(a_ref, g_ref, d_ref, w_ref, da_ref, dg_ref, dw_ref, pad_h, pad_d, acc):
        _zero_halo(pad_h, s)
        _zero_halo(pad_d, s)
        acc[...] = jnp.zeros_like(acc)

        def fill(ci, carry):
            r0 = pl.multiple_of(ci * rows, rows)
            rs = pl.ds(r0, rows)
            ps = pl.ds(pl.multiple_of(HALO + r0, 8), rows)
            pad_h[ps, :] = a_ref[rs, :] * _sigmoid(g_ref[rs, :])
            pad_d[ps, :] = d_ref[rs, :]
            return carry

        lax.fori_loop(0, nchunk, fill, 0)

        def step(ci, carry):
            r0 = pl.multiple_of(ci * rows, rows)
            rs = pl.ds(r0, rows)
            wh = _window(pad_h, r0, rows)
            wd = _window(pad_d, r0, rows)
            dcc_c = wd[HALO:HALO + rows]
            dhc = jnp.zeros((rows, HEAD), F32)
            for k in range(kw):
                dhc = dhc + w_ref[k:k + 1, :] * _advanced(wd, kw - 1 - k, rows)
                acc[k] += _fold8(dcc_c * _delayed(wh, kw - 1 - k, rows))
            sg = _sigmoid(g_ref[rs, :])
            da_ref[rs, :] = (dhc * sg).astype(BF16)
            dg_ref[rs, :] = (dhc * a_ref[rs, :] * sg * (1.0 - sg)).astype(BF16)
            return carry

        lax.fori_loop(0, nchunk, step, 0)
        for k in range(kw):
            dw_ref[k:k + 1, :] = jnp.sum(acc[k], axis=0, keepdims=True)

    col = _col_spec(s, 0)
    return pl.pallas_call(
        body, name=name, grid=(nb,),
        in_specs=[_col_spec(s, 20), _col_spec(s, 24), col, pl.BlockSpec((kw, HEAD), lambda j: (0, j))],
        out_specs=[col, col, pl.BlockSpec((kw, HEAD), lambda j: (0, j))],
        out_shape=[jax.ShapeDtypeStruct((s, GROUP), BF16)] * 2 + [jax.ShapeDtypeStruct((kw, GROUP), F32)],
        scratch_shapes=[pltpu.VMEM((s + 2 * HALO, HEAD), F32), pltpu.VMEM((s + 2 * HALO, HEAD), F32),
                        pltpu.VMEM((kw, 8, HEAD), F32)],
        compiler_params=_cp(("parallel",)),
    )(z, z, dcc, w)


def _pool_window_sum(win, level, rows, shift):
    n = win.shape[0]

    def moved(v, k):
        return pltpu.roll(v, k if shift is _delayed else n - k, axis=0)

    s2 = win + moved(win, 1)
    s4 = s2 + moved(s2, 2)
    s8 = s4 + moved(s4, 4)
    s16 = s8 + moved(s8, 8)
    sel = jnp.where(level == 0, s2, jnp.where(level == 1, s4, jnp.where(level == 2, s8, s16)))
    return sel[HALO:HALO + rows]


def _pool_count(level, r0, rows):
    t = r0 + lax.broadcasted_iota(jnp.int32, (rows, 1), 0)
    width = jnp.left_shift(jnp.int32(2), level)
    return jnp.minimum(t + 1, width).astype(F32)


def _pool_fwd(name, z, pool_w, scale):
    s = z.shape[0]
    rows, nchunk = _chunks(s)

    def body(z_ref, w_ref, sc_ref, o_ref, pad):
        level = pl.program_id(0)
        _zero_halo(pad, s)

        def fill(ci, carry):
            r0 = pl.multiple_of(ci * rows, rows)
            pad[pl.ds(pl.multiple_of(HALO + r0, 8), rows), :] = z_ref[pl.ds(r0, rows), :]
            return carry

        lax.fori_loop(0, nchunk, fill, 0)
        wb = w_ref[...].astype(BF16)

        def step(ci, carry):
            r0 = pl.multiple_of(ci * rows, rows)
            win = _window(pad, r0, rows)
            pm = _pool_window_sum(win, level, rows, _delayed) / _pool_count(level, r0, rows) - win[HALO:HALO + rows]
            r = jnp.dot(pm.astype(BF16), wb, preferred_element_type=F32)
            o_ref[pl.ds(r0, rows), :] = (r * sc_ref[...]).astype(BF16)
            return carry

        lax.fori_loop(0, nchunk, step, 0, unroll=min(4, nchunk))

    return pl.pallas_call(
        body, name=name, grid=(POOL_LEVELS,),
        in_specs=[_col_spec(s, 28), pl.BlockSpec((None, HEAD, HEAD), lambda j: (j, 0, 0)),
                  pl.BlockSpec((1, HEAD), lambda j: (0, j))],
        out_specs=_col_spec(s, 0),
        out_shape=jax.ShapeDtypeStruct((s, GROUP), BF16),
        scratch_shapes=[pltpu.VMEM((s + 2 * HALO, HEAD), F32)],
        compiler_params=_cp(("parallel",)),
    )(z, pool_w, scale)


def _pool_bwd(name, z, d_o, pool_w, scale):
    s = z.shape[0]
    rows, nchunk = _chunks(s)

    def body(z_ref, do_ref, w_ref, sc_ref, dz_ref, dw_ref, dsc_ref, pad, pad_q, dw_acc, dsc_acc):
        level = pl.program_id(0)
        _zero_halo(pad, s)
        _zero_halo(pad_q, s)
        dw_acc[...] = jnp.zeros_like(dw_acc)
        dsc_acc[...] = jnp.zeros_like(dsc_acc)

        def fill(ci, carry):
            r0 = pl.multiple_of(ci * rows, rows)
            pad[pl.ds(pl.multiple_of(HALO + r0, 8), rows), :] = z_ref[pl.ds(r0, rows), :]
            return carry

        lax.fori_loop(0, nchunk, fill, 0)
        wb = w_ref[...].astype(BF16)

        def first(ci, carry):
            r0 = pl.multiple_of(ci * rows, rows)
            win = _window(pad, r0, rows)
            cnt = _pool_count(level, r0, rows)
            pm = (_pool_window_sum(win, level, rows, _delayed) / cnt - win[HALO:HALO + rows]).astype(BF16)
            r = jnp.dot(pm, wb, preferred_element_type=F32)
            d_od = do_ref[pl.ds(r0, rows), :].astype(F32)
            dsc_acc[...] += _fold8(d_od * r)
            dr = (d_od * sc_ref[...]).astype(BF16)
            dw_acc[...] += lax.dot_general(pm, dr, TN, preferred_element_type=F32)
            dpm = lax.dot_general(dr, wb, NT, preferred_element_type=F32)
            pad_q[pl.ds(pl.multiple_of(HALO + r0, 8), rows), :] = dpm / cnt
            return carry

        lax.fori_loop(0, nchunk, first, 0, unroll=min(4, nchunk))

        def second(ci, carry):
            r0 = pl.multiple_of(ci * rows, rows)
            wq = _window(pad_q, r0, rows)
            dpm = wq[HALO:HALO + rows] * _pool_count(level, r0, rows)
            dz_ref[pl.ds(r0, rows), :] = (_pool_window_sum(wq, level, rows, _advanced) - dpm).astype(BF16)
            return carry

        lax.fori_loop(0, nchunk, second, 0)
        dw_ref[...] = dw_acc[...]
        dsc_ref[...] = jnp.sum(dsc_acc[...], axis=0, keepdims=True)

    col = _col_spec(s, 0)
    mat = pl.BlockSpec((None, HEAD, HEAD), lambda j: (j, 0, 0))
    vec = pl.BlockSpec((1, HEAD), lambda j: (0, j))
    return pl.pallas_call(
        body, name=name, grid=(POOL_LEVELS,),
        in_specs=[_col_spec(s, 28), col, mat, vec], out_specs=[col, mat, vec],
        out_shape=[jax.ShapeDtypeStruct((s, GROUP), BF16), jax.ShapeDtypeStruct((POOL_LEVELS, HEAD, HEAD), F32),
                   jax.ShapeDtypeStruct((1, GROUP), F32)],
        scratch_shapes=[pltpu.VMEM((s + 2 * HALO, HEAD), F32), pltpu.VMEM((s + 2 * HALO, HEAD), F32),
                        pltpu.VMEM((HEAD, HEAD), F32), pltpu.VMEM((8, HEAD), F32)],
        compiler_params=_cp(("parallel",)),
    )(z, d_o, pool_w, scale)


def _mm_residual(name, x, wg, h, tm, tn):
    m, d = h.shape
    tm = min(tm, m)

    def ep(accs, ins, outs, cols):
        outs[0][:, cols] = ins[5][:, cols] + accs[0]

    tile = pl.BlockSpec((tm, tn), lambda i, j: (i, j))
    return _mm_nn_wide(name, x, wg, "row", tm, tn, epilogue=ep, extra=[h], extra_specs=[tile], out_specs=[tile],
                       out_shape=[jax.ShapeDtypeStruct((m, d), F32)])[0]


def _swiglu_fwd(name, y, wg_gate, wg_up, tm):
    m, kdim = y.shape
    _, a, b = wg_gate.shape
    tm = min(tm, m)

    def ep(accs, ins, outs, cols):
        gt, up = accs
        outs[0][:, cols] = (gt * _sigmoid(gt) * up).astype(BF16)
        outs[1][:, cols] = gt.astype(BF16)
        outs[2][:, cols] = up.astype(BF16)

    gws, gspecs, gdots = _wide_nn_weights(wg_gate, "col", b, 1)
    uws, uspecs, udots = _wide_nn_weights(wg_up, "col", b, 1 + len(gws))
    out = pl.BlockSpec((tm, b), lambda i, j: (i, j))
    return _matmul_wide(name, (m // tm, 4), [y] + gws + uws,
                        [pl.BlockSpec((tm, kdim), lambda i, j: (i, 0))] + gspecs + uspecs, [gdots, udots], b, ep,
                        [out] * 3, [jax.ShapeDtypeStruct((m, 4 * b), BF16)] * 3)


def _swiglu_bwd(name, dh, wg_down, gate, up, tm):
    m, n = dh.shape
    _, a, b = wg_down.shape
    tm = min(tm, m)

    def ep(accs, ins, outs, cols):
        d_act = accs[0]
        gt = ins[3][:, cols].astype(F32)
        upv = ins[4][:, cols].astype(F32)
        sg = _sigmoid(gt)
        outs[0][:, cols] = (d_act * upv * sg * (1.0 + gt * (1.0 - sg))).astype(BF16)
        outs[1][:, cols] = (d_act * gt * sg).astype(BF16)
        outs[2][:, cols] = (gt * sg * upv).astype(BF16)

    ws, wspecs, dots = _wide_nt_weights(wg_down, "row", a, 1)
    tile = pl.BlockSpec((tm, a), lambda i, j: (i, j))
    return _matmul_wide(name, (m // tm, 4), [dh] + ws + [gate, up],
                        [pl.BlockSpec((tm, n), lambda i, j: (i, 0))] + wspecs + [tile, tile], [dots], a, ep,
                        [tile] * 3, [jax.ShapeDtypeStruct((m, 4 * a), BF16)] * 3)


def _ffn_dy(name, d_gate, d_up, wg_gate, wg_up, tiles, after=()):
    m, n = d_gate.shape
    _, a, b = wg_gate.shape
    kdim = 2 * a
    tm, to, tc = tiles
    tm = min(tm, m)
    grid = (m // tm, kdim // to, n // tc)
    lhs = pl.BlockSpec((tm, tc), lambda i, j, k: (i, k))
    wspec = _w_spec_nt(wg_gate, "col", to, tc)
    return _matmul(name, grid, [d_gate, d_up, wg_gate, wg_up], [lhs, lhs, wspec, wspec],
                   [(0, 2, 0, NT), (1, 3, 0, NT)], 1, (tm, to), _ep_store(BF16),
                   [pl.BlockSpec((tm, to), lambda i, j, k: (i, j))], [jax.ShapeDtypeStruct((m, kdim), BF16)],
                   after=after)[0]


def _ple_fwd(name, y, wg, h, pp, tm, tn, after=()):
    m, d = h.shape
    tm = min(tm, m)

    def ep(accs, ins, outs, cols):
        pg = accs[0]
        outs[0][:, cols] = ins[5][:, cols] + _sigmoid(pg) * ins[6][:, cols].astype(F32)
        outs[1][:, cols] = pg.astype(BF16)

    tile = pl.BlockSpec((tm, tn), lambda i, j: (i, j))
    return _mm_nn_wide(name, y, wg, "row", tm, tn, epilogue=ep, extra=[h, pp], extra_specs=[tile, tile],
                       out_specs=[tile, tile],
                       out_shape=[jax.ShapeDtypeStruct((m, d), F32), jax.ShapeDtypeStruct((m, d), BF16)], after=after)


def _ple_bwd(name, dh, pg, pp, after=()):
    s, d = dh.shape
    tr = _row_tile(s, 512)

    def body(dh_ref, pg_ref, pp_ref, *rest):
        dpp_ref, dpg_ref = rest[-2:]
        dhv = dh_ref[...]
        sg = _sigmoid(pg_ref[...].astype(F32))
        dpp_ref[...] = (dhv * sg).astype(BF16)
        dpg_ref[...] = (dhv * pp_ref[...].astype(F32) * sg * (1.0 - sg)).astype(BF16)

    row = pl.BlockSpec((tr, d), lambda i: (i, 0))
    return pl.pallas_call(
        body, name=name, grid=(s // tr,), in_specs=[row] * 3 + [ANY_SPEC] * len(after), out_specs=[row] * 2,
        out_shape=[jax.ShapeDtypeStruct((s, d), BF16)] * 2, compiler_params=_cp(("parallel",)),
    )(dh, pg, pp, *after)


BIG = ["w_in", "w_out", "w_gate", "w_up", "w_down", "w_ple_gate", "w_ple_proj"]
KIND = {"w_in": "col", "w_out": "row", "w_gate": "col", "w_up": "col", "w_down": "row", "w_ple_gate": "row",
        "w_ple_proj": "col"}
GATHER_GROUPS = (("w_in", "w_out"), ("w_gate", "w_up"), ("w_down", "w_ple_gate", "w_ple_proj"))
RS_GROUPS = (("w_ple_gate", "w_ple_proj", "w_down", "w_gate", "w_up"), ("w_out", "w_in"))
SMALL = ["norm_mix_g", "sgu_ln_g", "sgu_ln_b", "sgu_w", "sgu_b", "sc_conv_w", "cf_conv_w", "cf_conv_b", "cf_ln_g",
         "cf_ln_b", "pool_w", "pool_scale", "norm_ffn_g", "norm_ple_g", "final_norm_g"]
CHIP_SPLIT = ["sc_conv_w", "cf_conv_w"]
MATRIX_SMALL = ["sgu_w", "pool_w"]
WEIGHTS = ['norm_mix_g', 'w_in', 'sgu_ln_g', 'sgu_ln_b', 'sgu_w', 'sgu_b', 'sc_conv_w', 'cf_conv_w', 'cf_conv_b',
           'cf_ln_g', 'cf_ln_b', 'pool_w', 'pool_scale', 'w_out', 'norm_ffn_g', 'w_gate', 'w_up', 'w_down',
           'norm_ple_g', 'w_ple_gate', 'w_ple_proj', 'final_norm_g']


def _tile(n, want):
    if n <= want:
        return n
    t = (want // 128) * 128
    while n % t:
        t -= 128
    return t


def _pack_rows(vecs):
    flat = jnp.concatenate([v.reshape(-1) for v in vecs])
    n = flat.shape[0]
    quantum = PACK_ROWS * 128
    padded = ((n + quantum - 1) // quantum) * quantum
    return jnp.pad(flat, (0, padded - n)).reshape(padded // 128, 128), n


def _unpack(flat, shapes):
    out, off = [], 0
    for shp in shapes:
        size = math.prod(shp)
        out.append(flat[off:off + size].reshape(shp))
        off += size
    return out


def kernel(x, p, norm_mix_g, w_in, sgu_ln_g, sgu_ln_b, sgu_w, sgu_b, sc_conv_w, cf_conv_w, cf_conv_b, cf_ln_g, cf_ln_b, pool_w, pool_scale, w_out, norm_ffn_g, w_gate, w_up, w_down, norm_ple_g, w_ple_gate, w_ple_proj, final_norm_g, loss_target, m_norm_mix_g, m_w_in, m_sgu_ln_g, m_sgu_ln_b, m_sgu_w, m_sgu_b, m_sc_conv_w, m_cf_conv_w, m_cf_conv_b, m_cf_ln_g, m_cf_ln_b, m_pool_w, m_pool_scale, m_w_out, m_norm_ffn_g, m_w_gate, m_w_up, m_w_down, m_norm_ple_g, m_w_ple_gate, m_w_ple_proj, m_final_norm_g, v_norm_mix_g, v_w_in, v_sgu_ln_g, v_sgu_ln_b, v_sgu_w, v_sgu_b, v_sc_conv_w, v_cf_conv_w, v_cf_conv_b, v_cf_ln_g, v_cf_ln_b, v_pool_w, v_pool_scale, v_w_out, v_norm_ffn_g, v_w_gate, v_w_up, v_w_down, v_norm_ple_g, v_w_ple_gate, v_w_ple_proj, v_final_norm_g):
    args = dict(locals())
    w = {n: args[n] for n in WEIGHTS}
    mom = {n: args["m_" + n] for n in WEIGHTS}
    var = {n: args["v_" + n] for n in WEIGHTS}
    depth = w_in.shape[0]
    s, d = x.shape[1], x.shape[2]
    f_dim = 4 * w_gate.shape[2]
    xi, yi, ci = lax.axis_index("x"), lax.axis_index("y"), lax.axis_index("c")
    c_idx = ci.astype(jnp.int32).reshape(1)

    chip_idx = (2 * xi + yi).astype(jnp.int32).reshape(1)

    def start_gathers(l, dep):
        pending = []
        for gi, names in enumerate(GATHER_GROUPS):
            lands = [_cast_into_landing(f"cast_{n}", w[n], l, 0 if KIND[n] == "col" else 1, chip_idx) for n in names]
            pending.append(_gather_start(f"gather_start_{l}_{gi}", lands, dep))
            dep = pending[-1][3]
        return pending, dep

    def receive(l, gi, pending, after):
        send_sems, recv_sems, lands, _ = pending
        lands = _gather_wait(f"gather_wait_{l}_{gi}", send_sems, recv_sems, lands, after)
        return _share_start(f"gather_share_start_{l}_{gi}", lands)

    def complete(l, gi, share, after):
        send_sems, recv_sems, lands, _ = share
        lands = _share_wait(f"gather_share_wait_{l}_{gi}", send_sems, recv_sems, lands, after)
        return dict(zip(GATHER_GROUPS[gi], lands))

    conv_pack = jnp.concatenate([sc_conv_w, cf_conv_w], axis=1)
    taps = conv_pack.shape[1]
    rows_pad = ((depth * taps + 7) // 8) * 8
    conv_rows = jnp.pad(conv_pack.reshape(depth * taps, HEAD), ((0, rows_pad - depth * taps), (0, 0)))
    conv_all = _allgather8("gather_conv_weights", [conv_rows])[0]
    conv_full = conv_all[0::2, :depth * taps].reshape(4, depth, taps, HEAD)
    conv_full = jnp.transpose(conv_full, (1, 2, 0, 3)).reshape(depth, taps, GROUP)
    sc_w_full, cf_w_full = conv_full[:, :3], conv_full[:, 3:]

    pending, token = start_gathers(0, conv_all)
    h = x[0]
    saved = []
    gathered = []
    shares = [None] * len(GATHER_GROUPS)
    for l in range(depth):
        just_in_time = l < 2
        ahead = l + 1 < depth and l + 1 >= 2
        if just_in_time:
            shares[0] = receive(l, 0, pending[0], token if l == 0 else h)
        wg = complete(l, 0, shares[0], h)
        gathered.append(wg)
        started = ()
        this_layer = pending
        if l + 1 < depth:
            pending, token = start_gathers(l + 1, wg["w_in"])
            started = (token,)
        sv = {"h0": h}
        y1 = _rms_fwd("rms_mix", h, norm_mix_g[l:l + 1], after=started)
        z = _mm_nn_wide("mm_in", y1, wg["w_in"], "col", 1024, 1024)[0]
        lg, lb = sgu_ln_g[l][:, None, :], sgu_ln_b[l][:, None, :]
        sb = sgu_b[l][:, :, None]
        oa = _sgu_fwd("sgu_fwd", z, lg, lb, sgu_w[l], sb)
        ob = _shortconv_fwd("shortconv_fwd", z, sc_w_full[l])
        cc = _conformer_conv_fwd("conformer_conv_fwd", z, cf_w_full[l], cf_conv_b[l:l + 1])
        oc = _conformer_ln_fwd("conformer_ln_fwd", cc, cf_ln_g[l:l + 1], cf_ln_b[l:l + 1])
        od = _pool_fwd("pool_fwd", z, pool_w[l], pool_scale[l:l + 1])
        o = jnp.concatenate([oa, ob, oc, od], axis=1)
        h1 = _mm_residual("mm_out", o, wg["w_out"], h, 1024, 1024)
        if just_in_time:
            shares[1] = receive(l, 1, this_layer[1], h1)
        wg.update(complete(l, 1, shares[1], h1))
        y2 = _rms_fwd("rms_ffn", h1, norm_ffn_g[l:l + 1])
        act, gt, up = _swiglu_fwd("mm_swiglu", y2, wg["w_gate"], wg["w_up"], 512)
        if just_in_time:
            shares[2] = receive(l, 2, this_layer[2], act)
        wg.update(complete(l, 2, shares[2], act))
        h2 = _mm_residual("mm_down", act, wg["w_down"], h1, 1024, 512)
        started = ()
        if ahead:
            shares[0] = receive(l + 1, 0, pending[0], h2)
            shares[1] = receive(l + 1, 1, pending[1], shares[0][3])
            started = (shares[1][3],)
        y3 = _rms_fwd("rms_ple", h2, norm_ple_g[l:l + 1], after=started)
        pb = p[l, 0].astype(BF16)
        ptile = pl.BlockSpec((min(1024, s), 512), lambda i, j: (i, j))
        pp = _mm_nn_wide("mm_ple_proj", pb, wg["w_ple_proj"], "col", 1024, 512, epilogue=_wide_store(BF16),
                         out_specs=[ptile], out_shape=[jax.ShapeDtypeStruct((s, d), BF16)])[0]
        started = ()
        if ahead:
            shares[2] = receive(l + 1, 2, pending[2], pp)
            started = (shares[2][3],)
        h3, pg = _ple_fwd("mm_ple_gate", y3, wg["w_ple_gate"], h2, pp, 1024, 1024, after=started)
        sv.update(y1=y1, z=z, cc=cc, o=o, h1=h1, y2=y2, gt=gt, up=up, h2=h2, y3=y3, pb=pb, pp=pp, pg=pg)
        saved.append(sv)
        h = h3

    loss_part, dh, dhb, d_final_g = _loss_head("loss_head", h, final_norm_g[None, :], loss_target[0])

    small_grads = [None] * depth
    where = jnp.stack([2 * xi + yi, 2 * (1 - xi) + yi, 2 * xi + (1 - yi), 2 * (1 - xi) + (1 - yi), ci]).astype(jnp.int32)
    grad_bufs = {n: lax.empty((depth, 2, w[n].shape[1] // 2, w[n].shape[2]), F32) for n in BIG}
    exchanges = [None] * len(RS_GROUPS)
    joins = []
    behind_join = ()
    ple_grads = None

    def halves(g):
        return g.reshape(4, 2, g.shape[1] // 2, g.shape[2])

    def start_exchange(layer, gi, sibling, after):
        send_sems, recv_sems, gs, lands, _ = sibling
        gs, lands = _sibling_wait(f"rs_sibling_wait_{layer}_{gi}", send_sems, recv_sems, gs, lands, after)
        chip_sums = _add_halves("rs_add", gs, lands, c_idx)
        return _chip_exchange_start(f"rs_chips_start_{layer}_{gi}", chip_sums)

    def finish_exchange(layer, gi, after):
        send_sems, recv_sems, parts, lands, _ = exchanges[gi]
        parts, lands = _chip_exchange_wait(f"rs_chips_wait_{layer}_{gi}", send_sems, recv_sems, parts, lands, after)
        names = RS_GROUPS[gi]
        sums = _sum_parts("rs_sum", parts, lands, [grad_bufs[n] for n in names], layer, where)
        grad_bufs.update(zip(names, sums))

    for l in reversed(range(depth)):
        wg = gathered[l]
        sv = saved[l]
        fs = f_dim // 4
        started = () if exchanges[1] is None else (exchanges[1][4],)
        if ple_grads is None:
            ple_grads = _ple_bwd("ple_bwd", dh, sv["pg"], sv["pp"])
        d_pp, d_pg = ple_grads
        g_ple_proj = _mm_tn_wide("dw_ple_proj", sv["pb"], d_pp, "col", w_ple_proj.shape[1], 512)
        g_ple_gate = _mm_tn_wide("dw_ple_gate", sv["y3"], d_pg, "row", 512, 1024)
        dy3 = _mm_nt_wide("dx_ple_gate", d_pg, wg["w_ple_gate"], "row", 1024, 512, after=started + behind_join)
        dh, dhb, dg_ple = _rms_bwd("rms_ple_bwd", sv["h2"], norm_ple_g[l:l + 1], dy3, dh)

        d_gt, d_up, act = _swiglu_bwd("dx_down_swiglu", dhb, wg["w_down"], sv["gt"], sv["up"], 512)
        g_down = _mm_tn_wide("dw_down", act, dhb, "row", fs, 512)
        g_gate = _mm_tn_wide("dw_gate", sv["y2"], d_gt, "col", 512, fs)
        g_up = _mm_tn_wide("dw_up", sv["y2"], d_up, "col", 512, fs)
        big = dict(w_gate=g_gate, w_up=g_up, w_down=g_down, w_ple_gate=g_ple_gate, w_ple_proj=g_ple_proj)
        sibling = _sibling_start(f"rs_sibling_start_{l}_0", [halves(big[n]) for n in RS_GROUPS[0]])
        dy2 = _ffn_dy("dx_gate_up", d_gt, d_up, wg["w_gate"], wg["w_up"], (1024, 1024, fs), after=(sibling[4],))
        dh, dhb, dg_ffn = _rms_bwd("rms_ffn_bwd", sv["h1"], norm_ffn_g[l:l + 1], dy2, dh)

        g_out = _mm_tn_wide("dw_out", sv["o"], dhb, "row", 512, 1024)
        if exchanges[0] is not None:
            finish_exchange(l + 1, 0, g_out)
        exchanges[0] = start_exchange(l, 0, sibling, g_out)
        d_o = _mm_nt_wide("dx_out", dhb, wg["w_out"], "row", 1024, 512, after=(exchanges[0][4],))
        z = sv["z"]
        lg, lb = sgu_ln_g[l][:, None, :], sgu_ln_b[l][:, None, :]
        sb = sgu_b[l][:, :, None]
        dzu, dzv, d_lg, d_lb, d_sw, d_sb = _sgu_bwd("sgu_bwd", z, d_o[:, 0:GROUP], lg, lb, sgu_w[l], sb)
        dzh, dzbg, dzcg, d_scw = _shortconv_bwd("shortconv_bwd", z, d_o[:, GROUP:2 * GROUP], sc_w_full[l])
        dcc, d_cflg, d_cflb, d_cfb = _conformer_ln_bwd("conformer_ln_bwd", sv["cc"], d_o[:, 2 * GROUP:3 * GROUP],
                                                       cf_ln_g[l:l + 1], cf_ln_b[l:l + 1])
        dza, dzg, d_cfw = _conformer_conv_bwd("conformer_conv_bwd", z, dcc, cf_w_full[l])
        dzd, d_pw, d_psc = _pool_bwd("pool_bwd", z, d_o[:, 3 * GROUP:], pool_w[l], pool_scale[l:l + 1])
        dz = jnp.concatenate([dzu, dzv, dzh, dzbg, dzcg, dza, dzg, dzd], axis=1)
        g_in = _mm_tn_wide("dw_in", sv["y1"], dz, "col", 512, 1024)
        big.update(w_out=g_out, w_in=g_in)
        sibling = _sibling_start(f"rs_sibling_start_{l}_1", [halves(big[n]) for n in RS_GROUPS[1]])
        dy1 = _mm_nt_wide("dx_in", dz, wg["w_in"], "col", 1024, 512, after=(sibling[4],))
        below = None if l == 0 else (saved[l - 1]["pg"], saved[l - 1]["pp"])
        dh, dhb, dg_mix, *ple_grads = _rms_bwd("rms_mix_bwd", sv["h0"], norm_mix_g[l:l + 1], dy1, dh, ple=below)
        if exchanges[1] is not None:
            finish_exchange(l + 1, 1, dh)
            send_sems, recv_sems, bufs, join_token = _join_start(f"rs_join_start_{l + 1}",
                                                                 [grad_bufs[n] for n in BIG], l + 1)
            grad_bufs.update(zip(BIG, bufs))
            joins.append((l + 1, send_sems, recv_sems))
            behind_join = (join_token,)
        exchanges[1] = start_exchange(l, 1, sibling, dh)

        small_grads[l] = dict(norm_mix_g=dg_mix, sgu_ln_g=d_lg, sgu_ln_b=d_lb, sgu_w=d_sw, sgu_b=d_sb,
                              sc_conv_w=d_scw, cf_conv_w=d_cfw, cf_conv_b=d_cfb, cf_ln_g=d_cflg, cf_ln_b=d_cflb,
                              pool_w=d_pw, pool_scale=d_psc, norm_ffn_g=dg_ffn, norm_ple_g=dg_ple)
    grad_x = dh[None]

    grads, delta, new_m, new_v = {}, {}, {}, {}
    per_layer = [n for n in SMALL if n != "final_norm_g"]
    narrow = [n for n in per_layer if n not in MATRIX_SMALL]
    packed, _ = _pack_rows([small_grads[l][n] for n in narrow for l in range(depth)] + [d_final_g, loss_part])
    packed_wide, _ = _pack_rows([small_grads[l][n] for n in MATRIX_SMALL for l in range(depth)])
    everyone = _allgather8("gather_small", [packed, packed_wide.astype(BF16)])
    total = _sum8("sum_small", everyone[0]).reshape(-1)
    total_wide = _sum8("sum_small_wide", everyone[1]).reshape(-1)

    def stacked_shape(n):
        return (depth,) + (w[n].shape[1:] if n not in CHIP_SPLIT else (w[n].shape[1], GROUP))

    pieces = _unpack(total, [stacked_shape(n) for n in narrow] + [(d,), ()])
    loss = pieces[-1]
    grads["final_norm_g"] = pieces[-2]
    chip_off = (2 * xi + yi) * HEAD
    for n, g in zip(narrow, pieces):
        grads[n] = lax.dynamic_slice_in_dim(g, chip_off, HEAD, axis=2) if n in CHIP_SPLIT else g
    grads.update(zip(MATRIX_SMALL, _unpack(total_wide, [stacked_shape(n) for n in MATRIX_SMALL])))

    finish_exchange(0, 0, dh)
    for layer, send_sems, recv_sems in joins:
        bufs = _join_wait(f"rs_join_wait_{layer}", send_sems, recv_sems, [grad_bufs[n] for n in BIG], layer, total)
        grad_bufs.update(zip(BIG, bufs))
    behind = (exchanges[1][4],)
    for gi, names in enumerate(RS_GROUPS):
        if gi == 1:
            finish_exchange(0, 1, behind[0])
        joined = _sibling_join_halves(f"rs_join_{gi}", [grad_bufs[n] for n in names], 0, after=behind)
        for n, g in zip(names, joined):
            shp = w[n].shape
            two_d = (shp[0] * shp[1], shp[2])
            grads[n] = g.reshape(shp)
            dl, mn, vn = _adamw(f"adamw_{n}", w[n].reshape(two_d), g.reshape(two_d), mom[n].reshape(two_d),
                                var[n].reshape(two_d))
            delta[n], new_m[n], new_v[n] = dl.reshape(shp), mn.reshape(shp), vn.reshape(shp)
            behind = (dl,)
    small_shapes = [w[n].shape for n in SMALL]
    pw, _ = _pack_rows([w[n] for n in SMALL])
    pg_, _ = _pack_rows([grads[n] for n in SMALL])
    pm, _ = _pack_rows([mom[n] for n in SMALL])
    pv, _ = _pack_rows([var[n] for n in SMALL])
    dl, mn, vn = _adamw("adamw_small", pw, pg_, pm, pv)
    for n, a, b, cc_ in zip(SMALL, _unpack(dl.reshape(-1), small_shapes), _unpack(mn.reshape(-1), small_shapes),
                            _unpack(vn.reshape(-1), small_shapes)):
        delta[n], new_m[n], new_v[n] = a, b, cc_

    return (loss, grad_x, *[grads[n] for n in WEIGHTS], *[delta[n] for n in WEIGHTS],
            *[new_m[n] for n in WEIGHTS], *[new_v[n] for n in WEIGHTS])
```
